```python
import math
import jax, jax.numpy as jnp
from jax import lax
import numpy as np

D_MODEL = 1024
BATCH = 32
SEQ = 2048
DEPTH = 2

GRID_W = 64
CTX_LEN = 256

BRANCH_W = D_MODEL // 2
N_BRANCH = 3
M_HEADS = 4
M_DV = BRANCH_W // M_HEADS
M_DQK = M_DV // 2
M_CHUNK = 128
M_INIT = -1e30
POOL_GROUPS = 4
POOL_GC = BRANCH_W // POOL_GROUPS
POOL_WINDOWS = (2, 4, 8, 16)
NA_HEADS = 8
NA_DH = BRANCH_W // NA_HEADS
NA_KH = 8
NA_KW = 16
N_EXPERTS = 16
EXPERT_FF = D_MODEL
EC_FACTOR = 2
ROPE_BASE = 10000.0
RMS_EPS = 1e-6
N_MOD = 6
SPLIT_SIZES = (M_HEADS * M_DQK, M_HEADS * M_DQK, M_HEADS * M_DV, M_HEADS * M_DV, 4 * M_HEADS,
               POOL_GROUPS * POOL_GC, NA_HEADS * NA_DH, NA_HEADS * NA_DH, NA_HEADS * NA_DH,
               N_BRANCH * D_MODEL)
PROJ_W = sum(SPLIT_SIZES)

kernel_name = 'hybrid_mlstm_pool_natten_ecmoe_dit'


def rmsnorm(x, g):
    xf = x.astype(jnp.float32)
    y = xf * lax.rsqrt(jnp.mean(xf * xf, axis=-1, keepdims=True) + RMS_EPS)
    return (y * g.astype(jnp.float32)).astype(x.dtype)


def modulate(h, shift, scale):
    return h * (1 + scale) + shift


def rope_tables(n_tokens, dtype):
    t = jnp.arange(n_tokens)
    row = (t // GRID_W).astype(jnp.float32)
    col = (t % GRID_W).astype(jnp.float32)
    half = M_DQK // 2
    inv = ROPE_BASE ** (-jnp.arange(0, half, 2, dtype=jnp.float32) / half)
    ar = row[:, None] * inv[None, :]
    ac = col[:, None] * inv[None, :]
    return tuple(a.astype(dtype) for a in (jnp.cos(ar), jnp.sin(ar), jnp.cos(ac), jnp.sin(ac)))


def _rotate(x, cos, sin):
    x1, x2 = jnp.split(x, 2, axis=-1)
    cos = cos[:, None, :]
    sin = sin[:, None, :]
    return jnp.concatenate([x1 * cos - x2 * sin, x2 * cos + x1 * sin], axis=-1)


def rope_2d(x, tabs):
    cr, sr, cc, sc = tabs
    xr, xc = jnp.split(x, 2, axis=-1)
    return jnp.concatenate([_rotate(xr, cr, sr), _rotate(xc, cc, sc)], axis=-1)


def mlstm_scan(q, k, v, li, lf, state):
    B, T, H, _ = q.shape
    dv = v.shape[-1]
    L = M_CHUNK
    N = T // L

    def chunks(a):
        a = a.reshape((B, N, L, H) + a.shape[3:])
        return jnp.moveaxis(a, (1, 3), (0, 2))

    tril = jnp.tril(jnp.ones((L, L), dtype=bool))

    def step(carry, inp):
        C, n, m = carry
        qc, kc, vc, ic, fc = inp
        b = jnp.cumsum(fc, axis=-1)
        dmat = jnp.where(tril, b[..., :, None] - b[..., None, :] + ic[..., None, :], -jnp.inf)
        g = b + m[..., None]
        mt = jnp.maximum(g, dmat.max(-1))
        sc = jnp.einsum('bhtd,bhsd->bhts', qc, kc) * jnp.exp(dmat - mt[..., None])
        inter = jnp.exp(g - mt)
        num = inter[..., None] * jnp.einsum('bhvd,bhtd->bhtv', C, qc) + jnp.einsum('bhts,bhsv->bhtv', sc, vc)
        den = inter * jnp.einsum('bhd,bhtd->bht', n, qc) + sc.sum(-1)
        h = num / jnp.maximum(jnp.abs(den), jnp.exp(-mt))[..., None]
        bl = b[..., -1]
        w = bl[..., None] - b + ic
        m_new = jnp.maximum(bl + m, w.max(-1))
        decay = jnp.exp(bl + m - m_new)
        ws = jnp.exp(w - m_new[..., None])
        C_new = decay[..., None, None] * C + jnp.einsum('bhs,bhsv,bhsd->bhvd', ws, vc, kc)
        n_new = decay[..., None] * n + jnp.einsum('bhs,bhsd->bhd', ws, kc)
        return (C_new, n_new, m_new), h

    xs = (chunks(q), chunks(k), chunks(v), chunks(li), chunks(lf))
    state, h = lax.scan(step, state, xs)
    h = jnp.moveaxis(h, (0, 2), (1, 3)).reshape(B, T, H, dv)
    return state, h


def head_rms(h):
    return h * lax.rsqrt(jnp.mean(h * h, axis=-1, keepdims=True) + RMS_EPS)


def mlstm_branch(ctx_parts, lat_parts, gate_bias, need_ctx):
    f32 = jnp.float32
    qscale = M_DQK ** -0.5

    def prep(parts):
        q, k, v, o, gt = parts
        q = q.astype(f32) * qscale
        k = k.astype(f32)
        v = v.astype(f32)
        gt = gt.astype(f32) + gate_bias.astype(f32)
        fwd = (q, k, v, gt[..., 0, :], jax.nn.log_sigmoid(gt[..., 1, :]))
        bwd = tuple(jnp.flip(a, axis=1) for a in (q, k, v, gt[..., 2, :], jax.nn.log_sigmoid(gt[..., 3, :])))
        return fwd, bwd

    c_f, c_b = prep(ctx_parts)
    l_f, l_b = prep(lat_parts)
    B = l_f[0].shape[0]
    st0 = (jnp.zeros((B, M_HEADS, M_DV, M_DQK), f32), jnp.zeros((B, M_HEADS, M_DQK), f32),
           jnp.full((B, M_HEADS), M_INIT, f32))
    st_f, hc_f = mlstm_scan(*c_f, st0)
    _, hl_f = mlstm_scan(*l_f, st_f)
    st_b, hc_b = mlstm_scan(*c_b, st0)
    _, hl_b = mlstm_scan(*l_b, st_b)

    def finish(h_f, h_b_rev, o):
        h = head_rms(h_f + jnp.flip(h_b_rev, axis=1))
        return jax.nn.sigmoid(o) * h.reshape(o.shape).astype(o.dtype)

    out_l = finish(hl_f, hl_b, lat_parts[3])
    out_c = finish(hc_f, hc_b, ctx_parts[3]) if need_ctx else None
    return out_l, out_c


def pool_branch(u, pool_w, pool_scale):
    B, T, _ = u.shape
    uf = u.astype(jnp.float32).reshape(B, T, POOL_GROUPS, POOL_GC)
    csum = jnp.concatenate([jnp.zeros((B, 1, POOL_GROUPS, POOL_GC), jnp.float32),
                            jnp.cumsum(uf, axis=1)], axis=1)
    t = jnp.arange(T)
    outs = []
    for gi, w in enumerate(POOL_WINDOWS):
        lo = jnp.clip(t - w // 2, 0, T - 1)
        hi = jnp.clip(t - w // 2 + w - 1, 0, T - 1)
        cnt = (hi - lo + 1).astype(jnp.float32)[:, None]
        cg = csum[:, :, gi]
        mean = (cg[:, hi + 1] - cg[:, lo]) / cnt
        outs.append(jnp.einsum('btc,cd->btd', (mean - uf[:, :, gi]).astype(u.dtype), pool_w[gi]))
    return jnp.concatenate(outs, axis=-1) * pool_scale


def na_latent(q, k, v, k_ctx, v_ctx, rpb):
    B, S, H, dh = q.shape
    rows = S // GRID_W
    kh = min(NA_KH, rows)
    f32 = jnp.float32
    scale = dh ** -0.5

    def grid(a):
        return a.reshape(B, rows, GRID_W, H, dh).transpose(1, 0, 3, 2, 4)

    qg, kg, vg = grid(q), grid(k), grid(v)
    kc = k_ctx.transpose(0, 2, 1, 3)
    vc = v_ctx.transpose(0, 2, 1, 3)
    cols = jnp.arange(GRID_W)
    cs = jnp.clip(cols - NA_KW // 2, 0, GRID_W - NA_KW)
    colmask = (cols[None, :] >= cs[:, None]) & (cols[None, :] < cs[:, None] + NA_KW)
    mask = jnp.tile(colmask, (1, kh))
    dc_idx = jnp.clip(cols[None, :] - cols[:, None] + NA_KW - 1, 0, 2 * NA_KW - 2)
    n_loc = kh * GRID_W

    def row_fn(r):
        rs = jnp.clip(r - kh // 2, 0, rows - kh)
        qr = qg[r]
        kr = jnp.moveaxis(lax.dynamic_slice_in_dim(kg, rs, kh, axis=0), 0, 2).reshape(B, H, n_loc, dh)
        vr = jnp.moveaxis(lax.dynamic_slice_in_dim(vg, rs, kh, axis=0), 0, 2).reshape(B, H, n_loc, dh)
        dr_idx = rs + jnp.arange(kh) - r + NA_KH - 1
        bias = rpb[:, dr_idx][:, :, dc_idx]
        bias = bias.transpose(0, 2, 1, 3).reshape(H, GRID_W, n_loc).astype(f32)
        s_loc = jnp.einsum('bhqd,bhkd->bhqk', qr, kr, preferred_element_type=f32) * scale + bias
        s_loc = jnp.where(mask, s_loc, -jnp.inf)
        s_ctx = jnp.einsum('bhqd,bhkd->bhqk', qr, kc, preferred_element_type=f32) * scale
        p = jax.nn.softmax(jnp.concatenate([s_loc, s_ctx], axis=-1), axis=-1).astype(v.dtype)
        return (jnp.einsum('bhqk,bhkd->bhqd', p[..., :n_loc], vr)
                + jnp.einsum('bhqk,bhkd->bhqd', p[..., n_loc:], vc))

    out = lax.map(row_fn, jnp.arange(rows))
    return out.transpose(1, 0, 3, 2, 4).reshape(B, S, H * dh)


def na_context(q, k, v):
    B, T, H, dh = q.shape
    s = jnp.einsum('bqhd,bkhd->bhqk', q, k, preferred_element_type=jnp.float32) * dh ** -0.5
    p = jax.nn.softmax(s, axis=-1).astype(v.dtype)
    return jnp.einsum('bhqk,bkhd->bqhd', p, v).reshape(B, T, H * dh)


def merge_branches(branches, bg, w_branch, w_out):
    D = w_out.shape[0]
    acc = None
    for bi, br in enumerate(branches):
        term = jax.nn.sigmoid(bg[..., bi * D:(bi + 1) * D]) * (br @ w_branch[bi])
        acc = term if acc is None else acc + term
    return acc @ w_out


def mixer(h_lat, h_ctx, w_in, gate_bias, pool_w, pool_scale, rpb, w_branch, w_out, tabs, need_ctx):
    split_at = [int(s) for s in np.cumsum(SPLIT_SIZES)[:-1]]

    def project(h):
        B, T, _ = h.shape
        qm, km, vm, om, gm, pu, qn, kn, vn, bg = jnp.split(h @ w_in, split_at, axis=-1)
        return (qm.reshape(B, T, M_HEADS, M_DQK), km.reshape(B, T, M_HEADS, M_DQK),
                vm.reshape(B, T, M_HEADS, M_DV), om, gm.reshape(B, T, 4, M_HEADS), pu,
                qn.reshape(B, T, NA_HEADS, NA_DH), kn.reshape(B, T, NA_HEADS, NA_DH),
                vn.reshape(B, T, NA_HEADS, NA_DH), bg)

    qm_l, km_l, vm_l, om_l, gm_l, pu_l, qn_l, kn_l, vn_l, bg_l = project(h_lat)
    qm_c, km_c, vm_c, om_c, gm_c, pu_c, qn_c, kn_c, vn_c, bg_c = project(h_ctx)
    qm_l = rope_2d(qm_l, tabs)
    km_l = rope_2d(km_l, tabs)
    m_l, m_c = mlstm_branch((qm_c, km_c, vm_c, om_c, gm_c), (qm_l, km_l, vm_l, om_l, gm_l),
                            gate_bias, need_ctx)
    p_l = pool_branch(pu_l, pool_w, pool_scale)
    n_l = na_latent(qn_l, kn_l, vn_l, kn_c, vn_c, rpb)
    y_l = merge_branches((m_l, p_l, n_l), bg_l, w_branch, w_out)
    y_c = None
    if need_ctx:
        p_c = pool_branch(pu_c, pool_w, pool_scale)
        n_c = na_context(qn_c, kn_c, vn_c)
        y_c = merge_branches((m_c, p_c, n_c), bg_c, w_branch, w_out)
    return y_l, y_c


def ec_moe(h, w_router, w_gate, w_up, w_down):
    B, T, D = h.shape
    cap = EC_FACTOR * T // N_EXPERTS
    aff = jax.nn.softmax(jnp.einsum('btd,de->bte', h, w_router, preferred_element_type=jnp.float32), axis=-1)
    gate, idx = lax.top_k(jnp.swapaxes(aff, 1, 2), cap)
    xe = jax.vmap(lambda hb, ib: hb[ib])(h, idx)
    hid = jax.nn.silu(jnp.einsum('becd,edf->becf', xe, w_gate)) * jnp.einsum('becd,edf->becf', xe, w_up)
    ye = jnp.einsum('becf,efd->becd', hid, w_down) * gate[..., None].astype(h.dtype)
    return jax.vmap(lambda ib, yb: jnp.zeros((T, D), yb.dtype).at[ib.reshape(-1)].add(yb.reshape(-1, D)))(idx, ye)


def setup_inputs(seed: int = 0) -> dict:
    key = jax.random.key(seed)
    ks = jax.random.split(key, 20)
    f32 = jnp.float32

    def nrm(k, shape, scale):
        return jax.random.normal(k, shape, f32) * scale

    D = D_MODEL
    lin = jnp.linspace(3.0, 6.0, M_HEADS, dtype=f32)
    gb_base = jnp.zeros((4, M_HEADS), f32).at[1].set(lin).at[3].set(lin)
    return {
        'x': nrm(ks[0], (BATCH, SEQ, D), 1.0),
        'c': nrm(ks[1], (BATCH, D), 1.0),
        'ctx': nrm(ks[2], (BATCH, CTX_LEN, D), 1.0),
        'c_ctx': nrm(ks[3], (D,), 1.0),
        'norm_gain': 1.0 + nrm(ks[4], (DEPTH, 4, D), 0.05),
        'ada_w': nrm(ks[5], (DEPTH, D, N_MOD * D), 0.5 * D ** -0.5),
        'ada_b': nrm(ks[6], (DEPTH, N_MOD * D), 0.02),
        'w_in': nrm(ks[7], (DEPTH, D, PROJ_W), D ** -0.5),
        'mlstm_gate_bias': gb_base[None] + nrm(ks[8], (DEPTH, 4, M_HEADS), 0.1),
        'pool_w': nrm(ks[9], (DEPTH, POOL_GROUPS, POOL_GC, POOL_GC), POOL_GC ** -0.5),
        'pool_scale': 1.0 + nrm(ks[10], (DEPTH, BRANCH_W), 0.1),
        'na_rpb': nrm(ks[11], (DEPTH, NA_HEADS, 2 * NA_KH - 1, 2 * NA_KW - 1), 0.1),
        'w_branch': nrm(ks[12], (DEPTH, N_BRANCH, BRANCH_W, D), BRANCH_W ** -0.5),
        'w_out': nrm(ks[13], (DEPTH, D, D), D ** -0.5),
        'router_w': nrm(ks[14], (DEPTH, D, N_EXPERTS), D ** -0.5),
        'w_gate': nrm(ks[15], (DEPTH, N_EXPERTS, D, EXPERT_FF), D ** -0.5),
        'w_up': nrm(ks[16], (DEPTH, N_EXPERTS, D, EXPERT_FF), D ** -0.5),
        'w_down': nrm(ks[17], (DEPTH, N_EXPERTS, EXPERT_FF, D), EXPERT_FF ** -0.5),
    }


def reference(x, c, ctx, c_ctx, norm_gain, ada_w, ada_b, w_in, mlstm_gate_bias, pool_w, pool_scale,
              na_rpb, w_branch, w_out, router_w, w_gate, w_up, w_down):
    S = x.shape[1]
    tabs = rope_tables(S, x.dtype)
    for l in range(DEPTH):
        need_ctx = l < DEPTH - 1
        g = norm_gain[l]
        mod_l = jnp.split((jax.nn.silu(c) @ ada_w[l] + ada_b[l])[:, None, :], N_MOD, axis=-1)
        mod_c = jnp.split(jax.nn.silu(c_ctx) @ ada_w[l] + ada_b[l], N_MOD, axis=-1)
        h_l = modulate(rmsnorm(x, g[0]), mod_l[0], mod_l[1])
        h_c = modulate(rmsnorm(ctx, g[0]), mod_c[0], mod_c[1])
        y_l, y_c = mixer(h_l, h_c, w_in[l], mlstm_gate_bias[l], pool_w[l], pool_scale[l], na_rpb[l],
                         w_branch[l], w_out[l], tabs, need_ctx)
        x = x + mod_l[2] * rmsnorm(y_l, g[1])
        h2 = modulate(rmsnorm(x, g[2]), mod_l[3], mod_l[4])
        x = x + mod_l[5] * rmsnorm(ec_moe(h2, router_w[l], w_gate[l], w_up[l], w_down[l]), g[3])
        if need_ctx:
            ctx = ctx + mod_c[2] * rmsnorm(y_c, g[1])
            h2c = modulate(rmsnorm(ctx, g[2]), mod_c[3], mod_c[4])
            ctx = ctx + mod_c[5] * rmsnorm(ec_moe(h2c, router_w[l], w_gate[l], w_up[l], w_down[l]), g[3])
    return x
```

```python
import functools
import math

import jax
import jax.numpy as jnp
import numpy as np
from jax import lax
from jax.experimental import pallas as pl
from jax.experimental.pallas import tpu as pltpu

D_MODEL = 1024
DEPTH = 2
GRID_W = 64
BRANCH_W = D_MODEL // 2
N_BRANCH = 3
M_HEADS = 4
M_DV = BRANCH_W // M_HEADS
M_DQK = M_DV // 2
M_CHUNK = 128
M_INIT = -1e30
POOL_GROUPS = 4
POOL_GC = BRANCH_W // POOL_GROUPS
POOL_WINDOWS = (2, 4, 8, 16)
NA_HEADS = 8
NA_DH = BRANCH_W // NA_HEADS
NA_KH = 8
NA_KW = 16
N_EXPERTS = 16
EC_FACTOR = 2
ROPE_BASE = 10000.0
RMS_EPS = 1e-6
N_MOD = 6
SPLIT_SIZES = (M_HEADS * M_DQK, M_HEADS * M_DQK, M_HEADS * M_DV, M_HEADS * M_DV, 4 * M_HEADS,
               POOL_GROUPS * POOL_GC, NA_HEADS * NA_DH, NA_HEADS * NA_DH, NA_HEADS * NA_DH,
               N_BRANCH * D_MODEL)
PROJ_W = sum(SPLIT_SIZES)

VMEM_LIMIT = 48 * 1024 * 1024


def _norm_mod_matmul_kernel(x_ref, g_ref, shift_ref, scale_ref, w_ref, o_ref, h_ref):
    @pl.when(pl.program_id(2) == 0)
    def _():
        x = x_ref[0]
        y = x * lax.rsqrt(jnp.mean(x * x, axis=-1, keepdims=True) + RMS_EPS)
        h = (y * g_ref[...]) * (1.0 + scale_ref[0]) + shift_ref[0]
        h_ref[...] = h.astype(h_ref.dtype)

    o_ref[0] = jnp.dot(h_ref[...], w_ref[...], preferred_element_type=jnp.float32).astype(o_ref.dtype)


def norm_mod_matmul(x, g, shift, scale, w, *, tm, tn, out_dtype):
    B, T, D = x.shape
    N = w.shape[1]
    return pl.pallas_call(
        _norm_mod_matmul_kernel,
        grid=(B, T // tm, N // tn),
        in_specs=[
            pl.BlockSpec((1, tm, D), lambda b, i, j: (b, i, 0)),
            pl.BlockSpec((1, D), lambda b, i, j: (0, 0)),
            pl.BlockSpec((1, 1, D), lambda b, i, j: (b, 0, 0)),
            pl.BlockSpec((1, 1, D), lambda b, i, j: (b, 0, 0)),
            pl.BlockSpec((D, tn), lambda b, i, j: (0, j)),
        ],
        out_specs=pl.BlockSpec((1, tm, tn), lambda b, i, j: (b, i, j)),
        out_shape=jax.ShapeDtypeStruct((B, T, N), out_dtype),
        scratch_shapes=[pltpu.VMEM((tm, D), jnp.bfloat16)],
        compiler_params=pltpu.CompilerParams(
            dimension_semantics=("arbitrary", "arbitrary", "arbitrary"),
            vmem_limit_bytes=VMEM_LIMIT),
        name="norm_mod_matmul",
    )(x, g, shift, scale, w)


def rmsnorm(x, g):
    xf = x.astype(jnp.float32)
    y = xf * lax.rsqrt(jnp.mean(xf * xf, axis=-1, keepdims=True) + RMS_EPS)
    return (y * g.astype(jnp.float32)).astype(x.dtype)


def modulate(h, shift, scale):
    return h * (1 + scale) + shift


def rope_tables(n_tokens, dtype):
    t = jnp.arange(n_tokens)
    row = (t // GRID_W).astype(jnp.float32)
    col = (t % GRID_W).astype(jnp.float32)
    half = M_DQK // 2
    inv = ROPE_BASE ** (-jnp.arange(0, half, 2, dtype=jnp.float32) / half)
    ar = row[:, None] * inv[None, :]
    ac = col[:, None] * inv[None, :]
    return tuple(a.astype(dtype) for a in (jnp.cos(ar), jnp.sin(ar), jnp.cos(ac), jnp.sin(ac)))


def _rotate(x, cos, sin):
    x1, x2 = jnp.split(x, 2, axis=-1)
    cos = cos[:, None, :]
    sin = sin[:, None, :]
    return jnp.concatenate([x1 * cos - x2 * sin, x2 * cos + x1 * sin], axis=-1)


def rope_2d(x, tabs):
    cr, sr, cc, sc = tabs
    xr, xc = jnp.split(x, 2, axis=-1)
    return jnp.concatenate([_rotate(xr, cr, sr), _rotate(xc, cc, sc)], axis=-1)


def mlstm_scan(q, k, v, li, lf, state):
    B, T, H, _ = q.shape
    dv = v.shape[-1]
    L = M_CHUNK
    N = T // L

    def chunks(a):
        a = a.reshape((B, N, L, H) + a.shape[3:])
        return jnp.moveaxis(a, (1, 3), (0, 2))

    tril = jnp.tril(jnp.ones((L, L), dtype=bool))

    def step(carry, inp):
        C, n, m = carry
        qc, kc, vc, ic, fc = inp
        b = jnp.cumsum(fc, axis=-1)
        dmat = jnp.where(tril, b[..., :, None] - b[..., None, :] + ic[..., None, :], -jnp.inf)
        g = b + m[..., None]
        mt = jnp.maximum(g, dmat.max(-1))
        sc = jnp.einsum('bhtd,bhsd->bhts', qc, kc) * jnp.exp(dmat - mt[..., None])
        inter = jnp.exp(g - mt)
        num = inter[..., None] * jnp.einsum('bhvd,bhtd->bhtv', C, qc) + jnp.einsum('bhts,bhsv->bhtv', sc, vc)
        den = inter * jnp.einsum('bhd,bhtd->bht', n, qc) + sc.sum(-1)
        h = num / jnp.maximum(jnp.abs(den), jnp.exp(-mt))[..., None]
        bl = b[..., -1]
        w = bl[..., None] - b + ic
        m_new = jnp.maximum(bl + m, w.max(-1))
        decay = jnp.exp(bl + m - m_new)
        ws = jnp.exp(w - m_new[..., None])
        C_new = decay[..., None, None] * C + jnp.einsum('bhs,bhsv,bhsd->bhvd', ws, vc, kc)
        n_new = decay[..., None] * n + jnp.einsum('bhs,bhsd->bhd', ws, kc)
        return (C_new, n_new, m_new), h

    xs = (chunks(q), chunks(k), chunks(v), chunks(li), chunks(lf))
    state, h = lax.scan(step, state, xs)
    h = jnp.moveaxis(h, (0, 2), (1, 3)).reshape(B, T, H, dv)
    return state, h


def head_rms(h):
    return h * lax.rsqrt(jnp.mean(h * h, axis=-1, keepdims=True) + RMS_EPS)


def mlstm_branch(ctx_parts, lat_parts, gate_bias, need_ctx):
    f32 = jnp.float32
    qscale = M_DQK ** -0.5

    def prep(parts):
        q, k, v, o, gt = parts
        q = q.astype(f32) * qscale
        k = k.astype(f32)
        v = v.astype(f32)
        gt = gt.astype(f32) + gate_bias.astype(f32)
        fwd = (q, k, v, gt[..., 0, :], jax.nn.log_sigmoid(gt[..., 1, :]))
        bwd = tuple(jnp.flip(a, axis=1) for a in (q, k, v, gt[..., 2, :], jax.nn.log_sigmoid(gt[..., 3, :])))
        return fwd, bwd

    c_f, c_b = prep(ctx_parts)
    l_f, l_b = prep(lat_parts)
    B = l_f[0].shape[0]
    st0 = (jnp.zeros((B, M_HEADS, M_DV, M_DQK), f32), jnp.zeros((B, M_HEADS, M_DQK), f32),
           jnp.full((B, M_HEADS), M_INIT, f32))
    st_f, hc_f = mlstm_scan(*c_f, st0)
    _, hl_f = mlstm_scan(*l_f, st_f)
    st_b, hc_b = mlstm_scan(*c_b, st0)
    _, hl_b = mlstm_scan(*l_b, st_b)

    def finish(h_f, h_b_rev, o):
        h = head_rms(h_f + jnp.flip(h_b_rev, axis=1))
        return jax.nn.sigmoid(o) * h.reshape(o.shape).astype(o.dtype)

    out_l = finish(hl_f, hl_b, lat_parts[3])
    out_c = finish(hc_f, hc_b, ctx_parts[3]) if need_ctx else None
    return out_l, out_c


def pool_branch(u, pool_w, pool_scale):
    B, T, _ = u.shape
    uf = u.astype(jnp.float32).reshape(B, T, POOL_GROUPS, POOL_GC)
    csum = jnp.concatenate([jnp.zeros((B, 1, POOL_GROUPS, POOL_GC), jnp.float32),
                            jnp.cumsum(uf, axis=1)], axis=1)
    t = jnp.arange(T)
    outs = []
    for gi, w in enumerate(POOL_WINDOWS):
        lo = jnp.clip(t - w // 2, 0, T - 1)
        hi = jnp.clip(t - w // 2 + w - 1, 0, T - 1)
        cnt = (hi - lo + 1).astype(jnp.float32)[:, None]
        cg = csum[:, :, gi]
        mean = (cg[:, hi + 1] - cg[:, lo]) / cnt
        outs.append(jnp.einsum('btc,cd->btd', (mean - uf[:, :, gi]).astype(u.dtype), pool_w[gi]))
    return jnp.concatenate(outs, axis=-1) * pool_scale


def na_latent(q, k, v, k_ctx, v_ctx, rpb):
    B, S, H, dh = q.shape
    rows = S // GRID_W
    kh = min(NA_KH, rows)
    f32 = jnp.float32
    scale = dh ** -0.5

    def grid(a):
        return a.reshape(B, rows, GRID_W, H, dh).transpose(1, 0, 3, 2, 4)

    qg, kg, vg = grid(q), grid(k), grid(v)
    kc = k_ctx.transpose(0, 2, 1, 3)
    vc = v_ctx.transpose(0, 2, 1, 3)
    cols = jnp.arange(GRID_W)
    cs = jnp.clip(cols - NA_KW // 2, 0, GRID_W - NA_KW)
    colmask = (cols[None, :] >= cs[:, None]) & (cols[None, :] < cs[:, None] + NA_KW)
    mask = jnp.tile(colmask, (1, kh))
    dc_idx = jnp.clip(cols[None, :] - cols[:, None] + NA_KW - 1, 0, 2 * NA_KW - 2)
    n_loc = kh * GRID_W

    def row_fn(r):
        rs = jnp.clip(r - kh // 2, 0, rows - kh)
        qr = qg[r]
        kr = jnp.moveaxis(lax.dynamic_slice_in_dim(kg, rs, kh, axis=0), 0, 2).reshape(B, H, n_loc, dh)
        vr = jnp.moveaxis(lax.dynamic_slice_in_dim(vg, rs, kh, axis=0), 0, 2).reshape(B, H, n_loc, dh)
        dr_idx = rs + jnp.arange(kh) - r + NA_KH - 1
        bias = rpb[:, dr_idx][:, :, dc_idx]
        bias = bias.transpose(0, 2, 1, 3).reshape(H, GRID_W, n_loc).astype(f32)
        s_loc = jnp.einsum('bhqd,bhkd->bhqk', qr, kr, preferred_element_type=f32) * scale + bias
        s_loc = jnp.where(mask, s_loc, -jnp.inf)
        s_ctx = jnp.einsum('bhqd,bhkd->bhqk', qr, kc, preferred_element_type=f32) * scale
        p = jax.nn.softmax(jnp.concatenate([s_loc, s_ctx], axis=-1), axis=-1).astype(v.dtype)
        return (jnp.einsum('bhqk,bhkd->bhqd', p[..., :n_loc], vr)
                + jnp.einsum('bhqk,bhkd->bhqd', p[..., n_loc:], vc))

    out = lax.map(row_fn, jnp.arange(rows))
    return out.transpose(1, 0, 3, 2, 4).reshape(B, S, H * dh)


def na_context(q, k, v):
    B, T, H, dh = q.shape
    s = jnp.einsum('bqhd,bkhd->bhqk', q, k, preferred_element_type=jnp.float32) * dh ** -0.5
    p = jax.nn.softmax(s, axis=-1).astype(v.dtype)
    return jnp.einsum('bhqk,bkhd->bqhd', p, v).reshape(B, T, H * dh)


def merge_branches(branches, bg, w_branch, w_out):
    D = w_out.shape[0]
    acc = None
    for bi, br in enumerate(branches):
        term = jax.nn.sigmoid(bg[..., bi * D:(bi + 1) * D]) * (br @ w_branch[bi])
        acc = term if acc is None else acc + term
    return acc @ w_out


def mixer(proj_l, proj_c, gate_bias, pool_w, pool_scale, rpb, w_branch, w_out, tabs, need_ctx):
    split_at = [int(s) for s in np.cumsum(SPLIT_SIZES)[:-1]]

    def split(p):
        B, T, _ = p.shape
        qm, km, vm, om, gm, pu, qn, kn, vn, bg = jnp.split(p, split_at, axis=-1)
        return (qm.reshape(B, T, M_HEADS, M_DQK), km.reshape(B, T, M_HEADS, M_DQK),
                vm.reshape(B, T, M_HEADS, M_DV), om, gm.reshape(B, T, 4, M_HEADS), pu,
                qn.reshape(B, T, NA_HEADS, NA_DH), kn.reshape(B, T, NA_HEADS, NA_DH),
                vn.reshape(B, T, NA_HEADS, NA_DH), bg)

    qm_l, km_l, vm_l, om_l, gm_l, pu_l, qn_l, kn_l, vn_l, bg_l = split(proj_l)
    qm_c, km_c, vm_c, om_c, gm_c, pu_c, qn_c, kn_c, vn_c, bg_c = split(proj_c)
    qm_l = rope_2d(qm_l, tabs)
    km_l = rope_2d(km_l, tabs)
    m_l, m_c = mlstm_branch((qm_c, km_c, vm_c, om_c, gm_c), (qm_l, km_l, vm_l, om_l, gm_l),
                            gate_bias, need_ctx)
    p_l = pool_branch(pu_l, pool_w, pool_scale)
    n_l = na_latent(qn_l, kn_l, vn_l, kn_c, vn_c, rpb)
    y_l = merge_branches((m_l, p_l, n_l), bg_l, w_branch, w_out)
    y_c = None
    if need_ctx:
        p_c = pool_branch(pu_c, pool_w, pool_scale)
        n_c = na_context(qn_c, kn_c, vn_c)
        y_c = merge_branches((m_c, p_c, n_c), bg_c, w_branch, w_out)
    return y_l, y_c


def ec_moe(h, w_router, w_gate, w_up, w_down):
    B, T, D = h.shape
    cap = EC_FACTOR * T // N_EXPERTS
    aff = jax.nn.softmax(jnp.einsum('btd,de->bte', h, w_router, preferred_element_type=jnp.float32), axis=-1)
    gate, idx = lax.top_k(jnp.swapaxes(aff, 1, 2), cap)
    xe = jax.vmap(lambda hb, ib: hb[ib])(h, idx)
    hid = jax.nn.silu(jnp.einsum('becd,edf->becf', xe, w_gate)) * jnp.einsum('becd,edf->becf', xe, w_up)
    ye = jnp.einsum('becf,efd->becd', hid, w_down) * gate[..., None].astype(h.dtype)
    return jax.vmap(lambda ib, yb: jnp.zeros((T, D), yb.dtype).at[ib.reshape(-1)].add(yb.reshape(-1, D)))(idx, ye)


def kernel(x, c, ctx, c_ctx, norm_gain, ada_w, ada_b, w_in, mlstm_gate_bias, pool_w, pool_scale,
           na_rpb, w_branch, w_out, router_w, w_gate, w_up, w_down):
    B, S, D = x.shape
    Tc = ctx.shape[1]
    tabs = rope_tables(S, x.dtype)
    tn = 1792
    n_pad = (-PROJ_W) % tn
    for l in range(DEPTH):
        need_ctx = l < DEPTH - 1
        g = norm_gain[l]
        mod_l = jnp.split((jax.nn.silu(c) @ ada_w[l] + ada_b[l])[:, None, :], N_MOD, axis=-1)
        mod_c = jnp.split(jax.nn.silu(c_ctx) @ ada_w[l] + ada_b[l], N_MOD, axis=-1)
        w_in_p = jnp.pad(w_in[l], ((0, 0), (0, n_pad))).astype(jnp.bfloat16)
        proj_l = norm_mod_matmul(x, g[0][None], mod_l[0], mod_l[1], w_in_p,
                                 tm=512, tn=tn, out_dtype=jnp.float32)[..., :PROJ_W]
        sh_c = jnp.broadcast_to(mod_c[0][None, None], (B, 1, D))
        sc_c = jnp.broadcast_to(mod_c[1][None, None], (B, 1, D))
        proj_c = norm_mod_matmul(ctx, g[0][None], sh_c, sc_c, w_in_p,
                                 tm=256, tn=tn, out_dtype=jnp.float32)[..., :PROJ_W]
        y_l, y_c = mixer(proj_l, proj_c, mlstm_gate_bias[l], pool_w[l], pool_scale[l], na_rpb[l],
                         w_branch[l], w_out[l], tabs, need_ctx)
        x = x + mod_l[2] * rmsnorm(y_l, g[1])
        h2 = modulate(rmsnorm(x, g[2]), mod_l[3], mod_l[4])
        x = x + mod_l[5] * rmsnorm(ec_moe(h2, router_w[l], w_gate[l], w_up[l], w_down[l]), g[3])
        if need_ctx:
            ctx = ctx + mod_c[2] * rmsnorm(y_c, g[1])
            h2c = modulate(rmsnorm(ctx, g[2]), mod_c[3], mod_c[4])
            ctx = ctx + mod_c[5] * rmsnorm(ec_moe(h2c, router_w[l], w_gate[l], w_up[l], w_down[l]), g[3])
    return x
```

```python
import functools
import math

import jax
import jax.numpy as jnp
import numpy as np
from jax import lax
from jax.experimental import pallas as pl
from jax.experimental.pallas import tpu as pltpu

D_MODEL = 1024
DEPTH = 2
GRID_W = 64
BRANCH_W = D_MODEL // 2
N_BRANCH = 3
M_HEADS = 4
M_DV = BRANCH_W // M_HEADS
M_DQK = M_DV // 2
M_CHUNK = 128
M_INIT = -1e30
POOL_GROUPS = 4
POOL_GC = BRANCH_W // POOL_GROUPS
POOL_WINDOWS = (2, 4, 8, 16)
NA_HEADS = 8
NA_DH = BRANCH_W // NA_HEADS
NA_KH = 8
NA_KW = 16
N_EXPERTS = 16
EC_FACTOR = 2
ROPE_BASE = 10000.0
RMS_EPS = 1e-6
N_MOD = 6
SPLIT_SIZES = (M_HEADS * M_DQK, M_HEADS * M_DQK, M_HEADS * M_DV, M_HEADS * M_DV, 4 * M_HEADS,
               POOL_GROUPS * POOL_GC, NA_HEADS * NA_DH, NA_HEADS * NA_DH, NA_HEADS * NA_DH,
               N_BRANCH * D_MODEL)
PROJ_W = sum(SPLIT_SIZES)

LANES = 128
VMEM_LIMIT = 56 * 1024 * 1024
MASK_NEG = -1e30
GATE_W = LANES

IN_GROUPS = (("mqk", 2 * M_HEADS * M_DQK), ("mv", BRANCH_W), ("mo", BRANCH_W), ("pu", BRANCH_W),
             ("qn", BRANCH_W), ("kn", BRANCH_W), ("vn", BRANCH_W), ("bg", N_BRANCH * D_MODEL),
             ("g", GATE_W))
IN_W = sum(w for _, w in IN_GROUPS)
MM_COLS = 512


def _const_spec(shape):
    nd = len(shape)
    return pl.BlockSpec(shape, lambda *_: (0,) * nd, pipeline_mode=pl.Buffered(1))


def _in_proj_kernel(x_ref, g_ref, shift_ref, scale_ref, w_ref, gb_ref, *o_refs):
    x = x_ref[0]
    y = x * lax.rsqrt(jnp.mean(x * x, axis=-1, keepdims=True) + RMS_EPS)
    h = ((y * g_ref[...]) * (1.0 + scale_ref[0]) + shift_ref[0]).astype(jnp.bfloat16)
    off = 0
    for (name, width), o_ref in zip(IN_GROUPS, o_refs):
        for c0 in range(0, width, MM_COLS):
            cw = min(MM_COLS, width - c0)
            acc = jnp.dot(h, w_ref[:, off + c0:off + c0 + cw], preferred_element_type=jnp.float32)
            if name == "g":
                acc = acc + gb_ref[...]
            o_ref[0, :, c0:c0 + cw] = acc.astype(o_ref.dtype)
        off += width


def in_proj(x, g, shift, scale, w_perm, gate_bias_row, *, tm):
    B, T, D = x.shape
    out_shape = [jax.ShapeDtypeStruct((B, T, w), jnp.float32 if n == "g" else jnp.bfloat16) for n, w in IN_GROUPS]
    out_specs = [pl.BlockSpec((1, tm, w), lambda b, i: (b, i, 0)) for _, w in IN_GROUPS]
    return pl.pallas_call(
        _in_proj_kernel,
        grid=(B, T // tm),
        in_specs=[
            pl.BlockSpec((1, tm, D), lambda b, i: (b, i, 0)),
            _const_spec((1, D)),
            pl.BlockSpec((1, 1, D), lambda b, i: (b, 0, 0)),
            pl.BlockSpec((1, 1, D), lambda b, i: (b, 0, 0)),
            _const_spec((D, IN_W)),
            _const_spec((1, GATE_W)),
        ],
        out_specs=out_specs,
        out_shape=out_shape,
        compiler_params=pltpu.CompilerParams(
            dimension_semantics=("arbitrary", "arbitrary"), vmem_limit_bytes=VMEM_LIMIT),
        name="in_proj",
    )(x, g, shift, scale, w_perm, gate_bias_row)


def permute_w_in(w_in_l):
    qm, km, vm, om, gm, pu, qn, kn, vn, bg = jnp.split(w_in_l, [int(s) for s in np.cumsum(SPLIT_SIZES)[:-1]], axis=-1)
    gpad = jnp.pad(gm, ((0, 0), (0, GATE_W - gm.shape[1])))
    w = jnp.concatenate([qm * (M_DQK ** -0.5), km, vm, om, pu, qn * (NA_DH ** -0.5), kn, vn, bg, gpad], axis=-1)
    return w.astype(jnp.bfloat16)


def _pair_attention(qp, k_parts, v_parts, biases):
    n = qp.shape[0]
    lane_lo = lax.broadcasted_iota(jnp.int32, (n, LANES), 1) < NA_DH
    zero = jnp.zeros_like(qp)
    qq = jnp.concatenate([jnp.where(lane_lo, qp, zero), jnp.where(lane_lo, zero, qp)], axis=0)
    scores = []
    for kp, bias in zip(k_parts, biases):
        s = lax.dot_general(qq, kp, (((1,), (1,)), ((), ())), preferred_element_type=jnp.float32)
        scores.append(s if bias is None else s + bias)
    m = scores[0].max(axis=-1, keepdims=True)
    for s in scores[1:]:
        m = jnp.maximum(m, s.max(axis=-1, keepdims=True))
    l = None
    o = None
    for s, vp in zip(scores, v_parts):
        p = jnp.exp(s - m)
        ls = p.sum(axis=-1, keepdims=True)
        os_ = jnp.dot(p.astype(jnp.bfloat16), vp, preferred_element_type=jnp.float32)
        l = ls if l is None else l + ls
        o = os_ if o is None else o + os_
    o = o * (1.0 / l)
    return jnp.where(lane_lo, o[:n], o[n:])


def _na_kernel(q_ref, k_ref, v_ref, kc_ref, vc_ref, bias_ref, *rest, need_ctx):
    if need_ctx:
        qc_ref, o_ref, oc_ref = rest
    else:
        (o_ref,) = rest
    S = q_ref.shape[1]
    rows = S // GRID_W
    n_pairs = BRANCH_W // LANES

    def row_body(r, carry):
        rs = jnp.clip(r - NA_KH // 2, 0, rows - NA_KH)
        var = r - rs
        q0 = pl.multiple_of(r * GRID_W, GRID_W)
        k0 = pl.multiple_of(rs * GRID_W, GRID_W)
        outs = []
        for j in range(n_pairs):
            ls = slice(j * LANES, (j + 1) * LANES)
            qp = q_ref[0, pl.ds(q0, GRID_W), ls]
            kw = k_ref[0, pl.ds(k0, NA_KH * GRID_W), ls]
            vw = v_ref[0, pl.ds(k0, NA_KH * GRID_W), ls]
            outs.append(_pair_attention(qp, (kw, kc_ref[0, :, ls]), (vw, vc_ref[0, :, ls]),
                                        (bias_ref[var, j], None)))
        o_ref[0, pl.ds(q0, GRID_W), :] = jnp.concatenate(outs, axis=-1).astype(o_ref.dtype)
        return carry

    lax.fori_loop(0, rows, row_body, 0)

    if need_ctx:
        outs = []
        for j in range(n_pairs):
            ls = slice(j * LANES, (j + 1) * LANES)
            outs.append(_pair_attention(qc_ref[0, :, ls], (kc_ref[0, :, ls],), (vc_ref[0, :, ls],),
                                        (None,)))
        oc_ref[0] = jnp.concatenate(outs, axis=-1).astype(oc_ref.dtype)


def na_bias_table(rpb):
    H = rpb.shape[0]
    var = jnp.arange(NA_KH)
    kr = jnp.arange(NA_KH)
    dr = kr[None, :] - var[:, None] + NA_KH - 1
    cols = jnp.arange(GRID_W)
    dc = jnp.clip(cols[None, :] - cols[:, None] + NA_KW - 1, 0, 2 * NA_KW - 2)
    cs = jnp.clip(cols - NA_KW // 2, 0, GRID_W - NA_KW)
    colmask = (cols[None, :] >= cs[:, None]) & (cols[None, :] < cs[:, None] + NA_KW)
    tab = rpb[:, dr][:, :, :, dc]
    tab = jnp.where(colmask[None, None, None], tab.astype(jnp.float32), MASK_NEG)
    tab = tab.transpose(1, 0, 3, 2, 4)
    return tab.reshape(NA_KH, H // 2, 2 * GRID_W, NA_KH * GRID_W)


def neighbourhood_attention(qn, kn, vn, kc, vc, bias_tab, qc=None):
    B, S, W = qn.shape
    Tc = kc.shape[1]
    need_ctx = qc is not None
    lat = pl.BlockSpec((1, S, W), lambda b: (b, 0, 0))
    cx = pl.BlockSpec((1, Tc, W), lambda b: (b, 0, 0))
    in_specs = [lat, lat, lat, cx, cx, _const_spec(bias_tab.shape)]
    args = [qn, kn, vn, kc, vc, bias_tab]
    out_shape = [jax.ShapeDtypeStruct((B, S, W), jnp.bfloat16)]
    out_specs = [lat]
    if need_ctx:
        in_specs.append(cx)
        args.append(qc)
        out_shape.append(jax.ShapeDtypeStruct((B, Tc, W), jnp.bfloat16))
        out_specs.append(cx)
    res = pl.pallas_call(
        functools.partial(_na_kernel, need_ctx=need_ctx),
        grid=(B,),
        in_specs=in_specs,
        out_specs=out_specs,
        out_shape=out_shape,
        compiler_params=pltpu.CompilerParams(dimension_semantics=("arbitrary",), vmem_limit_bytes=VMEM_LIMIT),
        name="neighbourhood_attention",
    )(*args)
    return (res[0], res[1]) if need_ctx else (res[0], None)


def rmsnorm(x, g):
    xf = x.astype(jnp.float32)
    y = xf * lax.rsqrt(jnp.mean(xf * xf, axis=-1, keepdims=True) + RMS_EPS)
    return (y * g.astype(jnp.float32)).astype(x.dtype)


def modulate(h, shift, scale):
    return h * (1 + scale) + shift


def rope_tables(n_tokens, dtype):
    t = jnp.arange(n_tokens)
    row = (t // GRID_W).astype(jnp.float32)
    col = (t % GRID_W).astype(jnp.float32)
    half = M_DQK // 2
    inv = ROPE_BASE ** (-jnp.arange(0, half, 2, dtype=jnp.float32) / half)
    ar = row[:, None] * inv[None, :]
    ac = col[:, None] * inv[None, :]
    return tuple(a.astype(dtype) for a in (jnp.cos(ar), jnp.sin(ar), jnp.cos(ac), jnp.sin(ac)))


def _rotate(x, cos, sin):
    x1, x2 = jnp.split(x, 2, axis=-1)
    cos = cos[:, None, :]
    sin = sin[:, None, :]
    return jnp.concatenate([x1 * cos - x2 * sin, x2 * cos + x1 * sin], axis=-1)


def rope_2d(x, tabs):
    cr, sr, cc, sc = tabs
    xr, xc = jnp.split(x, 2, axis=-1)
    return jnp.concatenate([_rotate(xr, cr, sr), _rotate(xc, cc, sc)], axis=-1)


def mlstm_scan(q, k, v, li, lf, state):
    B, T, H, _ = q.shape
    dv = v.shape[-1]
    L = M_CHUNK
    N = T // L

    def chunks(a):
        a = a.reshape((B, N, L, H) + a.shape[3:])
        return jnp.moveaxis(a, (1, 3), (0, 2))

    tril = jnp.tril(jnp.ones((L, L), dtype=bool))

    def step(carry, inp):
        C, n, m = carry
        qc, kc, vc, ic, fc = inp
        b = jnp.cumsum(fc, axis=-1)
        dmat = jnp.where(tril, b[..., :, None] - b[..., None, :] + ic[..., None, :], -jnp.inf)
        g = b + m[..., None]
        mt = jnp.maximum(g, dmat.max(-1))
        sc = jnp.einsum('bhtd,bhsd->bhts', qc, kc) * jnp.exp(dmat - mt[..., None])
        inter = jnp.exp(g - mt)
        num = inter[..., None] * jnp.einsum('bhvd,bhtd->bhtv', C, qc) + jnp.einsum('bhts,bhsv->bhtv', sc, vc)
        den = inter * jnp.einsum('bhd,bhtd->bht', n, qc) + sc.sum(-1)
        h = num / jnp.maximum(jnp.abs(den), jnp.exp(-mt))[..., None]
        bl = b[..., -1]
        w = bl[..., None] - b + ic
        m_new = jnp.maximum(bl + m, w.max(-1))
        decay = jnp.exp(bl + m - m_new)
        ws = jnp.exp(w - m_new[..., None])
        C_new = decay[..., None, None] * C + jnp.einsum('bhs,bhsv,bhsd->bhvd', ws, vc, kc)
        n_new = decay[..., None] * n + jnp.einsum('bhs,bhsd->bhd', ws, kc)
        return (C_new, n_new, m_new), h

    xs = (chunks(q), chunks(k), chunks(v), chunks(li), chunks(lf))
    state, h = lax.scan(step, state, xs)
    h = jnp.moveaxis(h, (0, 2), (1, 3)).reshape(B, T, H, dv)
    return state, h


def head_rms(h):
    return h * lax.rsqrt(jnp.mean(h * h, axis=-1, keepdims=True) + RMS_EPS)


def mlstm_branch(ctx_parts, lat_parts, need_ctx):
    f32 = jnp.float32

    def prep(parts):
        q, k, v, o, gt = parts
        q = q.astype(f32)
        k = k.astype(f32)
        v = v.astype(f32)
        fwd = (q, k, v, gt[..., 0, :], jax.nn.log_sigmoid(gt[..., 1, :]))
        bwd = tuple(jnp.flip(a, axis=1) for a in (q, k, v, gt[..., 2, :], jax.nn.log_sigmoid(gt[..., 3, :])))
        return fwd, bwd

    c_f, c_b = prep(ctx_parts)
    l_f, l_b = prep(lat_parts)
    B = l_f[0].shape[0]
    st0 = (jnp.zeros((B, M_HEADS, M_DV, M_DQK), f32), jnp.zeros((B, M_HEADS, M_DQK), f32),
           jnp.full((B, M_HEADS), M_INIT, f32))
    st_f, hc_f = mlstm_scan(*c_f, st0)
    _, hl_f = mlstm_scan(*l_f, st_f)
    st_b, hc_b = mlstm_scan(*c_b, st0)
    _, hl_b = mlstm_scan(*l_b, st_b)

    def finish(h_f, h_b_rev, o):
        h = head_rms(h_f + jnp.flip(h_b_rev, axis=1))
        return jax.nn.sigmoid(o) * h.reshape(o.shape).astype(o.dtype)

    out_l = finish(hl_f, hl_b, lat_parts[3])
    out_c = finish(hc_f, hc_b, ctx_parts[3]) if need_ctx else None
    return out_l, out_c


def pool_branch(u, pool_w, pool_scale):
    B, T, _ = u.shape
    uf = u.astype(jnp.float32).reshape(B, T, POOL_GROUPS, POOL_GC)
    csum = jnp.concatenate([jnp.zeros((B, 1, POOL_GROUPS, POOL_GC), jnp.float32),
                            jnp.cumsum(uf, axis=1)], axis=1)
    t = jnp.arange(T)
    outs = []
    for gi, w in enumerate(POOL_WINDOWS):
        lo = jnp.clip(t - w // 2, 0, T - 1)
        hi = jnp.clip(t - w // 2 + w - 1, 0, T - 1)
        cnt = (hi - lo + 1).astype(jnp.float32)[:, None]
        cg = csum[:, :, gi]
        mean = (cg[:, hi + 1] - cg[:, lo]) / cnt
        outs.append(jnp.einsum('btc,cd->btd', (mean - uf[:, :, gi]).astype(u.dtype), pool_w[gi]))
    return jnp.concatenate(outs, axis=-1) * pool_scale


def merge_branches(branches, bg, w_branch, w_out):
    D = w_out.shape[0]
    acc = None
    for bi, br in enumerate(branches):
        term = jax.nn.sigmoid(bg[..., bi * D:(bi + 1) * D]) * (br @ w_branch[bi])
        acc = term if acc is None else acc + term
    return acc @ w_out


def ec_moe(h, w_router, w_gate, w_up, w_down):
    B, T, D = h.shape
    cap = EC_FACTOR * T // N_EXPERTS
    aff = jax.nn.softmax(jnp.einsum('btd,de->bte', h, w_router, preferred_element_type=jnp.float32), axis=-1)
    gate, idx = lax.top_k(jnp.swapaxes(aff, 1, 2), cap)
    xe = jax.vmap(lambda hb, ib: hb[ib])(h, idx)
    hid = jax.nn.silu(jnp.einsum('becd,edf->becf', xe, w_gate)) * jnp.einsum('becd,edf->becf', xe, w_up)
    ye = jnp.einsum('becf,efd->becd', hid, w_down) * gate[..., None].astype(h.dtype)
    return jax.vmap(lambda ib, yb: jnp.zeros((T, D), yb.dtype).at[ib.reshape(-1)].add(yb.reshape(-1, D)))(idx, ye)


def _mparts(p, rope_tabs=None):
    B, T, _ = p["mqk"].shape
    f32 = jnp.float32
    hq = M_HEADS * M_DQK
    q = p["mqk"][..., :hq].astype(f32).reshape(B, T, M_HEADS, M_DQK)
    k = p["mqk"][..., hq:].astype(f32).reshape(B, T, M_HEADS, M_DQK)
    if rope_tabs is not None:
        q = rope_2d(q, rope_tabs)
        k = rope_2d(k, rope_tabs)
    v = p["mv"].astype(f32).reshape(B, T, M_HEADS, M_DV)
    return q, k, v, p["mo"].astype(f32), p["g"][..., :4 * M_HEADS].reshape(B, T, 4, M_HEADS)


def kernel(x, c, ctx, c_ctx, norm_gain, ada_w, ada_b, w_in, mlstm_gate_bias, pool_w, pool_scale,
           na_rpb, w_branch, w_out, router_w, w_gate, w_up, w_down):
    B, S, D = x.shape
    f32 = jnp.float32
    tabs = rope_tables(S, x.dtype)
    names = [n for n, _ in IN_GROUPS]
    for l in range(DEPTH):
        need_ctx = l < DEPTH - 1
        g = norm_gain[l]
        mod_l = jnp.split((jax.nn.silu(c) @ ada_w[l] + ada_b[l])[:, None, :], N_MOD, axis=-1)
        mod_c = jnp.split(jax.nn.silu(c_ctx) @ ada_w[l] + ada_b[l], N_MOD, axis=-1)
        w_perm = permute_w_in(w_in[l])
        gb_row = jnp.pad(mlstm_gate_bias[l].reshape(1, -1).astype(f32), ((0, 0), (0, GATE_W - 4 * M_HEADS)))
        pl_ = dict(zip(names, in_proj(x, g[0][None], mod_l[0], mod_l[1], w_perm, gb_row, tm=512)))
        sh_c = jnp.broadcast_to(mod_c[0][None, None], (B, 1, D))
        sc_c = jnp.broadcast_to(mod_c[1][None, None], (B, 1, D))
        pc_ = dict(zip(names, in_proj(ctx, g[0][None], sh_c, sc_c, w_perm, gb_row, tm=ctx.shape[1])))

        m_l, m_c = mlstm_branch(_mparts(pc_), _mparts(pl_, tabs), need_ctx)
        n_l, n_c = neighbourhood_attention(pl_["qn"], pl_["kn"], pl_["vn"], pc_["kn"], pc_["vn"],
                                           na_bias_table(na_rpb[l]), pc_["qn"] if need_ctx else None)
        p_l = pool_branch(pl_["pu"].astype(f32), pool_w[l], pool_scale[l])
        y_l = merge_branches((m_l, p_l, n_l.astype(f32)), pl_["bg"].astype(f32), w_branch[l], w_out[l])
        x = x + mod_l[2] * rmsnorm(y_l, g[1])
        h2 = modulate(rmsnorm(x, g[2]), mod_l[3], mod_l[4])
        x = x + mod_l[5] * rmsnorm(ec_moe(h2, router_w[l], w_gate[l], w_up[l], w_down[l]), g[3])
        if need_ctx:
            p_c = pool_branch(pc_["pu"].astype(f32), pool_w[l], pool_scale[l])
            y_c = merge_branches((m_c, p_c, n_c.astype(f32)), pc_["bg"].astype(f32), w_branch[l], w_out[l])
            ctx = ctx + mod_c[2] * rmsnorm(y_c, g[1])
            h2c = modulate(rmsnorm(ctx, g[2]), mod_c[3], mod_c[4])
            ctx = ctx + mod_c[5] * rmsnorm(ec_moe(h2c, router_w[l], w_gate[l], w_up[l], w_down[l]), g[3])
    return x
```

```python
import functools
import math

import jax
import jax.numpy as jnp
import numpy as np
from jax import lax
from jax.experimental import pallas as pl
from jax.experimental.pallas import tpu as pltpu

D_MODEL = 1024
DEPTH = 2
GRID_W = 64
BRANCH_W = D_MODEL // 2
N_BRANCH = 3
M_HEADS = 4
M_DV = BRANCH_W // M_HEADS
M_DQK = M_DV // 2
M_CHUNK = 128
M_INIT = -1e30
POOL_GROUPS = 4
POOL_GC = BRANCH_W // POOL_GROUPS
POOL_WINDOWS = (2, 4, 8, 16)
NA_HEADS = 8
NA_DH = BRANCH_W // NA_HEADS
NA_KH = 8
NA_KW = 16
N_EXPERTS = 16
EC_FACTOR = 2
ROPE_BASE = 10000.0
RMS_EPS = 1e-6
N_MOD = 6
SPLIT_SIZES = (M_HEADS * M_DQK, M_HEADS * M_DQK, M_HEADS * M_DV, M_HEADS * M_DV, 4 * M_HEADS,
               POOL_GROUPS * POOL_GC, NA_HEADS * NA_DH, NA_HEADS * NA_DH, NA_HEADS * NA_DH,
               N_BRANCH * D_MODEL)
PROJ_W = sum(SPLIT_SIZES)

LANES = 128
VMEM_LIMIT = 56 * 1024 * 1024
MASK_NEG = -1e30
GATE_W = LANES

IN_GROUPS = (("mqk", 2 * M_HEADS * M_DQK), ("mv", BRANCH_W), ("mo", BRANCH_W), ("pu", BRANCH_W),
             ("qn", BRANCH_W), ("kn", BRANCH_W), ("vn", BRANCH_W), ("bg", N_BRANCH * D_MODEL),
             ("g", GATE_W))
IN_W = sum(w for _, w in IN_GROUPS)
MM_COLS = 512


def _const_spec(shape):
    nd = len(shape)
    return pl.BlockSpec(shape, lambda *_: (0,) * nd, pipeline_mode=pl.Buffered(1))


def _rope_tile(x, cos, sin_signed):
    half = M_DQK // 4
    lane = lax.broadcasted_iota(jnp.int32, x.shape, 1)
    partner = jnp.where((lane % (2 * half)) < half, pltpu.roll(x, LANES - half, 1), pltpu.roll(x, half, 1))
    return x * cos + partner * sin_signed


def _in_proj_kernel(x_ref, g_ref, shift_ref, scale_ref, w_ref, gb_ref, *rest, rope):
    if rope:
        cos_ref, sin_ref = rest[:2]
        o_refs = rest[2:]
    else:
        o_refs = rest
    x = x_ref[0]
    y = x * lax.rsqrt(jnp.mean(x * x, axis=-1, keepdims=True) + RMS_EPS)
    h = ((y * g_ref[...]) * (1.0 + scale_ref[0]) + shift_ref[0]).astype(jnp.bfloat16)
    off = 0
    for (name, width), o_ref in zip(IN_GROUPS, o_refs):
        for c0 in range(0, width, MM_COLS):
            cw = min(MM_COLS, width - c0)
            acc = jnp.dot(h, w_ref[:, off + c0:off + c0 + cw], preferred_element_type=jnp.float32)
            if name == "g":
                acc = acc + gb_ref[...]
            if name == "mqk" and rope:
                cos, sin = cos_ref[...], sin_ref[...]
                acc = jnp.concatenate([_rope_tile(acc[:, t:t + LANES], cos, sin) for t in range(0, cw, LANES)], axis=-1)
            o_ref[0, :, c0:c0 + cw] = acc.astype(o_ref.dtype)
        off += width


def in_proj(x, g, shift, scale, w_perm, gate_bias_row, rope_tabs=None, *, tm):
    B, T, D = x.shape
    out_shape = [jax.ShapeDtypeStruct((B, T, w), jnp.float32 if n == "g" else jnp.bfloat16) for n, w in IN_GROUPS]
    out_specs = [pl.BlockSpec((1, tm, w), lambda b, i: (b, i, 0)) for _, w in IN_GROUPS]
    in_specs = [
        pl.BlockSpec((1, tm, D), lambda b, i: (b, i, 0)),
        _const_spec((1, D)),
        pl.BlockSpec((1, 1, D), lambda b, i: (b, 0, 0)),
        pl.BlockSpec((1, 1, D), lambda b, i: (b, 0, 0)),
        _const_spec((D, IN_W)),
        _const_spec((1, GATE_W)),
    ]
    args = [x, g, shift, scale, w_perm, gate_bias_row]
    if rope_tabs is not None:
        in_specs += [pl.BlockSpec((tm, LANES), lambda b, i: (i, 0))] * 2
        args += list(rope_tabs)
    return pl.pallas_call(
        functools.partial(_in_proj_kernel, rope=rope_tabs is not None),
        grid=(B, T // tm),
        in_specs=in_specs,
        out_specs=out_specs,
        out_shape=out_shape,
        compiler_params=pltpu.CompilerParams(
            dimension_semantics=("arbitrary", "arbitrary"), vmem_limit_bytes=VMEM_LIMIT),
        name="in_proj",
    )(*args)


def rope_lane_tables(n_tokens):
    t = jnp.arange(n_tokens)
    row = (t // GRID_W).astype(jnp.float32)
    col = (t % GRID_W).astype(jnp.float32)
    half = M_DQK // 2
    inv = ROPE_BASE ** (-jnp.arange(0, half, 2, dtype=jnp.float32) / half)
    ar = row[:, None] * inv[None, :]
    ac = col[:, None] * inv[None, :]
    cos = jnp.concatenate([jnp.cos(ar), jnp.cos(ar), jnp.cos(ac), jnp.cos(ac)], axis=-1)
    sin = jnp.concatenate([-jnp.sin(ar), jnp.sin(ar), -jnp.sin(ac), jnp.sin(ac)], axis=-1)
    return jnp.tile(cos, (1, LANES // M_DQK)), jnp.tile(sin, (1, LANES // M_DQK))


def permute_w_in(w_in_l):
    qm, km, vm, om, gm, pu, qn, kn, vn, bg = jnp.split(w_in_l, [int(s) for s in np.cumsum(SPLIT_SIZES)[:-1]], axis=-1)
    gpad = jnp.pad(gm, ((0, 0), (0, GATE_W - gm.shape[1])))
    w = jnp.concatenate([qm * (M_DQK ** -0.5), km, vm, om, pu, qn * (NA_DH ** -0.5), kn, vn, bg, gpad], axis=-1)
    return w.astype(jnp.bfloat16)


def _pair_attention(qp, k_parts, v_parts, biases):
    n = qp.shape[0]
    lane_lo = lax.broadcasted_iota(jnp.int32, (n, LANES), 1) < NA_DH
    zero = jnp.zeros_like(qp)
    qq = jnp.concatenate([jnp.where(lane_lo, qp, zero), jnp.where(lane_lo, zero, qp)], axis=0)
    scores = []
    for kp, bias in zip(k_parts, biases):
        s = lax.dot_general(qq, kp, (((1,), (1,)), ((), ())), preferred_element_type=jnp.float32)
        scores.append(s if bias is None else s + bias)
    m = scores[0].max(axis=-1, keepdims=True)
    for s in scores[1:]:
        m = jnp.maximum(m, s.max(axis=-1, keepdims=True))
    l = None
    o = None
    for s, vp in zip(scores, v_parts):
        p = jnp.exp(s - m)
        ls = p.sum(axis=-1, keepdims=True)
        os_ = jnp.dot(p.astype(jnp.bfloat16), vp, preferred_element_type=jnp.float32)
        l = ls if l is None else l + ls
        o = os_ if o is None else o + os_
    o = o * (1.0 / l)
    return jnp.where(lane_lo, o[:n], o[n:])


def _na_kernel(q_ref, k_ref, v_ref, kc_ref, vc_ref, bias_ref, *rest, need_ctx):
    if need_ctx:
        qc_ref, o_ref, oc_ref = rest
    else:
        (o_ref,) = rest
    S = q_ref.shape[1]
    rows = S // GRID_W
    n_pairs = BRANCH_W // LANES

    def row_body(r, carry):
        rs = jnp.clip(r - NA_KH // 2, 0, rows - NA_KH)
        var = r - rs
        q0 = pl.multiple_of(r * GRID_W, GRID_W)
        k0 = pl.multiple_of(rs * GRID_W, GRID_W)
        outs = []
        for j in range(n_pairs):
            ls = slice(j * LANES, (j + 1) * LANES)
            qp = q_ref[0, pl.ds(q0, GRID_W), ls]
            kw = k_ref[0, pl.ds(k0, NA_KH * GRID_W), ls]
            vw = v_ref[0, pl.ds(k0, NA_KH * GRID_W), ls]
            outs.append(_pair_attention(qp, (kw, kc_ref[0, :, ls]), (vw, vc_ref[0, :, ls]),
                                        (bias_ref[var, j], None)))
        o_ref[0, pl.ds(q0, GRID_W), :] = jnp.concatenate(outs, axis=-1).astype(o_ref.dtype)
        return carry

    lax.fori_loop(0, rows, row_body, 0)

    if need_ctx:
        outs = []
        for j in range(n_pairs):
            ls = slice(j * LANES, (j + 1) * LANES)
            outs.append(_pair_attention(qc_ref[0, :, ls], (kc_ref[0, :, ls],), (vc_ref[0, :, ls],),
                                        (None,)))
        oc_ref[0] = jnp.concatenate(outs, axis=-1).astype(oc_ref.dtype)


def na_bias_table(rpb):
    H = rpb.shape[0]
    var = jnp.arange(NA_KH)
    kr = jnp.arange(NA_KH)
    dr = kr[None, :] - var[:, None] + NA_KH - 1
    cols = jnp.arange(GRID_W)
    dc = jnp.clip(cols[None, :] - cols[:, None] + NA_KW - 1, 0, 2 * NA_KW - 2)
    cs = jnp.clip(cols - NA_KW // 2, 0, GRID_W - NA_KW)
    colmask = (cols[None, :] >= cs[:, None]) & (cols[None, :] < cs[:, None] + NA_KW)
    tab = rpb[:, dr][:, :, :, dc]
    tab = jnp.where(colmask[None, None, None], tab.astype(jnp.float32), MASK_NEG)
    tab = tab.transpose(1, 0, 3, 2, 4)
    return tab.reshape(NA_KH, H // 2, 2 * GRID_W, NA_KH * GRID_W)


def neighbourhood_attention(qn, kn, vn, kc, vc, bias_tab, qc=None):
    B, S, W = qn.shape
    Tc = kc.shape[1]
    need_ctx = qc is not None
    lat = pl.BlockSpec((1, S, W), lambda b: (b, 0, 0))
    cx = pl.BlockSpec((1, Tc, W), lambda b: (b, 0, 0))
    in_specs = [lat, lat, lat, cx, cx, _const_spec(bias_tab.shape)]
    args = [qn, kn, vn, kc, vc, bias_tab]
    out_shape = [jax.ShapeDtypeStruct((B, S, W), jnp.bfloat16)]
    out_specs = [lat]
    if need_ctx:
        in_specs.append(cx)
        args.append(qc)
        out_shape.append(jax.ShapeDtypeStruct((B, Tc, W), jnp.bfloat16))
        out_specs.append(cx)
    res = pl.pallas_call(
        functools.partial(_na_kernel, need_ctx=need_ctx),
        grid=(B,),
        in_specs=in_specs,
        out_specs=out_specs,
        out_shape=out_shape,
        compiler_params=pltpu.CompilerParams(dimension_semantics=("arbitrary",), vmem_limit_bytes=VMEM_LIMIT),
        name="neighbourhood_attention",
    )(*args)
    return (res[0], res[1]) if need_ctx else (res[0], None)


def _split_bf16(x):
    hi = x.astype(jnp.bfloat16)
    r1 = x - hi.astype(jnp.float32)
    mid = r1.astype(jnp.bfloat16)
    lo = (r1 - mid.astype(jnp.float32)).astype(jnp.bfloat16)
    return hi, mid, lo


def _dot_f32(a, b):
    return jnp.dot(a, b, preferred_element_type=jnp.float32)


def _log_sigmoid(x):
    return jnp.minimum(x, 0.0) - jnp.log(1.0 + jnp.exp(-jnp.abs(x)))


def _mlstm_chunk(qk_ref, v_ref, g_ref, gt_ref, h_ref, ct_ref, m_ref, ci, fwd):
    L = M_CHUNK
    f32, bf16 = jnp.float32, jnp.bfloat16
    r0 = pl.multiple_of(ci * L, L)
    rows = pl.ds(r0, L)
    sq_r = lax.broadcasted_iota(jnp.int32, (L, L), 0)
    sq_c = lax.broadcasted_iota(jnp.int32, (L, L), 1)
    tri_l = jnp.where(sq_r >= sq_c, 1.0, 0.0).astype(bf16)
    tri_u = jnp.where(sq_r <= sq_c, 1.0, 0.0).astype(bf16)
    gates = g_ref[0, rows, :]
    gates_t = gt_ref[0, ci]
    lf, lf_t = _log_sigmoid(gates), _log_sigmoid(gates_t)
    cum = sum(_dot_f32(tri_l if fwd else tri_u, p) for p in _split_bf16(lf))
    cum_t = sum(_dot_f32(p, tri_u if fwd else tri_l) for p in _split_bf16(lf_t))
    t_i = 0 if fwd else 2
    last = L - 1 if fwd else 0
    st_r = lax.broadcasted_iota(jnp.int32, (2 * L, L), 0) & (L - 1)
    st_c = lax.broadcasted_iota(jnp.int32, (2 * L, L), 1)
    valid = (st_r >= st_c) if fwd else (st_r <= st_c)
    lane_lo = lax.broadcasted_iota(jnp.int32, (L, LANES), 1) < M_DQK
    row_lo = lax.broadcasted_iota(jnp.int32, (LANES, L), 0) < M_DQK
    ones = jnp.ones((L, LANES), bf16)

    def col2(tile, c0, c1):
        return jnp.concatenate([jnp.broadcast_to(tile[:, c0:c0 + 1], (L, LANES)),
                                jnp.broadcast_to(tile[:, c1:c1 + 1], (L, LANES))], axis=0)

    def row2(tile, c0, c1):
        return jnp.concatenate([jnp.broadcast_to(tile[c0:c0 + 1, :], (L, L)),
                                jnp.broadcast_to(tile[c1:c1 + 1, :], (L, L))], axis=0)

    for j in range(M_HEADS // 2):
        h0, h1 = 2 * j, 2 * j + 1
        sidx = 2 * j + (0 if fwd else 1)
        ci0, ci1 = t_i * M_HEADS + h0, t_i * M_HEADS + h1
        cf0, cf1 = ci0 + M_HEADS, ci1 + M_HEADS
        i_col, b_col = col2(gates, ci0, ci1), col2(cum, cf0, cf1)
        dm = jnp.where(valid, b_col - row2(cum_t, cf0, cf1) + row2(gates_t, ci0, ci1), MASK_NEG)
        m_col = m_ref[sidx]
        qp = qk_ref[0, rows, j * LANES:(j + 1) * LANES]
        kp = qk_ref[0, rows, M_HEADS * M_DQK + j * LANES:M_HEADS * M_DQK + (j + 1) * LANES]
        vx0 = jnp.concatenate([v_ref[0, rows, h0 * M_DV:(h0 + 1) * M_DV], ones], axis=-1)
        vx1 = jnp.concatenate([v_ref[0, rows, h1 * M_DV:(h1 + 1) * M_DV], ones], axis=-1)
        ct = ct_ref[sidx]
        if h_ref is not None:
            zq = jnp.zeros_like(qp)
            qq = jnp.concatenate([jnp.where(lane_lo, qp, zq), jnp.where(lane_lo, zq, qp)], axis=0)
            g_col = b_col + m_col
            mt = jnp.maximum(g_col, jnp.broadcast_to(dm.max(axis=-1, keepdims=True), (2 * L, LANES)))
            s = lax.dot_general(qq, kp, (((1,), (1,)), ((), ())), preferred_element_type=f32)
            sc = (s * jnp.exp(dm - mt)).astype(bf16)
            inter = jnp.exp(g_col - mt)
            t1 = _dot_f32(qq, ct.astype(bf16))
            t2 = jnp.concatenate([_dot_f32(sc[:L], vx0), _dot_f32(sc[L:], vx1)], axis=0)
            num = inter * t1[:, :M_DV] + t2[:, :M_DV]
            den = inter * t1[:, M_DV:] + t2[:, M_DV:]
            h = num / jnp.maximum(jnp.abs(den), jnp.exp(-mt))
            h_ref[rows, h0 * M_DV:(h0 + 1) * M_DV] = h[:L]
            h_ref[rows, h1 * M_DV:(h1 + 1) * M_DV] = h[L:]
        bl = jnp.concatenate([jnp.broadcast_to(b_col[last:last + 1], (L, LANES)),
                              jnp.broadcast_to(b_col[L + last:L + last + 1], (L, LANES))], axis=0)
        w = bl - b_col + i_col
        wm = jnp.concatenate([jnp.broadcast_to(w[:L].max(axis=0, keepdims=True), (L, LANES)),
                              jnp.broadcast_to(w[L:].max(axis=0, keepdims=True), (L, LANES))], axis=0)
        m_new = jnp.maximum(bl + m_col, wm)
        decay = jnp.exp(bl + m_col - m_new)
        ws = jnp.exp(w - m_new)
        kw_t = (kp.astype(f32) * jnp.where(lane_lo, ws[:L], ws[L:])).T
        zk = jnp.zeros_like(kw_t)
        upd = (_dot_f32(jnp.where(row_lo, kw_t, zk).astype(bf16), vx0)
               + _dot_f32(jnp.where(row_lo, zk, kw_t).astype(bf16), vx1))
        d_rows = jnp.concatenate([decay[:M_DQK], decay[L:L + M_DQK]], axis=0)
        ct_ref[sidx] = jnp.concatenate([d_rows, d_rows], axis=-1) * ct + upd
        m_ref[sidx] = m_new


def _mlstm_finish(hf_ref, hb_ref, o_ref, out_ref, n_chunks):
    L = M_CHUNK

    def body(ci, carry):
        rows = pl.ds(pl.multiple_of(ci * L, L), L)
        h = hf_ref[rows, :] + hb_ref[rows, :]
        parts = []
        for hd in range(M_HEADS):
            hh = h[:, hd * M_DV:(hd + 1) * M_DV]
            parts.append(hh * lax.rsqrt(jnp.mean(hh * hh, axis=-1, keepdims=True) + RMS_EPS))
        gate = jax.nn.sigmoid(o_ref[0, rows, :].astype(jnp.float32))
        out_ref[0, rows, :] = (gate * jnp.concatenate(parts, axis=-1)).astype(out_ref.dtype)
        return carry

    lax.fori_loop(0, n_chunks, body, 0)


def _mlstm_kernel(qk_c, v_c, o_c, g_c, gt_c, qk_l, v_l, o_l, g_l, gt_l, *rest, need_ctx):
    if need_ctx:
        out_l, out_c, hf_l, hb_l, hf_c, hb_c, ct_ref, m_ref = rest
    else:
        out_l, hf_l, hb_l, ct_ref, m_ref = rest
        hf_c = hb_c = None
    nc, nl = qk_c.shape[1] // M_CHUNK, qk_l.shape[1] // M_CHUNK
    ct_ref[...] = jnp.zeros_like(ct_ref)
    m_ref[...] = jnp.full_like(m_ref, M_INIT)

    def phase(qk, v, g, gt, hf, hb, n):
        def body(i, carry):
            _mlstm_chunk(qk, v, g, gt, hf, ct_ref, m_ref, i, True)
            _mlstm_chunk(qk, v, g, gt, hb, ct_ref, m_ref, n - 1 - i, False)
            return carry
        lax.fori_loop(0, n, body, 0)

    phase(qk_c, v_c, g_c, gt_c, hf_c, hb_c, nc)
    phase(qk_l, v_l, g_l, gt_l, hf_l, hb_l, nl)
    _mlstm_finish(hf_l, hb_l, o_l, out_l, nl)
    if need_ctx:
        _mlstm_finish(hf_c, hb_c, o_c, out_c, nc)


def _gates_chunk_major(g):
    B, T, _ = g.shape
    return g[..., :4 * M_HEADS].reshape(B, T // M_CHUNK, M_CHUNK, 4 * M_HEADS).transpose(0, 1, 3, 2)


def mlstm(pc, pl_, need_ctx):
    B, S, W = pl_["mv"].shape
    Tc = pc["mv"].shape[1]
    f32 = jnp.float32

    def specs(T):
        s = pl.BlockSpec((1, T, W), lambda b: (b, 0, 0))
        return [s, s, s, pl.BlockSpec((1, T, GATE_W), lambda b: (b, 0, 0)),
                pl.BlockSpec((1, T // M_CHUNK, 4 * M_HEADS, M_CHUNK), lambda b: (b, 0, 0, 0))]

    def args(p):
        return [p["mqk"], p["mv"], p["mo"], p["g"], _gates_chunk_major(p["g"])]

    out_shape = [jax.ShapeDtypeStruct((B, S, W), jnp.bfloat16)]
    out_specs = [pl.BlockSpec((1, S, W), lambda b: (b, 0, 0))]
    scratch = [pltpu.VMEM((S, W), f32), pltpu.VMEM((S, W), f32)]
    if need_ctx:
        out_shape.append(jax.ShapeDtypeStruct((B, Tc, W), jnp.bfloat16))
        out_specs.append(pl.BlockSpec((1, Tc, W), lambda b: (b, 0, 0)))
        scratch += [pltpu.VMEM((Tc, W), f32), pltpu.VMEM((Tc, W), f32)]
    scratch += [pltpu.VMEM((M_HEADS, LANES, 2 * M_DV), f32), pltpu.VMEM((M_HEADS, 2 * M_CHUNK, LANES), f32)]
    res = pl.pallas_call(
        functools.partial(_mlstm_kernel, need_ctx=need_ctx),
        grid=(B,),
        in_specs=specs(Tc) + specs(S),
        out_specs=out_specs,
        out_shape=out_shape,
        scratch_shapes=scratch,
        compiler_params=pltpu.CompilerParams(dimension_semantics=("arbitrary",), vmem_limit_bytes=VMEM_LIMIT),
        name="mlstm",
    )(*args(pc), *args(pl_))
    return (res[0], res[1]) if need_ctx else (res[0], None)


POOL_HALO = 16


def _pool_mix(pu_ref, prev_ref, next_ref, pw_ref, ps_ref, tile_idx, n_tiles, seq_len):
    f32 = jnp.float32
    tm = pu_ref.shape[1]
    cur = pu_ref[0].astype(f32)
    prev = jnp.where(tile_idx > 0, prev_ref[0].astype(f32), 0.0)
    nxt = jnp.where(tile_idx < n_tiles - 1, next_ref[0].astype(f32), 0.0)
    ext = jnp.concatenate([prev, cur, nxt], axis=0)
    n_ext = tm + 2 * POOL_HALO
    tok = tile_idx * tm + lax.broadcasted_iota(jnp.int32, (tm, POOL_GC), 0)

    def shifted(a, d):
        return pltpu.roll(a, (-d) % n_ext, 0)

    outs = []
    for gi, w in enumerate(POOL_WINDOWS):
        s = ext[:, gi * POOL_GC:(gi + 1) * POOL_GC]
        s = shifted(s, -1) + s
        span = 2
        while span < w:
            s = shifted(s, -(span // 2)) + shifted(s, span // 2)
            span *= 2
        lo = jnp.clip(tok - w // 2, 0, seq_len - 1)
        hi = jnp.clip(tok - w // 2 + w - 1, 0, seq_len - 1)
        mean = s[POOL_HALO:POOL_HALO + tm] / (hi - lo + 1).astype(f32)
        d = (mean - cur[:, gi * POOL_GC:(gi + 1) * POOL_GC]).astype(jnp.bfloat16)
        outs.append(_dot_f32(d, pw_ref[gi]))
    return jnp.concatenate(outs, axis=-1) * ps_ref[...]


def _pack_bf16_pairs(h):
    n = h.shape[1] // 2
    hb = h.astype(jnp.bfloat16).astype(jnp.float32)
    lo = pltpu.bitcast(hb[:, :n], jnp.uint32) >> 16
    hi = pltpu.bitcast(hb[:, n:], jnp.uint32) & jnp.uint32(0xFFFF0000)
    return lo | hi


def _merge_kernel(m_ref, n_ref, pu_ref, prev_ref, next_ref, bg_ref, x_ref, mods_ref, gains_ref,
                  pw_ref, ps_ref, wb_ref, wo_ref, wr_ref, x1_ref, h2p_ref, lg_ref, *, seq_len):
    f32, bf16 = jnp.float32, jnp.bfloat16
    i = pl.program_id(1)
    D = x_ref.shape[2]
    p = _pool_mix(pu_ref, prev_ref, next_ref, pw_ref, ps_ref, i, pl.num_programs(1), seq_len)
    branches = (m_ref[0], p.astype(bf16), n_ref[0])
    acc = None
    for bi, br in enumerate(branches):
        gate = jax.nn.sigmoid(bg_ref[0, :, bi * D:(bi + 1) * D].astype(f32))
        term = gate * _dot_f32(br, wb_ref[bi])
        acc = term if acc is None else acc + term
    y = _dot_f32(acc.astype(bf16), wo_ref[...])
    yn = y * lax.rsqrt(jnp.mean(y * y, axis=-1, keepdims=True) + RMS_EPS) * gains_ref[0:1, :]
    x1 = x_ref[0] + mods_ref[0, 0:1, :] * yn
    x1_ref[0] = x1
    xn = x1 * lax.rsqrt(jnp.mean(x1 * x1, axis=-1, keepdims=True) + RMS_EPS) * gains_ref[1:2, :]
    h2 = xn * (1.0 + mods_ref[0, 2:3, :]) + mods_ref[0, 1:2, :]
    h2p_ref[0] = _pack_bf16_pairs(h2)
    lg_ref[0] = lax.dot_general(wr_ref[...], h2.astype(bf16), (((1,), (1,)), ((), ())),
                                preferred_element_type=f32)


def merge_out(m, n, pu, bg, x, mods, gains, pool_w, pool_scale, w_branch, w_out, w_router_t, *, tm):
    B, T, D = x.shape
    W = m.shape[2]
    E = w_router_t.shape[0]
    hb = tm // POOL_HALO
    n_halo = T // POOL_HALO
    tile = lambda w: pl.BlockSpec((1, tm, w), lambda b, i: (b, i, 0))
    in_specs = [
        tile(W), tile(W), tile(W),
        pl.BlockSpec((1, POOL_HALO, W), lambda b, i: (b, jnp.maximum(i * hb - 1, 0), 0)),
        pl.BlockSpec((1, POOL_HALO, W), lambda b, i: (b, jnp.minimum((i + 1) * hb, n_halo - 1), 0)),
        tile(N_BRANCH * D), tile(D),
        pl.BlockSpec((1, 8, D), lambda b, i: (b, 0, 0)),
        _const_spec((8, D)),
        _const_spec(pool_w.shape), _const_spec(pool_scale.shape), _const_spec(w_branch.shape),
        _const_spec(w_out.shape), _const_spec(w_router_t.shape),
    ]
    return pl.pallas_call(
        functools.partial(_merge_kernel, seq_len=T),
        grid=(B, T // tm),
        in_specs=in_specs,
        out_specs=[tile(D), tile(D // 2), pl.BlockSpec((1, E, tm), lambda b, i: (b, 0, i))],
        out_shape=[jax.ShapeDtypeStruct((B, T, D), jnp.float32),
                   jax.ShapeDtypeStruct((B, T, D // 2), jnp.uint32),
                   jax.ShapeDtypeStruct((B, E, T), jnp.float32)],
        compiler_params=pltpu.CompilerParams(
            dimension_semantics=("arbitrary", "arbitrary"), vmem_limit_bytes=VMEM_LIMIT),
        name="merge_out",
    )(m, n, pu, pu, pu, bg, x, mods, gains, pool_w, pool_scale, w_branch, w_out, w_router_t)


def rmsnorm(x, g):
    xf = x.astype(jnp.float32)
    y = xf * lax.rsqrt(jnp.mean(xf * xf, axis=-1, keepdims=True) + RMS_EPS)
    return (y * g.astype(jnp.float32)).astype(x.dtype)


def modulate(h, shift, scale):
    return h * (1 + scale) + shift


def rope_tables(n_tokens, dtype):
    t = jnp.arange(n_tokens)
    row = (t // GRID_W).astype(jnp.float32)
    col = (t % GRID_W).astype(jnp.float32)
    half = M_DQK // 2
    inv = ROPE_BASE ** (-jnp.arange(0, half, 2, dtype=jnp.float32) / half)
    ar = row[:, None] * inv[None, :]
    ac = col[:, None] * inv[None, :]
    return tuple(a.astype(dtype) for a in (jnp.cos(ar), jnp.sin(ar), jnp.cos(ac), jnp.sin(ac)))


def _rotate(x, cos, sin):
    x1, x2 = jnp.split(x, 2, axis=-1)
    cos = cos[:, None, :]
    sin = sin[:, None, :]
    return jnp.concatenate([x1 * cos - x2 * sin, x2 * cos + x1 * sin], axis=-1)


def rope_2d(x, tabs):
    cr, sr, cc, sc = tabs
    xr, xc = jnp.split(x, 2, axis=-1)
    return jnp.concatenate([_rotate(xr, cr, sr), _rotate(xc, cc, sc)], axis=-1)


def mlstm_scan(q, k, v, li, lf, state):
    B, T, H, _ = q.shape
    dv = v.shape[-1]
    L = M_CHUNK
    N = T // L

    def chunks(a):
        a = a.reshape((B, N, L, H) + a.shape[3:])
        return jnp.moveaxis(a, (1, 3), (0, 2))

    tril = jnp.tril(jnp.ones((L, L), dtype=bool))

    def step(carry, inp):
        C, n, m = carry
        qc, kc, vc, ic, fc = inp
        b = jnp.cumsum(fc, axis=-1)
        dmat = jnp.where(tril, b[..., :, None] - b[..., None, :] + ic[..., None, :], -jnp.inf)
        g = b + m[..., None]
        mt = jnp.maximum(g, dmat.max(-1))
        sc = jnp.einsum('bhtd,bhsd->bhts', qc, kc) * jnp.exp(dmat - mt[..., None])
        inter = jnp.exp(g - mt)
        num = inter[..., None] * jnp.einsum('bhvd,bhtd->bhtv', C, qc) + jnp.einsum('bhts,bhsv->bhtv', sc, vc)
        den = inter * jnp.einsum('bhd,bhtd->bht', n, qc) + sc.sum(-1)
        h = num / jnp.maximum(jnp.abs(den), jnp.exp(-mt))[..., None]
        bl = b[..., -1]
        w = bl[..., None] - b + ic
        m_new = jnp.maximum(bl + m, w.max(-1))
        decay = jnp.exp(bl + m - m_new)
        ws = jnp.exp(w - m_new[..., None])
        C_new = decay[..., None, None] * C + jnp.einsum('bhs,bhsv,bhsd->bhvd', ws, vc, kc)
        n_new = decay[..., None] * n + jnp.einsum('bhs,bhsd->bhd', ws, kc)
        return (C_new, n_new, m_new), h

    xs = (chunks(q), chunks(k), chunks(v), chunks(li), chunks(lf))
    state, h = lax.scan(step, state, xs)
    h = jnp.moveaxis(h, (0, 2), (1, 3)).reshape(B, T, H, dv)
    return state, h


def head_rms(h):
    return h * lax.rsqrt(jnp.mean(h * h, axis=-1, keepdims=True) + RMS_EPS)


def mlstm_branch(ctx_parts, lat_parts, need_ctx):
    f32 = jnp.float32

    def prep(parts):
        q, k, v, o, gt = parts
        q = q.astype(f32)
        k = k.astype(f32)
        v = v.astype(f32)
        fwd = (q, k, v, gt[..., 0, :], jax.nn.log_sigmoid(gt[..., 1, :]))
        bwd = tuple(jnp.flip(a, axis=1) for a in (q, k, v, gt[..., 2, :], jax.nn.log_sigmoid(gt[..., 3, :])))
        return fwd, bwd

    c_f, c_b = prep(ctx_parts)
    l_f, l_b = prep(lat_parts)
    B = l_f[0].shape[0]
    st0 = (jnp.zeros((B, M_HEADS, M_DV, M_DQK), f32), jnp.zeros((B, M_HEADS, M_DQK), f32),
           jnp.full((B, M_HEADS), M_INIT, f32))
    st_f, hc_f = mlstm_scan(*c_f, st0)
    _, hl_f = mlstm_scan(*l_f, st_f)
    st_b, hc_b = mlstm_scan(*c_b, st0)
    _, hl_b = mlstm_scan(*l_b, st_b)

    def finish(h_f, h_b_rev, o):
        h = head_rms(h_f + jnp.flip(h_b_rev, axis=1))
        return jax.nn.sigmoid(o) * h.reshape(o.shape).astype(o.dtype)

    out_l = finish(hl_f, hl_b, lat_parts[3])
    out_c = finish(hc_f, hc_b, ctx_parts[3]) if need_ctx else None
    return out_l, out_c


def pool_branch(u, pool_w, pool_scale):
    B, T, _ = u.shape
    uf = u.astype(jnp.float32).reshape(B, T, POOL_GROUPS, POOL_GC)
    csum = jnp.concatenate([jnp.zeros((B, 1, POOL_GROUPS, POOL_GC), jnp.float32),
                            jnp.cumsum(uf, axis=1)], axis=1)
    t = jnp.arange(T)
    outs = []
    for gi, w in enumerate(POOL_WINDOWS):
        lo = jnp.clip(t - w // 2, 0, T - 1)
        hi = jnp.clip(t - w // 2 + w - 1, 0, T - 1)
        cnt = (hi - lo + 1).astype(jnp.float32)[:, None]
        cg = csum[:, :, gi]
        mean = (cg[:, hi + 1] - cg[:, lo]) / cnt
        outs.append(jnp.einsum('btc,cd->btd', (mean - uf[:, :, gi]).astype(u.dtype), pool_w[gi]))
    return jnp.concatenate(outs, axis=-1) * pool_scale


def merge_branches(branches, bg, w_branch, w_out):
    D = w_out.shape[0]
    acc = None
    for bi, br in enumerate(branches):
        term = jax.nn.sigmoid(bg[..., bi * D:(bi + 1) * D]) * (br @ w_branch[bi])
        acc = term if acc is None else acc + term
    return acc @ w_out


def ec_moe(h, w_router, w_gate, w_up, w_down):
    B, T, D = h.shape
    cap = EC_FACTOR * T // N_EXPERTS
    aff = jax.nn.softmax(jnp.einsum('btd,de->bte', h, w_router, preferred_element_type=jnp.float32), axis=-1)
    gate, idx = lax.top_k(jnp.swapaxes(aff, 1, 2), cap)
    xe = jax.vmap(lambda hb, ib: hb[ib])(h, idx)
    hid = jax.nn.silu(jnp.einsum('becd,edf->becf', xe, w_gate)) * jnp.einsum('becd,edf->becf', xe, w_up)
    ye = jnp.einsum('becf,efd->becd', hid, w_down) * gate[..., None].astype(h.dtype)
    return jax.vmap(lambda ib, yb: jnp.zeros((T, D), yb.dtype).at[ib.reshape(-1)].add(yb.reshape(-1, D)))(idx, ye)


def _mparts(p, rope_tabs=None):
    B, T, _ = p["mqk"].shape
    f32 = jnp.float32
    hq = M_HEADS * M_DQK
    q = p["mqk"][..., :hq].astype(f32).reshape(B, T, M_HEADS, M_DQK)
    k = p["mqk"][..., hq:].astype(f32).reshape(B, T, M_HEADS, M_DQK)
    if rope_tabs is not None:
        q = rope_2d(q, rope_tabs)
        k = rope_2d(k, rope_tabs)
    v = p["mv"].astype(f32).reshape(B, T, M_HEADS, M_DV)
    return q, k, v, p["mo"].astype(f32), p["g"][..., :4 * M_HEADS].reshape(B, T, 4, M_HEADS)


def kernel(x, c, ctx, c_ctx, norm_gain, ada_w, ada_b, w_in, mlstm_gate_bias, pool_w, pool_scale,
           na_rpb, w_branch, w_out, router_w, w_gate, w_up, w_down):
    B, S, D = x.shape
    Tc = ctx.shape[1]
    f32, bf16 = jnp.float32, jnp.bfloat16
    rope_tabs = rope_lane_tables(S)
    names = [n for n, _ in IN_GROUPS]
    for l in range(DEPTH):
        need_ctx = l < DEPTH - 1
        g = norm_gain[l]
        mod_l = jnp.split((jax.nn.silu(c) @ ada_w[l] + ada_b[l])[:, None, :], N_MOD, axis=-1)
        mod_c = jnp.split(jax.nn.silu(c_ctx) @ ada_w[l] + ada_b[l], N_MOD, axis=-1)
        mod_cb = [jnp.broadcast_to(m[None, None], (B, 1, D)) for m in mod_c]
        w_perm = permute_w_in(w_in[l])
        gb_row = jnp.pad(mlstm_gate_bias[l].reshape(1, -1).astype(f32), ((0, 0), (0, GATE_W - 4 * M_HEADS)))
        pl_ = dict(zip(names, in_proj(x, g[0][None], mod_l[0], mod_l[1], w_perm, gb_row, rope_tabs, tm=512)))
        pc_ = dict(zip(names, in_proj(ctx, g[0][None], mod_cb[0], mod_cb[1], w_perm, gb_row, tm=Tc)))

        m_l, m_c = mlstm(pc_, pl_, need_ctx)
        n_l, n_c = neighbourhood_attention(pl_["qn"], pl_["kn"], pl_["vn"], pc_["kn"], pc_["vn"],
                                           na_bias_table(na_rpb[l]), pc_["qn"] if need_ctx else None)
        gains = jnp.pad(g[1:3], ((0, 6), (0, 0)))
        merge_w = (pool_w[l].astype(bf16), pool_scale[l][None], w_branch[l].astype(bf16), w_out[l].astype(bf16),
                   router_w[l].T.astype(bf16))

        def mods8(mods):
            return jnp.pad(jnp.concatenate([mods[2], mods[3], mods[4]], axis=1), ((0, 0), (0, 5), (0, 0)))

        x, _, _ = merge_out(m_l, n_l, pl_["pu"], pl_["bg"], x, mods8(mod_l), gains, *merge_w, tm=512)
        h2 = modulate(rmsnorm(x, g[2]), mod_l[3], mod_l[4])
        x = x + mod_l[5] * rmsnorm(ec_moe(h2, router_w[l], w_gate[l], w_up[l], w_down[l]), g[3])
        if need_ctx:
            ctx, _, _ = merge_out(m_c, n_c, pc_["pu"], pc_["bg"], ctx, mods8(mod_cb), gains, *merge_w, tm=Tc)
            h2c = modulate(rmsnorm(ctx, g[2]), mod_c[3], mod_c[4])
            ctx = ctx + mod_c[5] * rmsnorm(ec_moe(h2c, router_w[l], w_gate[l], w_up[l], w_down[l]), g[3])
    return x
```

```python
import functools
import math

import jax
import jax.numpy as jnp
import numpy as np
from jax import lax
from jax.experimental import pallas as pl
from jax.experimental.pallas import tpu as pltpu

D_MODEL = 1024
DEPTH = 2
GRID_W = 64
BRANCH_W = D_MODEL // 2
N_BRANCH = 3
M_HEADS = 4
M_DV = BRANCH_W // M_HEADS
M_DQK = M_DV // 2
M_CHUNK = 128
M_INIT = -1e30
POOL_GROUPS = 4
POOL_GC = BRANCH_W // POOL_GROUPS
POOL_WINDOWS = (2, 4, 8, 16)
NA_HEADS = 8
NA_DH = BRANCH_W // NA_HEADS
NA_KH = 8
NA_KW = 16
N_EXPERTS = 16
EC_FACTOR = 2
ROPE_BASE = 10000.0
RMS_EPS = 1e-6
N_MOD = 6
SPLIT_SIZES = (M_HEADS * M_DQK, M_HEADS * M_DQK, M_HEADS * M_DV, M_HEADS * M_DV, 4 * M_HEADS,
               POOL_GROUPS * POOL_GC, NA_HEADS * NA_DH, NA_HEADS * NA_DH, NA_HEADS * NA_DH,
               N_BRANCH * D_MODEL)
PROJ_W = sum(SPLIT_SIZES)

LANES = 128
VMEM_LIMIT = 56 * 1024 * 1024
MASK_NEG = -1e30
GATE_W = LANES

IN_GROUPS = (("mqk", 2 * M_HEADS * M_DQK), ("mv", BRANCH_W), ("mo", BRANCH_W), ("pu", BRANCH_W),
             ("qn", BRANCH_W), ("kn", BRANCH_W), ("vn", BRANCH_W), ("bg", N_BRANCH * D_MODEL),
             ("g", GATE_W))
IN_W = sum(w for _, w in IN_GROUPS)
MM_COLS = 512
MOE_ROW = D_MODEL // 2 + LANES


def _const_spec(shape):
    nd = len(shape)
    return pl.BlockSpec(shape, lambda *_: (0,) * nd, pipeline_mode=pl.Buffered(1))


def _rope_tile(x, cos, sin_signed):
    half = M_DQK // 4
    lane = lax.broadcasted_iota(jnp.int32, x.shape, 1)
    partner = jnp.where((lane % (2 * half)) < half, pltpu.roll(x, LANES - half, 1), pltpu.roll(x, half, 1))
    return x * cos + partner * sin_signed


def _in_proj_kernel(x_ref, g_ref, shift_ref, scale_ref, w_ref, gb_ref, *rest, rope):
    if rope:
        cos_ref, sin_ref = rest[:2]
        o_refs = rest[2:]
    else:
        o_refs = rest
    x = x_ref[0]
    y = x * lax.rsqrt(jnp.mean(x * x, axis=-1, keepdims=True) + RMS_EPS)
    h = ((y * g_ref[...]) * (1.0 + scale_ref[0]) + shift_ref[0]).astype(jnp.bfloat16)
    off = 0
    for (name, width), o_ref in zip(IN_GROUPS, o_refs):
        for c0 in range(0, width, MM_COLS):
            cw = min(MM_COLS, width - c0)
            acc = jnp.dot(h, w_ref[:, off + c0:off + c0 + cw], preferred_element_type=jnp.float32)
            if name == "g":
                acc = acc + gb_ref[...]
            if name == "mqk" and rope:
                cos, sin = cos_ref[...], sin_ref[...]
                acc = jnp.concatenate([_rope_tile(acc[:, t:t + LANES], cos, sin) for t in range(0, cw, LANES)], axis=-1)
            o_ref[0, :, c0:c0 + cw] = acc.astype(o_ref.dtype)
        off += width


def in_proj(x, g, shift, scale, w_perm, gate_bias_row, rope_tabs=None, *, tm):
    B, T, D = x.shape
    out_shape = [jax.ShapeDtypeStruct((B, T, w), jnp.float32 if n == "g" else jnp.bfloat16) for n, w in IN_GROUPS]
    out_specs = [pl.BlockSpec((1, tm, w), lambda b, i: (b, i, 0)) for _, w in IN_GROUPS]
    in_specs = [
        pl.BlockSpec((1, tm, D), lambda b, i: (b, i, 0)),
        _const_spec((1, D)),
        pl.BlockSpec((1, 1, D), lambda b, i: (b, 0, 0)),
        pl.BlockSpec((1, 1, D), lambda b, i: (b, 0, 0)),
        _const_spec((D, IN_W)),
        _const_spec((1, GATE_W)),
    ]
    args = [x, g, shift, scale, w_perm, gate_bias_row]
    if rope_tabs is not None:
        in_specs += [pl.BlockSpec((tm, LANES), lambda b, i: (i, 0))] * 2
        args += list(rope_tabs)
    return pl.pallas_call(
        functools.partial(_in_proj_kernel, rope=rope_tabs is not None),
        grid=(B, T // tm),
        in_specs=in_specs,
        out_specs=out_specs,
        out_shape=out_shape,
        compiler_params=pltpu.CompilerParams(
            dimension_semantics=("arbitrary", "arbitrary"), vmem_limit_bytes=VMEM_LIMIT),
        name="in_proj",
    )(*args)


def rope_lane_tables(n_tokens):
    t = jnp.arange(n_tokens)
    row = (t // GRID_W).astype(jnp.float32)
    col = (t % GRID_W).astype(jnp.float32)
    half = M_DQK // 2
    inv = ROPE_BASE ** (-jnp.arange(0, half, 2, dtype=jnp.float32) / half)
    ar = row[:, None] * inv[None, :]
    ac = col[:, None] * inv[None, :]
    cos = jnp.concatenate([jnp.cos(ar), jnp.cos(ar), jnp.cos(ac), jnp.cos(ac)], axis=-1)
    sin = jnp.concatenate([-jnp.sin(ar), jnp.sin(ar), -jnp.sin(ac), jnp.sin(ac)], axis=-1)
    return jnp.tile(cos, (1, LANES // M_DQK)), jnp.tile(sin, (1, LANES // M_DQK))


def permute_w_in(w_in_l):
    qm, km, vm, om, gm, pu, qn, kn, vn, bg = jnp.split(w_in_l, [int(s) for s in np.cumsum(SPLIT_SIZES)[:-1]], axis=-1)
    gpad = jnp.pad(gm, ((0, 0), (0, GATE_W - gm.shape[1])))
    w = jnp.concatenate([qm * (M_DQK ** -0.5), km, vm, om, pu, qn * (NA_DH ** -0.5), kn, vn, bg, gpad], axis=-1)
    return w.astype(jnp.bfloat16)


def _pair_attention(qp, k_parts, v_parts, biases):
    n = qp.shape[0]
    lane_lo = lax.broadcasted_iota(jnp.int32, (n, LANES), 1) < NA_DH
    zero = jnp.zeros_like(qp)
    qq = jnp.concatenate([jnp.where(lane_lo, qp, zero), jnp.where(lane_lo, zero, qp)], axis=0)
    scores = []
    for kp, bias in zip(k_parts, biases):
        s = lax.dot_general(qq, kp, (((1,), (1,)), ((), ())), preferred_element_type=jnp.float32)
        scores.append(s if bias is None else s + bias)
    m = scores[0].max(axis=-1, keepdims=True)
    for s in scores[1:]:
        m = jnp.maximum(m, s.max(axis=-1, keepdims=True))
    l = None
    o = None
    for s, vp in zip(scores, v_parts):
        p = jnp.exp(s - m)
        ls = p.sum(axis=-1, keepdims=True)
        os_ = jnp.dot(p.astype(jnp.bfloat16), vp, preferred_element_type=jnp.float32)
        l = ls if l is None else l + ls
        o = os_ if o is None else o + os_
    o = o * (1.0 / l)
    return jnp.where(lane_lo, o[:n], o[n:])


def _na_kernel(q_ref, k_ref, v_ref, kc_ref, vc_ref, bias_ref, *rest, need_ctx):
    if need_ctx:
        qc_ref, o_ref, oc_ref = rest
    else:
        (o_ref,) = rest
    S = q_ref.shape[1]
    rows = S // GRID_W
    n_pairs = BRANCH_W // LANES

    def row_body(r, carry):
        rs = jnp.clip(r - NA_KH // 2, 0, rows - NA_KH)
        var = r - rs
        q0 = pl.multiple_of(r * GRID_W, GRID_W)
        k0 = pl.multiple_of(rs * GRID_W, GRID_W)
        outs = []
        for j in range(n_pairs):
            ls = slice(j * LANES, (j + 1) * LANES)
            qp = q_ref[0, pl.ds(q0, GRID_W), ls]
            kw = k_ref[0, pl.ds(k0, NA_KH * GRID_W), ls]
            vw = v_ref[0, pl.ds(k0, NA_KH * GRID_W), ls]
            outs.append(_pair_attention(qp, (kw, kc_ref[0, :, ls]), (vw, vc_ref[0, :, ls]),
                                        (bias_ref[var, j], None)))
        o_ref[0, pl.ds(q0, GRID_W), :] = jnp.concatenate(outs, axis=-1).astype(o_ref.dtype)
        return carry

    lax.fori_loop(0, rows, row_body, 0)

    if need_ctx:
        outs = []
        for j in range(n_pairs):
            ls = slice(j * LANES, (j + 1) * LANES)
            outs.append(_pair_attention(qc_ref[0, :, ls], (kc_ref[0, :, ls],), (vc_ref[0, :, ls],),
                                        (None,)))
        oc_ref[0] = jnp.concatenate(outs, axis=-1).astype(oc_ref.dtype)


def na_bias_table(rpb):
    H = rpb.shape[0]
    var = jnp.arange(NA_KH)
    kr = jnp.arange(NA_KH)
    dr = kr[None, :] - var[:, None] + NA_KH - 1
    cols = jnp.arange(GRID_W)
    dc = jnp.clip(cols[None, :] - cols[:, None] + NA_KW - 1, 0, 2 * NA_KW - 2)
    cs = jnp.clip(cols - NA_KW // 2, 0, GRID_W - NA_KW)
    colmask = (cols[None, :] >= cs[:, None]) & (cols[None, :] < cs[:, None] + NA_KW)
    tab = rpb[:, dr][:, :, :, dc]
    tab = jnp.where(colmask[None, None, None], tab.astype(jnp.float32), MASK_NEG)
    tab = tab.transpose(1, 0, 3, 2, 4)
    return tab.reshape(NA_KH, H // 2, 2 * GRID_W, NA_KH * GRID_W)


def neighbourhood_attention(qn, kn, vn, kc, vc, bias_tab, qc=None):
    B, S, W = qn.shape
    Tc = kc.shape[1]
    need_ctx = qc is not None
    lat = pl.BlockSpec((1, S, W), lambda b: (b, 0, 0))
    cx = pl.BlockSpec((1, Tc, W), lambda b: (b, 0, 0))
    in_specs = [lat, lat, lat, cx, cx, _const_spec(bias_tab.shape)]
    args = [qn, kn, vn, kc, vc, bias_tab]
    out_shape = [jax.ShapeDtypeStruct((B, S, W), jnp.bfloat16)]
    out_specs = [lat]
    if need_ctx:
        in_specs.append(cx)
        args.append(qc)
        out_shape.append(jax.ShapeDtypeStruct((B, Tc, W), jnp.bfloat16))
        out_specs.append(cx)
    res = pl.pallas_call(
        functools.partial(_na_kernel, need_ctx=need_ctx),
        grid=(B,),
        in_specs=in_specs,
        out_specs=out_specs,
        out_shape=out_shape,
        compiler_params=pltpu.CompilerParams(dimension_semantics=("arbitrary",), vmem_limit_bytes=VMEM_LIMIT),
        name="neighbourhood_attention",
    )(*args)
    return (res[0], res[1]) if need_ctx else (res[0], None)


def _split_bf16(x):
    hi = x.astype(jnp.bfloat16)
    r1 = x - hi.astype(jnp.float32)
    mid = r1.astype(jnp.bfloat16)
    lo = (r1 - mid.astype(jnp.float32)).astype(jnp.bfloat16)
    return hi, mid, lo


def _dot_f32(a, b):
    return jnp.dot(a, b, preferred_element_type=jnp.float32)


def _log_sigmoid(x):
    return jnp.minimum(x, 0.0) - jnp.log(1.0 + jnp.exp(-jnp.abs(x)))


def _mlstm_chunk(qk_ref, v_ref, g_ref, gt_ref, h_ref, ct_ref, m_ref, ci, fwd):
    L = M_CHUNK
    f32, bf16 = jnp.float32, jnp.bfloat16
    r0 = pl.multiple_of(ci * L, L)
    rows = pl.ds(r0, L)
    sq_r = lax.broadcasted_iota(jnp.int32, (L, L), 0)
    sq_c = lax.broadcasted_iota(jnp.int32, (L, L), 1)
    tri_l = jnp.where(sq_r >= sq_c, 1.0, 0.0).astype(bf16)
    tri_u = jnp.where(sq_r <= sq_c, 1.0, 0.0).astype(bf16)
    gates = g_ref[0, rows, :]
    gates_t = gt_ref[0, ci]
    lf, lf_t = _log_sigmoid(gates), _log_sigmoid(gates_t)
    cum = sum(_dot_f32(tri_l if fwd else tri_u, p) for p in _split_bf16(lf))
    cum_t = sum(_dot_f32(p, tri_u if fwd else tri_l) for p in _split_bf16(lf_t))
    t_i = 0 if fwd else 2
    last = L - 1 if fwd else 0
    st_r = lax.broadcasted_iota(jnp.int32, (2 * L, L), 0) & (L - 1)
    st_c = lax.broadcasted_iota(jnp.int32, (2 * L, L), 1)
    valid = (st_r >= st_c) if fwd else (st_r <= st_c)
    lane_lo = lax.broadcasted_iota(jnp.int32, (L, LANES), 1) < M_DQK
    row_lo = lax.broadcasted_iota(jnp.int32, (LANES, L), 0) < M_DQK
    ones = jnp.ones((L, LANES), bf16)

    def col2(tile, c0, c1):
        return jnp.concatenate([jnp.broadcast_to(tile[:, c0:c0 + 1], (L, LANES)),
                                jnp.broadcast_to(tile[:, c1:c1 + 1], (L, LANES))], axis=0)

    def row2(tile, c0, c1):
        return jnp.concatenate([jnp.broadcast_to(tile[c0:c0 + 1, :], (L, L)),
                                jnp.broadcast_to(tile[c1:c1 + 1, :], (L, L))], axis=0)

    for j in range(M_HEADS // 2):
        h0, h1 = 2 * j, 2 * j + 1
        sidx = 2 * j + (0 if fwd else 1)
        ci0, ci1 = t_i * M_HEADS + h0, t_i * M_HEADS + h1
        cf0, cf1 = ci0 + M_HEADS, ci1 + M_HEADS
        i_col, b_col = col2(gates, ci0, ci1), col2(cum, cf0, cf1)
        dm = jnp.where(valid, b_col - row2(cum_t, cf0, cf1) + row2(gates_t, ci0, ci1), MASK_NEG)
        m_col = m_ref[sidx]
        qp = qk_ref[0, rows, j * LANES:(j + 1) * LANES]
        kp = qk_ref[0, rows, M_HEADS * M_DQK + j * LANES:M_HEADS * M_DQK + (j + 1) * LANES]
        vx0 = jnp.concatenate([v_ref[0, rows, h0 * M_DV:(h0 + 1) * M_DV], ones], axis=-1)
        vx1 = jnp.concatenate([v_ref[0, rows, h1 * M_DV:(h1 + 1) * M_DV], ones], axis=-1)
        ct = ct_ref[sidx]
        if h_ref is not None:
            zq = jnp.zeros_like(qp)
            qq = jnp.concatenate([jnp.where(lane_lo, qp, zq), jnp.where(lane_lo, zq, qp)], axis=0)
            g_col = b_col + m_col
            mt = jnp.maximum(g_col, jnp.broadcast_to(dm.max(axis=-1, keepdims=True), (2 * L, LANES)))
            s = lax.dot_general(qq, kp, (((1,), (1,)), ((), ())), preferred_element_type=f32)
            sc = (s * jnp.exp(dm - mt)).astype(bf16)
            inter = jnp.exp(g_col - mt)
            t1 = _dot_f32(qq, ct.astype(bf16))
            t2 = jnp.concatenate([_dot_f32(sc[:L], vx0), _dot_f32(sc[L:], vx1)], axis=0)
            num = inter * t1[:, :M_DV] + t2[:, :M_DV]
            den = inter * t1[:, M_DV:] + t2[:, M_DV:]
            h = num / jnp.maximum(jnp.abs(den), jnp.exp(-mt))
            h_ref[rows, h0 * M_DV:(h0 + 1) * M_DV] = h[:L]
            h_ref[rows, h1 * M_DV:(h1 + 1) * M_DV] = h[L:]
        bl = jnp.concatenate([jnp.broadcast_to(b_col[last:last + 1], (L, LANES)),
                              jnp.broadcast_to(b_col[L + last:L + last + 1], (L, LANES))], axis=0)
        w = bl - b_col + i_col
        wm = jnp.concatenate([jnp.broadcast_to(w[:L].max(axis=0, keepdims=True), (L, LANES)),
                              jnp.broadcast_to(w[L:].max(axis=0, keepdims=True), (L, LANES))], axis=0)
        m_new = jnp.maximum(bl + m_col, wm)
        decay = jnp.exp(bl + m_col - m_new)
        ws = jnp.exp(w - m_new)
        kw_t = (kp.astype(f32) * jnp.where(lane_lo, ws[:L], ws[L:])).T
        zk = jnp.zeros_like(kw_t)
        upd = (_dot_f32(jnp.where(row_lo, kw_t, zk).astype(bf16), vx0)
               + _dot_f32(jnp.where(row_lo, zk, kw_t).astype(bf16), vx1))
        d_rows = jnp.concatenate([decay[:M_DQK], decay[L:L + M_DQK]], axis=0)
        ct_ref[sidx] = jnp.concatenate([d_rows, d_rows], axis=-1) * ct + upd
        m_ref[sidx] = m_new


def _mlstm_finish(hf_ref, hb_ref, o_ref, out_ref, n_chunks):
    L = M_CHUNK

    def body(ci, carry):
        rows = pl.ds(pl.multiple_of(ci * L, L), L)
        h = hf_ref[rows, :] + hb_ref[rows, :]
        parts = []
        for hd in range(M_HEADS):
            hh = h[:, hd * M_DV:(hd + 1) * M_DV]
            parts.append(hh * lax.rsqrt(jnp.mean(hh * hh, axis=-1, keepdims=True) + RMS_EPS))
        gate = jax.nn.sigmoid(o_ref[0, rows, :].astype(jnp.float32))
        out_ref[0, rows, :] = (gate * jnp.concatenate(parts, axis=-1)).astype(out_ref.dtype)
        return carry

    lax.fori_loop(0, n_chunks, body, 0)


def _mlstm_kernel(qk_c, v_c, o_c, g_c, gt_c, qk_l, v_l, o_l, g_l, gt_l, *rest, need_ctx):
    if need_ctx:
        out_l, out_c, hf_l, hb_l, hf_c, hb_c, ct_ref, m_ref = rest
    else:
        out_l, hf_l, hb_l, ct_ref, m_ref = rest
        hf_c = hb_c = None
    nc, nl = qk_c.shape[1] // M_CHUNK, qk_l.shape[1] // M_CHUNK
    ct_ref[...] = jnp.zeros_like(ct_ref)
    m_ref[...] = jnp.full_like(m_ref, M_INIT)

    def phase(qk, v, g, gt, hf, hb, n):
        def body(i, carry):
            _mlstm_chunk(qk, v, g, gt, hf, ct_ref, m_ref, i, True)
            _mlstm_chunk(qk, v, g, gt, hb, ct_ref, m_ref, n - 1 - i, False)
            return carry
        lax.fori_loop(0, n, body, 0)

    phase(qk_c, v_c, g_c, gt_c, hf_c, hb_c, nc)
    phase(qk_l, v_l, g_l, gt_l, hf_l, hb_l, nl)
    _mlstm_finish(hf_l, hb_l, o_l, out_l, nl)
    if need_ctx:
        _mlstm_finish(hf_c, hb_c, o_c, out_c, nc)


def _gates_chunk_major(g):
    B, T, _ = g.shape
    return g[..., :4 * M_HEADS].reshape(B, T // M_CHUNK, M_CHUNK, 4 * M_HEADS).transpose(0, 1, 3, 2)


def mlstm(pc, pl_, need_ctx):
    B, S, W = pl_["mv"].shape
    Tc = pc["mv"].shape[1]
    f32 = jnp.float32

    def specs(T):
        s = pl.BlockSpec((1, T, W), lambda b: (b, 0, 0))
        return [s, s, s, pl.BlockSpec((1, T, GATE_W), lambda b: (b, 0, 0)),
                pl.BlockSpec((1, T // M_CHUNK, 4 * M_HEADS, M_CHUNK), lambda b: (b, 0, 0, 0))]

    def args(p):
        return [p["mqk"], p["mv"], p["mo"], p["g"], _gates_chunk_major(p["g"])]

    out_shape = [jax.ShapeDtypeStruct((B, S, W), jnp.bfloat16)]
    out_specs = [pl.BlockSpec((1, S, W), lambda b: (b, 0, 0))]
    scratch = [pltpu.VMEM((S, W), f32), pltpu.VMEM((S, W), f32)]
    if need_ctx:
        out_shape.append(jax.ShapeDtypeStruct((B, Tc, W), jnp.bfloat16))
        out_specs.append(pl.BlockSpec((1, Tc, W), lambda b: (b, 0, 0)))
        scratch += [pltpu.VMEM((Tc, W), f32), pltpu.VMEM((Tc, W), f32)]
    scratch += [pltpu.VMEM((M_HEADS, LANES, 2 * M_DV), f32), pltpu.VMEM((M_HEADS, 2 * M_CHUNK, LANES), f32)]
    res = pl.pallas_call(
        functools.partial(_mlstm_kernel, need_ctx=need_ctx),
        grid=(B,),
        in_specs=specs(Tc) + specs(S),
        out_specs=out_specs,
        out_shape=out_shape,
        scratch_shapes=scratch,
        compiler_params=pltpu.CompilerParams(dimension_semantics=("arbitrary",), vmem_limit_bytes=VMEM_LIMIT),
        name="mlstm",
    )(*args(pc), *args(pl_))
    return (res[0], res[1]) if need_ctx else (res[0], None)


POOL_HALO = 16


def _pool_mix(pu_ref, prev_ref, next_ref, pw_ref, ps_ref, tile_idx, n_tiles, seq_len):
    f32 = jnp.float32
    tm = pu_ref.shape[1]
    cur = pu_ref[0].astype(f32)
    prev = jnp.where(tile_idx > 0, prev_ref[0].astype(f32), 0.0)
    nxt = jnp.where(tile_idx < n_tiles - 1, next_ref[0].astype(f32), 0.0)
    ext = jnp.concatenate([prev, cur, nxt], axis=0)
    n_ext = tm + 2 * POOL_HALO
    tok = tile_idx * tm + lax.broadcasted_iota(jnp.int32, (tm, POOL_GC), 0)

    def shifted(a, d):
        return pltpu.roll(a, (-d) % n_ext, 0)

    outs = []
    for gi, w in enumerate(POOL_WINDOWS):
        s = ext[:, gi * POOL_GC:(gi + 1) * POOL_GC]
        s = shifted(s, -1) + s
        span = 2
        while span < w:
            s = shifted(s, -(span // 2)) + shifted(s, span // 2)
            span *= 2
        lo = jnp.clip(tok - w // 2, 0, seq_len - 1)
        hi = jnp.clip(tok - w // 2 + w - 1, 0, seq_len - 1)
        mean = s[POOL_HALO:POOL_HALO + tm] / (hi - lo + 1).astype(f32)
        d = (mean - cur[:, gi * POOL_GC:(gi + 1) * POOL_GC]).astype(jnp.bfloat16)
        outs.append(_dot_f32(d, pw_ref[gi]))
    return jnp.concatenate(outs, axis=-1) * ps_ref[...]


def _pack_bf16_pairs(h):
    n = h.shape[1] // 2
    hb = h.astype(jnp.bfloat16).astype(jnp.float32)
    lo = pltpu.bitcast(hb[:, :n], jnp.uint32) >> 16
    hi = pltpu.bitcast(hb[:, n:], jnp.uint32) & jnp.uint32(0xFFFF0000)
    return lo | hi


def _merge_kernel(m_ref, n_ref, pu_ref, prev_ref, next_ref, bg_ref, x_ref, mods_ref, gains_ref,
                  pw_ref, ps_ref, wb_ref, wo_ref, wrt_ref, wrp_ref, x1_ref, h2a_ref, aff_ref, *, seq_len):
    f32, bf16 = jnp.float32, jnp.bfloat16
    i = pl.program_id(1)
    D = x_ref.shape[2]
    E = wrt_ref.shape[0]
    p = _pool_mix(pu_ref, prev_ref, next_ref, pw_ref, ps_ref, i, pl.num_programs(1), seq_len)
    branches = (m_ref[0], p.astype(bf16), n_ref[0])
    acc = None
    for bi, br in enumerate(branches):
        gate = jax.nn.sigmoid(bg_ref[0, :, bi * D:(bi + 1) * D].astype(f32))
        term = gate * _dot_f32(br, wb_ref[bi])
        acc = term if acc is None else acc + term
    y = _dot_f32(acc.astype(bf16), wo_ref[...])
    yn = y * lax.rsqrt(jnp.mean(y * y, axis=-1, keepdims=True) + RMS_EPS) * gains_ref[0:1, :]
    x1 = x_ref[0] + mods_ref[0, 0:1, :] * yn
    x1_ref[0] = x1
    xn = x1 * lax.rsqrt(jnp.mean(x1 * x1, axis=-1, keepdims=True) + RMS_EPS) * gains_ref[1:2, :]
    h2 = xn * (1.0 + mods_ref[0, 2:3, :]) + mods_ref[0, 1:2, :]
    h2b = h2.astype(bf16)
    lg_t = lax.dot_general(wrt_ref[...], h2b, (((1,), (1,)), ((), ())), preferred_element_type=f32)
    e_t = jnp.exp(lg_t - lg_t.max(axis=0, keepdims=True))
    aff_ref[0] = e_t / e_t.sum(axis=0, keepdims=True)
    lg = _dot_f32(h2b, wrp_ref[...])
    lg = jnp.where(lax.broadcasted_iota(jnp.int32, lg.shape, 1) < E, lg, MASK_NEG)
    e_r = jnp.exp(lg - lg.max(axis=-1, keepdims=True))
    aff_rows = e_r / e_r.sum(axis=-1, keepdims=True)
    h2a_ref[0] = jnp.concatenate([_pack_bf16_pairs(h2), pltpu.bitcast(aff_rows, jnp.uint32)], axis=-1)


def merge_out(m, n, pu, bg, x, mods, gains, pool_w, pool_scale, w_branch, w_out, w_router_t, w_router_pad, *, tm):
    B, T, D = x.shape
    W = m.shape[2]
    E = w_router_t.shape[0]
    hb = tm // POOL_HALO
    n_halo = T // POOL_HALO
    tile = lambda w: pl.BlockSpec((1, tm, w), lambda b, i: (b, i, 0))
    in_specs = [
        tile(W), tile(W), tile(W),
        pl.BlockSpec((1, POOL_HALO, W), lambda b, i: (b, jnp.maximum(i * hb - 1, 0), 0)),
        pl.BlockSpec((1, POOL_HALO, W), lambda b, i: (b, jnp.minimum((i + 1) * hb, n_halo - 1), 0)),
        tile(N_BRANCH * D), tile(D),
        pl.BlockSpec((1, 8, D), lambda b, i: (b, 0, 0)),
        _const_spec((8, D)),
        _const_spec(pool_w.shape), _const_spec(pool_scale.shape), _const_spec(w_branch.shape),
        _const_spec(w_out.shape), _const_spec(w_router_t.shape), _const_spec(w_router_pad.shape),
    ]
    return pl.pallas_call(
        functools.partial(_merge_kernel, seq_len=T),
        grid=(B, T // tm),
        in_specs=in_specs,
        out_specs=[tile(D), tile(MOE_ROW), pl.BlockSpec((1, E, tm), lambda b, i: (b, 0, i))],
        out_shape=[jax.ShapeDtypeStruct((B, T, D), jnp.float32),
                   jax.ShapeDtypeStruct((B, T, MOE_ROW), jnp.uint32),
                   jax.ShapeDtypeStruct((B, E, T), jnp.float32)],
        compiler_params=pltpu.CompilerParams(
            dimension_semantics=("arbitrary", "arbitrary"), vmem_limit_bytes=VMEM_LIMIT),
        name="merge_out",
    )(m, n, pu, pu, pu, bg, x, mods, gains, pool_w, pool_scale, w_branch, w_out, w_router_t, w_router_pad)


def _cumsum_lanes(x01):
    R, T = x01.shape
    r = lax.broadcasted_iota(jnp.int32, (LANES, LANES), 0)
    c = lax.broadcasted_iota(jnp.int32, (LANES, LANES), 1)
    tri_u = jnp.where(r <= c, 1.0, 0.0).astype(jnp.bfloat16)
    run = jnp.zeros((R, 1), jnp.float32)
    outs = []
    for t0 in range(0, T, LANES):
        cs = _dot_f32(x01[:, t0:t0 + LANES].astype(jnp.bfloat16), tri_u) + run
        run = cs[:, LANES - 1:LANES]
        outs.append(cs)
    return jnp.concatenate(outs, axis=-1)


def _route_kernel(aff_ref, idx_ref, *, cap):
    f32 = jnp.float32
    aff = aff_ref[0]
    E, T = aff.shape
    bits = pltpu.bitcast(aff, jnp.int32)

    def search(k, lo):
        cand = lo | (jnp.int32(1) << (30 - k))
        cnt = jnp.sum(jnp.where(bits >= cand, 1.0, 0.0), axis=-1, keepdims=True)
        return jnp.where(cnt >= cap, cand, lo)

    thr = lax.fori_loop(0, 31, search, jnp.zeros((E, 1), jnp.int32))
    gt = jnp.where(bits > thr, 1.0, 0.0)
    eq = jnp.where(bits == thr, 1.0, 0.0)
    room = cap - jnp.sum(gt, axis=-1, keepdims=True)
    sel = gt + eq * jnp.where(_cumsum_lanes(eq) <= room, 1.0, 0.0)
    csel = _cumsum_lanes(sel)
    slot = lax.broadcasted_iota(jnp.int32, (cap, T), 0).astype(f32)
    lane = lax.broadcasted_iota(jnp.int32, (cap, LANES), 1)
    out = jnp.zeros((cap, LANES), f32)
    for e in range(E):
        pos = jnp.sum(jnp.where(jnp.broadcast_to(csel[e:e + 1, :], (cap, T)) <= slot, 1.0, 0.0),
                      axis=-1, keepdims=True)
        out = jnp.where(lane == e, jnp.broadcast_to(pos, (cap, LANES)), out)
    idx_ref[0] = out.astype(jnp.int32)


def route(aff_t, cap):
    B, E, T = aff_t.shape
    idx_t = pl.pallas_call(
        functools.partial(_route_kernel, cap=cap),
        grid=(B,),
        in_specs=[pl.BlockSpec((1, E, T), lambda b: (b, 0, 0))],
        out_specs=pl.BlockSpec((1, cap, LANES), lambda b: (b, 0, 0)),
        out_shape=jax.ShapeDtypeStruct((B, cap, LANES), jnp.int32),
        compiler_params=pltpu.CompilerParams(dimension_semantics=("arbitrary",), vmem_limit_bytes=VMEM_LIMIT),
        name="route",
    )(aff_t)
    return idx_t[:, :, :E].transpose(0, 2, 1)


ROW_GROUP = 8


def _moe_kernel(idx_ref, rows_ref, wg_ref, wu_ref, wd_ref, out_ref, xg_ref, ye_ref):
    f32, bf16 = jnp.float32, jnp.bfloat16
    e = pl.program_id(1)
    cap = xg_ref.shape[0]
    half = wg_ref.shape[1] // 2

    @pl.when(e == 0)
    def _():
        out_ref[...] = jnp.zeros_like(out_ref)

    def gather(gi, carry):
        j0 = gi * ROW_GROUP
        for r in range(ROW_GROUP):
            xg_ref[pl.ds(j0 + r, 1), :] = rows_ref[0, pl.ds(idx_ref[0, 0, 0, j0 + r], 1), :]
        return carry

    lax.fori_loop(0, cap // ROW_GROUP, gather, 0)
    xg = xg_ref[...]
    packed = xg[:, :half]
    xe = jnp.concatenate([pltpu.bitcast(packed << 16, f32), pltpu.bitcast(packed & jnp.uint32(0xFFFF0000), f32)],
                         axis=-1).astype(bf16)
    aff = pltpu.bitcast(xg[:, half:], f32)
    gate = jnp.sum(jnp.where(lax.broadcasted_iota(jnp.int32, aff.shape, 1) == e, aff, 0.0), axis=-1, keepdims=True)
    a = _dot_f32(xe, wg_ref[0])
    hid = (a * jax.nn.sigmoid(a) * _dot_f32(xe, wu_ref[0])).astype(bf16)
    ye_ref[...] = _dot_f32(hid, wd_ref[0]) * gate

    def scatter(gi, carry):
        j0 = gi * ROW_GROUP
        ids = [idx_ref[0, 0, 0, j0 + r] for r in range(ROW_GROUP)]
        cur = [out_ref[0, pl.ds(i, 1), :] for i in ids]
        for r, i in enumerate(ids):
            out_ref[0, pl.ds(i, 1), :] = cur[r] + ye_ref[pl.ds(j0 + r, 1), :]
        return carry

    lax.fori_loop(0, cap // ROW_GROUP, scatter, 0)


def moe_experts(rows, idx, w_gate, w_up, w_down):
    G, T, RW = rows.shape
    _, E, cap = idx.shape
    D = w_gate.shape[1]
    wspec = pl.BlockSpec((1, D, D), lambda g, e: (e, 0, 0))
    return pl.pallas_call(
        _moe_kernel,
        grid=(G, E),
        in_specs=[pl.BlockSpec((1, 1, 1, cap), lambda g, e: (g, e, 0, 0), memory_space=pltpu.SMEM),
                  pl.BlockSpec((1, T, RW), lambda g, e: (g, 0, 0)), wspec, wspec, wspec],
        out_specs=pl.BlockSpec((1, T, D), lambda g, e: (g, 0, 0)),
        out_shape=jax.ShapeDtypeStruct((G, T, D), jnp.float32),
        scratch_shapes=[pltpu.VMEM((cap, RW), jnp.uint32), pltpu.VMEM((cap, D), jnp.float32)],
        compiler_params=pltpu.CompilerParams(
            dimension_semantics=("arbitrary", "arbitrary"), vmem_limit_bytes=VMEM_LIMIT),
        name="moe_experts",
    )(idx.reshape(G, E, 1, cap), rows, w_gate, w_up, w_down)


def _residual_norm_kernel(x_ref, y_ref, gate_ref, gain_ref, o_ref):
    y = y_ref[0]
    yn = y * lax.rsqrt(jnp.mean(y * y, axis=-1, keepdims=True) + RMS_EPS) * gain_ref[...]
    o_ref[0] = x_ref[0] + gate_ref[0] * yn


def residual_norm(x, y, gate, gain, *, tm):
    B, T, D = x.shape
    tile = pl.BlockSpec((1, tm, D), lambda b, i: (b, i, 0))
    return pl.pallas_call(
        _residual_norm_kernel,
        grid=(B, T // tm),
        in_specs=[tile, tile, pl.BlockSpec((1, 1, D), lambda b, i: (b, 0, 0)), _const_spec((1, D))],
        out_specs=tile,
        out_shape=jax.ShapeDtypeStruct((B, T, D), jnp.float32),
        compiler_params=pltpu.CompilerParams(
            dimension_semantics=("arbitrary", "arbitrary"), vmem_limit_bytes=VMEM_LIMIT),
        name="residual_norm",
    )(x, y, gate, gain)


def rmsnorm(x, g):
    xf = x.astype(jnp.float32)
    y = xf * lax.rsqrt(jnp.mean(xf * xf, axis=-1, keepdims=True) + RMS_EPS)
    return (y * g.astype(jnp.float32)).astype(x.dtype)


def modulate(h, shift, scale):
    return h * (1 + scale) + shift


def rope_tables(n_tokens, dtype):
    t = jnp.arange(n_tokens)
    row = (t // GRID_W).astype(jnp.float32)
    col = (t % GRID_W).astype(jnp.float32)
    half = M_DQK // 2
    inv = ROPE_BASE ** (-jnp.arange(0, half, 2, dtype=jnp.float32) / half)
    ar = row[:, None] * inv[None, :]
    ac = col[:, None] * inv[None, :]
    return tuple(a.astype(dtype) for a in (jnp.cos(ar), jnp.sin(ar), jnp.cos(ac), jnp.sin(ac)))


def _rotate(x, cos, sin):
    x1, x2 = jnp.split(x, 2, axis=-1)
    cos = cos[:, None, :]
    sin = sin[:, None, :]
    return jnp.concatenate([x1 * cos - x2 * sin, x2 * cos + x1 * sin], axis=-1)


def rope_2d(x, tabs):
    cr, sr, cc, sc = tabs
    xr, xc = jnp.split(x, 2, axis=-1)
    return jnp.concatenate([_rotate(xr, cr, sr), _rotate(xc, cc, sc)], axis=-1)


def mlstm_scan(q, k, v, li, lf, state):
    B, T, H, _ = q.shape
    dv = v.shape[-1]
    L = M_CHUNK
    N = T // L

    def chunks(a):
        a = a.reshape((B, N, L, H) + a.shape[3:])
        return jnp.moveaxis(a, (1, 3), (0, 2))

    tril = jnp.tril(jnp.ones((L, L), dtype=bool))

    def step(carry, inp):
        C, n, m = carry
        qc, kc, vc, ic, fc = inp
        b = jnp.cumsum(fc, axis=-1)
        dmat = jnp.where(tril, b[..., :, None] - b[..., None, :] + ic[..., None, :], -jnp.inf)
        g = b + m[..., None]
        mt = jnp.maximum(g, dmat.max(-1))
        sc = jnp.einsum('bhtd,bhsd->bhts', qc, kc) * jnp.exp(dmat - mt[..., None])
        inter = jnp.exp(g - mt)
        num = inter[..., None] * jnp.einsum('bhvd,bhtd->bhtv', C, qc) + jnp.einsum('bhts,bhsv->bhtv', sc, vc)
        den = inter * jnp.einsum('bhd,bhtd->bht', n, qc) + sc.sum(-1)
        h = num / jnp.maximum(jnp.abs(den), jnp.exp(-mt))[..., None]
        bl = b[..., -1]
        w = bl[..., None] - b + ic
        m_new = jnp.maximum(bl + m, w.max(-1))
        decay = jnp.exp(bl + m - m_new)
        ws = jnp.exp(w - m_new[..., None])
        C_new = decay[..., None, None] * C + jnp.einsum('bhs,bhsv,bhsd->bhvd', ws, vc, kc)
        n_new = decay[..., None] * n + jnp.einsum('bhs,bhsd->bhd', ws, kc)
        return (C_new, n_new, m_new), h

    xs = (chunks(q), chunks(k), chunks(v), chunks(li), chunks(lf))
    state, h = lax.scan(step, state, xs)
    h = jnp.moveaxis(h, (0, 2), (1, 3)).reshape(B, T, H, dv)
    return state, h


def head_rms(h):
    return h * lax.rsqrt(jnp.mean(h * h, axis=-1, keepdims=True) + RMS_EPS)


def mlstm_branch(ctx_parts, lat_parts, need_ctx):
    f32 = jnp.float32

    def prep(parts):
        q, k, v, o, gt = parts
        q = q.astype(f32)
        k = k.astype(f32)
        v = v.astype(f32)
        fwd = (q, k, v, gt[..., 0, :], jax.nn.log_sigmoid(gt[..., 1, :]))
        bwd = tuple(jnp.flip(a, axis=1) for a in (q, k, v, gt[..., 2, :], jax.nn.log_sigmoid(gt[..., 3, :])))
        return fwd, bwd

    c_f, c_b = prep(ctx_parts)
    l_f, l_b = prep(lat_parts)
    B = l_f[0].shape[0]
    st0 = (jnp.zeros((B, M_HEADS, M_DV, M_DQK), f32), jnp.zeros((B, M_HEADS, M_DQK), f32),
           jnp.full((B, M_HEADS), M_INIT, f32))
    st_f, hc_f = mlstm_scan(*c_f, st0)
    _, hl_f = mlstm_scan(*l_f, st_f)
    st_b, hc_b = mlstm_scan(*c_b, st0)
    _, hl_b = mlstm_scan(*l_b, st_b)

    def finish(h_f, h_b_rev, o):
        h = head_rms(h_f + jnp.flip(h_b_rev, axis=1))
        return jax.nn.sigmoid(o) * h.reshape(o.shape).astype(o.dtype)

    out_l = finish(hl_f, hl_b, lat_parts[3])
    out_c = finish(hc_f, hc_b, ctx_parts[3]) if need_ctx else None
    return out_l, out_c


def pool_branch(u, pool_w, pool_scale):
    B, T, _ = u.shape
    uf = u.astype(jnp.float32).reshape(B, T, POOL_GROUPS, POOL_GC)
    csum = jnp.concatenate([jnp.zeros((B, 1, POOL_GROUPS, POOL_GC), jnp.float32),
                            jnp.cumsum(uf, axis=1)], axis=1)
    t = jnp.arange(T)
    outs = []
    for gi, w in enumerate(POOL_WINDOWS):
        lo = jnp.clip(t - w // 2, 0, T - 1)
        hi = jnp.clip(t - w // 2 + w - 1, 0, T - 1)
        cnt = (hi - lo + 1).astype(jnp.float32)[:, None]
        cg = csum[:, :, gi]
        mean = (cg[:, hi + 1] - cg[:, lo]) / cnt
        outs.append(jnp.einsum('btc,cd->btd', (mean - uf[:, :, gi]).astype(u.dtype), pool_w[gi]))
    return jnp.concatenate(outs, axis=-1) * pool_scale


def merge_branches(branches, bg, w_branch, w_out):
    D = w_out.shape[0]
    acc = None
    for bi, br in enumerate(branches):
        term = jax.nn.sigmoid(bg[..., bi * D:(bi + 1) * D]) * (br @ w_branch[bi])
        acc = term if acc is None else acc + term
    return acc @ w_out


def ec_moe(h, w_router, w_gate, w_up, w_down):
    B, T, D = h.shape
    cap = EC_FACTOR * T // N_EXPERTS
    aff = jax.nn.softmax(jnp.einsum('btd,de->bte', h, w_router, preferred_element_type=jnp.float32), axis=-1)
    gate, idx = lax.top_k(jnp.swapaxes(aff, 1, 2), cap)
    xe = jax.vmap(lambda hb, ib: hb[ib])(h, idx)
    hid = jax.nn.silu(jnp.einsum('becd,edf->becf', xe, w_gate)) * jnp.einsum('becd,edf->becf', xe, w_up)
    ye = jnp.einsum('becf,efd->becd', hid, w_down) * gate[..., None].astype(h.dtype)
    return jax.vmap(lambda ib, yb: jnp.zeros((T, D), yb.dtype).at[ib.reshape(-1)].add(yb.reshape(-1, D)))(idx, ye)


def _mparts(p, rope_tabs=None):
    B, T, _ = p["mqk"].shape
    f32 = jnp.float32
    hq = M_HEADS * M_DQK
    q = p["mqk"][..., :hq].astype(f32).reshape(B, T, M_HEADS, M_DQK)
    k = p["mqk"][..., hq:].astype(f32).reshape(B, T, M_HEADS, M_DQK)
    if rope_tabs is not None:
        q = rope_2d(q, rope_tabs)
        k = rope_2d(k, rope_tabs)
    v = p["mv"].astype(f32).reshape(B, T, M_HEADS, M_DV)
    return q, k, v, p["mo"].astype(f32), p["g"][..., :4 * M_HEADS].reshape(B, T, 4, M_HEADS)


def kernel(x, c, ctx, c_ctx, norm_gain, ada_w, ada_b, w_in, mlstm_gate_bias, pool_w, pool_scale,
           na_rpb, w_branch, w_out, router_w, w_gate, w_up, w_down):
    B, S, D = x.shape
    Tc = ctx.shape[1]
    f32, bf16 = jnp.float32, jnp.bfloat16
    rope_tabs = rope_lane_tables(S)
    names = [n for n, _ in IN_GROUPS]
    for l in range(DEPTH):
        need_ctx = l < DEPTH - 1
        g = norm_gain[l]
        mod_l = jnp.split((jax.nn.silu(c) @ ada_w[l] + ada_b[l])[:, None, :], N_MOD, axis=-1)
        mod_c = jnp.split(jax.nn.silu(c_ctx) @ ada_w[l] + ada_b[l], N_MOD, axis=-1)
        mod_cb = [jnp.broadcast_to(m[None, None], (B, 1, D)) for m in mod_c]
        w_perm = permute_w_in(w_in[l])
        gb_row = jnp.pad(mlstm_gate_bias[l].reshape(1, -1).astype(f32), ((0, 0), (0, GATE_W - 4 * M_HEADS)))
        pl_ = dict(zip(names, in_proj(x, g[0][None], mod_l[0], mod_l[1], w_perm, gb_row, rope_tabs, tm=512)))
        pc_ = dict(zip(names, in_proj(ctx, g[0][None], mod_cb[0], mod_cb[1], w_perm, gb_row, tm=Tc)))

        m_l, m_c = mlstm(pc_, pl_, need_ctx)
        n_l, n_c = neighbourhood_attention(pl_["qn"], pl_["kn"], pl_["vn"], pc_["kn"], pc_["vn"],
                                           na_bias_table(na_rpb[l]), pc_["qn"] if need_ctx else None)
        gains = jnp.pad(g[1:3], ((0, 6), (0, 0)))
        merge_w = (pool_w[l].astype(bf16), pool_scale[l][None], w_branch[l].astype(bf16), w_out[l].astype(bf16),
                   router_w[l].T.astype(bf16),
                   jnp.pad(router_w[l], ((0, 0), (0, LANES - N_EXPERTS))).astype(bf16))
        moe_w = (w_gate[l].astype(bf16), w_up[l].astype(bf16), w_down[l].astype(bf16))

        def mods8(mods):
            return jnp.pad(jnp.concatenate([mods[2], mods[3], mods[4]], axis=1), ((0, 0), (0, 5), (0, 0)))

        x1, rows, aff_t = merge_out(m_l, n_l, pl_["pu"], pl_["bg"], x, mods8(mod_l), gains, *merge_w, tm=512)
        y = moe_experts(rows, route(aff_t, EC_FACTOR * S // N_EXPERTS), *moe_w)
        x = residual_norm(x1, y, mod_l[5], g[3][None], tm=512)
        if need_ctx:
            c1, rows, aff_t = merge_out(m_c, n_c, pc_["pu"], pc_["bg"], ctx, mods8(mod_cb), gains, *merge_w, tm=Tc)
            per = S // Tc
            cap_c = EC_FACTOR * Tc // N_EXPERTS
            idx = route(aff_t, cap_c) + (jnp.arange(B, dtype=jnp.int32) % per * Tc)[:, None, None]
            idx = idx.reshape(B // per, per, N_EXPERTS, cap_c).transpose(0, 2, 1, 3).reshape(B // per, N_EXPERTS, per * cap_c)
            y = moe_experts(rows.reshape(B // per, S, MOE_ROW), idx, *moe_w).reshape(B, Tc, D)
            ctx = residual_norm(c1, y, mod_cb[5], g[3][None], tm=Tc)
    return x
```

```python
import functools
import math

import jax
import jax.numpy as jnp
import numpy as np
from jax import lax
from jax.experimental import pallas as pl
from jax.experimental.pallas import tpu as pltpu

D_MODEL = 1024
DEPTH = 2
GRID_W = 64
BRANCH_W = D_MODEL // 2
N_BRANCH = 3
M_HEADS = 4
M_DV = BRANCH_W // M_HEADS
M_DQK = M_DV // 2
M_CHUNK = 128
M_INIT = -1e30
POOL_GROUPS = 4
POOL_GC = BRANCH_W // POOL_GROUPS
POOL_WINDOWS = (2, 4, 8, 16)
NA_HEADS = 8
NA_DH = BRANCH_W // NA_HEADS
NA_KH = 8
NA_KW = 16
N_EXPERTS = 16
EC_FACTOR = 2
ROPE_BASE = 10000.0
RMS_EPS = 1e-6
N_MOD = 6
SPLIT_SIZES = (M_HEADS * M_DQK, M_HEADS * M_DQK, M_HEADS * M_DV, M_HEADS * M_DV, 4 * M_HEADS,
               POOL_GROUPS * POOL_GC, NA_HEADS * NA_DH, NA_HEADS * NA_DH, NA_HEADS * NA_DH,
               N_BRANCH * D_MODEL)
PROJ_W = sum(SPLIT_SIZES)

LANES = 128
VMEM_LIMIT = 56 * 1024 * 1024
MASK_NEG = -1e30
GATE_W = LANES

IN_GROUPS = (("mqk", 2 * M_HEADS * M_DQK), ("mv", BRANCH_W), ("mo", BRANCH_W), ("pu", BRANCH_W),
             ("qn", BRANCH_W), ("kn", BRANCH_W), ("vn", BRANCH_W), ("bg", N_BRANCH * D_MODEL),
             ("g", GATE_W))
IN_W = sum(w for _, w in IN_GROUPS)
MM_COLS = 512
MOE_ROW = D_MODEL // 2 + LANES


def _const_spec(shape):
    nd = len(shape)
    return pl.BlockSpec(shape, lambda *_: (0,) * nd, pipeline_mode=pl.Buffered(1))


def _rope_tile(x, cos, sin_signed):
    half = M_DQK // 4
    lane = lax.broadcasted_iota(jnp.int32, x.shape, 1)
    partner = jnp.where((lane % (2 * half)) < half, pltpu.roll(x, LANES - half, 1), pltpu.roll(x, half, 1))
    return x * cos + partner * sin_signed


def _in_proj_kernel(x_ref, g_ref, shift_ref, scale_ref, w_ref, gb_ref, *rest, rope):
    if rope:
        cos_ref, sin_ref = rest[:2]
        o_refs = rest[2:]
    else:
        o_refs = rest
    x = x_ref[0]
    y = x * lax.rsqrt(jnp.mean(x * x, axis=-1, keepdims=True) + RMS_EPS)
    h = ((y * g_ref[...]) * (1.0 + scale_ref[0]) + shift_ref[0]).astype(jnp.bfloat16)
    off = 0
    for (name, width), o_ref in zip(IN_GROUPS, o_refs):
        for c0 in range(0, width, MM_COLS):
            cw = min(MM_COLS, width - c0)
            acc = jnp.dot(h, w_ref[:, off + c0:off + c0 + cw], preferred_element_type=jnp.float32)
            if name == "g":
                acc = acc + gb_ref[...]
            if name == "mqk" and rope:
                cos, sin = cos_ref[...], sin_ref[...]
                acc = jnp.concatenate([_rope_tile(acc[:, t:t + LANES], cos, sin) for t in range(0, cw, LANES)], axis=-1)
            o_ref[0, :, c0:c0 + cw] = acc.astype(o_ref.dtype)
        off += width


def in_proj(x, g, shift, scale, w_perm, gate_bias_row, rope_tabs=None, *, tm):
    B, T, D = x.shape
    out_shape = [jax.ShapeDtypeStruct((B, T, w), jnp.float32 if n == "g" else jnp.bfloat16) for n, w in IN_GROUPS]
    out_specs = [pl.BlockSpec((1, tm, w), lambda b, i: (b, i, 0)) for _, w in IN_GROUPS]
    in_specs = [
        pl.BlockSpec((1, tm, D), lambda b, i: (b, i, 0)),
        _const_spec((1, D)),
        pl.BlockSpec((1, 1, D), lambda b, i: (b, 0, 0)),
        pl.BlockSpec((1, 1, D), lambda b, i: (b, 0, 0)),
        _const_spec((D, IN_W)),
        _const_spec((1, GATE_W)),
    ]
    args = [x, g, shift, scale, w_perm, gate_bias_row]
    if rope_tabs is not None:
        in_specs += [pl.BlockSpec((tm, LANES), lambda b, i: (i, 0))] * 2
        args += list(rope_tabs)
    return pl.pallas_call(
        functools.partial(_in_proj_kernel, rope=rope_tabs is not None),
        grid=(B, T // tm),
        in_specs=in_specs,
        out_specs=out_specs,
        out_shape=out_shape,
        compiler_params=pltpu.CompilerParams(
            dimension_semantics=("arbitrary", "arbitrary"), vmem_limit_bytes=VMEM_LIMIT),
        name="in_proj",
    )(*args)


def rope_lane_tables(n_tokens):
    t = jnp.arange(n_tokens)
    row = (t // GRID_W).astype(jnp.float32)
    col = (t % GRID_W).astype(jnp.float32)
    half = M_DQK // 2
    inv = ROPE_BASE ** (-jnp.arange(0, half, 2, dtype=jnp.float32) / half)
    ar = row[:, None] * inv[None, :]
    ac = col[:, None] * inv[None, :]
    cos = jnp.concatenate([jnp.cos(ar), jnp.cos(ar), jnp.cos(ac), jnp.cos(ac)], axis=-1)
    sin = jnp.concatenate([-jnp.sin(ar), jnp.sin(ar), -jnp.sin(ac), jnp.sin(ac)], axis=-1)
    return jnp.tile(cos, (1, LANES // M_DQK)), jnp.tile(sin, (1, LANES // M_DQK))


def permute_w_in(w_in_l):
    qm, km, vm, om, gm, pu, qn, kn, vn, bg = jnp.split(w_in_l, [int(s) for s in np.cumsum(SPLIT_SIZES)[:-1]], axis=-1)
    gpad = jnp.pad(gm, ((0, 0), (0, GATE_W - gm.shape[1])))
    w = jnp.concatenate([qm * (M_DQK ** -0.5), km, vm, om, pu, qn * (NA_DH ** -0.5), kn, vn, bg, gpad], axis=-1)
    return w.astype(jnp.bfloat16)


def _pair_scores(qp, k_parts, biases):
    n = qp.shape[0]
    lane_lo = lax.broadcasted_iota(jnp.int32, (n, LANES), 1) < NA_DH
    zero = jnp.zeros_like(qp)
    qq = jnp.concatenate([jnp.where(lane_lo, qp, zero), jnp.where(lane_lo, zero, qp)], axis=0)
    scores = []
    for kp, bias in zip(k_parts, biases):
        s = lax.dot_general(qq, kp, (((1,), (1,)), ((), ())), preferred_element_type=jnp.float32)
        scores.append(s if bias is None else s + bias)
    return scores


def _pair_softmax_pv(scores, v_parts):
    n = scores[0].shape[0] // 2
    lane_lo = lax.broadcasted_iota(jnp.int32, (n, LANES), 1) < NA_DH
    m = scores[0].max(axis=-1, keepdims=True)
    for s in scores[1:]:
        m = jnp.maximum(m, s.max(axis=-1, keepdims=True))
    l = None
    o = None
    for s, vp in zip(scores, v_parts):
        p = jnp.exp(s - m)
        ls = p.sum(axis=-1, keepdims=True)
        os_ = jnp.dot(p.astype(jnp.bfloat16), vp, preferred_element_type=jnp.float32)
        l = ls if l is None else l + ls
        o = os_ if o is None else o + os_
    o = o * (1.0 / l)
    return jnp.where(lane_lo, o[:n], o[n:])


def _na_kernel(q_ref, k_ref, v_ref, kc_ref, vc_ref, bias_ref, *rest, need_ctx):
    if need_ctx:
        qc_ref, o_ref, oc_ref, s_ref = rest
    else:
        o_ref, s_ref = rest
    S = q_ref.shape[1]
    Tc = kc_ref.shape[1]
    rows = S // GRID_W
    n_pairs = BRANCH_W // LANES
    n_loc = NA_KH * GRID_W

    def window(r):
        rs = jnp.clip(r - NA_KH // 2, 0, rows - NA_KH)
        return r - rs, pl.multiple_of(r * GRID_W, GRID_W), pl.multiple_of(rs * GRID_W, GRID_W)

    def scores_stage(r, slot):
        var, q0, k0 = window(r)
        for j in range(n_pairs):
            ls = slice(j * LANES, (j + 1) * LANES)
            s_loc, s_ctx = _pair_scores(q_ref[0, pl.ds(q0, GRID_W), ls],
                                        (k_ref[0, pl.ds(k0, n_loc), ls], kc_ref[0, :, ls]),
                                        (bias_ref[var, j], None))
            s_ref[slot, j, :, :n_loc] = s_loc
            s_ref[slot, j, :, n_loc:] = s_ctx

    def output_stage(r, slot):
        _, q0, k0 = window(r)
        outs = []
        for j in range(n_pairs):
            ls = slice(j * LANES, (j + 1) * LANES)
            outs.append(_pair_softmax_pv((s_ref[slot, j, :, :n_loc], s_ref[slot, j, :, n_loc:]),
                                         (v_ref[0, pl.ds(k0, n_loc), ls], vc_ref[0, :, ls])))
        o_ref[0, pl.ds(q0, GRID_W), :] = jnp.concatenate(outs, axis=-1).astype(o_ref.dtype)

    scores_stage(0, 0)

    def two_rows(i, carry):
        r0 = 2 * i
        scores_stage(r0 + 1, 1)
        output_stage(r0, 0)
        scores_stage(jnp.minimum(r0 + 2, rows - 1), 0)
        output_stage(r0 + 1, 1)
        return carry

    lax.fori_loop(0, rows // 2, two_rows, 0)

    if need_ctx:
        outs = []
        for j in range(n_pairs):
            ls = slice(j * LANES, (j + 1) * LANES)
            scores = _pair_scores(qc_ref[0, :, ls], (kc_ref[0, :, ls],), (None,))
            outs.append(_pair_softmax_pv(scores, (vc_ref[0, :, ls],)))
        oc_ref[0] = jnp.concatenate(outs, axis=-1).astype(oc_ref.dtype)


def na_bias_table(rpb):
    H = rpb.shape[0]
    var = jnp.arange(NA_KH)
    kr = jnp.arange(NA_KH)
    dr = kr[None, :] - var[:, None] + NA_KH - 1
    cols = jnp.arange(GRID_W)
    dc = jnp.clip(cols[None, :] - cols[:, None] + NA_KW - 1, 0, 2 * NA_KW - 2)
    cs = jnp.clip(cols - NA_KW // 2, 0, GRID_W - NA_KW)
    colmask = (cols[None, :] >= cs[:, None]) & (cols[None, :] < cs[:, None] + NA_KW)
    tab = rpb[:, dr][:, :, :, dc]
    tab = jnp.where(colmask[None, None, None], tab.astype(jnp.float32), MASK_NEG)
    tab = tab.transpose(1, 0, 3, 2, 4)
    return tab.reshape(NA_KH, H // 2, 2 * GRID_W, NA_KH * GRID_W)


def neighbourhood_attention(qn, kn, vn, kc, vc, bias_tab, qc=None):
    B, S, W = qn.shape
    Tc = kc.shape[1]
    need_ctx = qc is not None
    lat = pl.BlockSpec((1, S, W), lambda b: (b, 0, 0))
    cx = pl.BlockSpec((1, Tc, W), lambda b: (b, 0, 0))
    in_specs = [lat, lat, lat, cx, cx, _const_spec(bias_tab.shape)]
    args = [qn, kn, vn, kc, vc, bias_tab]
    out_shape = [jax.ShapeDtypeStruct((B, S, W), jnp.bfloat16)]
    out_specs = [lat]
    if need_ctx:
        in_specs.append(cx)
        args.append(qc)
        out_shape.append(jax.ShapeDtypeStruct((B, Tc, W), jnp.bfloat16))
        out_specs.append(cx)
    res = pl.pallas_call(
        functools.partial(_na_kernel, need_ctx=need_ctx),
        grid=(B,),
        in_specs=in_specs,
        out_specs=out_specs,
        out_shape=out_shape,
        scratch_shapes=[pltpu.VMEM((2, W // LANES, 2 * GRID_W, NA_KH * GRID_W + Tc), jnp.float32)],
        compiler_params=pltpu.CompilerParams(dimension_semantics=("arbitrary",), vmem_limit_bytes=VMEM_LIMIT),
        name="neighbourhood_attention",
    )(*args)
    return (res[0], res[1]) if need_ctx else (res[0], None)


def _split_bf16(x):
    hi = x.astype(jnp.bfloat16)
    r1 = x - hi.astype(jnp.float32)
    mid = r1.astype(jnp.bfloat16)
    lo = (r1 - mid.astype(jnp.float32)).astype(jnp.bfloat16)
    return hi, mid, lo


def _dot_f32(a, b):
    return jnp.dot(a, b, preferred_element_type=jnp.float32)


def _log_sigmoid(x):
    return jnp.minimum(x, 0.0) - jnp.log(1.0 + jnp.exp(-jnp.abs(x)))


def _mlstm_chunk(qk_ref, v_ref, g_ref, gt_ref, h_ref, ct_ref, m_ref, ci, fwd):
    L = M_CHUNK
    f32, bf16 = jnp.float32, jnp.bfloat16
    r0 = pl.multiple_of(ci * L, L)
    rows = pl.ds(r0, L)
    sq_r = lax.broadcasted_iota(jnp.int32, (L, L), 0)
    sq_c = lax.broadcasted_iota(jnp.int32, (L, L), 1)
    tri_l = jnp.where(sq_r >= sq_c, 1.0, 0.0).astype(bf16)
    tri_u = jnp.where(sq_r <= sq_c, 1.0, 0.0).astype(bf16)
    gates = g_ref[0, rows, :]
    gates_t = gt_ref[0, ci]
    lf, lf_t = _log_sigmoid(gates), _log_sigmoid(gates_t)
    cum = sum(_dot_f32(tri_l if fwd else tri_u, p) for p in _split_bf16(lf))
    cum_t = sum(_dot_f32(p, tri_u if fwd else tri_l) for p in _split_bf16(lf_t))
    t_i = 0 if fwd else 2
    last = L - 1 if fwd else 0
    st_r = lax.broadcasted_iota(jnp.int32, (2 * L, L), 0) & (L - 1)
    st_c = lax.broadcasted_iota(jnp.int32, (2 * L, L), 1)
    valid = (st_r >= st_c) if fwd else (st_r <= st_c)
    lane_lo = lax.broadcasted_iota(jnp.int32, (L, LANES), 1) < M_DQK
    row_lo = lax.broadcasted_iota(jnp.int32, (LANES, L), 0) < M_DQK
    ones = jnp.ones((L, LANES), bf16)

    def col2(tile, c0, c1):
        return jnp.concatenate([jnp.broadcast_to(tile[:, c0:c0 + 1], (L, LANES)),
                                jnp.broadcast_to(tile[:, c1:c1 + 1], (L, LANES))], axis=0)

    def row2(tile, c0, c1):
        return jnp.concatenate([jnp.broadcast_to(tile[c0:c0 + 1, :], (L, L)),
                                jnp.broadcast_to(tile[c1:c1 + 1, :], (L, L))], axis=0)

    for j in range(M_HEADS // 2):
        h0, h1 = 2 * j, 2 * j + 1
        sidx = 2 * j + (0 if fwd else 1)
        ci0, ci1 = t_i * M_HEADS + h0, t_i * M_HEADS + h1
        cf0, cf1 = ci0 + M_HEADS, ci1 + M_HEADS
        i_col, b_col = col2(gates, ci0, ci1), col2(cum, cf0, cf1)
        dm = jnp.where(valid, b_col - row2(cum_t, cf0, cf1) + row2(gates_t, ci0, ci1), MASK_NEG)
        m_col = m_ref[sidx]
        qp = qk_ref[0, rows, j * LANES:(j + 1) * LANES]
        kp = qk_ref[0, rows, M_HEADS * M_DQK + j * LANES:M_HEADS * M_DQK + (j + 1) * LANES]
        vx0 = jnp.concatenate([v_ref[0, rows, h0 * M_DV:(h0 + 1) * M_DV], ones], axis=-1)
        vx1 = jnp.concatenate([v_ref[0, rows, h1 * M_DV:(h1 + 1) * M_DV], ones], axis=-1)
        ct = ct_ref[sidx]
        if h_ref is not None:
            zq = jnp.zeros_like(qp)
            qq = jnp.concatenate([jnp.where(lane_lo, qp, zq), jnp.where(lane_lo, zq, qp)], axis=0)
            g_col = b_col + m_col
            mt = jnp.maximum(g_col, jnp.broadcast_to(dm.max(axis=-1, keepdims=True), (2 * L, LANES)))
            s = lax.dot_general(qq, kp, (((1,), (1,)), ((), ())), preferred_element_type=f32)
            sc = (s * jnp.exp(dm - mt)).astype(bf16)
            inter = jnp.exp(g_col - mt)
            t1 = _dot_f32(qq, ct.astype(bf16))
            t2 = jnp.concatenate([_dot_f32(sc[:L], vx0), _dot_f32(sc[L:], vx1)], axis=0)
            num = inter * t1[:, :M_DV] + t2[:, :M_DV]
            den = inter * t1[:, M_DV:] + t2[:, M_DV:]
            h = num / jnp.maximum(jnp.abs(den), jnp.exp(-mt))
            h_ref[rows, h0 * M_DV:(h0 + 1) * M_DV] = h[:L]
            h_ref[rows, h1 * M_DV:(h1 + 1) * M_DV] = h[L:]
        bl = jnp.concatenate([jnp.broadcast_to(b_col[last:last + 1], (L, LANES)),
                              jnp.broadcast_to(b_col[L + last:L + last + 1], (L, LANES))], axis=0)
        w = bl - b_col + i_col
        wm = jnp.concatenate([jnp.broadcast_to(w[:L].max(axis=0, keepdims=True), (L, LANES)),
                              jnp.broadcast_to(w[L:].max(axis=0, keepdims=True), (L, LANES))], axis=0)
        m_new = jnp.maximum(bl + m_col, wm)
        decay = jnp.exp(bl + m_col - m_new)
        ws = jnp.exp(w - m_new)
        kw_t = (kp.astype(f32) * jnp.where(lane_lo, ws[:L], ws[L:])).T
        zk = jnp.zeros_like(kw_t)
        upd = (_dot_f32(jnp.where(row_lo, kw_t, zk).astype(bf16), vx0)
               + _dot_f32(jnp.where(row_lo, zk, kw_t).astype(bf16), vx1))
        d_rows = jnp.concatenate([decay[:M_DQK], decay[L:L + M_DQK]], axis=0)
        ct_ref[sidx] = jnp.concatenate([d_rows, d_rows], axis=-1) * ct + upd
        m_ref[sidx] = m_new


def _mlstm_finish(hf_ref, hb_ref, o_ref, out_ref, n_chunks):
    L = M_CHUNK

    def body(ci, carry):
        rows = pl.ds(pl.multiple_of(ci * L, L), L)
        h = hf_ref[rows, :] + hb_ref[rows, :]
        parts = []
        for hd in range(M_HEADS):
            hh = h[:, hd * M_DV:(hd + 1) * M_DV]
            parts.append(hh * lax.rsqrt(jnp.mean(hh * hh, axis=-1, keepdims=True) + RMS_EPS))
        gate = jax.nn.sigmoid(o_ref[0, rows, :].astype(jnp.float32))
        out_ref[0, rows, :] = (gate * jnp.concatenate(parts, axis=-1)).astype(out_ref.dtype)
        return carry

    lax.fori_loop(0, n_chunks, body, 0)


def _mlstm_kernel(qk_c, v_c, o_c, g_c, gt_c, qk_l, v_l, o_l, g_l, gt_l, *rest, need_ctx):
    if need_ctx:
        out_l, out_c, hf_l, hb_l, hf_c, hb_c, ct_ref, m_ref = rest
    else:
        out_l, hf_l, hb_l, ct_ref, m_ref = rest
        hf_c = hb_c = None
    nc, nl = qk_c.shape[1] // M_CHUNK, qk_l.shape[1] // M_CHUNK
    ct_ref[...] = jnp.zeros_like(ct_ref)
    m_ref[...] = jnp.full_like(m_ref, M_INIT)

    def phase(qk, v, g, gt, hf, hb, n):
        def body(i, carry):
            _mlstm_chunk(qk, v, g, gt, hf, ct_ref, m_ref, i, True)
            _mlstm_chunk(qk, v, g, gt, hb, ct_ref, m_ref, n - 1 - i, False)
            return carry
        lax.fori_loop(0, n, body, 0)

    phase(qk_c, v_c, g_c, gt_c, hf_c, hb_c, nc)
    phase(qk_l, v_l, g_l, gt_l, hf_l, hb_l, nl)
    _mlstm_finish(hf_l, hb_l, o_l, out_l, nl)
    if need_ctx:
        _mlstm_finish(hf_c, hb_c, o_c, out_c, nc)


def _gates_chunk_major(g):
    B, T, _ = g.shape
    return g[..., :4 * M_HEADS].reshape(B, T // M_CHUNK, M_CHUNK, 4 * M_HEADS).transpose(0, 1, 3, 2)


def mlstm(pc, pl_, need_ctx):
    B, S, W = pl_["mv"].shape
    Tc = pc["mv"].shape[1]
    f32 = jnp.float32

    def specs(T):
        s = pl.BlockSpec((1, T, W), lambda b: (b, 0, 0))
        return [s, s, s, pl.BlockSpec((1, T, GATE_W), lambda b: (b, 0, 0)),
                pl.BlockSpec((1, T // M_CHUNK, 4 * M_HEADS, M_CHUNK), lambda b: (b, 0, 0, 0))]

    def args(p):
        return [p["mqk"], p["mv"], p["mo"], p["g"], _gates_chunk_major(p["g"])]

    out_shape = [jax.ShapeDtypeStruct((B, S, W), jnp.bfloat16)]
    out_specs = [pl.BlockSpec((1, S, W), lambda b: (b, 0, 0))]
    scratch = [pltpu.VMEM((S, W), f32), pltpu.VMEM((S, W), f32)]
    if need_ctx:
        out_shape.append(jax.ShapeDtypeStruct((B, Tc, W), jnp.bfloat16))
        out_specs.append(pl.BlockSpec((1, Tc, W), lambda b: (b, 0, 0)))
        scratch += [pltpu.VMEM((Tc, W), f32), pltpu.VMEM((Tc, W), f32)]
    scratch += [pltpu.VMEM((M_HEADS, LANES, 2 * M_DV), f32), pltpu.VMEM((M_HEADS, 2 * M_CHUNK, LANES), f32)]
    res = pl.pallas_call(
        functools.partial(_mlstm_kernel, need_ctx=need_ctx),
        grid=(B,),
        in_specs=specs(Tc) + specs(S),
        out_specs=out_specs,
        out_shape=out_shape,
        scratch_shapes=scratch,
        compiler_params=pltpu.CompilerParams(dimension_semantics=("arbitrary",), vmem_limit_bytes=VMEM_LIMIT),
        name="mlstm",
    )(*args(pc), *args(pl_))
    return (res[0], res[1]) if need_ctx else (res[0], None)


POOL_HALO = 16


def _pool_mix(pu_ref, prev_ref, next_ref, pw_ref, ps_ref, tile_idx, n_tiles, seq_len):
    f32 = jnp.float32
    tm = pu_ref.shape[1]
    cur = pu_ref[0].astype(f32)
    prev = jnp.where(tile_idx > 0, prev_ref[0].astype(f32), 0.0)
    nxt = jnp.where(tile_idx < n_tiles - 1, next_ref[0].astype(f32), 0.0)
    ext = jnp.concatenate([prev, cur, nxt], axis=0)
    n_ext = tm + 2 * POOL_HALO
    tok = tile_idx * tm + lax.broadcasted_iota(jnp.int32, (tm, POOL_GC), 0)

    def shifted(a, d):
        return pltpu.roll(a, (-d) % n_ext, 0)

    outs = []
    for gi, w in enumerate(POOL_WINDOWS):
        s = ext[:, gi * POOL_GC:(gi + 1) * POOL_GC]
        s = shifted(s, -1) + s
        span = 2
        while span < w:
            s = shifted(s, -(span // 2)) + shifted(s, span // 2)
            span *= 2
        lo = jnp.clip(tok - w // 2, 0, seq_len - 1)
        hi = jnp.clip(tok - w // 2 + w - 1, 0, seq_len - 1)
        mean = s[POOL_HALO:POOL_HALO + tm] / (hi - lo + 1).astype(f32)
        d = (mean - cur[:, gi * POOL_GC:(gi + 1) * POOL_GC]).astype(jnp.bfloat16)
        outs.append(_dot_f32(d, pw_ref[gi]))
    return jnp.concatenate(outs, axis=-1) * ps_ref[...]


def _pack_bf16_pairs(h):
    n = h.shape[1] // 2
    hb = h.astype(jnp.bfloat16).astype(jnp.float32)
    lo = pltpu.bitcast(hb[:, :n], jnp.uint32) >> 16
    hi = pltpu.bitcast(hb[:, n:], jnp.uint32) & jnp.uint32(0xFFFF0000)
    return lo | hi


def _merge_kernel(m_ref, n_ref, pu_ref, prev_ref, next_ref, bg_ref, x_ref, mods_ref, gains_ref,
                  pw_ref, ps_ref, wb_ref, wo_ref, wrt_ref, wrp_ref, x1_ref, h2a_ref, aff_ref, *, seq_len):
    f32, bf16 = jnp.float32, jnp.bfloat16
    i = pl.program_id(1)
    D = x_ref.shape[2]
    E = wrt_ref.shape[0]
    p = _pool_mix(pu_ref, prev_ref, next_ref, pw_ref, ps_ref, i, pl.num_programs(1), seq_len)
    branches = (m_ref[0], p.astype(bf16), n_ref[0])
    acc = None
    for bi, br in enumerate(branches):
        gate = jax.nn.sigmoid(bg_ref[0, :, bi * D:(bi + 1) * D].astype(f32))
        term = gate * _dot_f32(br, wb_ref[bi])
        acc = term if acc is None else acc + term
    y = _dot_f32(acc.astype(bf16), wo_ref[...])
    yn = y * lax.rsqrt(jnp.mean(y * y, axis=-1, keepdims=True) + RMS_EPS) * gains_ref[0:1, :]
    x1 = x_ref[0] + mods_ref[0, 0:1, :] * yn
    x1_ref[0] = x1
    xn = x1 * lax.rsqrt(jnp.mean(x1 * x1, axis=-1, keepdims=True) + RMS_EPS) * gains_ref[1:2, :]
    h2 = xn * (1.0 + mods_ref[0, 2:3, :]) + mods_ref[0, 1:2, :]
    h2b = h2.astype(bf16)
    lg_t = lax.dot_general(wrt_ref[...], h2b, (((1,), (1,)), ((), ())), preferred_element_type=f32)
    e_t = jnp.exp(lg_t - lg_t.max(axis=0, keepdims=True))
    aff_ref[0] = e_t / e_t.sum(axis=0, keepdims=True)
    lg = _dot_f32(h2b, wrp_ref[...])
    lg = jnp.where(lax.broadcasted_iota(jnp.int32, lg.shape, 1) < E, lg, MASK_NEG)
    e_r = jnp.exp(lg - lg.max(axis=-1, keepdims=True))
    aff_rows = e_r / e_r.sum(axis=-1, keepdims=True)
    h2a_ref[0] = jnp.concatenate([_pack_bf16_pairs(h2), pltpu.bitcast(aff_rows, jnp.uint32)], axis=-1)


def merge_out(m, n, pu, bg, x, mods, gains, pool_w, pool_scale, w_branch, w_out, w_router_t, w_router_pad, *, tm):
    B, T, D = x.shape
    W = m.shape[2]
    E = w_router_t.shape[0]
    hb = tm // POOL_HALO
    n_halo = T // POOL_HALO
    tile = lambda w: pl.BlockSpec((1, tm, w), lambda b, i: (b, i, 0))
    in_specs = [
        tile(W), tile(W), tile(W),
        pl.BlockSpec((1, POOL_HALO, W), lambda b, i: (b, jnp.maximum(i * hb - 1, 0), 0)),
        pl.BlockSpec((1, POOL_HALO, W), lambda b, i: (b, jnp.minimum((i + 1) * hb, n_halo - 1), 0)),
        tile(N_BRANCH * D), tile(D),
        pl.BlockSpec((1, 8, D), lambda b, i: (b, 0, 0)),
        _const_spec((8, D)),
        _const_spec(pool_w.shape), _const_spec(pool_scale.shape), _const_spec(w_branch.shape),
        _const_spec(w_out.shape), _const_spec(w_router_t.shape), _const_spec(w_router_pad.shape),
    ]
    return pl.pallas_call(
        functools.partial(_merge_kernel, seq_len=T),
        grid=(B, T // tm),
        in_specs=in_specs,
        out_specs=[tile(D), tile(MOE_ROW), pl.BlockSpec((1, E, tm), lambda b, i: (b, 0, i))],
        out_shape=[jax.ShapeDtypeStruct((B, T, D), jnp.float32),
                   jax.ShapeDtypeStruct((B, T, MOE_ROW), jnp.uint32),
                   jax.ShapeDtypeStruct((B, E, T), jnp.float32)],
        compiler_params=pltpu.CompilerParams(
            dimension_semantics=("arbitrary", "arbitrary"), vmem_limit_bytes=VMEM_LIMIT),
        name="merge_out",
    )(m, n, pu, pu, pu, bg, x, mods, gains, pool_w, pool_scale, w_branch, w_out, w_router_t, w_router_pad)


def _cumsum_lanes(x01):
    R, T = x01.shape
    r = lax.broadcasted_iota(jnp.int32, (LANES, LANES), 0)
    c = lax.broadcasted_iota(jnp.int32, (LANES, LANES), 1)
    tri_u = jnp.where(r <= c, 1.0, 0.0).astype(jnp.bfloat16)
    run = jnp.zeros((R, 1), jnp.float32)
    outs = []
    for t0 in range(0, T, LANES):
        cs = _dot_f32(x01[:, t0:t0 + LANES].astype(jnp.bfloat16), tri_u) + run
        run = cs[:, LANES - 1:LANES]
        outs.append(cs)
    return jnp.concatenate(outs, axis=-1)


def _route_kernel(aff_ref, idx_ref, *, cap):
    f32 = jnp.float32
    aff = aff_ref[0]
    E, T = aff.shape
    bits = pltpu.bitcast(aff, jnp.int32)

    def search(k, lo):
        cand = lo | (jnp.int32(1) << (30 - k))
        cnt = jnp.sum(jnp.where(bits >= cand, 1.0, 0.0), axis=-1, keepdims=True)
        return jnp.where(cnt >= cap, cand, lo)

    thr = lax.fori_loop(0, 31, search, jnp.zeros((E, 1), jnp.int32))
    gt = jnp.where(bits > thr, 1.0, 0.0)
    eq = jnp.where(bits == thr, 1.0, 0.0)
    room = cap - jnp.sum(gt, axis=-1, keepdims=True)
    sel = gt + eq * jnp.where(_cumsum_lanes(eq) <= room, 1.0, 0.0)
    csel = _cumsum_lanes(sel)
    slot = lax.broadcasted_iota(jnp.int32, (cap, T), 0).astype(f32)
    lane = lax.broadcasted_iota(jnp.int32, (cap, LANES), 1)
    out = jnp.zeros((cap, LANES), f32)
    for e in range(E):
        pos = jnp.sum(jnp.where(jnp.broadcast_to(csel[e:e + 1, :], (cap, T)) <= slot, 1.0, 0.0),
                      axis=-1, keepdims=True)
        out = jnp.where(lane == e, jnp.broadcast_to(pos, (cap, LANES)), out)
    idx_ref[0] = out.astype(jnp.int32)


def route(aff_t, cap):
    B, E, T = aff_t.shape
    idx_t = pl.pallas_call(
        functools.partial(_route_kernel, cap=cap),
        grid=(B,),
        in_specs=[pl.BlockSpec((1, E, T), lambda b: (b, 0, 0))],
        out_specs=pl.BlockSpec((1, cap, LANES), lambda b: (b, 0, 0)),
        out_shape=jax.ShapeDtypeStruct((B, cap, LANES), jnp.int32),
        compiler_params=pltpu.CompilerParams(dimension_semantics=("arbitrary",), vmem_limit_bytes=VMEM_LIMIT),
        name="route",
    )(aff_t)
    return idx_t[:, :, :E].transpose(0, 2, 1)


ROW_GROUP = 8


def _moe_kernel(idx_ref, rows_ref, wg_ref, wu_ref, wd_ref, out_ref, xg0_ref, xg1_ref, ye0_ref, ye1_ref):
    f32, bf16 = jnp.float32, jnp.bfloat16
    e = pl.program_id(1)
    n_exp = pl.num_programs(1)
    cap = xg0_ref.shape[0]
    half = wg_ref.shape[1] // 2

    def gather_row(dst_ref, ee, j):
        dst_ref[pl.ds(j, 1), :] = rows_ref[0, pl.ds(idx_ref[0, 0, ee, j], 1), :]

    def scatter_group(src_ref, ee, j0):
        ids = [idx_ref[0, 0, ee, j0 + r] for r in range(ROW_GROUP)]
        cur = [out_ref[0, pl.ds(i, 1), :] for i in ids]
        for r, i in enumerate(ids):
            out_ref[0, pl.ds(i, 1), :] = cur[r] + src_ref[pl.ds(j0 + r, 1), :]

    @pl.when(e == 0)
    def _():
        out_ref[...] = jnp.zeros_like(out_ref)
        ye1_ref[...] = jnp.zeros_like(ye1_ref)

        def body(gi, carry):
            for r in range(ROW_GROUP):
                gather_row(xg0_ref, 0, gi * ROW_GROUP + r)
            return carry
        lax.fori_loop(0, cap // ROW_GROUP, body, 0)

    def step(xg_cur, xg_nxt, ye_cur, ye_prv):
        e_prv = jnp.maximum(e - 1, 0)
        e_nxt = jnp.minimum(e + 1, n_exp - 1)
        for j0 in range(0, cap, ROW_GROUP):
            scatter_group(ye_prv, e_prv, j0)
        for j in range(cap):
            gather_row(xg_nxt, e_nxt, j)
        xg = xg_cur[...]
        packed = xg[:, :half]
        xe = jnp.concatenate([pltpu.bitcast(packed << 16, f32),
                              pltpu.bitcast(packed & jnp.uint32(0xFFFF0000), f32)], axis=-1).astype(bf16)
        aff = pltpu.bitcast(xg[:, half:], f32)
        gate = jnp.sum(jnp.where(lax.broadcasted_iota(jnp.int32, aff.shape, 1) == e, aff, 0.0),
                       axis=-1, keepdims=True)
        a = _dot_f32(xe, wg_ref[0])
        hid = (a * jax.nn.sigmoid(a) * _dot_f32(xe, wu_ref[0])).astype(bf16)
        ye_cur[...] = _dot_f32(hid, wd_ref[0]) * gate

    @pl.when(e % 2 == 0)
    def _():
        step(xg0_ref, xg1_ref, ye0_ref, ye1_ref)

    @pl.when(e % 2 == 1)
    def _():
        step(xg1_ref, xg0_ref, ye1_ref, ye0_ref)

    @pl.when(e == n_exp - 1)
    def _():
        def body(gi, carry):
            scatter_group(ye1_ref, e, gi * ROW_GROUP)
            return carry
        lax.fori_loop(0, cap // ROW_GROUP, body, 0)


def moe_experts(rows, idx, w_gate, w_up, w_down):
    G, T, RW = rows.shape
    _, E, cap = idx.shape
    D = w_gate.shape[1]
    assert E % 2 == 0 and cap % ROW_GROUP == 0
    wspec = pl.BlockSpec((1, D, D), lambda g, e: (e, 0, 0))
    return pl.pallas_call(
        _moe_kernel,
        grid=(G, E),
        in_specs=[pl.BlockSpec((1, 1, E, cap), lambda g, e: (g, 0, 0, 0), memory_space=pltpu.SMEM),
                  pl.BlockSpec((1, T, RW), lambda g, e: (g, 0, 0)), wspec, wspec, wspec],
        out_specs=pl.BlockSpec((1, T, D), lambda g, e: (g, 0, 0)),
        out_shape=jax.ShapeDtypeStruct((G, T, D), jnp.float32),
        scratch_shapes=[pltpu.VMEM((cap, RW), jnp.uint32), pltpu.VMEM((cap, RW), jnp.uint32),
                        pltpu.VMEM((cap, D), jnp.float32), pltpu.VMEM((cap, D), jnp.float32)],
        compiler_params=pltpu.CompilerParams(
            dimension_semantics=("arbitrary", "arbitrary"), vmem_limit_bytes=VMEM_LIMIT),
        name="moe_experts",
    )(idx.reshape(G, 1, E, cap), rows, w_gate, w_up, w_down)


def _residual_norm_kernel(x_ref, y_ref, gate_ref, gain_ref, o_ref):
    y = y_ref[0]
    yn = y * lax.rsqrt(jnp.mean(y * y, axis=-1, keepdims=True) + RMS_EPS) * gain_ref[...]
    o_ref[0] = x_ref[0] + gate_ref[0] * yn


def residual_norm(x, y, gate, gain, *, tm):
    B, T, D = x.shape
    tile = pl.BlockSpec((1, tm, D), lambda b, i: (b, i, 0))
    return pl.pallas_call(
        _residual_norm_kernel,
        grid=(B, T // tm),
        in_specs=[tile, tile, pl.BlockSpec((1, 1, D), lambda b, i: (b, 0, 0)), _const_spec((1, D))],
        out_specs=tile,
        out_shape=jax.ShapeDtypeStruct((B, T, D), jnp.float32),
        compiler_params=pltpu.CompilerParams(
            dimension_semantics=("arbitrary", "arbitrary"), vmem_limit_bytes=VMEM_LIMIT),
        name="residual_norm",
    )(x, y, gate, gain)


def rmsnorm(x, g):
    xf = x.astype(jnp.float32)
    y = xf * lax.rsqrt(jnp.mean(xf * xf, axis=-1, keepdims=True) + RMS_EPS)
    return (y * g.astype(jnp.float32)).astype(x.dtype)


def modulate(h, shift, scale):
    return h * (1 + scale) + shift


def rope_tables(n_tokens, dtype):
    t = jnp.arange(n_tokens)
    row = (t // GRID_W).astype(jnp.float32)
    col = (t % GRID_W).astype(jnp.float32)
    half = M_DQK // 2
    inv = ROPE_BASE ** (-jnp.arange(0, half, 2, dtype=jnp.float32) / half)
    ar = row[:, None] * inv[None, :]
    ac = col[:, None] * inv[None, :]
    return tuple(a.astype(dtype) for a in (jnp.cos(ar), jnp.sin(ar), jnp.cos(ac), jnp.sin(ac)))


def _rotate(x, cos, sin):
    x1, x2 = jnp.split(x, 2, axis=-1)
    cos = cos[:, None, :]
    sin = sin[:, None, :]
    return jnp.concatenate([x1 * cos - x2 * sin, x2 * cos + x1 * sin], axis=-1)


def rope_2d(x, tabs):
    cr, sr, cc, sc = tabs
    xr, xc = jnp.split(x, 2, axis=-1)
    return jnp.concatenate([_rotate(xr, cr, sr), _rotate(xc, cc, sc)], axis=-1)


def mlstm_scan(q, k, v, li, lf, state):
    B, T, H, _ = q.shape
    dv = v.shape[-1]
    L = M_CHUNK
    N = T // L

    def chunks(a):
        a = a.reshape((B, N, L, H) + a.shape[3:])
        return jnp.moveaxis(a, (1, 3), (0, 2))

    tril = jnp.tril(jnp.ones((L, L), dtype=bool))

    def step(carry, inp):
        C, n, m = carry
        qc, kc, vc, ic, fc = inp
        b = jnp.cumsum(fc, axis=-1)
        dmat = jnp.where(tril, b[..., :, None] - b[..., None, :] + ic[..., None, :], -jnp.inf)
        g = b + m[..., None]
        mt = jnp.maximum(g, dmat.max(-1))
        sc = jnp.einsum('bhtd,bhsd->bhts', qc, kc) * jnp.exp(dmat - mt[..., None])
        inter = jnp.exp(g - mt)
        num = inter[..., None] * jnp.einsum('bhvd,bhtd->bhtv', C, qc) + jnp.einsum('bhts,bhsv->bhtv', sc, vc)
        den = inter * jnp.einsum('bhd,bhtd->bht', n, qc) + sc.sum(-1)
        h = num / jnp.maximum(jnp.abs(den), jnp.exp(-mt))[..., None]
        bl = b[..., -1]
        w = bl[..., None] - b + ic
        m_new = jnp.maximum(bl + m, w.max(-1))
        decay = jnp.exp(bl + m - m_new)
        ws = jnp.exp(w - m_new[..., None])
        C_new = decay[..., None, None] * C + jnp.einsum('bhs,bhsv,bhsd->bhvd', ws, vc, kc)
        n_new = decay[..., None] * n + jnp.einsum('bhs,bhsd->bhd', ws, kc)
        return (C_new, n_new, m_new), h

    xs = (chunks(q), chunks(k), chunks(v), chunks(li), chunks(lf))
    state, h = lax.scan(step, state, xs)
    h = jnp.moveaxis(h, (0, 2), (1, 3)).reshape(B, T, H, dv)
    return state, h


def head_rms(h):
    return h * lax.rsqrt(jnp.mean(h * h, axis=-1, keepdims=True) + RMS_EPS)


def mlstm_branch(ctx_parts, lat_parts, need_ctx):
    f32 = jnp.float32

    def prep(parts):
        q, k, v, o, gt = parts
        q = q.astype(f32)
        k = k.astype(f32)
        v = v.astype(f32)
        fwd = (q, k, v, gt[..., 0, :], jax.nn.log_sigmoid(gt[..., 1, :]))
        bwd = tuple(jnp.flip(a, axis=1) for a in (q, k, v, gt[..., 2, :], jax.nn.log_sigmoid(gt[..., 3, :])))
        return fwd, bwd

    c_f, c_b = prep(ctx_parts)
    l_f, l_b = prep(lat_parts)
    B = l_f[0].shape[0]
    st0 = (jnp.zeros((B, M_HEADS, M_DV, M_DQK), f32), jnp.zeros((B, M_HEADS, M_DQK), f32),
           jnp.full((B, M_HEADS), M_INIT, f32))
    st_f, hc_f = mlstm_scan(*c_f, st0)
    _, hl_f = mlstm_scan(*l_f, st_f)
    st_b, hc_b = mlstm_scan(*c_b, st0)
    _, hl_b = mlstm_scan(*l_b, st_b)

    def finish(h_f, h_b_rev, o):
        h = head_rms(h_f + jnp.flip(h_b_rev, axis=1))
        return jax.nn.sigmoid(o) * h.reshape(o.shape).astype(o.dtype)

    out_l = finish(hl_f, hl_b, lat_parts[3])
    out_c = finish(hc_f, hc_b, ctx_parts[3]) if need_ctx else None
    return out_l, out_c


def pool_branch(u, pool_w, pool_scale):
    B, T, _ = u.shape
    uf = u.astype(jnp.float32).reshape(B, T, POOL_GROUPS, POOL_GC)
    csum = jnp.concatenate([jnp.zeros((B, 1, POOL_GROUPS, POOL_GC), jnp.float32),
                            jnp.cumsum(uf, axis=1)], axis=1)
    t = jnp.arange(T)
    outs = []
    for gi, w in enumerate(POOL_WINDOWS):
        lo = jnp.clip(t - w // 2, 0, T - 1)
        hi = jnp.clip(t - w // 2 + w - 1, 0, T - 1)
        cnt = (hi - lo + 1).astype(jnp.float32)[:, None]
        cg = csum[:, :, gi]
        mean = (cg[:, hi + 1] - cg[:, lo]) / cnt
        outs.append(jnp.einsum('btc,cd->btd', (mean - uf[:, :, gi]).astype(u.dtype), pool_w[gi]))
    return jnp.concatenate(outs, axis=-1) * pool_scale


def merge_branches(branches, bg, w_branch, w_out):
    D = w_out.shape[0]
    acc = None
    for bi, br in enumerate(branches):
        term = jax.nn.sigmoid(bg[..., bi * D:(bi + 1) * D]) * (br @ w_branch[bi])
        acc = term if acc is None else acc + term
    return acc @ w_out


def ec_moe(h, w_router, w_gate, w_up, w_down):
    B, T, D = h.shape
    cap = EC_FACTOR * T // N_EXPERTS
    aff = jax.nn.softmax(jnp.einsum('btd,de->bte', h, w_router, preferred_element_type=jnp.float32), axis=-1)
    gate, idx = lax.top_k(jnp.swapaxes(aff, 1, 2), cap)
    xe = jax.vmap(lambda hb, ib: hb[ib])(h, idx)
    hid = jax.nn.silu(jnp.einsum('becd,edf->becf', xe, w_gate)) * jnp.einsum('becd,edf->becf', xe, w_up)
    ye = jnp.einsum('becf,efd->becd', hid, w_down) * gate[..., None].astype(h.dtype)
    return jax.vmap(lambda ib, yb: jnp.zeros((T, D), yb.dtype).at[ib.reshape(-1)].add(yb.reshape(-1, D)))(idx, ye)


def _mparts(p, rope_tabs=None):
    B, T, _ = p["mqk"].shape
    f32 = jnp.float32
    hq = M_HEADS * M_DQK
    q = p["mqk"][..., :hq].astype(f32).reshape(B, T, M_HEADS, M_DQK)
    k = p["mqk"][..., hq:].astype(f32).reshape(B, T, M_HEADS, M_DQK)
    if rope_tabs is not None:
        q = rope_2d(q, rope_tabs)
        k = rope_2d(k, rope_tabs)
    v = p["mv"].astype(f32).reshape(B, T, M_HEADS, M_DV)
    return q, k, v, p["mo"].astype(f32), p["g"][..., :4 * M_HEADS].reshape(B, T, 4, M_HEADS)


def kernel(x, c, ctx, c_ctx, norm_gain, ada_w, ada_b, w_in, mlstm_gate_bias, pool_w, pool_scale,
           na_rpb, w_branch, w_out, router_w, w_gate, w_up, w_down):
    B, S, D = x.shape
    Tc = ctx.shape[1]
    f32, bf16 = jnp.float32, jnp.bfloat16
    rope_tabs = rope_lane_tables(S)
    names = [n for n, _ in IN_GROUPS]
    for l in range(DEPTH):
        need_ctx = l < DEPTH - 1
        g = norm_gain[l]
        mod_l = jnp.split((jax.nn.silu(c) @ ada_w[l] + ada_b[l])[:, None, :], N_MOD, axis=-1)
        mod_c = jnp.split(jax.nn.silu(c_ctx) @ ada_w[l] + ada_b[l], N_MOD, axis=-1)
        mod_cb = [jnp.broadcast_to(m[None, None], (B, 1, D)) for m in mod_c]
        w_perm = permute_w_in(w_in[l])
        gb_row = jnp.pad(mlstm_gate_bias[l].reshape(1, -1).astype(f32), ((0, 0), (0, GATE_W - 4 * M_HEADS)))
        pl_ = dict(zip(names, in_proj(x, g[0][None], mod_l[0], mod_l[1], w_perm, gb_row, rope_tabs, tm=512)))
        pc_ = dict(zip(names, in_proj(ctx, g[0][None], mod_cb[0], mod_cb[1], w_perm, gb_row, tm=Tc)))

        m_l, m_c = mlstm(pc_, pl_, need_ctx)
        n_l, n_c = neighbourhood_attention(pl_["qn"], pl_["kn"], pl_["vn"], pc_["kn"], pc_["vn"],
                                           na_bias_table(na_rpb[l]), pc_["qn"] if need_ctx else None)
        gains = jnp.pad(g[1:3], ((0, 6), (0, 0)))
        merge_w = (pool_w[l].astype(bf16), pool_scale[l][None], w_branch[l].astype(bf16), w_out[l].astype(bf16),
                   router_w[l].T.astype(bf16),
                   jnp.pad(router_w[l], ((0, 0), (0, LANES - N_EXPERTS))).astype(bf16))
        moe_w = (w_gate[l].astype(bf16), w_up[l].astype(bf16), w_down[l].astype(bf16))

        def mods8(mods):
            return jnp.pad(jnp.concatenate([mods[2], mods[3], mods[4]], axis=1), ((0, 0), (0, 5), (0, 0)))

        x1, rows, aff_t = merge_out(m_l, n_l, pl_["pu"], pl_["bg"], x, mods8(mod_l), gains, *merge_w, tm=512)
        y = moe_experts(rows, route(aff_t, EC_FACTOR * S // N_EXPERTS), *moe_w)
        x = residual_norm(x1, y, mod_l[5], g[3][None], tm=512)
        if need_ctx:
            c1, rows, aff_t = merge_out(m_c, n_c, pc_["pu"], pc_["bg"], ctx, mods8(mod_cb), gains, *merge_w, tm=Tc)
            per = S // Tc
            cap_c = EC_FACTOR * Tc // N_EXPERTS
            idx = route(aff_t, cap_c) + (jnp.arange(B, dtype=jnp.int32) % per * Tc)[:, None, None]
            idx = idx.reshape(B // per, per, N_EXPERTS, cap_c).transpose(0, 2, 1, 3).reshape(B // per, N_EXPERTS, per * cap_c)
            y = moe_experts(rows.reshape(B // per, S, MOE_ROW), idx, *moe_w).reshape(B, Tc, D)
            ctx = residual_norm(c1, y, mod_cb[5], g[3][None], tm=Tc)
    return x
```

```python
import functools
import math

import jax
import jax.numpy as jnp
import numpy as np
from jax import lax
from jax.experimental import pallas as pl
from jax.experimental.pallas import tpu as pltpu

D_MODEL = 1024
DEPTH = 2
GRID_W = 64
BRANCH_W = D_MODEL // 2
N_BRANCH = 3
M_HEADS = 4
M_DV = BRANCH_W // M_HEADS
M_DQK = M_DV // 2
M_CHUNK = 128
M_INIT = -1e30
POOL_GROUPS = 4
POOL_GC = BRANCH_W // POOL_GROUPS
POOL_WINDOWS = (2, 4, 8, 16)
NA_HEADS = 8
NA_DH = BRANCH_W // NA_HEADS
NA_KH = 8
NA_KW = 16
N_EXPERTS = 16
EC_FACTOR = 2
ROPE_BASE = 10000.0
RMS_EPS = 1e-6
N_MOD = 6
SPLIT_SIZES = (M_HEADS * M_DQK, M_HEADS * M_DQK, M_HEADS * M_DV, M_HEADS * M_DV, 4 * M_HEADS,
               POOL_GROUPS * POOL_GC, NA_HEADS * NA_DH, NA_HEADS * NA_DH, NA_HEADS * NA_DH,
               N_BRANCH * D_MODEL)
PROJ_W = sum(SPLIT_SIZES)

LANES = 128
VMEM_LIMIT = 56 * 1024 * 1024
MASK_NEG = -1e30
GATE_W = LANES

IN_GROUPS = (("mqk", 2 * M_HEADS * M_DQK), ("mv", BRANCH_W), ("mo", BRANCH_W), ("pu", BRANCH_W),
             ("qn", BRANCH_W), ("kn", BRANCH_W), ("vn", BRANCH_W), ("bg", N_BRANCH * D_MODEL),
             ("g", GATE_W))
IN_W = sum(w for _, w in IN_GROUPS)
MM_COLS = 512
MOE_ROW = D_MODEL // 2 + LANES


def _const_spec(shape):
    nd = len(shape)
    return pl.BlockSpec(shape, lambda *_: (0,) * nd, pipeline_mode=pl.Buffered(1))


def _rope_tile(x, cos, sin_signed):
    half = M_DQK // 4
    lane = lax.broadcasted_iota(jnp.int32, x.shape, 1)
    partner = jnp.where((lane % (2 * half)) < half, pltpu.roll(x, LANES - half, 1), pltpu.roll(x, half, 1))
    return x * cos + partner * sin_signed


def _in_proj_kernel(x_ref, g_ref, shift_ref, scale_ref, w_ref, gb_ref, *rest, rope, pending):
    if rope:
        cos_ref, sin_ref = rest[:2]
        rest = rest[2:]
    if pending:
        r_ref, rgate_ref, rgain_ref = rest[:3]
        o_refs, xo_ref = rest[3:-1], rest[-1]
        r = r_ref[0]
        x = x_ref[0] + rgate_ref[0] * (r * lax.rsqrt(jnp.mean(r * r, axis=-1, keepdims=True) + RMS_EPS)
                                       * rgain_ref[...])
        xo_ref[0] = x
    else:
        o_refs = rest
        x = x_ref[0]
    y = x * lax.rsqrt(jnp.mean(x * x, axis=-1, keepdims=True) + RMS_EPS)
    h = ((y * g_ref[...]) * (1.0 + scale_ref[0]) + shift_ref[0]).astype(jnp.bfloat16)
    off = 0
    for (name, width), o_ref in zip(IN_GROUPS, o_refs):
        for c0 in range(0, width, MM_COLS):
            cw = min(MM_COLS, width - c0)
            acc = jnp.dot(h, w_ref[:, off + c0:off + c0 + cw], preferred_element_type=jnp.float32)
            if name == "g":
                acc = acc + gb_ref[...]
            if name == "mqk" and rope:
                cos, sin = cos_ref[...], sin_ref[...]
                acc = jnp.concatenate([_rope_tile(acc[:, t:t + LANES], cos, sin) for t in range(0, cw, LANES)], axis=-1)
            o_ref[0, :, c0:c0 + cw] = acc.astype(o_ref.dtype)
        off += width


def in_proj(x, g, shift, scale, w_perm, gate_bias_row, rope_tabs=None, pending=None, *, tm):
    B, T, D = x.shape
    out_shape = [jax.ShapeDtypeStruct((B, T, w), jnp.float32 if n == "g" else jnp.bfloat16) for n, w in IN_GROUPS]
    out_specs = [pl.BlockSpec((1, tm, w), lambda b, i: (b, i, 0)) for _, w in IN_GROUPS]
    in_specs = [
        pl.BlockSpec((1, tm, D), lambda b, i: (b, i, 0)),
        _const_spec((1, D)),
        pl.BlockSpec((1, 1, D), lambda b, i: (b, 0, 0)),
        pl.BlockSpec((1, 1, D), lambda b, i: (b, 0, 0)),
        _const_spec((D, IN_W)),
        _const_spec((1, GATE_W)),
    ]
    args = [x, g, shift, scale, w_perm, gate_bias_row]
    if rope_tabs is not None:
        in_specs += [pl.BlockSpec((tm, LANES), lambda b, i: (i, 0))] * 2
        args += list(rope_tabs)
    if pending is not None:
        in_specs += [pl.BlockSpec((1, tm, D), lambda b, i: (b, i, 0)), pl.BlockSpec((1, 1, D), lambda b, i: (b, 0, 0)),
                     _const_spec((1, D))]
        args += list(pending)
        out_shape.append(jax.ShapeDtypeStruct((B, T, D), jnp.float32))
        out_specs.append(pl.BlockSpec((1, tm, D), lambda b, i: (b, i, 0)))
    return pl.pallas_call(
        functools.partial(_in_proj_kernel, rope=rope_tabs is not None, pending=pending is not None),
        grid=(B, T // tm),
        in_specs=in_specs,
        out_specs=out_specs,
        out_shape=out_shape,
        compiler_params=pltpu.CompilerParams(
            dimension_semantics=("arbitrary", "arbitrary"), vmem_limit_bytes=VMEM_LIMIT),
        name="in_proj",
    )(*args)


def rope_lane_tables(n_tokens):
    t = jnp.arange(n_tokens)
    row = (t // GRID_W).astype(jnp.float32)
    col = (t % GRID_W).astype(jnp.float32)
    half = M_DQK // 2
    inv = ROPE_BASE ** (-jnp.arange(0, half, 2, dtype=jnp.float32) / half)
    ar = row[:, None] * inv[None, :]
    ac = col[:, None] * inv[None, :]
    cos = jnp.concatenate([jnp.cos(ar), jnp.cos(ar), jnp.cos(ac), jnp.cos(ac)], axis=-1)
    sin = jnp.concatenate([-jnp.sin(ar), jnp.sin(ar), -jnp.sin(ac), jnp.sin(ac)], axis=-1)
    return jnp.tile(cos, (1, LANES // M_DQK)), jnp.tile(sin, (1, LANES // M_DQK))


def permute_w_in(w_in_l):
    qm, km, vm, om, gm, pu, qn, kn, vn, bg = jnp.split(w_in_l, [int(s) for s in np.cumsum(SPLIT_SIZES)[:-1]], axis=-1)
    gpad = jnp.pad(gm, ((0, 0), (0, GATE_W - gm.shape[1])))
    w = jnp.concatenate([qm * (M_DQK ** -0.5), km, vm, om, pu, qn * (NA_DH ** -0.5), kn, vn, bg, gpad], axis=-1)
    return w.astype(jnp.bfloat16)


def _pair_scores(qp, k_parts, biases):
    n = qp.shape[0]
    lane_lo = lax.broadcasted_iota(jnp.int32, (n, LANES), 1) < NA_DH
    zero = jnp.zeros_like(qp)
    qq = jnp.concatenate([jnp.where(lane_lo, qp, zero), jnp.where(lane_lo, zero, qp)], axis=0)
    scores = []
    for kp, bias in zip(k_parts, biases):
        s = lax.dot_general(qq, kp, (((1,), (1,)), ((), ())), preferred_element_type=jnp.float32)
        scores.append(s if bias is None else s + bias)
    return scores


def _pair_softmax_pv(scores, v_parts):
    n = scores[0].shape[0] // 2
    lane_lo = lax.broadcasted_iota(jnp.int32, (n, LANES), 1) < NA_DH
    m = scores[0].max(axis=-1, keepdims=True)
    for s in scores[1:]:
        m = jnp.maximum(m, s.max(axis=-1, keepdims=True))
    l = None
    o = None
    for s, vp in zip(scores, v_parts):
        p = jnp.exp(s - m)
        ls = p.sum(axis=-1, keepdims=True)
        os_ = jnp.dot(p.astype(jnp.bfloat16), vp, preferred_element_type=jnp.float32)
        l = ls if l is None else l + ls
        o = os_ if o is None else o + os_
    o = o * (1.0 / l)
    return jnp.where(lane_lo, o[:n], o[n:])


def _na_kernel(q_ref, k_ref, v_ref, kc_ref, vc_ref, bias_ref, *rest, need_ctx):
    if need_ctx:
        qc_ref, o_ref, oc_ref, s_ref = rest
    else:
        o_ref, s_ref = rest
    S = q_ref.shape[1]
    Tc = kc_ref.shape[1]
    rows = S // GRID_W
    n_pairs = BRANCH_W // LANES
    n_loc = NA_KH * GRID_W

    def window(r):
        rs = jnp.clip(r - NA_KH // 2, 0, rows - NA_KH)
        return r - rs, pl.multiple_of(r * GRID_W, GRID_W), pl.multiple_of(rs * GRID_W, GRID_W)

    def scores_stage(r, slot):
        var, q0, k0 = window(r)
        for j in range(n_pairs):
            ls = slice(j * LANES, (j + 1) * LANES)
            s_loc, s_ctx = _pair_scores(q_ref[0, pl.ds(q0, GRID_W), ls],
                                        (k_ref[0, pl.ds(k0, n_loc), ls], kc_ref[0, :, ls]),
                                        (bias_ref[var, j], None))
            s_ref[slot, j, :, :n_loc] = s_loc
            s_ref[slot, j, :, n_loc:] = s_ctx

    def output_stage(r, slot):
        _, q0, k0 = window(r)
        outs = []
        for j in range(n_pairs):
            ls = slice(j * LANES, (j + 1) * LANES)
            outs.append(_pair_softmax_pv((s_ref[slot, j, :, :n_loc], s_ref[slot, j, :, n_loc:]),
                                         (v_ref[0, pl.ds(k0, n_loc), ls], vc_ref[0, :, ls])))
        o_ref[0, pl.ds(q0, GRID_W), :] = jnp.concatenate(outs, axis=-1).astype(o_ref.dtype)

    scores_stage(0, 0)

    def two_rows(i, carry):
        r0 = 2 * i
        scores_stage(r0 + 1, 1)
        output_stage(r0, 0)
        scores_stage(jnp.minimum(r0 + 2, rows - 1), 0)
        output_stage(r0 + 1, 1)
        return carry

    lax.fori_loop(0, rows // 2, two_rows, 0)

    if need_ctx:
        outs = []
        for j in range(n_pairs):
            ls = slice(j * LANES, (j + 1) * LANES)
            scores = _pair_scores(qc_ref[0, :, ls], (kc_ref[0, :, ls],), (None,))
            outs.append(_pair_softmax_pv(scores, (vc_ref[0, :, ls],)))
        oc_ref[0] = jnp.concatenate(outs, axis=-1).astype(oc_ref.dtype)


def na_bias_table(rpb):
    H = rpb.shape[0]
    var = jnp.arange(NA_KH)
    kr = jnp.arange(NA_KH)
    dr = kr[None, :] - var[:, None] + NA_KH - 1
    cols = jnp.arange(GRID_W)
    dc = jnp.clip(cols[None, :] - cols[:, None] + NA_KW - 1, 0, 2 * NA_KW - 2)
    cs = jnp.clip(cols - NA_KW // 2, 0, GRID_W - NA_KW)
    colmask = (cols[None, :] >= cs[:, None]) & (cols[None, :] < cs[:, None] + NA_KW)
    tab = rpb[:, dr][:, :, :, dc]
    tab = jnp.where(colmask[None, None, None], tab.astype(jnp.float32), MASK_NEG)
    tab = tab.transpose(1, 0, 3, 2, 4)
    return tab.reshape(NA_KH, H // 2, 2 * GRID_W, NA_KH * GRID_W)


def neighbourhood_attention(qn, kn, vn, kc, vc, bias_tab, qc=None):
    B, S, W = qn.shape
    Tc = kc.shape[1]
    need_ctx = qc is not None
    lat = pl.BlockSpec((1, S, W), lambda b: (b, 0, 0))
    cx = pl.BlockSpec((1, Tc, W), lambda b: (b, 0, 0))
    in_specs = [lat, lat, lat, cx, cx, _const_spec(bias_tab.shape)]
    args = [qn, kn, vn, kc, vc, bias_tab]
    out_shape = [jax.ShapeDtypeStruct((B, S, W), jnp.bfloat16)]
    out_specs = [lat]
    if need_ctx:
        in_specs.append(cx)
        args.append(qc)
        out_shape.append(jax.ShapeDtypeStruct((B, Tc, W), jnp.bfloat16))
        out_specs.append(cx)
    res = pl.pallas_call(
        functools.partial(_na_kernel, need_ctx=need_ctx),
        grid=(B,),
        in_specs=in_specs,
        out_specs=out_specs,
        out_shape=out_shape,
        scratch_shapes=[pltpu.VMEM((2, W // LANES, 2 * GRID_W, NA_KH * GRID_W + Tc), jnp.float32)],
        compiler_params=pltpu.CompilerParams(dimension_semantics=("arbitrary",), vmem_limit_bytes=VMEM_LIMIT),
        name="neighbourhood_attention",
    )(*args)
    return (res[0], res[1]) if need_ctx else (res[0], None)


def _split_bf16(x):
    hi = x.astype(jnp.bfloat16)
    r1 = x - hi.astype(jnp.float32)
    mid = r1.astype(jnp.bfloat16)
    lo = (r1 - mid.astype(jnp.float32)).astype(jnp.bfloat16)
    return hi, mid, lo


def _dot_f32(a, b):
    return jnp.dot(a, b, preferred_element_type=jnp.float32)


def _log_sigmoid(x):
    return jnp.minimum(x, 0.0) - jnp.log(1.0 + jnp.exp(-jnp.abs(x)))


def _mlstm_chunk(qk_ref, v_ref, g_ref, gt_ref, h_ref, ct_ref, m_ref, ci, fwd):
    L = M_CHUNK
    f32, bf16 = jnp.float32, jnp.bfloat16
    r0 = pl.multiple_of(ci * L, L)
    rows = pl.ds(r0, L)
    sq_r = lax.broadcasted_iota(jnp.int32, (L, L), 0)
    sq_c = lax.broadcasted_iota(jnp.int32, (L, L), 1)
    tri_l = jnp.where(sq_r >= sq_c, 1.0, 0.0).astype(bf16)
    tri_u = jnp.where(sq_r <= sq_c, 1.0, 0.0).astype(bf16)
    gates = g_ref[0, rows, :]
    gates_t = gt_ref[0, ci]
    lf, lf_t = _log_sigmoid(gates), _log_sigmoid(gates_t)
    cum = sum(_dot_f32(tri_l if fwd else tri_u, p) for p in _split_bf16(lf))
    cum_t = sum(_dot_f32(p, tri_u if fwd else tri_l) for p in _split_bf16(lf_t))
    t_i = 0 if fwd else 2
    last = L - 1 if fwd else 0
    st_r = lax.broadcasted_iota(jnp.int32, (2 * L, L), 0) & (L - 1)
    st_c = lax.broadcasted_iota(jnp.int32, (2 * L, L), 1)
    valid = (st_r >= st_c) if fwd else (st_r <= st_c)
    lane_lo = lax.broadcasted_iota(jnp.int32, (L, LANES), 1) < M_DQK
    row_lo = lax.broadcasted_iota(jnp.int32, (LANES, L), 0) < M_DQK
    ones = jnp.ones((L, LANES), bf16)

    def col2(tile, c0, c1):
        return jnp.concatenate([jnp.broadcast_to(tile[:, c0:c0 + 1], (L, LANES)),
                                jnp.broadcast_to(tile[:, c1:c1 + 1], (L, LANES))], axis=0)

    def row2(tile, c0, c1):
        return jnp.concatenate([jnp.broadcast_to(tile[c0:c0 + 1, :], (L, L)),
                                jnp.broadcast_to(tile[c1:c1 + 1, :], (L, L))], axis=0)

    for j in range(M_HEADS // 2):
        h0, h1 = 2 * j, 2 * j + 1
        sidx = 2 * j + (0 if fwd else 1)
        ci0, ci1 = t_i * M_HEADS + h0, t_i * M_HEADS + h1
        cf0, cf1 = ci0 + M_HEADS, ci1 + M_HEADS
        i_col, b_col = col2(gates, ci0, ci1), col2(cum, cf0, cf1)
        dm = jnp.where(valid, b_col - row2(cum_t, cf0, cf1) + row2(gates_t, ci0, ci1), MASK_NEG)
        m_col = m_ref[sidx]
        qp = qk_ref[0, rows, j * LANES:(j + 1) * LANES]
        kp = qk_ref[0, rows, M_HEADS * M_DQK + j * LANES:M_HEADS * M_DQK + (j + 1) * LANES]
        vx0 = jnp.concatenate([v_ref[0, rows, h0 * M_DV:(h0 + 1) * M_DV], ones], axis=-1)
        vx1 = jnp.concatenate([v_ref[0, rows, h1 * M_DV:(h1 + 1) * M_DV], ones], axis=-1)
        ct = ct_ref[sidx]
        if h_ref is not None:
            zq = jnp.zeros_like(qp)
            qq = jnp.concatenate([jnp.where(lane_lo, qp, zq), jnp.where(lane_lo, zq, qp)], axis=0)
            g_col = b_col + m_col
            mt = jnp.maximum(g_col, jnp.broadcast_to(dm.max(axis=-1, keepdims=True), (2 * L, LANES)))
            s = lax.dot_general(qq, kp, (((1,), (1,)), ((), ())), preferred_element_type=f32)
            sc = (s * jnp.exp(dm - mt)).astype(bf16)
            inter = jnp.exp(g_col - mt)
            t1 = _dot_f32(qq, ct.astype(bf16))
            t2 = jnp.concatenate([_dot_f32(sc[:L], vx0), _dot_f32(sc[L:], vx1)], axis=0)
            num = inter * t1[:, :M_DV] + t2[:, :M_DV]
            den = inter * t1[:, M_DV:] + t2[:, M_DV:]
            h = num / jnp.maximum(jnp.abs(den), jnp.exp(-mt))
            h_ref[rows, h0 * M_DV:(h0 + 1) * M_DV] = h[:L]
            h_ref[rows, h1 * M_DV:(h1 + 1) * M_DV] = h[L:]
        bl = jnp.concatenate([jnp.broadcast_to(b_col[last:last + 1], (L, LANES)),
                              jnp.broadcast_to(b_col[L + last:L + last + 1], (L, LANES))], axis=0)
        w = bl - b_col + i_col
        wm = jnp.concatenate([jnp.broadcast_to(w[:L].max(axis=0, keepdims=True), (L, LANES)),
                              jnp.broadcast_to(w[L:].max(axis=0, keepdims=True), (L, LANES))], axis=0)
        m_new = jnp.maximum(bl + m_col, wm)
        decay = jnp.exp(bl + m_col - m_new)
        ws = jnp.exp(w - m_new)
        kw_t = (kp.astype(f32) * jnp.where(lane_lo, ws[:L], ws[L:])).T
        zk = jnp.zeros_like(kw_t)
        upd = (_dot_f32(jnp.where(row_lo, kw_t, zk).astype(bf16), vx0)
               + _dot_f32(jnp.where(row_lo, zk, kw_t).astype(bf16), vx1))
        d_rows = jnp.concatenate([decay[:M_DQK], decay[L:L + M_DQK]], axis=0)
        ct_ref[sidx] = jnp.concatenate([d_rows, d_rows], axis=-1) * ct + upd
        m_ref[sidx] = m_new


def _mlstm_finish(hf_ref, hb_ref, o_ref, out_ref, n_chunks):
    L = M_CHUNK

    def body(ci, carry):
        rows = pl.ds(pl.multiple_of(ci * L, L), L)
        h = hf_ref[rows, :] + hb_ref[rows, :]
        parts = []
        for hd in range(M_HEADS):
            hh = h[:, hd * M_DV:(hd + 1) * M_DV]
            parts.append(hh * lax.rsqrt(jnp.mean(hh * hh, axis=-1, keepdims=True) + RMS_EPS))
        gate = jax.nn.sigmoid(o_ref[0, rows, :].astype(jnp.float32))
        out_ref[0, rows, :] = (gate * jnp.concatenate(parts, axis=-1)).astype(out_ref.dtype)
        return carry

    lax.fori_loop(0, n_chunks, body, 0)


def _mlstm_kernel(qk_c, v_c, o_c, g_c, gt_c, qk_l, v_l, o_l, g_l, gt_l, *rest, need_ctx):
    if need_ctx:
        out_l, out_c, hf_l, hb_l, hf_c, hb_c, ct_ref, m_ref = rest
    else:
        out_l, hf_l, hb_l, ct_ref, m_ref = rest
        hf_c = hb_c = None
    nc, nl = qk_c.shape[1] // M_CHUNK, qk_l.shape[1] // M_CHUNK
    ct_ref[...] = jnp.zeros_like(ct_ref)
    m_ref[...] = jnp.full_like(m_ref, M_INIT)

    def phase(qk, v, g, gt, hf, hb, n):
        def body(i, carry):
            _mlstm_chunk(qk, v, g, gt, hf, ct_ref, m_ref, i, True)
            _mlstm_chunk(qk, v, g, gt, hb, ct_ref, m_ref, n - 1 - i, False)
            return carry
        lax.fori_loop(0, n, body, 0)

    phase(qk_c, v_c, g_c, gt_c, hf_c, hb_c, nc)
    phase(qk_l, v_l, g_l, gt_l, hf_l, hb_l, nl)
    _mlstm_finish(hf_l, hb_l, o_l, out_l, nl)
    if need_ctx:
        _mlstm_finish(hf_c, hb_c, o_c, out_c, nc)


def _gates_chunk_major(g):
    B, T, _ = g.shape
    return g[..., :4 * M_HEADS].reshape(B, T // M_CHUNK, M_CHUNK, 4 * M_HEADS).transpose(0, 1, 3, 2)


def mlstm(pc, pl_, need_ctx):
    B, S, W = pl_["mv"].shape
    Tc = pc["mv"].shape[1]
    f32 = jnp.float32

    def specs(T):
        s = pl.BlockSpec((1, T, W), lambda b: (b, 0, 0))
        return [s, s, s, pl.BlockSpec((1, T, GATE_W), lambda b: (b, 0, 0)),
                pl.BlockSpec((1, T // M_CHUNK, 4 * M_HEADS, M_CHUNK), lambda b: (b, 0, 0, 0))]

    def args(p):
        return [p["mqk"], p["mv"], p["mo"], p["g"], _gates_chunk_major(p["g"])]

    out_shape = [jax.ShapeDtypeStruct((B, S, W), jnp.bfloat16)]
    out_specs = [pl.BlockSpec((1, S, W), lambda b: (b, 0, 0))]
    scratch = [pltpu.VMEM((S, W), f32), pltpu.VMEM((S, W), f32)]
    if need_ctx:
        out_shape.append(jax.ShapeDtypeStruct((B, Tc, W), jnp.bfloat16))
        out_specs.append(pl.BlockSpec((1, Tc, W), lambda b: (b, 0, 0)))
        scratch += [pltpu.VMEM((Tc, W), f32), pltpu.VMEM((Tc, W), f32)]
    scratch += [pltpu.VMEM((M_HEADS, LANES, 2 * M_DV), f32), pltpu.VMEM((M_HEADS, 2 * M_CHUNK, LANES), f32)]
    res = pl.pallas_call(
        functools.partial(_mlstm_kernel, need_ctx=need_ctx),
        grid=(B,),
        in_specs=specs(Tc) + specs(S),
        out_specs=out_specs,
        out_shape=out_shape,
        scratch_shapes=scratch,
        compiler_params=pltpu.CompilerParams(dimension_semantics=("arbitrary",), vmem_limit_bytes=VMEM_LIMIT),
        name="mlstm",
    )(*args(pc), *args(pl_))
    return (res[0], res[1]) if need_ctx else (res[0], None)


POOL_HALO = 16


def _pool_mix(pu_ref, prev_ref, next_ref, pw_ref, ps_ref, tile_idx, n_tiles, seq_len):
    f32 = jnp.float32
    tm = pu_ref.shape[1]
    cur = pu_ref[0].astype(f32)
    prev = jnp.where(tile_idx > 0, prev_ref[0].astype(f32), 0.0)
    nxt = jnp.where(tile_idx < n_tiles - 1, next_ref[0].astype(f32), 0.0)
    ext = jnp.concatenate([prev, cur, nxt], axis=0)
    n_ext = tm + 2 * POOL_HALO
    tok = tile_idx * tm + lax.broadcasted_iota(jnp.int32, (tm, POOL_GC), 0)

    def shifted(a, d):
        return pltpu.roll(a, (-d) % n_ext, 0)

    outs = []
    for gi, w in enumerate(POOL_WINDOWS):
        s = ext[:, gi * POOL_GC:(gi + 1) * POOL_GC]
        s = shifted(s, -1) + s
        span = 2
        while span < w:
            s = shifted(s, -(span // 2)) + shifted(s, span // 2)
            span *= 2
        lo = jnp.clip(tok - w // 2, 0, seq_len - 1)
        hi = jnp.clip(tok - w // 2 + w - 1, 0, seq_len - 1)
        mean = s[POOL_HALO:POOL_HALO + tm] / (hi - lo + 1).astype(f32)
        d = (mean - cur[:, gi * POOL_GC:(gi + 1) * POOL_GC]).astype(jnp.bfloat16)
        outs.append(_dot_f32(d, pw_ref[gi]))
    return jnp.concatenate(outs, axis=-1) * ps_ref[...]


def _pack_bf16_pairs(h):
    n = h.shape[1] // 2
    hb = h.astype(jnp.bfloat16).astype(jnp.float32)
    lo = pltpu.bitcast(hb[:, :n], jnp.uint32) >> 16
    hi = pltpu.bitcast(hb[:, n:], jnp.uint32) & jnp.uint32(0xFFFF0000)
    return lo | hi


def _merge_kernel(m_ref, n_ref, pu_ref, prev_ref, next_ref, bg_ref, x_ref, mods_ref, gains_ref,
                  pw_ref, ps_ref, wb_ref, wo_ref, wrt_ref, wrp_ref, x1_ref, h2a_ref, aff_ref, *, seq_len):
    f32, bf16 = jnp.float32, jnp.bfloat16
    i = pl.program_id(1)
    D = x_ref.shape[2]
    E = wrt_ref.shape[0]
    p = _pool_mix(pu_ref, prev_ref, next_ref, pw_ref, ps_ref, i, pl.num_programs(1), seq_len)
    branches = (m_ref[0], p.astype(bf16), n_ref[0])
    acc = None
    for bi, br in enumerate(branches):
        gate = jax.nn.sigmoid(bg_ref[0, :, bi * D:(bi + 1) * D].astype(f32))
        term = gate * _dot_f32(br, wb_ref[bi])
        acc = term if acc is None else acc + term
    y = _dot_f32(acc.astype(bf16), wo_ref[...])
    yn = y * lax.rsqrt(jnp.mean(y * y, axis=-1, keepdims=True) + RMS_EPS) * gains_ref[0:1, :]
    x1 = x_ref[0] + mods_ref[0, 0:1, :] * yn
    x1_ref[0] = x1
    xn = x1 * lax.rsqrt(jnp.mean(x1 * x1, axis=-1, keepdims=True) + RMS_EPS) * gains_ref[1:2, :]
    h2 = xn * (1.0 + mods_ref[0, 2:3, :]) + mods_ref[0, 1:2, :]
    h2b = h2.astype(bf16)
    lg_t = lax.dot_general(wrt_ref[...], h2b, (((1,), (1,)), ((), ())), preferred_element_type=f32)
    e_t = jnp.exp(lg_t - lg_t.max(axis=0, keepdims=True))
    aff_ref[0] = e_t / e_t.sum(axis=0, keepdims=True)
    lg = _dot_f32(h2b, wrp_ref[...])
    lg = jnp.where(lax.broadcasted_iota(jnp.int32, lg.shape, 1) < E, lg, MASK_NEG)
    e_r = jnp.exp(lg - lg.max(axis=-1, keepdims=True))
    aff_rows = e_r / e_r.sum(axis=-1, keepdims=True)
    h2a_ref[0] = jnp.concatenate([_pack_bf16_pairs(h2), pltpu.bitcast(aff_rows, jnp.uint32)], axis=-1)


def merge_out(m, n, pu, bg, x, mods, gains, pool_w, pool_scale, w_branch, w_out, w_router_t, w_router_pad, *, tm):
    B, T, D = x.shape
    W = m.shape[2]
    E = w_router_t.shape[0]
    hb = tm // POOL_HALO
    n_halo = T // POOL_HALO
    tile = lambda w: pl.BlockSpec((1, tm, w), lambda b, i: (b, i, 0))
    in_specs = [
        tile(W), tile(W), tile(W),
        pl.BlockSpec((1, POOL_HALO, W), lambda b, i: (b, jnp.maximum(i * hb - 1, 0), 0)),
        pl.BlockSpec((1, POOL_HALO, W), lambda b, i: (b, jnp.minimum((i + 1) * hb, n_halo - 1), 0)),
        tile(N_BRANCH * D), tile(D),
        pl.BlockSpec((1, 8, D), lambda b, i: (b, 0, 0)),
        _const_spec((8, D)),
        _const_spec(pool_w.shape), _const_spec(pool_scale.shape), _const_spec(w_branch.shape),
        _const_spec(w_out.shape), _const_spec(w_router_t.shape), _const_spec(w_router_pad.shape),
    ]
    return pl.pallas_call(
        functools.partial(_merge_kernel, seq_len=T),
        grid=(B, T // tm),
        in_specs=in_specs,
        out_specs=[tile(D), tile(MOE_ROW), pl.BlockSpec((1, E, tm), lambda b, i: (b, 0, i))],
        out_shape=[jax.ShapeDtypeStruct((B, T, D), jnp.float32),
                   jax.ShapeDtypeStruct((B, T, MOE_ROW), jnp.uint32),
                   jax.ShapeDtypeStruct((B, E, T), jnp.float32)],
        compiler_params=pltpu.CompilerParams(
            dimension_semantics=("arbitrary", "arbitrary"), vmem_limit_bytes=VMEM_LIMIT),
        name="merge_out",
    )(m, n, pu, pu, pu, bg, x, mods, gains, pool_w, pool_scale, w_branch, w_out, w_router_t, w_router_pad)


def _cumsum_lanes(x01):
    R, T = x01.shape
    r = lax.broadcasted_iota(jnp.int32, (LANES, LANES), 0)
    c = lax.broadcasted_iota(jnp.int32, (LANES, LANES), 1)
    tri_u = jnp.where(r <= c, 1.0, 0.0).astype(jnp.bfloat16)
    run = jnp.zeros((R, 1), jnp.float32)
    outs = []
    for t0 in range(0, T, LANES):
        cs = _dot_f32(x01[:, t0:t0 + LANES].astype(jnp.bfloat16), tri_u) + run
        run = cs[:, LANES - 1:LANES]
        outs.append(cs)
    return jnp.concatenate(outs, axis=-1)


def _route_kernel(aff_ref, idx_ref, *, cap):
    f32 = jnp.float32
    aff = aff_ref[0]
    E, T = aff.shape
    bits = pltpu.bitcast(aff, jnp.int32)

    def search(k, lo):
        cand = lo | (jnp.int32(1) << (30 - k))
        cnt = jnp.sum(jnp.where(bits >= cand, 1.0, 0.0), axis=-1, keepdims=True)
        return jnp.where(cnt >= cap, cand, lo)

    thr = lax.fori_loop(0, 31, search, jnp.zeros((E, 1), jnp.int32))
    gt = jnp.where(bits > thr, 1.0, 0.0)
    eq = jnp.where(bits == thr, 1.0, 0.0)
    room = cap - jnp.sum(gt, axis=-1, keepdims=True)
    sel = gt + eq * jnp.where(_cumsum_lanes(eq) <= room, 1.0, 0.0)
    csel = _cumsum_lanes(sel)
    slot = lax.broadcasted_iota(jnp.int32, (cap, T), 0).astype(f32)
    lane = lax.broadcasted_iota(jnp.int32, (cap, LANES), 1)
    out = jnp.zeros((cap, LANES), f32)
    for e in range(E):
        pos = jnp.sum(jnp.where(jnp.broadcast_to(csel[e:e + 1, :], (cap, T)) <= slot, 1.0, 0.0),
                      axis=-1, keepdims=True)
        out = jnp.where(lane == e, jnp.broadcast_to(pos, (cap, LANES)), out)
    idx_ref[0] = out.astype(jnp.int32)


def route(aff_t, cap):
    B, E, T = aff_t.shape
    idx_t = pl.pallas_call(
        functools.partial(_route_kernel, cap=cap),
        grid=(B,),
        in_specs=[pl.BlockSpec((1, E, T), lambda b: (b, 0, 0))],
        out_specs=pl.BlockSpec((1, cap, LANES), lambda b: (b, 0, 0)),
        out_shape=jax.ShapeDtypeStruct((B, cap, LANES), jnp.int32),
        compiler_params=pltpu.CompilerParams(dimension_semantics=("arbitrary",), vmem_limit_bytes=VMEM_LIMIT),
        name="route",
    )(aff_t)
    return idx_t[:, :, :E].transpose(0, 2, 1)


ROW_GROUP = 8


def _moe_kernel(idx_ref, rows_ref, wg_ref, wu_ref, wd_ref, out_ref, xg0_ref, xg1_ref, ye0_ref, ye1_ref):
    f32, bf16 = jnp.float32, jnp.bfloat16
    e = pl.program_id(1)
    n_exp = pl.num_programs(1)
    cap = xg0_ref.shape[0]
    half = wg_ref.shape[1] // 2

    def gather_row(dst_ref, ee, j):
        dst_ref[pl.ds(j, 1), :] = rows_ref[0, pl.ds(idx_ref[0, 0, ee, j], 1), :]

    def scatter_group(src_ref, ee, j0):
        ids = [idx_ref[0, 0, ee, j0 + r] for r in range(ROW_GROUP)]
        cur = [out_ref[0, pl.ds(i, 1), :] for i in ids]
        for r, i in enumerate(ids):
            out_ref[0, pl.ds(i, 1), :] = cur[r] + src_ref[pl.ds(j0 + r, 1), :]

    @pl.when(e == 0)
    def _():
        out_ref[...] = jnp.zeros_like(out_ref)
        ye1_ref[...] = jnp.zeros_like(ye1_ref)

        def body(gi, carry):
            for r in range(ROW_GROUP):
                gather_row(xg0_ref, 0, gi * ROW_GROUP + r)
            return carry
        lax.fori_loop(0, cap // ROW_GROUP, body, 0)

    def step(xg_cur, xg_nxt, ye_cur, ye_prv):
        e_prv = jnp.maximum(e - 1, 0)
        e_nxt = jnp.minimum(e + 1, n_exp - 1)
        for j0 in range(0, cap, ROW_GROUP):
            scatter_group(ye_prv, e_prv, j0)
        for j in range(cap):
            gather_row(xg_nxt, e_nxt, j)
        xg = xg_cur[...]
        packed = xg[:, :half]
        xe = jnp.concatenate([pltpu.bitcast(packed << 16, f32),
                              pltpu.bitcast(packed & jnp.uint32(0xFFFF0000), f32)], axis=-1).astype(bf16)
        aff = pltpu.bitcast(xg[:, half:], f32)
        gate = jnp.sum(jnp.where(lax.broadcasted_iota(jnp.int32, aff.shape, 1) == e, aff, 0.0),
                       axis=-1, keepdims=True)
        a = _dot_f32(xe, wg_ref[0])
        hid = (a * jax.nn.sigmoid(a) * _dot_f32(xe, wu_ref[0])).astype(bf16)
        ye_cur[...] = _dot_f32(hid, wd_ref[0]) * gate

    @pl.when(e % 2 == 0)
    def _():
        step(xg0_ref, xg1_ref, ye0_ref, ye1_ref)

    @pl.when(e % 2 == 1)
    def _():
        step(xg1_ref, xg0_ref, ye1_ref, ye0_ref)

    @pl.when(e == n_exp - 1)
    def _():
        def body(gi, carry):
            scatter_group(ye1_ref, e, gi * ROW_GROUP)
            return carry
        lax.fori_loop(0, cap // ROW_GROUP, body, 0)


def moe_experts(rows, idx, w_gate, w_up, w_down):
    G, T, RW = rows.shape
    _, E, cap = idx.shape
    D = w_gate.shape[1]
    assert E % 2 == 0 and cap % ROW_GROUP == 0
    wspec = pl.BlockSpec((1, D, D), lambda g, e: (e, 0, 0))
    return pl.pallas_call(
        _moe_kernel,
        grid=(G, E),
        in_specs=[pl.BlockSpec((1, 1, E, cap), lambda g, e: (g, 0, 0, 0), memory_space=pltpu.SMEM),
                  pl.BlockSpec((1, T, RW), lambda g, e: (g, 0, 0)), wspec, wspec, wspec],
        out_specs=pl.BlockSpec((1, T, D), lambda g, e: (g, 0, 0)),
        out_shape=jax.ShapeDtypeStruct((G, T, D), jnp.float32),
        scratch_shapes=[pltpu.VMEM((cap, RW), jnp.uint32), pltpu.VMEM((cap, RW), jnp.uint32),
                        pltpu.VMEM((cap, D), jnp.float32), pltpu.VMEM((cap, D), jnp.float32)],
        compiler_params=pltpu.CompilerParams(
            dimension_semantics=("arbitrary", "arbitrary"), vmem_limit_bytes=VMEM_LIMIT),
        name="moe_experts",
    )(idx.reshape(G, 1, E, cap), rows, w_gate, w_up, w_down)


def _residual_norm_kernel(x_ref, y_ref, gate_ref, gain_ref, o_ref):
    y = y_ref[0]
    yn = y * lax.rsqrt(jnp.mean(y * y, axis=-1, keepdims=True) + RMS_EPS) * gain_ref[...]
    o_ref[0] = x_ref[0] + gate_ref[0] * yn


def residual_norm(x, y, gate, gain, *, tm):
    B, T, D = x.shape
    tile = pl.BlockSpec((1, tm, D), lambda b, i: (b, i, 0))
    return pl.pallas_call(
        _residual_norm_kernel,
        grid=(B, T // tm),
        in_specs=[tile, tile, pl.BlockSpec((1, 1, D), lambda b, i: (b, 0, 0)), _const_spec((1, D))],
        out_specs=tile,
        out_shape=jax.ShapeDtypeStruct((B, T, D), jnp.float32),
        compiler_params=pltpu.CompilerParams(
            dimension_semantics=("arbitrary", "arbitrary"), vmem_limit_bytes=VMEM_LIMIT),
        name="residual_norm",
    )(x, y, gate, gain)


def rmsnorm(x, g):
    xf = x.astype(jnp.float32)
    y = xf * lax.rsqrt(jnp.mean(xf * xf, axis=-1, keepdims=True) + RMS_EPS)
    return (y * g.astype(jnp.float32)).astype(x.dtype)


def modulate(h, shift, scale):
    return h * (1 + scale) + shift


def rope_tables(n_tokens, dtype):
    t = jnp.arange(n_tokens)
    row = (t // GRID_W).astype(jnp.float32)
    col = (t % GRID_W).astype(jnp.float32)
    half = M_DQK // 2
    inv = ROPE_BASE ** (-jnp.arange(0, half, 2, dtype=jnp.float32) / half)
    ar = row[:, None] * inv[None, :]
    ac = col[:, None] * inv[None, :]
    return tuple(a.astype(dtype) for a in (jnp.cos(ar), jnp.sin(ar), jnp.cos(ac), jnp.sin(ac)))


def _rotate(x, cos, sin):
    x1, x2 = jnp.split(x, 2, axis=-1)
    cos = cos[:, None, :]
    sin = sin[:, None, :]
    return jnp.concatenate([x1 * cos - x2 * sin, x2 * cos + x1 * sin], axis=-1)


def rope_2d(x, tabs):
    cr, sr, cc, sc = tabs
    xr, xc = jnp.split(x, 2, axis=-1)
    return jnp.concatenate([_rotate(xr, cr, sr), _rotate(xc, cc, sc)], axis=-1)


def mlstm_scan(q, k, v, li, lf, state):
    B, T, H, _ = q.shape
    dv = v.shape[-1]
    L = M_CHUNK
    N = T // L

    def chunks(a):
        a = a.reshape((B, N, L, H) + a.shape[3:])
        return jnp.moveaxis(a, (1, 3), (0, 2))

    tril = jnp.tril(jnp.ones((L, L), dtype=bool))

    def step(carry, inp):
        C, n, m = carry
        qc, kc, vc, ic, fc = inp
        b = jnp.cumsum(fc, axis=-1)
        dmat = jnp.where(tril, b[..., :, None] - b[..., None, :] + ic[..., None, :], -jnp.inf)
        g = b + m[..., None]
        mt = jnp.maximum(g, dmat.max(-1))
        sc = jnp.einsum('bhtd,bhsd->bhts', qc, kc) * jnp.exp(dmat - mt[..., None])
        inter = jnp.exp(g - mt)
        num = inter[..., None] * jnp.einsum('bhvd,bhtd->bhtv', C, qc) + jnp.einsum('bhts,bhsv->bhtv', sc, vc)
        den = inter * jnp.einsum('bhd,bhtd->bht', n, qc) + sc.sum(-1)
        h = num / jnp.maximum(jnp.abs(den), jnp.exp(-mt))[..., None]
        bl = b[..., -1]
        w = bl[..., None] - b + ic
        m_new = jnp.maximum(bl + m, w.max(-1))
        decay = jnp.exp(bl + m - m_new)
        ws = jnp.exp(w - m_new[..., None])
        C_new = decay[..., None, None] * C + jnp.einsum('bhs,bhsv,bhsd->bhvd', ws, vc, kc)
        n_new = decay[..., None] * n + jnp.einsum('bhs,bhsd->bhd', ws, kc)
        return (C_new, n_new, m_new), h

    xs = (chunks(q), chunks(k), chunks(v), chunks(li), chunks(lf))
    state, h = lax.scan(step, state, xs)
    h = jnp.moveaxis(h, (0, 2), (1, 3)).reshape(B, T, H, dv)
    return state, h


def head_rms(h):
    return h * lax.rsqrt(jnp.mean(h * h, axis=-1, keepdims=True) + RMS_EPS)


def mlstm_branch(ctx_parts, lat_parts, need_ctx):
    f32 = jnp.float32

    def prep(parts):
        q, k, v, o, gt = parts
        q = q.astype(f32)
        k = k.astype(f32)
        v = v.astype(f32)
        fwd = (q, k, v, gt[..., 0, :], jax.nn.log_sigmoid(gt[..., 1, :]))
        bwd = tuple(jnp.flip(a, axis=1) for a in (q, k, v, gt[..., 2, :], jax.nn.log_sigmoid(gt[..., 3, :])))
        return fwd, bwd

    c_f, c_b = prep(ctx_parts)
    l_f, l_b = prep(lat_parts)
    B = l_f[0].shape[0]
    st0 = (jnp.zeros((B, M_HEADS, M_DV, M_DQK), f32), jnp.zeros((B, M_HEADS, M_DQK), f32),
           jnp.full((B, M_HEADS), M_INIT, f32))
    st_f, hc_f = mlstm_scan(*c_f, st0)
    _, hl_f = mlstm_scan(*l_f, st_f)
    st_b, hc_b = mlstm_scan(*c_b, st0)
    _, hl_b = mlstm_scan(*l_b, st_b)

    def finish(h_f, h_b_rev, o):
        h = head_rms(h_f + jnp.flip(h_b_rev, axis=1))
        return jax.nn.sigmoid(o) * h.reshape(o.shape).astype(o.dtype)

    out_l = finish(hl_f, hl_b, lat_parts[3])
    out_c = finish(hc_f, hc_b, ctx_parts[3]) if need_ctx else None
    return out_l, out_c


def pool_branch(u, pool_w, pool_scale):
    B, T, _ = u.shape
    uf = u.astype(jnp.float32).reshape(B, T, POOL_GROUPS, POOL_GC)
    csum = jnp.concatenate([jnp.zeros((B, 1, POOL_GROUPS, POOL_GC), jnp.float32),
                            jnp.cumsum(uf, axis=1)], axis=1)
    t = jnp.arange(T)
    outs = []
    for gi, w in enumerate(POOL_WINDOWS):
        lo = jnp.clip(t - w // 2, 0, T - 1)
        hi = jnp.clip(t - w // 2 + w - 1, 0, T - 1)
        cnt = (hi - lo + 1).astype(jnp.float32)[:, None]
        cg = csum[:, :, gi]
        mean = (cg[:, hi + 1] - cg[:, lo]) / cnt
        outs.append(jnp.einsum('btc,cd->btd', (mean - uf[:, :, gi]).astype(u.dtype), pool_w[gi]))
    return jnp.concatenate(outs, axis=-1) * pool_scale


def merge_branches(branches, bg, w_branch, w_out):
    D = w_out.shape[0]
    acc = None
    for bi, br in enumerate(branches):
        term = jax.nn.sigmoid(bg[..., bi * D:(bi + 1) * D]) * (br @ w_branch[bi])
        acc = term if acc is None else acc + term
    return acc @ w_out


def ec_moe(h, w_router, w_gate, w_up, w_down):
    B, T, D = h.shape
    cap = EC_FACTOR * T // N_EXPERTS
    aff = jax.nn.softmax(jnp.einsum('btd,de->bte', h, w_router, preferred_element_type=jnp.float32), axis=-1)
    gate, idx = lax.top_k(jnp.swapaxes(aff, 1, 2), cap)
    xe = jax.vmap(lambda hb, ib: hb[ib])(h, idx)
    hid = jax.nn.silu(jnp.einsum('becd,edf->becf', xe, w_gate)) * jnp.einsum('becd,edf->becf', xe, w_up)
    ye = jnp.einsum('becf,efd->becd', hid, w_down) * gate[..., None].astype(h.dtype)
    return jax.vmap(lambda ib, yb: jnp.zeros((T, D), yb.dtype).at[ib.reshape(-1)].add(yb.reshape(-1, D)))(idx, ye)


def _mparts(p, rope_tabs=None):
    B, T, _ = p["mqk"].shape
    f32 = jnp.float32
    hq = M_HEADS * M_DQK
    q = p["mqk"][..., :hq].astype(f32).reshape(B, T, M_HEADS, M_DQK)
    k = p["mqk"][..., hq:].astype(f32).reshape(B, T, M_HEADS, M_DQK)
    if rope_tabs is not None:
        q = rope_2d(q, rope_tabs)
        k = rope_2d(k, rope_tabs)
    v = p["mv"].astype(f32).reshape(B, T, M_HEADS, M_DV)
    return q, k, v, p["mo"].astype(f32), p["g"][..., :4 * M_HEADS].reshape(B, T, 4, M_HEADS)


def kernel(x, c, ctx, c_ctx, norm_gain, ada_w, ada_b, w_in, mlstm_gate_bias, pool_w, pool_scale,
           na_rpb, w_branch, w_out, router_w, w_gate, w_up, w_down):
    B, S, D = x.shape
    Tc = ctx.shape[1]
    f32, bf16 = jnp.float32, jnp.bfloat16
    rope_tabs = rope_lane_tables(S)
    names = [n for n, _ in IN_GROUPS]
    pending = None
    for l in range(DEPTH):
        need_ctx = l < DEPTH - 1
        g = norm_gain[l]
        mod_l = jnp.split((jax.nn.silu(c) @ ada_w[l] + ada_b[l])[:, None, :], N_MOD, axis=-1)
        mod_c = jnp.split(jax.nn.silu(c_ctx) @ ada_w[l] + ada_b[l], N_MOD, axis=-1)
        mod_cb = [jnp.broadcast_to(m[None, None], (B, 1, D)) for m in mod_c]
        w_perm = permute_w_in(w_in[l])
        gb_row = jnp.pad(mlstm_gate_bias[l].reshape(1, -1).astype(f32), ((0, 0), (0, GATE_W - 4 * M_HEADS)))
        proj = in_proj(x, g[0][None], mod_l[0], mod_l[1], w_perm, gb_row, rope_tabs, pending, tm=512)
        if pending is not None:
            x = proj[-1]
        pl_ = dict(zip(names, proj))
        pc_ = dict(zip(names, in_proj(ctx, g[0][None], mod_cb[0], mod_cb[1], w_perm, gb_row, tm=Tc)))

        m_l, m_c = mlstm(pc_, pl_, need_ctx)
        n_l, n_c = neighbourhood_attention(pl_["qn"], pl_["kn"], pl_["vn"], pc_["kn"], pc_["vn"],
                                           na_bias_table(na_rpb[l]), pc_["qn"] if need_ctx else None)
        gains = jnp.pad(g[1:3], ((0, 6), (0, 0)))
        merge_w = (pool_w[l].astype(bf16), pool_scale[l][None], w_branch[l].astype(bf16), w_out[l].astype(bf16),
                   router_w[l].T.astype(bf16),
                   jnp.pad(router_w[l], ((0, 0), (0, LANES - N_EXPERTS))).astype(bf16))
        moe_w = (w_gate[l].astype(bf16), w_up[l].astype(bf16), w_down[l].astype(bf16))

        def mods8(mods):
            return jnp.pad(jnp.concatenate([mods[2], mods[3], mods[4]], axis=1), ((0, 0), (0, 5), (0, 0)))

        x1, rows, aff_t = merge_out(m_l, n_l, pl_["pu"], pl_["bg"], x, mods8(mod_l), gains, *merge_w, tm=512)
        y = moe_experts(rows, route(aff_t, EC_FACTOR * S // N_EXPERTS), *moe_w)
        if need_ctx:
            x, pending = x1, (y, mod_l[5], g[3][None])
        else:
            x = residual_norm(x1, y, mod_l[5], g[3][None], tm=512)
        if need_ctx:
            c1, rows, aff_t = merge_out(m_c, n_c, pc_["pu"], pc_["bg"], ctx, mods8(mod_cb), gains, *merge_w, tm=Tc)
            per = S // Tc
            cap_c = EC_FACTOR * Tc // N_EXPERTS
            idx = route(aff_t, cap_c) + (jnp.arange(B, dtype=jnp.int32) % per * Tc)[:, None, None]
            idx = idx.reshape(B // per, per, N_EXPERTS, cap_c).transpose(0, 2, 1, 3).reshape(B // per, N_EXPERTS, per * cap_c)
            y = moe_experts(rows.reshape(B // per, S, MOE_ROW), idx, *moe_w).reshape(B, Tc, D)
            ctx = residual_norm(c1, y, mod_cb[5], g[3][None], tm=Tc)
    return x
```

```python
import functools
import math

import jax
import jax.numpy as jnp
import numpy as np
from jax import lax
from jax.experimental import pallas as pl
from jax.experimental.pallas import tpu as pltpu

D_MODEL = 1024
DEPTH = 2
GRID_W = 64
BRANCH_W = D_MODEL // 2
N_BRANCH = 3
M_HEADS = 4
M_DV = BRANCH_W // M_HEADS
M_DQK = M_DV // 2
M_CHUNK = 128
M_INIT = -1e30
POOL_GROUPS = 4
POOL_GC = BRANCH_W // POOL_GROUPS
POOL_WINDOWS = (2, 4, 8, 16)
NA_HEADS = 8
NA_DH = BRANCH_W // NA_HEADS
NA_KH = 8
NA_KW = 16
N_EXPERTS = 16
EC_FACTOR = 2
ROPE_BASE = 10000.0
RMS_EPS = 1e-6
N_MOD = 6
SPLIT_SIZES = (M_HEADS * M_DQK, M_HEADS * M_DQK, M_HEADS * M_DV, M_HEADS * M_DV, 4 * M_HEADS,
               POOL_GROUPS * POOL_GC, NA_HEADS * NA_DH, NA_HEADS * NA_DH, NA_HEADS * NA_DH,
               N_BRANCH * D_MODEL)
PROJ_W = sum(SPLIT_SIZES)

LANES = 128
VMEM_LIMIT = 56 * 1024 * 1024
MASK_NEG = -1e30
GATE_W = LANES

IN_GROUPS = (("mqk", 2 * M_HEADS * M_DQK), ("mvt", BRANCH_W), ("mo", BRANCH_W), ("pu", BRANCH_W),
             ("qn", BRANCH_W), ("kn", BRANCH_W), ("vn", BRANCH_W), ("bg", N_BRANCH * D_MODEL),
             ("g", GATE_W))
IN_W = sum(w for _, w in IN_GROUPS)
MM_COLS = 512
MOE_ROW = D_MODEL // 2 + LANES


def _const_spec(shape):
    nd = len(shape)
    return pl.BlockSpec(shape, lambda *_: (0,) * nd, pipeline_mode=pl.Buffered(1))


def _rope_tile(x, cos, sin_signed):
    half = M_DQK // 4
    lane = lax.broadcasted_iota(jnp.int32, x.shape, 1)
    partner = jnp.where((lane % (2 * half)) < half, pltpu.roll(x, LANES - half, 1), pltpu.roll(x, half, 1))
    return x * cos + partner * sin_signed


def _in_proj_kernel(x_ref, g_ref, shift_ref, scale_ref, w_ref, gb_ref, *rest, rope, pending):
    if rope:
        cos_ref, sin_ref = rest[:2]
        rest = rest[2:]
    if pending:
        r_ref, rgate_ref, rgain_ref = rest[:3]
        o_refs, xo_ref = rest[3:-1], rest[-1]
        r = r_ref[0]
        x = x_ref[0] + rgate_ref[0] * (r * lax.rsqrt(jnp.mean(r * r, axis=-1, keepdims=True) + RMS_EPS)
                                       * rgain_ref[...])
        xo_ref[0] = x
    else:
        o_refs = rest
        x = x_ref[0]
    y = x * lax.rsqrt(jnp.mean(x * x, axis=-1, keepdims=True) + RMS_EPS)
    h = ((y * g_ref[...]) * (1.0 + scale_ref[0]) + shift_ref[0]).astype(jnp.bfloat16)
    off = 0
    for (name, width), o_ref in zip(IN_GROUPS, o_refs):
        for c0 in range(0, width, MM_COLS):
            cw = min(MM_COLS, width - c0)
            acc = jnp.dot(h, w_ref[:, off + c0:off + c0 + cw], preferred_element_type=jnp.float32)
            if name == "g":
                acc = acc + gb_ref[...]
            if name == "mqk" and rope:
                cos, sin = cos_ref[...], sin_ref[...]
                acc = jnp.concatenate([_rope_tile(acc[:, t:t + LANES], cos, sin) for t in range(0, cw, LANES)], axis=-1)
            if name == "mvt":
                for c in range(acc.shape[0] // M_CHUNK):
                    o_ref[0, c] = acc[c * M_CHUNK:(c + 1) * M_CHUNK, :].T.astype(o_ref.dtype)
                continue
            o_ref[0, :, c0:c0 + cw] = acc.astype(o_ref.dtype)
        off += width


def in_proj(x, g, shift, scale, w_perm, gate_bias_row, rope_tabs=None, pending=None, *, tm):
    B, T, D = x.shape
    out_shape = [jax.ShapeDtypeStruct((B, T, w), jnp.float32 if n == "g" else jnp.bfloat16) for n, w in IN_GROUPS]
    out_specs = [pl.BlockSpec((1, tm, w), lambda b, i: (b, i, 0)) for _, w in IN_GROUPS]
    k_mvt = [n for n, _ in IN_GROUPS].index("mvt")
    assert BRANCH_W == MM_COLS and tm % M_CHUNK == 0
    out_shape[k_mvt] = jax.ShapeDtypeStruct((B, T // M_CHUNK, BRANCH_W, M_CHUNK), jnp.bfloat16)
    out_specs[k_mvt] = pl.BlockSpec((1, tm // M_CHUNK, BRANCH_W, M_CHUNK), lambda b, i: (b, i, 0, 0))
    in_specs = [
        pl.BlockSpec((1, tm, D), lambda b, i: (b, i, 0)),
        _const_spec((1, D)),
        pl.BlockSpec((1, 1, D), lambda b, i: (b, 0, 0)),
        pl.BlockSpec((1, 1, D), lambda b, i: (b, 0, 0)),
        _const_spec((D, IN_W)),
        _const_spec((1, GATE_W)),
    ]
    args = [x, g, shift, scale, w_perm, gate_bias_row]
    if rope_tabs is not None:
        in_specs += [pl.BlockSpec((tm, LANES), lambda b, i: (i, 0))] * 2
        args += list(rope_tabs)
    if pending is not None:
        in_specs += [pl.BlockSpec((1, tm, D), lambda b, i: (b, i, 0)), pl.BlockSpec((1, 1, D), lambda b, i: (b, 0, 0)),
                     _const_spec((1, D))]
        args += list(pending)
        out_shape.append(jax.ShapeDtypeStruct((B, T, D), jnp.float32))
        out_specs.append(pl.BlockSpec((1, tm, D), lambda b, i: (b, i, 0)))
    return pl.pallas_call(
        functools.partial(_in_proj_kernel, rope=rope_tabs is not None, pending=pending is not None),
        grid=(B, T // tm),
        in_specs=in_specs,
        out_specs=out_specs,
        out_shape=out_shape,
        compiler_params=pltpu.CompilerParams(
            dimension_semantics=("arbitrary", "arbitrary"), vmem_limit_bytes=VMEM_LIMIT),
        name="in_proj",
    )(*args)


def rope_lane_tables(n_tokens):
    t = jnp.arange(n_tokens)
    row = (t // GRID_W).astype(jnp.float32)
    col = (t % GRID_W).astype(jnp.float32)
    half = M_DQK // 2
    inv = ROPE_BASE ** (-jnp.arange(0, half, 2, dtype=jnp.float32) / half)
    ar = row[:, None] * inv[None, :]
    ac = col[:, None] * inv[None, :]
    cos = jnp.concatenate([jnp.cos(ar), jnp.cos(ar), jnp.cos(ac), jnp.cos(ac)], axis=-1)
    sin = jnp.concatenate([-jnp.sin(ar), jnp.sin(ar), -jnp.sin(ac), jnp.sin(ac)], axis=-1)
    return jnp.tile(cos, (1, LANES // M_DQK)), jnp.tile(sin, (1, LANES // M_DQK))


def permute_w_in(w_in_l):
    qm, km, vm, om, gm, pu, qn, kn, vn, bg = jnp.split(w_in_l, [int(s) for s in np.cumsum(SPLIT_SIZES)[:-1]], axis=-1)
    gpad = jnp.pad(gm, ((0, 0), (0, GATE_W - gm.shape[1])))
    w = jnp.concatenate([qm * (M_DQK ** -0.5), km, vm, om, pu, qn * (NA_DH ** -0.5), kn, vn, bg, gpad], axis=-1)
    return w.astype(jnp.bfloat16)


def _pair_scores(qp, k_parts, biases):
    n = qp.shape[0]
    lane_lo = lax.broadcasted_iota(jnp.int32, (n, LANES), 1) < NA_DH
    zero = jnp.zeros_like(qp)
    qq = jnp.concatenate([jnp.where(lane_lo, qp, zero), jnp.where(lane_lo, zero, qp)], axis=0)
    scores = []
    for kp, bias in zip(k_parts, biases):
        s = lax.dot_general(qq, kp, (((1,), (1,)), ((), ())), preferred_element_type=jnp.float32)
        scores.append(s if bias is None else s + bias)
    return scores


def _pair_softmax_pv(scores, v_parts):
    n = scores[0].shape[0] // 2
    lane_lo = lax.broadcasted_iota(jnp.int32, (n, LANES), 1) < NA_DH
    m = scores[0].max(axis=-1, keepdims=True)
    for s in scores[1:]:
        m = jnp.maximum(m, s.max(axis=-1, keepdims=True))
    l = None
    o = None
    for s, vp in zip(scores, v_parts):
        p = jnp.exp(s - m)
        ls = p.sum(axis=-1, keepdims=True)
        os_ = jnp.dot(p.astype(jnp.bfloat16), vp, preferred_element_type=jnp.float32)
        l = ls if l is None else l + ls
        o = os_ if o is None else o + os_
    o = o * (1.0 / l)
    return jnp.where(lane_lo, o[:n], o[n:])


def _na_kernel(q_ref, k_ref, v_ref, kc_ref, vc_ref, bias_ref, *rest, need_ctx):
    if need_ctx:
        qc_ref, o_ref, oc_ref, s_ref = rest
    else:
        o_ref, s_ref = rest
    S = q_ref.shape[1]
    Tc = kc_ref.shape[1]
    rows = S // GRID_W
    n_pairs = BRANCH_W // LANES
    n_loc = NA_KH * GRID_W

    def window(r):
        rs = jnp.clip(r - NA_KH // 2, 0, rows - NA_KH)
        return r - rs, pl.multiple_of(r * GRID_W, GRID_W), pl.multiple_of(rs * GRID_W, GRID_W)

    def scores_stage(r, slot):
        var, q0, k0 = window(r)
        for j in range(n_pairs):
            ls = slice(j * LANES, (j + 1) * LANES)
            s_loc, s_ctx = _pair_scores(q_ref[0, pl.ds(q0, GRID_W), ls],
                                        (k_ref[0, pl.ds(k0, n_loc), ls], kc_ref[0, :, ls]),
                                        (bias_ref[var, j], None))
            s_ref[slot, j, :, :n_loc] = s_loc
            s_ref[slot, j, :, n_loc:] = s_ctx

    def output_stage(r, slot):
        _, q0, k0 = window(r)
        outs = []
        for j in range(n_pairs):
            ls = slice(j * LANES, (j + 1) * LANES)
            outs.append(_pair_softmax_pv((s_ref[slot, j, :, :n_loc], s_ref[slot, j, :, n_loc:]),
                                         (v_ref[0, pl.ds(k0, n_loc), ls], vc_ref[0, :, ls])))
        o_ref[0, pl.ds(q0, GRID_W), :] = jnp.concatenate(outs, axis=-1).astype(o_ref.dtype)

    scores_stage(0, 0)

    def two_rows(i, carry):
        r0 = 2 * i
        scores_stage(r0 + 1, 1)
        output_stage(r0, 0)
        scores_stage(jnp.minimum(r0 + 2, rows - 1), 0)
        output_stage(r0 + 1, 1)
        return carry

    lax.fori_loop(0, rows // 2, two_rows, 0)

    if need_ctx:
        outs = []
        for j in range(n_pairs):
            ls = slice(j * LANES, (j + 1) * LANES)
            scores = _pair_scores(qc_ref[0, :, ls], (kc_ref[0, :, ls],), (None,))
            outs.append(_pair_softmax_pv(scores, (vc_ref[0, :, ls],)))
        oc_ref[0] = jnp.concatenate(outs, axis=-1).astype(oc_ref.dtype)


def na_bias_table(rpb):
    H = rpb.shape[0]
    var = jnp.arange(NA_KH)
    kr = jnp.arange(NA_KH)
    dr = kr[None, :] - var[:, None] + NA_KH - 1
    cols = jnp.arange(GRID_W)
    dc = jnp.clip(cols[None, :] - cols[:, None] + NA_KW - 1, 0, 2 * NA_KW - 2)
    cs = jnp.clip(cols - NA_KW // 2, 0, GRID_W - NA_KW)
    colmask = (cols[None, :] >= cs[:, None]) & (cols[None, :] < cs[:, None] + NA_KW)
    tab = rpb[:, dr][:, :, :, dc]
    tab = jnp.where(colmask[None, None, None], tab.astype(jnp.float32), MASK_NEG)
    tab = tab.transpose(1, 0, 3, 2, 4)
    return tab.reshape(NA_KH, H // 2, 2 * GRID_W, NA_KH * GRID_W)


def neighbourhood_attention(qn, kn, vn, kc, vc, bias_tab, qc=None):
    B, S, W = qn.shape
    Tc = kc.shape[1]
    need_ctx = qc is not None
    lat = pl.BlockSpec((1, S, W), lambda b: (b, 0, 0))
    cx = pl.BlockSpec((1, Tc, W), lambda b: (b, 0, 0))
    in_specs = [lat, lat, lat, cx, cx, _const_spec(bias_tab.shape)]
    args = [qn, kn, vn, kc, vc, bias_tab]
    out_shape = [jax.ShapeDtypeStruct((B, S, W), jnp.bfloat16)]
    out_specs = [lat]
    if need_ctx:
        in_specs.append(cx)
        args.append(qc)
        out_shape.append(jax.ShapeDtypeStruct((B, Tc, W), jnp.bfloat16))
        out_specs.append(cx)
    res = pl.pallas_call(
        functools.partial(_na_kernel, need_ctx=need_ctx),
        grid=(B,),
        in_specs=in_specs,
        out_specs=out_specs,
        out_shape=out_shape,
        scratch_shapes=[pltpu.VMEM((2, W // LANES, 2 * GRID_W, NA_KH * GRID_W + Tc), jnp.float32)],
        compiler_params=pltpu.CompilerParams(dimension_semantics=("arbitrary",), vmem_limit_bytes=VMEM_LIMIT),
        name="neighbourhood_attention",
    )(*args)
    return (res[0], res[1]) if need_ctx else (res[0], None)


def _split_bf16(x):
    hi = x.astype(jnp.bfloat16)
    r1 = x - hi.astype(jnp.float32)
    mid = r1.astype(jnp.bfloat16)
    lo = (r1 - mid.astype(jnp.float32)).astype(jnp.bfloat16)
    return hi, mid, lo


def _dot_f32(a, b):
    return jnp.dot(a, b, preferred_element_type=jnp.float32)


def _log_sigmoid(x):
    return jnp.minimum(x, 0.0) - jnp.log(1.0 + jnp.exp(-jnp.abs(x)))


def _pair_queries(qp):
    lane_lo = lax.broadcasted_iota(jnp.int32, qp.shape, 1) < M_DQK
    zq = jnp.zeros_like(qp)
    return jnp.concatenate([jnp.where(lane_lo, qp, zq), jnp.where(lane_lo, zq, qp)], axis=0)


def _mlstm_prep(qk_ref, g_ref, gt_ref, sa_ref, ra_ref, slot, ci, fwd):
    L = M_CHUNK
    f32, bf16 = jnp.float32, jnp.bfloat16
    d = 0 if fwd else 1
    rows = pl.ds(pl.multiple_of(ci * L, L), L)
    sq_r = lax.broadcasted_iota(jnp.int32, (L, L), 0)
    sq_c = lax.broadcasted_iota(jnp.int32, (L, L), 1)
    tri_l = jnp.where(sq_r >= sq_c, 1.0, 0.0).astype(bf16)
    tri_u = jnp.where(sq_r <= sq_c, 1.0, 0.0).astype(bf16)
    gates = g_ref[0, rows, :]
    gates_t = gt_ref[0, ci]
    lf, lf_t = _log_sigmoid(gates), _log_sigmoid(gates_t)
    cum = sum(_dot_f32(tri_l if fwd else tri_u, p) for p in _split_bf16(lf))
    cum_t = sum(_dot_f32(p, tri_u if fwd else tri_l) for p in _split_bf16(lf_t))
    t_i = 0 if fwd else 2
    st_s = lax.broadcasted_iota(jnp.int32, (L, 2 * L), 0)
    st_t = lax.broadcasted_iota(jnp.int32, (L, 2 * L), 1) & (L - 1)
    valid = (st_s <= st_t) if fwd else (st_s >= st_t)

    def row2(tile, c0, c1):
        return jnp.concatenate([tile[c0:c0 + 1, :], tile[c1:c1 + 1, :]], axis=-1)

    for j in range(M_HEADS // 2):
        ci0, ci1 = t_i * M_HEADS + 2 * j, t_i * M_HEADS + 2 * j + 1
        cf0, cf1 = ci0 + M_HEADS, ci1 + M_HEADS
        b_row, i_row = row2(cum_t, cf0, cf1), row2(gates_t, ci0, ci1)
        colb = jnp.concatenate([jnp.broadcast_to(gates[:, ci0:ci0 + 1] - cum[:, cf0:cf0 + 1], (L, L)),
                                jnp.broadcast_to(gates[:, ci1:ci1 + 1] - cum[:, cf1:cf1 + 1], (L, L))],
                               axis=-1)
        qp = qk_ref[0, rows, j * LANES:(j + 1) * LANES]
        kp = qk_ref[0, rows, M_HEADS * M_DQK + j * LANES:M_HEADS * M_DQK + (j + 1) * LANES]
        sa_ref[slot, d, j, 0] = jnp.where(valid, b_row + colb, MASK_NEG)
        sa_ref[slot, d, j, 1] = lax.dot_general(kp, _pair_queries(qp), (((1,), (1,)), ((), ())),
                                                preferred_element_type=f32)
        ra_ref[slot, d, j, 0:1, :] = b_row
        ra_ref[slot, d, j, 1:2, :] = i_row


def _mlstm_step(qk_ref, vt_ref, sa_ref, ra_ref, slot, h_ref, ct_ref, m_ref, ci, fwd):
    L = M_CHUNK
    f32, bf16 = jnp.float32, jnp.bfloat16
    d = 0 if fwd else 1
    rows = pl.ds(pl.multiple_of(ci * L, L), L)
    last = L - 1 if fwd else 0
    lane_lo = lax.broadcasted_iota(jnp.int32, (L, LANES), 1) < M_DQK
    lane_lo_row = lax.broadcasted_iota(jnp.int32, (1, LANES), 1) < M_DQK
    ones = jnp.ones((M_DV, L), bf16)

    def halves(row, f):
        return jnp.concatenate([jnp.broadcast_to(f(row[:, :L]), (1, L)), jnp.broadcast_to(f(row[:, L:]), (1, L))],
                               axis=-1)

    for j in range(M_HEADS // 2):
        h0, h1 = 2 * j, 2 * j + 1
        sidx = 2 * j + d
        b_row, i_row = ra_ref[slot, d, j, 0:1, :], ra_ref[slot, d, j, 1:2, :]
        m_row = m_ref[sidx, 0:1, :]
        kp = qk_ref[0, rows, M_HEADS * M_DQK + j * LANES:M_HEADS * M_DQK + (j + 1) * LANES]
        vx0 = jnp.concatenate([vt_ref[0, ci, h0 * M_DV:(h0 + 1) * M_DV, :], ones], axis=0)
        vx1 = jnp.concatenate([vt_ref[0, ci, h1 * M_DV:(h1 + 1) * M_DV, :], ones], axis=0)
        ct = ct_ref[sidx]
        if h_ref is not None:
            dm = sa_ref[slot, d, j, 0]
            qq = _pair_queries(qk_ref[0, rows, j * LANES:(j + 1) * LANES])
            g_row = b_row + m_row
            mt = jnp.maximum(g_row, dm.max(axis=0, keepdims=True))
            sc = (sa_ref[slot, d, j, 1] * jnp.exp(dm - mt)).astype(bf16)
            inter = jnp.exp(g_row - mt)
            t1 = lax.dot_general(ct.astype(bf16), qq, (((1,), (1,)), ((), ())), preferred_element_type=f32)
            t2 = jnp.concatenate([_dot_f32(vx0, sc[:, :L]), _dot_f32(vx1, sc[:, L:])], axis=-1)
            tot = inter * t1 + t2
            h_t = tot[:M_DV] / jnp.maximum(jnp.abs(tot[M_DV:]), jnp.exp(-mt))
            h_ref[rows, h0 * M_DV:(h0 + 1) * M_DV] = h_t[:, :L].T
            h_ref[rows, h1 * M_DV:(h1 + 1) * M_DV] = h_t[:, L:].T
        bl = halves(b_row, lambda r: r[:, last:last + 1])
        w = bl - b_row + i_row
        m_new = jnp.maximum(bl + m_row, halves(w, lambda r: r.max(axis=-1, keepdims=True)))
        decay = jnp.exp(bl + m_row - m_new)
        ws = jnp.exp(w - m_new)
        vxs = jnp.concatenate([vx0.astype(f32) * ws[:, :L], vx1.astype(f32) * ws[:, L:]], axis=-1).astype(bf16)
        zk = jnp.zeros_like(kp)
        kk = jnp.concatenate([jnp.where(lane_lo, kp, zk), jnp.where(lane_lo, zk, kp)], axis=0)
        d_cols = jnp.where(lane_lo_row, jnp.broadcast_to(decay[:, 0:1], (1, LANES)),
                           jnp.broadcast_to(decay[:, L:L + 1], (1, LANES)))
        ct_ref[sidx] = d_cols * ct + _dot_f32(vxs, kk)
        m_ref[sidx, 0:1, :] = m_new


def _mlstm_finish(hf_ref, hb_ref, o_ref, out_ref, n_chunks):
    L = M_CHUNK

    def body(ci, carry):
        rows = pl.ds(pl.multiple_of(ci * L, L), L)
        h = hf_ref[rows, :] + hb_ref[rows, :]
        parts = []
        for hd in range(M_HEADS):
            hh = h[:, hd * M_DV:(hd + 1) * M_DV]
            parts.append(hh * lax.rsqrt(jnp.mean(hh * hh, axis=-1, keepdims=True) + RMS_EPS))
        gate = jax.nn.sigmoid(o_ref[0, rows, :].astype(jnp.float32))
        out_ref[0, rows, :] = (gate * jnp.concatenate(parts, axis=-1)).astype(out_ref.dtype)
        return carry

    lax.fori_loop(0, n_chunks, body, 0)


def _mlstm_kernel(qk_c, v_c, o_c, g_c, gt_c, qk_l, v_l, o_l, g_l, gt_l, *rest, need_ctx):
    if need_ctx:
        out_l, out_c, hf_l, hb_l, hf_c, hb_c, ct_ref, m_ref, sa_ref, ra_ref = rest
    else:
        out_l, hf_l, hb_l, ct_ref, m_ref, sa_ref, ra_ref = rest
        hf_c = hb_c = None
    nc, nl = qk_c.shape[1] // M_CHUNK, qk_l.shape[1] // M_CHUNK
    ct_ref[...] = jnp.zeros_like(ct_ref)
    m_ref[...] = jnp.full_like(m_ref, M_INIT)

    def phase(qk, vt, g, gt, hf, hb, n):
        def prep(s, slot):
            _mlstm_prep(qk, g, gt, sa_ref, ra_ref, slot, s, True)
            _mlstm_prep(qk, g, gt, sa_ref, ra_ref, slot, n - 1 - s, False)

        def step(s, slot):
            _mlstm_step(qk, vt, sa_ref, ra_ref, slot, hf, ct_ref, m_ref, s, True)
            _mlstm_step(qk, vt, sa_ref, ra_ref, slot, hb, ct_ref, m_ref, n - 1 - s, False)

        prep(0, 0)

        def body(i, carry):
            s0 = 2 * i
            prep(s0 + 1, 1)
            step(s0, 0)
            prep(jnp.minimum(s0 + 2, n - 1), 0)
            step(s0 + 1, 1)
            return carry
        lax.fori_loop(0, n // 2, body, 0)

    phase(qk_c, v_c, g_c, gt_c, hf_c, hb_c, nc)
    phase(qk_l, v_l, g_l, gt_l, hf_l, hb_l, nl)
    _mlstm_finish(hf_l, hb_l, o_l, out_l, nl)
    if need_ctx:
        _mlstm_finish(hf_c, hb_c, o_c, out_c, nc)


def _gates_chunk_major(g):
    B, T, _ = g.shape
    return g[..., :4 * M_HEADS].reshape(B, T // M_CHUNK, M_CHUNK, 4 * M_HEADS).transpose(0, 1, 3, 2)


def mlstm(pc, pl_, need_ctx):
    B, S, W = pl_["mo"].shape
    Tc = pc["mo"].shape[1]
    f32 = jnp.float32

    def specs(T):
        s = pl.BlockSpec((1, T, W), lambda b: (b, 0, 0))
        return [s, pl.BlockSpec((1, T // M_CHUNK, W, M_CHUNK), lambda b: (b, 0, 0, 0)), s,
                pl.BlockSpec((1, T, GATE_W), lambda b: (b, 0, 0)),
                pl.BlockSpec((1, T // M_CHUNK, 4 * M_HEADS, M_CHUNK), lambda b: (b, 0, 0, 0))]

    def args(p):
        return [p["mqk"], p["mvt"], p["mo"], p["g"], _gates_chunk_major(p["g"])]

    out_shape = [jax.ShapeDtypeStruct((B, S, W), jnp.bfloat16)]
    out_specs = [pl.BlockSpec((1, S, W), lambda b: (b, 0, 0))]
    scratch = [pltpu.VMEM((S, W), f32), pltpu.VMEM((S, W), f32)]
    if need_ctx:
        out_shape.append(jax.ShapeDtypeStruct((B, Tc, W), jnp.bfloat16))
        out_specs.append(pl.BlockSpec((1, Tc, W), lambda b: (b, 0, 0)))
        scratch += [pltpu.VMEM((Tc, W), f32), pltpu.VMEM((Tc, W), f32)]
    assert (S // M_CHUNK) % 2 == 0 and (Tc // M_CHUNK) % 2 == 0
    n_pairs = M_HEADS // 2
    scratch += [pltpu.VMEM((M_HEADS, 2 * M_DV, LANES), f32), pltpu.VMEM((M_HEADS, 8, 2 * M_CHUNK), f32),
                pltpu.VMEM((2, 2, n_pairs, 2, M_CHUNK, 2 * M_CHUNK), f32),
                pltpu.VMEM((2, 2, n_pairs, 8, 2 * M_CHUNK), f32)]
    res = pl.pallas_call(
        functools.partial(_mlstm_kernel, need_ctx=need_ctx),
        grid=(B,),
        in_specs=specs(Tc) + specs(S),
        out_specs=out_specs,
        out_shape=out_shape,
        scratch_shapes=scratch,
        compiler_params=pltpu.CompilerParams(dimension_semantics=("arbitrary",), vmem_limit_bytes=VMEM_LIMIT),
        name="mlstm",
    )(*args(pc), *args(pl_))
    return (res[0], res[1]) if need_ctx else (res[0], None)


POOL_HALO = 16


def _pool_mix(pu_ref, prev_ref, next_ref, pw_ref, ps_ref, tile_idx, n_tiles, seq_len):
    f32 = jnp.float32
    tm = pu_ref.shape[1]
    cur = pu_ref[0].astype(f32)
    prev = jnp.where(tile_idx > 0, prev_ref[0].astype(f32), 0.0)
    nxt = jnp.where(tile_idx < n_tiles - 1, next_ref[0].astype(f32), 0.0)
    ext = jnp.concatenate([prev, cur, nxt], axis=0)
    n_ext = tm + 2 * POOL_HALO
    tok = tile_idx * tm + lax.broadcasted_iota(jnp.int32, (tm, POOL_GC), 0)

    def shifted(a, d):
        return pltpu.roll(a, (-d) % n_ext, 0)

    outs = []
    for gi, w in enumerate(POOL_WINDOWS):
        s = ext[:, gi * POOL_GC:(gi + 1) * POOL_GC]
        s = shifted(s, -1) + s
        span = 2
        while span < w:
            s = shifted(s, -(span // 2)) + shifted(s, span // 2)
            span *= 2
        lo = jnp.clip(tok - w // 2, 0, seq_len - 1)
        hi = jnp.clip(tok - w // 2 + w - 1, 0, seq_len - 1)
        mean = s[POOL_HALO:POOL_HALO + tm] / (hi - lo + 1).astype(f32)
        d = (mean - cur[:, gi * POOL_GC:(gi + 1) * POOL_GC]).astype(jnp.bfloat16)
        outs.append(_dot_f32(d, pw_ref[gi]))
    return jnp.concatenate(outs, axis=-1) * ps_ref[...]


def _pack_bf16_pairs(h):
    n = h.shape[1] // 2
    return pltpu.pack_elementwise([h[:, :n], h[:, n:]], packed_dtype=jnp.bfloat16)


def _merge_kernel(m_ref, n_ref, pu_ref, prev_ref, next_ref, bg_ref, x_ref, mods_ref, gains_ref,
                  pw_ref, ps_ref, wb_ref, wo_ref, wrt_ref, wrp_ref, x1_ref, h2a_ref, aff_ref, *, seq_len):
    f32, bf16 = jnp.float32, jnp.bfloat16
    i = pl.program_id(1)
    D = x_ref.shape[2]
    E = wrt_ref.shape[0]
    p = _pool_mix(pu_ref, prev_ref, next_ref, pw_ref, ps_ref, i, pl.num_programs(1), seq_len)
    branches = (m_ref[0], p.astype(bf16), n_ref[0])
    acc = None
    for bi, br in enumerate(branches):
        gate = jax.nn.sigmoid(bg_ref[0, :, bi * D:(bi + 1) * D].astype(f32))
        term = gate * _dot_f32(br, wb_ref[bi])
        acc = term if acc is None else acc + term
    y = _dot_f32(acc.astype(bf16), wo_ref[...])
    yn = y * lax.rsqrt(jnp.mean(y * y, axis=-1, keepdims=True) + RMS_EPS) * gains_ref[0:1, :]
    x1 = x_ref[0] + mods_ref[0, 0:1, :] * yn
    x1_ref[0] = x1
    xn = x1 * lax.rsqrt(jnp.mean(x1 * x1, axis=-1, keepdims=True) + RMS_EPS) * gains_ref[1:2, :]
    h2 = xn * (1.0 + mods_ref[0, 2:3, :]) + mods_ref[0, 1:2, :]
    h2b = h2.astype(bf16)
    lg_t = lax.dot_general(wrt_ref[...], h2b, (((1,), (1,)), ((), ())), preferred_element_type=f32)
    e_t = jnp.exp(lg_t - lg_t.max(axis=0, keepdims=True))
    aff_ref[0] = e_t / e_t.sum(axis=0, keepdims=True)
    lg = _dot_f32(h2b, wrp_ref[...])
    lg = jnp.where(lax.broadcasted_iota(jnp.int32, lg.shape, 1) < E, lg, MASK_NEG)
    e_r = jnp.exp(lg - lg.max(axis=-1, keepdims=True))
    aff_rows = e_r / e_r.sum(axis=-1, keepdims=True)
    h2a_ref[0] = jnp.concatenate([_pack_bf16_pairs(h2), pltpu.bitcast(aff_rows, jnp.uint32)], axis=-1)


def merge_out(m, n, pu, bg, x, mods, gains, pool_w, pool_scale, w_branch, w_out, w_router_t, w_router_pad, *, tm):
    B, T, D = x.shape
    W = m.shape[2]
    E = w_router_t.shape[0]
    hb = tm // POOL_HALO
    n_halo = T // POOL_HALO
    tile = lambda w: pl.BlockSpec((1, tm, w), lambda b, i: (b, i, 0))
    in_specs = [
        tile(W), tile(W), tile(W),
        pl.BlockSpec((1, POOL_HALO, W), lambda b, i: (b, jnp.maximum(i * hb - 1, 0), 0)),
        pl.BlockSpec((1, POOL_HALO, W), lambda b, i: (b, jnp.minimum((i + 1) * hb, n_halo - 1), 0)),
        tile(N_BRANCH * D), tile(D),
        pl.BlockSpec((1, 8, D), lambda b, i: (b, 0, 0)),
        _const_spec((8, D)),
        _const_spec(pool_w.shape), _const_spec(pool_scale.shape), _const_spec(w_branch.shape),
        _const_spec(w_out.shape), _const_spec(w_router_t.shape), _const_spec(w_router_pad.shape),
    ]
    return pl.pallas_call(
        functools.partial(_merge_kernel, seq_len=T),
        grid=(B, T // tm),
        in_specs=in_specs,
        out_specs=[tile(D), tile(MOE_ROW), pl.BlockSpec((1, E, tm), lambda b, i: (b, 0, i))],
        out_shape=[jax.ShapeDtypeStruct((B, T, D), jnp.float32),
                   jax.ShapeDtypeStruct((B, T, MOE_ROW), jnp.uint32),
                   jax.ShapeDtypeStruct((B, E, T), jnp.float32)],
        compiler_params=pltpu.CompilerParams(
            dimension_semantics=("arbitrary", "arbitrary"), vmem_limit_bytes=VMEM_LIMIT),
        name="merge_out",
    )(m, n, pu, pu, pu, bg, x, mods, gains, pool_w, pool_scale, w_branch, w_out, w_router_t, w_router_pad)


def _cumsum_lanes(x01):
    R, T = x01.shape
    r = lax.broadcasted_iota(jnp.int32, (LANES, LANES), 0)
    c = lax.broadcasted_iota(jnp.int32, (LANES, LANES), 1)
    tri_u = jnp.where(r <= c, 1.0, 0.0).astype(jnp.bfloat16)
    run = jnp.zeros((R, 1), jnp.float32)
    outs = []
    for t0 in range(0, T, LANES):
        cs = _dot_f32(x01[:, t0:t0 + LANES].astype(jnp.bfloat16), tri_u) + run
        run = cs[:, LANES - 1:LANES]
        outs.append(cs)
    return jnp.concatenate(outs, axis=-1)


def _route_kernel(aff_ref, idx_ref, *, cap):
    f32 = jnp.float32
    aff = aff_ref[0]
    E, T = aff.shape

    def search(k, lo):
        cand = lo | (jnp.int32(1) << (30 - k))
        cnt = jnp.sum(jnp.where(aff >= pltpu.bitcast(cand, f32), 1.0, 0.0), axis=-1, keepdims=True)
        return jnp.where(cnt >= cap, cand, lo)

    thr = pltpu.bitcast(lax.fori_loop(0, 31, search, jnp.zeros((E, 1), jnp.int32)), f32)
    gt = jnp.where(aff > thr, 1.0, 0.0)
    eq = jnp.where(aff == thr, 1.0, 0.0)
    room = cap - jnp.sum(gt, axis=-1, keepdims=True)
    sel = gt + eq * jnp.where(_cumsum_lanes(eq) <= room, 1.0, 0.0)
    csel = _cumsum_lanes(sel)
    slot = lax.broadcasted_iota(jnp.int32, (cap, T), 0).astype(f32)
    lane = lax.broadcasted_iota(jnp.int32, (cap, LANES), 1)
    out = jnp.zeros((cap, LANES), f32)
    for e in range(E):
        pos = jnp.sum(jnp.where(jnp.broadcast_to(csel[e:e + 1, :], (cap, T)) <= slot, 1.0, 0.0),
                      axis=-1, keepdims=True)
        out = jnp.where(lane == e, jnp.broadcast_to(pos, (cap, LANES)), out)
    idx_ref[0] = out.astype(jnp.int32)


def route(aff_t, cap):
    B, E, T = aff_t.shape
    idx_t = pl.pallas_call(
        functools.partial(_route_kernel, cap=cap),
        grid=(B,),
        in_specs=[pl.BlockSpec((1, E, T), lambda b: (b, 0, 0))],
        out_specs=pl.BlockSpec((1, cap, LANES), lambda b: (b, 0, 0)),
        out_shape=jax.ShapeDtypeStruct((B, cap, LANES), jnp.int32),
        compiler_params=pltpu.CompilerParams(dimension_semantics=("arbitrary",), vmem_limit_bytes=VMEM_LIMIT),
        name="route",
    )(aff_t)
    return idx_t[:, :, :E].transpose(0, 2, 1)


ROW_GROUP = 8


def _moe_kernel(idx_ref, rows_ref, wg_ref, wu_ref, wd_ref, out_ref, xg0_ref, xg1_ref, ye0_ref, ye1_ref):
    f32, bf16 = jnp.float32, jnp.bfloat16
    e = pl.program_id(1)
    n_exp = pl.num_programs(1)
    cap = xg0_ref.shape[0]
    half = wg_ref.shape[1] // 2

    def gather_row(dst_ref, ee, j):
        dst_ref[pl.ds(j, 1), :] = rows_ref[0, pl.ds(idx_ref[0, 0, ee, j], 1), :]

    def scatter_group(src_ref, ee, j0):
        ids = [idx_ref[0, 0, ee, j0 + r] for r in range(ROW_GROUP)]
        cur = [out_ref[0, pl.ds(i, 1), :] for i in ids]
        for r, i in enumerate(ids):
            out_ref[0, pl.ds(i, 1), :] = cur[r] + src_ref[pl.ds(j0 + r, 1), :]

    @pl.when(e == 0)
    def _():
        out_ref[...] = jnp.zeros_like(out_ref)
        ye1_ref[...] = jnp.zeros_like(ye1_ref)

        def body(gi, carry):
            for r in range(ROW_GROUP):
                gather_row(xg0_ref, 0, gi * ROW_GROUP + r)
            return carry
        lax.fori_loop(0, cap // ROW_GROUP, body, 0)

    def step(xg_cur, xg_nxt, ye_cur, ye_prv):
        e_prv = jnp.maximum(e - 1, 0)
        e_nxt = jnp.minimum(e + 1, n_exp - 1)
        for j0 in range(0, cap, ROW_GROUP):
            scatter_group(ye_prv, e_prv, j0)
        for j in range(cap):
            gather_row(xg_nxt, e_nxt, j)
        xg = xg_cur[...]
        packed = xg[:, :half]
        xe = jnp.concatenate(
            [pltpu.unpack_elementwise(packed, index=i, packed_dtype=bf16, unpacked_dtype=f32) for i in (0, 1)],
            axis=-1).astype(bf16)
        aff = pltpu.bitcast(xg[:, half:], f32)
        gate = jnp.sum(jnp.where(lax.broadcasted_iota(jnp.int32, aff.shape, 1) == e, aff, 0.0),
                       axis=-1, keepdims=True)
        a = _dot_f32(xe, wg_ref[0])
        hid = (a * jax.nn.sigmoid(a) * _dot_f32(xe, wu_ref[0])).astype(bf16)
        ye_cur[...] = _dot_f32(hid, wd_ref[0]) * gate

    @pl.when(e % 2 == 0)
    def _():
        step(xg0_ref, xg1_ref, ye0_ref, ye1_ref)

    @pl.when(e % 2 == 1)
    def _():
        step(xg1_ref, xg0_ref, ye1_ref, ye0_ref)

    @pl.when(e == n_exp - 1)
    def _():
        def body(gi, carry):
            scatter_group(ye1_ref, e, gi * ROW_GROUP)
            return carry
        lax.fori_loop(0, cap // ROW_GROUP, body, 0)


def moe_experts(rows, idx, w_gate, w_up, w_down):
    G, T, RW = rows.shape
    _, E, cap = idx.shape
    D = w_gate.shape[1]
    assert E % 2 == 0 and cap % ROW_GROUP == 0
    wspec = pl.BlockSpec((1, D, D), lambda g, e: (e, 0, 0))
    return pl.pallas_call(
        _moe_kernel,
        grid=(G, E),
        in_specs=[pl.BlockSpec((1, 1, E, cap), lambda g, e: (g, 0, 0, 0), memory_space=pltpu.SMEM),
                  pl.BlockSpec((1, T, RW), lambda g, e: (g, 0, 0)), wspec, wspec, wspec],
        out_specs=pl.BlockSpec((1, T, D), lambda g, e: (g, 0, 0)),
        out_shape=jax.ShapeDtypeStruct((G, T, D), jnp.float32),
        scratch_shapes=[pltpu.VMEM((cap, RW), jnp.uint32), pltpu.VMEM((cap, RW), jnp.uint32),
                        pltpu.VMEM((cap, D), jnp.float32), pltpu.VMEM((cap, D), jnp.float32)],
        compiler_params=pltpu.CompilerParams(
            dimension_semantics=("arbitrary", "arbitrary"), vmem_limit_bytes=VMEM_LIMIT),
        name="moe_experts",
    )(idx.reshape(G, 1, E, cap), rows, w_gate, w_up, w_down)


def _residual_norm_kernel(x_ref, y_ref, gate_ref, gain_ref, o_ref):
    y = y_ref[0]
    yn = y * lax.rsqrt(jnp.mean(y * y, axis=-1, keepdims=True) + RMS_EPS) * gain_ref[...]
    o_ref[0] = x_ref[0] + gate_ref[0] * yn


def residual_norm(x, y, gate, gain, *, tm):
    B, T, D = x.shape
    tile = pl.BlockSpec((1, tm, D), lambda b, i: (b, i, 0))
    return pl.pallas_call(
        _residual_norm_kernel,
        grid=(B, T // tm),
        in_specs=[tile, tile, pl.BlockSpec((1, 1, D), lambda b, i: (b, 0, 0)), _const_spec((1, D))],
        out_specs=tile,
        out_shape=jax.ShapeDtypeStruct((B, T, D), jnp.float32),
        compiler_params=pltpu.CompilerParams(
            dimension_semantics=("arbitrary", "arbitrary"), vmem_limit_bytes=VMEM_LIMIT),
        name="residual_norm",
    )(x, y, gate, gain)


def rmsnorm(x, g):
    xf = x.astype(jnp.float32)
    y = xf * lax.rsqrt(jnp.mean(xf * xf, axis=-1, keepdims=True) + RMS_EPS)
    return (y * g.astype(jnp.float32)).astype(x.dtype)


def modulate(h, shift, scale):
    return h * (1 + scale) + shift


def rope_tables(n_tokens, dtype):
    t = jnp.arange(n_tokens)
    row = (t // GRID_W).astype(jnp.float32)
    col = (t % GRID_W).astype(jnp.float32)
    half = M_DQK // 2
    inv = ROPE_BASE ** (-jnp.arange(0, half, 2, dtype=jnp.float32) / half)
    ar = row[:, None] * inv[None, :]
    ac = col[:, None] * inv[None, :]
    return tuple(a.astype(dtype) for a in (jnp.cos(ar), jnp.sin(ar), jnp.cos(ac), jnp.sin(ac)))


def _rotate(x, cos, sin):
    x1, x2 = jnp.split(x, 2, axis=-1)
    cos = cos[:, None, :]
    sin = sin[:, None, :]
    return jnp.concatenate([x1 * cos - x2 * sin, x2 * cos + x1 * sin], axis=-1)


def rope_2d(x, tabs):
    cr, sr, cc, sc = tabs
    xr, xc = jnp.split(x, 2, axis=-1)
    return jnp.concatenate([_rotate(xr, cr, sr), _rotate(xc, cc, sc)], axis=-1)


def mlstm_scan(q, k, v, li, lf, state):
    B, T, H, _ = q.shape
    dv = v.shape[-1]
    L = M_CHUNK
    N = T // L

    def chunks(a):
        a = a.reshape((B, N, L, H) + a.shape[3:])
        return jnp.moveaxis(a, (1, 3), (0, 2))

    tril = jnp.tril(jnp.ones((L, L), dtype=bool))

    def step(carry, inp):
        C, n, m = carry
        qc, kc, vc, ic, fc = inp
        b = jnp.cumsum(fc, axis=-1)
        dmat = jnp.where(tril, b[..., :, None] - b[..., None, :] + ic[..., None, :], -jnp.inf)
        g = b + m[..., None]
        mt = jnp.maximum(g, dmat.max(-1))
        sc = jnp.einsum('bhtd,bhsd->bhts', qc, kc) * jnp.exp(dmat - mt[..., None])
        inter = jnp.exp(g - mt)
        num = inter[..., None] * jnp.einsum('bhvd,bhtd->bhtv', C, qc) + jnp.einsum('bhts,bhsv->bhtv', sc, vc)
        den = inter * jnp.einsum('bhd,bhtd->bht', n, qc) + sc.sum(-1)
        h = num / jnp.maximum(jnp.abs(den), jnp.exp(-mt))[..., None]
        bl = b[..., -1]
        w = bl[..., None] - b + ic
        m_new = jnp.maximum(bl + m, w.max(-1))
        decay = jnp.exp(bl + m - m_new)
        ws = jnp.exp(w - m_new[..., None])
        C_new = decay[..., None, None] * C + jnp.einsum('bhs,bhsv,bhsd->bhvd', ws, vc, kc)
        n_new = decay[..., None] * n + jnp.einsum('bhs,bhsd->bhd', ws, kc)
        return (C_new, n_new, m_new), h

    xs = (chunks(q), chunks(k), chunks(v), chunks(li), chunks(lf))
    state, h = lax.scan(step, state, xs)
    h = jnp.moveaxis(h, (0, 2), (1, 3)).reshape(B, T, H, dv)
    return state, h


def head_rms(h):
    return h * lax.rsqrt(jnp.mean(h * h, axis=-1, keepdims=True) + RMS_EPS)


def mlstm_branch(ctx_parts, lat_parts, need_ctx):
    f32 = jnp.float32

    def prep(parts):
        q, k, v, o, gt = parts
        q = q.astype(f32)
        k = k.astype(f32)
        v = v.astype(f32)
        fwd = (q, k, v, gt[..., 0, :], jax.nn.log_sigmoid(gt[..., 1, :]))
        bwd = tuple(jnp.flip(a, axis=1) for a in (q, k, v, gt[..., 2, :], jax.nn.log_sigmoid(gt[..., 3, :])))
        return fwd, bwd

    c_f, c_b = prep(ctx_parts)
    l_f, l_b = prep(lat_parts)
    B = l_f[0].shape[0]
    st0 = (jnp.zeros((B, M_HEADS, M_DV, M_DQK), f32), jnp.zeros((B, M_HEADS, M_DQK), f32),
           jnp.full((B, M_HEADS), M_INIT, f32))
    st_f, hc_f = mlstm_scan(*c_f, st0)
    _, hl_f = mlstm_scan(*l_f, st_f)
    st_b, hc_b = mlstm_scan(*c_b, st0)
    _, hl_b = mlstm_scan(*l_b, st_b)

    def finish(h_f, h_b_rev, o):
        h = head_rms(h_f + jnp.flip(h_b_rev, axis=1))
        return jax.nn.sigmoid(o) * h.reshape(o.shape).astype(o.dtype)

    out_l = finish(hl_f, hl_b, lat_parts[3])
    out_c = finish(hc_f, hc_b, ctx_parts[3]) if need_ctx else None
    return out_l, out_c


def pool_branch(u, pool_w, pool_scale):
    B, T, _ = u.shape
    uf = u.astype(jnp.float32).reshape(B, T, POOL_GROUPS, POOL_GC)
    csum = jnp.concatenate([jnp.zeros((B, 1, POOL_GROUPS, POOL_GC), jnp.float32),
                            jnp.cumsum(uf, axis=1)], axis=1)
    t = jnp.arange(T)
    outs = []
    for gi, w in enumerate(POOL_WINDOWS):
        lo = jnp.clip(t - w // 2, 0, T - 1)
        hi = jnp.clip(t - w // 2 + w - 1, 0, T - 1)
        cnt = (hi - lo + 1).astype(jnp.float32)[:, None]
        cg = csum[:, :, gi]
        mean = (cg[:, hi + 1] - cg[:, lo]) / cnt
        outs.append(jnp.einsum('btc,cd->btd', (mean - uf[:, :, gi]).astype(u.dtype), pool_w[gi]))
    return jnp.concatenate(outs, axis=-1) * pool_scale


def merge_branches(branches, bg, w_branch, w_out):
    D = w_out.shape[0]
    acc = None
    for bi, br in enumerate(branches):
        term = jax.nn.sigmoid(bg[..., bi * D:(bi + 1) * D]) * (br @ w_branch[bi])
        acc = term if acc is None else acc + term
    return acc @ w_out


def ec_moe(h, w_router, w_gate, w_up, w_down):
    B, T, D = h.shape
    cap = EC_FACTOR * T // N_EXPERTS
    aff = jax.nn.softmax(jnp.einsum('btd,de->bte', h, w_router, preferred_element_type=jnp.float32), axis=-1)
    gate, idx = lax.top_k(jnp.swapaxes(aff, 1, 2), cap)
    xe = jax.vmap(lambda hb, ib: hb[ib])(h, idx)
    hid = jax.nn.silu(jnp.einsum('becd,edf->becf', xe, w_gate)) * jnp.einsum('becd,edf->becf', xe, w_up)
    ye = jnp.einsum('becf,efd->becd', hid, w_down) * gate[..., None].astype(h.dtype)
    return jax.vmap(lambda ib, yb: jnp.zeros((T, D), yb.dtype).at[ib.reshape(-1)].add(yb.reshape(-1, D)))(idx, ye)


def _mparts(p, rope_tabs=None):
    B, T, _ = p["mqk"].shape
    f32 = jnp.float32
    hq = M_HEADS * M_DQK
    q = p["mqk"][..., :hq].astype(f32).reshape(B, T, M_HEADS, M_DQK)
    k = p["mqk"][..., hq:].astype(f32).reshape(B, T, M_HEADS, M_DQK)
    if rope_tabs is not None:
        q = rope_2d(q, rope_tabs)
        k = rope_2d(k, rope_tabs)
    v = p["mv"].astype(f32).reshape(B, T, M_HEADS, M_DV)
    return q, k, v, p["mo"].astype(f32), p["g"][..., :4 * M_HEADS].reshape(B, T, 4, M_HEADS)


def kernel(x, c, ctx, c_ctx, norm_gain, ada_w, ada_b, w_in, mlstm_gate_bias, pool_w, pool_scale,
           na_rpb, w_branch, w_out, router_w, w_gate, w_up, w_down):
    B, S, D = x.shape
    Tc = ctx.shape[1]
    f32, bf16 = jnp.float32, jnp.bfloat16
    rope_tabs = rope_lane_tables(S)
    names = [n for n, _ in IN_GROUPS]
    pending = None
    for l in range(DEPTH):
        need_ctx = l < DEPTH - 1
        g = norm_gain[l]
        mod_l = jnp.split((jax.nn.silu(c) @ ada_w[l] + ada_b[l])[:, None, :], N_MOD, axis=-1)
        mod_c = jnp.split(jax.nn.silu(c_ctx) @ ada_w[l] + ada_b[l], N_MOD, axis=-1)
        mod_cb = [jnp.broadcast_to(m[None, None], (B, 1, D)) for m in mod_c]
        w_perm = permute_w_in(w_in[l])
        gb_row = jnp.pad(mlstm_gate_bias[l].reshape(1, -1).astype(f32), ((0, 0), (0, GATE_W - 4 * M_HEADS)))
        proj = in_proj(x, g[0][None], mod_l[0], mod_l[1], w_perm, gb_row, rope_tabs, pending, tm=512)
        if pending is not None:
            x = proj[-1]
        pl_ = dict(zip(names, proj))
        pc_ = dict(zip(names, in_proj(ctx, g[0][None], mod_cb[0], mod_cb[1], w_perm, gb_row, tm=Tc)))

        m_l, m_c = mlstm(pc_, pl_, need_ctx)
        n_l, n_c = neighbourhood_attention(pl_["qn"], pl_["kn"], pl_["vn"], pc_["kn"], pc_["vn"],
                                           na_bias_table(na_rpb[l]), pc_["qn"] if need_ctx else None)
        gains = jnp.pad(g[1:3], ((0, 6), (0, 0)))
        merge_w = (pool_w[l].astype(bf16), pool_scale[l][None], w_branch[l].astype(bf16), w_out[l].astype(bf16),
                   router_w[l].T.astype(bf16),
                   jnp.pad(router_w[l], ((0, 0), (0, LANES - N_EXPERTS))).astype(bf16))
        moe_w = (w_gate[l].astype(bf16), w_up[l].astype(bf16), w_down[l].astype(bf16))

        def mods8(mods):
            return jnp.pad(jnp.concatenate([mods[2], mods[3], mods[4]], axis=1), ((0, 0), (0, 5), (0, 0)))

        x1, rows, aff_t = merge_out(m_l, n_l, pl_["pu"], pl_["bg"], x, mods8(mod_l), gains, *merge_w, tm=512)
        y = moe_experts(rows, route(aff_t, EC_FACTOR * S // N_EXPERTS), *moe_w)
        if need_ctx:
            x, pending = x1, (y, mod_l[5], g[3][None])
        else:
            x = residual_norm(x1, y, mod_l[5], g[3][None], tm=512)
        if need_ctx:
            c1, rows, aff_t = merge_out(m_c, n_c, pc_["pu"], pc_["bg"], ctx, mods8(mod_cb), gains, *merge_w, tm=Tc)
            per = S // Tc
            cap_c = EC_FACTOR * Tc // N_EXPERTS
            idx = route(aff_t, cap_c) + (jnp.arange(B, dtype=jnp.int32) % per * Tc)[:, None, None]
            idx = idx.reshape(B // per, per, N_EXPERTS, cap_c).transpose(0, 2, 1, 3).reshape(B // per, N_EXPERTS, per * cap_c)
            y = moe_experts(rows.reshape(B // per, S, MOE_ROW), idx, *moe_w).reshape(B, Tc, D)
            ctx = residual_norm(c1, y, mod_cb[5], g[3][None], tm=Tc)
    return x
```

```python
import functools
import math

import jax
import jax.numpy as jnp
import numpy as np
from jax import lax
from jax.experimental import pallas as pl
from jax.experimental.pallas import tpu as pltpu

D_MODEL = 1024
DEPTH = 2
GRID_W = 64
BRANCH_W = D_MODEL // 2
N_BRANCH = 3
M_HEADS = 4
M_DV = BRANCH_W // M_HEADS
M_DQK = M_DV // 2
M_CHUNK = 128
M_INIT = -1e30
POOL_GROUPS = 4
POOL_GC = BRANCH_W // POOL_GROUPS
POOL_WINDOWS = (2, 4, 8, 16)
NA_HEADS = 8
NA_DH = BRANCH_W // NA_HEADS
NA_KH = 8
NA_KW = 16
N_EXPERTS = 16
EC_FACTOR = 2
ROPE_BASE = 10000.0
RMS_EPS = 1e-6
N_MOD = 6
SPLIT_SIZES = (M_HEADS * M_DQK, M_HEADS * M_DQK, M_HEADS * M_DV, M_HEADS * M_DV, 4 * M_HEADS,
               POOL_GROUPS * POOL_GC, NA_HEADS * NA_DH, NA_HEADS * NA_DH, NA_HEADS * NA_DH,
               N_BRANCH * D_MODEL)
PROJ_W = sum(SPLIT_SIZES)

LANES = 128
VMEM_LIMIT = 56 * 1024 * 1024
MASK_NEG = -1e30
GATE_W = LANES

IN_GROUPS = (("mqk", 2 * M_HEADS * M_DQK), ("mvt", BRANCH_W), ("mo", BRANCH_W), ("pu", BRANCH_W),
             ("qn", BRANCH_W), ("kn", BRANCH_W), ("vn", BRANCH_W), ("bg", N_BRANCH * D_MODEL),
             ("g", GATE_W))
IN_W = sum(w for _, w in IN_GROUPS)
MM_COLS = 512
MOE_ROW = D_MODEL // 2 + LANES


def _sigmoid(x):
    return 0.5 * jnp.tanh(0.5 * x) + 0.5


def _const_spec(shape):
    nd = len(shape)
    return pl.BlockSpec(shape, lambda *_: (0,) * nd, pipeline_mode=pl.Buffered(1))


def _rope_tile(x, cos, sin_signed):
    half = M_DQK // 4
    lane = lax.broadcasted_iota(jnp.int32, x.shape, 1)
    partner = jnp.where((lane % (2 * half)) < half, pltpu.roll(x, LANES - half, 1), pltpu.roll(x, half, 1))
    return x * cos + partner * sin_signed


def _in_proj_kernel(x_ref, g_ref, shift_ref, scale_ref, w_ref, gb_ref, *rest, rope, pending):
    if rope:
        cos_ref, sin_ref = rest[:2]
        rest = rest[2:]
    if pending:
        r_ref, rgate_ref, rgain_ref = rest[:3]
        o_refs, xo_ref = rest[3:-1], rest[-1]
        r = r_ref[0]
        x = x_ref[0] + rgate_ref[0] * (r * lax.rsqrt(jnp.mean(r * r, axis=-1, keepdims=True) + RMS_EPS)
                                       * rgain_ref[...])
        xo_ref[0] = x
    else:
        o_refs = rest
        x = x_ref[0]
    y = x * lax.rsqrt(jnp.mean(x * x, axis=-1, keepdims=True) + RMS_EPS)
    h = ((y * g_ref[...]) * (1.0 + scale_ref[0]) + shift_ref[0]).astype(jnp.bfloat16)
    off = 0
    for (name, width), o_ref in zip(IN_GROUPS, o_refs):
        for c0 in range(0, width, MM_COLS):
            cw = min(MM_COLS, width - c0)
            acc = jnp.dot(h, w_ref[:, off + c0:off + c0 + cw], preferred_element_type=jnp.float32)
            if name == "g":
                acc = acc + gb_ref[...]
            if name == "mqk" and rope:
                cos, sin = cos_ref[...], sin_ref[...]
                acc = jnp.concatenate([_rope_tile(acc[:, t:t + LANES], cos, sin) for t in range(0, cw, LANES)], axis=-1)
            if name == "mvt":
                for c in range(acc.shape[0] // M_CHUNK):
                    o_ref[0, c] = acc[c * M_CHUNK:(c + 1) * M_CHUNK, :].T.astype(o_ref.dtype)
                continue
            o_ref[0, :, c0:c0 + cw] = acc.astype(o_ref.dtype)
        off += width


def in_proj(x, g, shift, scale, w_perm, gate_bias_row, rope_tabs=None, pending=None, *, tm):
    B, T, D = x.shape
    out_shape = [jax.ShapeDtypeStruct((B, T, w), jnp.float32 if n == "g" else jnp.bfloat16) for n, w in IN_GROUPS]
    out_specs = [pl.BlockSpec((1, tm, w), lambda b, i: (b, i, 0)) for _, w in IN_GROUPS]
    k_mvt = [n for n, _ in IN_GROUPS].index("mvt")
    assert BRANCH_W == MM_COLS and tm % M_CHUNK == 0
    out_shape[k_mvt] = jax.ShapeDtypeStruct((B, T // M_CHUNK, BRANCH_W, M_CHUNK), jnp.bfloat16)
    out_specs[k_mvt] = pl.BlockSpec((1, tm // M_CHUNK, BRANCH_W, M_CHUNK), lambda b, i: (b, i, 0, 0))
    in_specs = [
        pl.BlockSpec((1, tm, D), lambda b, i: (b, i, 0)),
        _const_spec((1, D)),
        pl.BlockSpec((1, 1, D), lambda b, i: (b, 0, 0)),
        pl.BlockSpec((1, 1, D), lambda b, i: (b, 0, 0)),
        _const_spec((D, IN_W)),
        _const_spec((1, GATE_W)),
    ]
    args = [x, g, shift, scale, w_perm, gate_bias_row]
    if rope_tabs is not None:
        in_specs += [pl.BlockSpec((tm, LANES), lambda b, i: (i, 0))] * 2
        args += list(rope_tabs)
    if pending is not None:
        in_specs += [pl.BlockSpec((1, tm, D), lambda b, i: (b, i, 0)), pl.BlockSpec((1, 1, D), lambda b, i: (b, 0, 0)),
                     _const_spec((1, D))]
        args += list(pending)
        out_shape.append(jax.ShapeDtypeStruct((B, T, D), jnp.float32))
        out_specs.append(pl.BlockSpec((1, tm, D), lambda b, i: (b, i, 0)))
    return pl.pallas_call(
        functools.partial(_in_proj_kernel, rope=rope_tabs is not None, pending=pending is not None),
        grid=(B, T // tm),
        in_specs=in_specs,
        out_specs=out_specs,
        out_shape=out_shape,
        compiler_params=pltpu.CompilerParams(
            dimension_semantics=("arbitrary", "arbitrary"), vmem_limit_bytes=VMEM_LIMIT),
        name="in_proj",
    )(*args)


def rope_lane_tables(n_tokens):
    t = jnp.arange(n_tokens)
    row = (t // GRID_W).astype(jnp.float32)
    col = (t % GRID_W).astype(jnp.float32)
    half = M_DQK // 2
    inv = ROPE_BASE ** (-jnp.arange(0, half, 2, dtype=jnp.float32) / half)
    ar = row[:, None] * inv[None, :]
    ac = col[:, None] * inv[None, :]
    cos = jnp.concatenate([jnp.cos(ar), jnp.cos(ar), jnp.cos(ac), jnp.cos(ac)], axis=-1)
    sin = jnp.concatenate([-jnp.sin(ar), jnp.sin(ar), -jnp.sin(ac), jnp.sin(ac)], axis=-1)
    return jnp.tile(cos, (1, LANES // M_DQK)), jnp.tile(sin, (1, LANES // M_DQK))


def permute_w_in(w_in_l):
    qm, km, vm, om, gm, pu, qn, kn, vn, bg = jnp.split(w_in_l, [int(s) for s in np.cumsum(SPLIT_SIZES)[:-1]], axis=-1)
    gpad = jnp.pad(gm, ((0, 0), (0, GATE_W - gm.shape[1])))
    w = jnp.concatenate([qm * (M_DQK ** -0.5), km, vm, om, pu, qn * (NA_DH ** -0.5), kn, vn, bg * 0.5, gpad], axis=-1)
    return w.astype(jnp.bfloat16)


def _pair_scores(qp, k_parts, biases):
    n = qp.shape[0]
    lane_lo = lax.broadcasted_iota(jnp.int32, (n, LANES), 1) < NA_DH
    zero = jnp.zeros_like(qp)
    qq = jnp.concatenate([jnp.where(lane_lo, qp, zero), jnp.where(lane_lo, zero, qp)], axis=0)
    scores = []
    for kp, bias in zip(k_parts, biases):
        s = lax.dot_general(qq, kp, (((1,), (1,)), ((), ())), preferred_element_type=jnp.float32)
        scores.append(s if bias is None else s + bias)
    return scores


def _pair_softmax_pv(scores, v_parts):
    n = scores[0].shape[0] // 2
    lane_lo = lax.broadcasted_iota(jnp.int32, (n, LANES), 1) < NA_DH
    m = scores[0].max(axis=-1, keepdims=True)
    for s in scores[1:]:
        m = jnp.maximum(m, s.max(axis=-1, keepdims=True))
    l = None
    o = None
    for s, vp in zip(scores, v_parts):
        p = jnp.exp(s - m)
        ls = p.sum(axis=-1, keepdims=True)
        os_ = jnp.dot(p.astype(jnp.bfloat16), vp, preferred_element_type=jnp.float32)
        l = ls if l is None else l + ls
        o = os_ if o is None else o + os_
    o = o * (1.0 / l)
    return jnp.where(lane_lo, o[:n], o[n:])


def _na_kernel(q_ref, k_ref, v_ref, kc_ref, vc_ref, bias_ref, *rest, need_ctx):
    if need_ctx:
        qc_ref, o_ref, oc_ref, s_ref = rest
    else:
        o_ref, s_ref = rest
    S = q_ref.shape[1]
    Tc = kc_ref.shape[1]
    rows = S // GRID_W
    n_pairs = BRANCH_W // LANES
    n_loc = NA_KH * GRID_W

    def window(r):
        rs = jnp.clip(r - NA_KH // 2, 0, rows - NA_KH)
        return r - rs, pl.multiple_of(r * GRID_W, GRID_W), pl.multiple_of(rs * GRID_W, GRID_W)

    def scores_stage(r, slot):
        var, q0, k0 = window(r)
        for j in range(n_pairs):
            ls = slice(j * LANES, (j + 1) * LANES)
            s_loc, s_ctx = _pair_scores(q_ref[0, pl.ds(q0, GRID_W), ls],
                                        (k_ref[0, pl.ds(k0, n_loc), ls], kc_ref[0, :, ls]),
                                        (bias_ref[var, j], None))
            s_ref[slot, j, :, :n_loc] = s_loc
            s_ref[slot, j, :, n_loc:] = s_ctx

    def output_stage(r, slot):
        _, q0, k0 = window(r)
        outs = []
        for j in range(n_pairs):
            ls = slice(j * LANES, (j + 1) * LANES)
            outs.append(_pair_softmax_pv((s_ref[slot, j, :, :n_loc], s_ref[slot, j, :, n_loc:]),
                                         (v_ref[0, pl.ds(k0, n_loc), ls], vc_ref[0, :, ls])))
        o_ref[0, pl.ds(q0, GRID_W), :] = jnp.concatenate(outs, axis=-1).astype(o_ref.dtype)

    scores_stage(0, 0)

    def two_rows(i, carry):
        r0 = 2 * i
        scores_stage(r0 + 1, 1)
        output_stage(r0, 0)
        scores_stage(jnp.minimum(r0 + 2, rows - 1), 0)
        output_stage(r0 + 1, 1)
        return carry

    lax.fori_loop(0, rows // 2, two_rows, 0)

    if need_ctx:
        outs = []
        for j in range(n_pairs):
            ls = slice(j * LANES, (j + 1) * LANES)
            scores = _pair_scores(qc_ref[0, :, ls], (kc_ref[0, :, ls],), (None,))
            outs.append(_pair_softmax_pv(scores, (vc_ref[0, :, ls],)))
        oc_ref[0] = jnp.concatenate(outs, axis=-1).astype(oc_ref.dtype)


def na_bias_table(rpb):
    H = rpb.shape[0]
    var = jnp.arange(NA_KH)
    kr = jnp.arange(NA_KH)
    dr = kr[None, :] - var[:, None] + NA_KH - 1
    cols = jnp.arange(GRID_W)
    dc = jnp.clip(cols[None, :] - cols[:, None] + NA_KW - 1, 0, 2 * NA_KW - 2)
    cs = jnp.clip(cols - NA_KW // 2, 0, GRID_W - NA_KW)
    colmask = (cols[None, :] >= cs[:, None]) & (cols[None, :] < cs[:, None] + NA_KW)
    tab = rpb[:, dr][:, :, :, dc]
    tab = jnp.where(colmask[None, None, None], tab.astype(jnp.float32), MASK_NEG)
    tab = tab.transpose(1, 0, 3, 2, 4)
    return tab.reshape(NA_KH, H // 2, 2 * GRID_W, NA_KH * GRID_W)


def neighbourhood_attention(qn, kn, vn, kc, vc, bias_tab, qc=None):
    B, S, W = qn.shape
    Tc = kc.shape[1]
    need_ctx = qc is not None
    lat = pl.BlockSpec((1, S, W), lambda b: (b, 0, 0))
    cx = pl.BlockSpec((1, Tc, W), lambda b: (b, 0, 0))
    in_specs = [lat, lat, lat, cx, cx, _const_spec(bias_tab.shape)]
    args = [qn, kn, vn, kc, vc, bias_tab]
    out_shape = [jax.ShapeDtypeStruct((B, S, W), jnp.bfloat16)]
    out_specs = [lat]
    if need_ctx:
        in_specs.append(cx)
        args.append(qc)
        out_shape.append(jax.ShapeDtypeStruct((B, Tc, W), jnp.bfloat16))
        out_specs.append(cx)
    res = pl.pallas_call(
        functools.partial(_na_kernel, need_ctx=need_ctx),
        grid=(B,),
        in_specs=in_specs,
        out_specs=out_specs,
        out_shape=out_shape,
        scratch_shapes=[pltpu.VMEM((2, W // LANES, 2 * GRID_W, NA_KH * GRID_W + Tc), jnp.float32)],
        compiler_params=pltpu.CompilerParams(dimension_semantics=("arbitrary",), vmem_limit_bytes=VMEM_LIMIT),
        name="neighbourhood_attention",
    )(*args)
    return (res[0], res[1]) if need_ctx else (res[0], None)


def _split_bf16(x):
    hi = x.astype(jnp.bfloat16)
    r1 = x - hi.astype(jnp.float32)
    mid = r1.astype(jnp.bfloat16)
    lo = (r1 - mid.astype(jnp.float32)).astype(jnp.bfloat16)
    return hi, mid, lo


def _dot_f32(a, b):
    return jnp.dot(a, b, preferred_element_type=jnp.float32)


def _log_sigmoid(x):
    return jnp.minimum(x, 0.0) - jnp.log(1.0 + jnp.exp(-jnp.abs(x)))


def _pair_queries(qp):
    lane_lo = lax.broadcasted_iota(jnp.int32, qp.shape, 1) < M_DQK
    zq = jnp.zeros_like(qp)
    return jnp.concatenate([jnp.where(lane_lo, qp, zq), jnp.where(lane_lo, zq, qp)], axis=0)


def _mlstm_prep(qk_ref, g_ref, gt_ref, sa_ref, ra_ref, slot, ci, fwd):
    L = M_CHUNK
    f32, bf16 = jnp.float32, jnp.bfloat16
    d = 0 if fwd else 1
    rows = pl.ds(pl.multiple_of(ci * L, L), L)
    sq_r = lax.broadcasted_iota(jnp.int32, (L, L), 0)
    sq_c = lax.broadcasted_iota(jnp.int32, (L, L), 1)
    tri_l = jnp.where(sq_r >= sq_c, 1.0, 0.0).astype(bf16)
    tri_u = jnp.where(sq_r <= sq_c, 1.0, 0.0).astype(bf16)
    gates = g_ref[0, rows, :]
    gates_t = gt_ref[0, ci]
    lf, lf_t = _log_sigmoid(gates), _log_sigmoid(gates_t)
    cum = sum(_dot_f32(tri_l if fwd else tri_u, p) for p in _split_bf16(lf))
    cum_t = sum(_dot_f32(p, tri_u if fwd else tri_l) for p in _split_bf16(lf_t))
    t_i = 0 if fwd else 2
    st_s = lax.broadcasted_iota(jnp.int32, (L, 2 * L), 0)
    st_t = lax.broadcasted_iota(jnp.int32, (L, 2 * L), 1) & (L - 1)
    valid = (st_s <= st_t) if fwd else (st_s >= st_t)

    def row2(tile, c0, c1):
        return jnp.concatenate([tile[c0:c0 + 1, :], tile[c1:c1 + 1, :]], axis=-1)

    for j in range(M_HEADS // 2):
        ci0, ci1 = t_i * M_HEADS + 2 * j, t_i * M_HEADS + 2 * j + 1
        cf0, cf1 = ci0 + M_HEADS, ci1 + M_HEADS
        b_row, i_row = row2(cum_t, cf0, cf1), row2(gates_t, ci0, ci1)
        colb = jnp.concatenate([jnp.broadcast_to(gates[:, ci0:ci0 + 1] - cum[:, cf0:cf0 + 1], (L, L)),
                                jnp.broadcast_to(gates[:, ci1:ci1 + 1] - cum[:, cf1:cf1 + 1], (L, L))],
                               axis=-1)
        qp = qk_ref[0, rows, j * LANES:(j + 1) * LANES]
        kp = qk_ref[0, rows, M_HEADS * M_DQK + j * LANES:M_HEADS * M_DQK + (j + 1) * LANES]
        sa_ref[slot, d, j, 0] = jnp.where(valid, b_row + colb, MASK_NEG)
        sa_ref[slot, d, j, 1] = lax.dot_general(kp, _pair_queries(qp), (((1,), (1,)), ((), ())),
                                                preferred_element_type=f32)
        ra_ref[slot, d, j, 0:1, :] = b_row
        ra_ref[slot, d, j, 1:2, :] = i_row


def _mlstm_step(qk_ref, vt_ref, sa_ref, ra_ref, slot, h_ref, ct_ref, m_ref, ci, fwd):
    L = M_CHUNK
    f32, bf16 = jnp.float32, jnp.bfloat16
    d = 0 if fwd else 1
    rows = pl.ds(pl.multiple_of(ci * L, L), L)
    last = L - 1 if fwd else 0
    lane_lo = lax.broadcasted_iota(jnp.int32, (L, LANES), 1) < M_DQK
    lane_lo_row = lax.broadcasted_iota(jnp.int32, (1, LANES), 1) < M_DQK
    ones = jnp.ones((M_DV, L), bf16)

    def halves(row, f):
        return jnp.concatenate([jnp.broadcast_to(f(row[:, :L]), (1, L)), jnp.broadcast_to(f(row[:, L:]), (1, L))],
                               axis=-1)

    for j in range(M_HEADS // 2):
        h0, h1 = 2 * j, 2 * j + 1
        sidx = 2 * j + d
        b_row, i_row = ra_ref[slot, d, j, 0:1, :], ra_ref[slot, d, j, 1:2, :]
        m_row = m_ref[sidx, 0:1, :]
        kp = qk_ref[0, rows, M_HEADS * M_DQK + j * LANES:M_HEADS * M_DQK + (j + 1) * LANES]
        vx0 = jnp.concatenate([vt_ref[0, ci, h0 * M_DV:(h0 + 1) * M_DV, :], ones], axis=0)
        vx1 = jnp.concatenate([vt_ref[0, ci, h1 * M_DV:(h1 + 1) * M_DV, :], ones], axis=0)
        ct = ct_ref[sidx]
        if h_ref is not None:
            dm = sa_ref[slot, d, j, 0]
            qq = _pair_queries(qk_ref[0, rows, j * LANES:(j + 1) * LANES])
            g_row = b_row + m_row
            mt = jnp.maximum(g_row, dm.max(axis=0, keepdims=True))
            sc = (sa_ref[slot, d, j, 1] * jnp.exp(dm - mt)).astype(bf16)
            inter = jnp.exp(g_row - mt)
            t1 = lax.dot_general(ct.astype(bf16), qq, (((1,), (1,)), ((), ())), preferred_element_type=f32)
            t2 = jnp.concatenate([_dot_f32(vx0, sc[:, :L]), _dot_f32(vx1, sc[:, L:])], axis=-1)
            tot = inter * t1 + t2
            h_t = tot[:M_DV] / jnp.maximum(jnp.abs(tot[M_DV:]), jnp.exp(-mt))
            h_ref[rows, h0 * M_DV:(h0 + 1) * M_DV] = h_t[:, :L].T
            h_ref[rows, h1 * M_DV:(h1 + 1) * M_DV] = h_t[:, L:].T
        bl = halves(b_row, lambda r: r[:, last:last + 1])
        w = bl - b_row + i_row
        m_new = jnp.maximum(bl + m_row, halves(w, lambda r: r.max(axis=-1, keepdims=True)))
        decay = jnp.exp(bl + m_row - m_new)
        ws = jnp.exp(w - m_new)
        vxs = jnp.concatenate([vx0.astype(f32) * ws[:, :L], vx1.astype(f32) * ws[:, L:]], axis=-1).astype(bf16)
        zk = jnp.zeros_like(kp)
        kk = jnp.concatenate([jnp.where(lane_lo, kp, zk), jnp.where(lane_lo, zk, kp)], axis=0)
        d_cols = jnp.where(lane_lo_row, jnp.broadcast_to(decay[:, 0:1], (1, LANES)),
                           jnp.broadcast_to(decay[:, L:L + 1], (1, LANES)))
        ct_ref[sidx] = d_cols * ct + _dot_f32(vxs, kk)
        m_ref[sidx, 0:1, :] = m_new


def _mlstm_finish(hf_ref, hb_ref, o_ref, out_ref, n_chunks):
    L = M_CHUNK

    def body(ci, carry):
        rows = pl.ds(pl.multiple_of(ci * L, L), L)
        h = hf_ref[rows, :] + hb_ref[rows, :]
        parts = []
        for hd in range(M_HEADS):
            hh = h[:, hd * M_DV:(hd + 1) * M_DV]
            parts.append(hh * lax.rsqrt(jnp.mean(hh * hh, axis=-1, keepdims=True) + RMS_EPS))
        gate = _sigmoid(o_ref[0, rows, :].astype(jnp.float32))
        out_ref[0, rows, :] = (gate * jnp.concatenate(parts, axis=-1)).astype(out_ref.dtype)
        return carry

    lax.fori_loop(0, n_chunks, body, 0)


def _mlstm_kernel(qk_c, v_c, o_c, g_c, gt_c, qk_l, v_l, o_l, g_l, gt_l, *rest, need_ctx):
    if need_ctx:
        out_l, out_c, hf_l, hb_l, hf_c, hb_c, ct_ref, m_ref, sa_ref, ra_ref = rest
    else:
        out_l, hf_l, hb_l, ct_ref, m_ref, sa_ref, ra_ref = rest
        hf_c = hb_c = None
    nc, nl = qk_c.shape[1] // M_CHUNK, qk_l.shape[1] // M_CHUNK
    ct_ref[...] = jnp.zeros_like(ct_ref)
    m_ref[...] = jnp.full_like(m_ref, M_INIT)

    def phase(qk, vt, g, gt, hf, hb, n):
        def prep(s, slot):
            _mlstm_prep(qk, g, gt, sa_ref, ra_ref, slot, s, True)
            _mlstm_prep(qk, g, gt, sa_ref, ra_ref, slot, n - 1 - s, False)

        def step(s, slot):
            _mlstm_step(qk, vt, sa_ref, ra_ref, slot, hf, ct_ref, m_ref, s, True)
            _mlstm_step(qk, vt, sa_ref, ra_ref, slot, hb, ct_ref, m_ref, n - 1 - s, False)

        prep(0, 0)

        def body(i, carry):
            s0 = 2 * i
            prep(s0 + 1, 1)
            step(s0, 0)
            prep(jnp.minimum(s0 + 2, n - 1), 0)
            step(s0 + 1, 1)
            return carry
        lax.fori_loop(0, n // 2, body, 0)

    phase(qk_c, v_c, g_c, gt_c, hf_c, hb_c, nc)
    phase(qk_l, v_l, g_l, gt_l, hf_l, hb_l, nl)
    _mlstm_finish(hf_l, hb_l, o_l, out_l, nl)
    if need_ctx:
        _mlstm_finish(hf_c, hb_c, o_c, out_c, nc)


def _gates_chunk_major(g):
    B, T, _ = g.shape
    return g[..., :4 * M_HEADS].reshape(B, T // M_CHUNK, M_CHUNK, 4 * M_HEADS).transpose(0, 1, 3, 2)


def mlstm(pc, pl_, need_ctx):
    B, S, W = pl_["mo"].shape
    Tc = pc["mo"].shape[1]
    f32 = jnp.float32

    def specs(T):
        s = pl.BlockSpec((1, T, W), lambda b: (b, 0, 0))
        return [s, pl.BlockSpec((1, T // M_CHUNK, W, M_CHUNK), lambda b: (b, 0, 0, 0)), s,
                pl.BlockSpec((1, T, GATE_W), lambda b: (b, 0, 0)),
                pl.BlockSpec((1, T // M_CHUNK, 4 * M_HEADS, M_CHUNK), lambda b: (b, 0, 0, 0))]

    def args(p):
        return [p["mqk"], p["mvt"], p["mo"], p["g"], _gates_chunk_major(p["g"])]

    out_shape = [jax.ShapeDtypeStruct((B, S, W), jnp.bfloat16)]
    out_specs = [pl.BlockSpec((1, S, W), lambda b: (b, 0, 0))]
    scratch = [pltpu.VMEM((S, W), f32), pltpu.VMEM((S, W), f32)]
    if need_ctx:
        out_shape.append(jax.ShapeDtypeStruct((B, Tc, W), jnp.bfloat16))
        out_specs.append(pl.BlockSpec((1, Tc, W), lambda b: (b, 0, 0)))
        scratch += [pltpu.VMEM((Tc, W), f32), pltpu.VMEM((Tc, W), f32)]
    assert (S // M_CHUNK) % 2 == 0 and (Tc // M_CHUNK) % 2 == 0
    n_pairs = M_HEADS // 2
    scratch += [pltpu.VMEM((M_HEADS, 2 * M_DV, LANES), f32), pltpu.VMEM((M_HEADS, 8, 2 * M_CHUNK), f32),
                pltpu.VMEM((2, 2, n_pairs, 2, M_CHUNK, 2 * M_CHUNK), f32),
                pltpu.VMEM((2, 2, n_pairs, 8, 2 * M_CHUNK), f32)]
    res = pl.pallas_call(
        functools.partial(_mlstm_kernel, need_ctx=need_ctx),
        grid=(B,),
        in_specs=specs(Tc) + specs(S),
        out_specs=out_specs,
        out_shape=out_shape,
        scratch_shapes=scratch,
        compiler_params=pltpu.CompilerParams(dimension_semantics=("arbitrary",), vmem_limit_bytes=VMEM_LIMIT),
        name="mlstm",
    )(*args(pc), *args(pl_))
    return (res[0], res[1]) if need_ctx else (res[0], None)


POOL_HALO = 16


def _pool_mix(pu_ref, prev_ref, next_ref, pw_ref, ps_ref, tile_idx, n_tiles, seq_len):
    f32 = jnp.float32
    tm = pu_ref.shape[1]
    cur = pu_ref[0].astype(f32)
    prev = jnp.where(tile_idx > 0, prev_ref[0].astype(f32), 0.0)
    nxt = jnp.where(tile_idx < n_tiles - 1, next_ref[0].astype(f32), 0.0)
    ext = jnp.concatenate([prev, cur, nxt], axis=0)
    n_ext = tm + 2 * POOL_HALO
    tok = tile_idx * tm + lax.broadcasted_iota(jnp.int32, (tm, POOL_GC), 0)

    def shifted(a, d):
        return pltpu.roll(a, (-d) % n_ext, 0)

    outs = []
    for gi, w in enumerate(POOL_WINDOWS):
        s = ext[:, gi * POOL_GC:(gi + 1) * POOL_GC]
        s = shifted(s, -1) + s
        span = 2
        while span < w:
            s = shifted(s, -(span // 2)) + shifted(s, span // 2)
            span *= 2
        lo = jnp.clip(tok - w // 2, 0, seq_len - 1)
        hi = jnp.clip(tok - w // 2 + w - 1, 0, seq_len - 1)
        mean = s[POOL_HALO:POOL_HALO + tm] / (hi - lo + 1).astype(f32)
        d = (mean - cur[:, gi * POOL_GC:(gi + 1) * POOL_GC]).astype(jnp.bfloat16)
        outs.append(_dot_f32(d, pw_ref[gi]))
    return jnp.concatenate(outs, axis=-1) * ps_ref[...]


def _pack_bf16_pairs(h):
    n = h.shape[1] // 2
    return pltpu.pack_elementwise([h[:, :n], h[:, n:]], packed_dtype=jnp.bfloat16)


def _merge_kernel(m_ref, n_ref, pu_ref, prev_ref, next_ref, bg_ref, x_ref, mods_ref, gains_ref,
                  pw_ref, ps_ref, wb_ref, wo_ref, wrt_ref, wrp_ref, x1_ref, h2a_ref, aff_ref, *, seq_len):
    f32, bf16 = jnp.float32, jnp.bfloat16
    i = pl.program_id(1)
    D = x_ref.shape[2]
    E = wrt_ref.shape[0]
    p = _pool_mix(pu_ref, prev_ref, next_ref, pw_ref, ps_ref, i, pl.num_programs(1), seq_len)
    branches = (m_ref[0], p.astype(bf16), n_ref[0])
    acc = None
    for bi, br in enumerate(branches):
        gate = jnp.tanh(bg_ref[0, :, bi * D:(bi + 1) * D].astype(f32)) + 1.0
        term = gate * _dot_f32(br, wb_ref[bi])
        acc = term if acc is None else acc + term
    y = _dot_f32(acc.astype(bf16), wo_ref[...])
    yn = y * lax.rsqrt(jnp.mean(y * y, axis=-1, keepdims=True) + RMS_EPS) * gains_ref[0:1, :]
    x1 = x_ref[0] + mods_ref[0, 0:1, :] * yn
    x1_ref[0] = x1
    xn = x1 * lax.rsqrt(jnp.mean(x1 * x1, axis=-1, keepdims=True) + RMS_EPS) * gains_ref[1:2, :]
    h2 = xn * (1.0 + mods_ref[0, 2:3, :]) + mods_ref[0, 1:2, :]
    h2b = h2.astype(bf16)
    lg_t = lax.dot_general(wrt_ref[...], h2b, (((1,), (1,)), ((), ())), preferred_element_type=f32)
    e_t = jnp.exp(lg_t - lg_t.max(axis=0, keepdims=True))
    aff_ref[0] = e_t / e_t.sum(axis=0, keepdims=True)
    lg = _dot_f32(h2b, wrp_ref[...])
    lg = jnp.where(lax.broadcasted_iota(jnp.int32, lg.shape, 1) < E, lg, MASK_NEG)
    e_r = jnp.exp(lg - lg.max(axis=-1, keepdims=True))
    aff_rows = e_r / e_r.sum(axis=-1, keepdims=True)
    h2a_ref[0] = jnp.concatenate([_pack_bf16_pairs(h2), pltpu.bitcast(aff_rows, jnp.uint32)], axis=-1)


def merge_out(m, n, pu, bg, x, mods, gains, pool_w, pool_scale, w_branch, w_out, w_router_t, w_router_pad, *, tm):
    B, T, D = x.shape
    W = m.shape[2]
    E = w_router_t.shape[0]
    hb = tm // POOL_HALO
    n_halo = T // POOL_HALO
    tile = lambda w: pl.BlockSpec((1, tm, w), lambda b, i: (b, i, 0))
    in_specs = [
        tile(W), tile(W), tile(W),
        pl.BlockSpec((1, POOL_HALO, W), lambda b, i: (b, jnp.maximum(i * hb - 1, 0), 0)),
        pl.BlockSpec((1, POOL_HALO, W), lambda b, i: (b, jnp.minimum((i + 1) * hb, n_halo - 1), 0)),
        tile(N_BRANCH * D), tile(D),
        pl.BlockSpec((1, 8, D), lambda b, i: (b, 0, 0)),
        _const_spec((8, D)),
        _const_spec(pool_w.shape), _const_spec(pool_scale.shape), _const_spec(w_branch.shape),
        _const_spec(w_out.shape), _const_spec(w_router_t.shape), _const_spec(w_router_pad.shape),
    ]
    return pl.pallas_call(
        functools.partial(_merge_kernel, seq_len=T),
        grid=(B, T // tm),
        in_specs=in_specs,
        out_specs=[tile(D), tile(MOE_ROW), pl.BlockSpec((1, E, tm), lambda b, i: (b, 0, i))],
        out_shape=[jax.ShapeDtypeStruct((B, T, D), jnp.float32),
                   jax.ShapeDtypeStruct((B, T, MOE_ROW), jnp.uint32),
                   jax.ShapeDtypeStruct((B, E, T), jnp.float32)],
        compiler_params=pltpu.CompilerParams(
            dimension_semantics=("arbitrary", "arbitrary"), vmem_limit_bytes=VMEM_LIMIT),
        name="merge_out",
    )(m, n, pu, pu, pu, bg, x, mods, gains, pool_w, pool_scale, w_branch, w_out, w_router_t, w_router_pad)


def _cumsum_lanes(x01):
    R, T = x01.shape
    r = lax.broadcasted_iota(jnp.int32, (LANES, LANES), 0)
    c = lax.broadcasted_iota(jnp.int32, (LANES, LANES), 1)
    tri_u = jnp.where(r <= c, 1.0, 0.0).astype(jnp.bfloat16)
    run = jnp.zeros((R, 1), jnp.float32)
    outs = []
    for t0 in range(0, T, LANES):
        cs = _dot_f32(x01[:, t0:t0 + LANES].astype(jnp.bfloat16), tri_u) + run
        run = cs[:, LANES - 1:LANES]
        outs.append(cs)
    return jnp.concatenate(outs, axis=-1)


def _route_kernel(aff_ref, idx_ref, *, cap):
    f32 = jnp.float32
    aff = aff_ref[0]
    E, T = aff.shape

    def search(k, lo):
        cand = lo | (jnp.int32(1) << (30 - k))
        cnt = jnp.sum(jnp.where(aff >= pltpu.bitcast(cand, f32), 1.0, 0.0), axis=-1, keepdims=True)
        return jnp.where(cnt >= cap, cand, lo)

    thr = pltpu.bitcast(lax.fori_loop(0, 31, search, jnp.zeros((E, 1), jnp.int32)), f32)
    gt = jnp.where(aff > thr, 1.0, 0.0)
    eq = jnp.where(aff == thr, 1.0, 0.0)
    room = cap - jnp.sum(gt, axis=-1, keepdims=True)
    sel = gt + eq * jnp.where(_cumsum_lanes(eq) <= room, 1.0, 0.0)
    csel = _cumsum_lanes(sel)
    bf16 = jnp.bfloat16
    rank = jnp.minimum(csel, float(cap)).astype(bf16)
    slot = lax.broadcasted_iota(jnp.int32, (cap, T), 0).astype(f32).astype(bf16)
    ones = jnp.ones((T, LANES), bf16)
    lane = lax.broadcasted_iota(jnp.int32, (cap, LANES), 1)
    out = jnp.zeros((cap, LANES), f32)
    for e in range(E):
        below = jnp.where(jnp.broadcast_to(rank[e:e + 1, :], (cap, T)) <= slot,
                          jnp.ones((cap, T), bf16), jnp.zeros((cap, T), bf16))
        out = jnp.where(lane == e, _dot_f32(below, ones), out)
    idx_ref[0] = out.astype(jnp.int32)


def route(aff_t, cap):
    B, E, T = aff_t.shape
    assert cap <= 256, "slot ranks are compared in bf16, exact only up to 256"
    idx_t = pl.pallas_call(
        functools.partial(_route_kernel, cap=cap),
        grid=(B,),
        in_specs=[pl.BlockSpec((1, E, T), lambda b: (b, 0, 0))],
        out_specs=pl.BlockSpec((1, cap, LANES), lambda b: (b, 0, 0)),
        out_shape=jax.ShapeDtypeStruct((B, cap, LANES), jnp.int32),
        compiler_params=pltpu.CompilerParams(dimension_semantics=("arbitrary",), vmem_limit_bytes=VMEM_LIMIT),
        name="route",
    )(aff_t)
    return idx_t[:, :, :E].transpose(0, 2, 1)


ROW_GROUP = 8


def _moe_kernel(idx_ref, rows_ref, wg_ref, wu_ref, wd_ref, out_ref, xg0_ref, xg1_ref, ye0_ref, ye1_ref):
    f32, bf16 = jnp.float32, jnp.bfloat16
    e = pl.program_id(1)
    n_exp = pl.num_programs(1)
    cap = xg0_ref.shape[0]
    half = wg_ref.shape[1] // 2

    def gather_row(dst_ref, ee, j):
        dst_ref[pl.ds(j, 1), :] = rows_ref[0, pl.ds(idx_ref[0, 0, ee, j], 1), :]

    def scatter_group(src_ref, ee, j0):
        ids = [idx_ref[0, 0, ee, j0 + r] for r in range(ROW_GROUP)]
        cur = [out_ref[0, pl.ds(i, 1), :] for i in ids]
        for r, i in enumerate(ids):
            out_ref[0, pl.ds(i, 1), :] = cur[r] + src_ref[pl.ds(j0 + r, 1), :]

    @pl.when(e == 0)
    def _():
        out_ref[...] = jnp.zeros_like(out_ref)
        ye1_ref[...] = jnp.zeros_like(ye1_ref)

        def body(gi, carry):
            for r in range(ROW_GROUP):
                gather_row(xg0_ref, 0, gi * ROW_GROUP + r)
            return carry
        lax.fori_loop(0, cap // ROW_GROUP, body, 0)

    def step(xg_cur, xg_nxt, ye_cur, ye_prv):
        e_prv = jnp.maximum(e - 1, 0)
        e_nxt = jnp.minimum(e + 1, n_exp - 1)
        for j0 in range(0, cap, ROW_GROUP):
            scatter_group(ye_prv, e_prv, j0)
        for j in range(cap):
            gather_row(xg_nxt, e_nxt, j)
        xg = xg_cur[...]
        packed = xg[:, :half]
        xe = jnp.concatenate(
            [pltpu.unpack_elementwise(packed, index=i, packed_dtype=bf16, unpacked_dtype=f32) for i in (0, 1)],
            axis=-1).astype(bf16)
        aff = pltpu.bitcast(xg[:, half:], f32)
        gate = jnp.sum(jnp.where(lax.broadcasted_iota(jnp.int32, aff.shape, 1) == e, aff, 0.0),
                       axis=-1, keepdims=True)
        a = _dot_f32(xe, wg_ref[0])
        hid = (a * _sigmoid(a) * _dot_f32(xe, wu_ref[0])).astype(bf16)
        ye_cur[...] = _dot_f32(hid, wd_ref[0]) * gate

    @pl.when(e % 2 == 0)
    def _():
        step(xg0_ref, xg1_ref, ye0_ref, ye1_ref)

    @pl.when(e % 2 == 1)
    def _():
        step(xg1_ref, xg0_ref, ye1_ref, ye0_ref)

    @pl.when(e == n_exp - 1)
    def _():
        def body(gi, carry):
            scatter_group(ye1_ref, e, gi * ROW_GROUP)
            return carry
        lax.fori_loop(0, cap // ROW_GROUP, body, 0)


def moe_experts(rows, idx, w_gate, w_up, w_down):
    G, T, RW = rows.shape
    _, E, cap = idx.shape
    D = w_gate.shape[1]
    assert E % 2 == 0 and cap % ROW_GROUP == 0
    wspec = pl.BlockSpec((1, D, D), lambda g, e: (e, 0, 0))
    return pl.pallas_call(
        _moe_kernel,
        grid=(G, E),
        in_specs=[pl.BlockSpec((1, 1, E, cap), lambda g, e: (g, 0, 0, 0), memory_space=pltpu.SMEM),
                  pl.BlockSpec((1, T, RW), lambda g, e: (g, 0, 0)), wspec, wspec, wspec],
        out_specs=pl.BlockSpec((1, T, D), lambda g, e: (g, 0, 0)),
        out_shape=jax.ShapeDtypeStruct((G, T, D), jnp.float32),
        scratch_shapes=[pltpu.VMEM((cap, RW), jnp.uint32), pltpu.VMEM((cap, RW), jnp.uint32),
                        pltpu.VMEM((cap, D), jnp.float32), pltpu.VMEM((cap, D), jnp.float32)],
        compiler_params=pltpu.CompilerParams(
            dimension_semantics=("arbitrary", "arbitrary"), vmem_limit_bytes=VMEM_LIMIT),
        name="moe_experts",
    )(idx.reshape(G, 1, E, cap), rows, w_gate, w_up, w_down)


def _residual_norm_kernel(x_ref, y_ref, gate_ref, gain_ref, o_ref):
    y = y_ref[0]
    yn = y * lax.rsqrt(jnp.mean(y * y, axis=-1, keepdims=True) + RMS_EPS) * gain_ref[...]
    o_ref[0] = x_ref[0] + gate_ref[0] * yn


def residual_norm(x, y, gate, gain, *, tm):
    B, T, D = x.shape
    tile = pl.BlockSpec((1, tm, D), lambda b, i: (b, i, 0))
    return pl.pallas_call(
        _residual_norm_kernel,
        grid=(B, T // tm),
        in_specs=[tile, tile, pl.BlockSpec((1, 1, D), lambda b, i: (b, 0, 0)), _const_spec((1, D))],
        out_specs=tile,
        out_shape=jax.ShapeDtypeStruct((B, T, D), jnp.float32),
        compiler_params=pltpu.CompilerParams(
            dimension_semantics=("arbitrary", "arbitrary"), vmem_limit_bytes=VMEM_LIMIT),
        name="residual_norm",
    )(x, y, gate, gain)


ADA_COLS = 1536


def _ada_kernel(c_ref, w_ref, b_ref, o_ref):
    c = c_ref[...]
    h = (c * _sigmoid(c)).astype(jnp.bfloat16)
    o_ref[...] = _dot_f32(h, w_ref[...].astype(jnp.bfloat16)) + b_ref[...]


def ada_modulation(cond, ada_w_l, ada_b_l):
    R, D = cond.shape
    N = ada_w_l.shape[1]
    return pl.pallas_call(
        _ada_kernel,
        grid=(N // ADA_COLS,),
        in_specs=[pl.BlockSpec((R, D), lambda j: (0, 0)), pl.BlockSpec((D, ADA_COLS), lambda j: (0, j)),
                  pl.BlockSpec((1, ADA_COLS), lambda j: (0, j))],
        out_specs=pl.BlockSpec((R, ADA_COLS), lambda j: (0, j)),
        out_shape=jax.ShapeDtypeStruct((R, N), jnp.float32),
        compiler_params=pltpu.CompilerParams(dimension_semantics=("arbitrary",), vmem_limit_bytes=VMEM_LIMIT),
        name="ada_modulation",
    )(cond, ada_w_l, ada_b_l[None])


def kernel(x, c, ctx, c_ctx, norm_gain, ada_w, ada_b, w_in, mlstm_gate_bias, pool_w, pool_scale,
           na_rpb, w_branch, w_out, router_w, w_gate, w_up, w_down):
    B, S, D = x.shape
    Tc = ctx.shape[1]
    f32, bf16 = jnp.float32, jnp.bfloat16
    rope_tabs = rope_lane_tables(S)
    names = [n for n, _ in IN_GROUPS]
    cond = jnp.pad(jnp.concatenate([c, c_ctx[None]], axis=0), ((0, (-(B + 1)) % 8), (0, 0)))
    pending = None
    for l in range(DEPTH):
        need_ctx = l < DEPTH - 1
        g = norm_gain[l]
        mods = ada_modulation(cond, ada_w[l], ada_b[l])
        mod_l = jnp.split(mods[:B, None, :], N_MOD, axis=-1)
        mod_cb = [jnp.broadcast_to(m[None, None], (B, 1, D)) for m in jnp.split(mods[B], N_MOD, axis=-1)]
        w_perm = permute_w_in(w_in[l])
        gb_row = jnp.pad(mlstm_gate_bias[l].reshape(1, -1).astype(f32), ((0, 0), (0, GATE_W - 4 * M_HEADS)))
        proj = in_proj(x, g[0][None], mod_l[0], mod_l[1], w_perm, gb_row, rope_tabs, pending, tm=512)
        if pending is not None:
            x = proj[-1]
        pl_ = dict(zip(names, proj))
        pc_ = dict(zip(names, in_proj(ctx, g[0][None], mod_cb[0], mod_cb[1], w_perm, gb_row, tm=Tc)))

        m_l, m_c = mlstm(pc_, pl_, need_ctx)
        n_l, n_c = neighbourhood_attention(pl_["qn"], pl_["kn"], pl_["vn"], pc_["kn"], pc_["vn"],
                                           na_bias_table(na_rpb[l]), pc_["qn"] if need_ctx else None)
        gains = jnp.pad(g[1:3], ((0, 6), (0, 0)))
        merge_w = (pool_w[l].astype(bf16), pool_scale[l][None], w_branch[l].astype(bf16),
                   (0.5 * w_out[l]).astype(bf16),
                   router_w[l].T.astype(bf16),
                   jnp.pad(router_w[l], ((0, 0), (0, LANES - N_EXPERTS))).astype(bf16))
        moe_w = (w_gate[l].astype(bf16), w_up[l].astype(bf16), w_down[l].astype(bf16))

        def mods8(mods):
            return jnp.pad(jnp.concatenate([mods[2], mods[3], mods[4]], axis=1), ((0, 0), (0, 5), (0, 0)))

        x1, rows, aff_t = merge_out(m_l, n_l, pl_["pu"], pl_["bg"], x, mods8(mod_l), gains, *merge_w, tm=512)
        y = moe_experts(rows, route(aff_t, EC_FACTOR * S // N_EXPERTS), *moe_w)
        if need_ctx:
            x, pending = x1, (y, mod_l[5], g[3][None])
        else:
            x = residual_norm(x1, y, mod_l[5], g[3][None], tm=512)
        if need_ctx:
            c1, rows, aff_t = merge_out(m_c, n_c, pc_["pu"], pc_["bg"], ctx, mods8(mod_cb), gains, *merge_w, tm=Tc)
            per = S // Tc
            cap_c = EC_FACTOR * Tc // N_EXPERTS
            idx = route(aff_t, cap_c) + (jnp.arange(B, dtype=jnp.int32) % per * Tc)[:, None, None]
            idx = idx.reshape(B // per, per, N_EXPERTS, cap_c).transpose(0, 2, 1, 3).reshape(B // per, N_EXPERTS, per * cap_c)
            y = moe_experts(rows.reshape(B // per, S, MOE_ROW), idx, *moe_w).reshape(B, Tc, D)
            ctx = residual_norm(c1, y, mod_cb[5], g[3][None], tm=Tc)
    return x
```

```python
import functools
import math

import jax
import jax.numpy as jnp
import numpy as np
from jax import lax
from jax.experimental import pallas as pl
from jax.experimental.pallas import tpu as pltpu

D_MODEL = 1024
DEPTH = 2
GRID_W = 64
BRANCH_W = D_MODEL // 2
N_BRANCH = 3
M_HEADS = 4
M_DV = BRANCH_W // M_HEADS
M_DQK = M_DV // 2
M_CHUNK = 128
M_INIT = -1e30
POOL_GROUPS = 4
POOL_GC = BRANCH_W // POOL_GROUPS
POOL_WINDOWS = (2, 4, 8, 16)
NA_HEADS = 8
NA_DH = BRANCH_W // NA_HEADS
NA_KH = 8
NA_KW = 16
N_EXPERTS = 16
EC_FACTOR = 2
ROPE_BASE = 10000.0
RMS_EPS = 1e-6
N_MOD = 6
SPLIT_SIZES = (M_HEADS * M_DQK, M_HEADS * M_DQK, M_HEADS * M_DV, M_HEADS * M_DV, 4 * M_HEADS,
               POOL_GROUPS * POOL_GC, NA_HEADS * NA_DH, NA_HEADS * NA_DH, NA_HEADS * NA_DH,
               N_BRANCH * D_MODEL)
PROJ_W = sum(SPLIT_SIZES)

LANES = 128
VMEM_LIMIT = 56 * 1024 * 1024
MASK_NEG = -1e30
GATE_W = LANES

IN_GROUPS = (("mqk", 2 * M_HEADS * M_DQK), ("mvt", BRANCH_W), ("mo", BRANCH_W), ("pu", BRANCH_W),
             ("qn", BRANCH_W), ("kn", BRANCH_W), ("vn", BRANCH_W), ("bg", N_BRANCH * D_MODEL),
             ("g", GATE_W))
IN_W = sum(w for _, w in IN_GROUPS)
MM_COLS = 512
PACK_WORD = jnp.uint32


def _sigmoid(x):
    return 0.5 * jnp.tanh(0.5 * x) + 0.5


def _const_spec(shape):
    nd = len(shape)
    return pl.BlockSpec(shape, lambda *_: (0,) * nd, pipeline_mode=pl.Buffered(1))


def _rope_tile(x, cos, sin_signed):
    half = M_DQK // 4
    lane = lax.broadcasted_iota(jnp.int32, x.shape, 1)
    partner = jnp.where((lane % (2 * half)) < half, pltpu.roll(x, LANES - half, 1), pltpu.roll(x, half, 1))
    return x * cos + partner * sin_signed


def _in_proj_kernel(x_ref, g_ref, shift_ref, scale_ref, w_ref, gb_ref, *rest, rope, pending):
    if rope:
        cos_ref, sin_ref = rest[:2]
        rest = rest[2:]
    if pending:
        r_ref, rgate_ref, rgain_ref = rest[:3]
        o_refs, xo_ref = rest[3:-1], rest[-1]
        r = r_ref[0]
        x = x_ref[0] + rgate_ref[0] * (r * lax.rsqrt(jnp.mean(r * r, axis=-1, keepdims=True) + RMS_EPS)
                                       * rgain_ref[...])
        xo_ref[0] = x
    else:
        o_refs = rest
        x = x_ref[0]
    y = x * lax.rsqrt(jnp.mean(x * x, axis=-1, keepdims=True) + RMS_EPS)
    h = ((y * g_ref[...]) * (1.0 + scale_ref[0]) + shift_ref[0]).astype(jnp.bfloat16)
    off = 0
    for (name, width), o_ref in zip(IN_GROUPS, o_refs):
        for c0 in range(0, width, MM_COLS):
            cw = min(MM_COLS, width - c0)
            acc = jnp.dot(h, w_ref[:, off + c0:off + c0 + cw], preferred_element_type=jnp.float32)
            if name == "g":
                acc = acc + gb_ref[...]
            if name == "mqk" and rope:
                cos, sin = cos_ref[...], sin_ref[...]
                acc = jnp.concatenate([_rope_tile(acc[:, t:t + LANES], cos, sin) for t in range(0, cw, LANES)], axis=-1)
            if name == "mvt":
                for c in range(acc.shape[0] // M_CHUNK):
                    o_ref[0, c] = acc[c * M_CHUNK:(c + 1) * M_CHUNK, :].T.astype(o_ref.dtype)
                continue
            o_ref[0, :, c0:c0 + cw] = acc.astype(o_ref.dtype)
        off += width


def in_proj(x, g, shift, scale, w_perm, gate_bias_row, rope_tabs=None, pending=None, *, tm):
    B, T, D = x.shape
    out_shape = [jax.ShapeDtypeStruct((B, T, w), jnp.float32 if n == "g" else jnp.bfloat16) for n, w in IN_GROUPS]
    out_specs = [pl.BlockSpec((1, tm, w), lambda b, i: (b, i, 0)) for _, w in IN_GROUPS]
    k_mvt = [n for n, _ in IN_GROUPS].index("mvt")
    assert BRANCH_W == MM_COLS and tm % M_CHUNK == 0
    out_shape[k_mvt] = jax.ShapeDtypeStruct((B, T // M_CHUNK, BRANCH_W, M_CHUNK), jnp.bfloat16)
    out_specs[k_mvt] = pl.BlockSpec((1, tm // M_CHUNK, BRANCH_W, M_CHUNK), lambda b, i: (b, i, 0, 0))
    in_specs = [
        pl.BlockSpec((1, tm, D), lambda b, i: (b, i, 0)),
        _const_spec((1, D)),
        pl.BlockSpec((1, 1, D), lambda b, i: (b, 0, 0)),
        pl.BlockSpec((1, 1, D), lambda b, i: (b, 0, 0)),
        _const_spec((D, IN_W)),
        _const_spec((1, GATE_W)),
    ]
    args = [x, g, shift, scale, w_perm, gate_bias_row]
    if rope_tabs is not None:
        in_specs += [pl.BlockSpec((tm, LANES), lambda b, i: (i, 0))] * 2
        args += list(rope_tabs)
    if pending is not None:
        in_specs += [pl.BlockSpec((1, tm, D), lambda b, i: (b, i, 0)), pl.BlockSpec((1, 1, D), lambda b, i: (b, 0, 0)),
                     _const_spec((1, D))]
        args += list(pending)
        out_shape.append(jax.ShapeDtypeStruct((B, T, D), jnp.float32))
        out_specs.append(pl.BlockSpec((1, tm, D), lambda b, i: (b, i, 0)))
    return pl.pallas_call(
        functools.partial(_in_proj_kernel, rope=rope_tabs is not None, pending=pending is not None),
        grid=(B, T // tm),
        in_specs=in_specs,
        out_specs=out_specs,
        out_shape=out_shape,
        compiler_params=pltpu.CompilerParams(
            dimension_semantics=("arbitrary", "arbitrary"), vmem_limit_bytes=VMEM_LIMIT),
        name="in_proj",
    )(*args)


def rope_lane_tables(n_tokens):
    t = jnp.arange(n_tokens)
    row = (t // GRID_W).astype(jnp.float32)
    col = (t % GRID_W).astype(jnp.float32)
    half = M_DQK // 2
    inv = ROPE_BASE ** (-jnp.arange(0, half, 2, dtype=jnp.float32) / half)
    ar = row[:, None] * inv[None, :]
    ac = col[:, None] * inv[None, :]
    cos = jnp.concatenate([jnp.cos(ar), jnp.cos(ar), jnp.cos(ac), jnp.cos(ac)], axis=-1)
    sin = jnp.concatenate([-jnp.sin(ar), jnp.sin(ar), -jnp.sin(ac), jnp.sin(ac)], axis=-1)
    return jnp.tile(cos, (1, LANES // M_DQK)), jnp.tile(sin, (1, LANES // M_DQK))


def permute_w_in(w_in_l):
    qm, km, vm, om, gm, pu, qn, kn, vn, bg = jnp.split(w_in_l, [int(s) for s in np.cumsum(SPLIT_SIZES)[:-1]], axis=-1)
    gpad = jnp.pad(gm, ((0, 0), (0, GATE_W - gm.shape[1])))
    w = jnp.concatenate([qm * (M_DQK ** -0.5), km, vm, om, pu, qn * (NA_DH ** -0.5), kn, vn, bg * 0.5, gpad], axis=-1)
    return w.astype(jnp.bfloat16)


def _pair_scores(qp, k_parts, biases):
    n = qp.shape[0]
    lane_lo = lax.broadcasted_iota(jnp.int32, (n, LANES), 1) < NA_DH
    zero = jnp.zeros_like(qp)
    qq = jnp.concatenate([jnp.where(lane_lo, qp, zero), jnp.where(lane_lo, zero, qp)], axis=0)
    scores = []
    for kp, bias in zip(k_parts, biases):
        s = lax.dot_general(qq, kp, (((1,), (1,)), ((), ())), preferred_element_type=jnp.float32)
        scores.append(s if bias is None else s + bias)
    return scores


def _pair_softmax_pv(scores, v_parts):
    n = scores[0].shape[0] // 2
    lane_lo = lax.broadcasted_iota(jnp.int32, (n, LANES), 1) < NA_DH
    m = scores[0].max(axis=-1, keepdims=True)
    for s in scores[1:]:
        m = jnp.maximum(m, s.max(axis=-1, keepdims=True))
    l = None
    o = None
    for s, vp in zip(scores, v_parts):
        p = jnp.exp(s - m)
        ls = p.sum(axis=-1, keepdims=True)
        os_ = jnp.dot(p.astype(jnp.bfloat16), vp, preferred_element_type=jnp.float32)
        l = ls if l is None else l + ls
        o = os_ if o is None else o + os_
    o = o * (1.0 / l)
    return jnp.where(lane_lo, o[:n], o[n:])


def _na_kernel(q_ref, k_ref, v_ref, kc_ref, vc_ref, bias_ref, *rest, need_ctx):
    if need_ctx:
        qc_ref, o_ref, oc_ref, s_ref = rest
    else:
        o_ref, s_ref = rest
    S = q_ref.shape[1]
    Tc = kc_ref.shape[1]
    rows = S // GRID_W
    n_pairs = BRANCH_W // LANES
    n_loc = NA_KH * GRID_W

    def window(r):
        rs = jnp.clip(r - NA_KH // 2, 0, rows - NA_KH)
        return r - rs, pl.multiple_of(r * GRID_W, GRID_W), pl.multiple_of(rs * GRID_W, GRID_W)

    def scores_stage(r, slot):
        var, q0, k0 = window(r)
        for j in range(n_pairs):
            ls = slice(j * LANES, (j + 1) * LANES)
            s_loc, s_ctx = _pair_scores(q_ref[0, pl.ds(q0, GRID_W), ls],
                                        (k_ref[0, pl.ds(k0, n_loc), ls], kc_ref[0, :, ls]),
                                        (bias_ref[var, j], None))
            s_ref[slot, j, :, :n_loc] = s_loc
            s_ref[slot, j, :, n_loc:] = s_ctx

    def output_stage(r, slot):
        _, q0, k0 = window(r)
        outs = []
        for j in range(n_pairs):
            ls = slice(j * LANES, (j + 1) * LANES)
            outs.append(_pair_softmax_pv((s_ref[slot, j, :, :n_loc], s_ref[slot, j, :, n_loc:]),
                                         (v_ref[0, pl.ds(k0, n_loc), ls], vc_ref[0, :, ls])))
        o_ref[0, pl.ds(q0, GRID_W), :] = jnp.concatenate(outs, axis=-1).astype(o_ref.dtype)

    scores_stage(0, 0)

    def two_rows(i, carry):
        r0 = 2 * i
        scores_stage(r0 + 1, 1)
        output_stage(r0, 0)
        scores_stage(jnp.minimum(r0 + 2, rows - 1), 0)
        output_stage(r0 + 1, 1)
        return carry

    lax.fori_loop(0, rows // 2, two_rows, 0)

    if need_ctx:
        outs = []
        for j in range(n_pairs):
            ls = slice(j * LANES, (j + 1) * LANES)
            scores = _pair_scores(qc_ref[0, :, ls], (kc_ref[0, :, ls],), (None,))
            outs.append(_pair_softmax_pv(scores, (vc_ref[0, :, ls],)))
        oc_ref[0] = jnp.concatenate(outs, axis=-1).astype(oc_ref.dtype)


def na_bias_table(rpb):
    H = rpb.shape[0]
    var = jnp.arange(NA_KH)
    kr = jnp.arange(NA_KH)
    dr = kr[None, :] - var[:, None] + NA_KH - 1
    cols = jnp.arange(GRID_W)
    dc = jnp.clip(cols[None, :] - cols[:, None] + NA_KW - 1, 0, 2 * NA_KW - 2)
    cs = jnp.clip(cols - NA_KW // 2, 0, GRID_W - NA_KW)
    colmask = (cols[None, :] >= cs[:, None]) & (cols[None, :] < cs[:, None] + NA_KW)
    tab = rpb[:, dr][:, :, :, dc]
    tab = jnp.where(colmask[None, None, None], tab.astype(jnp.float32), MASK_NEG)
    tab = tab.transpose(1, 0, 3, 2, 4)
    return tab.reshape(NA_KH, H // 2, 2 * GRID_W, NA_KH * GRID_W)


def neighbourhood_attention(qn, kn, vn, kc, vc, bias_tab, qc=None):
    B, S, W = qn.shape
    Tc = kc.shape[1]
    need_ctx = qc is not None
    lat = pl.BlockSpec((1, S, W), lambda b: (b, 0, 0))
    cx = pl.BlockSpec((1, Tc, W), lambda b: (b, 0, 0))
    in_specs = [lat, lat, lat, cx, cx, _const_spec(bias_tab.shape)]
    args = [qn, kn, vn, kc, vc, bias_tab]
    out_shape = [jax.ShapeDtypeStruct((B, S, W), jnp.bfloat16)]
    out_specs = [lat]
    if need_ctx:
        in_specs.append(cx)
        args.append(qc)
        out_shape.append(jax.ShapeDtypeStruct((B, Tc, W), jnp.bfloat16))
        out_specs.append(cx)
    res = pl.pallas_call(
        functools.partial(_na_kernel, need_ctx=need_ctx),
        grid=(B,),
        in_specs=in_specs,
        out_specs=out_specs,
        out_shape=out_shape,
        scratch_shapes=[pltpu.VMEM((2, W // LANES, 2 * GRID_W, NA_KH * GRID_W + Tc), jnp.float32)],
        compiler_params=pltpu.CompilerParams(dimension_semantics=("arbitrary",), vmem_limit_bytes=VMEM_LIMIT),
        name="neighbourhood_attention",
    )(*args)
    return (res[0], res[1]) if need_ctx else (res[0], None)


def _split_bf16(x):
    hi = x.astype(jnp.bfloat16)
    r1 = x - hi.astype(jnp.float32)
    mid = r1.astype(jnp.bfloat16)
    lo = (r1 - mid.astype(jnp.float32)).astype(jnp.bfloat16)
    return hi, mid, lo


def _dot_f32(a, b):
    return jnp.dot(a, b, preferred_element_type=jnp.float32)


def _log_sigmoid(x):
    return jnp.minimum(x, 0.0) - jnp.log(1.0 + jnp.exp(-jnp.abs(x)))


def _pair_queries(qp):
    lane_lo = lax.broadcasted_iota(jnp.int32, qp.shape, 1) < M_DQK
    zq = jnp.zeros_like(qp)
    return jnp.concatenate([jnp.where(lane_lo, qp, zq), jnp.where(lane_lo, zq, qp)], axis=0)


def _mlstm_prep(qk_ref, g_ref, gt_ref, sa_ref, ra_ref, slot, ci, fwd):
    L = M_CHUNK
    f32, bf16 = jnp.float32, jnp.bfloat16
    d = 0 if fwd else 1
    rows = pl.ds(pl.multiple_of(ci * L, L), L)
    sq_r = lax.broadcasted_iota(jnp.int32, (L, L), 0)
    sq_c = lax.broadcasted_iota(jnp.int32, (L, L), 1)
    tri_l = jnp.where(sq_r >= sq_c, 1.0, 0.0).astype(bf16)
    tri_u = jnp.where(sq_r <= sq_c, 1.0, 0.0).astype(bf16)
    gates = g_ref[0, rows, :]
    gates_t = gt_ref[0, ci]
    lf, lf_t = _log_sigmoid(gates), _log_sigmoid(gates_t)
    cum = sum(_dot_f32(tri_l if fwd else tri_u, p) for p in _split_bf16(lf))
    cum_t = sum(_dot_f32(p, tri_u if fwd else tri_l) for p in _split_bf16(lf_t))
    t_i = 0 if fwd else 2
    st_s = lax.broadcasted_iota(jnp.int32, (L, 2 * L), 0)
    st_t = lax.broadcasted_iota(jnp.int32, (L, 2 * L), 1) & (L - 1)
    valid = (st_s <= st_t) if fwd else (st_s >= st_t)

    def row2(tile, c0, c1):
        return jnp.concatenate([tile[c0:c0 + 1, :], tile[c1:c1 + 1, :]], axis=-1)

    for j in range(M_HEADS // 2):
        ci0, ci1 = t_i * M_HEADS + 2 * j, t_i * M_HEADS + 2 * j + 1
        cf0, cf1 = ci0 + M_HEADS, ci1 + M_HEADS
        b_row, i_row = row2(cum_t, cf0, cf1), row2(gates_t, ci0, ci1)
        colb = jnp.concatenate([jnp.broadcast_to(gates[:, ci0:ci0 + 1] - cum[:, cf0:cf0 + 1], (L, L)),
                                jnp.broadcast_to(gates[:, ci1:ci1 + 1] - cum[:, cf1:cf1 + 1], (L, L))],
                               axis=-1)
        qp = qk_ref[0, rows, j * LANES:(j + 1) * LANES]
        kp = qk_ref[0, rows, M_HEADS * M_DQK + j * LANES:M_HEADS * M_DQK + (j + 1) * LANES]
        sa_ref[slot, d, j, 0] = jnp.where(valid, b_row + colb, MASK_NEG)
        sa_ref[slot, d, j, 1] = lax.dot_general(kp, _pair_queries(qp), (((1,), (1,)), ((), ())),
                                                preferred_element_type=f32)
        ra_ref[slot, d, j, 0:1, :] = b_row
        ra_ref[slot, d, j, 1:2, :] = i_row


def _mlstm_step(qk_ref, vt_ref, sa_ref, ra_ref, slot, h_ref, ct_ref, m_ref, ci, fwd):
    L = M_CHUNK
    f32, bf16 = jnp.float32, jnp.bfloat16
    d = 0 if fwd else 1
    rows = pl.ds(pl.multiple_of(ci * L, L), L)
    last = L - 1 if fwd else 0
    lane_lo = lax.broadcasted_iota(jnp.int32, (L, LANES), 1) < M_DQK
    lane_lo_row = lax.broadcasted_iota(jnp.int32, (1, LANES), 1) < M_DQK
    ones = jnp.ones((M_DV, L), bf16)

    def halves(row, f):
        return jnp.concatenate([jnp.broadcast_to(f(row[:, :L]), (1, L)), jnp.broadcast_to(f(row[:, L:]), (1, L))],
                               axis=-1)

    for j in range(M_HEADS // 2):
        h0, h1 = 2 * j, 2 * j + 1
        sidx = 2 * j + d
        b_row, i_row = ra_ref[slot, d, j, 0:1, :], ra_ref[slot, d, j, 1:2, :]
        m_row = m_ref[sidx, 0:1, :]
        kp = qk_ref[0, rows, M_HEADS * M_DQK + j * LANES:M_HEADS * M_DQK + (j + 1) * LANES]
        vx0 = jnp.concatenate([vt_ref[0, ci, h0 * M_DV:(h0 + 1) * M_DV, :], ones], axis=0)
        vx1 = jnp.concatenate([vt_ref[0, ci, h1 * M_DV:(h1 + 1) * M_DV, :], ones], axis=0)
        ct = ct_ref[sidx]
        if h_ref is not None:
            dm = sa_ref[slot, d, j, 0]
            qq = _pair_queries(qk_ref[0, rows, j * LANES:(j + 1) * LANES])
            g_row = b_row + m_row
            mt = jnp.maximum(g_row, dm.max(axis=0, keepdims=True))
            sc = (sa_ref[slot, d, j, 1] * jnp.exp(dm - mt)).astype(bf16)
            inter = jnp.exp(g_row - mt)
            t1 = lax.dot_general(ct.astype(bf16), qq, (((1,), (1,)), ((), ())), preferred_element_type=f32)
            t2 = jnp.concatenate([_dot_f32(vx0, sc[:, :L]), _dot_f32(vx1, sc[:, L:])], axis=-1)
            tot = inter * t1 + t2
            h_t = tot[:M_DV] / jnp.maximum(jnp.abs(tot[M_DV:]), jnp.exp(-mt))
            h_ref[rows, h0 * M_DV:(h0 + 1) * M_DV] = h_t[:, :L].T
            h_ref[rows, h1 * M_DV:(h1 + 1) * M_DV] = h_t[:, L:].T
        bl = halves(b_row, lambda r: r[:, last:last + 1])
        w = bl - b_row + i_row
        m_new = jnp.maximum(bl + m_row, halves(w, lambda r: r.max(axis=-1, keepdims=True)))
        decay = jnp.exp(bl + m_row - m_new)
        ws = jnp.exp(w - m_new)
        vxs = jnp.concatenate([vx0.astype(f32) * ws[:, :L], vx1.astype(f32) * ws[:, L:]], axis=-1).astype(bf16)
        zk = jnp.zeros_like(kp)
        kk = jnp.concatenate([jnp.where(lane_lo, kp, zk), jnp.where(lane_lo, zk, kp)], axis=0)
        d_cols = jnp.where(lane_lo_row, jnp.broadcast_to(decay[:, 0:1], (1, LANES)),
                           jnp.broadcast_to(decay[:, L:L + 1], (1, LANES)))
        ct_ref[sidx] = d_cols * ct + _dot_f32(vxs, kk)
        m_ref[sidx, 0:1, :] = m_new


def _mlstm_finish(hf_ref, hb_ref, o_ref, out_ref, n_chunks):
    L = M_CHUNK

    def body(ci, carry):
        rows = pl.ds(pl.multiple_of(ci * L, L), L)
        h = hf_ref[rows, :] + hb_ref[rows, :]
        parts = []
        for hd in range(M_HEADS):
            hh = h[:, hd * M_DV:(hd + 1) * M_DV]
            parts.append(hh * lax.rsqrt(jnp.mean(hh * hh, axis=-1, keepdims=True) + RMS_EPS))
        gate = _sigmoid(o_ref[0, rows, :].astype(jnp.float32))
        out_ref[0, rows, :] = (gate * jnp.concatenate(parts, axis=-1)).astype(out_ref.dtype)
        return carry

    lax.fori_loop(0, n_chunks, body, 0)


def _mlstm_kernel(qk_c, v_c, o_c, g_c, gt_c, qk_l, v_l, o_l, g_l, gt_l, *rest, need_ctx):
    if need_ctx:
        out_l, out_c, hf_l, hb_l, hf_c, hb_c, ct_ref, m_ref, sa_ref, ra_ref = rest
    else:
        out_l, hf_l, hb_l, ct_ref, m_ref, sa_ref, ra_ref = rest
        hf_c = hb_c = None
    nc, nl = qk_c.shape[1] // M_CHUNK, qk_l.shape[1] // M_CHUNK
    ct_ref[...] = jnp.zeros_like(ct_ref)
    m_ref[...] = jnp.full_like(m_ref, M_INIT)

    def phase(qk, vt, g, gt, hf, hb, n):
        def prep(s, slot):
            _mlstm_prep(qk, g, gt, sa_ref, ra_ref, slot, s, True)
            _mlstm_prep(qk, g, gt, sa_ref, ra_ref, slot, n - 1 - s, False)

        def step(s, slot):
            _mlstm_step(qk, vt, sa_ref, ra_ref, slot, hf, ct_ref, m_ref, s, True)
            _mlstm_step(qk, vt, sa_ref, ra_ref, slot, hb, ct_ref, m_ref, n - 1 - s, False)

        prep(0, 0)

        def body(i, carry):
            s0 = 2 * i
            prep(s0 + 1, 1)
            step(s0, 0)
            prep(jnp.minimum(s0 + 2, n - 1), 0)
            step(s0 + 1, 1)
            return carry
        lax.fori_loop(0, n // 2, body, 0)

    phase(qk_c, v_c, g_c, gt_c, hf_c, hb_c, nc)
    phase(qk_l, v_l, g_l, gt_l, hf_l, hb_l, nl)
    _mlstm_finish(hf_l, hb_l, o_l, out_l, nl)
    if need_ctx:
        _mlstm_finish(hf_c, hb_c, o_c, out_c, nc)


def _gates_chunk_major(g):
    B, T, _ = g.shape
    return g[..., :4 * M_HEADS].reshape(B, T // M_CHUNK, M_CHUNK, 4 * M_HEADS).transpose(0, 1, 3, 2)


def mlstm(pc, pl_, need_ctx):
    B, S, W = pl_["mo"].shape
    Tc = pc["mo"].shape[1]
    f32 = jnp.float32

    def specs(T):
        s = pl.BlockSpec((1, T, W), lambda b: (b, 0, 0))
        return [s, pl.BlockSpec((1, T // M_CHUNK, W, M_CHUNK), lambda b: (b, 0, 0, 0)), s,
                pl.BlockSpec((1, T, GATE_W), lambda b: (b, 0, 0)),
                pl.BlockSpec((1, T // M_CHUNK, 4 * M_HEADS, M_CHUNK), lambda b: (b, 0, 0, 0))]

    def args(p):
        return [p["mqk"], p["mvt"], p["mo"], p["g"], _gates_chunk_major(p["g"])]

    out_shape = [jax.ShapeDtypeStruct((B, S, W), jnp.bfloat16)]
    out_specs = [pl.BlockSpec((1, S, W), lambda b: (b, 0, 0))]
    scratch = [pltpu.VMEM((S, W), f32), pltpu.VMEM((S, W), f32)]
    if need_ctx:
        out_shape.append(jax.ShapeDtypeStruct((B, Tc, W), jnp.bfloat16))
        out_specs.append(pl.BlockSpec((1, Tc, W), lambda b: (b, 0, 0)))
        scratch += [pltpu.VMEM((Tc, W), f32), pltpu.VMEM((Tc, W), f32)]
    assert (S // M_CHUNK) % 2 == 0 and (Tc // M_CHUNK) % 2 == 0
    n_pairs = M_HEADS // 2
    scratch += [pltpu.VMEM((M_HEADS, 2 * M_DV, LANES), f32), pltpu.VMEM((M_HEADS, 8, 2 * M_CHUNK), f32),
                pltpu.VMEM((2, 2, n_pairs, 2, M_CHUNK, 2 * M_CHUNK), f32),
                pltpu.VMEM((2, 2, n_pairs, 8, 2 * M_CHUNK), f32)]
    res = pl.pallas_call(
        functools.partial(_mlstm_kernel, need_ctx=need_ctx),
        grid=(B,),
        in_specs=specs(Tc) + specs(S),
        out_specs=out_specs,
        out_shape=out_shape,
        scratch_shapes=scratch,
        compiler_params=pltpu.CompilerParams(dimension_semantics=("arbitrary",), vmem_limit_bytes=VMEM_LIMIT),
        name="mlstm",
    )(*args(pc), *args(pl_))
    return (res[0], res[1]) if need_ctx else (res[0], None)


POOL_HALO = 16


def _pool_mix(pu_ref, prev_ref, next_ref, pw_ref, ps_ref, tile_idx, n_tiles, seq_len):
    f32 = jnp.float32
    tm = pu_ref.shape[1]
    cur = pu_ref[0].astype(f32)
    prev = jnp.where(tile_idx > 0, prev_ref[0].astype(f32), 0.0)
    nxt = jnp.where(tile_idx < n_tiles - 1, next_ref[0].astype(f32), 0.0)
    ext = jnp.concatenate([prev, cur, nxt], axis=0)
    n_ext = tm + 2 * POOL_HALO
    tok = tile_idx * tm + lax.broadcasted_iota(jnp.int32, (tm, POOL_GC), 0)

    def shifted(a, d):
        return pltpu.roll(a, (-d) % n_ext, 0)

    outs = []
    for gi, w in enumerate(POOL_WINDOWS):
        s = ext[:, gi * POOL_GC:(gi + 1) * POOL_GC]
        s = shifted(s, -1) + s
        span = 2
        while span < w:
            s = shifted(s, -(span // 2)) + shifted(s, span // 2)
            span *= 2
        lo = jnp.clip(tok - w // 2, 0, seq_len - 1)
        hi = jnp.clip(tok - w // 2 + w - 1, 0, seq_len - 1)
        mean = s[POOL_HALO:POOL_HALO + tm] / (hi - lo + 1).astype(f32)
        d = (mean - cur[:, gi * POOL_GC:(gi + 1) * POOL_GC]).astype(jnp.bfloat16)
        outs.append(_dot_f32(d, pw_ref[gi]))
    return jnp.concatenate(outs, axis=-1) * ps_ref[...]


def _pack_bf16_pairs(h):
    n = h.shape[1] // 2
    return pltpu.pack_elementwise([h[:, :n], h[:, n:]], packed_dtype=jnp.bfloat16)


def _merge_kernel(m_ref, n_ref, pu_ref, prev_ref, next_ref, bg_ref, x_ref, mods_ref, gains_ref,
                  pw_ref, ps_ref, wb_ref, wo_ref, wrt_ref, wrp_ref, x1_ref, h2p_ref, affr_ref, aff_ref, *, seq_len):
    f32, bf16 = jnp.float32, jnp.bfloat16
    i = pl.program_id(1)
    D = x_ref.shape[2]
    E = wrt_ref.shape[0]
    p = _pool_mix(pu_ref, prev_ref, next_ref, pw_ref, ps_ref, i, pl.num_programs(1), seq_len)
    branches = (m_ref[0], p.astype(bf16), n_ref[0])
    acc = None
    for bi, br in enumerate(branches):
        gate = jnp.tanh(bg_ref[0, :, bi * D:(bi + 1) * D].astype(f32)) + 1.0
        term = gate * _dot_f32(br, wb_ref[bi])
        acc = term if acc is None else acc + term
    y = _dot_f32(acc.astype(bf16), wo_ref[...])
    yn = y * lax.rsqrt(jnp.mean(y * y, axis=-1, keepdims=True) + RMS_EPS) * gains_ref[0:1, :]
    x1 = x_ref[0] + mods_ref[0, 0:1, :] * yn
    x1_ref[0] = x1
    xn = x1 * lax.rsqrt(jnp.mean(x1 * x1, axis=-1, keepdims=True) + RMS_EPS) * gains_ref[1:2, :]
    h2 = xn * (1.0 + mods_ref[0, 2:3, :]) + mods_ref[0, 1:2, :]
    h2b = h2.astype(bf16)
    lg_t = lax.dot_general(wrt_ref[...], h2b, (((1,), (1,)), ((), ())), preferred_element_type=f32)
    e_t = jnp.exp(lg_t - lg_t.max(axis=0, keepdims=True))
    aff_ref[0] = e_t / e_t.sum(axis=0, keepdims=True)
    lg = _dot_f32(h2b, wrp_ref[...])
    lg = jnp.where(lax.broadcasted_iota(jnp.int32, lg.shape, 1) < E, lg, MASK_NEG)
    e_r = jnp.exp(lg - lg.max(axis=-1, keepdims=True))
    affr_ref[0] = e_r / e_r.sum(axis=-1, keepdims=True)
    h2p_ref[0] = _pack_bf16_pairs(h2)


def merge_out(m, n, pu, bg, x, mods, gains, pool_w, pool_scale, w_branch, w_out, w_router_t, w_router_pad, *, tm):
    B, T, D = x.shape
    W = m.shape[2]
    E = w_router_t.shape[0]
    hb = tm // POOL_HALO
    n_halo = T // POOL_HALO
    tile = lambda w: pl.BlockSpec((1, tm, w), lambda b, i: (b, i, 0))
    in_specs = [
        tile(W), tile(W), tile(W),
        pl.BlockSpec((1, POOL_HALO, W), lambda b, i: (b, jnp.maximum(i * hb - 1, 0), 0)),
        pl.BlockSpec((1, POOL_HALO, W), lambda b, i: (b, jnp.minimum((i + 1) * hb, n_halo - 1), 0)),
        tile(N_BRANCH * D), tile(D),
        pl.BlockSpec((1, 8, D), lambda b, i: (b, 0, 0)),
        _const_spec((8, D)),
        _const_spec(pool_w.shape), _const_spec(pool_scale.shape), _const_spec(w_branch.shape),
        _const_spec(w_out.shape), _const_spec(w_router_t.shape), _const_spec(w_router_pad.shape),
    ]
    return pl.pallas_call(
        functools.partial(_merge_kernel, seq_len=T),
        grid=(B, T // tm),
        in_specs=in_specs,
        out_specs=[tile(D), tile(D // 2), tile(LANES), pl.BlockSpec((1, E, tm), lambda b, i: (b, 0, i))],
        out_shape=[jax.ShapeDtypeStruct((B, T, D), jnp.float32),
                   jax.ShapeDtypeStruct((B, T, D // 2), PACK_WORD),
                   jax.ShapeDtypeStruct((B, T, LANES), jnp.float32),
                   jax.ShapeDtypeStruct((B, E, T), jnp.float32)],
        compiler_params=pltpu.CompilerParams(
            dimension_semantics=("arbitrary", "arbitrary"), vmem_limit_bytes=VMEM_LIMIT),
        name="merge_out",
    )(m, n, pu, pu, pu, bg, x, mods, gains, pool_w, pool_scale, w_branch, w_out, w_router_t, w_router_pad)


def _cumsum_lanes(x01):
    R, T = x01.shape
    r = lax.broadcasted_iota(jnp.int32, (LANES, LANES), 0)
    c = lax.broadcasted_iota(jnp.int32, (LANES, LANES), 1)
    tri_u = jnp.where(r <= c, 1.0, 0.0).astype(jnp.bfloat16)
    run = jnp.zeros((R, 1), jnp.float32)
    outs = []
    for t0 in range(0, T, LANES):
        cs = _dot_f32(x01[:, t0:t0 + LANES].astype(jnp.bfloat16), tri_u) + run
        run = cs[:, LANES - 1:LANES]
        outs.append(cs)
    return jnp.concatenate(outs, axis=-1)


def _route_kernel(aff_ref, idx_ref, *, cap):
    f32 = jnp.float32
    aff = aff_ref[0]
    E, T = aff.shape

    def search(k, lo):
        cand = lo | (jnp.int32(1) << (30 - k))
        cnt = jnp.sum(jnp.where(aff >= pltpu.bitcast(cand, f32), 1.0, 0.0), axis=-1, keepdims=True)
        return jnp.where(cnt >= cap, cand, lo)

    thr = pltpu.bitcast(lax.fori_loop(0, 31, search, jnp.zeros((E, 1), jnp.int32)), f32)
    gt = jnp.where(aff > thr, 1.0, 0.0)
    eq = jnp.where(aff == thr, 1.0, 0.0)
    room = cap - jnp.sum(gt, axis=-1, keepdims=True)
    sel = gt + eq * jnp.where(_cumsum_lanes(eq) <= room, 1.0, 0.0)
    csel = _cumsum_lanes(sel)
    slot = lax.broadcasted_iota(jnp.int32, (cap, T), 0).astype(f32)
    lane = lax.broadcasted_iota(jnp.int32, (cap, LANES), 1)
    out = jnp.zeros((cap, LANES), f32)
    for e in range(E):
        pos = jnp.sum(jnp.where(jnp.broadcast_to(csel[e:e + 1, :], (cap, T)) <= slot, 1.0, 0.0),
                      axis=-1, keepdims=True)
        out = jnp.where(lane == e, jnp.broadcast_to(pos, (cap, LANES)), out)
    idx_ref[0] = out.astype(jnp.int32)


def route(aff_t, cap):
    B, E, T = aff_t.shape
    idx_t = pl.pallas_call(
        functools.partial(_route_kernel, cap=cap),
        grid=(B,),
        in_specs=[pl.BlockSpec((1, E, T), lambda b: (b, 0, 0))],
        out_specs=pl.BlockSpec((1, cap, LANES), lambda b: (b, 0, 0)),
        out_shape=jax.ShapeDtypeStruct((B, cap, LANES), jnp.int32),
        compiler_params=pltpu.CompilerParams(dimension_semantics=("arbitrary",), vmem_limit_bytes=VMEM_LIMIT),
        name="route",
    )(aff_t)
    return idx_t[:, :, :E].transpose(0, 2, 1)


ROW_GROUP = 8


def _moe_kernel(idx_ref, rows_ref, affr_ref, wg_ref, wu_ref, wd_ref, out_ref,
                xg0_ref, xg1_ref, ag0_ref, ag1_ref, ye0_ref, ye1_ref):
    f32, bf16 = jnp.float32, jnp.bfloat16
    e = pl.program_id(1)
    n_exp = pl.num_programs(1)
    cap = xg0_ref.shape[0]

    def gather_row(dst, ee, j):
        i = idx_ref[0, 0, ee, j]
        dst[0][pl.ds(j, 1), :] = rows_ref[0, pl.ds(i, 1), :]
        dst[1][pl.ds(j, 1), :] = affr_ref[0, pl.ds(i, 1), :]

    def scatter_group(src_ref, ee, j0):
        ids = [idx_ref[0, 0, ee, j0 + r] for r in range(ROW_GROUP)]
        cur = [out_ref[0, pl.ds(i, 1), :] for i in ids]
        for r, i in enumerate(ids):
            out_ref[0, pl.ds(i, 1), :] = cur[r] + src_ref[pl.ds(j0 + r, 1), :]

    @pl.when(e == 0)
    def _():
        out_ref[...] = jnp.zeros_like(out_ref)
        ye1_ref[...] = jnp.zeros_like(ye1_ref)

        def body(gi, carry):
            for r in range(ROW_GROUP):
                gather_row((xg0_ref, ag0_ref), 0, gi * ROW_GROUP + r)
            return carry
        lax.fori_loop(0, cap // ROW_GROUP, body, 0)

    def step(cur, nxt, ye_cur, ye_prv):
        e_prv = jnp.maximum(e - 1, 0)
        e_nxt = jnp.minimum(e + 1, n_exp - 1)
        for j0 in range(0, cap, ROW_GROUP):
            scatter_group(ye_prv, e_prv, j0)
        for j in range(cap):
            gather_row(nxt, e_nxt, j)
        packed = cur[0][...]
        xe = jnp.concatenate(
            [pltpu.unpack_elementwise(packed, index=i, packed_dtype=bf16, unpacked_dtype=f32) for i in (0, 1)],
            axis=-1).astype(bf16)
        aff = cur[1][...]
        gate = jnp.sum(jnp.where(lax.broadcasted_iota(jnp.int32, aff.shape, 1) == e, aff, 0.0),
                       axis=-1, keepdims=True)
        a = _dot_f32(xe, wg_ref[0])
        hid = (a * _sigmoid(a) * _dot_f32(xe, wu_ref[0])).astype(bf16)
        ye_cur[...] = _dot_f32(hid, wd_ref[0]) * gate

    @pl.when(e % 2 == 0)
    def _():
        step((xg0_ref, ag0_ref), (xg1_ref, ag1_ref), ye0_ref, ye1_ref)

    @pl.when(e % 2 == 1)
    def _():
        step((xg1_ref, ag1_ref), (xg0_ref, ag0_ref), ye1_ref, ye0_ref)

    @pl.when(e == n_exp - 1)
    def _():
        def body(gi, carry):
            scatter_group(ye1_ref, e, gi * ROW_GROUP)
            return carry
        lax.fori_loop(0, cap // ROW_GROUP, body, 0)


def moe_experts(rows, aff_rows, idx, w_gate, w_up, w_down):
    G, T, RW = rows.shape
    AW = aff_rows.shape[2]
    _, E, cap = idx.shape
    D = w_gate.shape[1]
    assert E % 2 == 0 and cap % ROW_GROUP == 0
    wspec = pl.BlockSpec((1, D, D), lambda g, e: (e, 0, 0))
    return pl.pallas_call(
        _moe_kernel,
        grid=(G, E),
        in_specs=[pl.BlockSpec((1, 1, E, cap), lambda g, e: (g, 0, 0, 0), memory_space=pltpu.SMEM),
                  pl.BlockSpec((1, T, RW), lambda g, e: (g, 0, 0)), pl.BlockSpec((1, T, AW), lambda g, e: (g, 0, 0)),
                  wspec, wspec, wspec],
        out_specs=pl.BlockSpec((1, T, D), lambda g, e: (g, 0, 0)),
        out_shape=jax.ShapeDtypeStruct((G, T, D), jnp.float32),
        scratch_shapes=[pltpu.VMEM((cap, RW), rows.dtype), pltpu.VMEM((cap, RW), rows.dtype),
                        pltpu.VMEM((cap, AW), jnp.float32), pltpu.VMEM((cap, AW), jnp.float32),
                        pltpu.VMEM((cap, D), jnp.float32), pltpu.VMEM((cap, D), jnp.float32)],
        compiler_params=pltpu.CompilerParams(
            dimension_semantics=("arbitrary", "arbitrary"), vmem_limit_bytes=VMEM_LIMIT),
        name="moe_experts",
    )(idx.reshape(G, 1, E, cap), rows, aff_rows, w_gate, w_up, w_down)


def _residual_norm_kernel(x_ref, y_ref, gate_ref, gain_ref, o_ref):
    y = y_ref[0]
    yn = y * lax.rsqrt(jnp.mean(y * y, axis=-1, keepdims=True) + RMS_EPS) * gain_ref[...]
    o_ref[0] = x_ref[0] + gate_ref[0] * yn


def residual_norm(x, y, gate, gain, *, tm):
    B, T, D = x.shape
    tile = pl.BlockSpec((1, tm, D), lambda b, i: (b, i, 0))
    return pl.pallas_call(
        _residual_norm_kernel,
        grid=(B, T // tm),
        in_specs=[tile, tile, pl.BlockSpec((1, 1, D), lambda b, i: (b, 0, 0)), _const_spec((1, D))],
        out_specs=tile,
        out_shape=jax.ShapeDtypeStruct((B, T, D), jnp.float32),
        compiler_params=pltpu.CompilerParams(
            dimension_semantics=("arbitrary", "arbitrary"), vmem_limit_bytes=VMEM_LIMIT),
        name="residual_norm",
    )(x, y, gate, gain)


ADA_COLS = 1536


def _ada_kernel(c_ref, w_ref, b_ref, o_ref):
    c = c_ref[...]
    h = (c * _sigmoid(c)).astype(jnp.bfloat16)
    o_ref[...] = _dot_f32(h, w_ref[...].astype(jnp.bfloat16)) + b_ref[...]


def ada_modulation(cond, ada_w_l, ada_b_l):
    R, D = cond.shape
    N = ada_w_l.shape[1]
    return pl.pallas_call(
        _ada_kernel,
        grid=(N // ADA_COLS,),
        in_specs=[pl.BlockSpec((R, D), lambda j: (0, 0)), pl.BlockSpec((D, ADA_COLS), lambda j: (0, j)),
                  pl.BlockSpec((1, ADA_COLS), lambda j: (0, j))],
        out_specs=pl.BlockSpec((R, ADA_COLS), lambda j: (0, j)),
        out_shape=jax.ShapeDtypeStruct((R, N), jnp.float32),
        compiler_params=pltpu.CompilerParams(dimension_semantics=("arbitrary",), vmem_limit_bytes=VMEM_LIMIT),
        name="ada_modulation",
    )(cond, ada_w_l, ada_b_l[None])


def kernel(x, c, ctx, c_ctx, norm_gain, ada_w, ada_b, w_in, mlstm_gate_bias, pool_w, pool_scale,
           na_rpb, w_branch, w_out, router_w, w_gate, w_up, w_down):
    B, S, D = x.shape
    Tc = ctx.shape[1]
    f32, bf16 = jnp.float32, jnp.bfloat16
    rope_tabs = rope_lane_tables(S)
    names = [n for n, _ in IN_GROUPS]
    cond = jnp.pad(jnp.concatenate([c, c_ctx[None]], axis=0), ((0, (-(B + 1)) % 8), (0, 0)))
    pending = None
    for l in range(DEPTH):
        need_ctx = l < DEPTH - 1
        g = norm_gain[l]
        mods = ada_modulation(cond, ada_w[l], ada_b[l])
        mod_l = jnp.split(mods[:B, None, :], N_MOD, axis=-1)
        mod_cb = [jnp.broadcast_to(m[None, None], (B, 1, D)) for m in jnp.split(mods[B], N_MOD, axis=-1)]
        w_perm = permute_w_in(w_in[l])
        gb_row = jnp.pad(mlstm_gate_bias[l].reshape(1, -1).astype(f32), ((0, 0), (0, GATE_W - 4 * M_HEADS)))
        proj = in_proj(x, g[0][None], mod_l[0], mod_l[1], w_perm, gb_row, rope_tabs, pending, tm=512)
        if pending is not None:
            x = proj[-1]
        pl_ = dict(zip(names, proj))
        pc_ = dict(zip(names, in_proj(ctx, g[0][None], mod_cb[0], mod_cb[1], w_perm, gb_row, tm=Tc)))

        m_l, m_c = mlstm(pc_, pl_, need_ctx)
        n_l, n_c = neighbourhood_attention(pl_["qn"], pl_["kn"], pl_["vn"], pc_["kn"], pc_["vn"],
                                           na_bias_table(na_rpb[l]), pc_["qn"] if need_ctx else None)
        gains = jnp.pad(g[1:3], ((0, 6), (0, 0)))
        merge_w = (pool_w[l].astype(bf16), pool_scale[l][None], w_branch[l].astype(bf16),
                   (0.5 * w_out[l]).astype(bf16),
                   router_w[l].T.astype(bf16),
                   jnp.pad(router_w[l], ((0, 0), (0, LANES - N_EXPERTS))).astype(bf16))
        moe_w = (w_gate[l].astype(bf16), w_up[l].astype(bf16), w_down[l].astype(bf16))

        def mods8(mods):
            return jnp.pad(jnp.concatenate([mods[2], mods[3], mods[4]], axis=1), ((0, 0), (0, 5), (0, 0)))

        x1, rows, aff_r, aff_t = merge_out(m_l, n_l, pl_["pu"], pl_["bg"], x, mods8(mod_l), gains, *merge_w, tm=512)
        y = moe_experts(rows, aff_r, route(aff_t, EC_FACTOR * S // N_EXPERTS), *moe_w)
        if need_ctx:
            x, pending = x1, (y, mod_l[5], g[3][None])
        else:
            x = residual_norm(x1, y, mod_l[5], g[3][None], tm=512)
        if need_ctx:
            c1, rows, aff_r, aff_t = merge_out(m_c, n_c, pc_["pu"], pc_["bg"], ctx, mods8(mod_cb), gains, *merge_w, tm=Tc)
            per = S // Tc
            cap_c = EC_FACTOR * Tc // N_EXPERTS
            idx = route(aff_t, cap_c) + (jnp.arange(B, dtype=jnp.int32) % per * Tc)[:, None, None]
            idx = idx.reshape(B // per, per, N_EXPERTS, cap_c).transpose(0, 2, 1, 3).reshape(B // per, N_EXPERTS, per * cap_c)
            y = moe_experts(rows.reshape(B // per, S, D // 2), aff_r.reshape(B // per, S, LANES), idx,
                            *moe_w).reshape(B, Tc, D)
            ctx = residual_norm(c1, y, mod_cb[5], g[3][None], tm=Tc)
    return x
```

```python
import functools
import math

import jax
import jax.numpy as jnp
import numpy as np
from jax import lax
from jax.experimental import pallas as pl
from jax.experimental.pallas import tpu as pltpu

D_MODEL = 1024
DEPTH = 2
GRID_W = 64
BRANCH_W = D_MODEL // 2
N_BRANCH = 3
M_HEADS = 4
M_DV = BRANCH_W // M_HEADS
M_DQK = M_DV // 2
M_CHUNK = 128
M_INIT = -1e30
POOL_GROUPS = 4
POOL_GC = BRANCH_W // POOL_GROUPS
POOL_WINDOWS = (2, 4, 8, 16)
NA_HEADS = 8
NA_DH = BRANCH_W // NA_HEADS
NA_KH = 8
NA_KW = 16
N_EXPERTS = 16
EC_FACTOR = 2
ROPE_BASE = 10000.0
RMS_EPS = 1e-6
N_MOD = 6
SPLIT_SIZES = (M_HEADS * M_DQK, M_HEADS * M_DQK, M_HEADS * M_DV, M_HEADS * M_DV, 4 * M_HEADS,
               POOL_GROUPS * POOL_GC, NA_HEADS * NA_DH, NA_HEADS * NA_DH, NA_HEADS * NA_DH,
               N_BRANCH * D_MODEL)
PROJ_W = sum(SPLIT_SIZES)

LANES = 128
VMEM_LIMIT = 56 * 1024 * 1024
MASK_NEG = -1e30
GATE_W = LANES

IN_GROUPS = (("mqk", 2 * M_HEADS * M_DQK), ("mvt", BRANCH_W), ("mo", BRANCH_W), ("pu", BRANCH_W),
             ("qn", BRANCH_W), ("kn", BRANCH_W), ("vn", BRANCH_W), ("bg", N_BRANCH * D_MODEL),
             ("g", GATE_W))
IN_W = sum(w for _, w in IN_GROUPS)
MM_COLS = 512
PACK_WORD = jnp.uint32


def _sigmoid(x):
    return 0.5 * jnp.tanh(0.5 * x) + 0.5


def _const_spec(shape):
    nd = len(shape)
    return pl.BlockSpec(shape, lambda *_: (0,) * nd, pipeline_mode=pl.Buffered(1))


def _rope_tile(x, cos, sin_signed):
    half = M_DQK // 4
    lane = lax.broadcasted_iota(jnp.int32, x.shape, 1)
    partner = jnp.where((lane % (2 * half)) < half, pltpu.roll(x, LANES - half, 1), pltpu.roll(x, half, 1))
    return x * cos + partner * sin_signed


def _in_proj_kernel(x_ref, g_ref, shift_ref, scale_ref, w_ref, gb_ref, *rest, rope, pending):
    if rope:
        cos_ref, sin_ref = rest[:2]
        rest = rest[2:]
    if pending:
        r_ref, rgate_ref, rgain_ref = rest[:3]
        rest, xo_ref = rest[3:-1], rest[-1]
        r = r_ref[0]
        x = x_ref[0] + rgate_ref[0] * (r * lax.rsqrt(jnp.mean(r * r, axis=-1, keepdims=True) + RMS_EPS)
                                       * rgain_ref[...])
        xo_ref[0] = x
    else:
        x = x_ref[0]
    o_refs, gt_ref = rest[:-1], rest[-1]
    y = x * lax.rsqrt(jnp.mean(x * x, axis=-1, keepdims=True) + RMS_EPS)
    h = ((y * g_ref[...]) * (1.0 + scale_ref[0]) + shift_ref[0]).astype(jnp.bfloat16)
    off = 0
    for (name, width), o_ref in zip(IN_GROUPS, o_refs):
        for c0 in range(0, width, MM_COLS):
            cw = min(MM_COLS, width - c0)
            acc = jnp.dot(h, w_ref[:, off + c0:off + c0 + cw], preferred_element_type=jnp.float32)
            if name == "g":
                acc = acc + gb_ref[...]
                for c in range(acc.shape[0] // M_CHUNK):
                    gt_ref[0, c] = acc[c * M_CHUNK:(c + 1) * M_CHUNK, :].T[:4 * M_HEADS, :]
            if name == "mqk" and rope:
                cos, sin = cos_ref[...], sin_ref[...]
                acc = jnp.concatenate([_rope_tile(acc[:, t:t + LANES], cos, sin) for t in range(0, cw, LANES)], axis=-1)
            if name == "mvt":
                for c in range(acc.shape[0] // M_CHUNK):
                    o_ref[0, c] = acc[c * M_CHUNK:(c + 1) * M_CHUNK, :].T.astype(o_ref.dtype)
                continue
            o_ref[0, :, c0:c0 + cw] = acc.astype(o_ref.dtype)
        off += width


def in_proj(x, g, shift, scale, w_perm, gate_bias_row, rope_tabs=None, pending=None, *, tm):
    B, T, D = x.shape
    out_shape = [jax.ShapeDtypeStruct((B, T, w), jnp.float32 if n == "g" else jnp.bfloat16) for n, w in IN_GROUPS]
    out_specs = [pl.BlockSpec((1, tm, w), lambda b, i: (b, i, 0)) for _, w in IN_GROUPS]
    k_mvt = [n for n, _ in IN_GROUPS].index("mvt")
    assert BRANCH_W == MM_COLS and tm % M_CHUNK == 0
    out_shape[k_mvt] = jax.ShapeDtypeStruct((B, T // M_CHUNK, BRANCH_W, M_CHUNK), jnp.bfloat16)
    out_specs[k_mvt] = pl.BlockSpec((1, tm // M_CHUNK, BRANCH_W, M_CHUNK), lambda b, i: (b, i, 0, 0))
    out_shape.append(jax.ShapeDtypeStruct((B, T // M_CHUNK, 4 * M_HEADS, M_CHUNK), jnp.float32))
    out_specs.append(pl.BlockSpec((1, tm // M_CHUNK, 4 * M_HEADS, M_CHUNK), lambda b, i: (b, i, 0, 0)))
    in_specs = [
        pl.BlockSpec((1, tm, D), lambda b, i: (b, i, 0)),
        _const_spec((1, D)),
        pl.BlockSpec((1, 1, D), lambda b, i: (b, 0, 0)),
        pl.BlockSpec((1, 1, D), lambda b, i: (b, 0, 0)),
        _const_spec((D, IN_W)),
        _const_spec((1, GATE_W)),
    ]
    args = [x, g, shift, scale, w_perm, gate_bias_row]
    if rope_tabs is not None:
        in_specs += [pl.BlockSpec((tm, LANES), lambda b, i: (i, 0))] * 2
        args += list(rope_tabs)
    if pending is not None:
        in_specs += [pl.BlockSpec((1, tm, D), lambda b, i: (b, i, 0)), pl.BlockSpec((1, 1, D), lambda b, i: (b, 0, 0)),
                     _const_spec((1, D))]
        args += list(pending)
        out_shape.append(jax.ShapeDtypeStruct((B, T, D), jnp.float32))
        out_specs.append(pl.BlockSpec((1, tm, D), lambda b, i: (b, i, 0)))
    return pl.pallas_call(
        functools.partial(_in_proj_kernel, rope=rope_tabs is not None, pending=pending is not None),
        grid=(B, T // tm),
        in_specs=in_specs,
        out_specs=out_specs,
        out_shape=out_shape,
        compiler_params=pltpu.CompilerParams(
            dimension_semantics=("arbitrary", "arbitrary"), vmem_limit_bytes=VMEM_LIMIT),
        name="in_proj",
    )(*args)


def rope_lane_tables(n_tokens):
    t = jnp.arange(n_tokens)
    row = (t // GRID_W).astype(jnp.float32)
    col = (t % GRID_W).astype(jnp.float32)
    half = M_DQK // 2
    inv = ROPE_BASE ** (-jnp.arange(0, half, 2, dtype=jnp.float32) / half)
    ar = row[:, None] * inv[None, :]
    ac = col[:, None] * inv[None, :]
    cos = jnp.concatenate([jnp.cos(ar), jnp.cos(ar), jnp.cos(ac), jnp.cos(ac)], axis=-1)
    sin = jnp.concatenate([-jnp.sin(ar), jnp.sin(ar), -jnp.sin(ac), jnp.sin(ac)], axis=-1)
    return jnp.tile(cos, (1, LANES // M_DQK)), jnp.tile(sin, (1, LANES // M_DQK))


def permute_w_in(w_in_l):
    qm, km, vm, om, gm, pu, qn, kn, vn, bg = jnp.split(w_in_l, [int(s) for s in np.cumsum(SPLIT_SIZES)[:-1]], axis=-1)
    gpad = jnp.pad(gm, ((0, 0), (0, GATE_W - gm.shape[1])))
    w = jnp.concatenate([qm * (M_DQK ** -0.5), km, vm, om, pu, qn * (NA_DH ** -0.5), kn, vn, bg * 0.5, gpad], axis=-1)
    return w.astype(jnp.bfloat16)


def _pair_scores(qp, k_parts, biases):
    n = qp.shape[0]
    lane_lo = lax.broadcasted_iota(jnp.int32, (n, LANES), 1) < NA_DH
    zero = jnp.zeros_like(qp)
    qq = jnp.concatenate([jnp.where(lane_lo, qp, zero), jnp.where(lane_lo, zero, qp)], axis=0)
    scores = []
    for kp, bias in zip(k_parts, biases):
        s = lax.dot_general(qq, kp, (((1,), (1,)), ((), ())), preferred_element_type=jnp.float32)
        scores.append(s if bias is None else s + bias)
    return scores


def _pair_softmax_pv(scores, v_parts):
    n = scores[0].shape[0] // 2
    lane_lo = lax.broadcasted_iota(jnp.int32, (n, LANES), 1) < NA_DH
    m = scores[0].max(axis=-1, keepdims=True)
    for s in scores[1:]:
        m = jnp.maximum(m, s.max(axis=-1, keepdims=True))
    l = None
    o = None
    for s, vp in zip(scores, v_parts):
        p = jnp.exp(s - m)
        ls = p.sum(axis=-1, keepdims=True)
        os_ = jnp.dot(p.astype(jnp.bfloat16), vp, preferred_element_type=jnp.float32)
        l = ls if l is None else l + ls
        o = os_ if o is None else o + os_
    o = o * (1.0 / l)
    return jnp.where(lane_lo, o[:n], o[n:])


def _na_kernel(q_ref, k_ref, v_ref, kc_ref, vc_ref, bias_ref, *rest, need_ctx):
    if need_ctx:
        qc_ref, o_ref, oc_ref, s_ref = rest
    else:
        o_ref, s_ref = rest
    S = q_ref.shape[1]
    Tc = kc_ref.shape[1]
    rows = S // GRID_W
    n_pairs = BRANCH_W // LANES
    n_loc = NA_KH * GRID_W

    def window(r):
        rs = jnp.clip(r - NA_KH // 2, 0, rows - NA_KH)
        return r - rs, pl.multiple_of(r * GRID_W, GRID_W), pl.multiple_of(rs * GRID_W, GRID_W)

    def scores_stage(r, slot):
        var, q0, k0 = window(r)
        for j in range(n_pairs):
            ls = slice(j * LANES, (j + 1) * LANES)
            s_loc, s_ctx = _pair_scores(q_ref[0, pl.ds(q0, GRID_W), ls],
                                        (k_ref[0, pl.ds(k0, n_loc), ls], kc_ref[0, :, ls]),
                                        (bias_ref[var, j], None))
            s_ref[slot, j, :, :n_loc] = s_loc
            s_ref[slot, j, :, n_loc:] = s_ctx

    def output_stage(r, slot):
        _, q0, k0 = window(r)
        outs = []
        for j in range(n_pairs):
            ls = slice(j * LANES, (j + 1) * LANES)
            outs.append(_pair_softmax_pv((s_ref[slot, j, :, :n_loc], s_ref[slot, j, :, n_loc:]),
                                         (v_ref[0, pl.ds(k0, n_loc), ls], vc_ref[0, :, ls])))
        o_ref[0, pl.ds(q0, GRID_W), :] = jnp.concatenate(outs, axis=-1).astype(o_ref.dtype)

    scores_stage(0, 0)

    def two_rows(i, carry):
        r0 = 2 * i
        scores_stage(r0 + 1, 1)
        output_stage(r0, 0)
        scores_stage(jnp.minimum(r0 + 2, rows - 1), 0)
        output_stage(r0 + 1, 1)
        return carry

    lax.fori_loop(0, rows // 2, two_rows, 0)

    if need_ctx:
        outs = []
        for j in range(n_pairs):
            ls = slice(j * LANES, (j + 1) * LANES)
            scores = _pair_scores(qc_ref[0, :, ls], (kc_ref[0, :, ls],), (None,))
            outs.append(_pair_softmax_pv(scores, (vc_ref[0, :, ls],)))
        oc_ref[0] = jnp.concatenate(outs, axis=-1).astype(oc_ref.dtype)


def na_bias_table(rpb):
    H = rpb.shape[0]
    var = jnp.arange(NA_KH)
    kr = jnp.arange(NA_KH)
    dr = kr[None, :] - var[:, None] + NA_KH - 1
    cols = jnp.arange(GRID_W)
    dc = jnp.clip(cols[None, :] - cols[:, None] + NA_KW - 1, 0, 2 * NA_KW - 2)
    cs = jnp.clip(cols - NA_KW // 2, 0, GRID_W - NA_KW)
    colmask = (cols[None, :] >= cs[:, None]) & (cols[None, :] < cs[:, None] + NA_KW)
    pick_r = jax.nn.one_hot(dr, 2 * NA_KH - 1, dtype=jnp.float32)
    pick_c = jax.nn.one_hot(dc, 2 * NA_KW - 1, dtype=jnp.float32)
    tab = jnp.einsum('vka,hab,qcb->vhqkc', pick_r, rpb.astype(jnp.float32), pick_c, precision=lax.Precision.HIGHEST)
    tab = jnp.where(colmask[None, None, :, None, :], tab, MASK_NEG)
    return tab.reshape(NA_KH, H // 2, 2 * GRID_W, NA_KH * GRID_W)


def neighbourhood_attention(qn, kn, vn, kc, vc, bias_tab, qc=None):
    B, S, W = qn.shape
    Tc = kc.shape[1]
    need_ctx = qc is not None
    lat = pl.BlockSpec((1, S, W), lambda b: (b, 0, 0))
    cx = pl.BlockSpec((1, Tc, W), lambda b: (b, 0, 0))
    in_specs = [lat, lat, lat, cx, cx, _const_spec(bias_tab.shape)]
    args = [qn, kn, vn, kc, vc, bias_tab]
    out_shape = [jax.ShapeDtypeStruct((B, S, W), jnp.bfloat16)]
    out_specs = [lat]
    if need_ctx:
        in_specs.append(cx)
        args.append(qc)
        out_shape.append(jax.ShapeDtypeStruct((B, Tc, W), jnp.bfloat16))
        out_specs.append(cx)
    res = pl.pallas_call(
        functools.partial(_na_kernel, need_ctx=need_ctx),
        grid=(B,),
        in_specs=in_specs,
        out_specs=out_specs,
        out_shape=out_shape,
        scratch_shapes=[pltpu.VMEM((2, W // LANES, 2 * GRID_W, NA_KH * GRID_W + Tc), jnp.float32)],
        compiler_params=pltpu.CompilerParams(dimension_semantics=("arbitrary",), vmem_limit_bytes=VMEM_LIMIT),
        name="neighbourhood_attention",
    )(*args)
    return (res[0], res[1]) if need_ctx else (res[0], None)


def _split_bf16(x):
    hi = x.astype(jnp.bfloat16)
    r1 = x - hi.astype(jnp.float32)
    mid = r1.astype(jnp.bfloat16)
    lo = (r1 - mid.astype(jnp.float32)).astype(jnp.bfloat16)
    return hi, mid, lo


def _dot_f32(a, b):
    return jnp.dot(a, b, preferred_element_type=jnp.float32)


def _log_sigmoid(x):
    return jnp.minimum(x, 0.0) - jnp.log(1.0 + jnp.exp(-jnp.abs(x)))


def _pair_queries(qp):
    lane_lo = lax.broadcasted_iota(jnp.int32, qp.shape, 1) < M_DQK
    zq = jnp.zeros_like(qp)
    return jnp.concatenate([jnp.where(lane_lo, qp, zq), jnp.where(lane_lo, zq, qp)], axis=0)


def _mlstm_prep(qk_ref, g_ref, gt_ref, sa_ref, ra_ref, slot, ci, fwd):
    L = M_CHUNK
    f32, bf16 = jnp.float32, jnp.bfloat16
    d = 0 if fwd else 1
    rows = pl.ds(pl.multiple_of(ci * L, L), L)
    sq_r = lax.broadcasted_iota(jnp.int32, (L, L), 0)
    sq_c = lax.broadcasted_iota(jnp.int32, (L, L), 1)
    tri_l = jnp.where(sq_r >= sq_c, 1.0, 0.0).astype(bf16)
    tri_u = jnp.where(sq_r <= sq_c, 1.0, 0.0).astype(bf16)
    gates = g_ref[0, rows, :]
    gates_t = gt_ref[0, ci]
    lf, lf_t = _log_sigmoid(gates), _log_sigmoid(gates_t)
    cum = sum(_dot_f32(tri_l if fwd else tri_u, p) for p in _split_bf16(lf))
    cum_t = sum(_dot_f32(p, tri_u if fwd else tri_l) for p in _split_bf16(lf_t))
    t_i = 0 if fwd else 2
    st_s = lax.broadcasted_iota(jnp.int32, (L, 2 * L), 0)
    st_t = lax.broadcasted_iota(jnp.int32, (L, 2 * L), 1) & (L - 1)
    valid = (st_s <= st_t) if fwd else (st_s >= st_t)

    def row2(tile, c0, c1):
        return jnp.concatenate([tile[c0:c0 + 1, :], tile[c1:c1 + 1, :]], axis=-1)

    for j in range(M_HEADS // 2):
        ci0, ci1 = t_i * M_HEADS + 2 * j, t_i * M_HEADS + 2 * j + 1
        cf0, cf1 = ci0 + M_HEADS, ci1 + M_HEADS
        b_row, i_row = row2(cum_t, cf0, cf1), row2(gates_t, ci0, ci1)
        colb = jnp.concatenate([jnp.broadcast_to(gates[:, ci0:ci0 + 1] - cum[:, cf0:cf0 + 1], (L, L)),
                                jnp.broadcast_to(gates[:, ci1:ci1 + 1] - cum[:, cf1:cf1 + 1], (L, L))],
                               axis=-1)
        qp = qk_ref[0, rows, j * LANES:(j + 1) * LANES]
        kp = qk_ref[0, rows, M_HEADS * M_DQK + j * LANES:M_HEADS * M_DQK + (j + 1) * LANES]
        sa_ref[slot, d, j, 0] = jnp.where(valid, b_row + colb, MASK_NEG)
        sa_ref[slot, d, j, 1] = lax.dot_general(kp, _pair_queries(qp), (((1,), (1,)), ((), ())),
                                                preferred_element_type=f32)
        ra_ref[slot, d, j, 0:1, :] = b_row
        ra_ref[slot, d, j, 1:2, :] = i_row


def _mlstm_step(qk_ref, vt_ref, sa_ref, ra_ref, slot, h_ref, ct_ref, m_ref, ci, fwd):
    L = M_CHUNK
    f32, bf16 = jnp.float32, jnp.bfloat16
    d = 0 if fwd else 1
    rows = pl.ds(pl.multiple_of(ci * L, L), L)
    last = L - 1 if fwd else 0
    lane_lo = lax.broadcasted_iota(jnp.int32, (L, LANES), 1) < M_DQK
    lane_lo_row = lax.broadcasted_iota(jnp.int32, (1, LANES), 1) < M_DQK
    ones = jnp.ones((M_DV, L), bf16)

    def halves(row, f):
        return jnp.concatenate([jnp.broadcast_to(f(row[:, :L]), (1, L)), jnp.broadcast_to(f(row[:, L:]), (1, L))],
                               axis=-1)

    for j in range(M_HEADS // 2):
        h0, h1 = 2 * j, 2 * j + 1
        sidx = 2 * j + d
        b_row, i_row = ra_ref[slot, d, j, 0:1, :], ra_ref[slot, d, j, 1:2, :]
        m_row = m_ref[sidx, 0:1, :]
        kp = qk_ref[0, rows, M_HEADS * M_DQK + j * LANES:M_HEADS * M_DQK + (j + 1) * LANES]
        vx0 = jnp.concatenate([vt_ref[0, ci, h0 * M_DV:(h0 + 1) * M_DV, :], ones], axis=0)
        vx1 = jnp.concatenate([vt_ref[0, ci, h1 * M_DV:(h1 + 1) * M_DV, :], ones], axis=0)
        ct = ct_ref[sidx]
        if h_ref is not None:
            dm = sa_ref[slot, d, j, 0]
            qq = _pair_queries(qk_ref[0, rows, j * LANES:(j + 1) * LANES])
            g_row = b_row + m_row
            mt = jnp.maximum(g_row, dm.max(axis=0, keepdims=True))
            sc = (sa_ref[slot, d, j, 1] * jnp.exp(dm - mt)).astype(bf16)
            inter = jnp.exp(g_row - mt)
            t1 = lax.dot_general(ct.astype(bf16), qq, (((1,), (1,)), ((), ())), preferred_element_type=f32)
            t2 = jnp.concatenate([_dot_f32(vx0, sc[:, :L]), _dot_f32(vx1, sc[:, L:])], axis=-1)
            tot = inter * t1 + t2
            h_t = tot[:M_DV] / jnp.maximum(jnp.abs(tot[M_DV:]), jnp.exp(-mt))
            h_ref[rows, h0 * M_DV:(h0 + 1) * M_DV] = h_t[:, :L].T
            h_ref[rows, h1 * M_DV:(h1 + 1) * M_DV] = h_t[:, L:].T
        bl = halves(b_row, lambda r: r[:, last:last + 1])
        w = bl - b_row + i_row
        m_new = jnp.maximum(bl + m_row, halves(w, lambda r: r.max(axis=-1, keepdims=True)))
        decay = jnp.exp(bl + m_row - m_new)
        ws = jnp.exp(w - m_new)
        vxs = jnp.concatenate([vx0.astype(f32) * ws[:, :L], vx1.astype(f32) * ws[:, L:]], axis=-1).astype(bf16)
        zk = jnp.zeros_like(kp)
        kk = jnp.concatenate([jnp.where(lane_lo, kp, zk), jnp.where(lane_lo, zk, kp)], axis=0)
        d_cols = jnp.where(lane_lo_row, jnp.broadcast_to(decay[:, 0:1], (1, LANES)),
                           jnp.broadcast_to(decay[:, L:L + 1], (1, LANES)))
        ct_ref[sidx] = d_cols * ct + _dot_f32(vxs, kk)
        m_ref[sidx, 0:1, :] = m_new


def _mlstm_finish(hf_ref, hb_ref, o_ref, out_ref, n_chunks):
    L = M_CHUNK

    def body(ci, carry):
        rows = pl.ds(pl.multiple_of(ci * L, L), L)
        h = hf_ref[rows, :] + hb_ref[rows, :]
        parts = []
        for hd in range(M_HEADS):
            hh = h[:, hd * M_DV:(hd + 1) * M_DV]
            parts.append(hh * lax.rsqrt(jnp.mean(hh * hh, axis=-1, keepdims=True) + RMS_EPS))
        gate = _sigmoid(o_ref[0, rows, :].astype(jnp.float32))
        out_ref[0, rows, :] = (gate * jnp.concatenate(parts, axis=-1)).astype(out_ref.dtype)
        return carry

    lax.fori_loop(0, n_chunks, body, 0)


def _mlstm_kernel(qk_c, v_c, o_c, g_c, gt_c, qk_l, v_l, o_l, g_l, gt_l, *rest, need_ctx):
    if need_ctx:
        out_l, out_c, hf_l, hb_l, hf_c, hb_c, ct_ref, m_ref, sa_ref, ra_ref = rest
    else:
        out_l, hf_l, hb_l, ct_ref, m_ref, sa_ref, ra_ref = rest
        hf_c = hb_c = None
    nc, nl = qk_c.shape[1] // M_CHUNK, qk_l.shape[1] // M_CHUNK
    ct_ref[...] = jnp.zeros_like(ct_ref)
    m_ref[...] = jnp.full_like(m_ref, M_INIT)

    def phase(qk, vt, g, gt, hf, hb, n):
        def prep(s, slot):
            _mlstm_prep(qk, g, gt, sa_ref, ra_ref, slot, s, True)
            _mlstm_prep(qk, g, gt, sa_ref, ra_ref, slot, n - 1 - s, False)

        def step(s, slot):
            _mlstm_step(qk, vt, sa_ref, ra_ref, slot, hf, ct_ref, m_ref, s, True)
            _mlstm_step(qk, vt, sa_ref, ra_ref, slot, hb, ct_ref, m_ref, n - 1 - s, False)

        prep(0, 0)

        def body(i, carry):
            s0 = 2 * i
            prep(s0 + 1, 1)
            step(s0, 0)
            prep(jnp.minimum(s0 + 2, n - 1), 0)
            step(s0 + 1, 1)
            return carry
        lax.fori_loop(0, n // 2, body, 0)

    phase(qk_c, v_c, g_c, gt_c, hf_c, hb_c, nc)
    phase(qk_l, v_l, g_l, gt_l, hf_l, hb_l, nl)
    _mlstm_finish(hf_l, hb_l, o_l, out_l, nl)
    if need_ctx:
        _mlstm_finish(hf_c, hb_c, o_c, out_c, nc)


def mlstm(pc, pl_, need_ctx):
    B, S, W = pl_["mo"].shape
    Tc = pc["mo"].shape[1]
    f32 = jnp.float32

    def specs(T):
        s = pl.BlockSpec((1, T, W), lambda b: (b, 0, 0))
        return [s, pl.BlockSpec((1, T // M_CHUNK, W, M_CHUNK), lambda b: (b, 0, 0, 0)), s,
                pl.BlockSpec((1, T, GATE_W), lambda b: (b, 0, 0)),
                pl.BlockSpec((1, T // M_CHUNK, 4 * M_HEADS, M_CHUNK), lambda b: (b, 0, 0, 0))]

    def args(p):
        return [p["mqk"], p["mvt"], p["mo"], p["g"], p["gt"]]

    out_shape = [jax.ShapeDtypeStruct((B, S, W), jnp.bfloat16)]
    out_specs = [pl.BlockSpec((1, S, W), lambda b: (b, 0, 0))]
    scratch = [pltpu.VMEM((S, W), f32), pltpu.VMEM((S, W), f32)]
    if need_ctx:
        out_shape.append(jax.ShapeDtypeStruct((B, Tc, W), jnp.bfloat16))
        out_specs.append(pl.BlockSpec((1, Tc, W), lambda b: (b, 0, 0)))
        scratch += [pltpu.VMEM((Tc, W), f32), pltpu.VMEM((Tc, W), f32)]
    assert (S // M_CHUNK) % 2 == 0 and (Tc // M_CHUNK) % 2 == 0
    n_pairs = M_HEADS // 2
    scratch += [pltpu.VMEM((M_HEADS, 2 * M_DV, LANES), f32), pltpu.VMEM((M_HEADS, 8, 2 * M_CHUNK), f32),
                pltpu.VMEM((2, 2, n_pairs, 2, M_CHUNK, 2 * M_CHUNK), f32),
                pltpu.VMEM((2, 2, n_pairs, 8, 2 * M_CHUNK), f32)]
    res = pl.pallas_call(
        functools.partial(_mlstm_kernel, need_ctx=need_ctx),
        grid=(B,),
        in_specs=specs(Tc) + specs(S),
        out_specs=out_specs,
        out_shape=out_shape,
        scratch_shapes=scratch,
        compiler_params=pltpu.CompilerParams(dimension_semantics=("arbitrary",), vmem_limit_bytes=VMEM_LIMIT),
        name="mlstm",
    )(*args(pc), *args(pl_))
    return (res[0], res[1]) if need_ctx else (res[0], None)


POOL_HALO = 16


def _pool_mix(pu_ref, prev_ref, next_ref, pw_ref, ps_ref, tile_idx, n_tiles, seq_len):
    f32 = jnp.float32
    tm = pu_ref.shape[1]
    cur = pu_ref[0].astype(f32)
    prev = jnp.where(tile_idx > 0, prev_ref[0].astype(f32), 0.0)
    nxt = jnp.where(tile_idx < n_tiles - 1, next_ref[0].astype(f32), 0.0)
    ext = jnp.concatenate([prev, cur, nxt], axis=0)
    n_ext = tm + 2 * POOL_HALO
    tok = tile_idx * tm + lax.broadcasted_iota(jnp.int32, (tm, POOL_GC), 0)

    def shifted(a, d):
        return pltpu.roll(a, (-d) % n_ext, 0)

    outs = []
    for gi, w in enumerate(POOL_WINDOWS):
        s = ext[:, gi * POOL_GC:(gi + 1) * POOL_GC]
        s = shifted(s, -1) + s
        span = 2
        while span < w:
            s = shifted(s, -(span // 2)) + shifted(s, span // 2)
            span *= 2
        lo = jnp.clip(tok - w // 2, 0, seq_len - 1)
        hi = jnp.clip(tok - w // 2 + w - 1, 0, seq_len - 1)
        mean = s[POOL_HALO:POOL_HALO + tm] / (hi - lo + 1).astype(f32)
        d = (mean - cur[:, gi * POOL_GC:(gi + 1) * POOL_GC]).astype(jnp.bfloat16)
        outs.append(_dot_f32(d, pw_ref[gi]))
    return jnp.concatenate(outs, axis=-1) * ps_ref[...]


def _pack_bf16_pairs(h):
    n = h.shape[1] // 2
    return pltpu.pack_elementwise([h[:, :n], h[:, n:]], packed_dtype=jnp.bfloat16)


def _merge_kernel(m_ref, n_ref, pu_ref, prev_ref, next_ref, bg_ref, x_ref, mods_ref, gains_ref,
                  pw_ref, ps_ref, wb_ref, wo_ref, wrt_ref, wrp_ref, x1_ref, h2p_ref, affr_ref, aff_ref, *, seq_len):
    f32, bf16 = jnp.float32, jnp.bfloat16
    i = pl.program_id(1)
    D = x_ref.shape[2]
    E = wrt_ref.shape[0]
    p = _pool_mix(pu_ref, prev_ref, next_ref, pw_ref, ps_ref, i, pl.num_programs(1), seq_len)
    branches = (m_ref[0], p.astype(bf16), n_ref[0])
    acc = None
    for bi, br in enumerate(branches):
        gate = jnp.tanh(bg_ref[0, :, bi * D:(bi + 1) * D].astype(f32)) + 1.0
        term = gate * _dot_f32(br, wb_ref[bi])
        acc = term if acc is None else acc + term
    y = _dot_f32(acc.astype(bf16), wo_ref[...])
    yn = y * lax.rsqrt(jnp.mean(y * y, axis=-1, keepdims=True) + RMS_EPS) * gains_ref[0:1, :]
    x1 = x_ref[0] + mods_ref[0, 0:1, :] * yn
    x1_ref[0] = x1
    xn = x1 * lax.rsqrt(jnp.mean(x1 * x1, axis=-1, keepdims=True) + RMS_EPS) * gains_ref[1:2, :]
    h2 = xn * (1.0 + mods_ref[0, 2:3, :]) + mods_ref[0, 1:2, :]
    h2b = h2.astype(bf16)
    lg_t = lax.dot_general(wrt_ref[...], h2b, (((1,), (1,)), ((), ())), preferred_element_type=f32)
    e_t = jnp.exp(lg_t - lg_t.max(axis=0, keepdims=True))
    aff_ref[0] = e_t / e_t.sum(axis=0, keepdims=True)
    lg = _dot_f32(h2b, wrp_ref[...])
    lg = jnp.where(lax.broadcasted_iota(jnp.int32, lg.shape, 1) < E, lg, MASK_NEG)
    e_r = jnp.exp(lg - lg.max(axis=-1, keepdims=True))
    affr_ref[0] = e_r / e_r.sum(axis=-1, keepdims=True)
    h2p_ref[0] = _pack_bf16_pairs(h2)


def merge_out(m, n, pu, bg, x, mods, gains, pool_w, pool_scale, w_branch, w_out, w_router_t, w_router_pad, *, tm):
    B, T, D = x.shape
    W = m.shape[2]
    E = w_router_t.shape[0]
    hb = tm // POOL_HALO
    n_halo = T // POOL_HALO
    tile = lambda w: pl.BlockSpec((1, tm, w), lambda b, i: (b, i, 0))
    in_specs = [
        tile(W), tile(W), tile(W),
        pl.BlockSpec((1, POOL_HALO, W), lambda b, i: (b, jnp.maximum(i * hb - 1, 0), 0)),
        pl.BlockSpec((1, POOL_HALO, W), lambda b, i: (b, jnp.minimum((i + 1) * hb, n_halo - 1), 0)),
        tile(N_BRANCH * D), tile(D),
        pl.BlockSpec((1, 8, D), lambda b, i: (b, 0, 0)),
        _const_spec((8, D)),
        _const_spec(pool_w.shape), _const_spec(pool_scale.shape), _const_spec(w_branch.shape),
        _const_spec(w_out.shape), _const_spec(w_router_t.shape), _const_spec(w_router_pad.shape),
    ]
    return pl.pallas_call(
        functools.partial(_merge_kernel, seq_len=T),
        grid=(B, T // tm),
        in_specs=in_specs,
        out_specs=[tile(D), tile(D // 2), tile(LANES), pl.BlockSpec((1, E, tm), lambda b, i: (b, 0, i))],
        out_shape=[jax.ShapeDtypeStruct((B, T, D), jnp.float32),
                   jax.ShapeDtypeStruct((B, T, D // 2), PACK_WORD),
                   jax.ShapeDtypeStruct((B, T, LANES), jnp.float32),
                   jax.ShapeDtypeStruct((B, E, T), jnp.float32)],
        compiler_params=pltpu.CompilerParams(
            dimension_semantics=("arbitrary", "arbitrary"), vmem_limit_bytes=VMEM_LIMIT),
        name="merge_out",
    )(m, n, pu, pu, pu, bg, x, mods, gains, pool_w, pool_scale, w_branch, w_out, w_router_t, w_router_pad)


def _cumsum_lanes(x01):
    R, T = x01.shape
    r = lax.broadcasted_iota(jnp.int32, (LANES, LANES), 0)
    c = lax.broadcasted_iota(jnp.int32, (LANES, LANES), 1)
    tri_u = jnp.where(r <= c, 1.0, 0.0).astype(jnp.bfloat16)
    run = jnp.zeros((R, 1), jnp.float32)
    outs = []
    for t0 in range(0, T, LANES):
        cs = _dot_f32(x01[:, t0:t0 + LANES].astype(jnp.bfloat16), tri_u) + run
        run = cs[:, LANES - 1:LANES]
        outs.append(cs)
    return jnp.concatenate(outs, axis=-1)


def _route_kernel(aff_ref, idx_ref, *, cap):
    f32 = jnp.float32
    aff = aff_ref[0]
    E, T = aff.shape

    def search(k, lo):
        cand = lo | (jnp.int32(1) << (30 - k))
        cnt = jnp.sum(jnp.where(aff >= pltpu.bitcast(cand, f32), 1.0, 0.0), axis=-1, keepdims=True)
        return jnp.where(cnt >= cap, cand, lo)

    thr = pltpu.bitcast(lax.fori_loop(0, 31, search, jnp.zeros((E, 1), jnp.int32)), f32)
    gt = jnp.where(aff > thr, 1.0, 0.0)
    eq = jnp.where(aff == thr, 1.0, 0.0)
    room = cap - jnp.sum(gt, axis=-1, keepdims=True)
    sel = gt + eq * jnp.where(_cumsum_lanes(eq) <= room, 1.0, 0.0)
    csel = _cumsum_lanes(sel)
    slot = lax.broadcasted_iota(jnp.int32, (cap, T), 0).astype(f32)
    lane = lax.broadcasted_iota(jnp.int32, (cap, LANES), 1)
    out = jnp.zeros((cap, LANES), f32)
    for e in range(E):
        pos = jnp.sum(jnp.where(jnp.broadcast_to(csel[e:e + 1, :], (cap, T)) <= slot, 1.0, 0.0),
                      axis=-1, keepdims=True)
        out = jnp.where(lane == e, jnp.broadcast_to(pos, (cap, LANES)), out)
    idx_ref[0] = out.astype(jnp.int32)


def route(aff_t, cap):
    B, E, T = aff_t.shape
    idx_t = pl.pallas_call(
        functools.partial(_route_kernel, cap=cap),
        grid=(B,),
        in_specs=[pl.BlockSpec((1, E, T), lambda b: (b, 0, 0))],
        out_specs=pl.BlockSpec((1, cap, LANES), lambda b: (b, 0, 0)),
        out_shape=jax.ShapeDtypeStruct((B, cap, LANES), jnp.int32),
        compiler_params=pltpu.CompilerParams(dimension_semantics=("arbitrary",), vmem_limit_bytes=VMEM_LIMIT),
        name="route",
    )(aff_t)
    return idx_t[:, :, :E].transpose(0, 2, 1)


ROW_GROUP = 8


def _moe_kernel(idx_ref, rows_ref, affr_ref, wg_ref, wu_ref, wd_ref, out_ref,
                xg0_ref, xg1_ref, ag0_ref, ag1_ref, ye0_ref, ye1_ref):
    f32, bf16 = jnp.float32, jnp.bfloat16
    e = pl.program_id(1)
    n_exp = pl.num_programs(1)
    cap = xg0_ref.shape[0]

    def gather_row(dst, ee, j):
        i = idx_ref[0, 0, ee, j]
        dst[0][pl.ds(j, 1), :] = rows_ref[0, pl.ds(i, 1), :]
        dst[1][pl.ds(j, 1), :] = affr_ref[0, pl.ds(i, 1), :]

    def scatter_group(src_ref, ee, j0):
        ids = [idx_ref[0, 0, ee, j0 + r] for r in range(ROW_GROUP)]
        cur = [out_ref[0, pl.ds(i, 1), :] for i in ids]
        for r, i in enumerate(ids):
            out_ref[0, pl.ds(i, 1), :] = cur[r] + src_ref[pl.ds(j0 + r, 1), :]

    @pl.when(e == 0)
    def _():
        out_ref[...] = jnp.zeros_like(out_ref)
        ye1_ref[...] = jnp.zeros_like(ye1_ref)

        def body(gi, carry):
            for r in range(ROW_GROUP):
                gather_row((xg0_ref, ag0_ref), 0, gi * ROW_GROUP + r)
            return carry
        lax.fori_loop(0, cap // ROW_GROUP, body, 0)

    def step(cur, nxt, ye_cur, ye_prv):
        e_prv = jnp.maximum(e - 1, 0)
        e_nxt = jnp.minimum(e + 1, n_exp - 1)
        for j0 in range(0, cap, ROW_GROUP):
            scatter_group(ye_prv, e_prv, j0)
        for j in range(cap):
            gather_row(nxt, e_nxt, j)
        packed = cur[0][...]
        xe = jnp.concatenate(
            [pltpu.unpack_elementwise(packed, index=i, packed_dtype=bf16, unpacked_dtype=f32) for i in (0, 1)],
            axis=-1).astype(bf16)
        aff = cur[1][...]
        gate = jnp.sum(jnp.where(lax.broadcasted_iota(jnp.int32, aff.shape, 1) == e, aff, 0.0),
                       axis=-1, keepdims=True)
        a = _dot_f32(xe, wg_ref[0])
        hid = (a * _sigmoid(a) * _dot_f32(xe, wu_ref[0])).astype(bf16)
        ye_cur[...] = _dot_f32(hid, wd_ref[0]) * gate

    @pl.when(e % 2 == 0)
    def _():
        step((xg0_ref, ag0_ref), (xg1_ref, ag1_ref), ye0_ref, ye1_ref)

    @pl.when(e % 2 == 1)
    def _():
        step((xg1_ref, ag1_ref), (xg0_ref, ag0_ref), ye1_ref, ye0_ref)

    @pl.when(e == n_exp - 1)
    def _():
        def body(gi, carry):
            scatter_group(ye1_ref, e, gi * ROW_GROUP)
            return carry
        lax.fori_loop(0, cap // ROW_GROUP, body, 0)


def moe_experts(rows, aff_rows, idx, w_gate, w_up, w_down):
    G, T, RW = rows.shape
    AW = aff_rows.shape[2]
    _, E, cap = idx.shape
    D = w_gate.shape[1]
    assert E % 2 == 0 and cap % ROW_GROUP == 0
    wspec = pl.BlockSpec((1, D, D), lambda g, e: (e, 0, 0))
    return pl.pallas_call(
        _moe_kernel,
        grid=(G, E),
        in_specs=[pl.BlockSpec((1, 1, E, cap), lambda g, e: (g, 0, 0, 0), memory_space=pltpu.SMEM),
                  pl.BlockSpec((1, T, RW), lambda g, e: (g, 0, 0)), pl.BlockSpec((1, T, AW), lambda g, e: (g, 0, 0)),
                  wspec, wspec, wspec],
        out_specs=pl.BlockSpec((1, T, D), lambda g, e: (g, 0, 0)),
        out_shape=jax.ShapeDtypeStruct((G, T, D), jnp.float32),
        scratch_shapes=[pltpu.VMEM((cap, RW), rows.dtype), pltpu.VMEM((cap, RW), rows.dtype),
                        pltpu.VMEM((cap, AW), jnp.float32), pltpu.VMEM((cap, AW), jnp.float32),
                        pltpu.VMEM((cap, D), jnp.float32), pltpu.VMEM((cap, D), jnp.float32)],
        compiler_params=pltpu.CompilerParams(
            dimension_semantics=("arbitrary", "arbitrary"), vmem_limit_bytes=VMEM_LIMIT),
        name="moe_experts",
    )(idx.reshape(G, 1, E, cap), rows, aff_rows, w_gate, w_up, w_down)


def _residual_norm_kernel(x_ref, y_ref, gate_ref, gain_ref, o_ref):
    y = y_ref[0]
    yn = y * lax.rsqrt(jnp.mean(y * y, axis=-1, keepdims=True) + RMS_EPS) * gain_ref[...]
    o_ref[0] = x_ref[0] + gate_ref[0] * yn


def residual_norm(x, y, gate, gain, *, tm):
    B, T, D = x.shape
    tile = pl.BlockSpec((1, tm, D), lambda b, i: (b, i, 0))
    return pl.pallas_call(
        _residual_norm_kernel,
        grid=(B, T // tm),
        in_specs=[tile, tile, pl.BlockSpec((1, 1, D), lambda b, i: (b, 0, 0)), _const_spec((1, D))],
        out_specs=tile,
        out_shape=jax.ShapeDtypeStruct((B, T, D), jnp.float32),
        compiler_params=pltpu.CompilerParams(
            dimension_semantics=("arbitrary", "arbitrary"), vmem_limit_bytes=VMEM_LIMIT),
        name="residual_norm",
    )(x, y, gate, gain)


ADA_COLS = 1536


def _ada_kernel(c_ref, w_ref, b_ref, o_ref):
    c = c_ref[...]
    h = (c * _sigmoid(c)).astype(jnp.bfloat16)
    o_ref[...] = _dot_f32(h, w_ref[...].astype(jnp.bfloat16)) + b_ref[...]


def ada_modulation(cond, ada_w_l, ada_b_l):
    R, D = cond.shape
    N = ada_w_l.shape[1]
    return pl.pallas_call(
        _ada_kernel,
        grid=(N // ADA_COLS,),
        in_specs=[pl.BlockSpec((R, D), lambda j: (0, 0)), pl.BlockSpec((D, ADA_COLS), lambda j: (0, j)),
                  pl.BlockSpec((1, ADA_COLS), lambda j: (0, j))],
        out_specs=pl.BlockSpec((R, ADA_COLS), lambda j: (0, j)),
        out_shape=jax.ShapeDtypeStruct((R, N), jnp.float32),
        compiler_params=pltpu.CompilerParams(dimension_semantics=("arbitrary",), vmem_limit_bytes=VMEM_LIMIT),
        name="ada_modulation",
    )(cond, ada_w_l, ada_b_l[None])


def kernel(x, c, ctx, c_ctx, norm_gain, ada_w, ada_b, w_in, mlstm_gate_bias, pool_w, pool_scale,
           na_rpb, w_branch, w_out, router_w, w_gate, w_up, w_down):
    B, S, D = x.shape
    Tc = ctx.shape[1]
    f32, bf16 = jnp.float32, jnp.bfloat16
    rope_tabs = rope_lane_tables(S)
    names = [n for n, _ in IN_GROUPS] + ["gt"]
    cond = jnp.pad(jnp.concatenate([c, c_ctx[None]], axis=0), ((0, (-(B + 1)) % 8), (0, 0)))
    pending = None
    for l in range(DEPTH):
        need_ctx = l < DEPTH - 1
        g = norm_gain[l]
        mods = ada_modulation(cond, ada_w[l], ada_b[l])
        mod_l = jnp.split(mods[:B, None, :], N_MOD, axis=-1)
        mod_cb = [jnp.broadcast_to(m[None, None], (B, 1, D)) for m in jnp.split(mods[B], N_MOD, axis=-1)]
        w_perm = permute_w_in(w_in[l])
        gb_row = jnp.pad(mlstm_gate_bias[l].reshape(1, -1).astype(f32), ((0, 0), (0, GATE_W - 4 * M_HEADS)))
        proj = in_proj(x, g[0][None], mod_l[0], mod_l[1], w_perm, gb_row, rope_tabs, pending, tm=512)
        if pending is not None:
            x = proj[-1]
        pl_ = dict(zip(names, proj))
        pc_ = dict(zip(names, in_proj(ctx, g[0][None], mod_cb[0], mod_cb[1], w_perm, gb_row, tm=Tc)))

        m_l, m_c = mlstm(pc_, pl_, need_ctx)
        n_l, n_c = neighbourhood_attention(pl_["qn"], pl_["kn"], pl_["vn"], pc_["kn"], pc_["vn"],
                                           na_bias_table(na_rpb[l]), pc_["qn"] if need_ctx else None)
        gains = jnp.pad(g[1:3], ((0, 6), (0, 0)))
        merge_w = (pool_w[l].astype(bf16), pool_scale[l][None], w_branch[l].astype(bf16),
                   (0.5 * w_out[l]).astype(bf16),
                   router_w[l].T.astype(bf16),
                   jnp.pad(router_w[l], ((0, 0), (0, LANES - N_EXPERTS))).astype(bf16))
        moe_w = (w_gate[l].astype(bf16), w_up[l].astype(bf16), w_down[l].astype(bf16))

        def mods8(mods):
            return jnp.pad(jnp.concatenate([mods[2], mods[3], mods[4]], axis=1), ((0, 0), (0, 5), (0, 0)))

        x1, rows, aff_r, aff_t = merge_out(m_l, n_l, pl_["pu"], pl_["bg"], x, mods8(mod_l), gains, *merge_w, tm=512)
        y = moe_experts(rows, aff_r, route(aff_t, EC_FACTOR * S // N_EXPERTS), *moe_w)
        if need_ctx:
            x, pending = x1, (y, mod_l[5], g[3][None])
        else:
            x = residual_norm(x1, y, mod_l[5], g[3][None], tm=512)
        if need_ctx:
            c1, rows, aff_r, aff_t = merge_out(m_c, n_c, pc_["pu"], pc_["bg"], ctx, mods8(mod_cb), gains, *merge_w, tm=Tc)
            per = S // Tc
            cap_c = EC_FACTOR * Tc // N_EXPERTS
            idx = route(aff_t, cap_c) + (jnp.arange(B, dtype=jnp.int32) % per * Tc)[:, None, None]
            idx = idx.reshape(B // per, per, N_EXPERTS, cap_c).transpose(0, 2, 1, 3).reshape(B // per, N_EXPERTS, per * cap_c)
            y = moe_experts(rows.reshape(B // per, S, D // 2), aff_r.reshape(B // per, S, LANES), idx,
                            *moe_w).reshape(B, Tc, D)
            ctx = residual_norm(c1, y, mod_cb[5], g[3][None], tm=Tc)
    return x
```

```python
import functools
import math

import jax
import jax.numpy as jnp
import numpy as np
from jax import lax
from jax.experimental import pallas as pl
from jax.experimental.pallas import tpu as pltpu

D_MODEL = 1024
DEPTH = 2
GRID_W = 64
BRANCH_W = D_MODEL // 2
N_BRANCH = 3
M_HEADS = 4
M_DV = BRANCH_W // M_HEADS
M_DQK = M_DV // 2
M_CHUNK = 128
M_INIT = -1e30
POOL_GROUPS = 4
POOL_GC = BRANCH_W // POOL_GROUPS
POOL_WINDOWS = (2, 4, 8, 16)
NA_HEADS = 8
NA_DH = BRANCH_W // NA_HEADS
NA_KH = 8
NA_KW = 16
N_EXPERTS = 16
EC_FACTOR = 2
ROPE_BASE = 10000.0
RMS_EPS = 1e-6
N_MOD = 6
SPLIT_SIZES = (M_HEADS * M_DQK, M_HEADS * M_DQK, M_HEADS * M_DV, M_HEADS * M_DV, 4 * M_HEADS,
               POOL_GROUPS * POOL_GC, NA_HEADS * NA_DH, NA_HEADS * NA_DH, NA_HEADS * NA_DH,
               N_BRANCH * D_MODEL)
PROJ_W = sum(SPLIT_SIZES)

LANES = 128
VMEM_LIMIT = 56 * 1024 * 1024
MASK_NEG = -1e30
GATE_W = LANES

IN_GROUPS = (("mqk", 2 * M_HEADS * M_DQK), ("mvt", BRANCH_W), ("mo", BRANCH_W), ("pu", BRANCH_W),
             ("qn", BRANCH_W), ("kn", BRANCH_W), ("vn", BRANCH_W), ("bg", N_BRANCH * D_MODEL),
             ("g", GATE_W))
IN_W = sum(w for _, w in IN_GROUPS)
MM_COLS = 512
PACK_WORD = jnp.uint32


def _sigmoid(x):
    return 0.5 * jnp.tanh(0.5 * x) + 0.5


def _const_spec(shape):
    nd = len(shape)
    return pl.BlockSpec(shape, lambda *_: (0,) * nd, pipeline_mode=pl.Buffered(1))


def _rope_tile(x, cos, sin_signed):
    half = M_DQK // 4
    lane = lax.broadcasted_iota(jnp.int32, x.shape, 1)
    partner = jnp.where((lane % (2 * half)) < half, pltpu.roll(x, LANES - half, 1), pltpu.roll(x, half, 1))
    return x * cos + partner * sin_signed


def _in_proj_kernel(x_ref, g_ref, shift_ref, scale_ref, w_ref, gb_ref, *rest, rope, pending):
    if rope:
        cos_ref, sin_ref = rest[:2]
        rest = rest[2:]
    if pending:
        r_ref, rgate_ref, rgain_ref = rest[:3]
        rest, xo_ref = rest[3:-1], rest[-1]
        r = r_ref[0]
        x = x_ref[0] + rgate_ref[0] * (r * lax.rsqrt(jnp.mean(r * r, axis=-1, keepdims=True) + RMS_EPS)
                                       * rgain_ref[...])
        xo_ref[0] = x
    else:
        x = x_ref[0]
    o_refs, gt_ref = rest[:-1], rest[-1]
    y = x * lax.rsqrt(jnp.mean(x * x, axis=-1, keepdims=True) + RMS_EPS)
    h = ((y * g_ref[...]) * (1.0 + scale_ref[0]) + shift_ref[0]).astype(jnp.bfloat16)
    off = 0
    for (name, width), o_ref in zip(IN_GROUPS, o_refs):
        for c0 in range(0, width, MM_COLS):
            cw = min(MM_COLS, width - c0)
            acc = jnp.dot(h, w_ref[:, off + c0:off + c0 + cw], preferred_element_type=jnp.float32)
            if name == "g":
                acc = acc + gb_ref[...]
                for c in range(acc.shape[0] // M_CHUNK):
                    gt_ref[0, c] = acc[c * M_CHUNK:(c + 1) * M_CHUNK, :].T[:4 * M_HEADS, :]
            if name == "mqk" and rope:
                cos, sin = cos_ref[...], sin_ref[...]
                acc = jnp.concatenate([_rope_tile(acc[:, t:t + LANES], cos, sin) for t in range(0, cw, LANES)], axis=-1)
            if name == "mvt":
                for c in range(acc.shape[0] // M_CHUNK):
                    o_ref[0, c] = acc[c * M_CHUNK:(c + 1) * M_CHUNK, :].T.astype(o_ref.dtype)
                continue
            o_ref[0, :, c0:c0 + cw] = acc.astype(o_ref.dtype)
        off += width


def in_proj(x, g, shift, scale, w_perm, gate_bias_row, rope_tabs=None, pending=None, *, tm):
    B, T, D = x.shape
    out_shape = [jax.ShapeDtypeStruct((B, T, w), jnp.float32 if n == "g" else jnp.bfloat16) for n, w in IN_GROUPS]
    out_specs = [pl.BlockSpec((1, tm, w), lambda b, i: (b, i, 0)) for _, w in IN_GROUPS]
    k_mvt = [n for n, _ in IN_GROUPS].index("mvt")
    assert BRANCH_W == MM_COLS and tm % M_CHUNK == 0
    out_shape[k_mvt] = jax.ShapeDtypeStruct((B, T // M_CHUNK, BRANCH_W, M_CHUNK), jnp.bfloat16)
    out_specs[k_mvt] = pl.BlockSpec((1, tm // M_CHUNK, BRANCH_W, M_CHUNK), lambda b, i: (b, i, 0, 0))
    out_shape.append(jax.ShapeDtypeStruct((B, T // M_CHUNK, 4 * M_HEADS, M_CHUNK), jnp.float32))
    out_specs.append(pl.BlockSpec((1, tm // M_CHUNK, 4 * M_HEADS, M_CHUNK), lambda b, i: (b, i, 0, 0)))
    in_specs = [
        pl.BlockSpec((1, tm, D), lambda b, i: (b, i, 0)),
        _const_spec((1, D)),
        pl.BlockSpec((1, 1, D), lambda b, i: (b, 0, 0)),
        pl.BlockSpec((1, 1, D), lambda b, i: (b, 0, 0)),
        _const_spec((D, IN_W)),
        _const_spec((1, GATE_W)),
    ]
    args = [x, g, shift, scale, w_perm, gate_bias_row]
    if rope_tabs is not None:
        in_specs += [pl.BlockSpec((tm, LANES), lambda b, i: (i, 0))] * 2
        args += list(rope_tabs)
    if pending is not None:
        in_specs += [pl.BlockSpec((1, tm, D), lambda b, i: (b, i, 0)), pl.BlockSpec((1, 1, D), lambda b, i: (b, 0, 0)),
                     _const_spec((1, D))]
        args += list(pending)
        out_shape.append(jax.ShapeDtypeStruct((B, T, D), jnp.float32))
        out_specs.append(pl.BlockSpec((1, tm, D), lambda b, i: (b, i, 0)))
    return pl.pallas_call(
        functools.partial(_in_proj_kernel, rope=rope_tabs is not None, pending=pending is not None),
        grid=(B, T // tm),
        in_specs=in_specs,
        out_specs=out_specs,
        out_shape=out_shape,
        compiler_params=pltpu.CompilerParams(
            dimension_semantics=("arbitrary", "arbitrary"), vmem_limit_bytes=VMEM_LIMIT),
        name="in_proj",
    )(*args)


def rope_lane_tables(n_tokens):
    t = jnp.arange(n_tokens)
    row = (t // GRID_W).astype(jnp.float32)
    col = (t % GRID_W).astype(jnp.float32)
    half = M_DQK // 2
    inv = ROPE_BASE ** (-jnp.arange(0, half, 2, dtype=jnp.float32) / half)
    ar = row[:, None] * inv[None, :]
    ac = col[:, None] * inv[None, :]
    cos = jnp.concatenate([jnp.cos(ar), jnp.cos(ar), jnp.cos(ac), jnp.cos(ac)], axis=-1)
    sin = jnp.concatenate([-jnp.sin(ar), jnp.sin(ar), -jnp.sin(ac), jnp.sin(ac)], axis=-1)
    return jnp.tile(cos, (1, LANES // M_DQK)), jnp.tile(sin, (1, LANES // M_DQK))


def permute_w_in(w_in_l):
    qm, km, vm, om, gm, pu, qn, kn, vn, bg = jnp.split(w_in_l, [int(s) for s in np.cumsum(SPLIT_SIZES)[:-1]], axis=-1)
    gpad = jnp.pad(gm, ((0, 0), (0, GATE_W - gm.shape[1])))
    w = jnp.concatenate([qm * (M_DQK ** -0.5), km, vm, om, pu, qn * (NA_DH ** -0.5), kn, vn, bg * 0.5, gpad], axis=-1)
    return w.astype(jnp.bfloat16)


def _pair_scores(qp, k_parts, biases):
    n = qp.shape[0]
    lane_lo = lax.broadcasted_iota(jnp.int32, (n, LANES), 1) < NA_DH
    zero = jnp.zeros_like(qp)
    qq = jnp.concatenate([jnp.where(lane_lo, qp, zero), jnp.where(lane_lo, zero, qp)], axis=0)
    scores = []
    for kp, bias in zip(k_parts, biases):
        s = lax.dot_general(qq, kp, (((1,), (1,)), ((), ())), preferred_element_type=jnp.float32)
        scores.append(s if bias is None else s + bias)
    return scores


def _pair_softmax_pv(scores, v_parts):
    n = scores[0].shape[0] // 2
    lane_lo = lax.broadcasted_iota(jnp.int32, (n, LANES), 1) < NA_DH
    m = scores[0].max(axis=-1, keepdims=True)
    for s in scores[1:]:
        m = jnp.maximum(m, s.max(axis=-1, keepdims=True))
    l = None
    o = None
    for s, vp in zip(scores, v_parts):
        p = jnp.exp(s - m)
        ls = p.sum(axis=-1, keepdims=True)
        os_ = jnp.dot(p.astype(jnp.bfloat16), vp, preferred_element_type=jnp.float32)
        l = ls if l is None else l + ls
        o = os_ if o is None else o + os_
    o = o * (1.0 / l)
    return jnp.where(lane_lo, o[:n], o[n:])


def _na_kernel(q_ref, k_ref, v_ref, kc_ref, vc_ref, bias_ref, *rest, need_ctx):
    if need_ctx:
        qc_ref, o_ref, oc_ref, s_ref = rest
    else:
        o_ref, s_ref = rest
    S = q_ref.shape[1]
    Tc = kc_ref.shape[1]
    rows = S // GRID_W
    n_pairs = BRANCH_W // LANES
    n_loc = NA_KH * GRID_W

    def window(r):
        rs = jnp.clip(r - NA_KH // 2, 0, rows - NA_KH)
        return r - rs, pl.multiple_of(r * GRID_W, GRID_W), pl.multiple_of(rs * GRID_W, GRID_W)

    def scores_stage(r, slot):
        var, q0, k0 = window(r)
        for j in range(n_pairs):
            ls = slice(j * LANES, (j + 1) * LANES)
            s_loc, s_ctx = _pair_scores(q_ref[0, pl.ds(q0, GRID_W), ls],
                                        (k_ref[0, pl.ds(k0, n_loc), ls], kc_ref[0, :, ls]),
                                        (bias_ref[var, j], None))
            s_ref[slot, j, :, :n_loc] = s_loc
            s_ref[slot, j, :, n_loc:] = s_ctx

    def output_stage(r, slot):
        _, q0, k0 = window(r)
        outs = []
        for j in range(n_pairs):
            ls = slice(j * LANES, (j + 1) * LANES)
            outs.append(_pair_softmax_pv((s_ref[slot, j, :, :n_loc], s_ref[slot, j, :, n_loc:]),
                                         (v_ref[0, pl.ds(k0, n_loc), ls], vc_ref[0, :, ls])))
        o_ref[0, pl.ds(q0, GRID_W), :] = jnp.concatenate(outs, axis=-1).astype(o_ref.dtype)

    scores_stage(0, 0)

    def two_rows(i, carry):
        r0 = 2 * i
        scores_stage(r0 + 1, 1)
        output_stage(r0, 0)
        scores_stage(jnp.minimum(r0 + 2, rows - 1), 0)
        output_stage(r0 + 1, 1)
        return carry

    lax.fori_loop(0, rows // 2, two_rows, 0)

    if need_ctx:
        outs = []
        for j in range(n_pairs):
            ls = slice(j * LANES, (j + 1) * LANES)
            scores = _pair_scores(qc_ref[0, :, ls], (kc_ref[0, :, ls],), (None,))
            outs.append(_pair_softmax_pv(scores, (vc_ref[0, :, ls],)))
        oc_ref[0] = jnp.concatenate(outs, axis=-1).astype(oc_ref.dtype)


def na_bias_table(rpb):
    H = rpb.shape[0]
    var = jnp.arange(NA_KH)
    kr = jnp.arange(NA_KH)
    dr = kr[None, :] - var[:, None] + NA_KH - 1
    cols = jnp.arange(GRID_W)
    dc = jnp.clip(cols[None, :] - cols[:, None] + NA_KW - 1, 0, 2 * NA_KW - 2)
    cs = jnp.clip(cols - NA_KW // 2, 0, GRID_W - NA_KW)
    colmask = (cols[None, :] >= cs[:, None]) & (cols[None, :] < cs[:, None] + NA_KW)
    pick_r = jax.nn.one_hot(dr, 2 * NA_KH - 1, dtype=jnp.float32)
    pick_c = jax.nn.one_hot(dc, 2 * NA_KW - 1, dtype=jnp.float32)
    tab = jnp.einsum('vka,hab,qcb->vhqkc', pick_r, rpb.astype(jnp.float32), pick_c, precision=lax.Precision.HIGHEST)
    tab = jnp.where(colmask[None, None, :, None, :], tab, MASK_NEG)
    return tab.reshape(NA_KH, H // 2, 2 * GRID_W, NA_KH * GRID_W)


def neighbourhood_attention(qn, kn, vn, kc, vc, bias_tab, qc=None):
    B, S, W = qn.shape
    Tc = kc.shape[1]
    need_ctx = qc is not None
    lat = pl.BlockSpec((1, S, W), lambda b: (b, 0, 0))
    cx = pl.BlockSpec((1, Tc, W), lambda b: (b, 0, 0))
    in_specs = [lat, lat, lat, cx, cx, _const_spec(bias_tab.shape)]
    args = [qn, kn, vn, kc, vc, bias_tab]
    out_shape = [jax.ShapeDtypeStruct((B, S, W), jnp.bfloat16)]
    out_specs = [lat]
    if need_ctx:
        in_specs.append(cx)
        args.append(qc)
        out_shape.append(jax.ShapeDtypeStruct((B, Tc, W), jnp.bfloat16))
        out_specs.append(cx)
    res = pl.pallas_call(
        functools.partial(_na_kernel, need_ctx=need_ctx),
        grid=(B,),
        in_specs=in_specs,
        out_specs=out_specs,
        out_shape=out_shape,
        scratch_shapes=[pltpu.VMEM((2, W // LANES, 2 * GRID_W, NA_KH * GRID_W + Tc), jnp.float32)],
        compiler_params=pltpu.CompilerParams(dimension_semantics=("arbitrary",), vmem_limit_bytes=VMEM_LIMIT),
        name="neighbourhood_attention",
    )(*args)
    return (res[0], res[1]) if need_ctx else (res[0], None)


def _split_bf16(x):
    hi = x.astype(jnp.bfloat16)
    r1 = x - hi.astype(jnp.float32)
    mid = r1.astype(jnp.bfloat16)
    lo = (r1 - mid.astype(jnp.float32)).astype(jnp.bfloat16)
    return hi, mid, lo


def _dot_f32(a, b):
    return jnp.dot(a, b, preferred_element_type=jnp.float32)


def _log_sigmoid(x):
    return jnp.minimum(x, 0.0) - jnp.log(1.0 + jnp.exp(-jnp.abs(x)))


def _pair_queries(qp):
    lane_lo = lax.broadcasted_iota(jnp.int32, qp.shape, 1) < M_DQK
    zq = jnp.zeros_like(qp)
    return jnp.concatenate([jnp.where(lane_lo, qp, zq), jnp.where(lane_lo, zq, qp)], axis=0)


def _mlstm_prep(qk_ref, g_ref, gt_ref, sa_ref, ra_ref, slot, ci, fwd):
    L = M_CHUNK
    f32, bf16 = jnp.float32, jnp.bfloat16
    d = 0 if fwd else 1
    rows = pl.ds(pl.multiple_of(ci * L, L), L)
    sq_r = lax.broadcasted_iota(jnp.int32, (L, L), 0)
    sq_c = lax.broadcasted_iota(jnp.int32, (L, L), 1)
    tri_l = jnp.where(sq_r >= sq_c, 1.0, 0.0).astype(bf16)
    tri_u = jnp.where(sq_r <= sq_c, 1.0, 0.0).astype(bf16)
    gates = g_ref[0, rows, :]
    gates_t = gt_ref[0, ci]
    lf, lf_t = _log_sigmoid(gates), _log_sigmoid(gates_t)
    cum = sum(_dot_f32(tri_l if fwd else tri_u, p) for p in _split_bf16(lf))
    cum_t = sum(_dot_f32(p, tri_u if fwd else tri_l) for p in _split_bf16(lf_t))
    t_i = 0 if fwd else 2
    st_s = lax.broadcasted_iota(jnp.int32, (L, 2 * L), 0)
    st_t = lax.broadcasted_iota(jnp.int32, (L, 2 * L), 1) & (L - 1)
    valid = (st_s <= st_t) if fwd else (st_s >= st_t)

    def row2(tile, c0, c1):
        return jnp.concatenate([tile[c0:c0 + 1, :], tile[c1:c1 + 1, :]], axis=-1)

    for j in range(M_HEADS // 2):
        ci0, ci1 = t_i * M_HEADS + 2 * j, t_i * M_HEADS + 2 * j + 1
        cf0, cf1 = ci0 + M_HEADS, ci1 + M_HEADS
        b_row, i_row = row2(cum_t, cf0, cf1), row2(gates_t, ci0, ci1)
        colb = jnp.concatenate([jnp.broadcast_to(gates[:, ci0:ci0 + 1] - cum[:, cf0:cf0 + 1], (L, L)),
                                jnp.broadcast_to(gates[:, ci1:ci1 + 1] - cum[:, cf1:cf1 + 1], (L, L))],
                               axis=-1)
        qp = qk_ref[0, rows, j * LANES:(j + 1) * LANES]
        kp = qk_ref[0, rows, M_HEADS * M_DQK + j * LANES:M_HEADS * M_DQK + (j + 1) * LANES]
        sa_ref[slot, d, j, 0] = jnp.where(valid, b_row + colb, MASK_NEG)
        sa_ref[slot, d, j, 1] = lax.dot_general(kp, _pair_queries(qp), (((1,), (1,)), ((), ())),
                                                preferred_element_type=f32)
        ra_ref[slot, d, j, 0:1, :] = b_row
        ra_ref[slot, d, j, 1:2, :] = i_row


def _mlstm_step(qk_ref, vt_ref, sa_ref, ra_ref, slot, h_ref, ct_ref, m_ref, ci, fwd):
    L = M_CHUNK
    f32, bf16 = jnp.float32, jnp.bfloat16
    d = 0 if fwd else 1
    rows = pl.ds(pl.multiple_of(ci * L, L), L)
    last = L - 1 if fwd else 0
    lane_lo = lax.broadcasted_iota(jnp.int32, (L, LANES), 1) < M_DQK
    lane_lo_row = lax.broadcasted_iota(jnp.int32, (1, LANES), 1) < M_DQK
    ones = jnp.ones((M_DV, L), bf16)

    def halves(row, f):
        return jnp.concatenate([jnp.broadcast_to(f(row[:, :L]), (1, L)), jnp.broadcast_to(f(row[:, L:]), (1, L))],
                               axis=-1)

    for j in range(M_HEADS // 2):
        h0, h1 = 2 * j, 2 * j + 1
        sidx = 2 * j + d
        b_row, i_row = ra_ref[slot, d, j, 0:1, :], ra_ref[slot, d, j, 1:2, :]
        m_row = m_ref[sidx, 0:1, :]
        kp = qk_ref[0, rows, M_HEADS * M_DQK + j * LANES:M_HEADS * M_DQK + (j + 1) * LANES]
        vx0 = jnp.concatenate([vt_ref[0, ci, h0 * M_DV:(h0 + 1) * M_DV, :], ones], axis=0)
        vx1 = jnp.concatenate([vt_ref[0, ci, h1 * M_DV:(h1 + 1) * M_DV, :], ones], axis=0)
        ct = ct_ref[sidx]
        if h_ref is not None:
            dm = sa_ref[slot, d, j, 0]
            qq = _pair_queries(qk_ref[0, rows, j * LANES:(j + 1) * LANES])
            g_row = b_row + m_row
            mt = jnp.maximum(g_row, dm.max(axis=0, keepdims=True))
            sc = (sa_ref[slot, d, j, 1] * jnp.exp(dm - mt)).astype(bf16)
            inter = jnp.exp(g_row - mt)
            t1 = lax.dot_general(ct.astype(bf16), qq, (((1,), (1,)), ((), ())), preferred_element_type=f32)
            t2 = jnp.concatenate([_dot_f32(vx0, sc[:, :L]), _dot_f32(vx1, sc[:, L:])], axis=-1)
            tot = inter * t1 + t2
            h_t = tot[:M_DV] / jnp.maximum(jnp.abs(tot[M_DV:]), jnp.exp(-mt))
            h_ref[rows, h0 * M_DV:(h0 + 1) * M_DV] = h_t[:, :L].T
            h_ref[rows, h1 * M_DV:(h1 + 1) * M_DV] = h_t[:, L:].T
        bl = halves(b_row, lambda r: r[:, last:last + 1])
        w = bl - b_row + i_row
        m_new = jnp.maximum(bl + m_row, halves(w, lambda r: r.max(axis=-1, keepdims=True)))
        decay = jnp.exp(bl + m_row - m_new)
        ws = jnp.exp(w - m_new)
        vxs = jnp.concatenate([vx0.astype(f32) * ws[:, :L], vx1.astype(f32) * ws[:, L:]], axis=-1).astype(bf16)
        zk = jnp.zeros_like(kp)
        kk = jnp.concatenate([jnp.where(lane_lo, kp, zk), jnp.where(lane_lo, zk, kp)], axis=0)
        d_cols = jnp.where(lane_lo_row, jnp.broadcast_to(decay[:, 0:1], (1, LANES)),
                           jnp.broadcast_to(decay[:, L:L + 1], (1, LANES)))
        ct_ref[sidx] = d_cols * ct + _dot_f32(vxs, kk)
        m_ref[sidx, 0:1, :] = m_new


def _mlstm_finish(hf_ref, hb_ref, o_ref, out_ref, n_chunks):
    L = M_CHUNK

    def body(ci, carry):
        rows = pl.ds(pl.multiple_of(ci * L, L), L)
        h = hf_ref[rows, :] + hb_ref[rows, :]
        parts = []
        for hd in range(M_HEADS):
            hh = h[:, hd * M_DV:(hd + 1) * M_DV]
            parts.append(hh * lax.rsqrt(jnp.mean(hh * hh, axis=-1, keepdims=True) + RMS_EPS))
        gate = _sigmoid(o_ref[0, rows, :].astype(jnp.float32))
        out_ref[0, rows, :] = (gate * jnp.concatenate(parts, axis=-1)).astype(out_ref.dtype)
        return carry

    lax.fori_loop(0, n_chunks, body, 0)


def _mlstm_kernel(qk_c, v_c, o_c, g_c, gt_c, qk_l, v_l, o_l, g_l, gt_l, *rest, need_ctx):
    if need_ctx:
        out_l, out_c, hf_l, hb_l, hf_c, hb_c, ct_ref, m_ref, sa_ref, ra_ref = rest
    else:
        out_l, hf_l, hb_l, ct_ref, m_ref, sa_ref, ra_ref = rest
        hf_c = hb_c = None
    nc, nl = qk_c.shape[1] // M_CHUNK, qk_l.shape[1] // M_CHUNK
    ct_ref[...] = jnp.zeros_like(ct_ref)
    m_ref[...] = jnp.full_like(m_ref, M_INIT)

    def phase(qk, vt, g, gt, hf, hb, n):
        def prep(s, slot):
            _mlstm_prep(qk, g, gt, sa_ref, ra_ref, slot, s, True)
            _mlstm_prep(qk, g, gt, sa_ref, ra_ref, slot, n - 1 - s, False)

        def step(s, slot):
            _mlstm_step(qk, vt, sa_ref, ra_ref, slot, hf, ct_ref, m_ref, s, True)
            _mlstm_step(qk, vt, sa_ref, ra_ref, slot, hb, ct_ref, m_ref, n - 1 - s, False)

        prep(0, 0)

        def body(i, carry):
            s0 = 2 * i
            prep(s0 + 1, 1)
            step(s0, 0)
            prep(jnp.minimum(s0 + 2, n - 1), 0)
            step(s0 + 1, 1)
            return carry
        lax.fori_loop(0, n // 2, body, 0)

    phase(qk_c, v_c, g_c, gt_c, hf_c, hb_c, nc)
    phase(qk_l, v_l, g_l, gt_l, hf_l, hb_l, nl)
    _mlstm_finish(hf_l, hb_l, o_l, out_l, nl)
    if need_ctx:
        _mlstm_finish(hf_c, hb_c, o_c, out_c, nc)


def mlstm(pc, pl_, need_ctx):
    B, S, W = pl_["mo"].shape
    Tc = pc["mo"].shape[1]
    f32 = jnp.float32

    def specs(T):
        s = pl.BlockSpec((1, T, W), lambda b: (b, 0, 0))
        return [s, pl.BlockSpec((1, T // M_CHUNK, W, M_CHUNK), lambda b: (b, 0, 0, 0)), s,
                pl.BlockSpec((1, T, GATE_W), lambda b: (b, 0, 0)),
                pl.BlockSpec((1, T // M_CHUNK, 4 * M_HEADS, M_CHUNK), lambda b: (b, 0, 0, 0))]

    def args(p):
        return [p["mqk"], p["mvt"], p["mo"], p["g"], p["gt"]]

    out_shape = [jax.ShapeDtypeStruct((B, S, W), jnp.bfloat16)]
    out_specs = [pl.BlockSpec((1, S, W), lambda b: (b, 0, 0))]
    scratch = [pltpu.VMEM((S, W), f32), pltpu.VMEM((S, W), f32)]
    if need_ctx:
        out_shape.append(jax.ShapeDtypeStruct((B, Tc, W), jnp.bfloat16))
        out_specs.append(pl.BlockSpec((1, Tc, W), lambda b: (b, 0, 0)))
        scratch += [pltpu.VMEM((Tc, W), f32), pltpu.VMEM((Tc, W), f32)]
    assert (S // M_CHUNK) % 2 == 0 and (Tc // M_CHUNK) % 2 == 0
    n_pairs = M_HEADS // 2
    scratch += [pltpu.VMEM((M_HEADS, 2 * M_DV, LANES), f32), pltpu.VMEM((M_HEADS, 8, 2 * M_CHUNK), f32),
                pltpu.VMEM((2, 2, n_pairs, 2, M_CHUNK, 2 * M_CHUNK), f32),
                pltpu.VMEM((2, 2, n_pairs, 8, 2 * M_CHUNK), f32)]
    res = pl.pallas_call(
        functools.partial(_mlstm_kernel, need_ctx=need_ctx),
        grid=(B,),
        in_specs=specs(Tc) + specs(S),
        out_specs=out_specs,
        out_shape=out_shape,
        scratch_shapes=scratch,
        compiler_params=pltpu.CompilerParams(dimension_semantics=("arbitrary",), vmem_limit_bytes=VMEM_LIMIT),
        name="mlstm",
    )(*args(pc), *args(pl_))
    return (res[0], res[1]) if need_ctx else (res[0], None)


POOL_HALO = 16


def _pool_mix(pu_ref, prev_ref, next_ref, pw_ref, ps_ref, tile_idx, n_tiles, seq_len):
    f32 = jnp.float32
    tm = pu_ref.shape[1]
    cur = pu_ref[0].astype(f32)
    prev = jnp.where(tile_idx > 0, prev_ref[0].astype(f32), 0.0)
    nxt = jnp.where(tile_idx < n_tiles - 1, next_ref[0].astype(f32), 0.0)
    ext = jnp.concatenate([prev, cur, nxt], axis=0)
    n_ext = tm + 2 * POOL_HALO
    tok = tile_idx * tm + lax.broadcasted_iota(jnp.int32, (tm, POOL_GC), 0)

    def shifted(a, d):
        return pltpu.roll(a, (-d) % n_ext, 0)

    outs = []
    for gi, w in enumerate(POOL_WINDOWS):
        s = ext[:, gi * POOL_GC:(gi + 1) * POOL_GC]
        s = shifted(s, -1) + s
        span = 2
        while span < w:
            s = shifted(s, -(span // 2)) + shifted(s, span // 2)
            span *= 2
        lo = jnp.clip(tok - w // 2, 0, seq_len - 1)
        hi = jnp.clip(tok - w // 2 + w - 1, 0, seq_len - 1)
        mean = s[POOL_HALO:POOL_HALO + tm] / (hi - lo + 1).astype(f32)
        d = (mean - cur[:, gi * POOL_GC:(gi + 1) * POOL_GC]).astype(jnp.bfloat16)
        outs.append(_dot_f32(d, pw_ref[gi]))
    return jnp.concatenate(outs, axis=-1) * ps_ref[...]


def _pack_bf16_pairs(h):
    n = h.shape[1] // 2
    return pltpu.pack_elementwise([h[:, :n], h[:, n:]], packed_dtype=jnp.bfloat16)


def _merge_kernel(m_ref, n_ref, pu_ref, prev_ref, next_ref, bg_ref, x_ref, mods_ref, gains_ref,
                  pw_ref, ps_ref, wb_ref, wo_ref, wrt_ref, wrp_ref, x1_ref, h2p_ref, affr_ref, aff_ref, *, seq_len):
    f32, bf16 = jnp.float32, jnp.bfloat16
    i = pl.program_id(1)
    D = x_ref.shape[2]
    E = wrt_ref.shape[0]
    p = _pool_mix(pu_ref, prev_ref, next_ref, pw_ref, ps_ref, i, pl.num_programs(1), seq_len)
    branches = (m_ref[0], p.astype(bf16), n_ref[0])
    acc = None
    for bi, br in enumerate(branches):
        gate = jnp.tanh(bg_ref[0, :, bi * D:(bi + 1) * D].astype(f32)) + 1.0
        term = gate * _dot_f32(br, wb_ref[bi])
        acc = term if acc is None else acc + term
    y = _dot_f32(acc.astype(bf16), wo_ref[...])
    yn = y * lax.rsqrt(jnp.mean(y * y, axis=-1, keepdims=True) + RMS_EPS) * gains_ref[0:1, :]
    x1 = x_ref[0] + mods_ref[0, 0:1, :] * yn
    x1_ref[0] = x1
    xn = x1 * lax.rsqrt(jnp.mean(x1 * x1, axis=-1, keepdims=True) + RMS_EPS) * gains_ref[1:2, :]
    h2 = xn * (1.0 + mods_ref[0, 2:3, :]) + mods_ref[0, 1:2, :]
    h2b = h2.astype(bf16)
    lg_t = lax.dot_general(wrt_ref[...], h2b, (((1,), (1,)), ((), ())), preferred_element_type=f32)
    e_t = jnp.exp(lg_t - lg_t.max(axis=0, keepdims=True))
    aff_ref[0] = e_t / e_t.sum(axis=0, keepdims=True)
    lg = _dot_f32(h2b, wrp_ref[...])
    lg = jnp.where(lax.broadcasted_iota(jnp.int32, lg.shape, 1) < E, lg, MASK_NEG)
    e_r = jnp.exp(lg - lg.max(axis=-1, keepdims=True))
    affr_ref[0] = e_r / e_r.sum(axis=-1, keepdims=True)
    h2p_ref[0] = _pack_bf16_pairs(h2)


def merge_out(m, n, pu, bg, x, mods, gains, pool_w, pool_scale, w_branch, w_out, w_router_t, w_router_pad, *, tm):
    B, T, D = x.shape
    W = m.shape[2]
    E = w_router_t.shape[0]
    hb = tm // POOL_HALO
    n_halo = T // POOL_HALO
    tile = lambda w: pl.BlockSpec((1, tm, w), lambda b, i: (b, i, 0))
    in_specs = [
        tile(W), tile(W), tile(W),
        pl.BlockSpec((1, POOL_HALO, W), lambda b, i: (b, jnp.maximum(i * hb - 1, 0), 0)),
        pl.BlockSpec((1, POOL_HALO, W), lambda b, i: (b, jnp.minimum((i + 1) * hb, n_halo - 1), 0)),
        tile(N_BRANCH * D), tile(D),
        pl.BlockSpec((1, 8, D), lambda b, i: (b, 0, 0)),
        _const_spec((8, D)),
        _const_spec(pool_w.shape), _const_spec(pool_scale.shape), _const_spec(w_branch.shape),
        _const_spec(w_out.shape), _const_spec(w_router_t.shape), _const_spec(w_router_pad.shape),
    ]
    return pl.pallas_call(
        functools.partial(_merge_kernel, seq_len=T),
        grid=(B, T // tm),
        in_specs=in_specs,
        out_specs=[tile(D), tile(D // 2), tile(LANES), pl.BlockSpec((1, E, tm), lambda b, i: (b, 0, i))],
        out_shape=[jax.ShapeDtypeStruct((B, T, D), jnp.float32),
                   jax.ShapeDtypeStruct((B, T, D // 2), PACK_WORD),
                   jax.ShapeDtypeStruct((B, T, LANES), jnp.float32),
                   jax.ShapeDtypeStruct((B, E, T), jnp.float32)],
        compiler_params=pltpu.CompilerParams(
            dimension_semantics=("arbitrary", "arbitrary"), vmem_limit_bytes=VMEM_LIMIT),
        name="merge_out",
    )(m, n, pu, pu, pu, bg, x, mods, gains, pool_w, pool_scale, w_branch, w_out, w_router_t, w_router_pad)


def _cumsum_lanes(x01):
    R, T = x01.shape
    r = lax.broadcasted_iota(jnp.int32, (LANES, LANES), 0)
    c = lax.broadcasted_iota(jnp.int32, (LANES, LANES), 1)
    tri_u = jnp.where(r <= c, 1.0, 0.0).astype(jnp.bfloat16)
    run = jnp.zeros((R, 1), jnp.float32)
    outs = []
    for t0 in range(0, T, LANES):
        cs = _dot_f32(x01[:, t0:t0 + LANES].astype(jnp.bfloat16), tri_u) + run
        run = cs[:, LANES - 1:LANES]
        outs.append(cs)
    return jnp.concatenate(outs, axis=-1)


def _route_kernel(aff_ref, idx_ref, *, cap):
    f32 = jnp.float32
    aff = aff_ref[0]
    E, T = aff.shape

    def search(k, lo):
        cand = lo | (jnp.int32(1) << (30 - k))
        cnt = jnp.sum(jnp.where(aff >= pltpu.bitcast(cand, f32), 1.0, 0.0), axis=-1, keepdims=True)
        return jnp.where(cnt >= cap, cand, lo)

    thr = pltpu.bitcast(lax.fori_loop(0, 31, search, jnp.zeros((E, 1), jnp.int32)), f32)
    gt = jnp.where(aff > thr, 1.0, 0.0)
    eq = jnp.where(aff == thr, 1.0, 0.0)
    room = cap - jnp.sum(gt, axis=-1, keepdims=True)
    sel = gt + eq * jnp.where(_cumsum_lanes(eq) <= room, 1.0, 0.0)
    csel = _cumsum_lanes(sel)
    slot = lax.broadcasted_iota(jnp.int32, (cap, T), 0).astype(f32)
    lane = lax.broadcasted_iota(jnp.int32, (cap, LANES), 1)
    out = jnp.zeros((cap, LANES), f32)
    for e in range(E):
        pos = jnp.sum(jnp.where(jnp.broadcast_to(csel[e:e + 1, :], (cap, T)) <= slot, 1.0, 0.0),
                      axis=-1, keepdims=True)
        out = jnp.where(lane == e, jnp.broadcast_to(pos, (cap, LANES)), out)
    idx_ref[0] = out.astype(jnp.int32)


def route(aff_t, cap):
    B, E, T = aff_t.shape
    idx_t = pl.pallas_call(
        functools.partial(_route_kernel, cap=cap),
        grid=(B,),
        in_specs=[pl.BlockSpec((1, E, T), lambda b: (b, 0, 0))],
        out_specs=pl.BlockSpec((1, cap, LANES), lambda b: (b, 0, 0)),
        out_shape=jax.ShapeDtypeStruct((B, cap, LANES), jnp.int32),
        compiler_params=pltpu.CompilerParams(dimension_semantics=("arbitrary",), vmem_limit_bytes=VMEM_LIMIT),
        name="route",
    )(aff_t)
    return idx_t[:, :, :E].transpose(0, 2, 1)


ROW_GROUP = 8


def _moe_kernel(idx_ref, rows_ref, affr_ref, wg_ref, wu_ref, wd_ref, out_ref,
                xg0_ref, xg1_ref, ag0_ref, ag1_ref, ye0_ref, ye1_ref):
    f32, bf16 = jnp.float32, jnp.bfloat16
    e = pl.program_id(1)
    n_exp = pl.num_programs(1)
    cap = xg0_ref.shape[0]

    def gather_row(dst, ee, j):
        i = idx_ref[0, 0, ee, j]
        dst[0][pl.ds(j, 1), :] = rows_ref[0, pl.ds(i, 1), :]
        dst[1][pl.ds(j, 1), :] = affr_ref[0, pl.ds(i, 1), :]

    def scatter_group(src_ref, ee, j0):
        ids = [idx_ref[0, 0, ee, j0 + r] for r in range(ROW_GROUP)]
        cur = [out_ref[0, pl.ds(i, 1), :] for i in ids]
        for r, i in enumerate(ids):
            out_ref[0, pl.ds(i, 1), :] = cur[r] + src_ref[pl.ds(j0 + r, 1), :]

    @pl.when(e == 0)
    def _():
        out_ref[...] = jnp.zeros_like(out_ref)
        ye1_ref[...] = jnp.zeros_like(ye1_ref)

        def body(gi, carry):
            for r in range(ROW_GROUP):
                gather_row((xg0_ref, ag0_ref), 0, gi * ROW_GROUP + r)
            return carry
        lax.fori_loop(0, cap // ROW_GROUP, body, 0)

    def step(cur, nxt, ye_cur, ye_prv):
        e_prv = jnp.maximum(e - 1, 0)
        e_nxt = jnp.minimum(e + 1, n_exp - 1)
        for j0 in range(0, cap, ROW_GROUP):
            scatter_group(ye_prv, e_prv, j0)
        for j in range(cap):
            gather_row(nxt, e_nxt, j)
        packed = cur[0][...]
        xe = jnp.concatenate(
            [pltpu.unpack_elementwise(packed, index=i, packed_dtype=bf16, unpacked_dtype=f32) for i in (0, 1)],
            axis=-1).astype(bf16)
        aff = cur[1][...]
        gate = jnp.sum(jnp.where(lax.broadcasted_iota(jnp.int32, aff.shape, 1) == e, aff, 0.0),
                       axis=-1, keepdims=True)
        a = _dot_f32(xe, wg_ref[0])
        hid = (a * _sigmoid(a) * _dot_f32(xe, wu_ref[0])).astype(bf16)
        ye_cur[...] = _dot_f32(hid, wd_ref[0]) * gate

    @pl.when(e % 2 == 0)
    def _():
        step((xg0_ref, ag0_ref), (xg1_ref, ag1_ref), ye0_ref, ye1_ref)

    @pl.when(e % 2 == 1)
    def _():
        step((xg1_ref, ag1_ref), (xg0_ref, ag0_ref), ye1_ref, ye0_ref)

    @pl.when(e == n_exp - 1)
    def _():
        def body(gi, carry):
            scatter_group(ye1_ref, e, gi * ROW_GROUP)
            return carry
        lax.fori_loop(0, cap // ROW_GROUP, body, 0)


def moe_experts(rows, aff_rows, idx, w_gate, w_up, w_down, layer):
    G, T, RW = rows.shape
    AW = aff_rows.shape[2]
    _, E, cap = idx.shape
    D = w_gate.shape[2]
    assert E % 2 == 0 and cap % ROW_GROUP == 0
    wspec = pl.BlockSpec((None, 1, D, D), lambda g, e: (layer, e, 0, 0))
    return pl.pallas_call(
        _moe_kernel,
        grid=(G, E),
        in_specs=[pl.BlockSpec((1, 1, E, cap), lambda g, e: (g, 0, 0, 0), memory_space=pltpu.SMEM),
                  pl.BlockSpec((1, T, RW), lambda g, e: (g, 0, 0)), pl.BlockSpec((1, T, AW), lambda g, e: (g, 0, 0)),
                  wspec, wspec, wspec],
        out_specs=pl.BlockSpec((1, T, D), lambda g, e: (g, 0, 0)),
        out_shape=jax.ShapeDtypeStruct((G, T, D), jnp.float32),
        scratch_shapes=[pltpu.VMEM((cap, RW), rows.dtype), pltpu.VMEM((cap, RW), rows.dtype),
                        pltpu.VMEM((cap, AW), jnp.float32), pltpu.VMEM((cap, AW), jnp.float32),
                        pltpu.VMEM((cap, D), jnp.float32), pltpu.VMEM((cap, D), jnp.float32)],
        compiler_params=pltpu.CompilerParams(
            dimension_semantics=("arbitrary", "arbitrary"), vmem_limit_bytes=VMEM_LIMIT),
        name="moe_experts",
    )(idx.reshape(G, 1, E, cap), rows, aff_rows, w_gate, w_up, w_down)


def _residual_norm_kernel(x_ref, y_ref, gate_ref, gain_ref, o_ref):
    y = y_ref[0]
    yn = y * lax.rsqrt(jnp.mean(y * y, axis=-1, keepdims=True) + RMS_EPS) * gain_ref[...]
    o_ref[0] = x_ref[0] + gate_ref[0] * yn


def residual_norm(x, y, gate, gain, *, tm):
    B, T, D = x.shape
    tile = pl.BlockSpec((1, tm, D), lambda b, i: (b, i, 0))
    return pl.pallas_call(
        _residual_norm_kernel,
        grid=(B, T // tm),
        in_specs=[tile, tile, pl.BlockSpec((1, 1, D), lambda b, i: (b, 0, 0)), _const_spec((1, D))],
        out_specs=tile,
        out_shape=jax.ShapeDtypeStruct((B, T, D), jnp.float32),
        compiler_params=pltpu.CompilerParams(
            dimension_semantics=("arbitrary", "arbitrary"), vmem_limit_bytes=VMEM_LIMIT),
        name="residual_norm",
    )(x, y, gate, gain)


ADA_COLS = 1536


def _ada_kernel(c_ref, w_ref, b_ref, o_ref):
    c = c_ref[...]
    h = (c * _sigmoid(c)).astype(jnp.bfloat16)
    o_ref[...] = _dot_f32(h, w_ref[...].astype(jnp.bfloat16)) + b_ref[...]


def ada_modulation(cond, ada_w, ada_b, layer):
    R, D = cond.shape
    N = ada_w.shape[2]
    return pl.pallas_call(
        _ada_kernel,
        grid=(N // ADA_COLS,),
        in_specs=[pl.BlockSpec((R, D), lambda j: (0, 0)),
                  pl.BlockSpec((None, D, ADA_COLS), lambda j: (layer, 0, j)),
                  pl.BlockSpec((None, 1, ADA_COLS), lambda j: (layer, 0, j))],
        out_specs=pl.BlockSpec((R, ADA_COLS), lambda j: (0, j)),
        out_shape=jax.ShapeDtypeStruct((R, N), jnp.float32),
        compiler_params=pltpu.CompilerParams(dimension_semantics=("arbitrary",), vmem_limit_bytes=VMEM_LIMIT),
        name="ada_modulation",
    )(cond, ada_w, ada_b[:, None, :])


def kernel(x, c, ctx, c_ctx, norm_gain, ada_w, ada_b, w_in, mlstm_gate_bias, pool_w, pool_scale,
           na_rpb, w_branch, w_out, router_w, w_gate, w_up, w_down):
    B, S, D = x.shape
    Tc = ctx.shape[1]
    f32, bf16 = jnp.float32, jnp.bfloat16
    rope_tabs = rope_lane_tables(S)
    names = [n for n, _ in IN_GROUPS] + ["gt"]
    cond = jnp.pad(jnp.concatenate([c, c_ctx[None]], axis=0), ((0, (-(B + 1)) % 8), (0, 0)))
    moe_w = (w_gate.astype(bf16), w_up.astype(bf16), w_down.astype(bf16))
    pending = None
    for l in range(DEPTH):
        need_ctx = l < DEPTH - 1
        g = norm_gain[l]
        mods = ada_modulation(cond, ada_w, ada_b, l)
        mod_l = jnp.split(mods[:B, None, :], N_MOD, axis=-1)
        mod_cb = [jnp.broadcast_to(m[None, None], (B, 1, D)) for m in jnp.split(mods[B], N_MOD, axis=-1)]
        w_perm = permute_w_in(w_in[l])
        gb_row = jnp.pad(mlstm_gate_bias[l].reshape(1, -1).astype(f32), ((0, 0), (0, GATE_W - 4 * M_HEADS)))
        proj = in_proj(x, g[0][None], mod_l[0], mod_l[1], w_perm, gb_row, rope_tabs, pending, tm=512)
        if pending is not None:
            x = proj[-1]
        pl_ = dict(zip(names, proj))
        pc_ = dict(zip(names, in_proj(ctx, g[0][None], mod_cb[0], mod_cb[1], w_perm, gb_row, tm=Tc)))

        m_l, m_c = mlstm(pc_, pl_, need_ctx)
        n_l, n_c = neighbourhood_attention(pl_["qn"], pl_["kn"], pl_["vn"], pc_["kn"], pc_["vn"],
                                           na_bias_table(na_rpb[l]), pc_["qn"] if need_ctx else None)
        gains = jnp.pad(g[1:3], ((0, 6), (0, 0)))
        merge_w = (pool_w[l].astype(bf16), pool_scale[l][None], w_branch[l].astype(bf16),
                   (0.5 * w_out[l]).astype(bf16),
                   router_w[l].T.astype(bf16),
                   jnp.pad(router_w[l], ((0, 0), (0, LANES - N_EXPERTS))).astype(bf16))

        def mods8(mods):
            return jnp.pad(jnp.concatenate([mods[2], mods[3], mods[4]], axis=1), ((0, 0), (0, 5), (0, 0)))

        x1, rows, aff_r, aff_t = merge_out(m_l, n_l, pl_["pu"], pl_["bg"], x, mods8(mod_l), gains, *merge_w, tm=512)
        y = moe_experts(rows, aff_r, route(aff_t, EC_FACTOR * S // N_EXPERTS), *moe_w, l)
        if need_ctx:
            x, pending = x1, (y, mod_l[5], g[3][None])
        else:
            x = residual_norm(x1, y, mod_l[5], g[3][None], tm=512)
        if need_ctx:
            c1, rows, aff_r, aff_t = merge_out(m_c, n_c, pc_["pu"], pc_["bg"], ctx, mods8(mod_cb), gains, *merge_w, tm=Tc)
            per = S // Tc
            cap_c = EC_FACTOR * Tc // N_EXPERTS
            idx = route(aff_t, cap_c) + (jnp.arange(B, dtype=jnp.int32) % per * Tc)[:, None, None]
            idx = idx.reshape(B // per, per, N_EXPERTS, cap_c).transpose(0, 2, 1, 3).reshape(B // per, N_EXPERTS, per * cap_c)
            y = moe_experts(rows.reshape(B // per, S, D // 2), aff_r.reshape(B // per, S, LANES), idx,
                            *moe_w, l).reshape(B, Tc, D)
            ctx = residual_norm(c1, y, mod_cb[5], g[3][None], tm=Tc)
    return x
```

```python
import functools
import math

import jax
import jax.numpy as jnp
import numpy as np
from jax import lax
from jax.experimental import pallas as pl
from jax.experimental.pallas import tpu as pltpu

D_MODEL = 1024
DEPTH = 2
GRID_W = 64
BRANCH_W = D_MODEL // 2
N_BRANCH = 3
M_HEADS = 4
M_DV = BRANCH_W // M_HEADS
M_DQK = M_DV // 2
M_CHUNK = 128
M_INIT = -1e30
POOL_GROUPS = 4
POOL_GC = BRANCH_W // POOL_GROUPS
POOL_WINDOWS = (2, 4, 8, 16)
NA_HEADS = 8
NA_DH = BRANCH_W // NA_HEADS
NA_KH = 8
NA_KW = 16
N_EXPERTS = 16
EC_FACTOR = 2
ROPE_BASE = 10000.0
RMS_EPS = 1e-6
N_MOD = 6
SPLIT_SIZES = (M_HEADS * M_DQK, M_HEADS * M_DQK, M_HEADS * M_DV, M_HEADS * M_DV, 4 * M_HEADS,
               POOL_GROUPS * POOL_GC, NA_HEADS * NA_DH, NA_HEADS * NA_DH, NA_HEADS * NA_DH,
               N_BRANCH * D_MODEL)
PROJ_W = sum(SPLIT_SIZES)

LANES = 128
VMEM_LIMIT = 56 * 1024 * 1024
MASK_NEG = -1e30
GATE_W = LANES

IN_GROUPS = (("mqk", 2 * M_HEADS * M_DQK), ("mvt", BRANCH_W), ("mo", BRANCH_W), ("pu", BRANCH_W),
             ("qn", BRANCH_W), ("kn", BRANCH_W), ("vn", BRANCH_W), ("bg", N_BRANCH * D_MODEL),
             ("g", GATE_W))
IN_W = sum(w for _, w in IN_GROUPS)
MM_COLS = 512
PACK_WORD = jnp.uint32


def _sigmoid(x):
    return 0.5 * jnp.tanh(0.5 * x) + 0.5


def _const_spec(shape):
    nd = len(shape)
    return pl.BlockSpec(shape, lambda *_: (0,) * nd, pipeline_mode=pl.Buffered(1))


def _rope_tile(x, cos, sin_signed):
    half = M_DQK // 4
    lane = lax.broadcasted_iota(jnp.int32, x.shape, 1)
    partner = jnp.where((lane % (2 * half)) < half, pltpu.roll(x, LANES - half, 1), pltpu.roll(x, half, 1))
    return x * cos + partner * sin_signed


def _in_proj_kernel(x_ref, g_ref, shift_ref, scale_ref, w_ref, gb_ref, *rest, rope, pending):
    if rope:
        cos_ref, sin_ref = rest[:2]
        rest = rest[2:]
    if pending:
        r_ref, rgate_ref, rgain_ref = rest[:3]
        rest, xo_ref = rest[3:-1], rest[-1]
        r = r_ref[0]
        x = x_ref[0] + rgate_ref[0] * (r * lax.rsqrt(jnp.mean(r * r, axis=-1, keepdims=True) + RMS_EPS)
                                       * rgain_ref[...])
        xo_ref[0] = x
    else:
        x = x_ref[0]
    o_refs, gt_ref = rest[:-1], rest[-1]
    y = x * lax.rsqrt(jnp.mean(x * x, axis=-1, keepdims=True) + RMS_EPS)
    h = ((y * g_ref[...]) * (1.0 + scale_ref[0]) + shift_ref[0]).astype(jnp.bfloat16)
    off = 0
    for (name, width), o_ref in zip(IN_GROUPS, o_refs):
        for c0 in range(0, width, MM_COLS):
            cw = min(MM_COLS, width - c0)
            acc = jnp.dot(h, w_ref[:, off + c0:off + c0 + cw], preferred_element_type=jnp.float32)
            if name == "g":
                acc = acc + gb_ref[...]
                for c in range(acc.shape[0] // M_CHUNK):
                    gt_ref[0, c] = acc[c * M_CHUNK:(c + 1) * M_CHUNK, :].T[:4 * M_HEADS, :]
            if name == "mqk" and rope:
                cos, sin = cos_ref[...], sin_ref[...]
                acc = jnp.concatenate([_rope_tile(acc[:, t:t + LANES], cos, sin) for t in range(0, cw, LANES)], axis=-1)
            if name == "mvt":
                for c in range(acc.shape[0] // M_CHUNK):
                    o_ref[0, c] = acc[c * M_CHUNK:(c + 1) * M_CHUNK, :].T.astype(o_ref.dtype)
                continue
            o_ref[0, :, c0:c0 + cw] = acc.astype(o_ref.dtype)
        off += width


def in_proj(x, g, shift, scale, w_perm, gate_bias_row, rope_tabs=None, pending=None, *, tm):
    B, T, D = x.shape
    out_shape = [jax.ShapeDtypeStruct((B, T, w), jnp.float32 if n == "g" else jnp.bfloat16) for n, w in IN_GROUPS]
    out_specs = [pl.BlockSpec((1, tm, w), lambda b, i: (b, i, 0)) for _, w in IN_GROUPS]
    k_mvt = [n for n, _ in IN_GROUPS].index("mvt")
    assert BRANCH_W == MM_COLS and tm % M_CHUNK == 0
    out_shape[k_mvt] = jax.ShapeDtypeStruct((B, T // M_CHUNK, BRANCH_W, M_CHUNK), jnp.bfloat16)
    out_specs[k_mvt] = pl.BlockSpec((1, tm // M_CHUNK, BRANCH_W, M_CHUNK), lambda b, i: (b, i, 0, 0))
    out_shape.append(jax.ShapeDtypeStruct((B, T // M_CHUNK, 4 * M_HEADS, M_CHUNK), jnp.float32))
    out_specs.append(pl.BlockSpec((1, tm // M_CHUNK, 4 * M_HEADS, M_CHUNK), lambda b, i: (b, i, 0, 0)))
    in_specs = [
        pl.BlockSpec((1, tm, D), lambda b, i: (b, i, 0)),
        _const_spec((1, D)),
        pl.BlockSpec((1, 1, D), lambda b, i: (b, 0, 0)),
        pl.BlockSpec((1, 1, D), lambda b, i: (b, 0, 0)),
        _const_spec((D, IN_W)),
        _const_spec((1, GATE_W)),
    ]
    args = [x, g, shift, scale, w_perm, gate_bias_row]
    if rope_tabs is not None:
        in_specs += [pl.BlockSpec((tm, LANES), lambda b, i: (i, 0))] * 2
        args += list(rope_tabs)
    if pending is not None:
        in_specs += [pl.BlockSpec((1, tm, D), lambda b, i: (b, i, 0)), pl.BlockSpec((1, 1, D), lambda b, i: (b, 0, 0)),
                     _const_spec((1, D))]
        args += list(pending)
        out_shape.append(jax.ShapeDtypeStruct((B, T, D), jnp.float32))
        out_specs.append(pl.BlockSpec((1, tm, D), lambda b, i: (b, i, 0)))
    return pl.pallas_call(
        functools.partial(_in_proj_kernel, rope=rope_tabs is not None, pending=pending is not None),
        grid=(B, T // tm),
        in_specs=in_specs,
        out_specs=out_specs,
        out_shape=out_shape,
        compiler_params=pltpu.CompilerParams(
            dimension_semantics=("arbitrary", "arbitrary"), vmem_limit_bytes=VMEM_LIMIT),
        name="in_proj",
    )(*args)


def rope_lane_tables(n_tokens):
    t = jnp.arange(n_tokens)
    row = (t // GRID_W).astype(jnp.float32)
    col = (t % GRID_W).astype(jnp.float32)
    half = M_DQK // 2
    inv = ROPE_BASE ** (-jnp.arange(0, half, 2, dtype=jnp.float32) / half)
    ar = row[:, None] * inv[None, :]
    ac = col[:, None] * inv[None, :]
    cos = jnp.concatenate([jnp.cos(ar), jnp.cos(ar), jnp.cos(ac), jnp.cos(ac)], axis=-1)
    sin = jnp.concatenate([-jnp.sin(ar), jnp.sin(ar), -jnp.sin(ac), jnp.sin(ac)], axis=-1)
    return jnp.tile(cos, (1, LANES // M_DQK)), jnp.tile(sin, (1, LANES // M_DQK))


def permute_w_in(w_in_l):
    qm, km, vm, om, gm, pu, qn, kn, vn, bg = jnp.split(w_in_l, [int(s) for s in np.cumsum(SPLIT_SIZES)[:-1]], axis=-1)
    gpad = jnp.pad(gm, ((0, 0), (0, GATE_W - gm.shape[1])))
    w = jnp.concatenate([qm * (M_DQK ** -0.5), km, vm, om, pu, qn * (NA_DH ** -0.5), kn, vn, bg * 0.5, gpad], axis=-1)
    return w.astype(jnp.bfloat16)


def _pair_scores(qp, k_parts, biases):
    n = qp.shape[0]
    lane_lo = lax.broadcasted_iota(jnp.int32, (n, LANES), 1) < NA_DH
    zero = jnp.zeros_like(qp)
    qq = jnp.concatenate([jnp.where(lane_lo, qp, zero), jnp.where(lane_lo, zero, qp)], axis=0)
    scores = []
    for kp, bias in zip(k_parts, biases):
        s = lax.dot_general(qq, kp, (((1,), (1,)), ((), ())), preferred_element_type=jnp.float32)
        scores.append(s if bias is None else s + bias)
    return scores


def _pair_softmax_pv(scores, v_parts):
    n = scores[0].shape[0] // 2
    lane_lo = lax.broadcasted_iota(jnp.int32, (n, LANES), 1) < NA_DH
    m = scores[0].max(axis=-1, keepdims=True)
    for s in scores[1:]:
        m = jnp.maximum(m, s.max(axis=-1, keepdims=True))
    l = None
    o = None
    for s, vp in zip(scores, v_parts):
        p = jnp.exp(s - m)
        ls = p.sum(axis=-1, keepdims=True)
        os_ = jnp.dot(p.astype(jnp.bfloat16), vp, preferred_element_type=jnp.float32)
        l = ls if l is None else l + ls
        o = os_ if o is None else o + os_
    o = o * (1.0 / l)
    return jnp.where(lane_lo, o[:n], o[n:])


def _na_kernel(q_ref, k_ref, v_ref, kc_ref, vc_ref, bias_ref, *rest, need_ctx):
    if need_ctx:
        qc_ref, o_ref, oc_ref, s_ref = rest
    else:
        o_ref, s_ref = rest
    S = q_ref.shape[1]
    Tc = kc_ref.shape[1]
    rows = S // GRID_W
    n_pairs = BRANCH_W // LANES
    n_loc = NA_KH * GRID_W

    def window(r):
        rs = jnp.clip(r - NA_KH // 2, 0, rows - NA_KH)
        return r - rs, pl.multiple_of(r * GRID_W, GRID_W), pl.multiple_of(rs * GRID_W, GRID_W)

    def scores_stage(r, slot):
        var, q0, k0 = window(r)
        for j in range(n_pairs):
            ls = slice(j * LANES, (j + 1) * LANES)
            s_loc, s_ctx = _pair_scores(q_ref[0, pl.ds(q0, GRID_W), ls],
                                        (k_ref[0, pl.ds(k0, n_loc), ls], kc_ref[0, :, ls]),
                                        (bias_ref[var, j], None))
            s_ref[slot, j, :, :n_loc] = s_loc
            s_ref[slot, j, :, n_loc:] = s_ctx

    def output_stage(r, slot):
        _, q0, k0 = window(r)
        outs = []
        for j in range(n_pairs):
            ls = slice(j * LANES, (j + 1) * LANES)
            outs.append(_pair_softmax_pv((s_ref[slot, j, :, :n_loc], s_ref[slot, j, :, n_loc:]),
                                         (v_ref[0, pl.ds(k0, n_loc), ls], vc_ref[0, :, ls])))
        o_ref[0, pl.ds(q0, GRID_W), :] = jnp.concatenate(outs, axis=-1).astype(o_ref.dtype)

    scores_stage(0, 0)

    def two_rows(i, carry):
        r0 = 2 * i
        scores_stage(r0 + 1, 1)
        output_stage(r0, 0)
        scores_stage(jnp.minimum(r0 + 2, rows - 1), 0)
        output_stage(r0 + 1, 1)
        return carry

    lax.fori_loop(0, rows // 2, two_rows, 0)

    if need_ctx:
        outs = []
        for j in range(n_pairs):
            ls = slice(j * LANES, (j + 1) * LANES)
            scores = _pair_scores(qc_ref[0, :, ls], (kc_ref[0, :, ls],), (None,))
            outs.append(_pair_softmax_pv(scores, (vc_ref[0, :, ls],)))
        oc_ref[0] = jnp.concatenate(outs, axis=-1).astype(oc_ref.dtype)


def na_bias_table(rpb):
    H = rpb.shape[0]
    var = jnp.arange(NA_KH)
    kr = jnp.arange(NA_KH)
    dr = kr[None, :] - var[:, None] + NA_KH - 1
    cols = jnp.arange(GRID_W)
    dc = jnp.clip(cols[None, :] - cols[:, None] + NA_KW - 1, 0, 2 * NA_KW - 2)
    cs = jnp.clip(cols - NA_KW // 2, 0, GRID_W - NA_KW)
    colmask = (cols[None, :] >= cs[:, None]) & (cols[None, :] < cs[:, None] + NA_KW)
    pick_r = jax.nn.one_hot(dr, 2 * NA_KH - 1, dtype=jnp.float32)
    pick_c = jax.nn.one_hot(dc, 2 * NA_KW - 1, dtype=jnp.float32)
    tab = jnp.einsum('vka,hab,qcb->vhqkc', pick_r, rpb.astype(jnp.float32), pick_c, precision=lax.Precision.HIGHEST)
    tab = jnp.where(colmask[None, None, :, None, :], tab, MASK_NEG)
    return tab.reshape(NA_KH, H // 2, 2 * GRID_W, NA_KH * GRID_W)


def neighbourhood_attention(qn, kn, vn, kc, vc, bias_tab, qc=None):
    B, S, W = qn.shape
    Tc = kc.shape[1]
    need_ctx = qc is not None
    lat = pl.BlockSpec((1, S, W), lambda b: (b, 0, 0))
    cx = pl.BlockSpec((1, Tc, W), lambda b: (b, 0, 0))
    in_specs = [lat, lat, lat, cx, cx, _const_spec(bias_tab.shape)]
    args = [qn, kn, vn, kc, vc, bias_tab]
    out_shape = [jax.ShapeDtypeStruct((B, S, W), jnp.bfloat16)]
    out_specs = [lat]
    if need_ctx:
        in_specs.append(cx)
        args.append(qc)
        out_shape.append(jax.ShapeDtypeStruct((B, Tc, W), jnp.bfloat16))
        out_specs.append(cx)
    res = pl.pallas_call(
        functools.partial(_na_kernel, need_ctx=need_ctx),
        grid=(B,),
        in_specs=in_specs,
        out_specs=out_specs,
        out_shape=out_shape,
        scratch_shapes=[pltpu.VMEM((2, W // LANES, 2 * GRID_W, NA_KH * GRID_W + Tc), jnp.float32)],
        compiler_params=pltpu.CompilerParams(dimension_semantics=("arbitrary",), vmem_limit_bytes=VMEM_LIMIT),
        name="neighbourhood_attention",
    )(*args)
    return (res[0], res[1]) if need_ctx else (res[0], None)


def _split_bf16(x):
    hi = x.astype(jnp.bfloat16)
    r1 = x - hi.astype(jnp.float32)
    mid = r1.astype(jnp.bfloat16)
    lo = (r1 - mid.astype(jnp.float32)).astype(jnp.bfloat16)
    return hi, mid, lo


def _dot_f32(a, b):
    return jnp.dot(a, b, preferred_element_type=jnp.float32)


def _log_sigmoid(x):
    return jnp.minimum(x, 0.0) - jnp.log(1.0 + jnp.exp(-jnp.abs(x)))


def _pair_queries(qp):
    lane_lo = lax.broadcasted_iota(jnp.int32, qp.shape, 1) < M_DQK
    zq = jnp.zeros_like(qp)
    return jnp.concatenate([jnp.where(lane_lo, qp, zq), jnp.where(lane_lo, zq, qp)], axis=0)


def _mlstm_prep(qk_ref, g_ref, gt_ref, sa_ref, ra_ref, slot, ci, fwd):
    L = M_CHUNK
    f32, bf16 = jnp.float32, jnp.bfloat16
    d = 0 if fwd else 1
    rows = pl.ds(pl.multiple_of(ci * L, L), L)
    sq_r = lax.broadcasted_iota(jnp.int32, (L, L), 0)
    sq_c = lax.broadcasted_iota(jnp.int32, (L, L), 1)
    tri_l = jnp.where(sq_r >= sq_c, 1.0, 0.0).astype(bf16)
    tri_u = jnp.where(sq_r <= sq_c, 1.0, 0.0).astype(bf16)
    gates = g_ref[0, rows, :]
    gates_t = gt_ref[0, ci]
    lf, lf_t = _log_sigmoid(gates), _log_sigmoid(gates_t)
    cum = sum(_dot_f32(tri_l if fwd else tri_u, p) for p in _split_bf16(lf))
    cum_t = sum(_dot_f32(p, tri_u if fwd else tri_l) for p in _split_bf16(lf_t))
    t_i = 0 if fwd else 2
    st_s = lax.broadcasted_iota(jnp.int32, (L, 2 * L), 0)
    st_t = lax.broadcasted_iota(jnp.int32, (L, 2 * L), 1) & (L - 1)
    valid = (st_s <= st_t) if fwd else (st_s >= st_t)

    def row2(tile, c0, c1):
        return jnp.concatenate([tile[c0:c0 + 1, :], tile[c1:c1 + 1, :]], axis=-1)

    for j in range(M_HEADS // 2):
        ci0, ci1 = t_i * M_HEADS + 2 * j, t_i * M_HEADS + 2 * j + 1
        cf0, cf1 = ci0 + M_HEADS, ci1 + M_HEADS
        b_row, i_row = row2(cum_t, cf0, cf1), row2(gates_t, ci0, ci1)
        colb = jnp.concatenate([jnp.broadcast_to(gates[:, ci0:ci0 + 1] - cum[:, cf0:cf0 + 1], (L, L)),
                                jnp.broadcast_to(gates[:, ci1:ci1 + 1] - cum[:, cf1:cf1 + 1], (L, L))],
                               axis=-1)
        qp = qk_ref[0, rows, j * LANES:(j + 1) * LANES]
        kp = qk_ref[0, rows, M_HEADS * M_DQK + j * LANES:M_HEADS * M_DQK + (j + 1) * LANES]
        sa_ref[slot, d, j, 0] = jnp.where(valid, b_row + colb, MASK_NEG)
        sa_ref[slot, d, j, 1] = lax.dot_general(kp, _pair_queries(qp), (((1,), (1,)), ((), ())),
                                                preferred_element_type=f32)
        ra_ref[slot, d, j, 0:1, :] = b_row
        ra_ref[slot, d, j, 1:2, :] = i_row


def _mlstm_step(qk_ref, vt_ref, sa_ref, ra_ref, slot, h_ref, ct_ref, m_ref, ci, fwd):
    L = M_CHUNK
    f32, bf16 = jnp.float32, jnp.bfloat16
    d = 0 if fwd else 1
    rows = pl.ds(pl.multiple_of(ci * L, L), L)
    last = L - 1 if fwd else 0
    lane_lo = lax.broadcasted_iota(jnp.int32, (L, LANES), 1) < M_DQK
    lane_lo_row = lax.broadcasted_iota(jnp.int32, (1, LANES), 1) < M_DQK
    ones = jnp.ones((M_DV, L), bf16)

    def halves(row, f):
        return jnp.concatenate([jnp.broadcast_to(f(row[:, :L]), (1, L)), jnp.broadcast_to(f(row[:, L:]), (1, L))],
                               axis=-1)

    for j in range(M_HEADS // 2):
        h0, h1 = 2 * j, 2 * j + 1
        sidx = 2 * j + d
        b_row, i_row = ra_ref[slot, d, j, 0:1, :], ra_ref[slot, d, j, 1:2, :]
        m_row = m_ref[sidx, 0:1, :]
        kp = qk_ref[0, rows, M_HEADS * M_DQK + j * LANES:M_HEADS * M_DQK + (j + 1) * LANES]
        vx0 = jnp.concatenate([vt_ref[0, ci, h0 * M_DV:(h0 + 1) * M_DV, :], ones], axis=0)
        vx1 = jnp.concatenate([vt_ref[0, ci, h1 * M_DV:(h1 + 1) * M_DV, :], ones], axis=0)
        ct = ct_ref[sidx]
        if h_ref is not None:
            dm = sa_ref[slot, d, j, 0]
            qq = _pair_queries(qk_ref[0, rows, j * LANES:(j + 1) * LANES])
            g_row = b_row + m_row
            mt = jnp.maximum(g_row, dm.max(axis=0, keepdims=True))
            sc = (sa_ref[slot, d, j, 1] * jnp.exp(dm - mt)).astype(bf16)
            inter = jnp.exp(g_row - mt)
            t1 = lax.dot_general(ct.astype(bf16), qq, (((1,), (1,)), ((), ())), preferred_element_type=f32)
            t2 = jnp.concatenate([_dot_f32(vx0, sc[:, :L]), _dot_f32(vx1, sc[:, L:])], axis=-1)
            tot = inter * t1 + t2
            h_t = tot[:M_DV] / jnp.maximum(jnp.abs(tot[M_DV:]), jnp.exp(-mt))
            h_ref[rows, h0 * M_DV:(h0 + 1) * M_DV] = h_t[:, :L].T
            h_ref[rows, h1 * M_DV:(h1 + 1) * M_DV] = h_t[:, L:].T
        bl = halves(b_row, lambda r: r[:, last:last + 1])
        w = bl - b_row + i_row
        m_new = jnp.maximum(bl + m_row, halves(w, lambda r: r.max(axis=-1, keepdims=True)))
        decay = jnp.exp(bl + m_row - m_new)
        ws = jnp.exp(w - m_new)
        vxs = jnp.concatenate([vx0.astype(f32) * ws[:, :L], vx1.astype(f32) * ws[:, L:]], axis=-1).astype(bf16)
        zk = jnp.zeros_like(kp)
        kk = jnp.concatenate([jnp.where(lane_lo, kp, zk), jnp.where(lane_lo, zk, kp)], axis=0)
        d_cols = jnp.where(lane_lo_row, jnp.broadcast_to(decay[:, 0:1], (1, LANES)),
                           jnp.broadcast_to(decay[:, L:L + 1], (1, LANES)))
        ct_ref[sidx] = d_cols * ct + _dot_f32(vxs, kk)
        m_ref[sidx, 0:1, :] = m_new


def _mlstm_finish(hf_ref, hb_ref, o_ref, out_ref, n_chunks):
    L = M_CHUNK

    def body(ci, carry):
        rows = pl.ds(pl.multiple_of(ci * L, L), L)
        h = hf_ref[rows, :] + hb_ref[rows, :]
        parts = []
        for hd in range(M_HEADS):
            hh = h[:, hd * M_DV:(hd + 1) * M_DV]
            parts.append(hh * lax.rsqrt(jnp.mean(hh * hh, axis=-1, keepdims=True) + RMS_EPS))
        gate = _sigmoid(o_ref[0, rows, :].astype(jnp.float32))
        out_ref[0, rows, :] = (gate * jnp.concatenate(parts, axis=-1)).astype(out_ref.dtype)
        return carry

    lax.fori_loop(0, n_chunks, body, 0)


def _mlstm_kernel(qk_c, v_c, o_c, g_c, gt_c, qk_l, v_l, o_l, g_l, gt_l, *rest, need_ctx):
    if need_ctx:
        out_l, out_c, hf_l, hb_l, hf_c, hb_c, ct_ref, m_ref, sa_ref, ra_ref = rest
    else:
        out_l, hf_l, hb_l, ct_ref, m_ref, sa_ref, ra_ref = rest
        hf_c = hb_c = None
    nc, nl = qk_c.shape[1] // M_CHUNK, qk_l.shape[1] // M_CHUNK
    ct_ref[...] = jnp.zeros_like(ct_ref)
    m_ref[...] = jnp.full_like(m_ref, M_INIT)

    def phase(qk, vt, g, gt, hf, hb, n):
        def prep(s, slot):
            _mlstm_prep(qk, g, gt, sa_ref, ra_ref, slot, s, True)
            _mlstm_prep(qk, g, gt, sa_ref, ra_ref, slot, n - 1 - s, False)

        def step(s, slot):
            _mlstm_step(qk, vt, sa_ref, ra_ref, slot, hf, ct_ref, m_ref, s, True)
            _mlstm_step(qk, vt, sa_ref, ra_ref, slot, hb, ct_ref, m_ref, n - 1 - s, False)

        prep(0, 0)

        def body(i, carry):
            s0 = 2 * i
            prep(s0 + 1, 1)
            step(s0, 0)
            prep(jnp.minimum(s0 + 2, n - 1), 0)
            step(s0 + 1, 1)
            return carry
        lax.fori_loop(0, n // 2, body, 0)

    phase(qk_c, v_c, g_c, gt_c, hf_c, hb_c, nc)
    phase(qk_l, v_l, g_l, gt_l, hf_l, hb_l, nl)
    _mlstm_finish(hf_l, hb_l, o_l, out_l, nl)
    if need_ctx:
        _mlstm_finish(hf_c, hb_c, o_c, out_c, nc)


def mlstm(pc, pl_, need_ctx):
    B, S, W = pl_["mo"].shape
    Tc = pc["mo"].shape[1]
    f32 = jnp.float32

    def specs(T):
        s = pl.BlockSpec((1, T, W), lambda b: (b, 0, 0))
        return [s, pl.BlockSpec((1, T // M_CHUNK, W, M_CHUNK), lambda b: (b, 0, 0, 0)), s,
                pl.BlockSpec((1, T, GATE_W), lambda b: (b, 0, 0)),
                pl.BlockSpec((1, T // M_CHUNK, 4 * M_HEADS, M_CHUNK), lambda b: (b, 0, 0, 0))]

    def args(p):
        return [p["mqk"], p["mvt"], p["mo"], p["g"], p["gt"]]

    out_shape = [jax.ShapeDtypeStruct((B, S, W), jnp.bfloat16)]
    out_specs = [pl.BlockSpec((1, S, W), lambda b: (b, 0, 0))]
    scratch = [pltpu.VMEM((S, W), f32), pltpu.VMEM((S, W), f32)]
    if need_ctx:
        out_shape.append(jax.ShapeDtypeStruct((B, Tc, W), jnp.bfloat16))
        out_specs.append(pl.BlockSpec((1, Tc, W), lambda b: (b, 0, 0)))
        scratch += [pltpu.VMEM((Tc, W), f32), pltpu.VMEM((Tc, W), f32)]
    assert (S // M_CHUNK) % 2 == 0 and (Tc // M_CHUNK) % 2 == 0
    n_pairs = M_HEADS // 2
    scratch += [pltpu.VMEM((M_HEADS, 2 * M_DV, LANES), f32), pltpu.VMEM((M_HEADS, 8, 2 * M_CHUNK), f32),
                pltpu.VMEM((2, 2, n_pairs, 2, M_CHUNK, 2 * M_CHUNK), f32),
                pltpu.VMEM((2, 2, n_pairs, 8, 2 * M_CHUNK), f32)]
    res = pl.pallas_call(
        functools.partial(_mlstm_kernel, need_ctx=need_ctx),
        grid=(B,),
        in_specs=specs(Tc) + specs(S),
        out_specs=out_specs,
        out_shape=out_shape,
        scratch_shapes=scratch,
        compiler_params=pltpu.CompilerParams(dimension_semantics=("arbitrary",), vmem_limit_bytes=VMEM_LIMIT),
        name="mlstm",
    )(*args(pc), *args(pl_))
    return (res[0], res[1]) if need_ctx else (res[0], None)


POOL_HALO = 16


def _pool_mix(pu_ref, prev_ref, next_ref, pw_ref, ps_ref, tile_idx, n_tiles, seq_len):
    f32 = jnp.float32
    tm = pu_ref.shape[1]
    cur = pu_ref[0].astype(f32)
    prev = jnp.where(tile_idx > 0, prev_ref[0].astype(f32), 0.0)
    nxt = jnp.where(tile_idx < n_tiles - 1, next_ref[0].astype(f32), 0.0)
    ext = jnp.concatenate([prev, cur, nxt], axis=0)
    n_ext = tm + 2 * POOL_HALO
    tok = tile_idx * tm + lax.broadcasted_iota(jnp.int32, (tm, POOL_GC), 0)

    def shifted(a, d):
        return pltpu.roll(a, (-d) % n_ext, 0)

    outs = []
    for gi, w in enumerate(POOL_WINDOWS):
        s = ext[:, gi * POOL_GC:(gi + 1) * POOL_GC]
        s = shifted(s, -1) + s
        span = 2
        while span < w:
            s = shifted(s, -(span // 2)) + shifted(s, span // 2)
            span *= 2
        lo = jnp.clip(tok - w // 2, 0, seq_len - 1)
        hi = jnp.clip(tok - w // 2 + w - 1, 0, seq_len - 1)
        mean = s[POOL_HALO:POOL_HALO + tm] / (hi - lo + 1).astype(f32)
        d = (mean - cur[:, gi * POOL_GC:(gi + 1) * POOL_GC]).astype(jnp.bfloat16)
        outs.append(_dot_f32(d, pw_ref[gi]))
    return jnp.concatenate(outs, axis=-1) * ps_ref[...]


def _pack_bf16_pairs(h):
    n = h.shape[1] // 2
    return pltpu.pack_elementwise([h[:, :n], h[:, n:]], packed_dtype=jnp.bfloat16)


def _merge_kernel(m_ref, n_ref, pu_ref, prev_ref, next_ref, bg_ref, x_ref, mods_ref, gains_ref,
                  pw_ref, ps_ref, wb_ref, wo_ref, wrt_ref, wrp_ref, x1_ref, h2p_ref, affr_ref, aff_ref, *, seq_len):
    f32, bf16 = jnp.float32, jnp.bfloat16
    i = pl.program_id(1)
    D = x_ref.shape[2]
    E = wrt_ref.shape[0]
    p = _pool_mix(pu_ref, prev_ref, next_ref, pw_ref, ps_ref, i, pl.num_programs(1), seq_len)
    branches = (m_ref[0], p.astype(bf16), n_ref[0])
    acc = None
    for bi, br in enumerate(branches):
        gate = jnp.tanh(bg_ref[0, :, bi * D:(bi + 1) * D].astype(f32)) + 1.0
        term = gate * _dot_f32(br, wb_ref[bi])
        acc = term if acc is None else acc + term
    y = _dot_f32(acc.astype(bf16), wo_ref[...])
    yn = y * lax.rsqrt(jnp.mean(y * y, axis=-1, keepdims=True) + RMS_EPS) * gains_ref[0:1, :]
    x1 = x_ref[0] + mods_ref[0, 0:1, :] * yn
    x1_ref[0] = x1
    xn = x1 * lax.rsqrt(jnp.mean(x1 * x1, axis=-1, keepdims=True) + RMS_EPS) * gains_ref[1:2, :]
    h2 = xn * (1.0 + mods_ref[0, 2:3, :]) + mods_ref[0, 1:2, :]
    h2b = h2.astype(bf16)
    lg_t = lax.dot_general(wrt_ref[...], h2b, (((1,), (1,)), ((), ())), preferred_element_type=f32)
    e_t = jnp.exp(lg_t - lg_t.max(axis=0, keepdims=True))
    aff_ref[0] = e_t / e_t.sum(axis=0, keepdims=True)
    lg = _dot_f32(h2b, wrp_ref[...])
    lg = jnp.where(lax.broadcasted_iota(jnp.int32, lg.shape, 1) < E, lg, MASK_NEG)
    e_r = jnp.exp(lg - lg.max(axis=-1, keepdims=True))
    affr_ref[0] = e_r / e_r.sum(axis=-1, keepdims=True)
    h2p_ref[0] = _pack_bf16_pairs(h2)


def merge_out(m, n, pu, bg, x, mods, gains, pool_w, pool_scale, w_branch, w_out, w_router_t, w_router_pad, *, tm):
    B, T, D = x.shape
    W = m.shape[2]
    E = w_router_t.shape[0]
    hb = tm // POOL_HALO
    n_halo = T // POOL_HALO
    tile = lambda w: pl.BlockSpec((1, tm, w), lambda b, i: (b, i, 0))
    in_specs = [
        tile(W), tile(W), tile(W),
        pl.BlockSpec((1, POOL_HALO, W), lambda b, i: (b, jnp.maximum(i * hb - 1, 0), 0)),
        pl.BlockSpec((1, POOL_HALO, W), lambda b, i: (b, jnp.minimum((i + 1) * hb, n_halo - 1), 0)),
        tile(N_BRANCH * D), tile(D),
        pl.BlockSpec((1, 8, D), lambda b, i: (b, 0, 0)),
        _const_spec((8, D)),
        _const_spec(pool_w.shape), _const_spec(pool_scale.shape), _const_spec(w_branch.shape),
        _const_spec(w_out.shape), _const_spec(w_router_t.shape), _const_spec(w_router_pad.shape),
    ]
    return pl.pallas_call(
        functools.partial(_merge_kernel, seq_len=T),
        grid=(B, T // tm),
        in_specs=in_specs,
        out_specs=[tile(D), tile(D // 2), tile(LANES), pl.BlockSpec((1, E, tm), lambda b, i: (b, 0, i))],
        out_shape=[jax.ShapeDtypeStruct((B, T, D), jnp.float32),
                   jax.ShapeDtypeStruct((B, T, D // 2), PACK_WORD),
                   jax.ShapeDtypeStruct((B, T, LANES), jnp.float32),
                   jax.ShapeDtypeStruct((B, E, T), jnp.float32)],
        compiler_params=pltpu.CompilerParams(
            dimension_semantics=("arbitrary", "arbitrary"), vmem_limit_bytes=VMEM_LIMIT),
        name="merge_out",
    )(m, n, pu, pu, pu, bg, x, mods, gains, pool_w, pool_scale, w_branch, w_out, w_router_t, w_router_pad)


def _cumsum_lanes(x01):
    R, T = x01.shape
    r = lax.broadcasted_iota(jnp.int32, (LANES, LANES), 0)
    c = lax.broadcasted_iota(jnp.int32, (LANES, LANES), 1)
    tri_u = jnp.where(r <= c, 1.0, 0.0).astype(jnp.bfloat16)
    run = jnp.zeros((R, 1), jnp.float32)
    outs = []
    for t0 in range(0, T, LANES):
        cs = _dot_f32(x01[:, t0:t0 + LANES].astype(jnp.bfloat16), tri_u) + run
        run = cs[:, LANES - 1:LANES]
        outs.append(cs)
    return jnp.concatenate(outs, axis=-1)


def _route_kernel(aff_ref, idx_ref, *, cap):
    f32 = jnp.float32
    aff = aff_ref[0]
    E, T = aff.shape

    def search(k, lo):
        cand = lo | (jnp.int32(1) << (30 - k))
        cnt = jnp.sum(jnp.where(aff >= pltpu.bitcast(cand, f32), 1.0, 0.0), axis=-1, keepdims=True)
        return jnp.where(cnt >= cap, cand, lo)

    thr = pltpu.bitcast(lax.fori_loop(0, 31, search, jnp.zeros((E, 1), jnp.int32)), f32)
    gt = jnp.where(aff > thr, 1.0, 0.0)
    eq = jnp.where(aff == thr, 1.0, 0.0)
    room = cap - jnp.sum(gt, axis=-1, keepdims=True)
    sel = gt + eq * jnp.where(_cumsum_lanes(eq) <= room, 1.0, 0.0)
    csel = _cumsum_lanes(sel)
    slot = lax.broadcasted_iota(jnp.int32, (cap, T), 0).astype(f32)
    lane = lax.broadcasted_iota(jnp.int32, (cap, LANES), 1)
    out = jnp.zeros((cap, LANES), f32)
    for e in range(E):
        pos = jnp.sum(jnp.where(jnp.broadcast_to(csel[e:e + 1, :], (cap, T)) <= slot, 1.0, 0.0),
                      axis=-1, keepdims=True)
        out = jnp.where(lane == e, jnp.broadcast_to(pos, (cap, LANES)), out)
    idx_ref[0] = out.astype(jnp.int32)


def route(aff_t, cap):
    B, E, T = aff_t.shape
    idx_t = pl.pallas_call(
        functools.partial(_route_kernel, cap=cap),
        grid=(B,),
        in_specs=[pl.BlockSpec((1, E, T), lambda b: (b, 0, 0))],
        out_specs=pl.BlockSpec((1, cap, LANES), lambda b: (b, 0, 0)),
        out_shape=jax.ShapeDtypeStruct((B, cap, LANES), jnp.int32),
        compiler_params=pltpu.CompilerParams(dimension_semantics=("arbitrary",), vmem_limit_bytes=VMEM_LIMIT),
        name="route",
    )(aff_t)
    return idx_t[:, :, :E].transpose(0, 2, 1)


ROW_GROUP = 8


def _moe_kernel(idx_ref, rows_ref, affr_ref, wg_ref, wu_ref, wd_ref, out_ref, xg_ref, ag_ref, ye_ref):
    f32, bf16 = jnp.float32, jnp.bfloat16
    e = pl.program_id(1)
    n_exp = pl.num_programs(1)
    n_sets, cap = xg_ref.shape[1], xg_ref.shape[2]

    def gather_row(p, s, ee, j):
        i = idx_ref[0, s, ee, j]
        xg_ref[p, s, pl.ds(j, 1), :] = rows_ref[s, pl.ds(i, 1), :]
        ag_ref[p, s, pl.ds(j, 1), :] = affr_ref[s, pl.ds(i, 1), :]

    def scatter_group(p, s, ee, j0):
        ids = [idx_ref[0, s, ee, j0 + r] for r in range(ROW_GROUP)]
        cur = [out_ref[s, pl.ds(i, 1), :] for i in ids]
        for r, i in enumerate(ids):
            out_ref[s, pl.ds(i, 1), :] = cur[r] + ye_ref[p, s, pl.ds(j0 + r, 1), :]

    @pl.when(e == 0)
    def _():
        out_ref[...] = jnp.zeros_like(out_ref)
        ye_ref[1] = jnp.zeros_like(ye_ref[1])

        def body(gi, carry):
            for s in range(n_sets):
                for r in range(ROW_GROUP):
                    gather_row(0, s, 0, gi * ROW_GROUP + r)
            return carry
        lax.fori_loop(0, cap // ROW_GROUP, body, 0)

    def step(p):
        e_prv = jnp.maximum(e - 1, 0)
        e_nxt = jnp.minimum(e + 1, n_exp - 1)
        for s in range(n_sets):
            for j0 in range(0, cap, ROW_GROUP):
                scatter_group(1 - p, s, e_prv, j0)
            for j in range(cap):
                gather_row(1 - p, s, e_nxt, j)
        packed = jnp.concatenate([xg_ref[p, s] for s in range(n_sets)], axis=0)
        xe = jnp.concatenate(
            [pltpu.unpack_elementwise(packed, index=i, packed_dtype=bf16, unpacked_dtype=f32) for i in (0, 1)],
            axis=-1).astype(bf16)
        aff = jnp.concatenate([ag_ref[p, s] for s in range(n_sets)], axis=0)
        gate = jnp.sum(jnp.where(lax.broadcasted_iota(jnp.int32, aff.shape, 1) == e, aff, 0.0),
                       axis=-1, keepdims=True)
        a = _dot_f32(xe, wg_ref[0])
        hid = (a * _sigmoid(a) * _dot_f32(xe, wu_ref[0])).astype(bf16)
        ye = _dot_f32(hid, wd_ref[0]) * gate
        for s in range(n_sets):
            ye_ref[p, s] = ye[s * cap:(s + 1) * cap]

    @pl.when(e % 2 == 0)
    def _():
        step(0)

    @pl.when(e % 2 == 1)
    def _():
        step(1)

    @pl.when(e == n_exp - 1)
    def _():
        def body(gi, carry):
            for s in range(n_sets):
                scatter_group(1, s, e, gi * ROW_GROUP)
            return carry
        lax.fori_loop(0, cap // ROW_GROUP, body, 0)


MOE_SETS = 2


def moe_experts(rows, aff_rows, idx, w_gate, w_up, w_down, layer):
    G, T, RW = rows.shape
    AW = aff_rows.shape[2]
    _, E, cap = idx.shape
    D = w_gate.shape[2]
    ns = MOE_SETS
    assert E % 2 == 0 and cap % ROW_GROUP == 0 and G % ns == 0
    wspec = pl.BlockSpec((None, 1, D, D), lambda g, e: (layer, e, 0, 0))
    once = pl.Buffered(1)
    return pl.pallas_call(
        _moe_kernel,
        grid=(G // ns, E),
        in_specs=[pl.BlockSpec((1, ns, E, cap), lambda g, e: (g, 0, 0, 0), memory_space=pltpu.SMEM),
                  pl.BlockSpec((ns, T, RW), lambda g, e: (g, 0, 0), pipeline_mode=once),
                  pl.BlockSpec((ns, T, AW), lambda g, e: (g, 0, 0), pipeline_mode=once),
                  wspec, wspec, wspec],
        out_specs=pl.BlockSpec((ns, T, D), lambda g, e: (g, 0, 0), pipeline_mode=once),
        out_shape=jax.ShapeDtypeStruct((G, T, D), jnp.float32),
        scratch_shapes=[pltpu.VMEM((2, ns, cap, RW), rows.dtype), pltpu.VMEM((2, ns, cap, AW), jnp.float32),
                        pltpu.VMEM((2, ns, cap, D), jnp.float32)],
        compiler_params=pltpu.CompilerParams(
            dimension_semantics=("arbitrary", "arbitrary"), vmem_limit_bytes=VMEM_LIMIT),
        name="moe_experts",
    )(idx.reshape(G // ns, ns, E, cap), rows, aff_rows, w_gate, w_up, w_down)


def _residual_norm_kernel(x_ref, y_ref, gate_ref, gain_ref, o_ref):
    y = y_ref[0]
    yn = y * lax.rsqrt(jnp.mean(y * y, axis=-1, keepdims=True) + RMS_EPS) * gain_ref[...]
    o_ref[0] = x_ref[0] + gate_ref[0] * yn


def residual_norm(x, y, gate, gain, *, tm):
    B, T, D = x.shape
    tile = pl.BlockSpec((1, tm, D), lambda b, i: (b, i, 0))
    return pl.pallas_call(
        _residual_norm_kernel,
        grid=(B, T // tm),
        in_specs=[tile, tile, pl.BlockSpec((1, 1, D), lambda b, i: (b, 0, 0)), _const_spec((1, D))],
        out_specs=tile,
        out_shape=jax.ShapeDtypeStruct((B, T, D), jnp.float32),
        compiler_params=pltpu.CompilerParams(
            dimension_semantics=("arbitrary", "arbitrary"), vmem_limit_bytes=VMEM_LIMIT),
        name="residual_norm",
    )(x, y, gate, gain)


ADA_COLS = 1536


def _ada_kernel(c_ref, w_ref, b_ref, o_ref):
    c = c_ref[...]
    h = (c * _sigmoid(c)).astype(jnp.bfloat16)
    o_ref[...] = _dot_f32(h, w_ref[...].astype(jnp.bfloat16)) + b_ref[...]


def ada_modulation(cond, ada_w, ada_b, layer):
    R, D = cond.shape
    N = ada_w.shape[2]
    return pl.pallas_call(
        _ada_kernel,
        grid=(N // ADA_COLS,),
        in_specs=[pl.BlockSpec((R, D), lambda j: (0, 0)),
                  pl.BlockSpec((None, D, ADA_COLS), lambda j: (layer, 0, j)),
                  pl.BlockSpec((None, 1, ADA_COLS), lambda j: (layer, 0, j))],
        out_specs=pl.BlockSpec((R, ADA_COLS), lambda j: (0, j)),
        out_shape=jax.ShapeDtypeStruct((R, N), jnp.float32),
        compiler_params=pltpu.CompilerParams(dimension_semantics=("arbitrary",), vmem_limit_bytes=VMEM_LIMIT),
        name="ada_modulation",
    )(cond, ada_w, ada_b[:, None, :])


def kernel(x, c, ctx, c_ctx, norm_gain, ada_w, ada_b, w_in, mlstm_gate_bias, pool_w, pool_scale,
           na_rpb, w_branch, w_out, router_w, w_gate, w_up, w_down):
    B, S, D = x.shape
    Tc = ctx.shape[1]
    f32, bf16 = jnp.float32, jnp.bfloat16
    rope_tabs = rope_lane_tables(S)
    names = [n for n, _ in IN_GROUPS] + ["gt"]
    cond = jnp.pad(jnp.concatenate([c, c_ctx[None]], axis=0), ((0, (-(B + 1)) % 8), (0, 0)))
    moe_w = (w_gate.astype(bf16), w_up.astype(bf16), w_down.astype(bf16))
    pending = None
    for l in range(DEPTH):
        need_ctx = l < DEPTH - 1
        g = norm_gain[l]
        mods = ada_modulation(cond, ada_w, ada_b, l)
        mod_l = jnp.split(mods[:B, None, :], N_MOD, axis=-1)
        mod_cb = [jnp.broadcast_to(m[None, None], (B, 1, D)) for m in jnp.split(mods[B], N_MOD, axis=-1)]
        w_perm = permute_w_in(w_in[l])
        gb_row = jnp.pad(mlstm_gate_bias[l].reshape(1, -1).astype(f32), ((0, 0), (0, GATE_W - 4 * M_HEADS)))
        proj = in_proj(x, g[0][None], mod_l[0], mod_l[1], w_perm, gb_row, rope_tabs, pending, tm=512)
        if pending is not None:
            x = proj[-1]
        pl_ = dict(zip(names, proj))
        pc_ = dict(zip(names, in_proj(ctx, g[0][None], mod_cb[0], mod_cb[1], w_perm, gb_row, tm=Tc)))

        m_l, m_c = mlstm(pc_, pl_, need_ctx)
        n_l, n_c = neighbourhood_attention(pl_["qn"], pl_["kn"], pl_["vn"], pc_["kn"], pc_["vn"],
                                           na_bias_table(na_rpb[l]), pc_["qn"] if need_ctx else None)
        gains = jnp.pad(g[1:3], ((0, 6), (0, 0)))
        merge_w = (pool_w[l].astype(bf16), pool_scale[l][None], w_branch[l].astype(bf16),
                   (0.5 * w_out[l]).astype(bf16),
                   router_w[l].T.astype(bf16),
                   jnp.pad(router_w[l], ((0, 0), (0, LANES - N_EXPERTS))).astype(bf16))

        def mods8(mods):
            return jnp.pad(jnp.concatenate([mods[2], mods[3], mods[4]], axis=1), ((0, 0), (0, 5), (0, 0)))

        x1, rows, aff_r, aff_t = merge_out(m_l, n_l, pl_["pu"], pl_["bg"], x, mods8(mod_l), gains, *merge_w, tm=512)
        y = moe_experts(rows, aff_r, route(aff_t, EC_FACTOR * S // N_EXPERTS), *moe_w, l)
        if need_ctx:
            x, pending = x1, (y, mod_l[5], g[3][None])
        else:
            x = residual_norm(x1, y, mod_l[5], g[3][None], tm=512)
        if need_ctx:
            c1, rows, aff_r, aff_t = merge_out(m_c, n_c, pc_["pu"], pc_["bg"], ctx, mods8(mod_cb), gains, *merge_w, tm=Tc)
            per = S // Tc
            cap_c = EC_FACTOR * Tc // N_EXPERTS
            idx = route(aff_t, cap_c) + (jnp.arange(B, dtype=jnp.int32) % per * Tc)[:, None, None]
            idx = idx.reshape(B // per, per, N_EXPERTS, cap_c).transpose(0, 2, 1, 3).reshape(B // per, N_EXPERTS, per * cap_c)
            y = moe_experts(rows.reshape(B // per, S, D // 2), aff_r.reshape(B // per, S, LANES), idx,
                            *moe_w, l).reshape(B, Tc, D)
            ctx = residual_norm(c1, y, mod_cb[5], g[3][None], tm=Tc)
    return x
```

```python
import functools
import math

import jax
import jax.numpy as jnp
import numpy as np
from jax import lax
from jax.experimental import pallas as pl
from jax.experimental.pallas import tpu as pltpu

D_MODEL = 1024
DEPTH = 2
GRID_W = 64
BRANCH_W = D_MODEL // 2
N_BRANCH = 3
M_HEADS = 4
M_DV = BRANCH_W // M_HEADS
M_DQK = M_DV // 2
M_CHUNK = 128
M_INIT = -1e30
POOL_GROUPS = 4
POOL_GC = BRANCH_W // POOL_GROUPS
POOL_WINDOWS = (2, 4, 8, 16)
NA_HEADS = 8
NA_DH = BRANCH_W // NA_HEADS
NA_KH = 8
NA_KW = 16
N_EXPERTS = 16
EC_FACTOR = 2
ROPE_BASE = 10000.0
RMS_EPS = 1e-6
N_MOD = 6
SPLIT_SIZES = (M_HEADS * M_DQK, M_HEADS * M_DQK, M_HEADS * M_DV, M_HEADS * M_DV, 4 * M_HEADS,
               POOL_GROUPS * POOL_GC, NA_HEADS * NA_DH, NA_HEADS * NA_DH, NA_HEADS * NA_DH,
               N_BRANCH * D_MODEL)
PROJ_W = sum(SPLIT_SIZES)

LANES = 128
VMEM_LIMIT = 56 * 1024 * 1024
MASK_NEG = -1e30
GATE_W = LANES

IN_GROUPS = (("mqk", 2 * M_HEADS * M_DQK), ("mvt", BRANCH_W), ("mo", BRANCH_W), ("pu", BRANCH_W),
             ("qn", BRANCH_W), ("kn", BRANCH_W), ("vn", BRANCH_W), ("bg", N_BRANCH * D_MODEL),
             ("g", GATE_W))
IN_W = sum(w for _, w in IN_GROUPS)
MM_COLS = 512
PACK_WORD = jnp.uint32


def _sigmoid(x):
    return 0.5 * jnp.tanh(0.5 * x) + 0.5


def _const_spec(shape):
    nd = len(shape)
    return pl.BlockSpec(shape, lambda *_: (0,) * nd, pipeline_mode=pl.Buffered(1))


def _rope_tile(x, cos, sin_signed):
    half = M_DQK // 4
    lane = lax.broadcasted_iota(jnp.int32, x.shape, 1)
    partner = jnp.where((lane % (2 * half)) < half, pltpu.roll(x, LANES - half, 1), pltpu.roll(x, half, 1))
    return x * cos + partner * sin_signed


def _in_proj_kernel(x_ref, g_ref, shift_ref, scale_ref, w_ref, gb_ref, *rest, rope, pending):
    if rope:
        cos_ref, sin_ref = rest[:2]
        rest = rest[2:]
    if pending:
        r_ref, rgate_ref, rgain_ref = rest[:3]
        rest, xo_ref = rest[3:-1], rest[-1]
        r = r_ref[0]
        x = x_ref[0] + rgate_ref[0] * (r * lax.rsqrt(jnp.mean(r * r, axis=-1, keepdims=True) + RMS_EPS)
                                       * rgain_ref[...])
        xo_ref[0] = x
    else:
        x = x_ref[0]
    o_refs, gt_ref = rest[:-1], rest[-1]
    y = x * lax.rsqrt(jnp.mean(x * x, axis=-1, keepdims=True) + RMS_EPS)
    h = ((y * g_ref[...]) * (1.0 + scale_ref[0]) + shift_ref[0]).astype(jnp.bfloat16)
    off = 0
    for (name, width), o_ref in zip(IN_GROUPS, o_refs):
        for c0 in range(0, width, MM_COLS):
            cw = min(MM_COLS, width - c0)
            acc = jnp.dot(h, w_ref[:, off + c0:off + c0 + cw], preferred_element_type=jnp.float32)
            if name == "g":
                acc = acc + gb_ref[...]
                for c in range(acc.shape[0] // M_CHUNK):
                    gt_ref[0, c] = acc[c * M_CHUNK:(c + 1) * M_CHUNK, :].T[:4 * M_HEADS, :]
            if name == "mqk" and rope:
                cos, sin = cos_ref[...], sin_ref[...]
                acc = jnp.concatenate([_rope_tile(acc[:, t:t + LANES], cos, sin) for t in range(0, cw, LANES)], axis=-1)
            if name == "mvt":
                for c in range(acc.shape[0] // M_CHUNK):
                    o_ref[0, c] = acc[c * M_CHUNK:(c + 1) * M_CHUNK, :].T.astype(o_ref.dtype)
                continue
            o_ref[0, :, c0:c0 + cw] = acc.astype(o_ref.dtype)
        off += width


def in_proj(x, g, shift, scale, w_perm, gate_bias_row, rope_tabs=None, pending=None, *, tm):
    B, T, D = x.shape
    out_shape = [jax.ShapeDtypeStruct((B, T, w), jnp.float32 if n == "g" else jnp.bfloat16) for n, w in IN_GROUPS]
    out_specs = [pl.BlockSpec((1, tm, w), lambda b, i: (b, i, 0)) for _, w in IN_GROUPS]
    k_mvt = [n for n, _ in IN_GROUPS].index("mvt")
    assert BRANCH_W == MM_COLS and tm % M_CHUNK == 0
    out_shape[k_mvt] = jax.ShapeDtypeStruct((B, T // M_CHUNK, BRANCH_W, M_CHUNK), jnp.bfloat16)
    out_specs[k_mvt] = pl.BlockSpec((1, tm // M_CHUNK, BRANCH_W, M_CHUNK), lambda b, i: (b, i, 0, 0))
    out_shape.append(jax.ShapeDtypeStruct((B, T // M_CHUNK, 4 * M_HEADS, M_CHUNK), jnp.float32))
    out_specs.append(pl.BlockSpec((1, tm // M_CHUNK, 4 * M_HEADS, M_CHUNK), lambda b, i: (b, i, 0, 0)))
    in_specs = [
        pl.BlockSpec((1, tm, D), lambda b, i: (b, i, 0)),
        _const_spec((1, D)),
        pl.BlockSpec((1, 1, D), lambda b, i: (b, 0, 0)),
        pl.BlockSpec((1, 1, D), lambda b, i: (b, 0, 0)),
        _const_spec((D, IN_W)),
        _const_spec((1, GATE_W)),
    ]
    args = [x, g, shift, scale, w_perm, gate_bias_row]
    if rope_tabs is not None:
        in_specs += [pl.BlockSpec((tm, LANES), lambda b, i: (i, 0))] * 2
        args += list(rope_tabs)
    if pending is not None:
        in_specs += [pl.BlockSpec((1, tm, D), lambda b, i: (b, i, 0)), pl.BlockSpec((1, 1, D), lambda b, i: (b, 0, 0)),
                     _const_spec((1, D))]
        args += list(pending)
        out_shape.append(jax.ShapeDtypeStruct((B, T, D), jnp.float32))
        out_specs.append(pl.BlockSpec((1, tm, D), lambda b, i: (b, i, 0)))
    return pl.pallas_call(
        functools.partial(_in_proj_kernel, rope=rope_tabs is not None, pending=pending is not None),
        grid=(B, T // tm),
        in_specs=in_specs,
        out_specs=out_specs,
        out_shape=out_shape,
        compiler_params=pltpu.CompilerParams(
            dimension_semantics=("arbitrary", "arbitrary"), vmem_limit_bytes=VMEM_LIMIT),
        name="in_proj",
    )(*args)


def rope_lane_tables(n_tokens):
    t = jnp.arange(n_tokens)
    row = (t // GRID_W).astype(jnp.float32)
    col = (t % GRID_W).astype(jnp.float32)
    half = M_DQK // 2
    inv = ROPE_BASE ** (-jnp.arange(0, half, 2, dtype=jnp.float32) / half)
    ar = row[:, None] * inv[None, :]
    ac = col[:, None] * inv[None, :]
    cos = jnp.concatenate([jnp.cos(ar), jnp.cos(ar), jnp.cos(ac), jnp.cos(ac)], axis=-1)
    sin = jnp.concatenate([-jnp.sin(ar), jnp.sin(ar), -jnp.sin(ac), jnp.sin(ac)], axis=-1)
    return jnp.tile(cos, (1, LANES // M_DQK)), jnp.tile(sin, (1, LANES // M_DQK))


def permute_w_in(w_in_l):
    qm, km, vm, om, gm, pu, qn, kn, vn, bg = jnp.split(w_in_l, [int(s) for s in np.cumsum(SPLIT_SIZES)[:-1]], axis=-1)
    gpad = jnp.pad(gm, ((0, 0), (0, GATE_W - gm.shape[1])))
    w = jnp.concatenate([qm * (M_DQK ** -0.5), km, vm, om, pu, qn * (NA_DH ** -0.5), kn, vn, bg * 0.5, gpad], axis=-1)
    return w.astype(jnp.bfloat16)


def _pair_scores(qp, k_parts, biases):
    n = qp.shape[0]
    lane_lo = lax.broadcasted_iota(jnp.int32, (n, LANES), 1) < NA_DH
    zero = jnp.zeros_like(qp)
    qq = jnp.concatenate([jnp.where(lane_lo, qp, zero), jnp.where(lane_lo, zero, qp)], axis=0)
    scores = []
    for kp, bias in zip(k_parts, biases):
        s = lax.dot_general(qq, kp, (((1,), (1,)), ((), ())), preferred_element_type=jnp.float32)
        scores.append(s if bias is None else s + bias)
    return scores


def _pair_softmax_pv(scores, v_parts):
    n = scores[0].shape[0] // 2
    lane_lo = lax.broadcasted_iota(jnp.int32, (n, LANES), 1) < NA_DH
    m = scores[0].max(axis=-1, keepdims=True)
    for s in scores[1:]:
        m = jnp.maximum(m, s.max(axis=-1, keepdims=True))
    l = None
    o = None
    for s, vp in zip(scores, v_parts):
        p = jnp.exp(s - m)
        ls = p.sum(axis=-1, keepdims=True)
        os_ = jnp.dot(p.astype(jnp.bfloat16), vp, preferred_element_type=jnp.float32)
        l = ls if l is None else l + ls
        o = os_ if o is None else o + os_
    o = o * (1.0 / l)
    return jnp.where(lane_lo, o[:n], o[n:])


def _na_kernel(q_ref, k_ref, v_ref, kc_ref, vc_ref, bias_ref, *rest, need_ctx):
    if need_ctx:
        qc_ref, o_ref, oc_ref, s_ref = rest
    else:
        o_ref, s_ref = rest
    S = q_ref.shape[1]
    Tc = kc_ref.shape[1]
    rows = S // GRID_W
    n_pairs = BRANCH_W // LANES
    n_loc = NA_KH * GRID_W

    def window(r):
        rs = jnp.clip(r - NA_KH // 2, 0, rows - NA_KH)
        return r - rs, pl.multiple_of(r * GRID_W, GRID_W), pl.multiple_of(rs * GRID_W, GRID_W)

    def scores_stage(r, slot):
        var, q0, k0 = window(r)
        for j in range(n_pairs):
            ls = slice(j * LANES, (j + 1) * LANES)
            s_loc, s_ctx = _pair_scores(q_ref[0, pl.ds(q0, GRID_W), ls],
                                        (k_ref[0, pl.ds(k0, n_loc), ls], kc_ref[0, :, ls]),
                                        (bias_ref[var, j], None))
            s_ref[slot, j, :, :n_loc] = s_loc
            s_ref[slot, j, :, n_loc:] = s_ctx

    def output_stage(r, slot):
        _, q0, k0 = window(r)
        outs = []
        for j in range(n_pairs):
            ls = slice(j * LANES, (j + 1) * LANES)
            outs.append(_pair_softmax_pv((s_ref[slot, j, :, :n_loc], s_ref[slot, j, :, n_loc:]),
                                         (v_ref[0, pl.ds(k0, n_loc), ls], vc_ref[0, :, ls])))
        o_ref[0, pl.ds(q0, GRID_W), :] = jnp.concatenate(outs, axis=-1).astype(o_ref.dtype)

    scores_stage(0, 0)

    def two_rows(i, carry):
        r0 = 2 * i
        scores_stage(r0 + 1, 1)
        output_stage(r0, 0)
        scores_stage(jnp.minimum(r0 + 2, rows - 1), 0)
        output_stage(r0 + 1, 1)
        return carry

    lax.fori_loop(0, rows // 2, two_rows, 0)

    if need_ctx:
        outs = []
        for j in range(n_pairs):
            ls = slice(j * LANES, (j + 1) * LANES)
            scores = _pair_scores(qc_ref[0, :, ls], (kc_ref[0, :, ls],), (None,))
            outs.append(_pair_softmax_pv(scores, (vc_ref[0, :, ls],)))
        oc_ref[0] = jnp.concatenate(outs, axis=-1).astype(oc_ref.dtype)


def na_bias_table(rpb):
    H = rpb.shape[0]
    var = jnp.arange(NA_KH)
    kr = jnp.arange(NA_KH)
    dr = kr[None, :] - var[:, None] + NA_KH - 1
    cols = jnp.arange(GRID_W)
    dc = jnp.clip(cols[None, :] - cols[:, None] + NA_KW - 1, 0, 2 * NA_KW - 2)
    cs = jnp.clip(cols - NA_KW // 2, 0, GRID_W - NA_KW)
    colmask = (cols[None, :] >= cs[:, None]) & (cols[None, :] < cs[:, None] + NA_KW)
    pick_r = jax.nn.one_hot(dr, 2 * NA_KH - 1, dtype=jnp.float32)
    pick_c = jax.nn.one_hot(dc, 2 * NA_KW - 1, dtype=jnp.float32)
    tab = jnp.einsum('vka,hab,qcb->vhqkc', pick_r, rpb.astype(jnp.float32), pick_c, precision=lax.Precision.HIGHEST)
    tab = jnp.where(colmask[None, None, :, None, :], tab, MASK_NEG)
    return tab.reshape(NA_KH, H // 2, 2 * GRID_W, NA_KH * GRID_W)


def neighbourhood_attention(qn, kn, vn, kc, vc, bias_tab, qc=None):
    B, S, W = qn.shape
    Tc = kc.shape[1]
    need_ctx = qc is not None
    lat = pl.BlockSpec((1, S, W), lambda b: (b, 0, 0))
    cx = pl.BlockSpec((1, Tc, W), lambda b: (b, 0, 0))
    in_specs = [lat, lat, lat, cx, cx, _const_spec(bias_tab.shape)]
    args = [qn, kn, vn, kc, vc, bias_tab]
    out_shape = [jax.ShapeDtypeStruct((B, S, W), jnp.bfloat16)]
    out_specs = [lat]
    if need_ctx:
        in_specs.append(cx)
        args.append(qc)
        out_shape.append(jax.ShapeDtypeStruct((B, Tc, W), jnp.bfloat16))
        out_specs.append(cx)
    res = pl.pallas_call(
        functools.partial(_na_kernel, need_ctx=need_ctx),
        grid=(B,),
        in_specs=in_specs,
        out_specs=out_specs,
        out_shape=out_shape,
        scratch_shapes=[pltpu.VMEM((2, W // LANES, 2 * GRID_W, NA_KH * GRID_W + Tc), jnp.float32)],
        compiler_params=pltpu.CompilerParams(dimension_semantics=("arbitrary",), vmem_limit_bytes=VMEM_LIMIT),
        name="neighbourhood_attention",
    )(*args)
    return (res[0], res[1]) if need_ctx else (res[0], None)


def _split_bf16(x):
    hi = x.astype(jnp.bfloat16)
    r1 = x - hi.astype(jnp.float32)
    mid = r1.astype(jnp.bfloat16)
    lo = (r1 - mid.astype(jnp.float32)).astype(jnp.bfloat16)
    return hi, mid, lo


def _dot_f32(a, b):
    return jnp.dot(a, b, preferred_element_type=jnp.float32)


def _log_sigmoid(x):
    return jnp.minimum(x, 0.0) - jnp.log(1.0 + jnp.exp(-jnp.abs(x)))


def _pair_queries(qp):
    lane_lo = lax.broadcasted_iota(jnp.int32, qp.shape, 1) < M_DQK
    zq = jnp.zeros_like(qp)
    return jnp.concatenate([jnp.where(lane_lo, qp, zq), jnp.where(lane_lo, zq, qp)], axis=0)


def _mlstm_prep(qk_ref, g_ref, gt_ref, sa_ref, ra_ref, slot, ci, fwd):
    L = M_CHUNK
    f32, bf16 = jnp.float32, jnp.bfloat16
    d = 0 if fwd else 1
    rows = pl.ds(pl.multiple_of(ci * L, L), L)
    sq_r = lax.broadcasted_iota(jnp.int32, (L, L), 0)
    sq_c = lax.broadcasted_iota(jnp.int32, (L, L), 1)
    tri_l = jnp.where(sq_r >= sq_c, 1.0, 0.0).astype(bf16)
    tri_u = jnp.where(sq_r <= sq_c, 1.0, 0.0).astype(bf16)
    gates = g_ref[0, rows, :]
    gates_t = gt_ref[0, ci]
    lf, lf_t = _log_sigmoid(gates), _log_sigmoid(gates_t)
    cum = sum(_dot_f32(tri_l if fwd else tri_u, p) for p in _split_bf16(lf))
    cum_t = sum(_dot_f32(p, tri_u if fwd else tri_l) for p in _split_bf16(lf_t))
    t_i = 0 if fwd else 2
    st_s = lax.broadcasted_iota(jnp.int32, (L, 2 * L), 0)
    st_t = lax.broadcasted_iota(jnp.int32, (L, 2 * L), 1) & (L - 1)
    valid = (st_s <= st_t) if fwd else (st_s >= st_t)

    def row2(tile, c0, c1):
        return jnp.concatenate([tile[c0:c0 + 1, :], tile[c1:c1 + 1, :]], axis=-1)

    for j in range(M_HEADS // 2):
        ci0, ci1 = t_i * M_HEADS + 2 * j, t_i * M_HEADS + 2 * j + 1
        cf0, cf1 = ci0 + M_HEADS, ci1 + M_HEADS
        b_row, i_row = row2(cum_t, cf0, cf1), row2(gates_t, ci0, ci1)
        colb = jnp.concatenate([jnp.broadcast_to(gates[:, ci0:ci0 + 1] - cum[:, cf0:cf0 + 1], (L, L)),
                                jnp.broadcast_to(gates[:, ci1:ci1 + 1] - cum[:, cf1:cf1 + 1], (L, L))],
                               axis=-1)
        qp = qk_ref[0, rows, j * LANES:(j + 1) * LANES]
        kp = qk_ref[0, rows, M_HEADS * M_DQK + j * LANES:M_HEADS * M_DQK + (j + 1) * LANES]
        sa_ref[slot, d, j, 0] = jnp.where(valid, b_row + colb, MASK_NEG)
        sa_ref[slot, d, j, 1] = lax.dot_general(kp, _pair_queries(qp), (((1,), (1,)), ((), ())),
                                                preferred_element_type=f32)
        ra_ref[slot, d, j, 0:1, :] = b_row
        ra_ref[slot, d, j, 1:2, :] = i_row


def _mlstm_step(qk_ref, vt_ref, sa_ref, ra_ref, slot, h_ref, ct_ref, m_ref, ci, fwd):
    L = M_CHUNK
    f32, bf16 = jnp.float32, jnp.bfloat16
    d = 0 if fwd else 1
    rows = pl.ds(pl.multiple_of(ci * L, L), L)
    last = L - 1 if fwd else 0
    lane_lo = lax.broadcasted_iota(jnp.int32, (L, LANES), 1) < M_DQK
    lane_lo_row = lax.broadcasted_iota(jnp.int32, (1, LANES), 1) < M_DQK
    ones = jnp.ones((M_DV, L), bf16)

    def halves(row, f):
        return jnp.concatenate([jnp.broadcast_to(f(row[:, :L]), (1, L)), jnp.broadcast_to(f(row[:, L:]), (1, L))],
                               axis=-1)

    for j in range(M_HEADS // 2):
        h0, h1 = 2 * j, 2 * j + 1
        sidx = 2 * j + d
        b_row, i_row = ra_ref[slot, d, j, 0:1, :], ra_ref[slot, d, j, 1:2, :]
        m_row = m_ref[sidx, 0:1, :]
        kp = qk_ref[0, rows, M_HEADS * M_DQK + j * LANES:M_HEADS * M_DQK + (j + 1) * LANES]
        vx0 = jnp.concatenate([vt_ref[0, ci, h0 * M_DV:(h0 + 1) * M_DV, :], ones], axis=0)
        vx1 = jnp.concatenate([vt_ref[0, ci, h1 * M_DV:(h1 + 1) * M_DV, :], ones], axis=0)
        ct = ct_ref[sidx]
        if h_ref is not None:
            dm = sa_ref[slot, d, j, 0]
            qq = _pair_queries(qk_ref[0, rows, j * LANES:(j + 1) * LANES])
            g_row = b_row + m_row
            mt = jnp.maximum(g_row, dm.max(axis=0, keepdims=True))
            sc = (sa_ref[slot, d, j, 1] * jnp.exp(dm - mt)).astype(bf16)
            inter = jnp.exp(g_row - mt)
            t1 = lax.dot_general(ct.astype(bf16), qq, (((1,), (1,)), ((), ())), preferred_element_type=f32)
            t2 = jnp.concatenate([_dot_f32(vx0, sc[:, :L]), _dot_f32(vx1, sc[:, L:])], axis=-1)
            tot = inter * t1 + t2
            h_t = tot[:M_DV] / jnp.maximum(jnp.abs(tot[M_DV:]), jnp.exp(-mt))
            h_ref[rows, h0 * M_DV:(h0 + 1) * M_DV] = h_t[:, :L].T
            h_ref[rows, h1 * M_DV:(h1 + 1) * M_DV] = h_t[:, L:].T
        bl = halves(b_row, lambda r: r[:, last:last + 1])
        w = bl - b_row + i_row
        m_new = jnp.maximum(bl + m_row, halves(w, lambda r: r.max(axis=-1, keepdims=True)))
        decay = jnp.exp(bl + m_row - m_new)
        ws = jnp.exp(w - m_new)
        vxs = jnp.concatenate([vx0.astype(f32) * ws[:, :L], vx1.astype(f32) * ws[:, L:]], axis=-1).astype(bf16)
        zk = jnp.zeros_like(kp)
        kk = jnp.concatenate([jnp.where(lane_lo, kp, zk), jnp.where(lane_lo, zk, kp)], axis=0)
        d_cols = jnp.where(lane_lo_row, jnp.broadcast_to(decay[:, 0:1], (1, LANES)),
                           jnp.broadcast_to(decay[:, L:L + 1], (1, LANES)))
        ct_ref[sidx] = d_cols * ct + _dot_f32(vxs, kk)
        m_ref[sidx, 0:1, :] = m_new


def _mlstm_finish(hf_ref, hb_ref, o_ref, out_ref, n_chunks):
    L = M_CHUNK

    def body(ci, carry):
        rows = pl.ds(pl.multiple_of(ci * L, L), L)
        h = hf_ref[rows, :] + hb_ref[rows, :]
        parts = []
        for hd in range(M_HEADS):
            hh = h[:, hd * M_DV:(hd + 1) * M_DV]
            parts.append(hh * lax.rsqrt(jnp.mean(hh * hh, axis=-1, keepdims=True) + RMS_EPS))
        gate = _sigmoid(o_ref[0, rows, :].astype(jnp.float32))
        out_ref[0, rows, :] = (gate * jnp.concatenate(parts, axis=-1)).astype(out_ref.dtype)
        return carry

    lax.fori_loop(0, n_chunks, body, 0)


def _mlstm_kernel(qk_c, v_c, o_c, g_c, gt_c, qk_l, v_l, o_l, g_l, gt_l, *rest, need_ctx):
    if need_ctx:
        out_l, out_c, hf_l, hb_l, hf_c, hb_c, ct_ref, m_ref, sa_ref, ra_ref = rest
    else:
        out_l, hf_l, hb_l, ct_ref, m_ref, sa_ref, ra_ref = rest
        hf_c = hb_c = None
    nc, nl = qk_c.shape[1] // M_CHUNK, qk_l.shape[1] // M_CHUNK
    ct_ref[...] = jnp.zeros_like(ct_ref)
    m_ref[...] = jnp.full_like(m_ref, M_INIT)

    def phase(qk, vt, g, gt, hf, hb, n):
        def prep(s, slot):
            _mlstm_prep(qk, g, gt, sa_ref, ra_ref, slot, s, True)
            _mlstm_prep(qk, g, gt, sa_ref, ra_ref, slot, n - 1 - s, False)

        def step(s, slot):
            _mlstm_step(qk, vt, sa_ref, ra_ref, slot, hf, ct_ref, m_ref, s, True)
            _mlstm_step(qk, vt, sa_ref, ra_ref, slot, hb, ct_ref, m_ref, n - 1 - s, False)

        prep(0, 0)

        def body(i, carry):
            s0 = 2 * i
            prep(s0 + 1, 1)
            step(s0, 0)
            prep(jnp.minimum(s0 + 2, n - 1), 0)
            step(s0 + 1, 1)
            return carry
        lax.fori_loop(0, n // 2, body, 0)

    phase(qk_c, v_c, g_c, gt_c, hf_c, hb_c, nc)
    phase(qk_l, v_l, g_l, gt_l, hf_l, hb_l, nl)
    _mlstm_finish(hf_l, hb_l, o_l, out_l, nl)
    if need_ctx:
        _mlstm_finish(hf_c, hb_c, o_c, out_c, nc)


def mlstm(pc, pl_, need_ctx):
    B, S, W = pl_["mo"].shape
    Tc = pc["mo"].shape[1]
    f32 = jnp.float32

    def specs(T):
        s = pl.BlockSpec((1, T, W), lambda b: (b, 0, 0))
        return [s, pl.BlockSpec((1, T // M_CHUNK, W, M_CHUNK), lambda b: (b, 0, 0, 0)), s,
                pl.BlockSpec((1, T, GATE_W), lambda b: (b, 0, 0)),
                pl.BlockSpec((1, T // M_CHUNK, 4 * M_HEADS, M_CHUNK), lambda b: (b, 0, 0, 0))]

    def args(p):
        return [p["mqk"], p["mvt"], p["mo"], p["g"], p["gt"]]

    out_shape = [jax.ShapeDtypeStruct((B, S, W), jnp.bfloat16)]
    out_specs = [pl.BlockSpec((1, S, W), lambda b: (b, 0, 0))]
    scratch = [pltpu.VMEM((S, W), f32), pltpu.VMEM((S, W), f32)]
    if need_ctx:
        out_shape.append(jax.ShapeDtypeStruct((B, Tc, W), jnp.bfloat16))
        out_specs.append(pl.BlockSpec((1, Tc, W), lambda b: (b, 0, 0)))
        scratch += [pltpu.VMEM((Tc, W), f32), pltpu.VMEM((Tc, W), f32)]
    assert (S // M_CHUNK) % 2 == 0 and (Tc // M_CHUNK) % 2 == 0
    n_pairs = M_HEADS // 2
    scratch += [pltpu.VMEM((M_HEADS, 2 * M_DV, LANES), f32), pltpu.VMEM((M_HEADS, 8, 2 * M_CHUNK), f32),
                pltpu.VMEM((2, 2, n_pairs, 2, M_CHUNK, 2 * M_CHUNK), f32),
                pltpu.VMEM((2, 2, n_pairs, 8, 2 * M_CHUNK), f32)]
    res = pl.pallas_call(
        functools.partial(_mlstm_kernel, need_ctx=need_ctx),
        grid=(B,),
        in_specs=specs(Tc) + specs(S),
        out_specs=out_specs,
        out_shape=out_shape,
        scratch_shapes=scratch,
        compiler_params=pltpu.CompilerParams(dimension_semantics=("arbitrary",), vmem_limit_bytes=VMEM_LIMIT),
        name="mlstm",
    )(*args(pc), *args(pl_))
    return (res[0], res[1]) if need_ctx else (res[0], None)


POOL_HALO = 16


def _pool_mix(ext, tok0, pw_ref, ps_ref, seq_len):
    f32 = jnp.float32
    n_ext = ext.shape[0]
    ts = n_ext - 2 * POOL_HALO
    tok = tok0 + lax.broadcasted_iota(jnp.int32, (ts, POOL_GC), 0)

    def shifted(a, d):
        return pltpu.roll(a, (-d) % n_ext, 0)

    outs = []
    for gi, w in enumerate(POOL_WINDOWS):
        s = ext[:, gi * POOL_GC:(gi + 1) * POOL_GC]
        cur = s[POOL_HALO:POOL_HALO + ts]
        s = shifted(s, -1) + s
        span = 2
        while span < w:
            s = shifted(s, -(span // 2)) + shifted(s, span // 2)
            span *= 2
        lo = jnp.clip(tok - w // 2, 0, seq_len - 1)
        hi = jnp.clip(tok - w // 2 + w - 1, 0, seq_len - 1)
        mean = s[POOL_HALO:POOL_HALO + ts] / (hi - lo + 1).astype(f32)
        outs.append(_dot_f32((mean - cur).astype(jnp.bfloat16), pw_ref[gi]))
    return jnp.concatenate(outs, axis=-1) * ps_ref[...]


def _pack_bf16_pairs(h):
    n = h.shape[1] // 2
    return pltpu.pack_elementwise([h[:, :n], h[:, n:]], packed_dtype=jnp.bfloat16)


MERGE_SUB = 256


def _merge_kernel(m_ref, n_ref, pu_ref, prev_ref, next_ref, bg_ref, x_ref, mods_ref, gains_ref,
                  pw_ref, ps_ref, wb_ref, wo_ref, wrt_ref, wrp_ref, x1_ref, h2p_ref, affr_ref, aff_ref,
                  ext_ref, y_ref, *, seq_len):
    f32, bf16 = jnp.float32, jnp.bfloat16
    i = pl.program_id(1)
    tm, D = x_ref.shape[1], x_ref.shape[2]
    E = wrt_ref.shape[0]
    ts = min(MERGE_SUB, tm)
    ext_ref[0:POOL_HALO] = jnp.where(i > 0, prev_ref[0].astype(f32), 0.0)
    ext_ref[POOL_HALO:POOL_HALO + tm] = pu_ref[0].astype(f32)
    ext_ref[POOL_HALO + tm:] = jnp.where(i < pl.num_programs(1) - 1, next_ref[0].astype(f32), 0.0)

    def mix_stage(k):
        rows = slice(k * ts, (k + 1) * ts)
        p = _pool_mix(ext_ref[k * ts:(k + 1) * ts + 2 * POOL_HALO], i * tm + k * ts, pw_ref, ps_ref, seq_len)
        acc = None
        for bi, br in enumerate((m_ref[0, rows], p.astype(bf16), n_ref[0, rows])):
            gate = jnp.tanh(bg_ref[0, rows, bi * D:(bi + 1) * D].astype(f32)) + 1.0
            term = gate * _dot_f32(br, wb_ref[bi])
            acc = term if acc is None else acc + term
        y_ref[k % 2] = _dot_f32(acc.astype(bf16), wo_ref[...])

    def out_stage(k):
        rows = slice(k * ts, (k + 1) * ts)
        y = y_ref[k % 2]
        yn = y * lax.rsqrt(jnp.mean(y * y, axis=-1, keepdims=True) + RMS_EPS) * gains_ref[0:1, :]
        x1 = x_ref[0, rows] + mods_ref[0, 0:1, :] * yn
        x1_ref[0, rows] = x1
        xn = x1 * lax.rsqrt(jnp.mean(x1 * x1, axis=-1, keepdims=True) + RMS_EPS) * gains_ref[1:2, :]
        h2 = xn * (1.0 + mods_ref[0, 2:3, :]) + mods_ref[0, 1:2, :]
        h2b = h2.astype(bf16)
        lg_t = lax.dot_general(wrt_ref[...], h2b, (((1,), (1,)), ((), ())), preferred_element_type=f32)
        e_t = jnp.exp(lg_t - lg_t.max(axis=0, keepdims=True))
        aff_ref[0, :, rows] = e_t / e_t.sum(axis=0, keepdims=True)
        lg = _dot_f32(h2b, wrp_ref[...])
        lg = jnp.where(lax.broadcasted_iota(jnp.int32, lg.shape, 1) < E, lg, MASK_NEG)
        e_r = jnp.exp(lg - lg.max(axis=-1, keepdims=True))
        affr_ref[0, rows] = e_r / e_r.sum(axis=-1, keepdims=True)
        h2p_ref[0, rows] = _pack_bf16_pairs(h2)

    n_sub = tm // ts
    mix_stage(0)
    for k in range(n_sub):
        if k + 1 < n_sub:
            mix_stage(k + 1)
        out_stage(k)


def merge_out(m, n, pu, bg, x, mods, gains, pool_w, pool_scale, w_branch, w_out, w_router_t, w_router_pad, *, tm):
    B, T, D = x.shape
    W = m.shape[2]
    E = w_router_t.shape[0]
    hb = tm // POOL_HALO
    n_halo = T // POOL_HALO
    tile = lambda w: pl.BlockSpec((1, tm, w), lambda b, i: (b, i, 0))
    in_specs = [
        tile(W), tile(W), tile(W),
        pl.BlockSpec((1, POOL_HALO, W), lambda b, i: (b, jnp.maximum(i * hb - 1, 0), 0)),
        pl.BlockSpec((1, POOL_HALO, W), lambda b, i: (b, jnp.minimum((i + 1) * hb, n_halo - 1), 0)),
        tile(N_BRANCH * D), tile(D),
        pl.BlockSpec((1, 8, D), lambda b, i: (b, 0, 0)),
        _const_spec((8, D)),
        _const_spec(pool_w.shape), _const_spec(pool_scale.shape), _const_spec(w_branch.shape),
        _const_spec(w_out.shape), _const_spec(w_router_t.shape), _const_spec(w_router_pad.shape),
    ]
    return pl.pallas_call(
        functools.partial(_merge_kernel, seq_len=T),
        grid=(B, T // tm),
        in_specs=in_specs,
        out_specs=[tile(D), tile(D // 2), tile(LANES), pl.BlockSpec((1, E, tm), lambda b, i: (b, 0, i))],
        out_shape=[jax.ShapeDtypeStruct((B, T, D), jnp.float32),
                   jax.ShapeDtypeStruct((B, T, D // 2), PACK_WORD),
                   jax.ShapeDtypeStruct((B, T, LANES), jnp.float32),
                   jax.ShapeDtypeStruct((B, E, T), jnp.float32)],
        scratch_shapes=[pltpu.VMEM((tm + 2 * POOL_HALO, W), jnp.float32),
                        pltpu.VMEM((2, min(MERGE_SUB, tm), D), jnp.float32)],
        compiler_params=pltpu.CompilerParams(
            dimension_semantics=("arbitrary", "arbitrary"), vmem_limit_bytes=VMEM_LIMIT),
        name="merge_out",
    )(m, n, pu, pu, pu, bg, x, mods, gains, pool_w, pool_scale, w_branch, w_out, w_router_t, w_router_pad)


def _cumsum_lanes(x01):
    R, T = x01.shape
    r = lax.broadcasted_iota(jnp.int32, (LANES, LANES), 0)
    c = lax.broadcasted_iota(jnp.int32, (LANES, LANES), 1)
    tri_u = jnp.where(r <= c, 1.0, 0.0).astype(jnp.bfloat16)
    run = jnp.zeros((R, 1), jnp.float32)
    outs = []
    for t0 in range(0, T, LANES):
        cs = _dot_f32(x01[:, t0:t0 + LANES].astype(jnp.bfloat16), tri_u) + run
        run = cs[:, LANES - 1:LANES]
        outs.append(cs)
    return jnp.concatenate(outs, axis=-1)


def _route_kernel(aff_ref, idx_ref, *, cap):
    f32 = jnp.float32
    aff = aff_ref[0]
    E, T = aff.shape

    def search(k, lo):
        cand = lo | (jnp.int32(1) << (30 - k))
        cnt = jnp.sum(jnp.where(aff >= pltpu.bitcast(cand, f32), 1.0, 0.0), axis=-1, keepdims=True)
        return jnp.where(cnt >= cap, cand, lo)

    thr = pltpu.bitcast(lax.fori_loop(0, 31, search, jnp.zeros((E, 1), jnp.int32)), f32)
    gt = jnp.where(aff > thr, 1.0, 0.0)
    eq = jnp.where(aff == thr, 1.0, 0.0)
    room = cap - jnp.sum(gt, axis=-1, keepdims=True)
    sel = gt + eq * jnp.where(_cumsum_lanes(eq) <= room, 1.0, 0.0)
    csel = _cumsum_lanes(sel)
    slot = lax.broadcasted_iota(jnp.int32, (cap, T), 0).astype(f32)
    lane = lax.broadcasted_iota(jnp.int32, (cap, LANES), 1)
    out = jnp.zeros((cap, LANES), f32)
    for e in range(E):
        pos = jnp.sum(jnp.where(jnp.broadcast_to(csel[e:e + 1, :], (cap, T)) <= slot, 1.0, 0.0),
                      axis=-1, keepdims=True)
        out = jnp.where(lane == e, jnp.broadcast_to(pos, (cap, LANES)), out)
    idx_ref[0] = out.astype(jnp.int32)


def route(aff_t, cap):
    B, E, T = aff_t.shape
    idx_t = pl.pallas_call(
        functools.partial(_route_kernel, cap=cap),
        grid=(B,),
        in_specs=[pl.BlockSpec((1, E, T), lambda b: (b, 0, 0))],
        out_specs=pl.BlockSpec((1, cap, LANES), lambda b: (b, 0, 0)),
        out_shape=jax.ShapeDtypeStruct((B, cap, LANES), jnp.int32),
        compiler_params=pltpu.CompilerParams(dimension_semantics=("arbitrary",), vmem_limit_bytes=VMEM_LIMIT),
        name="route",
    )(aff_t)
    return idx_t[:, :, :E].transpose(0, 2, 1)


ROW_GROUP = 8


def _moe_kernel(idx_ref, rows_ref, affr_ref, wg_ref, wu_ref, wd_ref, out_ref, xg_ref, ag_ref, ye_ref):
    f32, bf16 = jnp.float32, jnp.bfloat16
    e = pl.program_id(1)
    n_exp = pl.num_programs(1)
    n_sets, cap = xg_ref.shape[1], xg_ref.shape[2]

    def gather_row(p, s, ee, j):
        i = idx_ref[0, s, ee, j]
        xg_ref[p, s, pl.ds(j, 1), :] = rows_ref[s, pl.ds(i, 1), :]
        ag_ref[p, s, pl.ds(j, 1), :] = affr_ref[s, pl.ds(i, 1), :]

    def scatter_group(p, s, ee, j0):
        ids = [idx_ref[0, s, ee, j0 + r] for r in range(ROW_GROUP)]
        cur = [out_ref[s, pl.ds(i, 1), :] for i in ids]
        for r, i in enumerate(ids):
            out_ref[s, pl.ds(i, 1), :] = cur[r] + ye_ref[p, s, pl.ds(j0 + r, 1), :]

    @pl.when(e == 0)
    def _():
        out_ref[...] = jnp.zeros_like(out_ref)
        ye_ref[1] = jnp.zeros_like(ye_ref[1])

        def body(gi, carry):
            for s in range(n_sets):
                for r in range(ROW_GROUP):
                    gather_row(0, s, 0, gi * ROW_GROUP + r)
            return carry
        lax.fori_loop(0, cap // ROW_GROUP, body, 0)

    def step(p):
        e_prv = jnp.maximum(e - 1, 0)
        e_nxt = jnp.minimum(e + 1, n_exp - 1)
        for s in range(n_sets):
            for j0 in range(0, cap, ROW_GROUP):
                scatter_group(1 - p, s, e_prv, j0)
            for j in range(cap):
                gather_row(1 - p, s, e_nxt, j)
        packed = jnp.concatenate([xg_ref[p, s] for s in range(n_sets)], axis=0)
        xe = jnp.concatenate(
            [pltpu.unpack_elementwise(packed, index=i, packed_dtype=bf16, unpacked_dtype=f32) for i in (0, 1)],
            axis=-1).astype(bf16)
        aff = jnp.concatenate([ag_ref[p, s] for s in range(n_sets)], axis=0)
        gate = jnp.sum(jnp.where(lax.broadcasted_iota(jnp.int32, aff.shape, 1) == e, aff, 0.0),
                       axis=-1, keepdims=True)
        a = _dot_f32(xe, wg_ref[0])
        hid = (a * _sigmoid(a) * _dot_f32(xe, wu_ref[0])).astype(bf16)
        ye = _dot_f32(hid, wd_ref[0]) * gate
        for s in range(n_sets):
            ye_ref[p, s] = ye[s * cap:(s + 1) * cap]

    @pl.when(e % 2 == 0)
    def _():
        step(0)

    @pl.when(e % 2 == 1)
    def _():
        step(1)

    @pl.when(e == n_exp - 1)
    def _():
        def body(gi, carry):
            for s in range(n_sets):
                scatter_group(1, s, e, gi * ROW_GROUP)
            return carry
        lax.fori_loop(0, cap // ROW_GROUP, body, 0)


MOE_SETS = 1


def moe_experts(rows, aff_rows, idx, w_gate, w_up, w_down, layer):
    G, T, RW = rows.shape
    AW = aff_rows.shape[2]
    _, E, cap = idx.shape
    D = w_gate.shape[2]
    ns = MOE_SETS
    assert E % 2 == 0 and cap % ROW_GROUP == 0 and G % ns == 0
    wspec = pl.BlockSpec((None, 1, D, D), lambda g, e: (layer, e, 0, 0))
    return pl.pallas_call(
        _moe_kernel,
        grid=(G // ns, E),
        in_specs=[pl.BlockSpec((1, ns, E, cap), lambda g, e: (g, 0, 0, 0), memory_space=pltpu.SMEM),
                  pl.BlockSpec((ns, T, RW), lambda g, e: (g, 0, 0)),
                  pl.BlockSpec((ns, T, AW), lambda g, e: (g, 0, 0)),
                  wspec, wspec, wspec],
        out_specs=pl.BlockSpec((ns, T, D), lambda g, e: (g, 0, 0)),
        out_shape=jax.ShapeDtypeStruct((G, T, D), jnp.float32),
        scratch_shapes=[pltpu.VMEM((2, ns, cap, RW), rows.dtype), pltpu.VMEM((2, ns, cap, AW), jnp.float32),
                        pltpu.VMEM((2, ns, cap, D), jnp.float32)],
        compiler_params=pltpu.CompilerParams(
            dimension_semantics=("arbitrary", "arbitrary"), vmem_limit_bytes=VMEM_LIMIT),
        name="moe_experts",
    )(idx.reshape(G // ns, ns, E, cap), rows, aff_rows, w_gate, w_up, w_down)


def _residual_norm_kernel(x_ref, y_ref, gate_ref, gain_ref, o_ref):
    y = y_ref[0]
    yn = y * lax.rsqrt(jnp.mean(y * y, axis=-1, keepdims=True) + RMS_EPS) * gain_ref[...]
    o_ref[0] = x_ref[0] + gate_ref[0] * yn


def residual_norm(x, y, gate, gain, *, tm):
    B, T, D = x.shape
    tile = pl.BlockSpec((1, tm, D), lambda b, i: (b, i, 0))
    return pl.pallas_call(
        _residual_norm_kernel,
        grid=(B, T // tm),
        in_specs=[tile, tile, pl.BlockSpec((1, 1, D), lambda b, i: (b, 0, 0)), _const_spec((1, D))],
        out_specs=tile,
        out_shape=jax.ShapeDtypeStruct((B, T, D), jnp.float32),
        compiler_params=pltpu.CompilerParams(
            dimension_semantics=("arbitrary", "arbitrary"), vmem_limit_bytes=VMEM_LIMIT),
        name="residual_norm",
    )(x, y, gate, gain)


ADA_COLS = 1536


def _ada_kernel(c_ref, w_ref, b_ref, o_ref):
    c = c_ref[...]
    h = (c * _sigmoid(c)).astype(jnp.bfloat16)
    o_ref[...] = _dot_f32(h, w_ref[...].astype(jnp.bfloat16)) + b_ref[...]


def ada_modulation(cond, ada_w, ada_b, layer):
    R, D = cond.shape
    N = ada_w.shape[2]
    return pl.pallas_call(
        _ada_kernel,
        grid=(N // ADA_COLS,),
        in_specs=[pl.BlockSpec((R, D), lambda j: (0, 0)),
                  pl.BlockSpec((None, D, ADA_COLS), lambda j: (layer, 0, j)),
                  pl.BlockSpec((None, 1, ADA_COLS), lambda j: (layer, 0, j))],
        out_specs=pl.BlockSpec((R, ADA_COLS), lambda j: (0, j)),
        out_shape=jax.ShapeDtypeStruct((R, N), jnp.float32),
        compiler_params=pltpu.CompilerParams(dimension_semantics=("arbitrary",), vmem_limit_bytes=VMEM_LIMIT),
        name="ada_modulation",
    )(cond, ada_w, ada_b[:, None, :])


def kernel(x, c, ctx, c_ctx, norm_gain, ada_w, ada_b, w_in, mlstm_gate_bias, pool_w, pool_scale,
           na_rpb, w_branch, w_out, router_w, w_gate, w_up, w_down):
    B, S, D = x.shape
    Tc = ctx.shape[1]
    f32, bf16 = jnp.float32, jnp.bfloat16
    rope_tabs = rope_lane_tables(S)
    names = [n for n, _ in IN_GROUPS] + ["gt"]
    cond = jnp.pad(jnp.concatenate([c, c_ctx[None]], axis=0), ((0, (-(B + 1)) % 8), (0, 0)))
    moe_w = (w_gate.astype(bf16), w_up.astype(bf16), w_down.astype(bf16))
    pending = None
    for l in range(DEPTH):
        need_ctx = l < DEPTH - 1
        g = norm_gain[l]
        mods = ada_modulation(cond, ada_w, ada_b, l)
        mod_l = jnp.split(mods[:B, None, :], N_MOD, axis=-1)
        mod_cb = [jnp.broadcast_to(m[None, None], (B, 1, D)) for m in jnp.split(mods[B], N_MOD, axis=-1)]
        w_perm = permute_w_in(w_in[l])
        gb_row = jnp.pad(mlstm_gate_bias[l].reshape(1, -1).astype(f32), ((0, 0), (0, GATE_W - 4 * M_HEADS)))
        proj = in_proj(x, g[0][None], mod_l[0], mod_l[1], w_perm, gb_row, rope_tabs, pending, tm=512)
        if pending is not None:
            x = proj[-1]
        pl_ = dict(zip(names, proj))
        pc_ = dict(zip(names, in_proj(ctx, g[0][None], mod_cb[0], mod_cb[1], w_perm, gb_row, tm=Tc)))

        m_l, m_c = mlstm(pc_, pl_, need_ctx)
        n_l, n_c = neighbourhood_attention(pl_["qn"], pl_["kn"], pl_["vn"], pc_["kn"], pc_["vn"],
                                           na_bias_table(na_rpb[l]), pc_["qn"] if need_ctx else None)
        gains = jnp.pad(g[1:3], ((0, 6), (0, 0)))
        merge_w = (pool_w[l].astype(bf16), pool_scale[l][None], w_branch[l].astype(bf16),
                   (0.5 * w_out[l]).astype(bf16),
                   router_w[l].T.astype(bf16),
                   jnp.pad(router_w[l], ((0, 0), (0, LANES - N_EXPERTS))).astype(bf16))

        def mods8(mods):
            return jnp.pad(jnp.concatenate([mods[2], mods[3], mods[4]], axis=1), ((0, 0), (0, 5), (0, 0)))

        x1, rows, aff_r, aff_t = merge_out(m_l, n_l, pl_["pu"], pl_["bg"], x, mods8(mod_l), gains, *merge_w, tm=1024)
        y = moe_experts(rows, aff_r, route(aff_t, EC_FACTOR * S // N_EXPERTS), *moe_w, l)
        if need_ctx:
            x, pending = x1, (y, mod_l[5], g[3][None])
        else:
            x = residual_norm(x1, y, mod_l[5], g[3][None], tm=512)
        if need_ctx:
            c1, rows, aff_r, aff_t = merge_out(m_c, n_c, pc_["pu"], pc_["bg"], ctx, mods8(mod_cb), gains, *merge_w, tm=Tc)
            per = S // Tc
            cap_c = EC_FACTOR * Tc // N_EXPERTS
            idx = route(aff_t, cap_c) + (jnp.arange(B, dtype=jnp.int32) % per * Tc)[:, None, None]
            idx = idx.reshape(B // per, per, N_EXPERTS, cap_c).transpose(0, 2, 1, 3).reshape(B // per, N_EXPERTS, per * cap_c)
            y = moe_experts(rows.reshape(B // per, S, D // 2), aff_r.reshape(B // per, S, LANES), idx,
                            *moe_w, l).reshape(B, Tc, D)
            ctx = residual_norm(c1, y, mod_cb[5], g[3][None], tm=Tc)
    return x
```

```python
import functools
import math

import jax
import jax.numpy as jnp
import numpy as np
from jax import lax
from jax.experimental import pallas as pl
from jax.experimental.pallas import tpu as pltpu

D_MODEL = 1024
DEPTH = 2
GRID_W = 64
BRANCH_W = D_MODEL // 2
N_BRANCH = 3
M_HEADS = 4
M_DV = BRANCH_W // M_HEADS
M_DQK = M_DV // 2
M_CHUNK = 128
M_INIT = -1e30
M_ONES = 16
POOL_GROUPS = 4
POOL_GC = BRANCH_W // POOL_GROUPS
POOL_WINDOWS = (2, 4, 8, 16)
NA_HEADS = 8
NA_DH = BRANCH_W // NA_HEADS
NA_KH = 8
NA_KW = 16
N_EXPERTS = 16
EC_FACTOR = 2
ROPE_BASE = 10000.0
RMS_EPS = 1e-6
N_MOD = 6
SPLIT_SIZES = (M_HEADS * M_DQK, M_HEADS * M_DQK, M_HEADS * M_DV, M_HEADS * M_DV, 4 * M_HEADS,
               POOL_GROUPS * POOL_GC, NA_HEADS * NA_DH, NA_HEADS * NA_DH, NA_HEADS * NA_DH,
               N_BRANCH * D_MODEL)
PROJ_W = sum(SPLIT_SIZES)

LANES = 128
VMEM_LIMIT = 56 * 1024 * 1024
MASK_NEG = -1e30
GATE_W = LANES

IN_GROUPS = (("mqk", 2 * M_HEADS * M_DQK), ("mvt", BRANCH_W), ("mo", BRANCH_W), ("pu", BRANCH_W),
             ("qn", BRANCH_W), ("kn", BRANCH_W), ("vn", BRANCH_W), ("bg", N_BRANCH * D_MODEL),
             ("g", GATE_W))
IN_W = sum(w for _, w in IN_GROUPS)
MM_COLS = 512
PACK_WORD = jnp.uint32


def _sigmoid(x):
    return 0.5 * jnp.tanh(0.5 * x) + 0.5


def _const_spec(shape):
    nd = len(shape)
    return pl.BlockSpec(shape, lambda *_: (0,) * nd, pipeline_mode=pl.Buffered(1))


def _rope_tile(x, cos, sin_signed):
    half = M_DQK // 4
    lane = lax.broadcasted_iota(jnp.int32, x.shape, 1)
    partner = jnp.where((lane % (2 * half)) < half, pltpu.roll(x, LANES - half, 1), pltpu.roll(x, half, 1))
    return x * cos + partner * sin_signed


def _in_proj_kernel(x_ref, g_ref, shift_ref, scale_ref, w_ref, gb_ref, *rest, rope, pending):
    if rope:
        cos_ref, sin_ref = rest[:2]
        rest = rest[2:]
    if pending:
        r_ref, rgate_ref, rgain_ref = rest[:3]
        rest, xo_ref = rest[3:-1], rest[-1]
        r = r_ref[0]
        x = x_ref[0] + rgate_ref[0] * (r * lax.rsqrt(jnp.mean(r * r, axis=-1, keepdims=True) + RMS_EPS)
                                       * rgain_ref[...])
        xo_ref[0] = x
    else:
        x = x_ref[0]
    o_refs, gt_ref = rest[:-1], rest[-1]
    y = x * lax.rsqrt(jnp.mean(x * x, axis=-1, keepdims=True) + RMS_EPS)
    h = ((y * g_ref[...]) * (1.0 + scale_ref[0]) + shift_ref[0]).astype(jnp.bfloat16)
    off = 0
    for (name, width), o_ref in zip(IN_GROUPS, o_refs):
        for c0 in range(0, width, MM_COLS):
            cw = min(MM_COLS, width - c0)
            acc = jnp.dot(h, w_ref[:, off + c0:off + c0 + cw], preferred_element_type=jnp.float32)
            if name == "g":
                acc = acc + gb_ref[...]
                for c in range(acc.shape[0] // M_CHUNK):
                    gt_ref[0, c] = acc[c * M_CHUNK:(c + 1) * M_CHUNK, :].T[:4 * M_HEADS, :]
            if name == "mqk" and rope:
                cos, sin = cos_ref[...], sin_ref[...]
                acc = jnp.concatenate([_rope_tile(acc[:, t:t + LANES], cos, sin) for t in range(0, cw, LANES)], axis=-1)
            if name == "mvt":
                for c in range(acc.shape[0] // M_CHUNK):
                    o_ref[0, c] = acc[c * M_CHUNK:(c + 1) * M_CHUNK, :].T.astype(o_ref.dtype)
                continue
            o_ref[0, :, c0:c0 + cw] = acc.astype(o_ref.dtype)
        off += width


def in_proj(x, g, shift, scale, w_perm, gate_bias_row, rope_tabs=None, pending=None, *, tm):
    B, T, D = x.shape
    out_shape = [jax.ShapeDtypeStruct((B, T, w), jnp.float32 if n == "g" else jnp.bfloat16) for n, w in IN_GROUPS]
    out_specs = [pl.BlockSpec((1, tm, w), lambda b, i: (b, i, 0)) for _, w in IN_GROUPS]
    k_mvt = [n for n, _ in IN_GROUPS].index("mvt")
    assert BRANCH_W == MM_COLS and tm % M_CHUNK == 0
    out_shape[k_mvt] = jax.ShapeDtypeStruct((B, T // M_CHUNK, BRANCH_W, M_CHUNK), jnp.bfloat16)
    out_specs[k_mvt] = pl.BlockSpec((1, tm // M_CHUNK, BRANCH_W, M_CHUNK), lambda b, i: (b, i, 0, 0))
    out_shape.append(jax.ShapeDtypeStruct((B, T // M_CHUNK, 4 * M_HEADS, M_CHUNK), jnp.float32))
    out_specs.append(pl.BlockSpec((1, tm // M_CHUNK, 4 * M_HEADS, M_CHUNK), lambda b, i: (b, i, 0, 0)))
    in_specs = [
        pl.BlockSpec((1, tm, D), lambda b, i: (b, i, 0)),
        _const_spec((1, D)),
        pl.BlockSpec((1, 1, D), lambda b, i: (b, 0, 0)),
        pl.BlockSpec((1, 1, D), lambda b, i: (b, 0, 0)),
        _const_spec((D, IN_W)),
        _const_spec((1, GATE_W)),
    ]
    args = [x, g, shift, scale, w_perm, gate_bias_row]
    if rope_tabs is not None:
        in_specs += [pl.BlockSpec((tm, LANES), lambda b, i: (i, 0))] * 2
        args += list(rope_tabs)
    if pending is not None:
        in_specs += [pl.BlockSpec((1, tm, D), lambda b, i: (b, i, 0)), pl.BlockSpec((1, 1, D), lambda b, i: (b, 0, 0)),
                     _const_spec((1, D))]
        args += list(pending)
        out_shape.append(jax.ShapeDtypeStruct((B, T, D), jnp.float32))
        out_specs.append(pl.BlockSpec((1, tm, D), lambda b, i: (b, i, 0)))
    return pl.pallas_call(
        functools.partial(_in_proj_kernel, rope=rope_tabs is not None, pending=pending is not None),
        grid=(B, T // tm),
        in_specs=in_specs,
        out_specs=out_specs,
        out_shape=out_shape,
        compiler_params=pltpu.CompilerParams(
            dimension_semantics=("arbitrary", "arbitrary"), vmem_limit_bytes=VMEM_LIMIT),
        name="in_proj",
    )(*args)


def rope_lane_tables(n_tokens):
    t = jnp.arange(n_tokens)
    row = (t // GRID_W).astype(jnp.float32)
    col = (t % GRID_W).astype(jnp.float32)
    half = M_DQK // 2
    inv = ROPE_BASE ** (-jnp.arange(0, half, 2, dtype=jnp.float32) / half)
    ar = row[:, None] * inv[None, :]
    ac = col[:, None] * inv[None, :]
    cos = jnp.concatenate([jnp.cos(ar), jnp.cos(ar), jnp.cos(ac), jnp.cos(ac)], axis=-1)
    sin = jnp.concatenate([-jnp.sin(ar), jnp.sin(ar), -jnp.sin(ac), jnp.sin(ac)], axis=-1)
    return jnp.tile(cos, (1, LANES // M_DQK)), jnp.tile(sin, (1, LANES // M_DQK))


def permute_w_in(w_in_l):
    qm, km, vm, om, gm, pu, qn, kn, vn, bg = jnp.split(w_in_l, [int(s) for s in np.cumsum(SPLIT_SIZES)[:-1]], axis=-1)
    gpad = jnp.pad(gm, ((0, 0), (0, GATE_W - gm.shape[1])))
    w = jnp.concatenate([qm * (M_DQK ** -0.5), km, vm, om, pu, qn * (NA_DH ** -0.5), kn, vn, bg * 0.5, gpad], axis=-1)
    return w.astype(jnp.bfloat16)


def _pair_scores(qp, k_parts, biases):
    n = qp.shape[0]
    lane_lo = lax.broadcasted_iota(jnp.int32, (n, LANES), 1) < NA_DH
    zero = jnp.zeros_like(qp)
    qq = jnp.concatenate([jnp.where(lane_lo, qp, zero), jnp.where(lane_lo, zero, qp)], axis=0)
    scores = []
    for kp, bias in zip(k_parts, biases):
        s = lax.dot_general(qq, kp, (((1,), (1,)), ((), ())), preferred_element_type=jnp.float32)
        scores.append(s if bias is None else s + bias)
    return scores


def _pair_softmax_pv(scores, v_parts):
    n = scores[0].shape[0] // 2
    lane_lo = lax.broadcasted_iota(jnp.int32, (n, LANES), 1) < NA_DH
    m = scores[0].max(axis=-1, keepdims=True)
    for s in scores[1:]:
        m = jnp.maximum(m, s.max(axis=-1, keepdims=True))
    l = None
    o = None
    for s, vp in zip(scores, v_parts):
        p = jnp.exp(s - m)
        ls = p.sum(axis=-1, keepdims=True)
        os_ = jnp.dot(p.astype(jnp.bfloat16), vp, preferred_element_type=jnp.float32)
        l = ls if l is None else l + ls
        o = os_ if o is None else o + os_
    o = o * (1.0 / l)
    return jnp.where(lane_lo, o[:n], o[n:])


def _na_kernel(q_ref, k_ref, v_ref, kc_ref, vc_ref, bias_ref, *rest, need_ctx):
    if need_ctx:
        qc_ref, o_ref, oc_ref, s_ref = rest
    else:
        o_ref, s_ref = rest
    S = q_ref.shape[1]
    Tc = kc_ref.shape[1]
    rows = S // GRID_W
    n_pairs = BRANCH_W // LANES
    n_loc = NA_KH * GRID_W

    def window(r):
        rs = jnp.clip(r - NA_KH // 2, 0, rows - NA_KH)
        return r - rs, pl.multiple_of(r * GRID_W, GRID_W), pl.multiple_of(rs * GRID_W, GRID_W)

    def scores_stage(r, slot):
        var, q0, k0 = window(r)
        for j in range(n_pairs):
            ls = slice(j * LANES, (j + 1) * LANES)
            s_loc, s_ctx = _pair_scores(q_ref[0, pl.ds(q0, GRID_W), ls],
                                        (k_ref[0, pl.ds(k0, n_loc), ls], kc_ref[0, :, ls]),
                                        (bias_ref[var, j], None))
            s_ref[slot, j, :, :n_loc] = s_loc
            s_ref[slot, j, :, n_loc:] = s_ctx

    def output_stage(r, slot):
        _, q0, k0 = window(r)
        outs = []
        for j in range(n_pairs):
            ls = slice(j * LANES, (j + 1) * LANES)
            outs.append(_pair_softmax_pv((s_ref[slot, j, :, :n_loc], s_ref[slot, j, :, n_loc:]),
                                         (v_ref[0, pl.ds(k0, n_loc), ls], vc_ref[0, :, ls])))
        o_ref[0, pl.ds(q0, GRID_W), :] = jnp.concatenate(outs, axis=-1).astype(o_ref.dtype)

    scores_stage(0, 0)

    def two_rows(i, carry):
        r0 = 2 * i
        scores_stage(r0 + 1, 1)
        output_stage(r0, 0)
        scores_stage(jnp.minimum(r0 + 2, rows - 1), 0)
        output_stage(r0 + 1, 1)
        return carry

    lax.fori_loop(0, rows // 2, two_rows, 0)

    if need_ctx:
        outs = []
        for j in range(n_pairs):
            ls = slice(j * LANES, (j + 1) * LANES)
            scores = _pair_scores(qc_ref[0, :, ls], (kc_ref[0, :, ls],), (None,))
            outs.append(_pair_softmax_pv(scores, (vc_ref[0, :, ls],)))
        oc_ref[0] = jnp.concatenate(outs, axis=-1).astype(oc_ref.dtype)


def na_bias_table(rpb):
    H = rpb.shape[0]
    var = jnp.arange(NA_KH)
    kr = jnp.arange(NA_KH)
    dr = kr[None, :] - var[:, None] + NA_KH - 1
    cols = jnp.arange(GRID_W)
    dc = jnp.clip(cols[None, :] - cols[:, None] + NA_KW - 1, 0, 2 * NA_KW - 2)
    cs = jnp.clip(cols - NA_KW // 2, 0, GRID_W - NA_KW)
    colmask = (cols[None, :] >= cs[:, None]) & (cols[None, :] < cs[:, None] + NA_KW)
    pick_r = jax.nn.one_hot(dr, 2 * NA_KH - 1, dtype=jnp.float32)
    pick_c = jax.nn.one_hot(dc, 2 * NA_KW - 1, dtype=jnp.float32)
    tab = jnp.einsum('vka,hab,qcb->vhqkc', pick_r, rpb.astype(jnp.float32), pick_c, precision=lax.Precision.HIGHEST)
    tab = jnp.where(colmask[None, None, :, None, :], tab, MASK_NEG)
    return tab.reshape(NA_KH, H // 2, 2 * GRID_W, NA_KH * GRID_W)


def neighbourhood_attention(qn, kn, vn, kc, vc, bias_tab, qc=None):
    B, S, W = qn.shape
    Tc = kc.shape[1]
    need_ctx = qc is not None
    lat = pl.BlockSpec((1, S, W), lambda b: (b, 0, 0))
    cx = pl.BlockSpec((1, Tc, W), lambda b: (b, 0, 0))
    in_specs = [lat, lat, lat, cx, cx, _const_spec(bias_tab.shape)]
    args = [qn, kn, vn, kc, vc, bias_tab]
    out_shape = [jax.ShapeDtypeStruct((B, S, W), jnp.bfloat16)]
    out_specs = [lat]
    if need_ctx:
        in_specs.append(cx)
        args.append(qc)
        out_shape.append(jax.ShapeDtypeStruct((B, Tc, W), jnp.bfloat16))
        out_specs.append(cx)
    res = pl.pallas_call(
        functools.partial(_na_kernel, need_ctx=need_ctx),
        grid=(B,),
        in_specs=in_specs,
        out_specs=out_specs,
        out_shape=out_shape,
        scratch_shapes=[pltpu.VMEM((2, W // LANES, 2 * GRID_W, NA_KH * GRID_W + Tc), jnp.float32)],
        compiler_params=pltpu.CompilerParams(dimension_semantics=("arbitrary",), vmem_limit_bytes=VMEM_LIMIT),
        name="neighbourhood_attention",
    )(*args)
    return (res[0], res[1]) if need_ctx else (res[0], None)


def _split_bf16(x):
    hi = x.astype(jnp.bfloat16)
    r1 = x - hi.astype(jnp.float32)
    mid = r1.astype(jnp.bfloat16)
    lo = (r1 - mid.astype(jnp.float32)).astype(jnp.bfloat16)
    return hi, mid, lo


def _dot_f32(a, b):
    return jnp.dot(a, b, preferred_element_type=jnp.float32)


def _log_sigmoid(x):
    return jnp.minimum(x, 0.0) - jnp.log(1.0 + jnp.exp(-jnp.abs(x)))


def _pair_queries(qp):
    lane_lo = lax.broadcasted_iota(jnp.int32, qp.shape, 1) < M_DQK
    zq = jnp.zeros_like(qp)
    return jnp.concatenate([jnp.where(lane_lo, qp, zq), jnp.where(lane_lo, zq, qp)], axis=0)


def _mlstm_prep(qk_ref, g_ref, gt_ref, sa_ref, ra_ref, slot, ci, fwd):
    L = M_CHUNK
    f32, bf16 = jnp.float32, jnp.bfloat16
    d = 0 if fwd else 1
    rows = pl.ds(pl.multiple_of(ci * L, L), L)
    sq_r = lax.broadcasted_iota(jnp.int32, (L, L), 0)
    sq_c = lax.broadcasted_iota(jnp.int32, (L, L), 1)
    tri_l = jnp.where(sq_r >= sq_c, 1.0, 0.0).astype(bf16)
    tri_u = jnp.where(sq_r <= sq_c, 1.0, 0.0).astype(bf16)
    gates = g_ref[0, rows, :]
    gates_t = gt_ref[0, ci]
    lf, lf_t = _log_sigmoid(gates), _log_sigmoid(gates_t)
    cum = sum(_dot_f32(tri_l if fwd else tri_u, p) for p in _split_bf16(lf))
    cum_t = sum(_dot_f32(p, tri_u if fwd else tri_l) for p in _split_bf16(lf_t))
    t_i = 0 if fwd else 2
    st_s = lax.broadcasted_iota(jnp.int32, (L, 2 * L), 0)
    st_t = lax.broadcasted_iota(jnp.int32, (L, 2 * L), 1) & (L - 1)
    valid = (st_s <= st_t) if fwd else (st_s >= st_t)

    def row2(tile, c0, c1):
        return jnp.concatenate([tile[c0:c0 + 1, :], tile[c1:c1 + 1, :]], axis=-1)

    for j in range(M_HEADS // 2):
        ci0, ci1 = t_i * M_HEADS + 2 * j, t_i * M_HEADS + 2 * j + 1
        cf0, cf1 = ci0 + M_HEADS, ci1 + M_HEADS
        b_row, i_row = row2(cum_t, cf0, cf1), row2(gates_t, ci0, ci1)
        colb = jnp.concatenate([jnp.broadcast_to(gates[:, ci0:ci0 + 1] - cum[:, cf0:cf0 + 1], (L, L)),
                                jnp.broadcast_to(gates[:, ci1:ci1 + 1] - cum[:, cf1:cf1 + 1], (L, L))],
                               axis=-1)
        qp = qk_ref[0, rows, j * LANES:(j + 1) * LANES]
        kp = qk_ref[0, rows, M_HEADS * M_DQK + j * LANES:M_HEADS * M_DQK + (j + 1) * LANES]
        sa_ref[slot, d, j, 0] = jnp.where(valid, b_row + colb, MASK_NEG)
        sa_ref[slot, d, j, 1] = lax.dot_general(kp, _pair_queries(qp), (((1,), (1,)), ((), ())),
                                                preferred_element_type=f32)
        ra_ref[slot, d, j, 0:1, :] = b_row
        ra_ref[slot, d, j, 1:2, :] = i_row


def _mlstm_step(qk_ref, vt_ref, sa_ref, ra_ref, slot, h_ref, ct_ref, m_ref, ci, fwd):
    L = M_CHUNK
    f32, bf16 = jnp.float32, jnp.bfloat16
    d = 0 if fwd else 1
    rows = pl.ds(pl.multiple_of(ci * L, L), L)
    last = L - 1 if fwd else 0
    lane_lo = lax.broadcasted_iota(jnp.int32, (L, LANES), 1) < M_DQK
    lane_lo_row = lax.broadcasted_iota(jnp.int32, (1, LANES), 1) < M_DQK
    ones = jnp.ones((M_ONES, L), bf16)

    def halves(row, f):
        return jnp.concatenate([jnp.broadcast_to(f(row[:, :L]), (1, L)), jnp.broadcast_to(f(row[:, L:]), (1, L))],
                               axis=-1)

    for j in range(M_HEADS // 2):
        h0, h1 = 2 * j, 2 * j + 1
        sidx = 2 * j + d
        b_row, i_row = ra_ref[slot, d, j, 0:1, :], ra_ref[slot, d, j, 1:2, :]
        m_row = m_ref[sidx, 0:1, :]
        kp = qk_ref[0, rows, M_HEADS * M_DQK + j * LANES:M_HEADS * M_DQK + (j + 1) * LANES]
        vx0 = jnp.concatenate([vt_ref[0, ci, h0 * M_DV:(h0 + 1) * M_DV, :], ones], axis=0)
        vx1 = jnp.concatenate([vt_ref[0, ci, h1 * M_DV:(h1 + 1) * M_DV, :], ones], axis=0)
        ct = ct_ref[sidx]
        if h_ref is not None:
            dm = sa_ref[slot, d, j, 0]
            qq = _pair_queries(qk_ref[0, rows, j * LANES:(j + 1) * LANES])
            g_row = b_row + m_row
            mt = jnp.maximum(g_row, dm.max(axis=0, keepdims=True))
            sc = (sa_ref[slot, d, j, 1] * jnp.exp(dm - mt)).astype(bf16)
            inter = jnp.exp(g_row - mt)
            t1 = lax.dot_general(ct.astype(bf16), qq, (((1,), (1,)), ((), ())), preferred_element_type=f32)
            t2 = jnp.concatenate([_dot_f32(vx0, sc[:, :L]), _dot_f32(vx1, sc[:, L:])], axis=-1)
            tot = inter * t1 + t2
            h_t = tot[:M_DV] * (1.0 / jnp.maximum(jnp.abs(tot[M_DV:M_DV + 1]), jnp.exp(-mt)))
            h_ref[rows, h0 * M_DV:(h0 + 1) * M_DV] = h_t[:, :L].T
            h_ref[rows, h1 * M_DV:(h1 + 1) * M_DV] = h_t[:, L:].T
        bl = halves(b_row, lambda r: r[:, last:last + 1])
        w = bl - b_row + i_row
        m_new = jnp.maximum(bl + m_row, halves(w, lambda r: r.max(axis=-1, keepdims=True)))
        decay = jnp.exp(bl + m_row - m_new)
        ws = jnp.exp(w - m_new)
        vxs = jnp.concatenate([vx0.astype(f32) * ws[:, :L], vx1.astype(f32) * ws[:, L:]], axis=-1).astype(bf16)
        zk = jnp.zeros_like(kp)
        kk = jnp.concatenate([jnp.where(lane_lo, kp, zk), jnp.where(lane_lo, zk, kp)], axis=0)
        d_cols = jnp.where(lane_lo_row, jnp.broadcast_to(decay[:, 0:1], (1, LANES)),
                           jnp.broadcast_to(decay[:, L:L + 1], (1, LANES)))
        ct_ref[sidx] = d_cols * ct + _dot_f32(vxs, kk)
        m_ref[sidx, 0:1, :] = m_new


def _mlstm_finish(hf_ref, hb_ref, o_ref, out_ref, n_chunks):
    L = M_CHUNK

    def body(ci, carry):
        rows = pl.ds(pl.multiple_of(ci * L, L), L)
        h = hf_ref[rows, :] + hb_ref[rows, :]
        parts = []
        for hd in range(M_HEADS):
            hh = h[:, hd * M_DV:(hd + 1) * M_DV]
            parts.append(hh * lax.rsqrt(jnp.mean(hh * hh, axis=-1, keepdims=True) + RMS_EPS))
        gate = _sigmoid(o_ref[0, rows, :].astype(jnp.float32))
        out_ref[0, rows, :] = (gate * jnp.concatenate(parts, axis=-1)).astype(out_ref.dtype)
        return carry

    lax.fori_loop(0, n_chunks, body, 0)


def _mlstm_kernel(qk_c, v_c, o_c, g_c, gt_c, qk_l, v_l, o_l, g_l, gt_l, *rest, need_ctx):
    if need_ctx:
        out_l, out_c, hf_l, hb_l, hf_c, hb_c, ct_ref, m_ref, sa_ref, ra_ref = rest
    else:
        out_l, hf_l, hb_l, ct_ref, m_ref, sa_ref, ra_ref = rest
        hf_c = hb_c = None
    nc, nl = qk_c.shape[1] // M_CHUNK, qk_l.shape[1] // M_CHUNK
    ct_ref[...] = jnp.zeros_like(ct_ref)
    m_ref[...] = jnp.full_like(m_ref, M_INIT)

    def phase(qk, vt, g, gt, hf, hb, n):
        def prep(s, slot):
            _mlstm_prep(qk, g, gt, sa_ref, ra_ref, slot, s, True)
            _mlstm_prep(qk, g, gt, sa_ref, ra_ref, slot, n - 1 - s, False)

        def step(s, slot):
            _mlstm_step(qk, vt, sa_ref, ra_ref, slot, hf, ct_ref, m_ref, s, True)
            _mlstm_step(qk, vt, sa_ref, ra_ref, slot, hb, ct_ref, m_ref, n - 1 - s, False)

        prep(0, 0)

        def body(i, carry):
            s0 = 2 * i
            prep(s0 + 1, 1)
            step(s0, 0)
            prep(jnp.minimum(s0 + 2, n - 1), 0)
            step(s0 + 1, 1)
            return carry
        lax.fori_loop(0, n // 2, body, 0)

    phase(qk_c, v_c, g_c, gt_c, hf_c, hb_c, nc)
    phase(qk_l, v_l, g_l, gt_l, hf_l, hb_l, nl)
    _mlstm_finish(hf_l, hb_l, o_l, out_l, nl)
    if need_ctx:
        _mlstm_finish(hf_c, hb_c, o_c, out_c, nc)


def mlstm(pc, pl_, need_ctx):
    B, S, W = pl_["mo"].shape
    Tc = pc["mo"].shape[1]
    f32 = jnp.float32

    def specs(T):
        s = pl.BlockSpec((1, T, W), lambda b: (b, 0, 0))
        return [s, pl.BlockSpec((1, T // M_CHUNK, W, M_CHUNK), lambda b: (b, 0, 0, 0)), s,
                pl.BlockSpec((1, T, GATE_W), lambda b: (b, 0, 0)),
                pl.BlockSpec((1, T // M_CHUNK, 4 * M_HEADS, M_CHUNK), lambda b: (b, 0, 0, 0))]

    def args(p):
        return [p["mqk"], p["mvt"], p["mo"], p["g"], p["gt"]]

    out_shape = [jax.ShapeDtypeStruct((B, S, W), jnp.bfloat16)]
    out_specs = [pl.BlockSpec((1, S, W), lambda b: (b, 0, 0))]
    scratch = [pltpu.VMEM((S, W), f32), pltpu.VMEM((S, W), f32)]
    if need_ctx:
        out_shape.append(jax.ShapeDtypeStruct((B, Tc, W), jnp.bfloat16))
        out_specs.append(pl.BlockSpec((1, Tc, W), lambda b: (b, 0, 0)))
        scratch += [pltpu.VMEM((Tc, W), f32), pltpu.VMEM((Tc, W), f32)]
    assert (S // M_CHUNK) % 2 == 0 and (Tc // M_CHUNK) % 2 == 0
    n_pairs = M_HEADS // 2
    scratch += [pltpu.VMEM((M_HEADS, M_DV + M_ONES, LANES), f32), pltpu.VMEM((M_HEADS, 8, 2 * M_CHUNK), f32),
                pltpu.VMEM((2, 2, n_pairs, 2, M_CHUNK, 2 * M_CHUNK), f32),
                pltpu.VMEM((2, 2, n_pairs, 8, 2 * M_CHUNK), f32)]
    res = pl.pallas_call(
        functools.partial(_mlstm_kernel, need_ctx=need_ctx),
        grid=(B,),
        in_specs=specs(Tc) + specs(S),
        out_specs=out_specs,
        out_shape=out_shape,
        scratch_shapes=scratch,
        compiler_params=pltpu.CompilerParams(dimension_semantics=("arbitrary",), vmem_limit_bytes=VMEM_LIMIT),
        name="mlstm",
    )(*args(pc), *args(pl_))
    return (res[0], res[1]) if need_ctx else (res[0], None)


POOL_HALO = 16


def _pool_mix(ext, tok0, pw_ref, ps_ref, seq_len):
    f32 = jnp.float32
    n_ext = ext.shape[0]
    ts = n_ext - 2 * POOL_HALO
    tok = tok0 + lax.broadcasted_iota(jnp.int32, (ts, POOL_GC), 0)

    def shifted(a, d):
        return pltpu.roll(a, (-d) % n_ext, 0)

    outs = []
    for gi, w in enumerate(POOL_WINDOWS):
        s = ext[:, gi * POOL_GC:(gi + 1) * POOL_GC]
        cur = s[POOL_HALO:POOL_HALO + ts]
        s = shifted(s, -1) + s
        span = 2
        while span < w:
            s = shifted(s, -(span // 2)) + shifted(s, span // 2)
            span *= 2
        lo = jnp.clip(tok - w // 2, 0, seq_len - 1)
        hi = jnp.clip(tok - w // 2 + w - 1, 0, seq_len - 1)
        mean = s[POOL_HALO:POOL_HALO + ts] / (hi - lo + 1).astype(f32)
        outs.append(_dot_f32((mean - cur).astype(jnp.bfloat16), pw_ref[gi]))
    return jnp.concatenate(outs, axis=-1) * ps_ref[...]


def _pack_bf16_pairs(h):
    n = h.shape[1] // 2
    return pltpu.pack_elementwise([h[:, :n], h[:, n:]], packed_dtype=jnp.bfloat16)


MERGE_SUB = 256


def _merge_kernel(m_ref, n_ref, pu_ref, prev_ref, next_ref, bg_ref, x_ref, mods_ref, gains_ref,
                  pw_ref, ps_ref, wb_ref, wo_ref, wrt_ref, wrp_ref, x1_ref, h2p_ref, affr_ref, aff_ref,
                  ext_ref, y_ref, *, seq_len):
    f32, bf16 = jnp.float32, jnp.bfloat16
    i = pl.program_id(1)
    tm, D = x_ref.shape[1], x_ref.shape[2]
    E = wrt_ref.shape[0]
    ts = min(MERGE_SUB, tm)
    ext_ref[0:POOL_HALO] = jnp.where(i > 0, prev_ref[0].astype(f32), 0.0)
    ext_ref[POOL_HALO:POOL_HALO + tm] = pu_ref[0].astype(f32)
    ext_ref[POOL_HALO + tm:] = jnp.where(i < pl.num_programs(1) - 1, next_ref[0].astype(f32), 0.0)

    def mix_stage(k):
        rows = slice(k * ts, (k + 1) * ts)
        p = _pool_mix(ext_ref[k * ts:(k + 1) * ts + 2 * POOL_HALO], i * tm + k * ts, pw_ref, ps_ref, seq_len)
        acc = None
        for bi, br in enumerate((m_ref[0, rows], p.astype(bf16), n_ref[0, rows])):
            gate = jnp.tanh(bg_ref[0, rows, bi * D:(bi + 1) * D].astype(f32)) + 1.0
            term = gate * _dot_f32(br, wb_ref[bi])
            acc = term if acc is None else acc + term
        y_ref[k % 2] = _dot_f32(acc.astype(bf16), wo_ref[...])

    def out_stage(k):
        rows = slice(k * ts, (k + 1) * ts)
        y = y_ref[k % 2]
        yn = y * lax.rsqrt(jnp.mean(y * y, axis=-1, keepdims=True) + RMS_EPS) * gains_ref[0:1, :]
        x1 = x_ref[0, rows] + mods_ref[0, 0:1, :] * yn
        x1_ref[0, rows] = x1
        xn = x1 * lax.rsqrt(jnp.mean(x1 * x1, axis=-1, keepdims=True) + RMS_EPS) * gains_ref[1:2, :]
        h2 = xn * (1.0 + mods_ref[0, 2:3, :]) + mods_ref[0, 1:2, :]
        h2b = h2.astype(bf16)
        lg_t = lax.dot_general(wrt_ref[...], h2b, (((1,), (1,)), ((), ())), preferred_element_type=f32)
        e_t = jnp.exp(lg_t - lg_t.max(axis=0, keepdims=True))
        aff_ref[0, :, rows] = e_t / e_t.sum(axis=0, keepdims=True)
        lg = _dot_f32(h2b, wrp_ref[...])
        lg = jnp.where(lax.broadcasted_iota(jnp.int32, lg.shape, 1) < E, lg, MASK_NEG)
        e_r = jnp.exp(lg - lg.max(axis=-1, keepdims=True))
        affr_ref[0, rows] = e_r / e_r.sum(axis=-1, keepdims=True)
        h2p_ref[0, rows] = _pack_bf16_pairs(h2)

    n_sub = tm // ts
    mix_stage(0)
    for k in range(n_sub):
        if k + 1 < n_sub:
            mix_stage(k + 1)
        out_stage(k)


def merge_out(m, n, pu, bg, x, mods, gains, pool_w, pool_scale, w_branch, w_out, w_router_t, w_router_pad, *, tm):
    B, T, D = x.shape
    W = m.shape[2]
    E = w_router_t.shape[0]
    hb = tm // POOL_HALO
    n_halo = T // POOL_HALO
    tile = lambda w: pl.BlockSpec((1, tm, w), lambda b, i: (b, i, 0))
    in_specs = [
        tile(W), tile(W), tile(W),
        pl.BlockSpec((1, POOL_HALO, W), lambda b, i: (b, jnp.maximum(i * hb - 1, 0), 0)),
        pl.BlockSpec((1, POOL_HALO, W), lambda b, i: (b, jnp.minimum((i + 1) * hb, n_halo - 1), 0)),
        tile(N_BRANCH * D), tile(D),
        pl.BlockSpec((1, 8, D), lambda b, i: (b, 0, 0)),
        _const_spec((8, D)),
        _const_spec(pool_w.shape), _const_spec(pool_scale.shape), _const_spec(w_branch.shape),
        _const_spec(w_out.shape), _const_spec(w_router_t.shape), _const_spec(w_router_pad.shape),
    ]
    return pl.pallas_call(
        functools.partial(_merge_kernel, seq_len=T),
        grid=(B, T // tm),
        in_specs=in_specs,
        out_specs=[tile(D), tile(D // 2), tile(LANES), pl.BlockSpec((1, E, tm), lambda b, i: (b, 0, i))],
        out_shape=[jax.ShapeDtypeStruct((B, T, D), jnp.float32),
                   jax.ShapeDtypeStruct((B, T, D // 2), PACK_WORD),
                   jax.ShapeDtypeStruct((B, T, LANES), jnp.float32),
                   jax.ShapeDtypeStruct((B, E, T), jnp.float32)],
        scratch_shapes=[pltpu.VMEM((tm + 2 * POOL_HALO, W), jnp.float32),
                        pltpu.VMEM((2, min(MERGE_SUB, tm), D), jnp.float32)],
        compiler_params=pltpu.CompilerParams(
            dimension_semantics=("arbitrary", "arbitrary"), vmem_limit_bytes=VMEM_LIMIT),
        name="merge_out",
    )(m, n, pu, pu, pu, bg, x, mods, gains, pool_w, pool_scale, w_branch, w_out, w_router_t, w_router_pad)


def _cumsum_chunks(x01):
    R, T = x01.shape
    r = lax.broadcasted_iota(jnp.int32, (LANES, LANES), 0)
    c = lax.broadcasted_iota(jnp.int32, (LANES, LANES), 1)
    tri_u = jnp.where(r <= c, 1.0, 0.0).astype(jnp.bfloat16)
    run = jnp.zeros((R, 1), jnp.float32)
    outs = []
    for t0 in range(0, T, LANES):
        cs = _dot_f32(x01[:, t0:t0 + LANES].astype(jnp.bfloat16), tri_u) + run
        run = cs[:, LANES - 1:LANES]
        outs.append(cs)
    return outs


def _route_kernel(aff_ref, idx_ref, *, cap):
    f32, bf16 = jnp.float32, jnp.bfloat16
    aff = aff_ref[0]
    E, T = aff.shape
    n_chunks = T // LANES

    def count_ge(cand):
        return jnp.sum(jnp.where(aff >= pltpu.bitcast(cand, f32), 1.0, 0.0), axis=-1, keepdims=True)

    top = jnp.full((E, 1), 1 << 30, jnp.int32)
    lo0 = jnp.where(count_ge(top) >= cap, top, jnp.zeros((E, 1), jnp.int32))

    def search(k, lo):
        shift = 28 - 2 * k
        best = lo
        for q in (1, 2, 3):
            cand = lo | (jnp.int32(q) << shift)
            best = jnp.where(count_ge(cand) >= cap, cand, best)
        return best

    thr = pltpu.bitcast(lax.fori_loop(0, 15, search, lo0), f32)
    gt = jnp.where(aff > thr, 1.0, 0.0)
    eq = jnp.where(aff == thr, 1.0, 0.0)
    room = cap - jnp.sum(gt, axis=-1, keepdims=True)
    sel = gt + eq * jnp.where(jnp.concatenate(_cumsum_chunks(eq), axis=-1) <= room, 1.0, 0.0)
    ranks = [jnp.minimum(cs, float(cap)) for cs in _cumsum_chunks(sel)]
    sel_r = lax.broadcasted_iota(jnp.int32, (T, LANES), 0)
    sel_c = lax.broadcasted_iota(jnp.int32, (T, LANES), 1)
    pick_last = jnp.where(sel_r == sel_c * LANES + (LANES - 1), 1.0, 0.0).astype(bf16)
    chunk_end = _dot_f32(jnp.concatenate(ranks, axis=-1).astype(bf16), pick_last)
    slot = lax.broadcasted_iota(jnp.int32, (cap, LANES), 0).astype(f32)
    lane = lax.broadcasted_iota(jnp.int32, (cap, LANES), 1)
    pad = jnp.zeros((LANES - n_chunks, LANES), f32)
    out = jnp.zeros((cap, LANES), f32)
    for e in range(E):
        ends = jnp.where(lane < n_chunks, jnp.broadcast_to(chunk_end[e:e + 1, :], (cap, LANES)), float(cap))
        n_full = jnp.sum(jnp.where(ends <= slot, 1.0, 0.0), axis=-1, keepdims=True)
        chunk_ranks = jnp.concatenate([r[e:e + 1, :] for r in ranks] + [pad], axis=0)
        mine = _dot_f32(jnp.where(lane.astype(f32) == n_full, 1.0, 0.0).astype(bf16), chunk_ranks.astype(bf16))
        pos = LANES * n_full + jnp.sum(jnp.where(mine <= slot, 1.0, 0.0), axis=-1, keepdims=True)
        out = jnp.where(lane == e, jnp.broadcast_to(pos, (cap, LANES)), out)
    idx_ref[0] = out.astype(jnp.int32)


def route(aff_t, cap):
    B, E, T = aff_t.shape
    assert cap <= 256 and cap % 8 == 0 and T % LANES == 0
    idx_t = pl.pallas_call(
        functools.partial(_route_kernel, cap=cap),
        grid=(B,),
        in_specs=[pl.BlockSpec((1, E, T), lambda b: (b, 0, 0))],
        out_specs=pl.BlockSpec((1, cap, LANES), lambda b: (b, 0, 0)),
        out_shape=jax.ShapeDtypeStruct((B, cap, LANES), jnp.int32),
        compiler_params=pltpu.CompilerParams(dimension_semantics=("arbitrary",), vmem_limit_bytes=VMEM_LIMIT),
        name="route",
    )(aff_t)
    return idx_t[:, :, :E].transpose(0, 2, 1)


ROW_GROUP = 8


def _moe_kernel(idx_ref, rows_ref, affr_ref, wg_ref, wu_ref, wd_ref, out_ref, xg_ref, ag_ref, ye_ref):
    f32, bf16 = jnp.float32, jnp.bfloat16
    e = pl.program_id(1)
    n_exp = pl.num_programs(1)
    n_sets, cap = xg_ref.shape[1], xg_ref.shape[2]

    def gather_row(p, s, ee, j):
        i = idx_ref[0, s, ee, j]
        xg_ref[p, s, pl.ds(j, 1), :] = rows_ref[s, pl.ds(i, 1), :]
        ag_ref[p, s, pl.ds(j, 1), :] = affr_ref[s, pl.ds(i, 1), :]

    def scatter_group(p, s, ee, j0):
        ids = [idx_ref[0, s, ee, j0 + r] for r in range(ROW_GROUP)]
        cur = [out_ref[s, pl.ds(i, 1), :] for i in ids]
        for r, i in enumerate(ids):
            out_ref[s, pl.ds(i, 1), :] = cur[r] + ye_ref[p, s, pl.ds(j0 + r, 1), :]

    @pl.when(e == 0)
    def _():
        out_ref[...] = jnp.zeros_like(out_ref)
        ye_ref[1] = jnp.zeros_like(ye_ref[1])

        def body(gi, carry):
            for s in range(n_sets):
                for r in range(ROW_GROUP):
                    gather_row(0, s, 0, gi * ROW_GROUP + r)
            return carry
        lax.fori_loop(0, cap // ROW_GROUP, body, 0)

    def step(p):
        e_prv = jnp.maximum(e - 1, 0)
        e_nxt = jnp.minimum(e + 1, n_exp - 1)
        for s in range(n_sets):
            for j0 in range(0, cap, ROW_GROUP):
                scatter_group(1 - p, s, e_prv, j0)
            for j in range(cap):
                gather_row(1 - p, s, e_nxt, j)
        packed = jnp.concatenate([xg_ref[p, s] for s in range(n_sets)], axis=0)
        xe = jnp.concatenate(
            [pltpu.unpack_elementwise(packed, index=i, packed_dtype=bf16, unpacked_dtype=f32) for i in (0, 1)],
            axis=-1).astype(bf16)
        aff = jnp.concatenate([ag_ref[p, s] for s in range(n_sets)], axis=0)
        gate = jnp.sum(jnp.where(lax.broadcasted_iota(jnp.int32, aff.shape, 1) == e, aff, 0.0),
                       axis=-1, keepdims=True)
        a = _dot_f32(xe, wg_ref[0])
        hid = (a * _sigmoid(a) * _dot_f32(xe, wu_ref[0])).astype(bf16)
        ye = _dot_f32(hid, wd_ref[0]) * gate
        for s in range(n_sets):
            ye_ref[p, s] = ye[s * cap:(s + 1) * cap]

    @pl.when(e % 2 == 0)
    def _():
        step(0)

    @pl.when(e % 2 == 1)
    def _():
        step(1)

    @pl.when(e == n_exp - 1)
    def _():
        def body(gi, carry):
            for s in range(n_sets):
                scatter_group(1, s, e, gi * ROW_GROUP)
            return carry
        lax.fori_loop(0, cap // ROW_GROUP, body, 0)


MOE_SETS = 1


def moe_experts(rows, aff_rows, idx, w_gate, w_up, w_down, layer):
    G, T, RW = rows.shape
    AW = aff_rows.shape[2]
    _, E, cap = idx.shape
    D = w_gate.shape[2]
    ns = MOE_SETS
    assert E % 2 == 0 and cap % ROW_GROUP == 0 and G % ns == 0
    wspec = pl.BlockSpec((None, 1, D, D), lambda g, e: (layer, e, 0, 0))
    return pl.pallas_call(
        _moe_kernel,
        grid=(G // ns, E),
        in_specs=[pl.BlockSpec((1, ns, E, cap), lambda g, e: (g, 0, 0, 0), memory_space=pltpu.SMEM),
                  pl.BlockSpec((ns, T, RW), lambda g, e: (g, 0, 0)),
                  pl.BlockSpec((ns, T, AW), lambda g, e: (g, 0, 0)),
                  wspec, wspec, wspec],
        out_specs=pl.BlockSpec((ns, T, D), lambda g, e: (g, 0, 0)),
        out_shape=jax.ShapeDtypeStruct((G, T, D), jnp.float32),
        scratch_shapes=[pltpu.VMEM((2, ns, cap, RW), rows.dtype), pltpu.VMEM((2, ns, cap, AW), jnp.float32),
                        pltpu.VMEM((2, ns, cap, D), jnp.float32)],
        compiler_params=pltpu.CompilerParams(
            dimension_semantics=("arbitrary", "arbitrary"), vmem_limit_bytes=VMEM_LIMIT),
        name="moe_experts",
    )(idx.reshape(G // ns, ns, E, cap), rows, aff_rows, w_gate, w_up, w_down)


def _residual_norm_kernel(x_ref, y_ref, gate_ref, gain_ref, o_ref):
    y = y_ref[0]
    yn = y * lax.rsqrt(jnp.mean(y * y, axis=-1, keepdims=True) + RMS_EPS) * gain_ref[...]
    o_ref[0] = x_ref[0] + gate_ref[0] * yn


def residual_norm(x, y, gate, gain, *, tm):
    B, T, D = x.shape
    tile = pl.BlockSpec((1, tm, D), lambda b, i: (b, i, 0))
    return pl.pallas_call(
        _residual_norm_kernel,
        grid=(B, T // tm),
        in_specs=[tile, tile, pl.BlockSpec((1, 1, D), lambda b, i: (b, 0, 0)), _const_spec((1, D))],
        out_specs=tile,
        out_shape=jax.ShapeDtypeStruct((B, T, D), jnp.float32),
        compiler_params=pltpu.CompilerParams(
            dimension_semantics=("arbitrary", "arbitrary"), vmem_limit_bytes=VMEM_LIMIT),
        name="residual_norm",
    )(x, y, gate, gain)


ADA_COLS = 1536


def _ada_kernel(c_ref, w_ref, b_ref, o_ref):
    c = c_ref[...]
    h = (c * _sigmoid(c)).astype(jnp.bfloat16)
    o_ref[...] = _dot_f32(h, w_ref[...].astype(jnp.bfloat16)) + b_ref[...]


def ada_modulation(cond, ada_w, ada_b, layer):
    R, D = cond.shape
    N = ada_w.shape[2]
    return pl.pallas_call(
        _ada_kernel,
        grid=(N // ADA_COLS,),
        in_specs=[pl.BlockSpec((R, D), lambda j: (0, 0)),
                  pl.BlockSpec((None, D, ADA_COLS), lambda j: (layer, 0, j)),
                  pl.BlockSpec((None, 1, ADA_COLS), lambda j: (layer, 0, j))],
        out_specs=pl.BlockSpec((R, ADA_COLS), lambda j: (0, j)),
        out_shape=jax.ShapeDtypeStruct((R, N), jnp.float32),
        compiler_params=pltpu.CompilerParams(dimension_semantics=("arbitrary",), vmem_limit_bytes=VMEM_LIMIT),
        name="ada_modulation",
    )(cond, ada_w, ada_b[:, None, :])


def kernel(x, c, ctx, c_ctx, norm_gain, ada_w, ada_b, w_in, mlstm_gate_bias, pool_w, pool_scale,
           na_rpb, w_branch, w_out, router_w, w_gate, w_up, w_down):
    B, S, D = x.shape
    Tc = ctx.shape[1]
    f32, bf16 = jnp.float32, jnp.bfloat16
    rope_tabs = rope_lane_tables(S)
    names = [n for n, _ in IN_GROUPS] + ["gt"]
    cond = jnp.pad(jnp.concatenate([c, c_ctx[None]], axis=0), ((0, (-(B + 1)) % 8), (0, 0)))
    moe_w = (w_gate.astype(bf16), w_up.astype(bf16), w_down.astype(bf16))
    pending = None
    for l in range(DEPTH):
        need_ctx = l < DEPTH - 1
        g = norm_gain[l]
        mods = ada_modulation(cond, ada_w, ada_b, l)
        mod_l = jnp.split(mods[:B, None, :], N_MOD, axis=-1)
        mod_cb = [jnp.broadcast_to(m[None, None], (B, 1, D)) for m in jnp.split(mods[B], N_MOD, axis=-1)]
        w_perm = permute_w_in(w_in[l])
        gb_row = jnp.pad(mlstm_gate_bias[l].reshape(1, -1).astype(f32), ((0, 0), (0, GATE_W - 4 * M_HEADS)))
        proj = in_proj(x, g[0][None], mod_l[0], mod_l[1], w_perm, gb_row, rope_tabs, pending, tm=512)
        if pending is not None:
            x = proj[-1]
        pl_ = dict(zip(names, proj))
        pc_ = dict(zip(names, in_proj(ctx, g[0][None], mod_cb[0], mod_cb[1], w_perm, gb_row, tm=Tc)))

        m_l, m_c = mlstm(pc_, pl_, need_ctx)
        n_l, n_c = neighbourhood_attention(pl_["qn"], pl_["kn"], pl_["vn"], pc_["kn"], pc_["vn"],
                                           na_bias_table(na_rpb[l]), pc_["qn"] if need_ctx else None)
        gains = jnp.pad(g[1:3], ((0, 6), (0, 0)))
        merge_w = (pool_w[l].astype(bf16), pool_scale[l][None], w_branch[l].astype(bf16),
                   (0.5 * w_out[l]).astype(bf16),
                   router_w[l].T.astype(bf16),
                   jnp.pad(router_w[l], ((0, 0), (0, LANES - N_EXPERTS))).astype(bf16))

        def mods8(mods):
            return jnp.pad(jnp.concatenate([mods[2], mods[3], mods[4]], axis=1), ((0, 0), (0, 5), (0, 0)))

        x1, rows, aff_r, aff_t = merge_out(m_l, n_l, pl_["pu"], pl_["bg"], x, mods8(mod_l), gains, *merge_w, tm=1024)
        y = moe_experts(rows, aff_r, route(aff_t, EC_FACTOR * S // N_EXPERTS), *moe_w, l)
        if need_ctx:
            x, pending = x1, (y, mod_l[5], g[3][None])
        else:
            x = residual_norm(x1, y, mod_l[5], g[3][None], tm=512)
        if need_ctx:
            c1, rows, aff_r, aff_t = merge_out(m_c, n_c, pc_["pu"], pc_["bg"], ctx, mods8(mod_cb), gains, *merge_w, tm=Tc)
            per = S // Tc
            cap_c = EC_FACTOR * Tc // N_EXPERTS
            idx = route(aff_t, cap_c) + (jnp.arange(B, dtype=jnp.int32) % per * Tc)[:, None, None]
            idx = idx.reshape(B // per, per, N_EXPERTS, cap_c).transpose(0, 2, 1, 3).reshape(B // per, N_EXPERTS, per * cap_c)
            y = moe_experts(rows.reshape(B // per, S, D // 2), aff_r.reshape(B // per, S, LANES), idx,
                            *moe_w, l).reshape(B, Tc, D)
            ctx = residual_norm(c1, y, mod_cb[5], g[3][None], tm=Tc)
    return x
```

```python
import functools
import math

import jax
import jax.numpy as jnp
import numpy as np
from jax import lax
from jax.experimental import pallas as pl
from jax.experimental.pallas import tpu as pltpu

D_MODEL = 1024
DEPTH = 2
GRID_W = 64
BRANCH_W = D_MODEL // 2
N_BRANCH = 3
M_HEADS = 4
M_DV = BRANCH_W // M_HEADS
M_DQK = M_DV // 2
M_CHUNK = 128
M_INIT = -1e30
M_ONES = 16
POOL_GROUPS = 4
POOL_GC = BRANCH_W // POOL_GROUPS
POOL_WINDOWS = (2, 4, 8, 16)
NA_HEADS = 8
NA_DH = BRANCH_W // NA_HEADS
NA_KH = 8
NA_KW = 16
N_EXPERTS = 16
EC_FACTOR = 2
ROPE_BASE = 10000.0
RMS_EPS = 1e-6
N_MOD = 6
SPLIT_SIZES = (M_HEADS * M_DQK, M_HEADS * M_DQK, M_HEADS * M_DV, M_HEADS * M_DV, 4 * M_HEADS,
               POOL_GROUPS * POOL_GC, NA_HEADS * NA_DH, NA_HEADS * NA_DH, NA_HEADS * NA_DH,
               N_BRANCH * D_MODEL)
PROJ_W = sum(SPLIT_SIZES)

LANES = 128
VMEM_LIMIT = 56 * 1024 * 1024
MASK_NEG = -1e30
GATE_W = LANES

IN_GROUPS = (("mqk", 2 * M_HEADS * M_DQK), ("mvt", BRANCH_W), ("mo", BRANCH_W), ("pu", BRANCH_W),
             ("qn", BRANCH_W), ("kn", BRANCH_W), ("vn", BRANCH_W), ("bg", N_BRANCH * D_MODEL),
             ("g", GATE_W))
IN_W = sum(w for _, w in IN_GROUPS)
MM_COLS = 512
PACK_WORD = jnp.uint32


def _sigmoid(x):
    return 0.5 * jnp.tanh(0.5 * x) + 0.5


def _const_spec(shape):
    nd = len(shape)
    return pl.BlockSpec(shape, lambda *_: (0,) * nd, pipeline_mode=pl.Buffered(1))


def _rope_tile(x, cos, sin_signed):
    half = M_DQK // 4
    lane = lax.broadcasted_iota(jnp.int32, x.shape, 1)
    partner = jnp.where((lane % (2 * half)) < half, pltpu.roll(x, LANES - half, 1), pltpu.roll(x, half, 1))
    return x * cos + partner * sin_signed


def _in_proj_kernel(x_ref, g_ref, shift_ref, scale_ref, w_ref, gb_ref, *rest, rope, pending):
    if rope:
        cos_ref, sin_ref = rest[:2]
        rest = rest[2:]
    if pending:
        r_ref, rgate_ref, rgain_ref = rest[:3]
        rest, xo_ref = rest[3:-1], rest[-1]
        r = r_ref[0]
        x = x_ref[0] + rgate_ref[0] * (r * lax.rsqrt(jnp.mean(r * r, axis=-1, keepdims=True) + RMS_EPS)
                                       * rgain_ref[...])
        xo_ref[0] = x
    else:
        x = x_ref[0]
    o_refs, gt_ref = rest[:-1], rest[-1]
    y = x * lax.rsqrt(jnp.mean(x * x, axis=-1, keepdims=True) + RMS_EPS)
    h = ((y * g_ref[...]) * (1.0 + scale_ref[0]) + shift_ref[0]).astype(jnp.bfloat16)
    off = 0
    for (name, width), o_ref in zip(IN_GROUPS, o_refs):
        for c0 in range(0, width, MM_COLS):
            cw = min(MM_COLS, width - c0)
            acc = jnp.dot(h, w_ref[:, off + c0:off + c0 + cw], preferred_element_type=jnp.float32)
            if name == "g":
                acc = acc + gb_ref[...]
                for c in range(acc.shape[0] // M_CHUNK):
                    gt_ref[0, c] = acc[c * M_CHUNK:(c + 1) * M_CHUNK, :].T[:4 * M_HEADS, :]
            if name == "mqk" and rope:
                cos, sin = cos_ref[...], sin_ref[...]
                acc = jnp.concatenate([_rope_tile(acc[:, t:t + LANES], cos, sin) for t in range(0, cw, LANES)], axis=-1)
            if name == "mvt":
                for c in range(acc.shape[0] // M_CHUNK):
                    o_ref[0, c] = acc[c * M_CHUNK:(c + 1) * M_CHUNK, :].T.astype(o_ref.dtype)
                continue
            o_ref[0, :, c0:c0 + cw] = acc.astype(o_ref.dtype)
        off += width


def in_proj(x, g, shift, scale, w_perm, gate_bias_row, rope_tabs=None, pending=None, *, tm):
    B, T, D = x.shape
    out_shape = [jax.ShapeDtypeStruct((B, T, w), jnp.float32 if n == "g" else jnp.bfloat16) for n, w in IN_GROUPS]
    out_specs = [pl.BlockSpec((1, tm, w), lambda b, i: (b, i, 0)) for _, w in IN_GROUPS]
    k_mvt = [n for n, _ in IN_GROUPS].index("mvt")
    assert BRANCH_W == MM_COLS and tm % M_CHUNK == 0
    out_shape[k_mvt] = jax.ShapeDtypeStruct((B, T // M_CHUNK, BRANCH_W, M_CHUNK), jnp.bfloat16)
    out_specs[k_mvt] = pl.BlockSpec((1, tm // M_CHUNK, BRANCH_W, M_CHUNK), lambda b, i: (b, i, 0, 0))
    out_shape.append(jax.ShapeDtypeStruct((B, T // M_CHUNK, 4 * M_HEADS, M_CHUNK), jnp.float32))
    out_specs.append(pl.BlockSpec((1, tm // M_CHUNK, 4 * M_HEADS, M_CHUNK), lambda b, i: (b, i, 0, 0)))
    in_specs = [
        pl.BlockSpec((1, tm, D), lambda b, i: (b, i, 0)),
        _const_spec((1, D)),
        pl.BlockSpec((1, 1, D), lambda b, i: (b, 0, 0)),
        pl.BlockSpec((1, 1, D), lambda b, i: (b, 0, 0)),
        _const_spec((D, IN_W)),
        _const_spec((1, GATE_W)),
    ]
    args = [x, g, shift, scale, w_perm, gate_bias_row]
    if rope_tabs is not None:
        in_specs += [pl.BlockSpec((tm, LANES), lambda b, i: (i, 0))] * 2
        args += list(rope_tabs)
    if pending is not None:
        in_specs += [pl.BlockSpec((1, tm, D), lambda b, i: (b, i, 0)), pl.BlockSpec((1, 1, D), lambda b, i: (b, 0, 0)),
                     _const_spec((1, D))]
        args += list(pending)
        out_shape.append(jax.ShapeDtypeStruct((B, T, D), jnp.float32))
        out_specs.append(pl.BlockSpec((1, tm, D), lambda b, i: (b, i, 0)))
    return pl.pallas_call(
        functools.partial(_in_proj_kernel, rope=rope_tabs is not None, pending=pending is not None),
        grid=(B, T // tm),
        in_specs=in_specs,
        out_specs=out_specs,
        out_shape=out_shape,
        compiler_params=pltpu.CompilerParams(
            dimension_semantics=("arbitrary", "arbitrary"), vmem_limit_bytes=VMEM_LIMIT),
        name="in_proj",
    )(*args)


def rope_lane_tables(n_tokens):
    t = jnp.arange(n_tokens)
    row = (t // GRID_W).astype(jnp.float32)
    col = (t % GRID_W).astype(jnp.float32)
    half = M_DQK // 2
    inv = ROPE_BASE ** (-jnp.arange(0, half, 2, dtype=jnp.float32) / half)
    ar = row[:, None] * inv[None, :]
    ac = col[:, None] * inv[None, :]
    cos = jnp.concatenate([jnp.cos(ar), jnp.cos(ar), jnp.cos(ac), jnp.cos(ac)], axis=-1)
    sin = jnp.concatenate([-jnp.sin(ar), jnp.sin(ar), -jnp.sin(ac), jnp.sin(ac)], axis=-1)
    return jnp.tile(cos, (1, LANES // M_DQK)), jnp.tile(sin, (1, LANES // M_DQK))


def permute_w_in(w_in_l):
    qm, km, vm, om, gm, pu, qn, kn, vn, bg = jnp.split(w_in_l, [int(s) for s in np.cumsum(SPLIT_SIZES)[:-1]], axis=-1)
    gpad = jnp.pad(gm, ((0, 0), (0, GATE_W - gm.shape[1])))
    w = jnp.concatenate([qm * (M_DQK ** -0.5), km, vm, om, pu, qn * (NA_DH ** -0.5), kn, vn, bg * 0.5, gpad], axis=-1)
    return w.astype(jnp.bfloat16)


def _pair_scores(qp, k_parts, biases):
    n = qp.shape[0]
    lane_lo = lax.broadcasted_iota(jnp.int32, (n, LANES), 1) < NA_DH
    zero = jnp.zeros_like(qp)
    qq = jnp.concatenate([jnp.where(lane_lo, qp, zero), jnp.where(lane_lo, zero, qp)], axis=0)
    scores = []
    for kp, bias in zip(k_parts, biases):
        s = lax.dot_general(qq, kp, (((1,), (1,)), ((), ())), preferred_element_type=jnp.float32)
        scores.append(s if bias is None else s + bias)
    return scores


def _pair_softmax_pv(scores, v_parts):
    n = scores[0].shape[0] // 2
    lane_lo = lax.broadcasted_iota(jnp.int32, (n, LANES), 1) < NA_DH
    m = scores[0].max(axis=-1, keepdims=True)
    for s in scores[1:]:
        m = jnp.maximum(m, s.max(axis=-1, keepdims=True))
    l = None
    o = None
    for s, vp in zip(scores, v_parts):
        p = jnp.exp(s - m)
        ls = p.sum(axis=-1, keepdims=True)
        os_ = jnp.dot(p.astype(jnp.bfloat16), vp, preferred_element_type=jnp.float32)
        l = ls if l is None else l + ls
        o = os_ if o is None else o + os_
    o = o * (1.0 / l)
    return jnp.where(lane_lo, o[:n], o[n:])


def _na_kernel(q_ref, k_ref, v_ref, kc_ref, vc_ref, bias_ref, *rest, need_ctx):
    if need_ctx:
        qc_ref, o_ref, oc_ref, s_ref = rest
    else:
        o_ref, s_ref = rest
    S = q_ref.shape[1]
    Tc = kc_ref.shape[1]
    rows = S // GRID_W
    n_pairs = BRANCH_W // LANES
    n_loc = NA_KH * GRID_W

    def window(r):
        rs = jnp.clip(r - NA_KH // 2, 0, rows - NA_KH)
        return r - rs, pl.multiple_of(r * GRID_W, GRID_W), pl.multiple_of(rs * GRID_W, GRID_W)

    def scores_stage(r, slot, j):
        var, q0, k0 = window(r)
        ls = slice(j * LANES, (j + 1) * LANES)
        s_loc, s_ctx = _pair_scores(q_ref[0, pl.ds(q0, GRID_W), ls],
                                    (k_ref[0, pl.ds(k0, n_loc), ls], kc_ref[0, :, ls]),
                                    (bias_ref[var, j], None))
        s_ref[slot, j, :, :n_loc] = s_loc
        s_ref[slot, j, :, n_loc:] = s_ctx

    def output_stage(r, slot, j):
        _, q0, k0 = window(r)
        ls = slice(j * LANES, (j + 1) * LANES)
        o = _pair_softmax_pv((s_ref[slot, j, :, :n_loc], s_ref[slot, j, :, n_loc:]),
                             (v_ref[0, pl.ds(k0, n_loc), ls], vc_ref[0, :, ls]))
        o_ref[0, pl.ds(q0, GRID_W), ls] = o.astype(o_ref.dtype)

    for j in range(n_pairs):
        scores_stage(0, 0, j)

    def two_rows(i, carry):
        r0 = 2 * i
        for j in range(n_pairs):
            scores_stage(r0 + 1, 1, j)
            output_stage(r0, 0, j)
        for j in range(n_pairs):
            scores_stage(jnp.minimum(r0 + 2, rows - 1), 0, j)
            output_stage(r0 + 1, 1, j)
        return carry

    lax.fori_loop(0, rows // 2, two_rows, 0)

    if need_ctx:
        outs = []
        for j in range(n_pairs):
            ls = slice(j * LANES, (j + 1) * LANES)
            scores = _pair_scores(qc_ref[0, :, ls], (kc_ref[0, :, ls],), (None,))
            outs.append(_pair_softmax_pv(scores, (vc_ref[0, :, ls],)))
        oc_ref[0] = jnp.concatenate(outs, axis=-1).astype(oc_ref.dtype)


def na_bias_table(rpb):
    H = rpb.shape[0]
    var = jnp.arange(NA_KH)
    kr = jnp.arange(NA_KH)
    dr = kr[None, :] - var[:, None] + NA_KH - 1
    cols = jnp.arange(GRID_W)
    dc = jnp.clip(cols[None, :] - cols[:, None] + NA_KW - 1, 0, 2 * NA_KW - 2)
    cs = jnp.clip(cols - NA_KW // 2, 0, GRID_W - NA_KW)
    colmask = (cols[None, :] >= cs[:, None]) & (cols[None, :] < cs[:, None] + NA_KW)
    pick_r = jax.nn.one_hot(dr, 2 * NA_KH - 1, dtype=jnp.float32)
    pick_c = jax.nn.one_hot(dc, 2 * NA_KW - 1, dtype=jnp.float32)
    tab = jnp.einsum('vka,hab,qcb->vhqkc', pick_r, rpb.astype(jnp.float32), pick_c, precision=lax.Precision.HIGHEST)
    tab = jnp.where(colmask[None, None, :, None, :], tab, MASK_NEG)
    return tab.reshape(NA_KH, H // 2, 2 * GRID_W, NA_KH * GRID_W)


def neighbourhood_attention(qn, kn, vn, kc, vc, bias_tab, qc=None):
    B, S, W = qn.shape
    Tc = kc.shape[1]
    need_ctx = qc is not None
    lat = pl.BlockSpec((1, S, W), lambda b: (b, 0, 0))
    cx = pl.BlockSpec((1, Tc, W), lambda b: (b, 0, 0))
    in_specs = [lat, lat, lat, cx, cx, _const_spec(bias_tab.shape)]
    args = [qn, kn, vn, kc, vc, bias_tab]
    out_shape = [jax.ShapeDtypeStruct((B, S, W), jnp.bfloat16)]
    out_specs = [lat]
    if need_ctx:
        in_specs.append(cx)
        args.append(qc)
        out_shape.append(jax.ShapeDtypeStruct((B, Tc, W), jnp.bfloat16))
        out_specs.append(cx)
    res = pl.pallas_call(
        functools.partial(_na_kernel, need_ctx=need_ctx),
        grid=(B,),
        in_specs=in_specs,
        out_specs=out_specs,
        out_shape=out_shape,
        scratch_shapes=[pltpu.VMEM((2, W // LANES, 2 * GRID_W, NA_KH * GRID_W + Tc), jnp.float32)],
        compiler_params=pltpu.CompilerParams(dimension_semantics=("arbitrary",), vmem_limit_bytes=VMEM_LIMIT),
        name="neighbourhood_attention",
    )(*args)
    return (res[0], res[1]) if need_ctx else (res[0], None)


def _split_bf16(x):
    hi = x.astype(jnp.bfloat16)
    r1 = x - hi.astype(jnp.float32)
    mid = r1.astype(jnp.bfloat16)
    lo = (r1 - mid.astype(jnp.float32)).astype(jnp.bfloat16)
    return hi, mid, lo


def _dot_f32(a, b):
    return jnp.dot(a, b, preferred_element_type=jnp.float32)


def _log_sigmoid(x):
    return jnp.minimum(x, 0.0) - jnp.log(1.0 + jnp.exp(-jnp.abs(x)))


def _pair_queries(qp):
    lane_lo = lax.broadcasted_iota(jnp.int32, qp.shape, 1) < M_DQK
    zq = jnp.zeros_like(qp)
    return jnp.concatenate([jnp.where(lane_lo, qp, zq), jnp.where(lane_lo, zq, qp)], axis=0)


def _mlstm_prep(qk_ref, g_ref, gt_ref, sa_ref, ra_ref, slot, ci, fwd):
    L = M_CHUNK
    f32, bf16 = jnp.float32, jnp.bfloat16
    d = 0 if fwd else 1
    rows = pl.ds(pl.multiple_of(ci * L, L), L)
    sq_r = lax.broadcasted_iota(jnp.int32, (L, L), 0)
    sq_c = lax.broadcasted_iota(jnp.int32, (L, L), 1)
    tri_l = jnp.where(sq_r >= sq_c, 1.0, 0.0).astype(bf16)
    tri_u = jnp.where(sq_r <= sq_c, 1.0, 0.0).astype(bf16)
    gates = g_ref[0, rows, :]
    gates_t = gt_ref[0, ci]
    lf, lf_t = _log_sigmoid(gates), _log_sigmoid(gates_t)
    cum = sum(_dot_f32(tri_l if fwd else tri_u, p) for p in _split_bf16(lf))
    cum_t = sum(_dot_f32(p, tri_u if fwd else tri_l) for p in _split_bf16(lf_t))
    t_i = 0 if fwd else 2
    st_s = lax.broadcasted_iota(jnp.int32, (L, 2 * L), 0)
    st_t = lax.broadcasted_iota(jnp.int32, (L, 2 * L), 1) & (L - 1)
    valid = (st_s <= st_t) if fwd else (st_s >= st_t)

    def row2(tile, c0, c1):
        return jnp.concatenate([tile[c0:c0 + 1, :], tile[c1:c1 + 1, :]], axis=-1)

    for j in range(M_HEADS // 2):
        ci0, ci1 = t_i * M_HEADS + 2 * j, t_i * M_HEADS + 2 * j + 1
        cf0, cf1 = ci0 + M_HEADS, ci1 + M_HEADS
        b_row, i_row = row2(cum_t, cf0, cf1), row2(gates_t, ci0, ci1)
        colb = jnp.concatenate([jnp.broadcast_to(gates[:, ci0:ci0 + 1] - cum[:, cf0:cf0 + 1], (L, L)),
                                jnp.broadcast_to(gates[:, ci1:ci1 + 1] - cum[:, cf1:cf1 + 1], (L, L))],
                               axis=-1)
        qp = qk_ref[0, rows, j * LANES:(j + 1) * LANES]
        kp = qk_ref[0, rows, M_HEADS * M_DQK + j * LANES:M_HEADS * M_DQK + (j + 1) * LANES]
        sa_ref[slot, d, j, 0] = jnp.where(valid, b_row + colb, MASK_NEG)
        sa_ref[slot, d, j, 1] = lax.dot_general(kp, _pair_queries(qp), (((1,), (1,)), ((), ())),
                                                preferred_element_type=f32)
        ra_ref[slot, d, j, 0:1, :] = b_row
        ra_ref[slot, d, j, 1:2, :] = i_row


def _mlstm_step(qk_ref, vt_ref, sa_ref, ra_ref, slot, h_ref, ct_ref, m_ref, ci, fwd):
    L = M_CHUNK
    f32, bf16 = jnp.float32, jnp.bfloat16
    d = 0 if fwd else 1
    rows = pl.ds(pl.multiple_of(ci * L, L), L)
    last = L - 1 if fwd else 0
    lane_lo = lax.broadcasted_iota(jnp.int32, (L, LANES), 1) < M_DQK
    lane_lo_row = lax.broadcasted_iota(jnp.int32, (1, LANES), 1) < M_DQK
    ones = jnp.ones((M_ONES, L), bf16)

    def halves(row, f):
        return jnp.concatenate([jnp.broadcast_to(f(row[:, :L]), (1, L)), jnp.broadcast_to(f(row[:, L:]), (1, L))],
                               axis=-1)

    for j in range(M_HEADS // 2):
        h0, h1 = 2 * j, 2 * j + 1
        sidx = 2 * j + d
        b_row, i_row = ra_ref[slot, d, j, 0:1, :], ra_ref[slot, d, j, 1:2, :]
        m_row = m_ref[sidx, 0:1, :]
        kp = qk_ref[0, rows, M_HEADS * M_DQK + j * LANES:M_HEADS * M_DQK + (j + 1) * LANES]
        vx0 = jnp.concatenate([vt_ref[0, ci, h0 * M_DV:(h0 + 1) * M_DV, :], ones], axis=0)
        vx1 = jnp.concatenate([vt_ref[0, ci, h1 * M_DV:(h1 + 1) * M_DV, :], ones], axis=0)
        ct = ct_ref[sidx]
        if h_ref is not None:
            dm = sa_ref[slot, d, j, 0]
            qq = _pair_queries(qk_ref[0, rows, j * LANES:(j + 1) * LANES])
            g_row = b_row + m_row
            mt = jnp.maximum(g_row, dm.max(axis=0, keepdims=True))
            sc = (sa_ref[slot, d, j, 1] * jnp.exp(dm - mt)).astype(bf16)
            inter = jnp.exp(g_row - mt)
            t1 = lax.dot_general(ct.astype(bf16), qq, (((1,), (1,)), ((), ())), preferred_element_type=f32)
            t2 = jnp.concatenate([_dot_f32(vx0, sc[:, :L]), _dot_f32(vx1, sc[:, L:])], axis=-1)
            tot = inter * t1 + t2
            h_t = tot[:M_DV] * (1.0 / jnp.maximum(jnp.abs(tot[M_DV:M_DV + 1]), jnp.exp(-mt)))
            h_ref[rows, h0 * M_DV:(h0 + 1) * M_DV] = h_t[:, :L].T
            h_ref[rows, h1 * M_DV:(h1 + 1) * M_DV] = h_t[:, L:].T
        bl = halves(b_row, lambda r: r[:, last:last + 1])
        w = bl - b_row + i_row
        m_new = jnp.maximum(bl + m_row, halves(w, lambda r: r.max(axis=-1, keepdims=True)))
        decay = jnp.exp(bl + m_row - m_new)
        ws = jnp.exp(w - m_new)
        vxs = jnp.concatenate([vx0.astype(f32) * ws[:, :L], vx1.astype(f32) * ws[:, L:]], axis=-1).astype(bf16)
        zk = jnp.zeros_like(kp)
        kk = jnp.concatenate([jnp.where(lane_lo, kp, zk), jnp.where(lane_lo, zk, kp)], axis=0)
        d_cols = jnp.where(lane_lo_row, jnp.broadcast_to(decay[:, 0:1], (1, LANES)),
                           jnp.broadcast_to(decay[:, L:L + 1], (1, LANES)))
        ct_ref[sidx] = d_cols * ct + _dot_f32(vxs, kk)
        m_ref[sidx, 0:1, :] = m_new


def _mlstm_finish(hf_ref, hb_ref, o_ref, out_ref, n_chunks):
    L = M_CHUNK

    def body(ci, carry):
        rows = pl.ds(pl.multiple_of(ci * L, L), L)
        h = hf_ref[rows, :] + hb_ref[rows, :]
        parts = []
        for hd in range(M_HEADS):
            hh = h[:, hd * M_DV:(hd + 1) * M_DV]
            parts.append(hh * lax.rsqrt(jnp.mean(hh * hh, axis=-1, keepdims=True) + RMS_EPS))
        gate = _sigmoid(o_ref[0, rows, :].astype(jnp.float32))
        out_ref[0, rows, :] = (gate * jnp.concatenate(parts, axis=-1)).astype(out_ref.dtype)
        return carry

    lax.fori_loop(0, n_chunks, body, 0)


def _mlstm_kernel(qk_c, v_c, o_c, g_c, gt_c, qk_l, v_l, o_l, g_l, gt_l, *rest, need_ctx):
    if need_ctx:
        out_l, out_c, hf_l, hb_l, hf_c, hb_c, ct_ref, m_ref, sa_ref, ra_ref = rest
    else:
        out_l, hf_l, hb_l, ct_ref, m_ref, sa_ref, ra_ref = rest
        hf_c = hb_c = None
    nc, nl = qk_c.shape[1] // M_CHUNK, qk_l.shape[1] // M_CHUNK
    ct_ref[...] = jnp.zeros_like(ct_ref)
    m_ref[...] = jnp.full_like(m_ref, M_INIT)

    def phase(qk, vt, g, gt, hf, hb, n):
        def prep(s, slot):
            _mlstm_prep(qk, g, gt, sa_ref, ra_ref, slot, s, True)
            _mlstm_prep(qk, g, gt, sa_ref, ra_ref, slot, n - 1 - s, False)

        def step(s, slot):
            _mlstm_step(qk, vt, sa_ref, ra_ref, slot, hf, ct_ref, m_ref, s, True)
            _mlstm_step(qk, vt, sa_ref, ra_ref, slot, hb, ct_ref, m_ref, n - 1 - s, False)

        prep(0, 0)

        def body(i, carry):
            s0 = 2 * i
            prep(s0 + 1, 1)
            step(s0, 0)
            prep(jnp.minimum(s0 + 2, n - 1), 0)
            step(s0 + 1, 1)
            return carry
        lax.fori_loop(0, n // 2, body, 0)

    phase(qk_c, v_c, g_c, gt_c, hf_c, hb_c, nc)
    phase(qk_l, v_l, g_l, gt_l, hf_l, hb_l, nl)
    _mlstm_finish(hf_l, hb_l, o_l, out_l, nl)
    if need_ctx:
        _mlstm_finish(hf_c, hb_c, o_c, out_c, nc)


def mlstm(pc, pl_, need_ctx):
    B, S, W = pl_["mo"].shape
    Tc = pc["mo"].shape[1]
    f32 = jnp.float32

    def specs(T):
        s = pl.BlockSpec((1, T, W), lambda b: (b, 0, 0))
        return [s, pl.BlockSpec((1, T // M_CHUNK, W, M_CHUNK), lambda b: (b, 0, 0, 0)), s,
                pl.BlockSpec((1, T, GATE_W), lambda b: (b, 0, 0)),
                pl.BlockSpec((1, T // M_CHUNK, 4 * M_HEADS, M_CHUNK), lambda b: (b, 0, 0, 0))]

    def args(p):
        return [p["mqk"], p["mvt"], p["mo"], p["g"], p["gt"]]

    out_shape = [jax.ShapeDtypeStruct((B, S, W), jnp.bfloat16)]
    out_specs = [pl.BlockSpec((1, S, W), lambda b: (b, 0, 0))]
    scratch = [pltpu.VMEM((S, W), f32), pltpu.VMEM((S, W), f32)]
    if need_ctx:
        out_shape.append(jax.ShapeDtypeStruct((B, Tc, W), jnp.bfloat16))
        out_specs.append(pl.BlockSpec((1, Tc, W), lambda b: (b, 0, 0)))
        scratch += [pltpu.VMEM((Tc, W), f32), pltpu.VMEM((Tc, W), f32)]
    assert (S // M_CHUNK) % 2 == 0 and (Tc // M_CHUNK) % 2 == 0
    n_pairs = M_HEADS // 2
    scratch += [pltpu.VMEM((M_HEADS, M_DV + M_ONES, LANES), f32), pltpu.VMEM((M_HEADS, 8, 2 * M_CHUNK), f32),
                pltpu.VMEM((2, 2, n_pairs, 2, M_CHUNK, 2 * M_CHUNK), f32),
                pltpu.VMEM((2, 2, n_pairs, 8, 2 * M_CHUNK), f32)]
    res = pl.pallas_call(
        functools.partial(_mlstm_kernel, need_ctx=need_ctx),
        grid=(B,),
        in_specs=specs(Tc) + specs(S),
        out_specs=out_specs,
        out_shape=out_shape,
        scratch_shapes=scratch,
        compiler_params=pltpu.CompilerParams(dimension_semantics=("arbitrary",), vmem_limit_bytes=VMEM_LIMIT),
        name="mlstm",
    )(*args(pc), *args(pl_))
    return (res[0], res[1]) if need_ctx else (res[0], None)


POOL_HALO = 16


def _pool_mix(ext, tok0, pw_ref, ps_ref, seq_len):
    f32 = jnp.float32
    n_ext = ext.shape[0]
    ts = n_ext - 2 * POOL_HALO
    tok = tok0 + lax.broadcasted_iota(jnp.int32, (ts, POOL_GC), 0)

    def shifted(a, d):
        return pltpu.roll(a, (-d) % n_ext, 0)

    outs = []
    for gi, w in enumerate(POOL_WINDOWS):
        s = ext[:, gi * POOL_GC:(gi + 1) * POOL_GC]
        cur = s[POOL_HALO:POOL_HALO + ts]
        s = shifted(s, -1) + s
        span = 2
        while span < w:
            s = shifted(s, -(span // 2)) + shifted(s, span // 2)
            span *= 2
        lo = jnp.clip(tok - w // 2, 0, seq_len - 1)
        hi = jnp.clip(tok - w // 2 + w - 1, 0, seq_len - 1)
        mean = s[POOL_HALO:POOL_HALO + ts] / (hi - lo + 1).astype(f32)
        outs.append(_dot_f32((mean - cur).astype(jnp.bfloat16), pw_ref[gi]))
    return jnp.concatenate(outs, axis=-1) * ps_ref[...]


def _pack_bf16_pairs(h):
    n = h.shape[1] // 2
    return pltpu.pack_elementwise([h[:, :n], h[:, n:]], packed_dtype=jnp.bfloat16)


MERGE_SUB = 256


def _merge_kernel(m_ref, n_ref, pu_ref, prev_ref, next_ref, bg_ref, x_ref, mods_ref, gains_ref,
                  pw_ref, ps_ref, wb_ref, wo_ref, wrt_ref, wrp_ref, x1_ref, h2p_ref, affr_ref, aff_ref,
                  ext_ref, y_ref, *, seq_len):
    f32, bf16 = jnp.float32, jnp.bfloat16
    i = pl.program_id(1)
    tm, D = x_ref.shape[1], x_ref.shape[2]
    E = wrt_ref.shape[0]
    ts = min(MERGE_SUB, tm)
    ext_ref[0:POOL_HALO] = jnp.where(i > 0, prev_ref[0].astype(f32), 0.0)
    ext_ref[POOL_HALO:POOL_HALO + tm] = pu_ref[0].astype(f32)
    ext_ref[POOL_HALO + tm:] = jnp.where(i < pl.num_programs(1) - 1, next_ref[0].astype(f32), 0.0)

    def mix_stage(k):
        rows = slice(k * ts, (k + 1) * ts)
        p = _pool_mix(ext_ref[k * ts:(k + 1) * ts + 2 * POOL_HALO], i * tm + k * ts, pw_ref, ps_ref, seq_len)
        acc = None
        for bi, br in enumerate((m_ref[0, rows], p.astype(bf16), n_ref[0, rows])):
            gate = jnp.tanh(bg_ref[0, rows, bi * D:(bi + 1) * D].astype(f32)) + 1.0
            term = gate * _dot_f32(br, wb_ref[bi])
            acc = term if acc is None else acc + term
        y_ref[k % 2] = _dot_f32(acc.astype(bf16), wo_ref[...])

    def out_stage(k):
        rows = slice(k * ts, (k + 1) * ts)
        y = y_ref[k % 2]
        yn = y * lax.rsqrt(jnp.mean(y * y, axis=-1, keepdims=True) + RMS_EPS) * gains_ref[0:1, :]
        x1 = x_ref[0, rows] + mods_ref[0, 0:1, :] * yn
        x1_ref[0, rows] = x1
        xn = x1 * lax.rsqrt(jnp.mean(x1 * x1, axis=-1, keepdims=True) + RMS_EPS) * gains_ref[1:2, :]
        h2 = xn * (1.0 + mods_ref[0, 2:3, :]) + mods_ref[0, 1:2, :]
        h2b = h2.astype(bf16)
        lg_t = lax.dot_general(wrt_ref[...], h2b, (((1,), (1,)), ((), ())), preferred_element_type=f32)
        e_t = jnp.exp(lg_t - lg_t.max(axis=0, keepdims=True))
        aff_ref[0, :, rows] = e_t / e_t.sum(axis=0, keepdims=True)
        lg = _dot_f32(h2b, wrp_ref[...])
        lg = jnp.where(lax.broadcasted_iota(jnp.int32, lg.shape, 1) < E, lg, MASK_NEG)
        e_r = jnp.exp(lg - lg.max(axis=-1, keepdims=True))
        affr_ref[0, rows] = e_r / e_r.sum(axis=-1, keepdims=True)
        h2p_ref[0, rows] = _pack_bf16_pairs(h2)

    n_sub = tm // ts
    mix_stage(0)
    for k in range(n_sub):
        if k + 1 < n_sub:
            mix_stage(k + 1)
        out_stage(k)


def merge_out(m, n, pu, bg, x, mods, gains, pool_w, pool_scale, w_branch, w_out, w_router_t, w_router_pad, *, tm):
    B, T, D = x.shape
    W = m.shape[2]
    E = w_router_t.shape[0]
    hb = tm // POOL_HALO
    n_halo = T // POOL_HALO
    tile = lambda w: pl.BlockSpec((1, tm, w), lambda b, i: (b, i, 0))
    in_specs = [
        tile(W), tile(W), tile(W),
        pl.BlockSpec((1, POOL_HALO, W), lambda b, i: (b, jnp.maximum(i * hb - 1, 0), 0)),
        pl.BlockSpec((1, POOL_HALO, W), lambda b, i: (b, jnp.minimum((i + 1) * hb, n_halo - 1), 0)),
        tile(N_BRANCH * D), tile(D),
        pl.BlockSpec((1, 8, D), lambda b, i: (b, 0, 0)),
        _const_spec((8, D)),
        _const_spec(pool_w.shape), _const_spec(pool_scale.shape), _const_spec(w_branch.shape),
        _const_spec(w_out.shape), _const_spec(w_router_t.shape), _const_spec(w_router_pad.shape),
    ]
    return pl.pallas_call(
        functools.partial(_merge_kernel, seq_len=T),
        grid=(B, T // tm),
        in_specs=in_specs,
        out_specs=[tile(D), tile(D // 2), tile(LANES), pl.BlockSpec((1, E, tm), lambda b, i: (b, 0, i))],
        out_shape=[jax.ShapeDtypeStruct((B, T, D), jnp.float32),
                   jax.ShapeDtypeStruct((B, T, D // 2), PACK_WORD),
                   jax.ShapeDtypeStruct((B, T, LANES), jnp.float32),
                   jax.ShapeDtypeStruct((B, E, T), jnp.float32)],
        scratch_shapes=[pltpu.VMEM((tm + 2 * POOL_HALO, W), jnp.float32),
                        pltpu.VMEM((2, min(MERGE_SUB, tm), D), jnp.float32)],
        compiler_params=pltpu.CompilerParams(
            dimension_semantics=("arbitrary", "arbitrary"), vmem_limit_bytes=VMEM_LIMIT),
        name="merge_out",
    )(m, n, pu, pu, pu, bg, x, mods, gains, pool_w, pool_scale, w_branch, w_out, w_router_t, w_router_pad)


def _cumsum_chunks(x01):
    R, T = x01.shape
    r = lax.broadcasted_iota(jnp.int32, (LANES, LANES), 0)
    c = lax.broadcasted_iota(jnp.int32, (LANES, LANES), 1)
    tri_u = jnp.where(r <= c, 1.0, 0.0).astype(jnp.bfloat16)
    run = jnp.zeros((R, 1), jnp.float32)
    outs = []
    for t0 in range(0, T, LANES):
        cs = _dot_f32(x01[:, t0:t0 + LANES].astype(jnp.bfloat16), tri_u) + run
        run = cs[:, LANES - 1:LANES]
        outs.append(cs)
    return outs


def _route_kernel(aff_ref, idx_ref, *, cap):
    f32, bf16 = jnp.float32, jnp.bfloat16
    aff = aff_ref[0]
    E, T = aff.shape
    n_chunks = T // LANES

    def count_ge(cand):
        return jnp.sum(jnp.where(aff >= pltpu.bitcast(cand, f32), 1.0, 0.0), axis=-1, keepdims=True)

    top = jnp.full((E, 1), 1 << 30, jnp.int32)
    lo0 = jnp.where(count_ge(top) >= cap, top, jnp.zeros((E, 1), jnp.int32))

    def search(k, lo):
        shift = 28 - 2 * k
        best = lo
        for q in (1, 2, 3):
            cand = lo | (jnp.int32(q) << shift)
            best = jnp.where(count_ge(cand) >= cap, cand, best)
        return best

    thr = pltpu.bitcast(lax.fori_loop(0, 15, search, lo0), f32)
    gt = jnp.where(aff > thr, 1.0, 0.0)
    eq = jnp.where(aff == thr, 1.0, 0.0)
    room = cap - jnp.sum(gt, axis=-1, keepdims=True)
    sel = gt + eq * jnp.where(jnp.concatenate(_cumsum_chunks(eq), axis=-1) <= room, 1.0, 0.0)
    ranks = [jnp.minimum(cs, float(cap)) for cs in _cumsum_chunks(sel)]
    sel_r = lax.broadcasted_iota(jnp.int32, (T, LANES), 0)
    sel_c = lax.broadcasted_iota(jnp.int32, (T, LANES), 1)
    pick_last = jnp.where(sel_r == sel_c * LANES + (LANES - 1), 1.0, 0.0).astype(bf16)
    chunk_end = _dot_f32(jnp.concatenate(ranks, axis=-1).astype(bf16), pick_last)
    slot = lax.broadcasted_iota(jnp.int32, (cap, LANES), 0).astype(f32)
    lane = lax.broadcasted_iota(jnp.int32, (cap, LANES), 1)
    pad = jnp.zeros((LANES - n_chunks, LANES), f32)
    out = jnp.zeros((cap, LANES), f32)
    for e in range(E):
        ends = jnp.where(lane < n_chunks, jnp.broadcast_to(chunk_end[e:e + 1, :], (cap, LANES)), float(cap))
        n_full = jnp.sum(jnp.where(ends <= slot, 1.0, 0.0), axis=-1, keepdims=True)
        chunk_ranks = jnp.concatenate([r[e:e + 1, :] for r in ranks] + [pad], axis=0)
        mine = _dot_f32(jnp.where(lane.astype(f32) == n_full, 1.0, 0.0).astype(bf16), chunk_ranks.astype(bf16))
        pos = LANES * n_full + jnp.sum(jnp.where(mine <= slot, 1.0, 0.0), axis=-1, keepdims=True)
        out = jnp.where(lane == e, jnp.broadcast_to(pos, (cap, LANES)), out)
    idx_ref[0] = out.astype(jnp.int32)


def route(aff_t, cap):
    B, E, T = aff_t.shape
    assert cap <= 256 and cap % 8 == 0 and T % LANES == 0
    idx_t = pl.pallas_call(
        functools.partial(_route_kernel, cap=cap),
        grid=(B,),
        in_specs=[pl.BlockSpec((1, E, T), lambda b: (b, 0, 0))],
        out_specs=pl.BlockSpec((1, cap, LANES), lambda b: (b, 0, 0)),
        out_shape=jax.ShapeDtypeStruct((B, cap, LANES), jnp.int32),
        compiler_params=pltpu.CompilerParams(dimension_semantics=("arbitrary",), vmem_limit_bytes=VMEM_LIMIT),
        name="route",
    )(aff_t)
    return idx_t[:, :, :E].transpose(0, 2, 1)


ROW_GROUP = 8


def _moe_kernel(idx_ref, rows_ref, affr_ref, wg_ref, wu_ref, wd_ref, out_ref, xg_ref, ag_ref, ye_ref):
    f32, bf16 = jnp.float32, jnp.bfloat16
    e = pl.program_id(1)
    n_exp = pl.num_programs(1)
    n_sets, cap = xg_ref.shape[1], xg_ref.shape[2]

    def gather_row(p, s, ee, j):
        i = idx_ref[0, s, ee, j]
        xg_ref[p, s, pl.ds(j, 1), :] = rows_ref[s, pl.ds(i, 1), :]
        ag_ref[p, s, pl.ds(j, 1), :] = affr_ref[s, pl.ds(i, 1), :]

    def scatter_group(p, s, ee, j0):
        ids = [idx_ref[0, s, ee, j0 + r] for r in range(ROW_GROUP)]
        cur = [out_ref[s, pl.ds(i, 1), :] for i in ids]
        for r, i in enumerate(ids):
            out_ref[s, pl.ds(i, 1), :] = cur[r] + ye_ref[p, s, pl.ds(j0 + r, 1), :]

    @pl.when(e == 0)
    def _():
        out_ref[...] = jnp.zeros_like(out_ref)
        ye_ref[1] = jnp.zeros_like(ye_ref[1])

        def body(gi, carry):
            for s in range(n_sets):
                for r in range(ROW_GROUP):
                    gather_row(0, s, 0, gi * ROW_GROUP + r)
            return carry
        lax.fori_loop(0, cap // ROW_GROUP, body, 0)

    def step(p):
        e_prv = jnp.maximum(e - 1, 0)
        e_nxt = jnp.minimum(e + 1, n_exp - 1)
        for s in range(n_sets):
            for j0 in range(0, cap, ROW_GROUP):
                scatter_group(1 - p, s, e_prv, j0)
            for j in range(cap):
                gather_row(1 - p, s, e_nxt, j)
        packed = jnp.concatenate([xg_ref[p, s] for s in range(n_sets)], axis=0)
        xe = jnp.concatenate(
            [pltpu.unpack_elementwise(packed, index=i, packed_dtype=bf16, unpacked_dtype=f32) for i in (0, 1)],
            axis=-1).astype(bf16)
        aff = jnp.concatenate([ag_ref[p, s] for s in range(n_sets)], axis=0)
        gate = jnp.sum(jnp.where(lax.broadcasted_iota(jnp.int32, aff.shape, 1) == e, aff, 0.0),
                       axis=-1, keepdims=True)
        a = _dot_f32(xe, wg_ref[0])
        hid = (a * _sigmoid(a) * _dot_f32(xe, wu_ref[0])).astype(bf16)
        ye = _dot_f32(hid, wd_ref[0]) * gate
        for s in range(n_sets):
            ye_ref[p, s] = ye[s * cap:(s + 1) * cap]

    @pl.when(e % 2 == 0)
    def _():
        step(0)

    @pl.when(e % 2 == 1)
    def _():
        step(1)

    @pl.when(e == n_exp - 1)
    def _():
        def body(gi, carry):
            for s in range(n_sets):
                scatter_group(1, s, e, gi * ROW_GROUP)
            return carry
        lax.fori_loop(0, cap // ROW_GROUP, body, 0)


MOE_SETS = 1


def moe_experts(rows, aff_rows, idx, w_gate, w_up, w_down, layer):
    G, T, RW = rows.shape
    AW = aff_rows.shape[2]
    _, E, cap = idx.shape
    D = w_gate.shape[2]
    ns = MOE_SETS
    assert E % 2 == 0 and cap % ROW_GROUP == 0 and G % ns == 0
    wspec = pl.BlockSpec((None, 1, D, D), lambda g, e: (layer, e, 0, 0))
    return pl.pallas_call(
        _moe_kernel,
        grid=(G // ns, E),
        in_specs=[pl.BlockSpec((1, ns, E, cap), lambda g, e: (g, 0, 0, 0), memory_space=pltpu.SMEM),
                  pl.BlockSpec((ns, T, RW), lambda g, e: (g, 0, 0)),
                  pl.BlockSpec((ns, T, AW), lambda g, e: (g, 0, 0)),
                  wspec, wspec, wspec],
        out_specs=pl.BlockSpec((ns, T, D), lambda g, e: (g, 0, 0)),
        out_shape=jax.ShapeDtypeStruct((G, T, D), jnp.float32),
        scratch_shapes=[pltpu.VMEM((2, ns, cap, RW), rows.dtype), pltpu.VMEM((2, ns, cap, AW), jnp.float32),
                        pltpu.VMEM((2, ns, cap, D), jnp.float32)],
        compiler_params=pltpu.CompilerParams(
            dimension_semantics=("arbitrary", "arbitrary"), vmem_limit_bytes=VMEM_LIMIT),
        name="moe_experts",
    )(idx.reshape(G // ns, ns, E, cap), rows, aff_rows, w_gate, w_up, w_down)


def _residual_norm_kernel(x_ref, y_ref, gate_ref, gain_ref, o_ref):
    y = y_ref[0]
    yn = y * lax.rsqrt(jnp.mean(y * y, axis=-1, keepdims=True) + RMS_EPS) * gain_ref[...]
    o_ref[0] = x_ref[0] + gate_ref[0] * yn


def residual_norm(x, y, gate, gain, *, tm):
    B, T, D = x.shape
    tile = pl.BlockSpec((1, tm, D), lambda b, i: (b, i, 0))
    return pl.pallas_call(
        _residual_norm_kernel,
        grid=(B, T // tm),
        in_specs=[tile, tile, pl.BlockSpec((1, 1, D), lambda b, i: (b, 0, 0)), _const_spec((1, D))],
        out_specs=tile,
        out_shape=jax.ShapeDtypeStruct((B, T, D), jnp.float32),
        compiler_params=pltpu.CompilerParams(
            dimension_semantics=("arbitrary", "arbitrary"), vmem_limit_bytes=VMEM_LIMIT),
        name="residual_norm",
    )(x, y, gate, gain)


ADA_COLS = 1536


def _ada_kernel(c_ref, w_ref, b_ref, o_ref):
    c = c_ref[...]
    h = (c * _sigmoid(c)).astype(jnp.bfloat16)
    o_ref[...] = _dot_f32(h, w_ref[...].astype(jnp.bfloat16)) + b_ref[...]


def ada_modulation(cond, ada_w, ada_b, layer):
    R, D = cond.shape
    N = ada_w.shape[2]
    return pl.pallas_call(
        _ada_kernel,
        grid=(N // ADA_COLS,),
        in_specs=[pl.BlockSpec((R, D), lambda j: (0, 0)),
                  pl.BlockSpec((None, D, ADA_COLS), lambda j: (layer, 0, j)),
                  pl.BlockSpec((None, 1, ADA_COLS), lambda j: (layer, 0, j))],
        out_specs=pl.BlockSpec((R, ADA_COLS), lambda j: (0, j)),
        out_shape=jax.ShapeDtypeStruct((R, N), jnp.float32),
        compiler_params=pltpu.CompilerParams(dimension_semantics=("arbitrary",), vmem_limit_bytes=VMEM_LIMIT),
        name="ada_modulation",
    )(cond, ada_w, ada_b[:, None, :])


def kernel(x, c, ctx, c_ctx, norm_gain, ada_w, ada_b, w_in, mlstm_gate_bias, pool_w, pool_scale,
           na_rpb, w_branch, w_out, router_w, w_gate, w_up, w_down):
    B, S, D = x.shape
    Tc = ctx.shape[1]
    f32, bf16 = jnp.float32, jnp.bfloat16
    rope_tabs = rope_lane_tables(S)
    names = [n for n, _ in IN_GROUPS] + ["gt"]
    cond = jnp.pad(jnp.concatenate([c, c_ctx[None]], axis=0), ((0, (-(B + 1)) % 8), (0, 0)))
    moe_w = (w_gate.astype(bf16), w_up.astype(bf16), w_down.astype(bf16))
    pending = None
    for l in range(DEPTH):
        need_ctx = l < DEPTH - 1
        g = norm_gain[l]
        mods = ada_modulation(cond, ada_w, ada_b, l)
        mod_l = jnp.split(mods[:B, None, :], N_MOD, axis=-1)
        mod_cb = [jnp.broadcast_to(m[None, None], (B, 1, D)) for m in jnp.split(mods[B], N_MOD, axis=-1)]
        w_perm = permute_w_in(w_in[l])
        gb_row = jnp.pad(mlstm_gate_bias[l].reshape(1, -1).astype(f32), ((0, 0), (0, GATE_W - 4 * M_HEADS)))
        proj = in_proj(x, g[0][None], mod_l[0], mod_l[1], w_perm, gb_row, rope_tabs, pending, tm=512)
        if pending is not None:
            x = proj[-1]
        pl_ = dict(zip(names, proj))
        pc_ = dict(zip(names, in_proj(ctx, g[0][None], mod_cb[0], mod_cb[1], w_perm, gb_row, tm=Tc)))

        m_l, m_c = mlstm(pc_, pl_, need_ctx)
        n_l, n_c = neighbourhood_attention(pl_["qn"], pl_["kn"], pl_["vn"], pc_["kn"], pc_["vn"],
                                           na_bias_table(na_rpb[l]), pc_["qn"] if need_ctx else None)
        gains = jnp.pad(g[1:3], ((0, 6), (0, 0)))
        merge_w = (pool_w[l].astype(bf16), pool_scale[l][None], w_branch[l].astype(bf16),
                   (0.5 * w_out[l]).astype(bf16),
                   router_w[l].T.astype(bf16),
                   jnp.pad(router_w[l], ((0, 0), (0, LANES - N_EXPERTS))).astype(bf16))

        def mods8(mods):
            return jnp.pad(jnp.concatenate([mods[2], mods[3], mods[4]], axis=1), ((0, 0), (0, 5), (0, 0)))

        x1, rows, aff_r, aff_t = merge_out(m_l, n_l, pl_["pu"], pl_["bg"], x, mods8(mod_l), gains, *merge_w, tm=1024)
        y = moe_experts(rows, aff_r, route(aff_t, EC_FACTOR * S // N_EXPERTS), *moe_w, l)
        if need_ctx:
            x, pending = x1, (y, mod_l[5], g[3][None])
        else:
            x = residual_norm(x1, y, mod_l[5], g[3][None], tm=512)
        if need_ctx:
            c1, rows, aff_r, aff_t = merge_out(m_c, n_c, pc_["pu"], pc_["bg"], ctx, mods8(mod_cb), gains, *merge_w, tm=Tc)
            per = S // Tc
            cap_c = EC_FACTOR * Tc // N_EXPERTS
            idx = route(aff_t, cap_c) + (jnp.arange(B, dtype=jnp.int32) % per * Tc)[:, None, None]
            idx = idx.reshape(B // per, per, N_EXPERTS, cap_c).transpose(0, 2, 1, 3).reshape(B // per, N_EXPERTS, per * cap_c)
            y = moe_experts(rows.reshape(B // per, S, D // 2), aff_r.reshape(B // per, S, LANES), idx,
                            *moe_w, l).reshape(B, Tc, D)
            ctx = residual_norm(c1, y, mod_cb[5], g[3][None], tm=Tc)
    return x
```

```python
import functools
import math

import jax
import jax.numpy as jnp
import numpy as np
from jax import lax
from jax.experimental import pallas as pl
from jax.experimental.pallas import tpu as pltpu

D_MODEL = 1024
DEPTH = 2
GRID_W = 64
BRANCH_W = D_MODEL // 2
N_BRANCH = 3
M_HEADS = 4
M_DV = BRANCH_W // M_HEADS
M_DQK = M_DV // 2
M_CHUNK = 128
M_INIT = -1e30
M_ONES = 16
POOL_GROUPS = 4
POOL_GC = BRANCH_W // POOL_GROUPS
POOL_WINDOWS = (2, 4, 8, 16)
NA_HEADS = 8
NA_DH = BRANCH_W // NA_HEADS
NA_KH = 8
NA_KW = 16
N_EXPERTS = 16
EC_FACTOR = 2
ROPE_BASE = 10000.0
RMS_EPS = 1e-6
N_MOD = 6
SPLIT_SIZES = (M_HEADS * M_DQK, M_HEADS * M_DQK, M_HEADS * M_DV, M_HEADS * M_DV, 4 * M_HEADS,
               POOL_GROUPS * POOL_GC, NA_HEADS * NA_DH, NA_HEADS * NA_DH, NA_HEADS * NA_DH,
               N_BRANCH * D_MODEL)
PROJ_W = sum(SPLIT_SIZES)

LANES = 128
VMEM_LIMIT = 56 * 1024 * 1024
MASK_NEG = -1e30
GATE_W = LANES

IN_GROUPS = (("mqk", 2 * M_HEADS * M_DQK), ("mvt", BRANCH_W), ("mo", BRANCH_W), ("pu", BRANCH_W),
             ("qn", BRANCH_W), ("kn", BRANCH_W), ("vn", BRANCH_W), ("bg", N_BRANCH * D_MODEL),
             ("g", GATE_W))
IN_W = sum(w for _, w in IN_GROUPS)
MM_COLS = 512
PACK_WORD = jnp.uint32


def _sigmoid(x):
    return 0.5 * jnp.tanh(0.5 * x) + 0.5


def _const_spec(shape):
    nd = len(shape)
    return pl.BlockSpec(shape, lambda *_: (0,) * nd, pipeline_mode=pl.Buffered(1))


def _rope_tile(x, cos, sin_signed):
    half = M_DQK // 4
    lane = lax.broadcasted_iota(jnp.int32, x.shape, 1)
    partner = jnp.where((lane % (2 * half)) < half, pltpu.roll(x, LANES - half, 1), pltpu.roll(x, half, 1))
    return x * cos + partner * sin_signed


def _in_proj_kernel(x_ref, g_ref, shift_ref, scale_ref, w_ref, gb_ref, *rest, rope, pending, keep):
    if rope:
        cos_ref, sin_ref = rest[:2]
        rest = rest[2:]
    if pending:
        r_ref, rgate_ref, rgain_ref = rest[:3]
        rest, xo_ref = rest[3:-1], rest[-1]
        r = r_ref[0]
        x = x_ref[0] + rgate_ref[0] * (r * lax.rsqrt(jnp.mean(r * r, axis=-1, keepdims=True) + RMS_EPS)
                                       * rgain_ref[...])
        xo_ref[0] = x
    else:
        x = x_ref[0]
    o_refs, gt_ref = rest[:-1], rest[-1]
    y = x * lax.rsqrt(jnp.mean(x * x, axis=-1, keepdims=True) + RMS_EPS)
    h = ((y * g_ref[...]) * (1.0 + scale_ref[0]) + shift_ref[0]).astype(jnp.bfloat16)
    off = 0
    o_iter = iter(o_refs)
    for name, width in IN_GROUPS:
        if name not in keep:
            off += width
            continue
        o_ref = next(o_iter)
        for c0 in range(0, width, MM_COLS):
            cw = min(MM_COLS, width - c0)
            acc = jnp.dot(h, w_ref[:, off + c0:off + c0 + cw], preferred_element_type=jnp.float32)
            if name == "g":
                acc = acc + gb_ref[...]
                for c in range(acc.shape[0] // M_CHUNK):
                    gt_ref[0, c] = acc[c * M_CHUNK:(c + 1) * M_CHUNK, :].T[:4 * M_HEADS, :]
            if name == "mqk" and rope:
                cos, sin = cos_ref[...], sin_ref[...]
                acc = jnp.concatenate([_rope_tile(acc[:, t:t + LANES], cos, sin) for t in range(0, cw, LANES)], axis=-1)
            if name == "mvt":
                for c in range(acc.shape[0] // M_CHUNK):
                    o_ref[0, c] = acc[c * M_CHUNK:(c + 1) * M_CHUNK, :].T.astype(o_ref.dtype)
                continue
            o_ref[0, :, c0:c0 + cw] = acc.astype(o_ref.dtype)
        off += width


def in_proj(x, g, shift, scale, w_perm, gate_bias_row, rope_tabs=None, pending=None, keep=None, *, tm):
    B, T, D = x.shape
    keep = tuple(n for n, _ in IN_GROUPS) if keep is None else tuple(keep)
    assert "g" in keep
    out_shape = [jax.ShapeDtypeStruct((B, T, w), jnp.float32 if n == "g" else jnp.bfloat16) for n, w in IN_GROUPS]
    out_specs = [pl.BlockSpec((1, tm, w), lambda b, i: (b, i, 0)) for _, w in IN_GROUPS]
    k_mvt = [n for n, _ in IN_GROUPS].index("mvt")
    assert BRANCH_W == MM_COLS and tm % M_CHUNK == 0
    out_shape[k_mvt] = jax.ShapeDtypeStruct((B, T // M_CHUNK, BRANCH_W, M_CHUNK), jnp.bfloat16)
    out_specs[k_mvt] = pl.BlockSpec((1, tm // M_CHUNK, BRANCH_W, M_CHUNK), lambda b, i: (b, i, 0, 0))
    kept = [k for k, (n, _) in enumerate(IN_GROUPS) if n in keep]
    out_shape, out_specs = [out_shape[k] for k in kept], [out_specs[k] for k in kept]
    out_shape.append(jax.ShapeDtypeStruct((B, T // M_CHUNK, 4 * M_HEADS, M_CHUNK), jnp.float32))
    out_specs.append(pl.BlockSpec((1, tm // M_CHUNK, 4 * M_HEADS, M_CHUNK), lambda b, i: (b, i, 0, 0)))
    in_specs = [
        pl.BlockSpec((1, tm, D), lambda b, i: (b, i, 0)),
        _const_spec((1, D)),
        pl.BlockSpec((1, 1, D), lambda b, i: (b, 0, 0)),
        pl.BlockSpec((1, 1, D), lambda b, i: (b, 0, 0)),
        _const_spec((D, IN_W)),
        _const_spec((1, GATE_W)),
    ]
    args = [x, g, shift, scale, w_perm, gate_bias_row]
    if rope_tabs is not None:
        in_specs += [pl.BlockSpec((tm, LANES), lambda b, i: (i, 0))] * 2
        args += list(rope_tabs)
    if pending is not None:
        in_specs += [pl.BlockSpec((1, tm, D), lambda b, i: (b, i, 0)), pl.BlockSpec((1, 1, D), lambda b, i: (b, 0, 0)),
                     _const_spec((1, D))]
        args += list(pending)
        out_shape.append(jax.ShapeDtypeStruct((B, T, D), jnp.float32))
        out_specs.append(pl.BlockSpec((1, tm, D), lambda b, i: (b, i, 0)))
    return pl.pallas_call(
        functools.partial(_in_proj_kernel, rope=rope_tabs is not None, pending=pending is not None, keep=keep),
        grid=(B, T // tm),
        in_specs=in_specs,
        out_specs=out_specs,
        out_shape=out_shape,
        compiler_params=pltpu.CompilerParams(
            dimension_semantics=("arbitrary", "arbitrary"), vmem_limit_bytes=VMEM_LIMIT),
        name="in_proj",
    )(*args)


def rope_lane_tables(n_tokens):
    t = jnp.arange(n_tokens)
    row = (t // GRID_W).astype(jnp.float32)
    col = (t % GRID_W).astype(jnp.float32)
    half = M_DQK // 2
    inv = ROPE_BASE ** (-jnp.arange(0, half, 2, dtype=jnp.float32) / half)
    ar = row[:, None] * inv[None, :]
    ac = col[:, None] * inv[None, :]
    cos = jnp.concatenate([jnp.cos(ar), jnp.cos(ar), jnp.cos(ac), jnp.cos(ac)], axis=-1)
    sin = jnp.concatenate([-jnp.sin(ar), jnp.sin(ar), -jnp.sin(ac), jnp.sin(ac)], axis=-1)
    return jnp.tile(cos, (1, LANES // M_DQK)), jnp.tile(sin, (1, LANES // M_DQK))


def permute_w_in(w_in_l):
    qm, km, vm, om, gm, pu, qn, kn, vn, bg = jnp.split(w_in_l, [int(s) for s in np.cumsum(SPLIT_SIZES)[:-1]], axis=-1)
    gpad = jnp.pad(gm, ((0, 0), (0, GATE_W - gm.shape[1])))
    w = jnp.concatenate([qm * (M_DQK ** -0.5), km, vm, om, pu, qn * (NA_DH ** -0.5), kn, vn, bg * 0.5, gpad], axis=-1)
    return w.astype(jnp.bfloat16)


def _pair_scores(qp, k_parts, biases):
    n = qp.shape[0]
    lane_lo = lax.broadcasted_iota(jnp.int32, (n, LANES), 1) < NA_DH
    zero = jnp.zeros_like(qp)
    qq = jnp.concatenate([jnp.where(lane_lo, qp, zero), jnp.where(lane_lo, zero, qp)], axis=0)
    scores = []
    for kp, bias in zip(k_parts, biases):
        s = lax.dot_general(qq, kp, (((1,), (1,)), ((), ())), preferred_element_type=jnp.float32)
        scores.append(s if bias is None else s + bias)
    return scores


def _pair_softmax_pv(scores, v_parts):
    n = scores[0].shape[0] // 2
    lane_lo = lax.broadcasted_iota(jnp.int32, (n, LANES), 1) < NA_DH
    m = scores[0].max(axis=-1, keepdims=True)
    for s in scores[1:]:
        m = jnp.maximum(m, s.max(axis=-1, keepdims=True))
    l = None
    o = None
    for s, vp in zip(scores, v_parts):
        p = jnp.exp(s - m)
        ls = p.sum(axis=-1, keepdims=True)
        os_ = jnp.dot(p.astype(jnp.bfloat16), vp, preferred_element_type=jnp.float32)
        l = ls if l is None else l + ls
        o = os_ if o is None else o + os_
    o = o * (1.0 / l)
    return jnp.where(lane_lo, o[:n], o[n:])


def _na_kernel(q_ref, k_ref, v_ref, kc_ref, vc_ref, bias_ref, *rest, need_ctx):
    if need_ctx:
        qc_ref, o_ref, oc_ref, s_ref = rest
    else:
        o_ref, s_ref = rest
    S = q_ref.shape[1]
    Tc = kc_ref.shape[1]
    rows = S // GRID_W
    n_pairs = BRANCH_W // LANES
    n_loc = NA_KH * GRID_W

    def window(r):
        rs = jnp.clip(r - NA_KH // 2, 0, rows - NA_KH)
        return r - rs, pl.multiple_of(r * GRID_W, GRID_W), pl.multiple_of(rs * GRID_W, GRID_W)

    def scores_stage(r, slot, j):
        var, q0, k0 = window(r)
        ls = slice(j * LANES, (j + 1) * LANES)
        s_loc, s_ctx = _pair_scores(q_ref[0, pl.ds(q0, GRID_W), ls],
                                    (k_ref[0, pl.ds(k0, n_loc), ls], kc_ref[0, :, ls]),
                                    (bias_ref[var, j], None))
        s_ref[slot, j, :, :n_loc] = s_loc
        s_ref[slot, j, :, n_loc:] = s_ctx

    def output_stage(r, slot, j):
        _, q0, k0 = window(r)
        ls = slice(j * LANES, (j + 1) * LANES)
        o = _pair_softmax_pv((s_ref[slot, j, :, :n_loc], s_ref[slot, j, :, n_loc:]),
                             (v_ref[0, pl.ds(k0, n_loc), ls], vc_ref[0, :, ls]))
        o_ref[0, pl.ds(q0, GRID_W), ls] = o.astype(o_ref.dtype)

    for j in range(n_pairs):
        scores_stage(0, 0, j)

    def two_rows(i, carry):
        r0 = 2 * i
        for j in range(n_pairs):
            scores_stage(r0 + 1, 1, j)
            output_stage(r0, 0, j)
        for j in range(n_pairs):
            scores_stage(jnp.minimum(r0 + 2, rows - 1), 0, j)
            output_stage(r0 + 1, 1, j)
        return carry

    lax.fori_loop(0, rows // 2, two_rows, 0)

    if need_ctx:
        outs = []
        for j in range(n_pairs):
            ls = slice(j * LANES, (j + 1) * LANES)
            scores = _pair_scores(qc_ref[0, :, ls], (kc_ref[0, :, ls],), (None,))
            outs.append(_pair_softmax_pv(scores, (vc_ref[0, :, ls],)))
        oc_ref[0] = jnp.concatenate(outs, axis=-1).astype(oc_ref.dtype)


def na_bias_table(rpb):
    H = rpb.shape[0]
    var = jnp.arange(NA_KH)
    kr = jnp.arange(NA_KH)
    dr = kr[None, :] - var[:, None] + NA_KH - 1
    cols = jnp.arange(GRID_W)
    dc = jnp.clip(cols[None, :] - cols[:, None] + NA_KW - 1, 0, 2 * NA_KW - 2)
    cs = jnp.clip(cols - NA_KW // 2, 0, GRID_W - NA_KW)
    colmask = (cols[None, :] >= cs[:, None]) & (cols[None, :] < cs[:, None] + NA_KW)
    pick_r = jax.nn.one_hot(dr, 2 * NA_KH - 1, dtype=jnp.float32)
    pick_c = jax.nn.one_hot(dc, 2 * NA_KW - 1, dtype=jnp.float32)
    tab = jnp.einsum('vka,hab,qcb->vhqkc', pick_r, rpb.astype(jnp.float32), pick_c, precision=lax.Precision.HIGHEST)
    tab = jnp.where(colmask[None, None, :, None, :], tab, MASK_NEG)
    return tab.reshape(NA_KH, H // 2, 2 * GRID_W, NA_KH * GRID_W)


def neighbourhood_attention(qn, kn, vn, kc, vc, bias_tab, qc=None):
    B, S, W = qn.shape
    Tc = kc.shape[1]
    need_ctx = qc is not None
    lat = pl.BlockSpec((1, S, W), lambda b: (b, 0, 0))
    cx = pl.BlockSpec((1, Tc, W), lambda b: (b, 0, 0))
    in_specs = [lat, lat, lat, cx, cx, _const_spec(bias_tab.shape)]
    args = [qn, kn, vn, kc, vc, bias_tab]
    out_shape = [jax.ShapeDtypeStruct((B, S, W), jnp.bfloat16)]
    out_specs = [lat]
    if need_ctx:
        in_specs.append(cx)
        args.append(qc)
        out_shape.append(jax.ShapeDtypeStruct((B, Tc, W), jnp.bfloat16))
        out_specs.append(cx)
    res = pl.pallas_call(
        functools.partial(_na_kernel, need_ctx=need_ctx),
        grid=(B,),
        in_specs=in_specs,
        out_specs=out_specs,
        out_shape=out_shape,
        scratch_shapes=[pltpu.VMEM((2, W // LANES, 2 * GRID_W, NA_KH * GRID_W + Tc), jnp.float32)],
        compiler_params=pltpu.CompilerParams(dimension_semantics=("arbitrary",), vmem_limit_bytes=VMEM_LIMIT),
        name="neighbourhood_attention",
    )(*args)
    return (res[0], res[1]) if need_ctx else (res[0], None)


def _split_bf16(x):
    hi = x.astype(jnp.bfloat16)
    r1 = x - hi.astype(jnp.float32)
    mid = r1.astype(jnp.bfloat16)
    lo = (r1 - mid.astype(jnp.float32)).astype(jnp.bfloat16)
    return hi, mid, lo


def _dot_f32(a, b):
    return jnp.dot(a, b, preferred_element_type=jnp.float32)


def _log_sigmoid(x):
    return jnp.minimum(x, 0.0) - jnp.log(1.0 + jnp.exp(-jnp.abs(x)))


def _pair_queries(qp):
    lane_lo = lax.broadcasted_iota(jnp.int32, qp.shape, 1) < M_DQK
    zq = jnp.zeros_like(qp)
    return jnp.concatenate([jnp.where(lane_lo, qp, zq), jnp.where(lane_lo, zq, qp)], axis=0)


def _mlstm_prep(qk_ref, g_ref, gt_ref, sa_ref, ra_ref, slot, ci, fwd):
    L = M_CHUNK
    f32, bf16 = jnp.float32, jnp.bfloat16
    d = 0 if fwd else 1
    rows = pl.ds(pl.multiple_of(ci * L, L), L)
    sq_r = lax.broadcasted_iota(jnp.int32, (L, L), 0)
    sq_c = lax.broadcasted_iota(jnp.int32, (L, L), 1)
    tri_l = jnp.where(sq_r >= sq_c, 1.0, 0.0).astype(bf16)
    tri_u = jnp.where(sq_r <= sq_c, 1.0, 0.0).astype(bf16)
    gates = g_ref[0, rows, :]
    gates_t = gt_ref[0, ci]
    lf, lf_t = _log_sigmoid(gates), _log_sigmoid(gates_t)
    cum = sum(_dot_f32(tri_l if fwd else tri_u, p) for p in _split_bf16(lf))
    cum_t = sum(_dot_f32(p, tri_u if fwd else tri_l) for p in _split_bf16(lf_t))
    t_i = 0 if fwd else 2
    st_s = lax.broadcasted_iota(jnp.int32, (L, 2 * L), 0)
    st_t = lax.broadcasted_iota(jnp.int32, (L, 2 * L), 1) & (L - 1)
    valid = (st_s <= st_t) if fwd else (st_s >= st_t)

    def row2(tile, c0, c1):
        return jnp.concatenate([tile[c0:c0 + 1, :], tile[c1:c1 + 1, :]], axis=-1)

    for j in range(M_HEADS // 2):
        ci0, ci1 = t_i * M_HEADS + 2 * j, t_i * M_HEADS + 2 * j + 1
        cf0, cf1 = ci0 + M_HEADS, ci1 + M_HEADS
        b_row, i_row = row2(cum_t, cf0, cf1), row2(gates_t, ci0, ci1)
        colb = jnp.concatenate([jnp.broadcast_to(gates[:, ci0:ci0 + 1] - cum[:, cf0:cf0 + 1], (L, L)),
                                jnp.broadcast_to(gates[:, ci1:ci1 + 1] - cum[:, cf1:cf1 + 1], (L, L))],
                               axis=-1)
        qp = qk_ref[0, rows, j * LANES:(j + 1) * LANES]
        kp = qk_ref[0, rows, M_HEADS * M_DQK + j * LANES:M_HEADS * M_DQK + (j + 1) * LANES]
        sa_ref[slot, d, j, 0] = jnp.where(valid, b_row + colb, MASK_NEG)
        sa_ref[slot, d, j, 1] = lax.dot_general(kp, _pair_queries(qp), (((1,), (1,)), ((), ())),
                                                preferred_element_type=f32)
        ra_ref[slot, d, j, 0:1, :] = b_row
        ra_ref[slot, d, j, 1:2, :] = i_row


def _mlstm_step(qk_ref, vt_ref, sa_ref, ra_ref, slot, h_ref, ct_ref, m_ref, ci, fwd):
    L = M_CHUNK
    f32, bf16 = jnp.float32, jnp.bfloat16
    d = 0 if fwd else 1
    rows = pl.ds(pl.multiple_of(ci * L, L), L)
    last = L - 1 if fwd else 0
    lane_lo = lax.broadcasted_iota(jnp.int32, (L, LANES), 1) < M_DQK
    lane_lo_row = lax.broadcasted_iota(jnp.int32, (1, LANES), 1) < M_DQK
    ones = jnp.ones((M_ONES, L), bf16)

    def halves(row, f):
        return jnp.concatenate([jnp.broadcast_to(f(row[:, :L]), (1, L)), jnp.broadcast_to(f(row[:, L:]), (1, L))],
                               axis=-1)

    for j in range(M_HEADS // 2):
        h0, h1 = 2 * j, 2 * j + 1
        sidx = 2 * j + d
        b_row, i_row = ra_ref[slot, d, j, 0:1, :], ra_ref[slot, d, j, 1:2, :]
        m_row = m_ref[sidx, 0:1, :]
        kp = qk_ref[0, rows, M_HEADS * M_DQK + j * LANES:M_HEADS * M_DQK + (j + 1) * LANES]
        vx0 = jnp.concatenate([vt_ref[0, ci, h0 * M_DV:(h0 + 1) * M_DV, :], ones], axis=0)
        vx1 = jnp.concatenate([vt_ref[0, ci, h1 * M_DV:(h1 + 1) * M_DV, :], ones], axis=0)
        ct = ct_ref[sidx]
        if h_ref is not None:
            dm = sa_ref[slot, d, j, 0]
            qq = _pair_queries(qk_ref[0, rows, j * LANES:(j + 1) * LANES])
            g_row = b_row + m_row
            mt = jnp.maximum(g_row, dm.max(axis=0, keepdims=True))
            sc = (sa_ref[slot, d, j, 1] * jnp.exp(dm - mt)).astype(bf16)
            inter = jnp.exp(g_row - mt)
            t1 = lax.dot_general(ct.astype(bf16), qq, (((1,), (1,)), ((), ())), preferred_element_type=f32)
            t2 = jnp.concatenate([_dot_f32(vx0, sc[:, :L]), _dot_f32(vx1, sc[:, L:])], axis=-1)
            tot = inter * t1 + t2
            h_t = tot[:M_DV] * (1.0 / jnp.maximum(jnp.abs(tot[M_DV:M_DV + 1]), jnp.exp(-mt)))
            h_ref[rows, h0 * M_DV:(h0 + 1) * M_DV] = h_t[:, :L].T
            h_ref[rows, h1 * M_DV:(h1 + 1) * M_DV] = h_t[:, L:].T
        bl = halves(b_row, lambda r: r[:, last:last + 1])
        w = bl - b_row + i_row
        m_new = jnp.maximum(bl + m_row, halves(w, lambda r: r.max(axis=-1, keepdims=True)))
        decay = jnp.exp(bl + m_row - m_new)
        ws = jnp.exp(w - m_new)
        vxs = jnp.concatenate([vx0.astype(f32) * ws[:, :L], vx1.astype(f32) * ws[:, L:]], axis=-1).astype(bf16)
        zk = jnp.zeros_like(kp)
        kk = jnp.concatenate([jnp.where(lane_lo, kp, zk), jnp.where(lane_lo, zk, kp)], axis=0)
        d_cols = jnp.where(lane_lo_row, jnp.broadcast_to(decay[:, 0:1], (1, LANES)),
                           jnp.broadcast_to(decay[:, L:L + 1], (1, LANES)))
        ct_ref[sidx] = d_cols * ct + _dot_f32(vxs, kk)
        m_ref[sidx, 0:1, :] = m_new


def _mlstm_finish(hf_ref, hb_ref, o_ref, out_ref, n_chunks):
    L = M_CHUNK

    def body(ci, carry):
        rows = pl.ds(pl.multiple_of(ci * L, L), L)
        h = hf_ref[rows, :] + hb_ref[rows, :]
        parts = []
        for hd in range(M_HEADS):
            hh = h[:, hd * M_DV:(hd + 1) * M_DV]
            parts.append(hh * lax.rsqrt(jnp.mean(hh * hh, axis=-1, keepdims=True) + RMS_EPS))
        gate = _sigmoid(o_ref[0, rows, :].astype(jnp.float32))
        out_ref[0, rows, :] = (gate * jnp.concatenate(parts, axis=-1)).astype(out_ref.dtype)
        return carry

    lax.fori_loop(0, n_chunks, body, 0)


def _mlstm_kernel(qk_c, v_c, g_c, gt_c, qk_l, v_l, g_l, gt_l, o_l, *rest, need_ctx):
    if need_ctx:
        o_c, out_l, out_c, hf_l, hb_l, hf_c, hb_c, ct_ref, m_ref, sa_ref, ra_ref = rest
    else:
        out_l, hf_l, hb_l, ct_ref, m_ref, sa_ref, ra_ref = rest
        hf_c = hb_c = None
    nc, nl = qk_c.shape[1] // M_CHUNK, qk_l.shape[1] // M_CHUNK
    ct_ref[...] = jnp.zeros_like(ct_ref)
    m_ref[...] = jnp.full_like(m_ref, M_INIT)

    def phase(qk, vt, g, gt, hf, hb, n):
        def prep(s, slot):
            _mlstm_prep(qk, g, gt, sa_ref, ra_ref, slot, s, True)
            _mlstm_prep(qk, g, gt, sa_ref, ra_ref, slot, n - 1 - s, False)

        def step(s, slot):
            _mlstm_step(qk, vt, sa_ref, ra_ref, slot, hf, ct_ref, m_ref, s, True)
            _mlstm_step(qk, vt, sa_ref, ra_ref, slot, hb, ct_ref, m_ref, n - 1 - s, False)

        prep(0, 0)

        def body(i, carry):
            s0 = 2 * i
            prep(s0 + 1, 1)
            step(s0, 0)
            prep(jnp.minimum(s0 + 2, n - 1), 0)
            step(s0 + 1, 1)
            return carry
        lax.fori_loop(0, n // 2, body, 0)

    phase(qk_c, v_c, g_c, gt_c, hf_c, hb_c, nc)
    phase(qk_l, v_l, g_l, gt_l, hf_l, hb_l, nl)
    _mlstm_finish(hf_l, hb_l, o_l, out_l, nl)
    if need_ctx:
        _mlstm_finish(hf_c, hb_c, o_c, out_c, nc)


def mlstm(pc, pl_, need_ctx):
    B, S, W = pl_["mo"].shape
    Tc = pc["mqk"].shape[1]
    f32 = jnp.float32

    def row_spec(T):
        return pl.BlockSpec((1, T, W), lambda b: (b, 0, 0))

    def specs(T):
        return [row_spec(T), pl.BlockSpec((1, T // M_CHUNK, W, M_CHUNK), lambda b: (b, 0, 0, 0)),
                pl.BlockSpec((1, T, GATE_W), lambda b: (b, 0, 0)),
                pl.BlockSpec((1, T // M_CHUNK, 4 * M_HEADS, M_CHUNK), lambda b: (b, 0, 0, 0))]

    def args(p):
        return [p["mqk"], p["mvt"], p["g"], p["gt"]]

    gate_specs = [row_spec(S)] + ([row_spec(Tc)] if need_ctx else [])
    gate_args = [pl_["mo"]] + ([pc["mo"]] if need_ctx else [])

    out_shape = [jax.ShapeDtypeStruct((B, S, W), jnp.bfloat16)]
    out_specs = [pl.BlockSpec((1, S, W), lambda b: (b, 0, 0))]
    scratch = [pltpu.VMEM((S, W), f32), pltpu.VMEM((S, W), f32)]
    if need_ctx:
        out_shape.append(jax.ShapeDtypeStruct((B, Tc, W), jnp.bfloat16))
        out_specs.append(pl.BlockSpec((1, Tc, W), lambda b: (b, 0, 0)))
        scratch += [pltpu.VMEM((Tc, W), f32), pltpu.VMEM((Tc, W), f32)]
    assert (S // M_CHUNK) % 2 == 0 and (Tc // M_CHUNK) % 2 == 0
    n_pairs = M_HEADS // 2
    scratch += [pltpu.VMEM((M_HEADS, M_DV + M_ONES, LANES), f32), pltpu.VMEM((M_HEADS, 8, 2 * M_CHUNK), f32),
                pltpu.VMEM((2, 2, n_pairs, 2, M_CHUNK, 2 * M_CHUNK), f32),
                pltpu.VMEM((2, 2, n_pairs, 8, 2 * M_CHUNK), f32)]
    res = pl.pallas_call(
        functools.partial(_mlstm_kernel, need_ctx=need_ctx),
        grid=(B,),
        in_specs=specs(Tc) + specs(S) + gate_specs,
        out_specs=out_specs,
        out_shape=out_shape,
        scratch_shapes=scratch,
        compiler_params=pltpu.CompilerParams(dimension_semantics=("arbitrary",), vmem_limit_bytes=VMEM_LIMIT),
        name="mlstm",
    )(*args(pc), *args(pl_), *gate_args)
    return (res[0], res[1]) if need_ctx else (res[0], None)


POOL_HALO = 16


def _pool_mix(ext, tok0, pw_ref, ps_ref, seq_len):
    f32 = jnp.float32
    n_ext = ext.shape[0]
    ts = n_ext - 2 * POOL_HALO
    tok = tok0 + lax.broadcasted_iota(jnp.int32, (ts, POOL_GC), 0)

    def shifted(a, d):
        return pltpu.roll(a, (-d) % n_ext, 0)

    outs = []
    for gi, w in enumerate(POOL_WINDOWS):
        s = ext[:, gi * POOL_GC:(gi + 1) * POOL_GC]
        cur = s[POOL_HALO:POOL_HALO + ts]
        s = shifted(s, -1) + s
        span = 2
        while span < w:
            s = shifted(s, -(span // 2)) + shifted(s, span // 2)
            span *= 2
        lo = jnp.clip(tok - w // 2, 0, seq_len - 1)
        hi = jnp.clip(tok - w // 2 + w - 1, 0, seq_len - 1)
        mean = s[POOL_HALO:POOL_HALO + ts] / (hi - lo + 1).astype(f32)
        outs.append(_dot_f32((mean - cur).astype(jnp.bfloat16), pw_ref[gi]))
    return jnp.concatenate(outs, axis=-1) * ps_ref[...]


def _pack_bf16_pairs(h):
    n = h.shape[1] // 2
    return pltpu.pack_elementwise([h[:, :n], h[:, n:]], packed_dtype=jnp.bfloat16)


MERGE_SUB = 256


def _merge_kernel(m_ref, n_ref, pu_ref, prev_ref, next_ref, bg_ref, x_ref, mods_ref, gains_ref,
                  pw_ref, ps_ref, wb_ref, wo_ref, wrt_ref, wrp_ref, x1_ref, h2p_ref, affr_ref, aff_ref,
                  ext_ref, y_ref, *, seq_len):
    f32, bf16 = jnp.float32, jnp.bfloat16
    i = pl.program_id(1)
    tm, D = x_ref.shape[1], x_ref.shape[2]
    E = wrt_ref.shape[0]
    ts = min(MERGE_SUB, tm)
    ext_ref[0:POOL_HALO] = jnp.where(i > 0, prev_ref[0].astype(f32), 0.0)
    ext_ref[POOL_HALO:POOL_HALO + tm] = pu_ref[0].astype(f32)
    ext_ref[POOL_HALO + tm:] = jnp.where(i < pl.num_programs(1) - 1, next_ref[0].astype(f32), 0.0)

    def mix_stage(k):
        rows = slice(k * ts, (k + 1) * ts)
        p = _pool_mix(ext_ref[k * ts:(k + 1) * ts + 2 * POOL_HALO], i * tm + k * ts, pw_ref, ps_ref, seq_len)
        acc = None
        for bi, br in enumerate((m_ref[0, rows], p.astype(bf16), n_ref[0, rows])):
            gate = jnp.tanh(bg_ref[0, rows, bi * D:(bi + 1) * D].astype(f32)) + 1.0
            term = gate * _dot_f32(br, wb_ref[bi])
            acc = term if acc is None else acc + term
        y_ref[k % 2] = _dot_f32(acc.astype(bf16), wo_ref[...])

    def out_stage(k):
        rows = slice(k * ts, (k + 1) * ts)
        y = y_ref[k % 2]
        yn = y * lax.rsqrt(jnp.mean(y * y, axis=-1, keepdims=True) + RMS_EPS) * gains_ref[0:1, :]
        x1 = x_ref[0, rows] + mods_ref[0, 0:1, :] * yn
        x1_ref[0, rows] = x1
        xn = x1 * lax.rsqrt(jnp.mean(x1 * x1, axis=-1, keepdims=True) + RMS_EPS) * gains_ref[1:2, :]
        h2 = xn * (1.0 + mods_ref[0, 2:3, :]) + mods_ref[0, 1:2, :]
        h2b = h2.astype(bf16)
        lg_t = lax.dot_general(wrt_ref[...], h2b, (((1,), (1,)), ((), ())), preferred_element_type=f32)
        e_t = jnp.exp(lg_t - lg_t.max(axis=0, keepdims=True))
        aff_ref[0, :, rows] = e_t / e_t.sum(axis=0, keepdims=True)
        lg = _dot_f32(h2b, wrp_ref[...])
        lg = jnp.where(lax.broadcasted_iota(jnp.int32, lg.shape, 1) < E, lg, MASK_NEG)
        e_r = jnp.exp(lg - lg.max(axis=-1, keepdims=True))
        affr_ref[0, rows] = e_r / e_r.sum(axis=-1, keepdims=True)
        h2p_ref[0, rows] = _pack_bf16_pairs(h2)

    n_sub = tm // ts
    mix_stage(0)
    for k in range(n_sub):
        if k + 1 < n_sub:
            mix_stage(k + 1)
        out_stage(k)


def merge_out(m, n, pu, bg, x, mods, gains, pool_w, pool_scale, w_branch, w_out, w_router_t, w_router_pad, *, tm):
    B, T, D = x.shape
    W = m.shape[2]
    E = w_router_t.shape[0]
    hb = tm // POOL_HALO
    n_halo = T // POOL_HALO
    tile = lambda w: pl.BlockSpec((1, tm, w), lambda b, i: (b, i, 0))
    in_specs = [
        tile(W), tile(W), tile(W),
        pl.BlockSpec((1, POOL_HALO, W), lambda b, i: (b, jnp.maximum(i * hb - 1, 0), 0)),
        pl.BlockSpec((1, POOL_HALO, W), lambda b, i: (b, jnp.minimum((i + 1) * hb, n_halo - 1), 0)),
        tile(N_BRANCH * D), tile(D),
        pl.BlockSpec((1, 8, D), lambda b, i: (b, 0, 0)),
        _const_spec((8, D)),
        _const_spec(pool_w.shape), _const_spec(pool_scale.shape), _const_spec(w_branch.shape),
        _const_spec(w_out.shape), _const_spec(w_router_t.shape), _const_spec(w_router_pad.shape),
    ]
    return pl.pallas_call(
        functools.partial(_merge_kernel, seq_len=T),
        grid=(B, T // tm),
        in_specs=in_specs,
        out_specs=[tile(D), tile(D // 2), tile(LANES), pl.BlockSpec((1, E, tm), lambda b, i: (b, 0, i))],
        out_shape=[jax.ShapeDtypeStruct((B, T, D), jnp.float32),
                   jax.ShapeDtypeStruct((B, T, D // 2), PACK_WORD),
                   jax.ShapeDtypeStruct((B, T, LANES), jnp.float32),
                   jax.ShapeDtypeStruct((B, E, T), jnp.float32)],
        scratch_shapes=[pltpu.VMEM((tm + 2 * POOL_HALO, W), jnp.float32),
                        pltpu.VMEM((2, min(MERGE_SUB, tm), D), jnp.float32)],
        compiler_params=pltpu.CompilerParams(
            dimension_semantics=("arbitrary", "arbitrary"), vmem_limit_bytes=VMEM_LIMIT),
        name="merge_out",
    )(m, n, pu, pu, pu, bg, x, mods, gains, pool_w, pool_scale, w_branch, w_out, w_router_t, w_router_pad)


def _cumsum_chunks(x01):
    R, T = x01.shape
    r = lax.broadcasted_iota(jnp.int32, (LANES, LANES), 0)
    c = lax.broadcasted_iota(jnp.int32, (LANES, LANES), 1)
    tri_u = jnp.where(r <= c, 1.0, 0.0).astype(jnp.bfloat16)
    run = jnp.zeros((R, 1), jnp.float32)
    outs = []
    for t0 in range(0, T, LANES):
        cs = _dot_f32(x01[:, t0:t0 + LANES].astype(jnp.bfloat16), tri_u) + run
        run = cs[:, LANES - 1:LANES]
        outs.append(cs)
    return outs


def _route_kernel(aff_ref, idx_ref, *, cap):
    f32, bf16 = jnp.float32, jnp.bfloat16
    aff = aff_ref[0]
    E, T = aff.shape
    n_chunks = T // LANES

    def count_ge(cand):
        return jnp.sum(jnp.where(aff >= pltpu.bitcast(cand, f32), 1.0, 0.0), axis=-1, keepdims=True)

    top = jnp.full((E, 1), 1 << 30, jnp.int32)
    lo0 = jnp.where(count_ge(top) >= cap, top, jnp.zeros((E, 1), jnp.int32))

    def search(k, lo):
        shift = 28 - 2 * k
        best = lo
        for q in (1, 2, 3):
            cand = lo | (jnp.int32(q) << shift)
            best = jnp.where(count_ge(cand) >= cap, cand, best)
        return best

    thr = pltpu.bitcast(lax.fori_loop(0, 15, search, lo0), f32)
    gt = jnp.where(aff > thr, 1.0, 0.0)
    eq = jnp.where(aff == thr, 1.0, 0.0)
    room = cap - jnp.sum(gt, axis=-1, keepdims=True)
    sel = gt + eq * jnp.where(jnp.concatenate(_cumsum_chunks(eq), axis=-1) <= room, 1.0, 0.0)
    ranks = [jnp.minimum(cs, float(cap)) for cs in _cumsum_chunks(sel)]
    sel_r = lax.broadcasted_iota(jnp.int32, (T, LANES), 0)
    sel_c = lax.broadcasted_iota(jnp.int32, (T, LANES), 1)
    pick_last = jnp.where(sel_r == sel_c * LANES + (LANES - 1), 1.0, 0.0).astype(bf16)
    chunk_end = _dot_f32(jnp.concatenate(ranks, axis=-1).astype(bf16), pick_last)
    slot = lax.broadcasted_iota(jnp.int32, (cap, LANES), 0).astype(f32)
    lane = lax.broadcasted_iota(jnp.int32, (cap, LANES), 1)
    pad = jnp.zeros((LANES - n_chunks, LANES), f32)
    out = jnp.zeros((cap, LANES), f32)
    for e in range(E):
        ends = jnp.where(lane < n_chunks, jnp.broadcast_to(chunk_end[e:e + 1, :], (cap, LANES)), float(cap))
        n_full = jnp.sum(jnp.where(ends <= slot, 1.0, 0.0), axis=-1, keepdims=True)
        chunk_ranks = jnp.concatenate([r[e:e + 1, :] for r in ranks] + [pad], axis=0)
        mine = _dot_f32(jnp.where(lane.astype(f32) == n_full, 1.0, 0.0).astype(bf16), chunk_ranks.astype(bf16))
        pos = LANES * n_full + jnp.sum(jnp.where(mine <= slot, 1.0, 0.0), axis=-1, keepdims=True)
        out = jnp.where(lane == e, jnp.broadcast_to(pos, (cap, LANES)), out)
    idx_ref[0] = out.astype(jnp.int32)


def route(aff_t, cap):
    B, E, T = aff_t.shape
    assert cap <= 256 and cap % 8 == 0 and T % LANES == 0
    idx_t = pl.pallas_call(
        functools.partial(_route_kernel, cap=cap),
        grid=(B,),
        in_specs=[pl.BlockSpec((1, E, T), lambda b: (b, 0, 0))],
        out_specs=pl.BlockSpec((1, cap, LANES), lambda b: (b, 0, 0)),
        out_shape=jax.ShapeDtypeStruct((B, cap, LANES), jnp.int32),
        compiler_params=pltpu.CompilerParams(dimension_semantics=("arbitrary",), vmem_limit_bytes=VMEM_LIMIT),
        name="route",
    )(aff_t)
    return idx_t[:, :, :E].transpose(0, 2, 1)


ROW_GROUP = 8


def _moe_kernel(idx_ref, rows_ref, affr_ref, wg_ref, wu_ref, wd_ref, out_ref, xg_ref, ag_ref, ye_ref):
    f32, bf16 = jnp.float32, jnp.bfloat16
    e = pl.program_id(1)
    n_exp = pl.num_programs(1)
    n_sets, cap = xg_ref.shape[1], xg_ref.shape[2]

    def gather_row(p, s, ee, j):
        i = idx_ref[0, s, ee, j]
        xg_ref[p, s, pl.ds(j, 1), :] = rows_ref[s, pl.ds(i, 1), :]
        ag_ref[p, s, pl.ds(j, 1), :] = affr_ref[s, pl.ds(i, 1), :]

    def scatter_group(p, s, ee, j0):
        ids = [idx_ref[0, s, ee, j0 + r] for r in range(ROW_GROUP)]
        cur = [out_ref[s, pl.ds(i, 1), :] for i in ids]
        for r, i in enumerate(ids):
            out_ref[s, pl.ds(i, 1), :] = cur[r] + ye_ref[p, s, pl.ds(j0 + r, 1), :]

    @pl.when(e == 0)
    def _():
        out_ref[...] = jnp.zeros_like(out_ref)
        ye_ref[1] = jnp.zeros_like(ye_ref[1])

        def body(gi, carry):
            for s in range(n_sets):
                for r in range(ROW_GROUP):
                    gather_row(0, s, 0, gi * ROW_GROUP + r)
            return carry
        lax.fori_loop(0, cap // ROW_GROUP, body, 0)

    def step(p):
        e_prv = jnp.maximum(e - 1, 0)
        e_nxt = jnp.minimum(e + 1, n_exp - 1)
        for s in range(n_sets):
            for j0 in range(0, cap, ROW_GROUP):
                scatter_group(1 - p, s, e_prv, j0)
            for j in range(cap):
                gather_row(1 - p, s, e_nxt, j)
        packed = jnp.concatenate([xg_ref[p, s] for s in range(n_sets)], axis=0)
        xe = jnp.concatenate(
            [pltpu.unpack_elementwise(packed, index=i, packed_dtype=bf16, unpacked_dtype=f32) for i in (0, 1)],
            axis=-1).astype(bf16)
        aff = jnp.concatenate([ag_ref[p, s] for s in range(n_sets)], axis=0)
        gate = jnp.sum(jnp.where(lax.broadcasted_iota(jnp.int32, aff.shape, 1) == e, aff, 0.0),
                       axis=-1, keepdims=True)
        a = _dot_f32(xe, wg_ref[0])
        hid = (a * _sigmoid(a) * _dot_f32(xe, wu_ref[0])).astype(bf16)
        ye = _dot_f32(hid, wd_ref[0]) * gate
        for s in range(n_sets):
            ye_ref[p, s] = ye[s * cap:(s + 1) * cap]

    @pl.when(e % 2 == 0)
    def _():
        step(0)

    @pl.when(e % 2 == 1)
    def _():
        step(1)

    @pl.when(e == n_exp - 1)
    def _():
        def body(gi, carry):
            for s in range(n_sets):
                scatter_group(1, s, e, gi * ROW_GROUP)
            return carry
        lax.fori_loop(0, cap // ROW_GROUP, body, 0)


MOE_SETS = 1


def moe_experts(rows, aff_rows, idx, w_gate, w_up, w_down, layer):
    G, T, RW = rows.shape
    AW = aff_rows.shape[2]
    _, E, cap = idx.shape
    D = w_gate.shape[2]
    ns = MOE_SETS
    assert E % 2 == 0 and cap % ROW_GROUP == 0 and G % ns == 0
    wspec = pl.BlockSpec((None, 1, D, D), lambda g, e: (layer, e, 0, 0))
    return pl.pallas_call(
        _moe_kernel,
        grid=(G // ns, E),
        in_specs=[pl.BlockSpec((1, ns, E, cap), lambda g, e: (g, 0, 0, 0), memory_space=pltpu.SMEM),
                  pl.BlockSpec((ns, T, RW), lambda g, e: (g, 0, 0)),
                  pl.BlockSpec((ns, T, AW), lambda g, e: (g, 0, 0)),
                  wspec, wspec, wspec],
        out_specs=pl.BlockSpec((ns, T, D), lambda g, e: (g, 0, 0)),
        out_shape=jax.ShapeDtypeStruct((G, T, D), jnp.float32),
        scratch_shapes=[pltpu.VMEM((2, ns, cap, RW), rows.dtype), pltpu.VMEM((2, ns, cap, AW), jnp.float32),
                        pltpu.VMEM((2, ns, cap, D), jnp.float32)],
        compiler_params=pltpu.CompilerParams(
            dimension_semantics=("arbitrary", "arbitrary"), vmem_limit_bytes=VMEM_LIMIT),
        name="moe_experts",
    )(idx.reshape(G // ns, ns, E, cap), rows, aff_rows, w_gate, w_up, w_down)


def _residual_norm_kernel(x_ref, y_ref, gate_ref, gain_ref, o_ref):
    y = y_ref[0]
    yn = y * lax.rsqrt(jnp.mean(y * y, axis=-1, keepdims=True) + RMS_EPS) * gain_ref[...]
    o_ref[0] = x_ref[0] + gate_ref[0] * yn


def residual_norm(x, y, gate, gain, *, tm):
    B, T, D = x.shape
    tile = pl.BlockSpec((1, tm, D), lambda b, i: (b, i, 0))
    return pl.pallas_call(
        _residual_norm_kernel,
        grid=(B, T // tm),
        in_specs=[tile, tile, pl.BlockSpec((1, 1, D), lambda b, i: (b, 0, 0)), _const_spec((1, D))],
        out_specs=tile,
        out_shape=jax.ShapeDtypeStruct((B, T, D), jnp.float32),
        compiler_params=pltpu.CompilerParams(
            dimension_semantics=("arbitrary", "arbitrary"), vmem_limit_bytes=VMEM_LIMIT),
        name="residual_norm",
    )(x, y, gate, gain)


ADA_COLS = 1536


def _ada_kernel(c_ref, w_ref, b_ref, o_ref):
    c = c_ref[...]
    h = (c * _sigmoid(c)).astype(jnp.bfloat16)
    o_ref[...] = _dot_f32(h, w_ref[...].astype(jnp.bfloat16)) + b_ref[...]


def ada_modulation(cond, ada_w, ada_b, layer):
    R, D = cond.shape
    N = ada_w.shape[2]
    return pl.pallas_call(
        _ada_kernel,
        grid=(N // ADA_COLS,),
        in_specs=[pl.BlockSpec((R, D), lambda j: (0, 0)),
                  pl.BlockSpec((None, D, ADA_COLS), lambda j: (layer, 0, j)),
                  pl.BlockSpec((None, 1, ADA_COLS), lambda j: (layer, 0, j))],
        out_specs=pl.BlockSpec((R, ADA_COLS), lambda j: (0, j)),
        out_shape=jax.ShapeDtypeStruct((R, N), jnp.float32),
        compiler_params=pltpu.CompilerParams(dimension_semantics=("arbitrary",), vmem_limit_bytes=VMEM_LIMIT),
        name="ada_modulation",
    )(cond, ada_w, ada_b[:, None, :])


def kernel(x, c, ctx, c_ctx, norm_gain, ada_w, ada_b, w_in, mlstm_gate_bias, pool_w, pool_scale,
           na_rpb, w_branch, w_out, router_w, w_gate, w_up, w_down):
    B, S, D = x.shape
    Tc = ctx.shape[1]
    f32, bf16 = jnp.float32, jnp.bfloat16
    rope_tabs = rope_lane_tables(S)
    names = [n for n, _ in IN_GROUPS] + ["gt"]
    cond = jnp.pad(jnp.concatenate([c, c_ctx[None]], axis=0), ((0, (-(B + 1)) % 8), (0, 0)))
    moe_w = (w_gate.astype(bf16), w_up.astype(bf16), w_down.astype(bf16))
    assert B % (S // Tc) == 0
    pending = pending_c = None
    for l in range(DEPTH):
        need_ctx = l < DEPTH - 1
        g = norm_gain[l]
        mods = ada_modulation(cond, ada_w, ada_b, l)
        mod_l = jnp.split(mods[:B, None, :], N_MOD, axis=-1)
        mod_cb = [jnp.broadcast_to(m[None, None], (B, 1, D)) for m in jnp.split(mods[B], N_MOD, axis=-1)]
        w_perm = permute_w_in(w_in[l])
        gb_row = jnp.pad(mlstm_gate_bias[l].reshape(1, -1).astype(f32), ((0, 0), (0, GATE_W - 4 * M_HEADS)))
        proj = in_proj(x, g[0][None], mod_l[0], mod_l[1], w_perm, gb_row, rope_tabs, pending, tm=512)
        if pending is not None:
            x = proj[-1]
        pl_ = dict(zip(names, proj))
        keep_c = None if need_ctx else ("mqk", "mvt", "kn", "vn", "g")
        proj_c = in_proj(ctx, g[0][None], mod_cb[0], mod_cb[1], w_perm, gb_row, None, pending_c, keep_c, tm=Tc)
        if pending_c is not None:
            ctx = proj_c[-1]
        pc_ = dict(zip(names if keep_c is None else list(keep_c) + ["gt"], proj_c))

        m_l, m_c = mlstm(pc_, pl_, need_ctx)
        n_l, n_c = neighbourhood_attention(pl_["qn"], pl_["kn"], pl_["vn"], pc_["kn"], pc_["vn"],
                                           na_bias_table(na_rpb[l]), pc_["qn"] if need_ctx else None)
        gains = jnp.pad(g[1:3], ((0, 6), (0, 0)))
        merge_w = (pool_w[l].astype(bf16), pool_scale[l][None], w_branch[l].astype(bf16),
                   (0.5 * w_out[l]).astype(bf16),
                   router_w[l].T.astype(bf16),
                   jnp.pad(router_w[l], ((0, 0), (0, LANES - N_EXPERTS))).astype(bf16))

        def mods8(mods):
            return jnp.pad(jnp.concatenate([mods[2], mods[3], mods[4]], axis=1), ((0, 0), (0, 5), (0, 0)))

        x1, rows, aff_r, aff_t = merge_out(m_l, n_l, pl_["pu"], pl_["bg"], x, mods8(mod_l), gains, *merge_w, tm=1024)
        y = moe_experts(rows, aff_r, route(aff_t, EC_FACTOR * S // N_EXPERTS), *moe_w, l)
        if need_ctx:
            x, pending = x1, (y, mod_l[5], g[3][None])
        else:
            x = residual_norm(x1, y, mod_l[5], g[3][None], tm=512)
        if need_ctx:
            c1, rows, aff_r, aff_t = merge_out(m_c, n_c, pc_["pu"], pc_["bg"], ctx, mods8(mod_cb), gains, *merge_w, tm=Tc)
            per = S // Tc
            cap_c = EC_FACTOR * Tc // N_EXPERTS
            idx = route(aff_t, cap_c) + (jnp.arange(B, dtype=jnp.int32) % per * Tc)[:, None, None]
            idx = idx.reshape(B // per, per, N_EXPERTS, cap_c).transpose(0, 2, 1, 3).reshape(B // per, N_EXPERTS, per * cap_c)
            y = moe_experts(rows.reshape(B // per, S, D // 2), aff_r.reshape(B // per, S, LANES), idx,
                            *moe_w, l).reshape(B, Tc, D)
            ctx, pending_c = c1, (y, mod_cb[5], g[3][None])
    return x
```

```python
import functools
import math

import jax
import jax.numpy as jnp
import numpy as np
from jax import lax
from jax.experimental import pallas as pl
from jax.experimental.pallas import tpu as pltpu

D_MODEL = 1024
DEPTH = 2
GRID_W = 64
BRANCH_W = D_MODEL // 2
N_BRANCH = 3
M_HEADS = 4
M_DV = BRANCH_W // M_HEADS
M_DQK = M_DV // 2
M_CHUNK = 128
M_INIT = -1e30
M_ONES = 16
POOL_GROUPS = 4
POOL_GC = BRANCH_W // POOL_GROUPS
POOL_WINDOWS = (2, 4, 8, 16)
NA_HEADS = 8
NA_DH = BRANCH_W // NA_HEADS
NA_KH = 8
NA_KW = 16
N_EXPERTS = 16
EC_FACTOR = 2
ROPE_BASE = 10000.0
RMS_EPS = 1e-6
N_MOD = 6
SPLIT_SIZES = (M_HEADS * M_DQK, M_HEADS * M_DQK, M_HEADS * M_DV, M_HEADS * M_DV, 4 * M_HEADS,
               POOL_GROUPS * POOL_GC, NA_HEADS * NA_DH, NA_HEADS * NA_DH, NA_HEADS * NA_DH,
               N_BRANCH * D_MODEL)
PROJ_W = sum(SPLIT_SIZES)

LANES = 128
VMEM_LIMIT = 56 * 1024 * 1024
MASK_NEG = -1e30
GATE_W = LANES

IN_GROUPS = (("mqk", 2 * M_HEADS * M_DQK), ("mvt", BRANCH_W), ("mo", BRANCH_W), ("pu", BRANCH_W),
             ("qn", BRANCH_W), ("kn", BRANCH_W), ("vn", BRANCH_W), ("bg", N_BRANCH * D_MODEL),
             ("g", GATE_W))
IN_W = sum(w for _, w in IN_GROUPS)
MM_COLS = 512
IN_SUB = 256
PACK_WORD = jnp.uint32


def _sigmoid(x):
    return 0.5 * jnp.tanh(0.5 * x) + 0.5


def _const_spec(shape):
    nd = len(shape)
    return pl.BlockSpec(shape, lambda *_: (0,) * nd, pipeline_mode=pl.Buffered(1))


def _rope_tile(x, cos, sin_signed):
    half = M_DQK // 4
    lane = lax.broadcasted_iota(jnp.int32, x.shape, 1)
    partner = jnp.where((lane % (2 * half)) < half, pltpu.roll(x, LANES - half, 1), pltpu.roll(x, half, 1))
    return x * cos + partner * sin_signed


def _in_proj_kernel(x_ref, g_ref, shift_ref, scale_ref, w_ref, gb_ref, *rest, rope, pending, keep):
    if rope:
        cos_ref, sin_ref = rest[:2]
        rest = rest[2:]
    rest, h_ref = rest[:-1], rest[-1]
    if pending:
        r_ref, rgate_ref, rgain_ref = rest[:3]
        rest, xo_ref = rest[3:-1], rest[-1]
    o_refs, gt_ref = rest[:-1], rest[-1]
    tm = x_ref.shape[1]
    ts = h_ref.shape[1]
    cps = ts // M_CHUNK

    def prologue(k):
        rows = slice(k * ts, (k + 1) * ts)
        x = x_ref[0, rows]
        if pending:
            r = r_ref[0, rows]
            x = x + rgate_ref[0] * (r * lax.rsqrt(jnp.mean(r * r, axis=-1, keepdims=True) + RMS_EPS)
                                    * rgain_ref[...])
            xo_ref[0, rows] = x
        y = x * lax.rsqrt(jnp.mean(x * x, axis=-1, keepdims=True) + RMS_EPS)
        h_ref[k % 2] = ((y * g_ref[...]) * (1.0 + scale_ref[0]) + shift_ref[0]).astype(jnp.bfloat16)

    def project(k):
        rows = slice(k * ts, (k + 1) * ts)
        h = h_ref[k % 2]
        off = 0
        o_iter = iter(o_refs)
        for name, width in IN_GROUPS:
            if name not in keep:
                off += width
                continue
            o_ref = next(o_iter)
            for c0 in range(0, width, MM_COLS):
                cw = min(MM_COLS, width - c0)
                acc = jnp.dot(h, w_ref[:, off + c0:off + c0 + cw], preferred_element_type=jnp.float32)
                if name == "g":
                    acc = acc + gb_ref[...]
                    for c in range(cps):
                        gt_ref[0, k * cps + c] = acc[c * M_CHUNK:(c + 1) * M_CHUNK, :].T[:4 * M_HEADS, :]
                if name == "mqk" and rope:
                    cos, sin = cos_ref[rows], sin_ref[rows]
                    acc = jnp.concatenate([_rope_tile(acc[:, t:t + LANES], cos, sin) for t in range(0, cw, LANES)],
                                          axis=-1)
                if name == "mvt":
                    for c in range(cps):
                        o_ref[0, k * cps + c] = acc[c * M_CHUNK:(c + 1) * M_CHUNK, :].T.astype(o_ref.dtype)
                    continue
                o_ref[0, rows, c0:c0 + cw] = acc.astype(o_ref.dtype)
            off += width

    n_sub = tm // ts
    prologue(0)
    for k in range(n_sub):
        if k + 1 < n_sub:
            prologue(k + 1)
        project(k)


def in_proj(x, g, shift, scale, w_perm, gate_bias_row, rope_tabs=None, pending=None, keep=None, *, tm):
    B, T, D = x.shape
    keep = tuple(n for n, _ in IN_GROUPS) if keep is None else tuple(keep)
    assert "g" in keep
    out_shape = [jax.ShapeDtypeStruct((B, T, w), jnp.float32 if n == "g" else jnp.bfloat16) for n, w in IN_GROUPS]
    out_specs = [pl.BlockSpec((1, tm, w), lambda b, i: (b, i, 0)) for _, w in IN_GROUPS]
    k_mvt = [n for n, _ in IN_GROUPS].index("mvt")
    assert BRANCH_W == MM_COLS and tm % M_CHUNK == 0
    out_shape[k_mvt] = jax.ShapeDtypeStruct((B, T // M_CHUNK, BRANCH_W, M_CHUNK), jnp.bfloat16)
    out_specs[k_mvt] = pl.BlockSpec((1, tm // M_CHUNK, BRANCH_W, M_CHUNK), lambda b, i: (b, i, 0, 0))
    kept = [k for k, (n, _) in enumerate(IN_GROUPS) if n in keep]
    out_shape, out_specs = [out_shape[k] for k in kept], [out_specs[k] for k in kept]
    out_shape.append(jax.ShapeDtypeStruct((B, T // M_CHUNK, 4 * M_HEADS, M_CHUNK), jnp.float32))
    out_specs.append(pl.BlockSpec((1, tm // M_CHUNK, 4 * M_HEADS, M_CHUNK), lambda b, i: (b, i, 0, 0)))
    in_specs = [
        pl.BlockSpec((1, tm, D), lambda b, i: (b, i, 0)),
        _const_spec((1, D)),
        pl.BlockSpec((1, 1, D), lambda b, i: (b, 0, 0)),
        pl.BlockSpec((1, 1, D), lambda b, i: (b, 0, 0)),
        _const_spec((D, IN_W)),
        _const_spec((1, GATE_W)),
    ]
    args = [x, g, shift, scale, w_perm, gate_bias_row]
    if rope_tabs is not None:
        in_specs += [pl.BlockSpec((tm, LANES), lambda b, i: (i, 0))] * 2
        args += list(rope_tabs)
    if pending is not None:
        in_specs += [pl.BlockSpec((1, tm, D), lambda b, i: (b, i, 0)), pl.BlockSpec((1, 1, D), lambda b, i: (b, 0, 0)),
                     _const_spec((1, D))]
        args += list(pending)
        out_shape.append(jax.ShapeDtypeStruct((B, T, D), jnp.float32))
        out_specs.append(pl.BlockSpec((1, tm, D), lambda b, i: (b, i, 0)))
    return pl.pallas_call(
        functools.partial(_in_proj_kernel, rope=rope_tabs is not None, pending=pending is not None, keep=keep),
        grid=(B, T // tm),
        in_specs=in_specs,
        out_specs=out_specs,
        out_shape=out_shape,
        scratch_shapes=[pltpu.VMEM((2, min(IN_SUB, tm), D), jnp.bfloat16)],
        compiler_params=pltpu.CompilerParams(
            dimension_semantics=("arbitrary", "arbitrary"), vmem_limit_bytes=VMEM_LIMIT),
        name="in_proj",
    )(*args)


def rope_lane_tables(n_tokens):
    t = jnp.arange(n_tokens)
    row = (t // GRID_W).astype(jnp.float32)
    col = (t % GRID_W).astype(jnp.float32)
    half = M_DQK // 2
    inv = ROPE_BASE ** (-jnp.arange(0, half, 2, dtype=jnp.float32) / half)
    ar = row[:, None] * inv[None, :]
    ac = col[:, None] * inv[None, :]
    cos = jnp.concatenate([jnp.cos(ar), jnp.cos(ar), jnp.cos(ac), jnp.cos(ac)], axis=-1)
    sin = jnp.concatenate([-jnp.sin(ar), jnp.sin(ar), -jnp.sin(ac), jnp.sin(ac)], axis=-1)
    return jnp.tile(cos, (1, LANES // M_DQK)), jnp.tile(sin, (1, LANES // M_DQK))


def permute_w_in(w_in_l):
    qm, km, vm, om, gm, pu, qn, kn, vn, bg = jnp.split(w_in_l, [int(s) for s in np.cumsum(SPLIT_SIZES)[:-1]], axis=-1)
    gpad = jnp.pad(gm, ((0, 0), (0, GATE_W - gm.shape[1])))
    w = jnp.concatenate([qm * (M_DQK ** -0.5), km, vm, om, pu, qn * (NA_DH ** -0.5), kn, vn, bg * 0.5, gpad], axis=-1)
    return w.astype(jnp.bfloat16)


def _pair_scores(qp, k_parts, biases):
    n = qp.shape[0]
    lane_lo = lax.broadcasted_iota(jnp.int32, (n, LANES), 1) < NA_DH
    zero = jnp.zeros_like(qp)
    qq = jnp.concatenate([jnp.where(lane_lo, qp, zero), jnp.where(lane_lo, zero, qp)], axis=0)
    scores = []
    for kp, bias in zip(k_parts, biases):
        s = lax.dot_general(qq, kp, (((1,), (1,)), ((), ())), preferred_element_type=jnp.float32)
        scores.append(s if bias is None else s + bias)
    return scores


def _pair_softmax_pv(scores, v_parts):
    n = scores[0].shape[0] // 2
    lane_lo = lax.broadcasted_iota(jnp.int32, (n, LANES), 1) < NA_DH
    m = scores[0].max(axis=-1, keepdims=True)
    for s in scores[1:]:
        m = jnp.maximum(m, s.max(axis=-1, keepdims=True))
    l = None
    o = None
    for s, vp in zip(scores, v_parts):
        p = jnp.exp(s - m)
        ls = p.sum(axis=-1, keepdims=True)
        os_ = jnp.dot(p.astype(jnp.bfloat16), vp, preferred_element_type=jnp.float32)
        l = ls if l is None else l + ls
        o = os_ if o is None else o + os_
    o = o * (1.0 / l)
    return jnp.where(lane_lo, o[:n], o[n:])


def _na_kernel(q_ref, k_ref, v_ref, kc_ref, vc_ref, bias_ref, *rest, need_ctx):
    if need_ctx:
        qc_ref, o_ref, oc_ref, s_ref = rest
    else:
        o_ref, s_ref = rest
    S = q_ref.shape[1]
    Tc = kc_ref.shape[1]
    rows = S // GRID_W
    n_pairs = BRANCH_W // LANES
    n_loc = NA_KH * GRID_W

    def window(r):
        rs = jnp.clip(r - NA_KH // 2, 0, rows - NA_KH)
        return r - rs, pl.multiple_of(r * GRID_W, GRID_W), pl.multiple_of(rs * GRID_W, GRID_W)

    def scores_stage(r, slot, j):
        var, q0, k0 = window(r)
        ls = slice(j * LANES, (j + 1) * LANES)
        s_loc, s_ctx = _pair_scores(q_ref[0, pl.ds(q0, GRID_W), ls],
                                    (k_ref[0, pl.ds(k0, n_loc), ls], kc_ref[0, :, ls]),
                                    (bias_ref[var, j], None))
        s_ref[slot, j, :, :n_loc] = s_loc
        s_ref[slot, j, :, n_loc:] = s_ctx

    def output_stage(r, slot, j):
        _, q0, k0 = window(r)
        ls = slice(j * LANES, (j + 1) * LANES)
        o = _pair_softmax_pv((s_ref[slot, j, :, :n_loc], s_ref[slot, j, :, n_loc:]),
                             (v_ref[0, pl.ds(k0, n_loc), ls], vc_ref[0, :, ls]))
        o_ref[0, pl.ds(q0, GRID_W), ls] = o.astype(o_ref.dtype)

    for j in range(n_pairs):
        scores_stage(0, 0, j)

    def two_rows(i, carry):
        r0 = 2 * i
        for j in range(n_pairs):
            scores_stage(r0 + 1, 1, j)
            output_stage(r0, 0, j)
        for j in range(n_pairs):
            scores_stage(jnp.minimum(r0 + 2, rows - 1), 0, j)
            output_stage(r0 + 1, 1, j)
        return carry

    lax.fori_loop(0, rows // 2, two_rows, 0)

    if need_ctx:
        outs = []
        for j in range(n_pairs):
            ls = slice(j * LANES, (j + 1) * LANES)
            scores = _pair_scores(qc_ref[0, :, ls], (kc_ref[0, :, ls],), (None,))
            outs.append(_pair_softmax_pv(scores, (vc_ref[0, :, ls],)))
        oc_ref[0] = jnp.concatenate(outs, axis=-1).astype(oc_ref.dtype)


def na_bias_table(rpb):
    H = rpb.shape[0]
    var = jnp.arange(NA_KH)
    kr = jnp.arange(NA_KH)
    dr = kr[None, :] - var[:, None] + NA_KH - 1
    cols = jnp.arange(GRID_W)
    dc = jnp.clip(cols[None, :] - cols[:, None] + NA_KW - 1, 0, 2 * NA_KW - 2)
    cs = jnp.clip(cols - NA_KW // 2, 0, GRID_W - NA_KW)
    colmask = (cols[None, :] >= cs[:, None]) & (cols[None, :] < cs[:, None] + NA_KW)
    pick_r = jax.nn.one_hot(dr, 2 * NA_KH - 1, dtype=jnp.float32)
    pick_c = jax.nn.one_hot(dc, 2 * NA_KW - 1, dtype=jnp.float32)
    tab = jnp.einsum('vka,hab,qcb->vhqkc', pick_r, rpb.astype(jnp.float32), pick_c, precision=lax.Precision.HIGHEST)
    tab = jnp.where(colmask[None, None, :, None, :], tab, MASK_NEG)
    return tab.reshape(NA_KH, H // 2, 2 * GRID_W, NA_KH * GRID_W)


def neighbourhood_attention(qn, kn, vn, kc, vc, bias_tab, qc=None):
    B, S, W = qn.shape
    Tc = kc.shape[1]
    need_ctx = qc is not None
    lat = pl.BlockSpec((1, S, W), lambda b: (b, 0, 0))
    cx = pl.BlockSpec((1, Tc, W), lambda b: (b, 0, 0))
    in_specs = [lat, lat, lat, cx, cx, _const_spec(bias_tab.shape)]
    args = [qn, kn, vn, kc, vc, bias_tab]
    out_shape = [jax.ShapeDtypeStruct((B, S, W), jnp.bfloat16)]
    out_specs = [lat]
    if need_ctx:
        in_specs.append(cx)
        args.append(qc)
        out_shape.append(jax.ShapeDtypeStruct((B, Tc, W), jnp.bfloat16))
        out_specs.append(cx)
    res = pl.pallas_call(
        functools.partial(_na_kernel, need_ctx=need_ctx),
        grid=(B,),
        in_specs=in_specs,
        out_specs=out_specs,
        out_shape=out_shape,
        scratch_shapes=[pltpu.VMEM((2, W // LANES, 2 * GRID_W, NA_KH * GRID_W + Tc), jnp.float32)],
        compiler_params=pltpu.CompilerParams(dimension_semantics=("arbitrary",), vmem_limit_bytes=VMEM_LIMIT),
        name="neighbourhood_attention",
    )(*args)
    return (res[0], res[1]) if need_ctx else (res[0], None)


def _split_bf16(x):
    hi = x.astype(jnp.bfloat16)
    r1 = x - hi.astype(jnp.float32)
    mid = r1.astype(jnp.bfloat16)
    lo = (r1 - mid.astype(jnp.float32)).astype(jnp.bfloat16)
    return hi, mid, lo


def _dot_f32(a, b):
    return jnp.dot(a, b, preferred_element_type=jnp.float32)


def _log_sigmoid(x):
    return jnp.minimum(x, 0.0) - jnp.log(1.0 + jnp.exp(-jnp.abs(x)))


def _pair_queries(qp):
    lane_lo = lax.broadcasted_iota(jnp.int32, qp.shape, 1) < M_DQK
    zq = jnp.zeros_like(qp)
    return jnp.concatenate([jnp.where(lane_lo, qp, zq), jnp.where(lane_lo, zq, qp)], axis=0)


def _mlstm_prep(qk_ref, g_ref, gt_ref, sa_ref, ra_ref, slot, ci, fwd):
    L = M_CHUNK
    f32, bf16 = jnp.float32, jnp.bfloat16
    d = 0 if fwd else 1
    rows = pl.ds(pl.multiple_of(ci * L, L), L)
    sq_r = lax.broadcasted_iota(jnp.int32, (L, L), 0)
    sq_c = lax.broadcasted_iota(jnp.int32, (L, L), 1)
    tri_l = jnp.where(sq_r >= sq_c, 1.0, 0.0).astype(bf16)
    tri_u = jnp.where(sq_r <= sq_c, 1.0, 0.0).astype(bf16)
    gates = g_ref[0, rows, :]
    gates_t = gt_ref[0, ci]
    lf, lf_t = _log_sigmoid(gates), _log_sigmoid(gates_t)
    cum = sum(_dot_f32(tri_l if fwd else tri_u, p) for p in _split_bf16(lf))
    cum_t = sum(_dot_f32(p, tri_u if fwd else tri_l) for p in _split_bf16(lf_t))
    t_i = 0 if fwd else 2
    st_s = lax.broadcasted_iota(jnp.int32, (L, 2 * L), 0)
    st_t = lax.broadcasted_iota(jnp.int32, (L, 2 * L), 1) & (L - 1)
    valid = (st_s <= st_t) if fwd else (st_s >= st_t)

    def row2(tile, c0, c1):
        return jnp.concatenate([tile[c0:c0 + 1, :], tile[c1:c1 + 1, :]], axis=-1)

    for j in range(M_HEADS // 2):
        ci0, ci1 = t_i * M_HEADS + 2 * j, t_i * M_HEADS + 2 * j + 1
        cf0, cf1 = ci0 + M_HEADS, ci1 + M_HEADS
        b_row, i_row = row2(cum_t, cf0, cf1), row2(gates_t, ci0, ci1)
        colb = jnp.concatenate([jnp.broadcast_to(gates[:, ci0:ci0 + 1] - cum[:, cf0:cf0 + 1], (L, L)),
                                jnp.broadcast_to(gates[:, ci1:ci1 + 1] - cum[:, cf1:cf1 + 1], (L, L))],
                               axis=-1)
        qp = qk_ref[0, rows, j * LANES:(j + 1) * LANES]
        kp = qk_ref[0, rows, M_HEADS * M_DQK + j * LANES:M_HEADS * M_DQK + (j + 1) * LANES]
        sa_ref[slot, d, j, 0] = jnp.where(valid, b_row + colb, MASK_NEG)
        sa_ref[slot, d, j, 1] = lax.dot_general(kp, _pair_queries(qp), (((1,), (1,)), ((), ())),
                                                preferred_element_type=f32)
        ra_ref[slot, d, j, 0:1, :] = b_row
        ra_ref[slot, d, j, 1:2, :] = i_row


def _mlstm_step(qk_ref, vt_ref, sa_ref, ra_ref, slot, h_ref, ct_ref, m_ref, ci, fwd):
    L = M_CHUNK
    f32, bf16 = jnp.float32, jnp.bfloat16
    d = 0 if fwd else 1
    rows = pl.ds(pl.multiple_of(ci * L, L), L)
    last = L - 1 if fwd else 0
    lane_lo = lax.broadcasted_iota(jnp.int32, (L, LANES), 1) < M_DQK
    lane_lo_row = lax.broadcasted_iota(jnp.int32, (1, LANES), 1) < M_DQK
    ones = jnp.ones((M_ONES, L), bf16)

    def halves(row, f):
        return jnp.concatenate([jnp.broadcast_to(f(row[:, :L]), (1, L)), jnp.broadcast_to(f(row[:, L:]), (1, L))],
                               axis=-1)

    for j in range(M_HEADS // 2):
        h0, h1 = 2 * j, 2 * j + 1
        sidx = 2 * j + d
        b_row, i_row = ra_ref[slot, d, j, 0:1, :], ra_ref[slot, d, j, 1:2, :]
        m_row = m_ref[sidx, 0:1, :]
        kp = qk_ref[0, rows, M_HEADS * M_DQK + j * LANES:M_HEADS * M_DQK + (j + 1) * LANES]
        vx0 = jnp.concatenate([vt_ref[0, ci, h0 * M_DV:(h0 + 1) * M_DV, :], ones], axis=0)
        vx1 = jnp.concatenate([vt_ref[0, ci, h1 * M_DV:(h1 + 1) * M_DV, :], ones], axis=0)
        ct = ct_ref[sidx]
        if h_ref is not None:
            dm = sa_ref[slot, d, j, 0]
            qq = _pair_queries(qk_ref[0, rows, j * LANES:(j + 1) * LANES])
            g_row = b_row + m_row
            mt = jnp.maximum(g_row, dm.max(axis=0, keepdims=True))
            sc = (sa_ref[slot, d, j, 1] * jnp.exp(dm - mt)).astype(bf16)
            inter = jnp.exp(g_row - mt)
            t1 = lax.dot_general(ct.astype(bf16), qq, (((1,), (1,)), ((), ())), preferred_element_type=f32)
            t2 = jnp.concatenate([_dot_f32(vx0, sc[:, :L]), _dot_f32(vx1, sc[:, L:])], axis=-1)
            tot = inter * t1 + t2
            h_t = tot[:M_DV] * (1.0 / jnp.maximum(jnp.abs(tot[M_DV:M_DV + 1]), jnp.exp(-mt)))
            h_ref[rows, h0 * M_DV:(h0 + 1) * M_DV] = h_t[:, :L].T
            h_ref[rows, h1 * M_DV:(h1 + 1) * M_DV] = h_t[:, L:].T
        bl = halves(b_row, lambda r: r[:, last:last + 1])
        w = bl - b_row + i_row
        m_new = jnp.maximum(bl + m_row, halves(w, lambda r: r.max(axis=-1, keepdims=True)))
        decay = jnp.exp(bl + m_row - m_new)
        ws = jnp.exp(w - m_new)
        vxs = jnp.concatenate([vx0.astype(f32) * ws[:, :L], vx1.astype(f32) * ws[:, L:]], axis=-1).astype(bf16)
        zk = jnp.zeros_like(kp)
        kk = jnp.concatenate([jnp.where(lane_lo, kp, zk), jnp.where(lane_lo, zk, kp)], axis=0)
        d_cols = jnp.where(lane_lo_row, jnp.broadcast_to(decay[:, 0:1], (1, LANES)),
                           jnp.broadcast_to(decay[:, L:L + 1], (1, LANES)))
        ct_ref[sidx] = d_cols * ct + _dot_f32(vxs, kk)
        m_ref[sidx, 0:1, :] = m_new


def _mlstm_finish(hf_ref, hb_ref, o_ref, out_ref, n_chunks):
    L = M_CHUNK

    def body(ci, carry):
        rows = pl.ds(pl.multiple_of(ci * L, L), L)
        h = hf_ref[rows, :] + hb_ref[rows, :]
        parts = []
        for hd in range(M_HEADS):
            hh = h[:, hd * M_DV:(hd + 1) * M_DV]
            parts.append(hh * lax.rsqrt(jnp.mean(hh * hh, axis=-1, keepdims=True) + RMS_EPS))
        gate = _sigmoid(o_ref[0, rows, :].astype(jnp.float32))
        out_ref[0, rows, :] = (gate * jnp.concatenate(parts, axis=-1)).astype(out_ref.dtype)
        return carry

    lax.fori_loop(0, n_chunks, body, 0)


def _mlstm_kernel(qk_c, v_c, g_c, gt_c, qk_l, v_l, g_l, gt_l, o_l, *rest, need_ctx):
    if need_ctx:
        o_c, out_l, out_c, hf_l, hb_l, hf_c, hb_c, ct_ref, m_ref, sa_ref, ra_ref = rest
    else:
        out_l, hf_l, hb_l, ct_ref, m_ref, sa_ref, ra_ref = rest
        hf_c = hb_c = None
    nc, nl = qk_c.shape[1] // M_CHUNK, qk_l.shape[1] // M_CHUNK
    ct_ref[...] = jnp.zeros_like(ct_ref)
    m_ref[...] = jnp.full_like(m_ref, M_INIT)

    def phase(qk, vt, g, gt, hf, hb, n):
        def prep(s, slot):
            _mlstm_prep(qk, g, gt, sa_ref, ra_ref, slot, s, True)
            _mlstm_prep(qk, g, gt, sa_ref, ra_ref, slot, n - 1 - s, False)

        def step(s, slot):
            _mlstm_step(qk, vt, sa_ref, ra_ref, slot, hf, ct_ref, m_ref, s, True)
            _mlstm_step(qk, vt, sa_ref, ra_ref, slot, hb, ct_ref, m_ref, n - 1 - s, False)

        prep(0, 0)

        def body(i, carry):
            s0 = 2 * i
            prep(s0 + 1, 1)
            step(s0, 0)
            prep(jnp.minimum(s0 + 2, n - 1), 0)
            step(s0 + 1, 1)
            return carry
        lax.fori_loop(0, n // 2, body, 0)

    phase(qk_c, v_c, g_c, gt_c, hf_c, hb_c, nc)
    phase(qk_l, v_l, g_l, gt_l, hf_l, hb_l, nl)
    _mlstm_finish(hf_l, hb_l, o_l, out_l, nl)
    if need_ctx:
        _mlstm_finish(hf_c, hb_c, o_c, out_c, nc)


def mlstm(pc, pl_, need_ctx):
    B, S, W = pl_["mo"].shape
    Tc = pc["mqk"].shape[1]
    f32 = jnp.float32

    def row_spec(T):
        return pl.BlockSpec((1, T, W), lambda b: (b, 0, 0))

    def specs(T):
        return [row_spec(T), pl.BlockSpec((1, T // M_CHUNK, W, M_CHUNK), lambda b: (b, 0, 0, 0)),
                pl.BlockSpec((1, T, GATE_W), lambda b: (b, 0, 0)),
                pl.BlockSpec((1, T // M_CHUNK, 4 * M_HEADS, M_CHUNK), lambda b: (b, 0, 0, 0))]

    def args(p):
        return [p["mqk"], p["mvt"], p["g"], p["gt"]]

    gate_specs = [row_spec(S)] + ([row_spec(Tc)] if need_ctx else [])
    gate_args = [pl_["mo"]] + ([pc["mo"]] if need_ctx else [])

    out_shape = [jax.ShapeDtypeStruct((B, S, W), jnp.bfloat16)]
    out_specs = [pl.BlockSpec((1, S, W), lambda b: (b, 0, 0))]
    scratch = [pltpu.VMEM((S, W), f32), pltpu.VMEM((S, W), f32)]
    if need_ctx:
        out_shape.append(jax.ShapeDtypeStruct((B, Tc, W), jnp.bfloat16))
        out_specs.append(pl.BlockSpec((1, Tc, W), lambda b: (b, 0, 0)))
        scratch += [pltpu.VMEM((Tc, W), f32), pltpu.VMEM((Tc, W), f32)]
    assert (S // M_CHUNK) % 2 == 0 and (Tc // M_CHUNK) % 2 == 0
    n_pairs = M_HEADS // 2
    scratch += [pltpu.VMEM((M_HEADS, M_DV + M_ONES, LANES), f32), pltpu.VMEM((M_HEADS, 8, 2 * M_CHUNK), f32),
                pltpu.VMEM((2, 2, n_pairs, 2, M_CHUNK, 2 * M_CHUNK), f32),
                pltpu.VMEM((2, 2, n_pairs, 8, 2 * M_CHUNK), f32)]
    res = pl.pallas_call(
        functools.partial(_mlstm_kernel, need_ctx=need_ctx),
        grid=(B,),
        in_specs=specs(Tc) + specs(S) + gate_specs,
        out_specs=out_specs,
        out_shape=out_shape,
        scratch_shapes=scratch,
        compiler_params=pltpu.CompilerParams(dimension_semantics=("arbitrary",), vmem_limit_bytes=VMEM_LIMIT),
        name="mlstm",
    )(*args(pc), *args(pl_), *gate_args)
    return (res[0], res[1]) if need_ctx else (res[0], None)


POOL_HALO = 16


def _pool_mix(ext, tok0, pw_ref, ps_ref, seq_len):
    f32 = jnp.float32
    n_ext = ext.shape[0]
    ts = n_ext - 2 * POOL_HALO
    tok = tok0 + lax.broadcasted_iota(jnp.int32, (ts, POOL_GC), 0)

    def shifted(a, d):
        return pltpu.roll(a, (-d) % n_ext, 0)

    outs = []
    for gi, w in enumerate(POOL_WINDOWS):
        s = ext[:, gi * POOL_GC:(gi + 1) * POOL_GC]
        cur = s[POOL_HALO:POOL_HALO + ts]
        s = shifted(s, -1) + s
        span = 2
        while span < w:
            s = shifted(s, -(span // 2)) + shifted(s, span // 2)
            span *= 2
        lo = jnp.clip(tok - w // 2, 0, seq_len - 1)
        hi = jnp.clip(tok - w // 2 + w - 1, 0, seq_len - 1)
        mean = s[POOL_HALO:POOL_HALO + ts] / (hi - lo + 1).astype(f32)
        outs.append(_dot_f32((mean - cur).astype(jnp.bfloat16), pw_ref[gi]))
    return jnp.concatenate(outs, axis=-1) * ps_ref[...]


def _pack_bf16_pairs(h):
    n = h.shape[1] // 2
    return pltpu.pack_elementwise([h[:, :n], h[:, n:]], packed_dtype=jnp.bfloat16)


MERGE_SUB = 256


def _merge_kernel(m_ref, n_ref, pu_ref, prev_ref, next_ref, bg_ref, x_ref, mods_ref, gains_ref,
                  pw_ref, ps_ref, wb_ref, wo_ref, wrt_ref, wrp_ref, x1_ref, h2p_ref, affr_ref, aff_ref,
                  ext_ref, y_ref, *, seq_len):
    f32, bf16 = jnp.float32, jnp.bfloat16
    i = pl.program_id(1)
    tm, D = x_ref.shape[1], x_ref.shape[2]
    E = wrt_ref.shape[0]
    ts = min(MERGE_SUB, tm)
    ext_ref[0:POOL_HALO] = jnp.where(i > 0, prev_ref[0].astype(f32), 0.0)
    ext_ref[POOL_HALO:POOL_HALO + tm] = pu_ref[0].astype(f32)
    ext_ref[POOL_HALO + tm:] = jnp.where(i < pl.num_programs(1) - 1, next_ref[0].astype(f32), 0.0)

    def mix_stage(k):
        rows = slice(k * ts, (k + 1) * ts)
        p = _pool_mix(ext_ref[k * ts:(k + 1) * ts + 2 * POOL_HALO], i * tm + k * ts, pw_ref, ps_ref, seq_len)
        acc = None
        for bi, br in enumerate((m_ref[0, rows], p.astype(bf16), n_ref[0, rows])):
            gate = jnp.tanh(bg_ref[0, rows, bi * D:(bi + 1) * D].astype(f32)) + 1.0
            term = gate * _dot_f32(br, wb_ref[bi])
            acc = term if acc is None else acc + term
        y_ref[k % 2] = _dot_f32(acc.astype(bf16), wo_ref[...])

    def out_stage(k):
        rows = slice(k * ts, (k + 1) * ts)
        y = y_ref[k % 2]
        yn = y * lax.rsqrt(jnp.mean(y * y, axis=-1, keepdims=True) + RMS_EPS) * gains_ref[0:1, :]
        x1 = x_ref[0, rows] + mods_ref[0, 0:1, :] * yn
        x1_ref[0, rows] = x1
        xn = x1 * lax.rsqrt(jnp.mean(x1 * x1, axis=-1, keepdims=True) + RMS_EPS) * gains_ref[1:2, :]
        h2 = xn * (1.0 + mods_ref[0, 2:3, :]) + mods_ref[0, 1:2, :]
        h2b = h2.astype(bf16)
        lg_t = lax.dot_general(wrt_ref[...], h2b, (((1,), (1,)), ((), ())), preferred_element_type=f32)
        e_t = jnp.exp(lg_t - lg_t.max(axis=0, keepdims=True))
        aff_ref[0, :, rows] = e_t / e_t.sum(axis=0, keepdims=True)
        lg = _dot_f32(h2b, wrp_ref[...])
        lg = jnp.where(lax.broadcasted_iota(jnp.int32, lg.shape, 1) < E, lg, MASK_NEG)
        e_r = jnp.exp(lg - lg.max(axis=-1, keepdims=True))
        affr_ref[0, rows] = e_r / e_r.sum(axis=-1, keepdims=True)
        h2p_ref[0, rows] = _pack_bf16_pairs(h2)

    n_sub = tm // ts
    mix_stage(0)
    for k in range(n_sub):
        if k + 1 < n_sub:
            mix_stage(k + 1)
        out_stage(k)


def merge_out(m, n, pu, bg, x, mods, gains, pool_w, pool_scale, w_branch, w_out, w_router_t, w_router_pad, *, tm):
    B, T, D = x.shape
    W = m.shape[2]
    E = w_router_t.shape[0]
    hb = tm // POOL_HALO
    n_halo = T // POOL_HALO
    tile = lambda w: pl.BlockSpec((1, tm, w), lambda b, i: (b, i, 0))
    in_specs = [
        tile(W), tile(W), tile(W),
        pl.BlockSpec((1, POOL_HALO, W), lambda b, i: (b, jnp.maximum(i * hb - 1, 0), 0)),
        pl.BlockSpec((1, POOL_HALO, W), lambda b, i: (b, jnp.minimum((i + 1) * hb, n_halo - 1), 0)),
        tile(N_BRANCH * D), tile(D),
        pl.BlockSpec((1, 8, D), lambda b, i: (b, 0, 0)),
        _const_spec((8, D)),
        _const_spec(pool_w.shape), _const_spec(pool_scale.shape), _const_spec(w_branch.shape),
        _const_spec(w_out.shape), _const_spec(w_router_t.shape), _const_spec(w_router_pad.shape),
    ]
    return pl.pallas_call(
        functools.partial(_merge_kernel, seq_len=T),
        grid=(B, T // tm),
        in_specs=in_specs,
        out_specs=[tile(D), tile(D // 2), tile(LANES), pl.BlockSpec((1, E, tm), lambda b, i: (b, 0, i))],
        out_shape=[jax.ShapeDtypeStruct((B, T, D), jnp.float32),
                   jax.ShapeDtypeStruct((B, T, D // 2), PACK_WORD),
                   jax.ShapeDtypeStruct((B, T, LANES), jnp.float32),
                   jax.ShapeDtypeStruct((B, E, T), jnp.float32)],
        scratch_shapes=[pltpu.VMEM((tm + 2 * POOL_HALO, W), jnp.float32),
                        pltpu.VMEM((2, min(MERGE_SUB, tm), D), jnp.float32)],
        compiler_params=pltpu.CompilerParams(
            dimension_semantics=("arbitrary", "arbitrary"), vmem_limit_bytes=VMEM_LIMIT),
        name="merge_out",
    )(m, n, pu, pu, pu, bg, x, mods, gains, pool_w, pool_scale, w_branch, w_out, w_router_t, w_router_pad)


def _cumsum_chunks(x01):
    R, T = x01.shape
    r = lax.broadcasted_iota(jnp.int32, (LANES, LANES), 0)
    c = lax.broadcasted_iota(jnp.int32, (LANES, LANES), 1)
    tri_u = jnp.where(r <= c, 1.0, 0.0).astype(jnp.bfloat16)
    run = jnp.zeros((R, 1), jnp.float32)
    outs = []
    for t0 in range(0, T, LANES):
        cs = _dot_f32(x01[:, t0:t0 + LANES].astype(jnp.bfloat16), tri_u) + run
        run = cs[:, LANES - 1:LANES]
        outs.append(cs)
    return outs


def _route_kernel(aff_ref, idx_ref, *, cap):
    f32, bf16 = jnp.float32, jnp.bfloat16
    aff = aff_ref[0]
    E, T = aff.shape
    n_chunks = T // LANES

    def count_ge(cand):
        return jnp.sum(jnp.where(aff >= pltpu.bitcast(cand, f32), 1.0, 0.0), axis=-1, keepdims=True)

    top = jnp.full((E, 1), 1 << 30, jnp.int32)
    lo0 = jnp.where(count_ge(top) >= cap, top, jnp.zeros((E, 1), jnp.int32))

    def search(k, lo):
        shift = 28 - 2 * k
        best = lo
        for q in (1, 2, 3):
            cand = lo | (jnp.int32(q) << shift)
            best = jnp.where(count_ge(cand) >= cap, cand, best)
        return best

    thr = pltpu.bitcast(lax.fori_loop(0, 15, search, lo0), f32)
    gt = jnp.where(aff > thr, 1.0, 0.0)
    eq = jnp.where(aff == thr, 1.0, 0.0)
    room = cap - jnp.sum(gt, axis=-1, keepdims=True)
    sel = gt + eq * jnp.where(jnp.concatenate(_cumsum_chunks(eq), axis=-1) <= room, 1.0, 0.0)
    ranks = [jnp.minimum(cs, float(cap)) for cs in _cumsum_chunks(sel)]
    sel_r = lax.broadcasted_iota(jnp.int32, (T, LANES), 0)
    sel_c = lax.broadcasted_iota(jnp.int32, (T, LANES), 1)
    pick_last = jnp.where(sel_r == sel_c * LANES + (LANES - 1), 1.0, 0.0).astype(bf16)
    chunk_end = _dot_f32(jnp.concatenate(ranks, axis=-1).astype(bf16), pick_last)
    slot = lax.broadcasted_iota(jnp.int32, (cap, LANES), 0).astype(f32)
    lane = lax.broadcasted_iota(jnp.int32, (cap, LANES), 1)
    pad = jnp.zeros((LANES - n_chunks, LANES), f32)
    out = jnp.zeros((cap, LANES), f32)
    for e in range(E):
        ends = jnp.where(lane < n_chunks, jnp.broadcast_to(chunk_end[e:e + 1, :], (cap, LANES)), float(cap))
        n_full = jnp.sum(jnp.where(ends <= slot, 1.0, 0.0), axis=-1, keepdims=True)
        chunk_ranks = jnp.concatenate([r[e:e + 1, :] for r in ranks] + [pad], axis=0)
        mine = _dot_f32(jnp.where(lane.astype(f32) == n_full, 1.0, 0.0).astype(bf16), chunk_ranks.astype(bf16))
        pos = LANES * n_full + jnp.sum(jnp.where(mine <= slot, 1.0, 0.0), axis=-1, keepdims=True)
        out = jnp.where(lane == e, jnp.broadcast_to(pos, (cap, LANES)), out)
    idx_ref[0] = out.astype(jnp.int32)


def route(aff_t, cap):
    B, E, T = aff_t.shape
    assert cap <= 256 and cap % 8 == 0 and T % LANES == 0
    idx_t = pl.pallas_call(
        functools.partial(_route_kernel, cap=cap),
        grid=(B,),
        in_specs=[pl.BlockSpec((1, E, T), lambda b: (b, 0, 0))],
        out_specs=pl.BlockSpec((1, cap, LANES), lambda b: (b, 0, 0)),
        out_shape=jax.ShapeDtypeStruct((B, cap, LANES), jnp.int32),
        compiler_params=pltpu.CompilerParams(dimension_semantics=("arbitrary",), vmem_limit_bytes=VMEM_LIMIT),
        name="route",
    )(aff_t)
    return idx_t[:, :, :E].transpose(0, 2, 1)


ROW_GROUP = 8


def _moe_kernel(idx_ref, rows_ref, affr_ref, wg_ref, wu_ref, wd_ref, out_ref, xg_ref, ag_ref, ye_ref):
    f32, bf16 = jnp.float32, jnp.bfloat16
    e = pl.program_id(1)
    n_exp = pl.num_programs(1)
    n_sets, cap = xg_ref.shape[1], xg_ref.shape[2]

    def gather_row(p, s, ee, j):
        i = idx_ref[0, s, ee, j]
        xg_ref[p, s, pl.ds(j, 1), :] = rows_ref[s, pl.ds(i, 1), :]
        ag_ref[p, s, pl.ds(j, 1), :] = affr_ref[s, pl.ds(i, 1), :]

    def scatter_group(p, s, ee, j0):
        ids = [idx_ref[0, s, ee, j0 + r] for r in range(ROW_GROUP)]
        cur = [out_ref[s, pl.ds(i, 1), :] for i in ids]
        for r, i in enumerate(ids):
            out_ref[s, pl.ds(i, 1), :] = cur[r] + ye_ref[p, s, pl.ds(j0 + r, 1), :]

    @pl.when(e == 0)
    def _():
        out_ref[...] = jnp.zeros_like(out_ref)
        ye_ref[1] = jnp.zeros_like(ye_ref[1])

        def body(gi, carry):
            for s in range(n_sets):
                for r in range(ROW_GROUP):
                    gather_row(0, s, 0, gi * ROW_GROUP + r)
            return carry
        lax.fori_loop(0, cap // ROW_GROUP, body, 0)

    def step(p):
        e_prv = jnp.maximum(e - 1, 0)
        e_nxt = jnp.minimum(e + 1, n_exp - 1)
        for s in range(n_sets):
            for j0 in range(0, cap, ROW_GROUP):
                scatter_group(1 - p, s, e_prv, j0)
            for j in range(cap):
                gather_row(1 - p, s, e_nxt, j)
        packed = jnp.concatenate([xg_ref[p, s] for s in range(n_sets)], axis=0)
        xe = jnp.concatenate(
            [pltpu.unpack_elementwise(packed, index=i, packed_dtype=bf16, unpacked_dtype=f32) for i in (0, 1)],
            axis=-1).astype(bf16)
        aff = jnp.concatenate([ag_ref[p, s] for s in range(n_sets)], axis=0)
        gate = jnp.sum(jnp.where(lax.broadcasted_iota(jnp.int32, aff.shape, 1) == e, aff, 0.0),
                       axis=-1, keepdims=True)
        a = _dot_f32(xe, wg_ref[0])
        hid = (a * _sigmoid(a) * _dot_f32(xe, wu_ref[0])).astype(bf16)
        ye = _dot_f32(hid, wd_ref[0]) * gate
        for s in range(n_sets):
            ye_ref[p, s] = ye[s * cap:(s + 1) * cap]

    @pl.when(e % 2 == 0)
    def _():
        step(0)

    @pl.when(e % 2 == 1)
    def _():
        step(1)

    @pl.when(e == n_exp - 1)
    def _():
        def body(gi, carry):
            for s in range(n_sets):
                scatter_group(1, s, e, gi * ROW_GROUP)
            return carry
        lax.fori_loop(0, cap // ROW_GROUP, body, 0)


MOE_SETS = 1


def moe_experts(rows, aff_rows, idx, w_gate, w_up, w_down, layer):
    G, T, RW = rows.shape
    AW = aff_rows.shape[2]
    _, E, cap = idx.shape
    D = w_gate.shape[2]
    ns = MOE_SETS
    assert E % 2 == 0 and cap % ROW_GROUP == 0 and G % ns == 0
    wspec = pl.BlockSpec((None, 1, D, D), lambda g, e: (layer, e, 0, 0))
    return pl.pallas_call(
        _moe_kernel,
        grid=(G // ns, E),
        in_specs=[pl.BlockSpec((1, ns, E, cap), lambda g, e: (g, 0, 0, 0), memory_space=pltpu.SMEM),
                  pl.BlockSpec((ns, T, RW), lambda g, e: (g, 0, 0)),
                  pl.BlockSpec((ns, T, AW), lambda g, e: (g, 0, 0)),
                  wspec, wspec, wspec],
        out_specs=pl.BlockSpec((ns, T, D), lambda g, e: (g, 0, 0)),
        out_shape=jax.ShapeDtypeStruct((G, T, D), jnp.float32),
        scratch_shapes=[pltpu.VMEM((2, ns, cap, RW), rows.dtype), pltpu.VMEM((2, ns, cap, AW), jnp.float32),
                        pltpu.VMEM((2, ns, cap, D), jnp.float32)],
        compiler_params=pltpu.CompilerParams(
            dimension_semantics=("arbitrary", "arbitrary"), vmem_limit_bytes=VMEM_LIMIT),
        name="moe_experts",
    )(idx.reshape(G // ns, ns, E, cap), rows, aff_rows, w_gate, w_up, w_down)


def _residual_norm_kernel(x_ref, y_ref, gate_ref, gain_ref, o_ref):
    y = y_ref[0]
    yn = y * lax.rsqrt(jnp.mean(y * y, axis=-1, keepdims=True) + RMS_EPS) * gain_ref[...]
    o_ref[0] = x_ref[0] + gate_ref[0] * yn


def residual_norm(x, y, gate, gain, *, tm):
    B, T, D = x.shape
    tile = pl.BlockSpec((1, tm, D), lambda b, i: (b, i, 0))
    return pl.pallas_call(
        _residual_norm_kernel,
        grid=(B, T // tm),
        in_specs=[tile, tile, pl.BlockSpec((1, 1, D), lambda b, i: (b, 0, 0)), _const_spec((1, D))],
        out_specs=tile,
        out_shape=jax.ShapeDtypeStruct((B, T, D), jnp.float32),
        compiler_params=pltpu.CompilerParams(
            dimension_semantics=("arbitrary", "arbitrary"), vmem_limit_bytes=VMEM_LIMIT),
        name="residual_norm",
    )(x, y, gate, gain)


ADA_COLS = 1536


def _ada_kernel(c_ref, w_ref, b_ref, o_ref):
    c = c_ref[...]
    h = (c * _sigmoid(c)).astype(jnp.bfloat16)
    o_ref[...] = _dot_f32(h, w_ref[...].astype(jnp.bfloat16)) + b_ref[...]


def ada_modulation(cond, ada_w, ada_b, layer):
    R, D = cond.shape
    N = ada_w.shape[2]
    return pl.pallas_call(
        _ada_kernel,
        grid=(N // ADA_COLS,),
        in_specs=[pl.BlockSpec((R, D), lambda j: (0, 0)),
                  pl.BlockSpec((None, D, ADA_COLS), lambda j: (layer, 0, j)),
                  pl.BlockSpec((None, 1, ADA_COLS), lambda j: (layer, 0, j))],
        out_specs=pl.BlockSpec((R, ADA_COLS), lambda j: (0, j)),
        out_shape=jax.ShapeDtypeStruct((R, N), jnp.float32),
        compiler_params=pltpu.CompilerParams(dimension_semantics=("arbitrary",), vmem_limit_bytes=VMEM_LIMIT),
        name="ada_modulation",
    )(cond, ada_w, ada_b[:, None, :])


def kernel(x, c, ctx, c_ctx, norm_gain, ada_w, ada_b, w_in, mlstm_gate_bias, pool_w, pool_scale,
           na_rpb, w_branch, w_out, router_w, w_gate, w_up, w_down):
    B, S, D = x.shape
    Tc = ctx.shape[1]
    f32, bf16 = jnp.float32, jnp.bfloat16
    rope_tabs = rope_lane_tables(S)
    names = [n for n, _ in IN_GROUPS] + ["gt"]
    cond = jnp.pad(jnp.concatenate([c, c_ctx[None]], axis=0), ((0, (-(B + 1)) % 8), (0, 0)))
    moe_w = (w_gate.astype(bf16), w_up.astype(bf16), w_down.astype(bf16))
    assert B % (S // Tc) == 0
    pending = pending_c = None
    for l in range(DEPTH):
        need_ctx = l < DEPTH - 1
        g = norm_gain[l]
        mods = ada_modulation(cond, ada_w, ada_b, l)
        mod_l = jnp.split(mods[:B, None, :], N_MOD, axis=-1)
        mod_cb = [jnp.broadcast_to(m[None, None], (B, 1, D)) for m in jnp.split(mods[B], N_MOD, axis=-1)]
        w_perm = permute_w_in(w_in[l])
        gb_row = jnp.pad(mlstm_gate_bias[l].reshape(1, -1).astype(f32), ((0, 0), (0, GATE_W - 4 * M_HEADS)))
        proj = in_proj(x, g[0][None], mod_l[0], mod_l[1], w_perm, gb_row, rope_tabs, pending, tm=512)
        if pending is not None:
            x = proj[-1]
        pl_ = dict(zip(names, proj))
        keep_c = None if need_ctx else ("mqk", "mvt", "kn", "vn", "g")
        proj_c = in_proj(ctx, g[0][None], mod_cb[0], mod_cb[1], w_perm, gb_row, None, pending_c, keep_c, tm=Tc)
        if pending_c is not None:
            ctx = proj_c[-1]
        pc_ = dict(zip(names if keep_c is None else list(keep_c) + ["gt"], proj_c))

        m_l, m_c = mlstm(pc_, pl_, need_ctx)
        n_l, n_c = neighbourhood_attention(pl_["qn"], pl_["kn"], pl_["vn"], pc_["kn"], pc_["vn"],
                                           na_bias_table(na_rpb[l]), pc_["qn"] if need_ctx else None)
        gains = jnp.pad(g[1:3], ((0, 6), (0, 0)))
        merge_w = (pool_w[l].astype(bf16), pool_scale[l][None], w_branch[l].astype(bf16),
                   (0.5 * w_out[l]).astype(bf16),
                   router_w[l].T.astype(bf16),
                   jnp.pad(router_w[l], ((0, 0), (0, LANES - N_EXPERTS))).astype(bf16))

        def mods8(mods):
            return jnp.pad(jnp.concatenate([mods[2], mods[3], mods[4]], axis=1), ((0, 0), (0, 5), (0, 0)))

        x1, rows, aff_r, aff_t = merge_out(m_l, n_l, pl_["pu"], pl_["bg"], x, mods8(mod_l), gains, *merge_w, tm=1024)
        y = moe_experts(rows, aff_r, route(aff_t, EC_FACTOR * S // N_EXPERTS), *moe_w, l)
        if need_ctx:
            x, pending = x1, (y, mod_l[5], g[3][None])
        else:
            x = residual_norm(x1, y, mod_l[5], g[3][None], tm=512)
        if need_ctx:
            c1, rows, aff_r, aff_t = merge_out(m_c, n_c, pc_["pu"], pc_["bg"], ctx, mods8(mod_cb), gains, *merge_w, tm=Tc)
            per = S // Tc
            cap_c = EC_FACTOR * Tc // N_EXPERTS
            idx = route(aff_t, cap_c) + (jnp.arange(B, dtype=jnp.int32) % per * Tc)[:, None, None]
            idx = idx.reshape(B // per, per, N_EXPERTS, cap_c).transpose(0, 2, 1, 3).reshape(B // per, N_EXPERTS, per * cap_c)
            y = moe_experts(rows.reshape(B // per, S, D // 2), aff_r.reshape(B // per, S, LANES), idx,
                            *moe_w, l).reshape(B, Tc, D)
            ctx, pending_c = c1, (y, mod_cb[5], g[3][None])
    return x
```

```python
import functools

import jax
import jax.numpy as jnp
import numpy as np
from jax import lax
from jax.experimental import pallas as pl
from jax.experimental.pallas import tpu as pltpu

D_MODEL = 1024
DEPTH = 2
GRID_W = 64
BRANCH_W = D_MODEL // 2
N_BRANCH = 3
M_HEADS = 4
M_DV = BRANCH_W // M_HEADS
M_DQK = M_DV // 2
M_CHUNK = 128
M_INIT = -1e30
M_ONES = 16
POOL_GROUPS = 4
POOL_GC = BRANCH_W // POOL_GROUPS
POOL_WINDOWS = (2, 4, 8, 16)
NA_HEADS = 8
NA_DH = BRANCH_W // NA_HEADS
NA_KH = 8
NA_KW = 16
N_EXPERTS = 16
EC_FACTOR = 2
ROPE_BASE = 10000.0
RMS_EPS = 1e-6
N_MOD = 6
SPLIT_SIZES = (M_HEADS * M_DQK, M_HEADS * M_DQK, M_HEADS * M_DV, M_HEADS * M_DV, 4 * M_HEADS,
               POOL_GROUPS * POOL_GC, NA_HEADS * NA_DH, NA_HEADS * NA_DH, NA_HEADS * NA_DH,
               N_BRANCH * D_MODEL)
PROJ_W = sum(SPLIT_SIZES)

LANES = 128
VMEM_LIMIT = 56 * 1024 * 1024
MASK_NEG = -1e30
GATE_W = LANES

IN_GROUPS = (("mqk", 2 * M_HEADS * M_DQK), ("mvt", BRANCH_W), ("mo", BRANCH_W), ("pu", BRANCH_W),
             ("qn", BRANCH_W), ("kn", BRANCH_W), ("vn", BRANCH_W), ("bg", N_BRANCH * D_MODEL),
             ("g", GATE_W))
IN_W = sum(w for _, w in IN_GROUPS)
MM_COLS = 512
PACK_WORD = jnp.uint32


def _sigmoid(x):
    return 0.5 * jnp.tanh(0.5 * x) + 0.5


def _const_spec(shape):
    nd = len(shape)
    return pl.BlockSpec(shape, lambda *_: (0,) * nd, pipeline_mode=pl.Buffered(1))


def _rope_tile(x, cos, sin_signed):
    half = M_DQK // 4
    lane = lax.broadcasted_iota(jnp.int32, x.shape, 1)
    partner = jnp.where((lane % (2 * half)) < half, pltpu.roll(x, LANES - half, 1), pltpu.roll(x, half, 1))
    return x * cos + partner * sin_signed


def _in_proj_kernel(x_ref, g_ref, shift_ref, scale_ref, w_ref, gb_ref, *rest, rope, pending, keep):
    if rope:
        cos_ref, sin_ref = rest[:2]
        rest = rest[2:]
    if pending:
        r_ref, rgate_ref, rgain_ref = rest[:3]
        rest, xo_ref = rest[3:-1], rest[-1]
        r = r_ref[0]
        x = x_ref[0] + rgate_ref[0] * (r * lax.rsqrt(jnp.mean(r * r, axis=-1, keepdims=True) + RMS_EPS)
                                       * rgain_ref[...])
        xo_ref[0] = x
    else:
        x = x_ref[0]
    o_refs, gt_ref = rest[:-1], rest[-1]
    y = x * lax.rsqrt(jnp.mean(x * x, axis=-1, keepdims=True) + RMS_EPS)
    h = ((y * g_ref[...]) * (1.0 + scale_ref[0]) + shift_ref[0]).astype(jnp.bfloat16)
    off = 0
    o_iter = iter(o_refs)
    for name, width in IN_GROUPS:
        if name not in keep:
            off += width
            continue
        o_ref = next(o_iter)
        for c0 in range(0, width, MM_COLS):
            cw = min(MM_COLS, width - c0)
            acc = jnp.dot(h, w_ref[:, off + c0:off + c0 + cw], preferred_element_type=jnp.float32)
            if name == "g":
                acc = acc + gb_ref[...]
                for c in range(acc.shape[0] // M_CHUNK):
                    gt_ref[0, c] = acc[c * M_CHUNK:(c + 1) * M_CHUNK, :].T[:4 * M_HEADS, :]
            if name == "mqk" and rope:
                cos, sin = cos_ref[...], sin_ref[...]
                acc = jnp.concatenate([_rope_tile(acc[:, t:t + LANES], cos, sin) for t in range(0, cw, LANES)], axis=-1)
            if name == "mvt":
                for c in range(acc.shape[0] // M_CHUNK):
                    o_ref[0, c] = acc[c * M_CHUNK:(c + 1) * M_CHUNK, :].T.astype(o_ref.dtype)
                continue
            o_ref[0, :, c0:c0 + cw] = acc.astype(o_ref.dtype)
        off += width


def in_proj(x, g, shift, scale, w_perm, gate_bias_row, rope_tabs=None, pending=None, keep=None, *, tm):
    B, T, D = x.shape
    keep = tuple(n for n, _ in IN_GROUPS) if keep is None else tuple(keep)
    assert "g" in keep
    out_shape = [jax.ShapeDtypeStruct((B, T, w), jnp.float32 if n == "g" else jnp.bfloat16) for n, w in IN_GROUPS]
    out_specs = [pl.BlockSpec((1, tm, w), lambda b, i: (b, i, 0)) for _, w in IN_GROUPS]
    k_mvt = [n for n, _ in IN_GROUPS].index("mvt")
    assert BRANCH_W == MM_COLS and tm % M_CHUNK == 0
    out_shape[k_mvt] = jax.ShapeDtypeStruct((B, T // M_CHUNK, BRANCH_W, M_CHUNK), jnp.bfloat16)
    out_specs[k_mvt] = pl.BlockSpec((1, tm // M_CHUNK, BRANCH_W, M_CHUNK), lambda b, i: (b, i, 0, 0))
    kept = [k for k, (n, _) in enumerate(IN_GROUPS) if n in keep]
    out_shape, out_specs = [out_shape[k] for k in kept], [out_specs[k] for k in kept]
    out_shape.append(jax.ShapeDtypeStruct((B, T // M_CHUNK, 4 * M_HEADS, M_CHUNK), jnp.float32))
    out_specs.append(pl.BlockSpec((1, tm // M_CHUNK, 4 * M_HEADS, M_CHUNK), lambda b, i: (b, i, 0, 0)))
    in_specs = [
        pl.BlockSpec((1, tm, D), lambda b, i: (b, i, 0)),
        _const_spec((1, D)),
        pl.BlockSpec((1, 1, D), lambda b, i: (b, 0, 0)),
        pl.BlockSpec((1, 1, D), lambda b, i: (b, 0, 0)),
        _const_spec((D, IN_W)),
        _const_spec((1, GATE_W)),
    ]
    args = [x, g, shift, scale, w_perm, gate_bias_row]
    if rope_tabs is not None:
        in_specs += [pl.BlockSpec((tm, LANES), lambda b, i: (i, 0))] * 2
        args += list(rope_tabs)
    if pending is not None:
        in_specs += [pl.BlockSpec((1, tm, D), lambda b, i: (b, i, 0)), pl.BlockSpec((1, 1, D), lambda b, i: (b, 0, 0)),
                     _const_spec((1, D))]
        args += list(pending)
        out_shape.append(jax.ShapeDtypeStruct((B, T, D), jnp.float32))
        out_specs.append(pl.BlockSpec((1, tm, D), lambda b, i: (b, i, 0)))
    return pl.pallas_call(
        functools.partial(_in_proj_kernel, rope=rope_tabs is not None, pending=pending is not None, keep=keep),
        grid=(B, T // tm),
        in_specs=in_specs,
        out_specs=out_specs,
        out_shape=out_shape,
        compiler_params=pltpu.CompilerParams(
            dimension_semantics=("arbitrary", "arbitrary"), vmem_limit_bytes=VMEM_LIMIT),
        name="in_proj",
    )(*args)


def rope_lane_tables(n_tokens):
    t = jnp.arange(n_tokens)
    row = (t // GRID_W).astype(jnp.float32)
    col = (t % GRID_W).astype(jnp.float32)
    half = M_DQK // 2
    inv = ROPE_BASE ** (-jnp.arange(0, half, 2, dtype=jnp.float32) / half)
    ar = row[:, None] * inv[None, :]
    ac = col[:, None] * inv[None, :]
    cos = jnp.concatenate([jnp.cos(ar), jnp.cos(ar), jnp.cos(ac), jnp.cos(ac)], axis=-1)
    sin = jnp.concatenate([-jnp.sin(ar), jnp.sin(ar), -jnp.sin(ac), jnp.sin(ac)], axis=-1)
    return jnp.tile(cos, (1, LANES // M_DQK)), jnp.tile(sin, (1, LANES // M_DQK))


def permute_w_in(w_in_l):
    qm, km, vm, om, gm, pu, qn, kn, vn, bg = jnp.split(w_in_l, [int(s) for s in np.cumsum(SPLIT_SIZES)[:-1]], axis=-1)
    gpad = jnp.pad(gm, ((0, 0), (0, GATE_W - gm.shape[1])))
    w = jnp.concatenate([qm * (M_DQK ** -0.5), km, vm, om, pu, qn * (NA_DH ** -0.5), kn, vn, bg * 0.5, gpad], axis=-1)
    return w.astype(jnp.bfloat16)


def _pair_scores(qp, k_parts, biases):
    n = qp.shape[0]
    lane_lo = lax.broadcasted_iota(jnp.int32, (n, LANES), 1) < NA_DH
    zero = jnp.zeros_like(qp)
    qq = jnp.concatenate([jnp.where(lane_lo, qp, zero), jnp.where(lane_lo, zero, qp)], axis=0)
    scores = []
    for kp, bias in zip(k_parts, biases):
        s = lax.dot_general(qq, kp, (((1,), (1,)), ((), ())), preferred_element_type=jnp.float32)
        scores.append(s if bias is None else s + bias)
    return scores


def _pair_softmax_pv(scores, v_parts):
    n = scores[0].shape[0] // 2
    lane_lo = lax.broadcasted_iota(jnp.int32, (n, LANES), 1) < NA_DH
    m = scores[0].max(axis=-1, keepdims=True)
    for s in scores[1:]:
        m = jnp.maximum(m, s.max(axis=-1, keepdims=True))
    l = None
    o = None
    for s, vp in zip(scores, v_parts):
        p = jnp.exp(s - m)
        ls = p.sum(axis=-1, keepdims=True)
        os_ = jnp.dot(p.astype(jnp.bfloat16), vp, preferred_element_type=jnp.float32)
        l = ls if l is None else l + ls
        o = os_ if o is None else o + os_
    o = o * (1.0 / l)
    return jnp.where(lane_lo, o[:n], o[n:])


def _na_kernel(q_ref, k_ref, v_ref, kc_ref, vc_ref, bias_ref, *rest, need_ctx):
    if need_ctx:
        qc_ref, o_ref, oc_ref, s_ref = rest
    else:
        o_ref, s_ref = rest
    S = q_ref.shape[1]
    Tc = kc_ref.shape[1]
    rows = S // GRID_W
    n_pairs = BRANCH_W // LANES
    n_loc = NA_KH * GRID_W

    def window(r):
        rs = jnp.clip(r - NA_KH // 2, 0, rows - NA_KH)
        return r - rs, pl.multiple_of(r * GRID_W, GRID_W), pl.multiple_of(rs * GRID_W, GRID_W)

    def scores_stage(r, slot, j):
        var, q0, k0 = window(r)
        ls = slice(j * LANES, (j + 1) * LANES)
        s_loc, s_ctx = _pair_scores(q_ref[0, pl.ds(q0, GRID_W), ls],
                                    (k_ref[0, pl.ds(k0, n_loc), ls], kc_ref[0, :, ls]),
                                    (bias_ref[var, j], None))
        s_ref[slot, j, :, :n_loc] = s_loc
        s_ref[slot, j, :, n_loc:] = s_ctx

    def output_stage(r, slot, j):
        _, q0, k0 = window(r)
        ls = slice(j * LANES, (j + 1) * LANES)
        o = _pair_softmax_pv((s_ref[slot, j, :, :n_loc], s_ref[slot, j, :, n_loc:]),
                             (v_ref[0, pl.ds(k0, n_loc), ls], vc_ref[0, :, ls]))
        o_ref[0, pl.ds(q0, GRID_W), ls] = o.astype(o_ref.dtype)

    for j in range(n_pairs):
        scores_stage(0, 0, j)

    def two_rows(i, carry):
        r0 = 2 * i
        for j in range(n_pairs):
            scores_stage(r0 + 1, 1, j)
            output_stage(r0, 0, j)
        for j in range(n_pairs):
            scores_stage(jnp.minimum(r0 + 2, rows - 1), 0, j)
            output_stage(r0 + 1, 1, j)
        return carry

    lax.fori_loop(0, rows // 2, two_rows, 0)

    if need_ctx:
        outs = []
        for j in range(n_pairs):
            ls = slice(j * LANES, (j + 1) * LANES)
            scores = _pair_scores(qc_ref[0, :, ls], (kc_ref[0, :, ls],), (None,))
            outs.append(_pair_softmax_pv(scores, (vc_ref[0, :, ls],)))
        oc_ref[0] = jnp.concatenate(outs, axis=-1).astype(oc_ref.dtype)


def na_bias_table(rpb):
    H = rpb.shape[0]
    var = jnp.arange(NA_KH)
    kr = jnp.arange(NA_KH)
    dr = kr[None, :] - var[:, None] + NA_KH - 1
    cols = jnp.arange(GRID_W)
    dc = jnp.clip(cols[None, :] - cols[:, None] + NA_KW - 1, 0, 2 * NA_KW - 2)
    cs = jnp.clip(cols - NA_KW // 2, 0, GRID_W - NA_KW)
    colmask = (cols[None, :] >= cs[:, None]) & (cols[None, :] < cs[:, None] + NA_KW)
    pick_r = jax.nn.one_hot(dr, 2 * NA_KH - 1, dtype=jnp.float32)
    pick_c = jax.nn.one_hot(dc, 2 * NA_KW - 1, dtype=jnp.float32)
    tab = jnp.einsum('vka,hab,qcb->vhqkc', pick_r, rpb.astype(jnp.float32), pick_c, precision=lax.Precision.HIGHEST)
    tab = jnp.where(colmask[None, None, :, None, :], tab, MASK_NEG)
    return tab.reshape(NA_KH, H // 2, 2 * GRID_W, NA_KH * GRID_W)


def neighbourhood_attention(qn, kn, vn, kc, vc, bias_tab, qc=None):
    B, S, W = qn.shape
    Tc = kc.shape[1]
    need_ctx = qc is not None
    lat = pl.BlockSpec((1, S, W), lambda b: (b, 0, 0))
    cx = pl.BlockSpec((1, Tc, W), lambda b: (b, 0, 0))
    in_specs = [lat, lat, lat, cx, cx, _const_spec(bias_tab.shape)]
    args = [qn, kn, vn, kc, vc, bias_tab]
    out_shape = [jax.ShapeDtypeStruct((B, S, W), jnp.bfloat16)]
    out_specs = [lat]
    if need_ctx:
        in_specs.append(cx)
        args.append(qc)
        out_shape.append(jax.ShapeDtypeStruct((B, Tc, W), jnp.bfloat16))
        out_specs.append(cx)
    res = pl.pallas_call(
        functools.partial(_na_kernel, need_ctx=need_ctx),
        grid=(B,),
        in_specs=in_specs,
        out_specs=out_specs,
        out_shape=out_shape,
        scratch_shapes=[pltpu.VMEM((2, W // LANES, 2 * GRID_W, NA_KH * GRID_W + Tc), jnp.float32)],
        compiler_params=pltpu.CompilerParams(dimension_semantics=("arbitrary",), vmem_limit_bytes=VMEM_LIMIT),
        name="neighbourhood_attention",
    )(*args)
    return (res[0], res[1]) if need_ctx else (res[0], None)


def _split_bf16(x):
    hi = x.astype(jnp.bfloat16)
    r1 = x - hi.astype(jnp.float32)
    mid = r1.astype(jnp.bfloat16)
    lo = (r1 - mid.astype(jnp.float32)).astype(jnp.bfloat16)
    return hi, mid, lo


def _dot_f32(a, b):
    return jnp.dot(a, b, preferred_element_type=jnp.float32)


def _log_sigmoid(x):
    return jnp.minimum(x, 0.0) - jnp.log(1.0 + jnp.exp(-jnp.abs(x)))


def _pair_queries(qp):
    lane_lo = lax.broadcasted_iota(jnp.int32, qp.shape, 1) < M_DQK
    zq = jnp.zeros_like(qp)
    return jnp.concatenate([jnp.where(lane_lo, qp, zq), jnp.where(lane_lo, zq, qp)], axis=0)


def _mlstm_prep(qk_ref, g_ref, gt_ref, sa_ref, ra_ref, slot, ci, fwd):
    L = M_CHUNK
    f32, bf16 = jnp.float32, jnp.bfloat16
    d = 0 if fwd else 1
    rows = pl.ds(pl.multiple_of(ci * L, L), L)
    sq_r = lax.broadcasted_iota(jnp.int32, (L, L), 0)
    sq_c = lax.broadcasted_iota(jnp.int32, (L, L), 1)
    tri_l = jnp.where(sq_r >= sq_c, 1.0, 0.0).astype(bf16)
    tri_u = jnp.where(sq_r <= sq_c, 1.0, 0.0).astype(bf16)
    gates = g_ref[0, rows, :]
    gates_t = gt_ref[0, ci]
    lf, lf_t = _log_sigmoid(gates), _log_sigmoid(gates_t)
    cum = sum(_dot_f32(tri_l if fwd else tri_u, p) for p in _split_bf16(lf))
    cum_t = sum(_dot_f32(p, tri_u if fwd else tri_l) for p in _split_bf16(lf_t))
    t_i = 0 if fwd else 2
    st_s = lax.broadcasted_iota(jnp.int32, (L, 2 * L), 0)
    st_t = lax.broadcasted_iota(jnp.int32, (L, 2 * L), 1) & (L - 1)
    valid = (st_s <= st_t) if fwd else (st_s >= st_t)

    def row2(tile, c0, c1):
        return jnp.concatenate([tile[c0:c0 + 1, :], tile[c1:c1 + 1, :]], axis=-1)

    for j in range(M_HEADS // 2):
        ci0, ci1 = t_i * M_HEADS + 2 * j, t_i * M_HEADS + 2 * j + 1
        cf0, cf1 = ci0 + M_HEADS, ci1 + M_HEADS
        b_row, i_row = row2(cum_t, cf0, cf1), row2(gates_t, ci0, ci1)
        colb = jnp.concatenate([jnp.broadcast_to(gates[:, ci0:ci0 + 1] - cum[:, cf0:cf0 + 1], (L, L)),
                                jnp.broadcast_to(gates[:, ci1:ci1 + 1] - cum[:, cf1:cf1 + 1], (L, L))],
                               axis=-1)
        qp = qk_ref[0, rows, j * LANES:(j + 1) * LANES]
        kp = qk_ref[0, rows, M_HEADS * M_DQK + j * LANES:M_HEADS * M_DQK + (j + 1) * LANES]
        sa_ref[slot, d, j, 0] = jnp.where(valid, b_row + colb, MASK_NEG)
        sa_ref[slot, d, j, 1] = lax.dot_general(kp, _pair_queries(qp), (((1,), (1,)), ((), ())),
                                                preferred_element_type=f32)
        ra_ref[slot, d, j, 0:1, :] = b_row
        ra_ref[slot, d, j, 1:2, :] = i_row


def _mlstm_step(qk_ref, vt_ref, sa_ref, ra_ref, slot, h_ref, ct_ref, m_ref, ci, fwd):
    L = M_CHUNK
    f32, bf16 = jnp.float32, jnp.bfloat16
    d = 0 if fwd else 1
    rows = pl.ds(pl.multiple_of(ci * L, L), L)
    last = L - 1 if fwd else 0
    lane_lo = lax.broadcasted_iota(jnp.int32, (L, LANES), 1) < M_DQK
    lane_lo_row = lax.broadcasted_iota(jnp.int32, (1, LANES), 1) < M_DQK
    ones = jnp.ones((M_ONES, L), bf16)

    def halves(row, f):
        return jnp.concatenate([jnp.broadcast_to(f(row[:, :L]), (1, L)), jnp.broadcast_to(f(row[:, L:]), (1, L))],
                               axis=-1)

    for j in range(M_HEADS // 2):
        h0, h1 = 2 * j, 2 * j + 1
        sidx = 2 * j + d
        b_row, i_row = ra_ref[slot, d, j, 0:1, :], ra_ref[slot, d, j, 1:2, :]
        m_row = m_ref[sidx, 0:1, :]
        kp = qk_ref[0, rows, M_HEADS * M_DQK + j * LANES:M_HEADS * M_DQK + (j + 1) * LANES]
        vx0 = jnp.concatenate([vt_ref[0, ci, h0 * M_DV:(h0 + 1) * M_DV, :], ones], axis=0)
        vx1 = jnp.concatenate([vt_ref[0, ci, h1 * M_DV:(h1 + 1) * M_DV, :], ones], axis=0)
        ct = ct_ref[sidx]
        if h_ref is not None:
            dm = sa_ref[slot, d, j, 0]
            qq = _pair_queries(qk_ref[0, rows, j * LANES:(j + 1) * LANES])
            g_row = b_row + m_row
            mt = jnp.maximum(g_row, dm.max(axis=0, keepdims=True))
            sc = (sa_ref[slot, d, j, 1] * jnp.exp(dm - mt)).astype(bf16)
            inter = jnp.exp(g_row - mt)
            t1 = lax.dot_general(ct.astype(bf16), qq, (((1,), (1,)), ((), ())), preferred_element_type=f32)
            t2 = jnp.concatenate([_dot_f32(vx0, sc[:, :L]), _dot_f32(vx1, sc[:, L:])], axis=-1)
            tot = inter * t1 + t2
            h_t = tot[:M_DV] * (1.0 / jnp.maximum(jnp.abs(tot[M_DV:M_DV + 1]), jnp.exp(-mt)))
            h_ref[rows, h0 * M_DV:(h0 + 1) * M_DV] = h_t[:, :L].T
            h_ref[rows, h1 * M_DV:(h1 + 1) * M_DV] = h_t[:, L:].T
        bl = halves(b_row, lambda r: r[:, last:last + 1])
        w = bl - b_row + i_row
        m_new = jnp.maximum(bl + m_row, halves(w, lambda r: r.max(axis=-1, keepdims=True)))
        decay = jnp.exp(bl + m_row - m_new)
        ws = jnp.exp(w - m_new)
        vxs = jnp.concatenate([vx0.astype(f32) * ws[:, :L], vx1.astype(f32) * ws[:, L:]], axis=-1).astype(bf16)
        zk = jnp.zeros_like(kp)
        kk = jnp.concatenate([jnp.where(lane_lo, kp, zk), jnp.where(lane_lo, zk, kp)], axis=0)
        d_cols = jnp.where(lane_lo_row, jnp.broadcast_to(decay[:, 0:1], (1, LANES)),
                           jnp.broadcast_to(decay[:, L:L + 1], (1, LANES)))
        ct_ref[sidx] = d_cols * ct + _dot_f32(vxs, kk)
        m_ref[sidx, 0:1, :] = m_new


def _mlstm_finish(hf_ref, hb_ref, o_ref, out_ref, n_chunks):
    L = M_CHUNK

    def body(ci, carry):
        rows = pl.ds(pl.multiple_of(ci * L, L), L)
        h = hf_ref[rows, :] + hb_ref[rows, :]
        parts = []
        for hd in range(M_HEADS):
            hh = h[:, hd * M_DV:(hd + 1) * M_DV]
            parts.append(hh * lax.rsqrt(jnp.mean(hh * hh, axis=-1, keepdims=True) + RMS_EPS))
        gate = _sigmoid(o_ref[0, rows, :].astype(jnp.float32))
        out_ref[0, rows, :] = (gate * jnp.concatenate(parts, axis=-1)).astype(out_ref.dtype)
        return carry

    lax.fori_loop(0, n_chunks, body, 0)


def _mlstm_kernel(qk_c, v_c, g_c, gt_c, qk_l, v_l, g_l, gt_l, o_l, *rest, need_ctx):
    if need_ctx:
        o_c, out_l, out_c, hf_l, hb_l, hf_c, hb_c, ct_ref, m_ref, sa_ref, ra_ref = rest
    else:
        out_l, hf_l, hb_l, ct_ref, m_ref, sa_ref, ra_ref = rest
        hf_c = hb_c = None
    nc, nl = qk_c.shape[1] // M_CHUNK, qk_l.shape[1] // M_CHUNK
    ct_ref[...] = jnp.zeros_like(ct_ref)
    m_ref[...] = jnp.full_like(m_ref, M_INIT)

    def phase(qk, vt, g, gt, hf, hb, n):
        def prep(s, slot):
            _mlstm_prep(qk, g, gt, sa_ref, ra_ref, slot, s, True)
            _mlstm_prep(qk, g, gt, sa_ref, ra_ref, slot, n - 1 - s, False)

        def step(s, slot):
            _mlstm_step(qk, vt, sa_ref, ra_ref, slot, hf, ct_ref, m_ref, s, True)
            _mlstm_step(qk, vt, sa_ref, ra_ref, slot, hb, ct_ref, m_ref, n - 1 - s, False)

        prep(0, 0)

        def body(i, carry):
            s0 = 2 * i
            prep(s0 + 1, 1)
            step(s0, 0)
            prep(jnp.minimum(s0 + 2, n - 1), 0)
            step(s0 + 1, 1)
            return carry
        lax.fori_loop(0, n // 2, body, 0)

    phase(qk_c, v_c, g_c, gt_c, hf_c, hb_c, nc)
    phase(qk_l, v_l, g_l, gt_l, hf_l, hb_l, nl)
    _mlstm_finish(hf_l, hb_l, o_l, out_l, nl)
    if need_ctx:
        _mlstm_finish(hf_c, hb_c, o_c, out_c, nc)


def mlstm(pc, pl_, need_ctx):
    B, S, W = pl_["mo"].shape
    Tc = pc["mqk"].shape[1]
    f32 = jnp.float32

    def row_spec(T):
        return pl.BlockSpec((1, T, W), lambda b: (b, 0, 0))

    def specs(T):
        return [row_spec(T), pl.BlockSpec((1, T // M_CHUNK, W, M_CHUNK), lambda b: (b, 0, 0, 0)),
                pl.BlockSpec((1, T, GATE_W), lambda b: (b, 0, 0)),
                pl.BlockSpec((1, T // M_CHUNK, 4 * M_HEADS, M_CHUNK), lambda b: (b, 0, 0, 0))]

    def args(p):
        return [p["mqk"], p["mvt"], p["g"], p["gt"]]

    gate_specs = [row_spec(S)] + ([row_spec(Tc)] if need_ctx else [])
    gate_args = [pl_["mo"]] + ([pc["mo"]] if need_ctx else [])

    out_shape = [jax.ShapeDtypeStruct((B, S, W), jnp.bfloat16)]
    out_specs = [pl.BlockSpec((1, S, W), lambda b: (b, 0, 0))]
    scratch = [pltpu.VMEM((S, W), f32), pltpu.VMEM((S, W), f32)]
    if need_ctx:
        out_shape.append(jax.ShapeDtypeStruct((B, Tc, W), jnp.bfloat16))
        out_specs.append(pl.BlockSpec((1, Tc, W), lambda b: (b, 0, 0)))
        scratch += [pltpu.VMEM((Tc, W), f32), pltpu.VMEM((Tc, W), f32)]
    assert (S // M_CHUNK) % 2 == 0 and (Tc // M_CHUNK) % 2 == 0
    n_pairs = M_HEADS // 2
    scratch += [pltpu.VMEM((M_HEADS, M_DV + M_ONES, LANES), f32), pltpu.VMEM((M_HEADS, 8, 2 * M_CHUNK), f32),
                pltpu.VMEM((2, 2, n_pairs, 2, M_CHUNK, 2 * M_CHUNK), f32),
                pltpu.VMEM((2, 2, n_pairs, 8, 2 * M_CHUNK), f32)]
    res = pl.pallas_call(
        functools.partial(_mlstm_kernel, need_ctx=need_ctx),
        grid=(B,),
        in_specs=specs(Tc) + specs(S) + gate_specs,
        out_specs=out_specs,
        out_shape=out_shape,
        scratch_shapes=scratch,
        compiler_params=pltpu.CompilerParams(dimension_semantics=("arbitrary",), vmem_limit_bytes=VMEM_LIMIT),
        name="mlstm",
    )(*args(pc), *args(pl_), *gate_args)
    return (res[0], res[1]) if need_ctx else (res[0], None)


POOL_HALO = 16


def _pool_mix(ext, tok0, pw_ref, ps_ref, seq_len):
    f32 = jnp.float32
    n_ext = ext.shape[0]
    ts = n_ext - 2 * POOL_HALO
    tok = tok0 + lax.broadcasted_iota(jnp.int32, (ts, POOL_GC), 0)

    def shifted(a, d):
        return pltpu.roll(a, (-d) % n_ext, 0)

    outs = []
    for gi, w in enumerate(POOL_WINDOWS):
        s = ext[:, gi * POOL_GC:(gi + 1) * POOL_GC]
        cur = s[POOL_HALO:POOL_HALO + ts]
        s = shifted(s, -1) + s
        span = 2
        while span < w:
            s = shifted(s, -(span // 2)) + shifted(s, span // 2)
            span *= 2
        lo = jnp.clip(tok - w // 2, 0, seq_len - 1)
        hi = jnp.clip(tok - w // 2 + w - 1, 0, seq_len - 1)
        mean = s[POOL_HALO:POOL_HALO + ts] / (hi - lo + 1).astype(f32)
        outs.append(_dot_f32((mean - cur).astype(jnp.bfloat16), pw_ref[gi]))
    return jnp.concatenate(outs, axis=-1) * ps_ref[...]


def _pack_bf16_pairs(h):
    n = h.shape[1] // 2
    return pltpu.pack_elementwise([h[:, :n], h[:, n:]], packed_dtype=jnp.bfloat16)


MERGE_SUB = 256


def _merge_kernel(m_ref, n_ref, pu_ref, prev_ref, next_ref, bg_ref, x_ref, mods_ref, gains_ref,
                  pw_ref, ps_ref, wb_ref, wo_ref, wrt_ref, wrp_ref, x1_ref, h2p_ref, affr_ref, aff_ref,
                  ext_ref, y_ref, *, seq_len):
    f32, bf16 = jnp.float32, jnp.bfloat16
    i = pl.program_id(1)
    tm, D = x_ref.shape[1], x_ref.shape[2]
    E = wrt_ref.shape[0]
    ts = min(MERGE_SUB, tm)
    ext_ref[0:POOL_HALO] = jnp.where(i > 0, prev_ref[0].astype(f32), 0.0)
    ext_ref[POOL_HALO:POOL_HALO + tm] = pu_ref[0].astype(f32)
    ext_ref[POOL_HALO + tm:] = jnp.where(i < pl.num_programs(1) - 1, next_ref[0].astype(f32), 0.0)

    def mix_stage(k):
        rows = slice(k * ts, (k + 1) * ts)
        p = _pool_mix(ext_ref[k * ts:(k + 1) * ts + 2 * POOL_HALO], i * tm + k * ts, pw_ref, ps_ref, seq_len)
        acc = None
        for bi, br in enumerate((m_ref[0, rows], p.astype(bf16), n_ref[0, rows])):
            gate = jnp.tanh(bg_ref[0, rows, bi * D:(bi + 1) * D].astype(f32)) + 1.0
            term = gate * _dot_f32(br, wb_ref[bi])
            acc = term if acc is None else acc + term
        y_ref[k % 2] = _dot_f32(acc.astype(bf16), wo_ref[...])

    def out_stage(k):
        rows = slice(k * ts, (k + 1) * ts)
        y = y_ref[k % 2]
        yn = y * lax.rsqrt(jnp.mean(y * y, axis=-1, keepdims=True) + RMS_EPS) * gains_ref[0:1, :]
        x1 = x_ref[0, rows] + mods_ref[0, 0:1, :] * yn
        x1_ref[0, rows] = x1
        xn = x1 * lax.rsqrt(jnp.mean(x1 * x1, axis=-1, keepdims=True) + RMS_EPS) * gains_ref[1:2, :]
        h2 = xn * (1.0 + mods_ref[0, 2:3, :]) + mods_ref[0, 1:2, :]
        h2b = h2.astype(bf16)
        lg_t = lax.dot_general(wrt_ref[...], h2b, (((1,), (1,)), ((), ())), preferred_element_type=f32)
        e_t = jnp.exp(lg_t - lg_t.max(axis=0, keepdims=True))
        aff_ref[0, :, rows] = e_t / e_t.sum(axis=0, keepdims=True)
        lg = _dot_f32(h2b, wrp_ref[...])
        lg = jnp.where(lax.broadcasted_iota(jnp.int32, lg.shape, 1) < E, lg, MASK_NEG)
        e_r = jnp.exp(lg - lg.max(axis=-1, keepdims=True))
        affr_ref[0, rows] = e_r / e_r.sum(axis=-1, keepdims=True)
        h2p_ref[0, rows] = _pack_bf16_pairs(h2)

    n_sub = tm // ts
    mix_stage(0)
    for k in range(n_sub):
        if k + 1 < n_sub:
            mix_stage(k + 1)
        out_stage(k)


def merge_out(m, n, pu, bg, x, mods, gains, pool_w, pool_scale, w_branch, w_out, w_router_t, w_router_pad, *, tm):
    B, T, D = x.shape
    W = m.shape[2]
    E = w_router_t.shape[0]
    hb = tm // POOL_HALO
    n_halo = T // POOL_HALO
    tile = lambda w: pl.BlockSpec((1, tm, w), lambda b, i: (b, i, 0))
    in_specs = [
        tile(W), tile(W), tile(W),
        pl.BlockSpec((1, POOL_HALO, W), lambda b, i: (b, jnp.maximum(i * hb - 1, 0), 0)),
        pl.BlockSpec((1, POOL_HALO, W), lambda b, i: (b, jnp.minimum((i + 1) * hb, n_halo - 1), 0)),
        tile(N_BRANCH * D), tile(D),
        pl.BlockSpec((1, 8, D), lambda b, i: (b, 0, 0)),
        _const_spec((8, D)),
        _const_spec(pool_w.shape), _const_spec(pool_scale.shape), _const_spec(w_branch.shape),
        _const_spec(w_out.shape), _const_spec(w_router_t.shape), _const_spec(w_router_pad.shape),
    ]
    return pl.pallas_call(
        functools.partial(_merge_kernel, seq_len=T),
        grid=(B, T // tm),
        in_specs=in_specs,
        out_specs=[tile(D), tile(D // 2), tile(LANES), pl.BlockSpec((1, E, tm), lambda b, i: (b, 0, i))],
        out_shape=[jax.ShapeDtypeStruct((B, T, D), jnp.float32),
                   jax.ShapeDtypeStruct((B, T, D // 2), PACK_WORD),
                   jax.ShapeDtypeStruct((B, T, LANES), jnp.float32),
                   jax.ShapeDtypeStruct((B, E, T), jnp.float32)],
        scratch_shapes=[pltpu.VMEM((tm + 2 * POOL_HALO, W), jnp.float32),
                        pltpu.VMEM((2, min(MERGE_SUB, tm), D), jnp.float32)],
        compiler_params=pltpu.CompilerParams(
            dimension_semantics=("arbitrary", "arbitrary"), vmem_limit_bytes=VMEM_LIMIT),
        name="merge_out",
    )(m, n, pu, pu, pu, bg, x, mods, gains, pool_w, pool_scale, w_branch, w_out, w_router_t, w_router_pad)


def _cumsum_chunks(x01):
    R, T = x01.shape
    r = lax.broadcasted_iota(jnp.int32, (LANES, LANES), 0)
    c = lax.broadcasted_iota(jnp.int32, (LANES, LANES), 1)
    tri_u = jnp.where(r <= c, 1.0, 0.0).astype(jnp.bfloat16)
    run = jnp.zeros((R, 1), jnp.float32)
    outs = []
    for t0 in range(0, T, LANES):
        cs = _dot_f32(x01[:, t0:t0 + LANES].astype(jnp.bfloat16), tri_u) + run
        run = cs[:, LANES - 1:LANES]
        outs.append(cs)
    return outs


def _route_kernel(aff_ref, idx_ref, *, cap):
    f32, bf16 = jnp.float32, jnp.bfloat16
    aff = aff_ref[0]
    E, T = aff.shape
    n_chunks = T // LANES

    def count_ge(cand):
        return jnp.sum(jnp.where(aff >= pltpu.bitcast(cand, f32), 1.0, 0.0), axis=-1, keepdims=True)

    top = jnp.full((E, 1), 1 << 30, jnp.int32)
    lo0 = jnp.where(count_ge(top) >= cap, top, jnp.zeros((E, 1), jnp.int32))

    def search(k, lo):
        shift = 28 - 2 * k
        best = lo
        for q in (1, 2, 3):
            cand = lo | (jnp.int32(q) << shift)
            best = jnp.where(count_ge(cand) >= cap, cand, best)
        return best

    thr = pltpu.bitcast(lax.fori_loop(0, 15, search, lo0), f32)
    gt = jnp.where(aff > thr, 1.0, 0.0)
    eq = jnp.where(aff == thr, 1.0, 0.0)
    room = cap - jnp.sum(gt, axis=-1, keepdims=True)
    sel = gt + eq * jnp.where(jnp.concatenate(_cumsum_chunks(eq), axis=-1) <= room, 1.0, 0.0)
    ranks = [jnp.minimum(cs, float(cap)) for cs in _cumsum_chunks(sel)]
    sel_r = lax.broadcasted_iota(jnp.int32, (T, LANES), 0)
    sel_c = lax.broadcasted_iota(jnp.int32, (T, LANES), 1)
    pick_last = jnp.where(sel_r == sel_c * LANES + (LANES - 1), 1.0, 0.0).astype(bf16)
    chunk_end = _dot_f32(jnp.concatenate(ranks, axis=-1).astype(bf16), pick_last)
    slot = lax.broadcasted_iota(jnp.int32, (cap, LANES), 0).astype(f32)
    lane = lax.broadcasted_iota(jnp.int32, (cap, LANES), 1)
    pad = jnp.zeros((LANES - n_chunks, LANES), f32)
    out = jnp.zeros((cap, LANES), f32)
    for e in range(E):
        ends = jnp.where(lane < n_chunks, jnp.broadcast_to(chunk_end[e:e + 1, :], (cap, LANES)), float(cap))
        n_full = jnp.sum(jnp.where(ends <= slot, 1.0, 0.0), axis=-1, keepdims=True)
        chunk_ranks = jnp.concatenate([r[e:e + 1, :] for r in ranks] + [pad], axis=0)
        mine = _dot_f32(jnp.where(lane.astype(f32) == n_full, 1.0, 0.0).astype(bf16), chunk_ranks.astype(bf16))
        pos = LANES * n_full + jnp.sum(jnp.where(mine <= slot, 1.0, 0.0), axis=-1, keepdims=True)
        out = jnp.where(lane == e, jnp.broadcast_to(pos, (cap, LANES)), out)
    idx_ref[0] = out.astype(jnp.int32)


def route(aff_t, cap):
    B, E, T = aff_t.shape
    assert cap <= 256 and cap % 8 == 0 and T % LANES == 0
    idx_t = pl.pallas_call(
        functools.partial(_route_kernel, cap=cap),
        grid=(B,),
        in_specs=[pl.BlockSpec((1, E, T), lambda b: (b, 0, 0))],
        out_specs=pl.BlockSpec((1, cap, LANES), lambda b: (b, 0, 0)),
        out_shape=jax.ShapeDtypeStruct((B, cap, LANES), jnp.int32),
        compiler_params=pltpu.CompilerParams(dimension_semantics=("arbitrary",), vmem_limit_bytes=VMEM_LIMIT),
        name="route",
    )(aff_t)
    return idx_t[:, :, :E].transpose(0, 2, 1)


ROW_GROUP = 8
RESIDUAL_ROWS = 256


def _moe_kernel(idx_ref, rows_ref, affr_ref, wg_ref, wu_ref, wd_ref, *rest, residual):
    if residual:
        x_ref, rgate_ref, rgain_ref, out_ref, xg_ref, ag_ref, ye_ref = rest
    else:
        out_ref, xg_ref, ag_ref, ye_ref = rest
    f32, bf16 = jnp.float32, jnp.bfloat16
    e = pl.program_id(1)
    n_exp = pl.num_programs(1)
    n_sets, cap = xg_ref.shape[1], xg_ref.shape[2]

    def gather_row(p, s, ee, j):
        i = idx_ref[0, s, ee, j]
        xg_ref[p, s, pl.ds(j, 1), :] = rows_ref[s, pl.ds(i, 1), :]
        ag_ref[p, s, pl.ds(j, 1), :] = affr_ref[s, pl.ds(i, 1), :]

    def scatter_group(p, s, ee, j0):
        ids = [idx_ref[0, s, ee, j0 + r] for r in range(ROW_GROUP)]
        cur = [out_ref[s, pl.ds(i, 1), :] for i in ids]
        for r, i in enumerate(ids):
            out_ref[s, pl.ds(i, 1), :] = cur[r] + ye_ref[p, s, pl.ds(j0 + r, 1), :]

    @pl.when(e == 0)
    def _():
        out_ref[...] = jnp.zeros_like(out_ref)
        ye_ref[1] = jnp.zeros_like(ye_ref[1])

        def body(gi, carry):
            for s in range(n_sets):
                for r in range(ROW_GROUP):
                    gather_row(0, s, 0, gi * ROW_GROUP + r)
            return carry
        lax.fori_loop(0, cap // ROW_GROUP, body, 0)

    def step(p):
        e_prv = jnp.maximum(e - 1, 0)
        e_nxt = jnp.minimum(e + 1, n_exp - 1)
        for s in range(n_sets):
            for j0 in range(0, cap, ROW_GROUP):
                scatter_group(1 - p, s, e_prv, j0)
            for j in range(cap):
                gather_row(1 - p, s, e_nxt, j)
        packed = jnp.concatenate([xg_ref[p, s] for s in range(n_sets)], axis=0)
        xe = jnp.concatenate(
            [pltpu.unpack_elementwise(packed, index=i, packed_dtype=bf16, unpacked_dtype=f32) for i in (0, 1)],
            axis=-1).astype(bf16)
        aff = jnp.concatenate([ag_ref[p, s] for s in range(n_sets)], axis=0)
        gate = jnp.sum(jnp.where(lax.broadcasted_iota(jnp.int32, aff.shape, 1) == e, aff, 0.0),
                       axis=-1, keepdims=True)
        a = _dot_f32(xe, wg_ref[0])
        hid = (a * _sigmoid(a) * _dot_f32(xe, wu_ref[0])).astype(bf16)
        ye = _dot_f32(hid, wd_ref[0]) * gate
        for s in range(n_sets):
            ye_ref[p, s] = ye[s * cap:(s + 1) * cap]

    @pl.when(e % 2 == 0)
    def _():
        step(0)

    @pl.when(e % 2 == 1)
    def _():
        step(1)

    @pl.when(e == n_exp - 1)
    def _():
        def body(gi, carry):
            for s in range(n_sets):
                scatter_group(1, s, e, gi * ROW_GROUP)
            return carry
        lax.fori_loop(0, cap // ROW_GROUP, body, 0)

        if residual:
            def finish(ci, carry):
                rows = pl.ds(pl.multiple_of(ci * RESIDUAL_ROWS, RESIDUAL_ROWS), RESIDUAL_ROWS)
                for s in range(n_sets):
                    y = out_ref[s, rows, :]
                    yn = y * lax.rsqrt(jnp.mean(y * y, axis=-1, keepdims=True) + RMS_EPS) * rgain_ref[...]
                    out_ref[s, rows, :] = x_ref[s, rows, :] + rgate_ref[s] * yn
                return carry
            lax.fori_loop(0, out_ref.shape[1] // RESIDUAL_ROWS, finish, 0)


MOE_SETS = 1


def moe_experts(rows, aff_rows, idx, w_gate, w_up, w_down, layer, residual=None):
    G, T, RW = rows.shape
    AW = aff_rows.shape[2]
    _, E, cap = idx.shape
    D = w_gate.shape[2]
    ns = MOE_SETS
    assert E % 2 == 0 and cap % ROW_GROUP == 0 and G % ns == 0 and T % RESIDUAL_ROWS == 0
    wspec = pl.BlockSpec((None, 1, D, D), lambda g, e: (layer, e, 0, 0))
    in_specs = [pl.BlockSpec((1, ns, E, cap), lambda g, e: (g, 0, 0, 0), memory_space=pltpu.SMEM),
                pl.BlockSpec((ns, T, RW), lambda g, e: (g, 0, 0)),
                pl.BlockSpec((ns, T, AW), lambda g, e: (g, 0, 0)),
                wspec, wspec, wspec]
    args = [idx.reshape(G // ns, ns, E, cap), rows, aff_rows, w_gate, w_up, w_down]
    if residual is not None:
        in_specs += [pl.BlockSpec((ns, T, D), lambda g, e: (g, 0, 0), pipeline_mode=pl.Buffered(1)),
                     pl.BlockSpec((ns, 1, D), lambda g, e: (g, 0, 0)), _const_spec((1, D))]
        args += list(residual)
    return pl.pallas_call(
        functools.partial(_moe_kernel, residual=residual is not None),
        grid=(G // ns, E),
        in_specs=in_specs,
        out_specs=pl.BlockSpec((ns, T, D), lambda g, e: (g, 0, 0)),
        out_shape=jax.ShapeDtypeStruct((G, T, D), jnp.float32),
        scratch_shapes=[pltpu.VMEM((2, ns, cap, RW), rows.dtype), pltpu.VMEM((2, ns, cap, AW), jnp.float32),
                        pltpu.VMEM((2, ns, cap, D), jnp.float32)],
        compiler_params=pltpu.CompilerParams(
            dimension_semantics=("arbitrary", "arbitrary"), vmem_limit_bytes=VMEM_LIMIT),
        name="moe_experts",
    )(*args)


ADA_COLS = 1536


def _ada_kernel(c_ref, w_ref, b_ref, o_ref):
    c = c_ref[...]
    h = (c * _sigmoid(c)).astype(jnp.bfloat16)
    o_ref[...] = _dot_f32(h, w_ref[...].astype(jnp.bfloat16)) + b_ref[...]


def ada_modulation(cond, ada_w, ada_b, layer):
    R, D = cond.shape
    N = ada_w.shape[2]
    return pl.pallas_call(
        _ada_kernel,
        grid=(N // ADA_COLS,),
        in_specs=[pl.BlockSpec((R, D), lambda j: (0, 0)),
                  pl.BlockSpec((None, D, ADA_COLS), lambda j: (layer, 0, j)),
                  pl.BlockSpec((None, 1, ADA_COLS), lambda j: (layer, 0, j))],
        out_specs=pl.BlockSpec((R, ADA_COLS), lambda j: (0, j)),
        out_shape=jax.ShapeDtypeStruct((R, N), jnp.float32),
        compiler_params=pltpu.CompilerParams(dimension_semantics=("arbitrary",), vmem_limit_bytes=VMEM_LIMIT),
        name="ada_modulation",
    )(cond, ada_w, ada_b[:, None, :])


def kernel(x, c, ctx, c_ctx, norm_gain, ada_w, ada_b, w_in, mlstm_gate_bias, pool_w, pool_scale,
           na_rpb, w_branch, w_out, router_w, w_gate, w_up, w_down):
    B, S, D = x.shape
    Tc = ctx.shape[1]
    f32, bf16 = jnp.float32, jnp.bfloat16
    rope_tabs = rope_lane_tables(S)
    names = [n for n, _ in IN_GROUPS] + ["gt"]
    cond = jnp.pad(jnp.concatenate([c, c_ctx[None]], axis=0), ((0, (-(B + 1)) % 8), (0, 0)))
    moe_w = (w_gate.astype(bf16), w_up.astype(bf16), w_down.astype(bf16))
    assert B % (S // Tc) == 0
    pending = pending_c = None
    for l in range(DEPTH):
        need_ctx = l < DEPTH - 1
        g = norm_gain[l]
        mods = ada_modulation(cond, ada_w, ada_b, l)
        mod_l = jnp.split(mods[:B, None, :], N_MOD, axis=-1)
        mod_cb = [jnp.broadcast_to(m[None, None], (B, 1, D)) for m in jnp.split(mods[B], N_MOD, axis=-1)]
        w_perm = permute_w_in(w_in[l])
        gb_row = jnp.pad(mlstm_gate_bias[l].reshape(1, -1).astype(f32), ((0, 0), (0, GATE_W - 4 * M_HEADS)))
        proj = in_proj(x, g[0][None], mod_l[0], mod_l[1], w_perm, gb_row, rope_tabs, pending, tm=512)
        if pending is not None:
            x = proj[-1]
        pl_ = dict(zip(names, proj))
        keep_c = None if need_ctx else ("mqk", "mvt", "kn", "vn", "g")
        proj_c = in_proj(ctx, g[0][None], mod_cb[0], mod_cb[1], w_perm, gb_row, None, pending_c, keep_c, tm=Tc)
        if pending_c is not None:
            ctx = proj_c[-1]
        pc_ = dict(zip(names if keep_c is None else list(keep_c) + ["gt"], proj_c))

        m_l, m_c = mlstm(pc_, pl_, need_ctx)
        n_l, n_c = neighbourhood_attention(pl_["qn"], pl_["kn"], pl_["vn"], pc_["kn"], pc_["vn"],
                                           na_bias_table(na_rpb[l]), pc_["qn"] if need_ctx else None)
        gains = jnp.pad(g[1:3], ((0, 6), (0, 0)))
        merge_w = (pool_w[l].astype(bf16), pool_scale[l][None], w_branch[l].astype(bf16),
                   (0.5 * w_out[l]).astype(bf16),
                   router_w[l].T.astype(bf16),
                   jnp.pad(router_w[l], ((0, 0), (0, LANES - N_EXPERTS))).astype(bf16))

        def mods8(mods):
            return jnp.pad(jnp.concatenate([mods[2], mods[3], mods[4]], axis=1), ((0, 0), (0, 5), (0, 0)))

        x1, rows, aff_r, aff_t = merge_out(m_l, n_l, pl_["pu"], pl_["bg"], x, mods8(mod_l), gains, *merge_w, tm=1024)
        idx_l = route(aff_t, EC_FACTOR * S // N_EXPERTS)
        if need_ctx:
            x, pending = x1, (moe_experts(rows, aff_r, idx_l, *moe_w, l), mod_l[5], g[3][None])
        else:
            x = moe_experts(rows, aff_r, idx_l, *moe_w, l, residual=(x1, mod_l[5], g[3][None]))
        if need_ctx:
            c1, rows, aff_r, aff_t = merge_out(m_c, n_c, pc_["pu"], pc_["bg"], ctx, mods8(mod_cb), gains, *merge_w, tm=Tc)
            per = S // Tc
            cap_c = EC_FACTOR * Tc // N_EXPERTS
            idx = route(aff_t, cap_c) + (jnp.arange(B, dtype=jnp.int32) % per * Tc)[:, None, None]
            idx = idx.reshape(B // per, per, N_EXPERTS, cap_c).transpose(0, 2, 1, 3).reshape(B // per, N_EXPERTS, per * cap_c)
            y = moe_experts(rows.reshape(B // per, S, D // 2), aff_r.reshape(B // per, S, LANES), idx,
                            *moe_w, l).reshape(B, Tc, D)
            ctx, pending_c = c1, (y, mod_cb[5], g[3][None])
    return x
```

```python
import functools
import math

import jax
import jax.numpy as jnp
import numpy as np
from jax import lax
from jax.experimental import pallas as pl
from jax.experimental.pallas import tpu as pltpu

D_MODEL = 1024
DEPTH = 2
GRID_W = 64
BRANCH_W = D_MODEL // 2
N_BRANCH = 3
M_HEADS = 4
M_DV = BRANCH_W // M_HEADS
M_DQK = M_DV // 2
M_CHUNK = 128
M_INIT = -1e30
M_ONES = 16
POOL_GROUPS = 4
POOL_GC = BRANCH_W // POOL_GROUPS
POOL_WINDOWS = (2, 4, 8, 16)
NA_HEADS = 8
NA_DH = BRANCH_W // NA_HEADS
NA_KH = 8
NA_KW = 16
N_EXPERTS = 16
EC_FACTOR = 2
ROPE_BASE = 10000.0
RMS_EPS = 1e-6
N_MOD = 6
SPLIT_SIZES = (M_HEADS * M_DQK, M_HEADS * M_DQK, M_HEADS * M_DV, M_HEADS * M_DV, 4 * M_HEADS,
               POOL_GROUPS * POOL_GC, NA_HEADS * NA_DH, NA_HEADS * NA_DH, NA_HEADS * NA_DH,
               N_BRANCH * D_MODEL)
PROJ_W = sum(SPLIT_SIZES)

LANES = 128
VMEM_LIMIT = 56 * 1024 * 1024
MASK_NEG = -1e30
GATE_W = LANES

IN_GROUPS = (("mqk", 2 * M_HEADS * M_DQK), ("mvt", BRANCH_W), ("mo", BRANCH_W), ("pu", BRANCH_W),
             ("qn", BRANCH_W), ("kn", BRANCH_W), ("vn", BRANCH_W), ("bg", N_BRANCH * D_MODEL),
             ("g", GATE_W))
IN_W = sum(w for _, w in IN_GROUPS)
MM_COLS = 512
PACK_WORD = jnp.uint32


def _sigmoid(x):
    return 0.5 * jnp.tanh(0.5 * x) + 0.5


def _const_spec(shape):
    nd = len(shape)
    return pl.BlockSpec(shape, lambda *_: (0,) * nd, pipeline_mode=pl.Buffered(1))


def _rope_tile(x, cos, sin_signed):
    half = M_DQK // 4
    lane = lax.broadcasted_iota(jnp.int32, x.shape, 1)
    partner = jnp.where((lane % (2 * half)) < half, pltpu.roll(x, LANES - half, 1), pltpu.roll(x, half, 1))
    return x * cos + partner * sin_signed


def _in_proj_kernel(x_ref, g_ref, shift_ref, scale_ref, w_ref, gb_ref, *rest, rope, pending, keep):
    if rope:
        cos_ref, sin_ref = rest[:2]
        rest = rest[2:]
    if pending:
        r_ref, rgate_ref, rgain_ref = rest[:3]
        rest, xo_ref = rest[3:-1], rest[-1]
        r = r_ref[0]
        x = x_ref[0] + rgate_ref[0] * (r * lax.rsqrt(jnp.mean(r * r, axis=-1, keepdims=True) + RMS_EPS)
                                       * rgain_ref[...])
        xo_ref[0] = x
    else:
        x = x_ref[0]
    o_refs, gt_ref = rest[:-1], rest[-1]
    y = x * lax.rsqrt(jnp.mean(x * x, axis=-1, keepdims=True) + RMS_EPS)
    h = ((y * g_ref[...]) * (1.0 + scale_ref[0]) + shift_ref[0]).astype(jnp.bfloat16)
    off = 0
    o_iter = iter(o_refs)
    for name, width in IN_GROUPS:
        if name not in keep:
            off += width
            continue
        o_ref = next(o_iter)
        for c0 in range(0, width, MM_COLS):
            cw = min(MM_COLS, width - c0)
            acc = jnp.dot(h, w_ref[:, off + c0:off + c0 + cw], preferred_element_type=jnp.float32)
            if name == "g":
                acc = acc + gb_ref[...]
                for c in range(acc.shape[0] // M_CHUNK):
                    gt_ref[0, c] = acc[c * M_CHUNK:(c + 1) * M_CHUNK, :].T[:4 * M_HEADS, :]
            if name == "mqk" and rope:
                cos, sin = cos_ref[...], sin_ref[...]
                acc = jnp.concatenate([_rope_tile(acc[:, t:t + LANES], cos, sin) for t in range(0, cw, LANES)], axis=-1)
            if name == "mvt":
                for c in range(acc.shape[0] // M_CHUNK):
                    o_ref[0, c] = acc[c * M_CHUNK:(c + 1) * M_CHUNK, :].T.astype(o_ref.dtype)
                continue
            o_ref[0, :, c0:c0 + cw] = acc.astype(o_ref.dtype)
        off += width


def in_proj(x, g, shift, scale, w_perm, gate_bias_row, rope_tabs=None, pending=None, keep=None, *, tm):
    B, T, D = x.shape
    keep = tuple(n for n, _ in IN_GROUPS) if keep is None else tuple(keep)
    assert "g" in keep
    out_shape = [jax.ShapeDtypeStruct((B, T, w), jnp.float32 if n == "g" else jnp.bfloat16) for n, w in IN_GROUPS]
    out_specs = [pl.BlockSpec((1, tm, w), lambda b, i: (b, i, 0)) for _, w in IN_GROUPS]
    k_mvt = [n for n, _ in IN_GROUPS].index("mvt")
    assert BRANCH_W == MM_COLS and tm % M_CHUNK == 0
    out_shape[k_mvt] = jax.ShapeDtypeStruct((B, T // M_CHUNK, BRANCH_W, M_CHUNK), jnp.bfloat16)
    out_specs[k_mvt] = pl.BlockSpec((1, tm // M_CHUNK, BRANCH_W, M_CHUNK), lambda b, i: (b, i, 0, 0))
    kept = [k for k, (n, _) in enumerate(IN_GROUPS) if n in keep]
    out_shape, out_specs = [out_shape[k] for k in kept], [out_specs[k] for k in kept]
    out_shape.append(jax.ShapeDtypeStruct((B, T // M_CHUNK, 4 * M_HEADS, M_CHUNK), jnp.float32))
    out_specs.append(pl.BlockSpec((1, tm // M_CHUNK, 4 * M_HEADS, M_CHUNK), lambda b, i: (b, i, 0, 0)))
    in_specs = [
        pl.BlockSpec((1, tm, D), lambda b, i: (b, i, 0)),
        _const_spec((1, D)),
        pl.BlockSpec((1, 1, D), lambda b, i: (b, 0, 0)),
        pl.BlockSpec((1, 1, D), lambda b, i: (b, 0, 0)),
        _const_spec((D, IN_W)),
        _const_spec((1, GATE_W)),
    ]
    args = [x, g, shift, scale, w_perm, gate_bias_row]
    if rope_tabs is not None:
        in_specs += [pl.BlockSpec((tm, LANES), lambda b, i: (i, 0))] * 2
        args += list(rope_tabs)
    if pending is not None:
        in_specs += [pl.BlockSpec((1, tm, D), lambda b, i: (b, i, 0)), pl.BlockSpec((1, 1, D), lambda b, i: (b, 0, 0)),
                     _const_spec((1, D))]
        args += list(pending)
        out_shape.append(jax.ShapeDtypeStruct((B, T, D), jnp.float32))
        out_specs.append(pl.BlockSpec((1, tm, D), lambda b, i: (b, i, 0)))
    return pl.pallas_call(
        functools.partial(_in_proj_kernel, rope=rope_tabs is not None, pending=pending is not None, keep=keep),
        grid=(B, T // tm),
        in_specs=in_specs,
        out_specs=out_specs,
        out_shape=out_shape,
        compiler_params=pltpu.CompilerParams(
            dimension_semantics=("arbitrary", "arbitrary"), vmem_limit_bytes=VMEM_LIMIT),
        name="in_proj",
    )(*args)


def rope_lane_tables(n_tokens):
    t = jnp.arange(n_tokens)
    row = (t // GRID_W).astype(jnp.float32)
    col = (t % GRID_W).astype(jnp.float32)
    half = M_DQK // 2
    inv = ROPE_BASE ** (-jnp.arange(0, half, 2, dtype=jnp.float32) / half)
    ar = row[:, None] * inv[None, :]
    ac = col[:, None] * inv[None, :]
    cos = jnp.concatenate([jnp.cos(ar), jnp.cos(ar), jnp.cos(ac), jnp.cos(ac)], axis=-1)
    sin = jnp.concatenate([-jnp.sin(ar), jnp.sin(ar), -jnp.sin(ac), jnp.sin(ac)], axis=-1)
    return jnp.tile(cos, (1, LANES // M_DQK)), jnp.tile(sin, (1, LANES // M_DQK))


def permute_w_in(w_in_l):
    qm, km, vm, om, gm, pu, qn, kn, vn, bg = jnp.split(w_in_l, [int(s) for s in np.cumsum(SPLIT_SIZES)[:-1]], axis=-1)
    gpad = jnp.pad(gm, ((0, 0), (0, GATE_W - gm.shape[1])))
    w = jnp.concatenate([qm * (M_DQK ** -0.5), km, vm, om, pu, qn * (NA_DH ** -0.5), kn, vn, bg * 0.5, gpad], axis=-1)
    return w.astype(jnp.bfloat16)


def _pair_scores(qp, k_parts, biases):
    n = qp.shape[0]
    lane_lo = lax.broadcasted_iota(jnp.int32, (n, LANES), 1) < NA_DH
    zero = jnp.zeros_like(qp)
    qq = jnp.concatenate([jnp.where(lane_lo, qp, zero), jnp.where(lane_lo, zero, qp)], axis=0)
    scores = []
    for kp, bias in zip(k_parts, biases):
        s = lax.dot_general(qq, kp, (((1,), (1,)), ((), ())), preferred_element_type=jnp.float32)
        scores.append(s if bias is None else s + bias)
    return scores


def _pair_softmax_pv(scores, v_parts):
    n = scores[0].shape[0] // 2
    lane_lo = lax.broadcasted_iota(jnp.int32, (n, LANES), 1) < NA_DH
    m = scores[0].max(axis=-1, keepdims=True)
    for s in scores[1:]:
        m = jnp.maximum(m, s.max(axis=-1, keepdims=True))
    l = None
    o = None
    for s, vp in zip(scores, v_parts):
        p = jnp.exp((s - m).astype(jnp.bfloat16))
        ls = p.astype(jnp.float32).sum(axis=-1, keepdims=True)
        os_ = jnp.dot(p, vp, preferred_element_type=jnp.float32)
        l = ls if l is None else l + ls
        o = os_ if o is None else o + os_
    o = o * (1.0 / l)
    return jnp.where(lane_lo, o[:n], o[n:])


def _na_kernel(q_ref, k_ref, v_ref, kc_ref, vc_ref, bias_ref, *rest, need_ctx):
    if need_ctx:
        qc_ref, o_ref, oc_ref, s_ref = rest
    else:
        o_ref, s_ref = rest
    S = q_ref.shape[1]
    Tc = kc_ref.shape[1]
    rows = S // GRID_W
    n_pairs = BRANCH_W // LANES
    n_loc = NA_KH * GRID_W

    def window(r):
        rs = jnp.clip(r - NA_KH // 2, 0, rows - NA_KH)
        return r - rs, pl.multiple_of(r * GRID_W, GRID_W), pl.multiple_of(rs * GRID_W, GRID_W)

    def scores_stage(r, slot, j):
        var, q0, k0 = window(r)
        ls = slice(j * LANES, (j + 1) * LANES)
        s_loc, s_ctx = _pair_scores(q_ref[0, pl.ds(q0, GRID_W), ls],
                                    (k_ref[0, pl.ds(k0, n_loc), ls], kc_ref[0, :, ls]),
                                    (bias_ref[var, j], None))
        s_ref[slot, j, :, :n_loc] = s_loc
        s_ref[slot, j, :, n_loc:] = s_ctx

    def output_stage(r, slot, j):
        _, q0, k0 = window(r)
        ls = slice(j * LANES, (j + 1) * LANES)
        o = _pair_softmax_pv((s_ref[slot, j, :, :n_loc], s_ref[slot, j, :, n_loc:]),
                             (v_ref[0, pl.ds(k0, n_loc), ls], vc_ref[0, :, ls]))
        o_ref[0, pl.ds(q0, GRID_W), ls] = o.astype(o_ref.dtype)

    for j in range(n_pairs):
        scores_stage(0, 0, j)

    def two_rows(i, carry):
        r0 = 2 * i
        for j in range(n_pairs):
            scores_stage(r0 + 1, 1, j)
            output_stage(r0, 0, j)
        for j in range(n_pairs):
            scores_stage(jnp.minimum(r0 + 2, rows - 1), 0, j)
            output_stage(r0 + 1, 1, j)
        return carry

    lax.fori_loop(0, rows // 2, two_rows, 0)

    if need_ctx:
        outs = []
        for j in range(n_pairs):
            ls = slice(j * LANES, (j + 1) * LANES)
            scores = _pair_scores(qc_ref[0, :, ls], (kc_ref[0, :, ls],), (None,))
            outs.append(_pair_softmax_pv(scores, (vc_ref[0, :, ls],)))
        oc_ref[0] = jnp.concatenate(outs, axis=-1).astype(oc_ref.dtype)


def na_bias_table(rpb):
    H = rpb.shape[0]
    var = jnp.arange(NA_KH)
    kr = jnp.arange(NA_KH)
    dr = kr[None, :] - var[:, None] + NA_KH - 1
    cols = jnp.arange(GRID_W)
    dc = jnp.clip(cols[None, :] - cols[:, None] + NA_KW - 1, 0, 2 * NA_KW - 2)
    cs = jnp.clip(cols - NA_KW // 2, 0, GRID_W - NA_KW)
    colmask = (cols[None, :] >= cs[:, None]) & (cols[None, :] < cs[:, None] + NA_KW)
    pick_r = jax.nn.one_hot(dr, 2 * NA_KH - 1, dtype=jnp.float32)
    pick_c = jax.nn.one_hot(dc, 2 * NA_KW - 1, dtype=jnp.float32)
    tab = jnp.einsum('vka,hab,qcb->vhqkc', pick_r, rpb.astype(jnp.float32), pick_c, precision=lax.Precision.HIGHEST)
    tab = jnp.where(colmask[None, None, :, None, :], tab, MASK_NEG)
    return tab.reshape(NA_KH, H // 2, 2 * GRID_W, NA_KH * GRID_W)


def neighbourhood_attention(qn, kn, vn, kc, vc, bias_tab, qc=None):
    B, S, W = qn.shape
    Tc = kc.shape[1]
    need_ctx = qc is not None
    lat = pl.BlockSpec((1, S, W), lambda b: (b, 0, 0))
    cx = pl.BlockSpec((1, Tc, W), lambda b: (b, 0, 0))
    in_specs = [lat, lat, lat, cx, cx, _const_spec(bias_tab.shape)]
    args = [qn, kn, vn, kc, vc, bias_tab]
    out_shape = [jax.ShapeDtypeStruct((B, S, W), jnp.bfloat16)]
    out_specs = [lat]
    if need_ctx:
        in_specs.append(cx)
        args.append(qc)
        out_shape.append(jax.ShapeDtypeStruct((B, Tc, W), jnp.bfloat16))
        out_specs.append(cx)
    res = pl.pallas_call(
        functools.partial(_na_kernel, need_ctx=need_ctx),
        grid=(B,),
        in_specs=in_specs,
        out_specs=out_specs,
        out_shape=out_shape,
        scratch_shapes=[pltpu.VMEM((2, W // LANES, 2 * GRID_W, NA_KH * GRID_W + Tc), jnp.float32)],
        compiler_params=pltpu.CompilerParams(dimension_semantics=("arbitrary",), vmem_limit_bytes=VMEM_LIMIT),
        name="neighbourhood_attention",
    )(*args)
    return (res[0], res[1]) if need_ctx else (res[0], None)


def _split_bf16(x):
    hi = x.astype(jnp.bfloat16)
    r1 = x - hi.astype(jnp.float32)
    mid = r1.astype(jnp.bfloat16)
    lo = (r1 - mid.astype(jnp.float32)).astype(jnp.bfloat16)
    return hi, mid, lo


def _dot_f32(a, b):
    return jnp.dot(a, b, preferred_element_type=jnp.float32)


def _log_sigmoid(x):
    return jnp.minimum(x, 0.0) - jnp.log(1.0 + jnp.exp(-jnp.abs(x)))


def _pair_queries(qp):
    lane_lo = lax.broadcasted_iota(jnp.int32, qp.shape, 1) < M_DQK
    zq = jnp.zeros_like(qp)
    return jnp.concatenate([jnp.where(lane_lo, qp, zq), jnp.where(lane_lo, zq, qp)], axis=0)


def _mlstm_prep(qk_ref, g_ref, gt_ref, sa_ref, ra_ref, slot, ci, fwd):
    L = M_CHUNK
    f32, bf16 = jnp.float32, jnp.bfloat16
    d = 0 if fwd else 1
    rows = pl.ds(pl.multiple_of(ci * L, L), L)
    sq_r = lax.broadcasted_iota(jnp.int32, (L, L), 0)
    sq_c = lax.broadcasted_iota(jnp.int32, (L, L), 1)
    tri_l = jnp.where(sq_r >= sq_c, 1.0, 0.0).astype(bf16)
    tri_u = jnp.where(sq_r <= sq_c, 1.0, 0.0).astype(bf16)
    gates = g_ref[0, rows, :]
    gates_t = gt_ref[0, ci]
    lf, lf_t = _log_sigmoid(gates), _log_sigmoid(gates_t)
    cum = sum(_dot_f32(tri_l if fwd else tri_u, p) for p in _split_bf16(lf))
    cum_t = sum(_dot_f32(p, tri_u if fwd else tri_l) for p in _split_bf16(lf_t))
    t_i = 0 if fwd else 2
    st_s = lax.broadcasted_iota(jnp.int32, (L, 2 * L), 0)
    st_t = lax.broadcasted_iota(jnp.int32, (L, 2 * L), 1) & (L - 1)
    valid = (st_s <= st_t) if fwd else (st_s >= st_t)

    def row2(tile, c0, c1):
        return jnp.concatenate([tile[c0:c0 + 1, :], tile[c1:c1 + 1, :]], axis=-1)

    for j in range(M_HEADS // 2):
        ci0, ci1 = t_i * M_HEADS + 2 * j, t_i * M_HEADS + 2 * j + 1
        cf0, cf1 = ci0 + M_HEADS, ci1 + M_HEADS
        b_row, i_row = row2(cum_t, cf0, cf1), row2(gates_t, ci0, ci1)
        colb = jnp.concatenate([jnp.broadcast_to(gates[:, ci0:ci0 + 1] - cum[:, cf0:cf0 + 1], (L, L)),
                                jnp.broadcast_to(gates[:, ci1:ci1 + 1] - cum[:, cf1:cf1 + 1], (L, L))],
                               axis=-1)
        qp = qk_ref[0, rows, j * LANES:(j + 1) * LANES]
        kp = qk_ref[0, rows, M_HEADS * M_DQK + j * LANES:M_HEADS * M_DQK + (j + 1) * LANES]
        sa_ref[slot, d, j, 0] = jnp.where(valid, b_row + colb, MASK_NEG)
        sa_ref[slot, d, j, 1] = lax.dot_general(kp, _pair_queries(qp), (((1,), (1,)), ((), ())),
                                                preferred_element_type=f32)
        ra_ref[slot, d, j, 0:1, :] = b_row
        ra_ref[slot, d, j, 1:2, :] = i_row


def _mlstm_step(qk_ref, vt_ref, sa_ref, ra_ref, slot, h_ref, ct_ref, m_ref, ci, fwd):
    L = M_CHUNK
    f32, bf16 = jnp.float32, jnp.bfloat16
    d = 0 if fwd else 1
    rows = pl.ds(pl.multiple_of(ci * L, L), L)
    last = L - 1 if fwd else 0
    lane_lo = lax.broadcasted_iota(jnp.int32, (L, LANES), 1) < M_DQK
    lane_lo_row = lax.broadcasted_iota(jnp.int32, (1, LANES), 1) < M_DQK
    ones = jnp.ones((M_ONES, L), bf16)

    def halves(row, f):
        return jnp.concatenate([jnp.broadcast_to(f(row[:, :L]), (1, L)), jnp.broadcast_to(f(row[:, L:]), (1, L))],
                               axis=-1)

    for j in range(M_HEADS // 2):
        h0, h1 = 2 * j, 2 * j + 1
        sidx = 2 * j + d
        b_row, i_row = ra_ref[slot, d, j, 0:1, :], ra_ref[slot, d, j, 1:2, :]
        m_row = m_ref[sidx, 0:1, :]
        kp = qk_ref[0, rows, M_HEADS * M_DQK + j * LANES:M_HEADS * M_DQK + (j + 1) * LANES]
        vx0 = jnp.concatenate([vt_ref[0, ci, h0 * M_DV:(h0 + 1) * M_DV, :], ones], axis=0)
        vx1 = jnp.concatenate([vt_ref[0, ci, h1 * M_DV:(h1 + 1) * M_DV, :], ones], axis=0)
        ct = ct_ref[sidx]
        if h_ref is not None:
            dm = sa_ref[slot, d, j, 0]
            qq = _pair_queries(qk_ref[0, rows, j * LANES:(j + 1) * LANES])
            g_row = b_row + m_row
            mt = jnp.maximum(g_row, dm.max(axis=0, keepdims=True))
            sc = (sa_ref[slot, d, j, 1] * jnp.exp(dm - mt)).astype(bf16)
            inter = jnp.exp(g_row - mt)
            t1 = lax.dot_general(ct.astype(bf16), qq, (((1,), (1,)), ((), ())), preferred_element_type=f32)
            t2 = jnp.concatenate([_dot_f32(vx0, sc[:, :L]), _dot_f32(vx1, sc[:, L:])], axis=-1)
            tot = inter * t1 + t2
            h_t = tot[:M_DV] * (1.0 / jnp.maximum(jnp.abs(tot[M_DV:M_DV + 1]), jnp.exp(-mt)))
            h_ref[rows, h0 * M_DV:(h0 + 1) * M_DV] = h_t[:, :L].T
            h_ref[rows, h1 * M_DV:(h1 + 1) * M_DV] = h_t[:, L:].T
        bl = halves(b_row, lambda r: r[:, last:last + 1])
        w = bl - b_row + i_row
        m_new = jnp.maximum(bl + m_row, halves(w, lambda r: r.max(axis=-1, keepdims=True)))
        decay = jnp.exp(bl + m_row - m_new)
        ws = jnp.exp(w - m_new)
        vxs = jnp.concatenate([vx0.astype(f32) * ws[:, :L], vx1.astype(f32) * ws[:, L:]], axis=-1).astype(bf16)
        zk = jnp.zeros_like(kp)
        kk = jnp.concatenate([jnp.where(lane_lo, kp, zk), jnp.where(lane_lo, zk, kp)], axis=0)
        d_cols = jnp.where(lane_lo_row, jnp.broadcast_to(decay[:, 0:1], (1, LANES)),
                           jnp.broadcast_to(decay[:, L:L + 1], (1, LANES)))
        ct_ref[sidx] = d_cols * ct + _dot_f32(vxs, kk)
        m_ref[sidx, 0:1, :] = m_new


def _mlstm_finish(hf_ref, hb_ref, o_ref, out_ref, n_chunks):
    L = M_CHUNK

    def body(ci, carry):
        rows = pl.ds(pl.multiple_of(ci * L, L), L)
        h = hf_ref[rows, :] + hb_ref[rows, :]
        parts = []
        for hd in range(M_HEADS):
            hh = h[:, hd * M_DV:(hd + 1) * M_DV]
            parts.append(hh * lax.rsqrt(jnp.mean(hh * hh, axis=-1, keepdims=True) + RMS_EPS))
        gate = _sigmoid(o_ref[0, rows, :].astype(jnp.float32))
        out_ref[0, rows, :] = (gate * jnp.concatenate(parts, axis=-1)).astype(out_ref.dtype)
        return carry

    lax.fori_loop(0, n_chunks, body, 0)


def _mlstm_kernel(qk_c, v_c, g_c, gt_c, qk_l, v_l, g_l, gt_l, o_l, *rest, need_ctx):
    if need_ctx:
        o_c, out_l, out_c, hf_l, hb_l, hf_c, hb_c, ct_ref, m_ref, sa_ref, ra_ref = rest
    else:
        out_l, hf_l, hb_l, ct_ref, m_ref, sa_ref, ra_ref = rest
        hf_c = hb_c = None
    nc, nl = qk_c.shape[1] // M_CHUNK, qk_l.shape[1] // M_CHUNK
    ct_ref[...] = jnp.zeros_like(ct_ref)
    m_ref[...] = jnp.full_like(m_ref, M_INIT)

    def phase(qk, vt, g, gt, hf, hb, n):
        def prep(s, slot):
            _mlstm_prep(qk, g, gt, sa_ref, ra_ref, slot, s, True)
            _mlstm_prep(qk, g, gt, sa_ref, ra_ref, slot, n - 1 - s, False)

        def step(s, slot):
            _mlstm_step(qk, vt, sa_ref, ra_ref, slot, hf, ct_ref, m_ref, s, True)
            _mlstm_step(qk, vt, sa_ref, ra_ref, slot, hb, ct_ref, m_ref, n - 1 - s, False)

        prep(0, 0)

        def body(i, carry):
            s0 = 2 * i
            prep(s0 + 1, 1)
            step(s0, 0)
            prep(jnp.minimum(s0 + 2, n - 1), 0)
            step(s0 + 1, 1)
            return carry
        lax.fori_loop(0, n // 2, body, 0)

    phase(qk_c, v_c, g_c, gt_c, hf_c, hb_c, nc)
    phase(qk_l, v_l, g_l, gt_l, hf_l, hb_l, nl)
    _mlstm_finish(hf_l, hb_l, o_l, out_l, nl)
    if need_ctx:
        _mlstm_finish(hf_c, hb_c, o_c, out_c, nc)


def mlstm(pc, pl_, need_ctx):
    B, S, W = pl_["mo"].shape
    Tc = pc["mqk"].shape[1]
    f32 = jnp.float32

    def row_spec(T):
        return pl.BlockSpec((1, T, W), lambda b: (b, 0, 0))

    def specs(T):
        return [row_spec(T), pl.BlockSpec((1, T // M_CHUNK, W, M_CHUNK), lambda b: (b, 0, 0, 0)),
                pl.BlockSpec((1, T, GATE_W), lambda b: (b, 0, 0)),
                pl.BlockSpec((1, T // M_CHUNK, 4 * M_HEADS, M_CHUNK), lambda b: (b, 0, 0, 0))]

    def args(p):
        return [p["mqk"], p["mvt"], p["g"], p["gt"]]

    gate_specs = [row_spec(S)] + ([row_spec(Tc)] if need_ctx else [])
    gate_args = [pl_["mo"]] + ([pc["mo"]] if need_ctx else [])

    out_shape = [jax.ShapeDtypeStruct((B, S, W), jnp.bfloat16)]
    out_specs = [pl.BlockSpec((1, S, W), lambda b: (b, 0, 0))]
    scratch = [pltpu.VMEM((S, W), f32), pltpu.VMEM((S, W), f32)]
    if need_ctx:
        out_shape.append(jax.ShapeDtypeStruct((B, Tc, W), jnp.bfloat16))
        out_specs.append(pl.BlockSpec((1, Tc, W), lambda b: (b, 0, 0)))
        scratch += [pltpu.VMEM((Tc, W), f32), pltpu.VMEM((Tc, W), f32)]
    assert (S // M_CHUNK) % 2 == 0 and (Tc // M_CHUNK) % 2 == 0
    n_pairs = M_HEADS // 2
    scratch += [pltpu.VMEM((M_HEADS, M_DV + M_ONES, LANES), f32), pltpu.VMEM((M_HEADS, 8, 2 * M_CHUNK), f32),
                pltpu.VMEM((2, 2, n_pairs, 2, M_CHUNK, 2 * M_CHUNK), f32),
                pltpu.VMEM((2, 2, n_pairs, 8, 2 * M_CHUNK), f32)]
    res = pl.pallas_call(
        functools.partial(_mlstm_kernel, need_ctx=need_ctx),
        grid=(B,),
        in_specs=specs(Tc) + specs(S) + gate_specs,
        out_specs=out_specs,
        out_shape=out_shape,
        scratch_shapes=scratch,
        compiler_params=pltpu.CompilerParams(dimension_semantics=("arbitrary",), vmem_limit_bytes=VMEM_LIMIT),
        name="mlstm",
    )(*args(pc), *args(pl_), *gate_args)
    return (res[0], res[1]) if need_ctx else (res[0], None)


POOL_HALO = 16


def _pool_mix(ext, tok0, pw_ref, ps_ref, seq_len):
    f32 = jnp.float32
    n_ext = ext.shape[0]
    ts = n_ext - 2 * POOL_HALO
    tok = tok0 + lax.broadcasted_iota(jnp.int32, (ts, POOL_GC), 0)

    def shifted(a, d):
        return pltpu.roll(a, (-d) % n_ext, 0)

    outs = []
    for gi, w in enumerate(POOL_WINDOWS):
        s = ext[:, gi * POOL_GC:(gi + 1) * POOL_GC]
        cur = s[POOL_HALO:POOL_HALO + ts]
        s = shifted(s, -1) + s
        span = 2
        while span < w:
            s = shifted(s, -(span // 2)) + shifted(s, span // 2)
            span *= 2
        lo = jnp.clip(tok - w // 2, 0, seq_len - 1)
        hi = jnp.clip(tok - w // 2 + w - 1, 0, seq_len - 1)
        mean = s[POOL_HALO:POOL_HALO + ts] / (hi - lo + 1).astype(f32)
        outs.append(_dot_f32((mean - cur).astype(jnp.bfloat16), pw_ref[gi]))
    return jnp.concatenate(outs, axis=-1) * ps_ref[...]


def _pack_bf16_pairs(h):
    n = h.shape[1] // 2
    return pltpu.pack_elementwise([h[:, :n], h[:, n:]], packed_dtype=jnp.bfloat16)


MERGE_SUB = 256


def _merge_kernel(m_ref, n_ref, pu_ref, prev_ref, next_ref, bg_ref, x_ref, mods_ref, gains_ref,
                  pw_ref, ps_ref, wb_ref, wo_ref, wrt_ref, wrp_ref, x1_ref, h2p_ref, affr_ref, aff_ref,
                  ext_ref, y_ref, *, seq_len):
    f32, bf16 = jnp.float32, jnp.bfloat16
    i = pl.program_id(1)
    tm, D = x_ref.shape[1], x_ref.shape[2]
    E = wrt_ref.shape[0]
    ts = min(MERGE_SUB, tm)
    ext_ref[0:POOL_HALO] = jnp.where(i > 0, prev_ref[0].astype(f32), 0.0)
    ext_ref[POOL_HALO:POOL_HALO + tm] = pu_ref[0].astype(f32)
    ext_ref[POOL_HALO + tm:] = jnp.where(i < pl.num_programs(1) - 1, next_ref[0].astype(f32), 0.0)

    def mix_stage(k):
        rows = slice(k * ts, (k + 1) * ts)
        p = _pool_mix(ext_ref[k * ts:(k + 1) * ts + 2 * POOL_HALO], i * tm + k * ts, pw_ref, ps_ref, seq_len)
        acc = None
        for bi, br in enumerate((m_ref[0, rows], p.astype(bf16), n_ref[0, rows])):
            gate = jnp.tanh(bg_ref[0, rows, bi * D:(bi + 1) * D].astype(f32)) + 1.0
            term = gate * _dot_f32(br, wb_ref[bi])
            acc = term if acc is None else acc + term
        y_ref[k % 2] = _dot_f32(acc.astype(bf16), wo_ref[...])

    def out_stage(k):
        rows = slice(k * ts, (k + 1) * ts)
        y = y_ref[k % 2]
        yn = y * lax.rsqrt(jnp.mean(y * y, axis=-1, keepdims=True) + RMS_EPS) * gains_ref[0:1, :]
        x1 = x_ref[0, rows] + mods_ref[0, 0:1, :] * yn
        x1_ref[0, rows] = x1
        xn = x1 * lax.rsqrt(jnp.mean(x1 * x1, axis=-1, keepdims=True) + RMS_EPS) * gains_ref[1:2, :]
        h2 = xn * (1.0 + mods_ref[0, 2:3, :]) + mods_ref[0, 1:2, :]
        h2b = h2.astype(bf16)
        lg_t = lax.dot_general(wrt_ref[...], h2b, (((1,), (1,)), ((), ())), preferred_element_type=f32)
        e_t = jnp.exp(lg_t - lg_t.max(axis=0, keepdims=True))
        aff_ref[0, :, rows] = e_t / e_t.sum(axis=0, keepdims=True)
        lg = _dot_f32(h2b, wrp_ref[...])
        lg = jnp.where(lax.broadcasted_iota(jnp.int32, lg.shape, 1) < E, lg, MASK_NEG)
        e_r = jnp.exp(lg - lg.max(axis=-1, keepdims=True))
        affr_ref[0, rows] = e_r / e_r.sum(axis=-1, keepdims=True)
        h2p_ref[0, rows] = _pack_bf16_pairs(h2)

    n_sub = tm // ts
    mix_stage(0)
    for k in range(n_sub):
        if k + 1 < n_sub:
            mix_stage(k + 1)
        out_stage(k)


def merge_out(m, n, pu, bg, x, mods, gains, pool_w, pool_scale, w_branch, w_out, w_router_t, w_router_pad, *, tm):
    B, T, D = x.shape
    W = m.shape[2]
    E = w_router_t.shape[0]
    hb = tm // POOL_HALO
    n_halo = T // POOL_HALO
    tile = lambda w: pl.BlockSpec((1, tm, w), lambda b, i: (b, i, 0))
    in_specs = [
        tile(W), tile(W), tile(W),
        pl.BlockSpec((1, POOL_HALO, W), lambda b, i: (b, jnp.maximum(i * hb - 1, 0), 0)),
        pl.BlockSpec((1, POOL_HALO, W), lambda b, i: (b, jnp.minimum((i + 1) * hb, n_halo - 1), 0)),
        tile(N_BRANCH * D), tile(D),
        pl.BlockSpec((1, 8, D), lambda b, i: (b, 0, 0)),
        _const_spec((8, D)),
        _const_spec(pool_w.shape), _const_spec(pool_scale.shape), _const_spec(w_branch.shape),
        _const_spec(w_out.shape), _const_spec(w_router_t.shape), _const_spec(w_router_pad.shape),
    ]
    return pl.pallas_call(
        functools.partial(_merge_kernel, seq_len=T),
        grid=(B, T // tm),
        in_specs=in_specs,
        out_specs=[tile(D), tile(D // 2), tile(LANES), pl.BlockSpec((1, E, tm), lambda b, i: (b, 0, i))],
        out_shape=[jax.ShapeDtypeStruct((B, T, D), jnp.float32),
                   jax.ShapeDtypeStruct((B, T, D // 2), PACK_WORD),
                   jax.ShapeDtypeStruct((B, T, LANES), jnp.float32),
                   jax.ShapeDtypeStruct((B, E, T), jnp.float32)],
        scratch_shapes=[pltpu.VMEM((tm + 2 * POOL_HALO, W), jnp.float32),
                        pltpu.VMEM((2, min(MERGE_SUB, tm), D), jnp.float32)],
        compiler_params=pltpu.CompilerParams(
            dimension_semantics=("arbitrary", "arbitrary"), vmem_limit_bytes=VMEM_LIMIT),
        name="merge_out",
    )(m, n, pu, pu, pu, bg, x, mods, gains, pool_w, pool_scale, w_branch, w_out, w_router_t, w_router_pad)


def _cumsum_chunks(x01):
    R, T = x01.shape
    r = lax.broadcasted_iota(jnp.int32, (LANES, LANES), 0)
    c = lax.broadcasted_iota(jnp.int32, (LANES, LANES), 1)
    tri_u = jnp.where(r <= c, 1.0, 0.0).astype(jnp.bfloat16)
    run = jnp.zeros((R, 1), jnp.float32)
    outs = []
    for t0 in range(0, T, LANES):
        cs = _dot_f32(x01[:, t0:t0 + LANES].astype(jnp.bfloat16), tri_u) + run
        run = cs[:, LANES - 1:LANES]
        outs.append(cs)
    return outs


def _route_kernel(aff_ref, idx_ref, *, cap):
    f32, bf16 = jnp.float32, jnp.bfloat16
    aff = aff_ref[0]
    E, T = aff.shape
    n_chunks = T // LANES

    def count_ge(cand):
        return jnp.sum(jnp.where(aff >= pltpu.bitcast(cand, f32), 1.0, 0.0), axis=-1, keepdims=True)

    top = jnp.full((E, 1), 1 << 30, jnp.int32)
    lo0 = jnp.where(count_ge(top) >= cap, top, jnp.zeros((E, 1), jnp.int32))

    def search(k, lo):
        shift = 28 - 2 * k
        best = lo
        for q in (1, 2, 3):
            cand = lo | (jnp.int32(q) << shift)
            best = jnp.where(count_ge(cand) >= cap, cand, best)
        return best

    thr = pltpu.bitcast(lax.fori_loop(0, 15, search, lo0), f32)
    gt = jnp.where(aff > thr, 1.0, 0.0)
    eq = jnp.where(aff == thr, 1.0, 0.0)
    room = cap - jnp.sum(gt, axis=-1, keepdims=True)
    sel = gt + eq * jnp.where(jnp.concatenate(_cumsum_chunks(eq), axis=-1) <= room, 1.0, 0.0)
    ranks = [jnp.minimum(cs, float(cap)) for cs in _cumsum_chunks(sel)]
    sel_r = lax.broadcasted_iota(jnp.int32, (T, LANES), 0)
    sel_c = lax.broadcasted_iota(jnp.int32, (T, LANES), 1)
    pick_last = jnp.where(sel_r == sel_c * LANES + (LANES - 1), 1.0, 0.0).astype(bf16)
    chunk_end = _dot_f32(jnp.concatenate(ranks, axis=-1).astype(bf16), pick_last)
    slot = lax.broadcasted_iota(jnp.int32, (cap, LANES), 0).astype(f32)
    lane = lax.broadcasted_iota(jnp.int32, (cap, LANES), 1)
    pad = jnp.zeros((LANES - n_chunks, LANES), f32)
    out = jnp.zeros((cap, LANES), f32)
    for e in range(E):
        ends = jnp.where(lane < n_chunks, jnp.broadcast_to(chunk_end[e:e + 1, :], (cap, LANES)), float(cap))
        n_full = jnp.sum(jnp.where(ends <= slot, 1.0, 0.0), axis=-1, keepdims=True)
        chunk_ranks = jnp.concatenate([r[e:e + 1, :] for r in ranks] + [pad], axis=0)
        mine = _dot_f32(jnp.where(lane.astype(f32) == n_full, 1.0, 0.0).astype(bf16), chunk_ranks.astype(bf16))
        pos = LANES * n_full + jnp.sum(jnp.where(mine <= slot, 1.0, 0.0), axis=-1, keepdims=True)
        out = jnp.where(lane == e, jnp.broadcast_to(pos, (cap, LANES)), out)
    idx_ref[0] = out.astype(jnp.int32)


def route(aff_t, cap):
    B, E, T = aff_t.shape
    assert cap <= 256 and cap % 8 == 0 and T % LANES == 0
    idx_t = pl.pallas_call(
        functools.partial(_route_kernel, cap=cap),
        grid=(B,),
        in_specs=[pl.BlockSpec((1, E, T), lambda b: (b, 0, 0))],
        out_specs=pl.BlockSpec((1, cap, LANES), lambda b: (b, 0, 0)),
        out_shape=jax.ShapeDtypeStruct((B, cap, LANES), jnp.int32),
        compiler_params=pltpu.CompilerParams(dimension_semantics=("arbitrary",), vmem_limit_bytes=VMEM_LIMIT),
        name="route",
    )(aff_t)
    return idx_t[:, :, :E].transpose(0, 2, 1)


ROW_GROUP = 8


def _moe_kernel(idx_ref, rows_ref, affr_ref, wg_ref, wu_ref, wd_ref, out_ref, xg_ref, ag_ref, ye_ref):
    f32, bf16 = jnp.float32, jnp.bfloat16
    e = pl.program_id(1)
    n_exp = pl.num_programs(1)
    n_sets, cap = xg_ref.shape[1], xg_ref.shape[2]

    def gather_row(p, s, ee, j):
        i = idx_ref[0, s, ee, j]
        xg_ref[p, s, pl.ds(j, 1), :] = rows_ref[s, pl.ds(i, 1), :]
        ag_ref[p, s, pl.ds(j, 1), :] = affr_ref[s, pl.ds(i, 1), :]

    def scatter_group(p, s, ee, j0):
        ids = [idx_ref[0, s, ee, j0 + r] for r in range(ROW_GROUP)]
        cur = [out_ref[s, pl.ds(i, 1), :] for i in ids]
        for r, i in enumerate(ids):
            out_ref[s, pl.ds(i, 1), :] = cur[r] + ye_ref[p, s, pl.ds(j0 + r, 1), :]

    @pl.when(e == 0)
    def _():
        out_ref[...] = jnp.zeros_like(out_ref)
        ye_ref[1] = jnp.zeros_like(ye_ref[1])

        def body(gi, carry):
            for s in range(n_sets):
                for r in range(ROW_GROUP):
                    gather_row(0, s, 0, gi * ROW_GROUP + r)
            return carry
        lax.fori_loop(0, cap // ROW_GROUP, body, 0)

    def step(p):
        e_prv = jnp.maximum(e - 1, 0)
        e_nxt = jnp.minimum(e + 1, n_exp - 1)
        for s in range(n_sets):
            for j0 in range(0, cap, ROW_GROUP):
                scatter_group(1 - p, s, e_prv, j0)
            for j in range(cap):
                gather_row(1 - p, s, e_nxt, j)
        packed = jnp.concatenate([xg_ref[p, s] for s in range(n_sets)], axis=0)
        xe = jnp.concatenate(
            [pltpu.unpack_elementwise(packed, index=i, packed_dtype=bf16, unpacked_dtype=f32) for i in (0, 1)],
            axis=-1).astype(bf16)
        aff = jnp.concatenate([ag_ref[p, s] for s in range(n_sets)], axis=0)
        gate = jnp.sum(jnp.where(lax.broadcasted_iota(jnp.int32, aff.shape, 1) == e, aff, 0.0),
                       axis=-1, keepdims=True)
        a = _dot_f32(xe, wg_ref[0])
        hid = (a * _sigmoid(a) * _dot_f32(xe, wu_ref[0])).astype(bf16)
        ye = _dot_f32(hid, wd_ref[0]) * gate
        for s in range(n_sets):
            ye_ref[p, s] = ye[s * cap:(s + 1) * cap]

    @pl.when(e % 2 == 0)
    def _():
        step(0)

    @pl.when(e % 2 == 1)
    def _():
        step(1)

    @pl.when(e == n_exp - 1)
    def _():
        def body(gi, carry):
            for s in range(n_sets):
                scatter_group(1, s, e, gi * ROW_GROUP)
            return carry
        lax.fori_loop(0, cap // ROW_GROUP, body, 0)


MOE_SETS = 1


def moe_experts(rows, aff_rows, idx, w_gate, w_up, w_down, layer):
    G, T, RW = rows.shape
    AW = aff_rows.shape[2]
    _, E, cap = idx.shape
    D = w_gate.shape[2]
    ns = MOE_SETS
    assert E % 2 == 0 and cap % ROW_GROUP == 0 and G % ns == 0
    wspec = pl.BlockSpec((None, 1, D, D), lambda g, e: (layer, e, 0, 0))
    return pl.pallas_call(
        _moe_kernel,
        grid=(G // ns, E),
        in_specs=[pl.BlockSpec((1, ns, E, cap), lambda g, e: (g, 0, 0, 0), memory_space=pltpu.SMEM),
                  pl.BlockSpec((ns, T, RW), lambda g, e: (g, 0, 0)),
                  pl.BlockSpec((ns, T, AW), lambda g, e: (g, 0, 0)),
                  wspec, wspec, wspec],
        out_specs=pl.BlockSpec((ns, T, D), lambda g, e: (g, 0, 0)),
        out_shape=jax.ShapeDtypeStruct((G, T, D), jnp.float32),
        scratch_shapes=[pltpu.VMEM((2, ns, cap, RW), rows.dtype), pltpu.VMEM((2, ns, cap, AW), jnp.float32),
                        pltpu.VMEM((2, ns, cap, D), jnp.float32)],
        compiler_params=pltpu.CompilerParams(
            dimension_semantics=("arbitrary", "arbitrary"), vmem_limit_bytes=VMEM_LIMIT),
        name="moe_experts",
    )(idx.reshape(G // ns, ns, E, cap), rows, aff_rows, w_gate, w_up, w_down)


def _residual_norm_kernel(x_ref, y_ref, gate_ref, gain_ref, o_ref):
    y = y_ref[0]
    yn = y * lax.rsqrt(jnp.mean(y * y, axis=-1, keepdims=True) + RMS_EPS) * gain_ref[...]
    o_ref[0] = x_ref[0] + gate_ref[0] * yn


def residual_norm(x, y, gate, gain, *, tm):
    B, T, D = x.shape
    tile = pl.BlockSpec((1, tm, D), lambda b, i: (b, i, 0))
    return pl.pallas_call(
        _residual_norm_kernel,
        grid=(B, T // tm),
        in_specs=[tile, tile, pl.BlockSpec((1, 1, D), lambda b, i: (b, 0, 0)), _const_spec((1, D))],
        out_specs=tile,
        out_shape=jax.ShapeDtypeStruct((B, T, D), jnp.float32),
        compiler_params=pltpu.CompilerParams(
            dimension_semantics=("arbitrary", "arbitrary"), vmem_limit_bytes=VMEM_LIMIT),
        name="residual_norm",
    )(x, y, gate, gain)


ADA_COLS = 1536


def _ada_kernel(c_ref, w_ref, b_ref, o_ref):
    c = c_ref[...]
    h = (c * _sigmoid(c)).astype(jnp.bfloat16)
    o_ref[...] = _dot_f32(h, w_ref[...].astype(jnp.bfloat16)) + b_ref[...]


def ada_modulation(cond, ada_w, ada_b, layer):
    R, D = cond.shape
    N = ada_w.shape[2]
    return pl.pallas_call(
        _ada_kernel,
        grid=(N // ADA_COLS,),
        in_specs=[pl.BlockSpec((R, D), lambda j: (0, 0)),
                  pl.BlockSpec((None, D, ADA_COLS), lambda j: (layer, 0, j)),
                  pl.BlockSpec((None, 1, ADA_COLS), lambda j: (layer, 0, j))],
        out_specs=pl.BlockSpec((R, ADA_COLS), lambda j: (0, j)),
        out_shape=jax.ShapeDtypeStruct((R, N), jnp.float32),
        compiler_params=pltpu.CompilerParams(dimension_semantics=("arbitrary",), vmem_limit_bytes=VMEM_LIMIT),
        name="ada_modulation",
    )(cond, ada_w, ada_b[:, None, :])


def kernel(x, c, ctx, c_ctx, norm_gain, ada_w, ada_b, w_in, mlstm_gate_bias, pool_w, pool_scale,
           na_rpb, w_branch, w_out, router_w, w_gate, w_up, w_down):
    B, S, D = x.shape
    Tc = ctx.shape[1]
    f32, bf16 = jnp.float32, jnp.bfloat16
    rope_tabs = rope_lane_tables(S)
    names = [n for n, _ in IN_GROUPS] + ["gt"]
    cond = jnp.pad(jnp.concatenate([c, c_ctx[None]], axis=0), ((0, (-(B + 1)) % 8), (0, 0)))
    moe_w = (w_gate.astype(bf16), w_up.astype(bf16), w_down.astype(bf16))
    assert B % (S // Tc) == 0
    pending = pending_c = None
    for l in range(DEPTH):
        need_ctx = l < DEPTH - 1
        g = norm_gain[l]
        mods = ada_modulation(cond, ada_w, ada_b, l)
        mod_l = jnp.split(mods[:B, None, :], N_MOD, axis=-1)
        mod_cb = [jnp.broadcast_to(m[None, None], (B, 1, D)) for m in jnp.split(mods[B], N_MOD, axis=-1)]
        w_perm = permute_w_in(w_in[l])
        gb_row = jnp.pad(mlstm_gate_bias[l].reshape(1, -1).astype(f32), ((0, 0), (0, GATE_W - 4 * M_HEADS)))
        proj = in_proj(x, g[0][None], mod_l[0], mod_l[1], w_perm, gb_row, rope_tabs, pending, tm=512)
        if pending is not None:
            x = proj[-1]
        pl_ = dict(zip(names, proj))
        keep_c = None if need_ctx else ("mqk", "mvt", "kn", "vn", "g")
        proj_c = in_proj(ctx, g[0][None], mod_cb[0], mod_cb[1], w_perm, gb_row, None, pending_c, keep_c, tm=Tc)
        if pending_c is not None:
            ctx = proj_c[-1]
        pc_ = dict(zip(names if keep_c is None else list(keep_c) + ["gt"], proj_c))

        m_l, m_c = mlstm(pc_, pl_, need_ctx)
        n_l, n_c = neighbourhood_attention(pl_["qn"], pl_["kn"], pl_["vn"], pc_["kn"], pc_["vn"],
                                           na_bias_table(na_rpb[l]), pc_["qn"] if need_ctx else None)
        gains = jnp.pad(g[1:3], ((0, 6), (0, 0)))
        merge_w = (pool_w[l].astype(bf16), pool_scale[l][None], w_branch[l].astype(bf16),
                   (0.5 * w_out[l]).astype(bf16),
                   router_w[l].T.astype(bf16),
                   jnp.pad(router_w[l], ((0, 0), (0, LANES - N_EXPERTS))).astype(bf16))

        def mods8(mods):
            return jnp.pad(jnp.concatenate([mods[2], mods[3], mods[4]], axis=1), ((0, 0), (0, 5), (0, 0)))

        x1, rows, aff_r, aff_t = merge_out(m_l, n_l, pl_["pu"], pl_["bg"], x, mods8(mod_l), gains, *merge_w, tm=1024)
        y = moe_experts(rows, aff_r, route(aff_t, EC_FACTOR * S // N_EXPERTS), *moe_w, l)
        if need_ctx:
            x, pending = x1, (y, mod_l[5], g[3][None])
        else:
            x = residual_norm(x1, y, mod_l[5], g[3][None], tm=512)
        if need_ctx:
            c1, rows, aff_r, aff_t = merge_out(m_c, n_c, pc_["pu"], pc_["bg"], ctx, mods8(mod_cb), gains, *merge_w, tm=Tc)
            per = S // Tc
            cap_c = EC_FACTOR * Tc // N_EXPERTS
            idx = route(aff_t, cap_c) + (jnp.arange(B, dtype=jnp.int32) % per * Tc)[:, None, None]
            idx = idx.reshape(B // per, per, N_EXPERTS, cap_c).transpose(0, 2, 1, 3).reshape(B // per, N_EXPERTS, per * cap_c)
            y = moe_experts(rows.reshape(B // per, S, D // 2), aff_r.reshape(B // per, S, LANES), idx,
                            *moe_w, l).reshape(B, Tc, D)
            ctx, pending_c = c1, (y, mod_cb[5], g[3][None])
    return x
```

```python
import functools
import math

import jax
import jax.numpy as jnp
import numpy as np
from jax import lax
from jax.experimental import pallas as pl
from jax.experimental.pallas import tpu as pltpu

D_MODEL = 1024
DEPTH = 2
GRID_W = 64
BRANCH_W = D_MODEL // 2
N_BRANCH = 3
M_HEADS = 4
M_DV = BRANCH_W // M_HEADS
M_DQK = M_DV // 2
M_CHUNK = 128
M_INIT = -1e30
M_ONES = 16
POOL_GROUPS = 4
POOL_GC = BRANCH_W // POOL_GROUPS
POOL_WINDOWS = (2, 4, 8, 16)
NA_HEADS = 8
NA_DH = BRANCH_W // NA_HEADS
NA_KH = 8
NA_KW = 16
N_EXPERTS = 16
EC_FACTOR = 2
ROPE_BASE = 10000.0
RMS_EPS = 1e-6
N_MOD = 6
SPLIT_SIZES = (M_HEADS * M_DQK, M_HEADS * M_DQK, M_HEADS * M_DV, M_HEADS * M_DV, 4 * M_HEADS,
               POOL_GROUPS * POOL_GC, NA_HEADS * NA_DH, NA_HEADS * NA_DH, NA_HEADS * NA_DH,
               N_BRANCH * D_MODEL)
PROJ_W = sum(SPLIT_SIZES)

LANES = 128
VMEM_LIMIT = 56 * 1024 * 1024
MASK_NEG = -1e30
GATE_W = LANES

IN_GROUPS = (("mqk", 2 * M_HEADS * M_DQK), ("mvt", BRANCH_W), ("mo", BRANCH_W), ("pu", BRANCH_W),
             ("qn", BRANCH_W), ("kn", BRANCH_W), ("vn", BRANCH_W), ("bg", N_BRANCH * D_MODEL),
             ("g", GATE_W))
IN_W = sum(w for _, w in IN_GROUPS)
MM_COLS = 512
PACK_WORD = jnp.uint32


def _sigmoid(x):
    return 0.5 * jnp.tanh(0.5 * x) + 0.5


def _const_spec(shape):
    nd = len(shape)
    return pl.BlockSpec(shape, lambda *_: (0,) * nd, pipeline_mode=pl.Buffered(1))


def _rope_tile(x, cos, sin_signed):
    half = M_DQK // 4
    lane = lax.broadcasted_iota(jnp.int32, x.shape, 1)
    partner = jnp.where((lane % (2 * half)) < half, pltpu.roll(x, LANES - half, 1), pltpu.roll(x, half, 1))
    return x * cos + partner * sin_signed


def _in_proj_kernel(x_ref, g_ref, shift_ref, scale_ref, w_ref, gb_ref, *rest, rope, pending, keep):
    if rope:
        cos_ref, sin_ref = rest[:2]
        rest = rest[2:]
    if pending:
        r_ref, rgate_ref, rgain_ref = rest[:3]
        rest, xo_ref = rest[3:-1], rest[-1]
        r = r_ref[0]
        x = x_ref[0] + rgate_ref[0] * (r * lax.rsqrt(jnp.mean(r * r, axis=-1, keepdims=True) + RMS_EPS)
                                       * rgain_ref[...])
        xo_ref[0] = x
    else:
        x = x_ref[0]
    o_refs, gt_ref = rest[:-1], rest[-1]
    y = x * lax.rsqrt(jnp.mean(x * x, axis=-1, keepdims=True) + RMS_EPS)
    h = ((y * g_ref[...]) * (1.0 + scale_ref[0]) + shift_ref[0]).astype(jnp.bfloat16)
    off = 0
    o_iter = iter(o_refs)
    for name, width in IN_GROUPS:
        if name not in keep:
            off += width
            continue
        o_ref = next(o_iter)
        for c0 in range(0, width, MM_COLS):
            cw = min(MM_COLS, width - c0)
            acc = jnp.dot(h, w_ref[:, off + c0:off + c0 + cw], preferred_element_type=jnp.float32)
            if name == "g":
                acc = acc + gb_ref[...]
                for c in range(acc.shape[0] // M_CHUNK):
                    gt_ref[0, c] = acc[c * M_CHUNK:(c + 1) * M_CHUNK, :].T[:4 * M_HEADS, :]
            if name == "mqk" and rope:
                cos, sin = cos_ref[...], sin_ref[...]
                acc = jnp.concatenate([_rope_tile(acc[:, t:t + LANES], cos, sin) for t in range(0, cw, LANES)], axis=-1)
            if name == "mvt":
                for c in range(acc.shape[0] // M_CHUNK):
                    o_ref[0, c] = acc[c * M_CHUNK:(c + 1) * M_CHUNK, :].T.astype(o_ref.dtype)
                continue
            o_ref[0, :, c0:c0 + cw] = acc.astype(o_ref.dtype)
        off += width


def in_proj(x, g, shift, scale, w_perm, gate_bias_row, rope_tabs=None, pending=None, keep=None, *, tm):
    B, T, D = x.shape
    keep = tuple(n for n, _ in IN_GROUPS) if keep is None else tuple(keep)
    assert "g" in keep
    out_shape = [jax.ShapeDtypeStruct((B, T, w), jnp.float32 if n == "g" else jnp.bfloat16) for n, w in IN_GROUPS]
    out_specs = [pl.BlockSpec((1, tm, w), lambda b, i: (b, i, 0)) for _, w in IN_GROUPS]
    k_mvt = [n for n, _ in IN_GROUPS].index("mvt")
    assert BRANCH_W == MM_COLS and tm % M_CHUNK == 0
    out_shape[k_mvt] = jax.ShapeDtypeStruct((B, T // M_CHUNK, BRANCH_W, M_CHUNK), jnp.bfloat16)
    out_specs[k_mvt] = pl.BlockSpec((1, tm // M_CHUNK, BRANCH_W, M_CHUNK), lambda b, i: (b, i, 0, 0))
    kept = [k for k, (n, _) in enumerate(IN_GROUPS) if n in keep]
    out_shape, out_specs = [out_shape[k] for k in kept], [out_specs[k] for k in kept]
    out_shape.append(jax.ShapeDtypeStruct((B, T // M_CHUNK, 4 * M_HEADS, M_CHUNK), jnp.float32))
    out_specs.append(pl.BlockSpec((1, tm // M_CHUNK, 4 * M_HEADS, M_CHUNK), lambda b, i: (b, i, 0, 0)))
    in_specs = [
        pl.BlockSpec((1, tm, D), lambda b, i: (b, i, 0)),
        _const_spec((1, D)),
        pl.BlockSpec((1, 1, D), lambda b, i: (b, 0, 0)),
        pl.BlockSpec((1, 1, D), lambda b, i: (b, 0, 0)),
        _const_spec((D, IN_W)),
        _const_spec((1, GATE_W)),
    ]
    args = [x, g, shift, scale, w_perm, gate_bias_row]
    if rope_tabs is not None:
        in_specs += [pl.BlockSpec((tm, LANES), lambda b, i: (i, 0))] * 2
        args += list(rope_tabs)
    if pending is not None:
        in_specs += [pl.BlockSpec((1, tm, D), lambda b, i: (b, i, 0)), pl.BlockSpec((1, 1, D), lambda b, i: (b, 0, 0)),
                     _const_spec((1, D))]
        args += list(pending)
        out_shape.append(jax.ShapeDtypeStruct((B, T, D), jnp.float32))
        out_specs.append(pl.BlockSpec((1, tm, D), lambda b, i: (b, i, 0)))
    return pl.pallas_call(
        functools.partial(_in_proj_kernel, rope=rope_tabs is not None, pending=pending is not None, keep=keep),
        grid=(B, T // tm),
        in_specs=in_specs,
        out_specs=out_specs,
        out_shape=out_shape,
        compiler_params=pltpu.CompilerParams(
            dimension_semantics=("arbitrary", "arbitrary"), vmem_limit_bytes=VMEM_LIMIT),
        name="in_proj",
    )(*args)


def rope_lane_tables(n_tokens):
    t = jnp.arange(n_tokens)
    row = (t // GRID_W).astype(jnp.float32)
    col = (t % GRID_W).astype(jnp.float32)
    half = M_DQK // 2
    inv = ROPE_BASE ** (-jnp.arange(0, half, 2, dtype=jnp.float32) / half)
    ar = row[:, None] * inv[None, :]
    ac = col[:, None] * inv[None, :]
    cos = jnp.concatenate([jnp.cos(ar), jnp.cos(ar), jnp.cos(ac), jnp.cos(ac)], axis=-1)
    sin = jnp.concatenate([-jnp.sin(ar), jnp.sin(ar), -jnp.sin(ac), jnp.sin(ac)], axis=-1)
    return jnp.tile(cos, (1, LANES // M_DQK)), jnp.tile(sin, (1, LANES // M_DQK))


def permute_w_in(w_in_l):
    qm, km, vm, om, gm, pu, qn, kn, vn, bg = jnp.split(w_in_l, [int(s) for s in np.cumsum(SPLIT_SIZES)[:-1]], axis=-1)
    gpad = jnp.pad(gm, ((0, 0), (0, GATE_W - gm.shape[1])))
    w = jnp.concatenate([qm * (M_DQK ** -0.5), km, vm, om, pu, qn * (NA_DH ** -0.5), kn, vn, bg * 0.5, gpad], axis=-1)
    return w.astype(jnp.bfloat16)


def _pair_scores(qp, k_parts, biases):
    n = qp.shape[0]
    lane_lo = lax.broadcasted_iota(jnp.int32, (n, LANES), 1) < NA_DH
    zero = jnp.zeros_like(qp)
    qq = jnp.concatenate([jnp.where(lane_lo, qp, zero), jnp.where(lane_lo, zero, qp)], axis=0)
    scores = []
    for kp, bias in zip(k_parts, biases):
        s = lax.dot_general(qq, kp, (((1,), (1,)), ((), ())), preferred_element_type=jnp.float32)
        scores.append(s if bias is None else s + bias)
    return scores


def _pair_softmax_pv(scores, v_parts):
    n = scores[0].shape[0] // 2
    lane_lo = lax.broadcasted_iota(jnp.int32, (n, LANES), 1) < NA_DH
    m = scores[0].max(axis=-1, keepdims=True)
    for s in scores[1:]:
        m = jnp.maximum(m, s.max(axis=-1, keepdims=True))
    l = None
    o = None
    for s, vp in zip(scores, v_parts):
        p = jnp.exp((s - m).astype(jnp.bfloat16))
        ls = p.astype(jnp.float32).sum(axis=-1, keepdims=True)
        os_ = jnp.dot(p, vp, preferred_element_type=jnp.float32)
        l = ls if l is None else l + ls
        o = os_ if o is None else o + os_
    o = o * (1.0 / l)
    return jnp.where(lane_lo, o[:n], o[n:])


def _na_kernel(q_ref, k_ref, v_ref, kc_ref, vc_ref, bias_ref, *rest, need_ctx):
    if need_ctx:
        qc_ref, o_ref, oc_ref, s_ref = rest
    else:
        o_ref, s_ref = rest
    S = q_ref.shape[1]
    Tc = kc_ref.shape[1]
    rows = S // GRID_W
    n_pairs = BRANCH_W // LANES
    n_loc = NA_KH * GRID_W

    def window(r):
        rs = jnp.clip(r - NA_KH // 2, 0, rows - NA_KH)
        return r - rs, pl.multiple_of(r * GRID_W, GRID_W), pl.multiple_of(rs * GRID_W, GRID_W)

    def scores_stage(r, slot, j):
        var, q0, k0 = window(r)
        ls = slice(j * LANES, (j + 1) * LANES)
        s_loc, s_ctx = _pair_scores(q_ref[0, pl.ds(q0, GRID_W), ls],
                                    (k_ref[0, pl.ds(k0, n_loc), ls], kc_ref[0, :, ls]),
                                    (bias_ref[var, j], None))
        s_ref[slot, j, :, :n_loc] = s_loc
        s_ref[slot, j, :, n_loc:] = s_ctx

    def output_stage(r, slot, j):
        _, q0, k0 = window(r)
        ls = slice(j * LANES, (j + 1) * LANES)
        o = _pair_softmax_pv((s_ref[slot, j, :, :n_loc], s_ref[slot, j, :, n_loc:]),
                             (v_ref[0, pl.ds(k0, n_loc), ls], vc_ref[0, :, ls]))
        o_ref[0, pl.ds(q0, GRID_W), ls] = o.astype(o_ref.dtype)

    for j in range(n_pairs):
        scores_stage(0, 0, j)

    def two_rows(i, carry):
        r0 = 2 * i
        for j in range(n_pairs):
            scores_stage(r0 + 1, 1, j)
            output_stage(r0, 0, j)
        for j in range(n_pairs):
            scores_stage(jnp.minimum(r0 + 2, rows - 1), 0, j)
            output_stage(r0 + 1, 1, j)
        return carry

    lax.fori_loop(0, rows // 2, two_rows, 0)

    if need_ctx:
        outs = []
        for j in range(n_pairs):
            ls = slice(j * LANES, (j + 1) * LANES)
            scores = _pair_scores(qc_ref[0, :, ls], (kc_ref[0, :, ls],), (None,))
            outs.append(_pair_softmax_pv(scores, (vc_ref[0, :, ls],)))
        oc_ref[0] = jnp.concatenate(outs, axis=-1).astype(oc_ref.dtype)


def na_bias_table(rpb):
    H = rpb.shape[0]
    var = jnp.arange(NA_KH)
    kr = jnp.arange(NA_KH)
    dr = kr[None, :] - var[:, None] + NA_KH - 1
    cols = jnp.arange(GRID_W)
    dc = jnp.clip(cols[None, :] - cols[:, None] + NA_KW - 1, 0, 2 * NA_KW - 2)
    cs = jnp.clip(cols - NA_KW // 2, 0, GRID_W - NA_KW)
    colmask = (cols[None, :] >= cs[:, None]) & (cols[None, :] < cs[:, None] + NA_KW)
    pick_r = jax.nn.one_hot(dr, 2 * NA_KH - 1, dtype=jnp.float32)
    pick_c = jax.nn.one_hot(dc, 2 * NA_KW - 1, dtype=jnp.float32)
    tab = jnp.einsum('vka,hab,qcb->vhqkc', pick_r, rpb.astype(jnp.float32), pick_c, precision=lax.Precision.HIGHEST)
    tab = jnp.where(colmask[None, None, :, None, :], tab, MASK_NEG)
    return tab.reshape(NA_KH, H // 2, 2 * GRID_W, NA_KH * GRID_W)


def neighbourhood_attention(qn, kn, vn, kc, vc, bias_tab, qc=None):
    B, S, W = qn.shape
    Tc = kc.shape[1]
    need_ctx = qc is not None
    lat = pl.BlockSpec((1, S, W), lambda b: (b, 0, 0))
    cx = pl.BlockSpec((1, Tc, W), lambda b: (b, 0, 0))
    in_specs = [lat, lat, lat, cx, cx, _const_spec(bias_tab.shape)]
    args = [qn, kn, vn, kc, vc, bias_tab]
    out_shape = [jax.ShapeDtypeStruct((B, S, W), jnp.bfloat16)]
    out_specs = [lat]
    if need_ctx:
        in_specs.append(cx)
        args.append(qc)
        out_shape.append(jax.ShapeDtypeStruct((B, Tc, W), jnp.bfloat16))
        out_specs.append(cx)
    res = pl.pallas_call(
        functools.partial(_na_kernel, need_ctx=need_ctx),
        grid=(B,),
        in_specs=in_specs,
        out_specs=out_specs,
        out_shape=out_shape,
        scratch_shapes=[pltpu.VMEM((2, W // LANES, 2 * GRID_W, NA_KH * GRID_W + Tc), jnp.float32)],
        compiler_params=pltpu.CompilerParams(dimension_semantics=("arbitrary",), vmem_limit_bytes=VMEM_LIMIT),
        name="neighbourhood_attention",
    )(*args)
    return (res[0], res[1]) if need_ctx else (res[0], None)


def _split_bf16(x):
    hi = x.astype(jnp.bfloat16)
    r1 = x - hi.astype(jnp.float32)
    mid = r1.astype(jnp.bfloat16)
    lo = (r1 - mid.astype(jnp.float32)).astype(jnp.bfloat16)
    return hi, mid, lo


def _dot_f32(a, b):
    return jnp.dot(a, b, preferred_element_type=jnp.float32)


def _log_sigmoid(x):
    return jnp.minimum(x, 0.0) - jnp.log(1.0 + jnp.exp(-jnp.abs(x)))


def _pair_queries(qp):
    lane_lo = lax.broadcasted_iota(jnp.int32, qp.shape, 1) < M_DQK
    zq = jnp.zeros_like(qp)
    return jnp.concatenate([jnp.where(lane_lo, qp, zq), jnp.where(lane_lo, zq, qp)], axis=0)


def _mlstm_prep(qk_ref, g_ref, gt_ref, sa_ref, ra_ref, slot, ci, fwd):
    L = M_CHUNK
    f32, bf16 = jnp.float32, jnp.bfloat16
    d = 0 if fwd else 1
    rows = pl.ds(pl.multiple_of(ci * L, L), L)
    sq_r = lax.broadcasted_iota(jnp.int32, (L, L), 0)
    sq_c = lax.broadcasted_iota(jnp.int32, (L, L), 1)
    tri_l = jnp.where(sq_r >= sq_c, 1.0, 0.0).astype(bf16)
    tri_u = jnp.where(sq_r <= sq_c, 1.0, 0.0).astype(bf16)
    gates = g_ref[0, rows, :]
    gates_t = gt_ref[0, ci]
    lf, lf_t = _log_sigmoid(gates), _log_sigmoid(gates_t)
    cum = sum(_dot_f32(tri_l if fwd else tri_u, p) for p in _split_bf16(lf))
    cum_t = sum(_dot_f32(p, tri_u if fwd else tri_l) for p in _split_bf16(lf_t))
    t_i = 0 if fwd else 2
    st_s = lax.broadcasted_iota(jnp.int32, (L, 2 * L), 0)
    st_t = lax.broadcasted_iota(jnp.int32, (L, 2 * L), 1) & (L - 1)
    valid = (st_s <= st_t) if fwd else (st_s >= st_t)

    def row2(tile, c0, c1):
        return jnp.concatenate([tile[c0:c0 + 1, :], tile[c1:c1 + 1, :]], axis=-1)

    for j in range(M_HEADS // 2):
        ci0, ci1 = t_i * M_HEADS + 2 * j, t_i * M_HEADS + 2 * j + 1
        cf0, cf1 = ci0 + M_HEADS, ci1 + M_HEADS
        b_row, i_row = row2(cum_t, cf0, cf1), row2(gates_t, ci0, ci1)
        colb = jnp.concatenate([jnp.broadcast_to(gates[:, ci0:ci0 + 1] - cum[:, cf0:cf0 + 1], (L, L)),
                                jnp.broadcast_to(gates[:, ci1:ci1 + 1] - cum[:, cf1:cf1 + 1], (L, L))],
                               axis=-1)
        qp = qk_ref[0, rows, j * LANES:(j + 1) * LANES]
        kp = qk_ref[0, rows, M_HEADS * M_DQK + j * LANES:M_HEADS * M_DQK + (j + 1) * LANES]
        sa_ref[slot, d, j, 0] = jnp.where(valid, b_row + colb, MASK_NEG)
        sa_ref[slot, d, j, 1] = lax.dot_general(kp, _pair_queries(qp), (((1,), (1,)), ((), ())),
                                                preferred_element_type=f32)
        ra_ref[slot, d, j, 0:1, :] = b_row
        ra_ref[slot, d, j, 1:2, :] = i_row


def _mlstm_step(qk_ref, vt_ref, sa_ref, ra_ref, slot, h_ref, ct_ref, m_ref, ci, fwd):
    L = M_CHUNK
    f32, bf16 = jnp.float32, jnp.bfloat16
    d = 0 if fwd else 1
    rows = pl.ds(pl.multiple_of(ci * L, L), L)
    last = L - 1 if fwd else 0
    lane_lo = lax.broadcasted_iota(jnp.int32, (L, LANES), 1) < M_DQK
    lane_lo_row = lax.broadcasted_iota(jnp.int32, (1, LANES), 1) < M_DQK
    ones = jnp.ones((M_ONES, L), bf16)

    def halves(row, f):
        return jnp.concatenate([jnp.broadcast_to(f(row[:, :L]), (1, L)), jnp.broadcast_to(f(row[:, L:]), (1, L))],
                               axis=-1)

    for j in range(M_HEADS // 2):
        h0, h1 = 2 * j, 2 * j + 1
        sidx = 2 * j + d
        b_row, i_row = ra_ref[slot, d, j, 0:1, :], ra_ref[slot, d, j, 1:2, :]
        m_row = m_ref[sidx, 0:1, :]
        kp = qk_ref[0, rows, M_HEADS * M_DQK + j * LANES:M_HEADS * M_DQK + (j + 1) * LANES]
        vx0 = jnp.concatenate([vt_ref[0, ci, h0 * M_DV:(h0 + 1) * M_DV, :], ones], axis=0)
        vx1 = jnp.concatenate([vt_ref[0, ci, h1 * M_DV:(h1 + 1) * M_DV, :], ones], axis=0)
        ct = ct_ref[sidx]
        if h_ref is not None:
            dm = sa_ref[slot, d, j, 0]
            qq = _pair_queries(qk_ref[0, rows, j * LANES:(j + 1) * LANES])
            g_row = b_row + m_row
            mt = jnp.maximum(g_row, dm.max(axis=0, keepdims=True))
            sc = (sa_ref[slot, d, j, 1] * jnp.exp(dm - mt)).astype(bf16)
            inter = jnp.exp(g_row - mt)
            t1 = lax.dot_general(ct.astype(bf16), qq, (((1,), (1,)), ((), ())), preferred_element_type=f32)
            t2 = jnp.concatenate([_dot_f32(vx0, sc[:, :L]), _dot_f32(vx1, sc[:, L:])], axis=-1)
            tot = inter * t1 + t2
            h_t = tot[:M_DV] * (1.0 / jnp.maximum(jnp.abs(tot[M_DV:M_DV + 1]), jnp.exp(-mt)))
            h_ref[rows, h0 * M_DV:(h0 + 1) * M_DV] = h_t[:, :L].T
            h_ref[rows, h1 * M_DV:(h1 + 1) * M_DV] = h_t[:, L:].T
        bl = halves(b_row, lambda r: r[:, last:last + 1])
        w = bl - b_row + i_row
        m_new = jnp.maximum(bl + m_row, halves(w, lambda r: r.max(axis=-1, keepdims=True)))
        decay = jnp.exp(bl + m_row - m_new)
        ws = jnp.exp(w - m_new)
        vxs = jnp.concatenate([vx0.astype(f32) * ws[:, :L], vx1.astype(f32) * ws[:, L:]], axis=-1).astype(bf16)
        zk = jnp.zeros_like(kp)
        kk = jnp.concatenate([jnp.where(lane_lo, kp, zk), jnp.where(lane_lo, zk, kp)], axis=0)
        d_cols = jnp.where(lane_lo_row, jnp.broadcast_to(decay[:, 0:1], (1, LANES)),
                           jnp.broadcast_to(decay[:, L:L + 1], (1, LANES)))
        ct_ref[sidx] = d_cols * ct + _dot_f32(vxs, kk)
        m_ref[sidx, 0:1, :] = m_new


def _mlstm_finish(hf_ref, hb_ref, o_ref, out_ref, n_chunks):
    L = M_CHUNK

    def body(ci, carry):
        rows = pl.ds(pl.multiple_of(ci * L, L), L)
        h = hf_ref[rows, :] + hb_ref[rows, :]
        parts = []
        for hd in range(M_HEADS):
            hh = h[:, hd * M_DV:(hd + 1) * M_DV]
            parts.append(hh * lax.rsqrt(jnp.mean(hh * hh, axis=-1, keepdims=True) + RMS_EPS))
        gate = _sigmoid(o_ref[0, rows, :].astype(jnp.float32))
        out_ref[0, rows, :] = (gate * jnp.concatenate(parts, axis=-1)).astype(out_ref.dtype)
        return carry

    lax.fori_loop(0, n_chunks, body, 0)


def _mlstm_kernel(qk_c, v_c, g_c, gt_c, qk_l, v_l, g_l, gt_l, o_l, *rest, need_ctx):
    if need_ctx:
        o_c, out_l, out_c, hf_l, hb_l, hf_c, hb_c, ct_ref, m_ref, sa_ref, ra_ref = rest
    else:
        out_l, hf_l, hb_l, ct_ref, m_ref, sa_ref, ra_ref = rest
        hf_c = hb_c = None
    nc, nl = qk_c.shape[1] // M_CHUNK, qk_l.shape[1] // M_CHUNK
    ct_ref[...] = jnp.zeros_like(ct_ref)
    m_ref[...] = jnp.full_like(m_ref, M_INIT)

    def phase(qk, vt, g, gt, hf, hb, n):
        def prep(s, slot):
            _mlstm_prep(qk, g, gt, sa_ref, ra_ref, slot, s, True)
            _mlstm_prep(qk, g, gt, sa_ref, ra_ref, slot, n - 1 - s, False)

        def step(s, slot):
            _mlstm_step(qk, vt, sa_ref, ra_ref, slot, hf, ct_ref, m_ref, s, True)
            _mlstm_step(qk, vt, sa_ref, ra_ref, slot, hb, ct_ref, m_ref, n - 1 - s, False)

        prep(0, 0)

        def body(i, carry):
            s0 = 2 * i
            prep(s0 + 1, 1)
            step(s0, 0)
            prep(jnp.minimum(s0 + 2, n - 1), 0)
            step(s0 + 1, 1)
            return carry
        lax.fori_loop(0, n // 2, body, 0)

    phase(qk_c, v_c, g_c, gt_c, hf_c, hb_c, nc)
    phase(qk_l, v_l, g_l, gt_l, hf_l, hb_l, nl)
    _mlstm_finish(hf_l, hb_l, o_l, out_l, nl)
    if need_ctx:
        _mlstm_finish(hf_c, hb_c, o_c, out_c, nc)


def mlstm(pc, pl_, need_ctx):
    B, S, W = pl_["mo"].shape
    Tc = pc["mqk"].shape[1]
    f32 = jnp.float32

    def row_spec(T):
        return pl.BlockSpec((1, T, W), lambda b: (b, 0, 0))

    def specs(T):
        return [row_spec(T), pl.BlockSpec((1, T // M_CHUNK, W, M_CHUNK), lambda b: (b, 0, 0, 0)),
                pl.BlockSpec((1, T, GATE_W), lambda b: (b, 0, 0)),
                pl.BlockSpec((1, T // M_CHUNK, 4 * M_HEADS, M_CHUNK), lambda b: (b, 0, 0, 0))]

    def args(p):
        return [p["mqk"], p["mvt"], p["g"], p["gt"]]

    gate_specs = [row_spec(S)] + ([row_spec(Tc)] if need_ctx else [])
    gate_args = [pl_["mo"]] + ([pc["mo"]] if need_ctx else [])

    out_shape = [jax.ShapeDtypeStruct((B, S, W), jnp.bfloat16)]
    out_specs = [pl.BlockSpec((1, S, W), lambda b: (b, 0, 0))]
    scratch = [pltpu.VMEM((S, W), f32), pltpu.VMEM((S, W), f32)]
    if need_ctx:
        out_shape.append(jax.ShapeDtypeStruct((B, Tc, W), jnp.bfloat16))
        out_specs.append(pl.BlockSpec((1, Tc, W), lambda b: (b, 0, 0)))
        scratch += [pltpu.VMEM((Tc, W), f32), pltpu.VMEM((Tc, W), f32)]
    assert (S // M_CHUNK) % 2 == 0 and (Tc // M_CHUNK) % 2 == 0
    n_pairs = M_HEADS // 2
    scratch += [pltpu.VMEM((M_HEADS, M_DV + M_ONES, LANES), f32), pltpu.VMEM((M_HEADS, 8, 2 * M_CHUNK), f32),
                pltpu.VMEM((2, 2, n_pairs, 2, M_CHUNK, 2 * M_CHUNK), f32),
                pltpu.VMEM((2, 2, n_pairs, 8, 2 * M_CHUNK), f32)]
    res = pl.pallas_call(
        functools.partial(_mlstm_kernel, need_ctx=need_ctx),
        grid=(B,),
        in_specs=specs(Tc) + specs(S) + gate_specs,
        out_specs=out_specs,
        out_shape=out_shape,
        scratch_shapes=scratch,
        compiler_params=pltpu.CompilerParams(dimension_semantics=("arbitrary",), vmem_limit_bytes=VMEM_LIMIT),
        name="mlstm",
    )(*args(pc), *args(pl_), *gate_args)
    return (res[0], res[1]) if need_ctx else (res[0], None)


POOL_HALO = 16


def _pool_mix(ext, inv_cnt, pw_ref, ps_ref):
    n_ext = ext.shape[0]
    ts = n_ext - 2 * POOL_HALO

    def shifted(a, d):
        return pltpu.roll(a, (-d) % n_ext, 0)

    outs = []
    for gi, w in enumerate(POOL_WINDOWS):
        s = ext[:, gi * POOL_GC:(gi + 1) * POOL_GC]
        cur = s[POOL_HALO:POOL_HALO + ts]
        s = shifted(s, -1) + s
        span = 2
        while span < w:
            s = shifted(s, -(span // 2)) + shifted(s, span // 2)
            span *= 2
        mean = s[POOL_HALO:POOL_HALO + ts] * inv_cnt[:, gi * POOL_GC:(gi + 1) * POOL_GC]
        outs.append(_dot_f32((mean - cur).astype(jnp.bfloat16), pw_ref[gi]))
    return jnp.concatenate(outs, axis=-1) * ps_ref[...]


def _pack_bf16_pairs(h):
    n = h.shape[1] // 2
    return pltpu.pack_elementwise([h[:, :n], h[:, n:]], packed_dtype=jnp.bfloat16)


MERGE_SUB = 256


def _merge_kernel(m_ref, n_ref, pu_ref, prev_ref, next_ref, inv_ref, bg_ref, x_ref, mods_ref, gains_ref,
                  pw_ref, ps_ref, wb_ref, wo_ref, wrt_ref, wrp_ref, x1_ref, h2p_ref, affr_ref, aff_ref,
                  ext_ref, y_ref):
    f32, bf16 = jnp.float32, jnp.bfloat16
    i = pl.program_id(1)
    tm, D = x_ref.shape[1], x_ref.shape[2]
    E = wrt_ref.shape[0]
    ts = min(MERGE_SUB, tm)
    ext_ref[0:POOL_HALO] = jnp.where(i > 0, prev_ref[0].astype(f32), 0.0)
    ext_ref[POOL_HALO:POOL_HALO + tm] = pu_ref[0].astype(f32)
    ext_ref[POOL_HALO + tm:] = jnp.where(i < pl.num_programs(1) - 1, next_ref[0].astype(f32), 0.0)

    def mix_stage(k):
        rows = slice(k * ts, (k + 1) * ts)
        p = _pool_mix(ext_ref[k * ts:(k + 1) * ts + 2 * POOL_HALO], inv_ref[rows], pw_ref, ps_ref)
        acc = None
        for bi, br in enumerate((m_ref[0, rows], p.astype(bf16), n_ref[0, rows])):
            gate = jnp.tanh(bg_ref[0, rows, bi * D:(bi + 1) * D].astype(f32)) + 1.0
            term = gate * _dot_f32(br, wb_ref[bi])
            acc = term if acc is None else acc + term
        y_ref[k % 2] = _dot_f32(acc.astype(bf16), wo_ref[...])

    def out_stage(k):
        rows = slice(k * ts, (k + 1) * ts)
        y = y_ref[k % 2]
        yn = y * lax.rsqrt(jnp.mean(y * y, axis=-1, keepdims=True) + RMS_EPS) * gains_ref[0:1, :]
        x1 = x_ref[0, rows] + mods_ref[0, 0:1, :] * yn
        x1_ref[0, rows] = x1
        xn = x1 * lax.rsqrt(jnp.mean(x1 * x1, axis=-1, keepdims=True) + RMS_EPS) * gains_ref[1:2, :]
        h2 = xn * (1.0 + mods_ref[0, 2:3, :]) + mods_ref[0, 1:2, :]
        h2b = h2.astype(bf16)
        lg_t = lax.dot_general(wrt_ref[...], h2b, (((1,), (1,)), ((), ())), preferred_element_type=f32)
        e_t = jnp.exp(lg_t - lg_t.max(axis=0, keepdims=True))
        aff_ref[0, :, rows] = e_t / e_t.sum(axis=0, keepdims=True)
        lg = _dot_f32(h2b, wrp_ref[...])
        lg = jnp.where(lax.broadcasted_iota(jnp.int32, lg.shape, 1) < E, lg, MASK_NEG)
        e_r = jnp.exp(lg - lg.max(axis=-1, keepdims=True))
        affr_ref[0, rows] = e_r / e_r.sum(axis=-1, keepdims=True)
        h2p_ref[0, rows] = _pack_bf16_pairs(h2)

    n_sub = tm // ts
    mix_stage(0)
    for k in range(n_sub):
        if k + 1 < n_sub:
            mix_stage(k + 1)
        out_stage(k)


def pool_inv_counts(seq_len):
    t = jnp.arange(seq_len)
    cols = []
    for w in POOL_WINDOWS:
        lo = jnp.clip(t - w // 2, 0, seq_len - 1)
        hi = jnp.clip(t - w // 2 + w - 1, 0, seq_len - 1)
        cols.append(jnp.broadcast_to((1.0 / (hi - lo + 1).astype(jnp.float32))[:, None], (seq_len, POOL_GC)))
    return jnp.concatenate(cols, axis=-1)


def merge_out(m, n, pu, bg, x, mods, gains, pool_w, pool_scale, w_branch, w_out, w_router_t, w_router_pad, *, tm):
    B, T, D = x.shape
    W = m.shape[2]
    E = w_router_t.shape[0]
    hb = tm // POOL_HALO
    n_halo = T // POOL_HALO
    tile = lambda w: pl.BlockSpec((1, tm, w), lambda b, i: (b, i, 0))
    in_specs = [
        tile(W), tile(W), tile(W),
        pl.BlockSpec((1, POOL_HALO, W), lambda b, i: (b, jnp.maximum(i * hb - 1, 0), 0)),
        pl.BlockSpec((1, POOL_HALO, W), lambda b, i: (b, jnp.minimum((i + 1) * hb, n_halo - 1), 0)),
        pl.BlockSpec((tm, W), lambda b, i: (i, 0)),
        tile(N_BRANCH * D), tile(D),
        pl.BlockSpec((1, 8, D), lambda b, i: (b, 0, 0)),
        _const_spec((8, D)),
        _const_spec(pool_w.shape), _const_spec(pool_scale.shape), _const_spec(w_branch.shape),
        _const_spec(w_out.shape), _const_spec(w_router_t.shape), _const_spec(w_router_pad.shape),
    ]
    return pl.pallas_call(
        _merge_kernel,
        grid=(B, T // tm),
        in_specs=in_specs,
        out_specs=[tile(D), tile(D // 2), tile(LANES), pl.BlockSpec((1, E, tm), lambda b, i: (b, 0, i))],
        out_shape=[jax.ShapeDtypeStruct((B, T, D), jnp.float32),
                   jax.ShapeDtypeStruct((B, T, D // 2), PACK_WORD),
                   jax.ShapeDtypeStruct((B, T, LANES), jnp.float32),
                   jax.ShapeDtypeStruct((B, E, T), jnp.float32)],
        scratch_shapes=[pltpu.VMEM((tm + 2 * POOL_HALO, W), jnp.float32),
                        pltpu.VMEM((2, min(MERGE_SUB, tm), D), jnp.float32)],
        compiler_params=pltpu.CompilerParams(
            dimension_semantics=("arbitrary", "arbitrary"), vmem_limit_bytes=VMEM_LIMIT),
        name="merge_out",
    )(m, n, pu, pu, pu, pool_inv_counts(T), bg, x, mods, gains, pool_w, pool_scale, w_branch, w_out, w_router_t,
      w_router_pad)


def _cumsum_chunks(x01):
    R, T = x01.shape
    r = lax.broadcasted_iota(jnp.int32, (LANES, LANES), 0)
    c = lax.broadcasted_iota(jnp.int32, (LANES, LANES), 1)
    tri_u = jnp.where(r <= c, 1.0, 0.0).astype(jnp.bfloat16)
    run = jnp.zeros((R, 1), jnp.float32)
    outs = []
    for t0 in range(0, T, LANES):
        cs = _dot_f32(x01[:, t0:t0 + LANES].astype(jnp.bfloat16), tri_u) + run
        run = cs[:, LANES - 1:LANES]
        outs.append(cs)
    return outs


def _route_kernel(aff_ref, idx_ref, *, cap):
    f32, bf16 = jnp.float32, jnp.bfloat16
    aff = aff_ref[0]
    E, T = aff.shape
    n_chunks = T // LANES

    def count_ge(cand):
        return jnp.sum(jnp.where(aff >= pltpu.bitcast(cand, f32), 1.0, 0.0), axis=-1, keepdims=True)

    top = jnp.full((E, 1), 1 << 30, jnp.int32)
    lo0 = jnp.where(count_ge(top) >= cap, top, jnp.zeros((E, 1), jnp.int32))

    def search(k, lo):
        shift = 28 - 2 * k
        best = lo
        for q in (1, 2, 3):
            cand = lo | (jnp.int32(q) << shift)
            best = jnp.where(count_ge(cand) >= cap, cand, best)
        return best

    thr = pltpu.bitcast(lax.fori_loop(0, 15, search, lo0), f32)
    gt = jnp.where(aff > thr, 1.0, 0.0)
    eq = jnp.where(aff == thr, 1.0, 0.0)
    room = cap - jnp.sum(gt, axis=-1, keepdims=True)
    sel = gt + eq * jnp.where(jnp.concatenate(_cumsum_chunks(eq), axis=-1) <= room, 1.0, 0.0)
    ranks = [jnp.minimum(cs, float(cap)) for cs in _cumsum_chunks(sel)]
    sel_r = lax.broadcasted_iota(jnp.int32, (T, LANES), 0)
    sel_c = lax.broadcasted_iota(jnp.int32, (T, LANES), 1)
    pick_last = jnp.where(sel_r == sel_c * LANES + (LANES - 1), 1.0, 0.0).astype(bf16)
    chunk_end = _dot_f32(jnp.concatenate(ranks, axis=-1).astype(bf16), pick_last)
    slot = lax.broadcasted_iota(jnp.int32, (cap, LANES), 0).astype(f32)
    lane = lax.broadcasted_iota(jnp.int32, (cap, LANES), 1)
    pad = jnp.zeros((LANES - n_chunks, LANES), f32)
    out = jnp.zeros((cap, LANES), f32)
    for e in range(E):
        ends = jnp.where(lane < n_chunks, jnp.broadcast_to(chunk_end[e:e + 1, :], (cap, LANES)), float(cap))
        n_full = jnp.sum(jnp.where(ends <= slot, 1.0, 0.0), axis=-1, keepdims=True)
        chunk_ranks = jnp.concatenate([r[e:e + 1, :] for r in ranks] + [pad], axis=0)
        mine = _dot_f32(jnp.where(lane.astype(f32) == n_full, 1.0, 0.0).astype(bf16), chunk_ranks.astype(bf16))
        pos = LANES * n_full + jnp.sum(jnp.where(mine <= slot, 1.0, 0.0), axis=-1, keepdims=True)
        out = jnp.where(lane == e, jnp.broadcast_to(pos, (cap, LANES)), out)
    idx_ref[0] = out.astype(jnp.int32)


def route(aff_t, cap):
    B, E, T = aff_t.shape
    assert cap <= 256 and cap % 8 == 0 and T % LANES == 0
    idx_t = pl.pallas_call(
        functools.partial(_route_kernel, cap=cap),
        grid=(B,),
        in_specs=[pl.BlockSpec((1, E, T), lambda b: (b, 0, 0))],
        out_specs=pl.BlockSpec((1, cap, LANES), lambda b: (b, 0, 0)),
        out_shape=jax.ShapeDtypeStruct((B, cap, LANES), jnp.int32),
        compiler_params=pltpu.CompilerParams(dimension_semantics=("arbitrary",), vmem_limit_bytes=VMEM_LIMIT),
        name="route",
    )(aff_t)
    return idx_t[:, :, :E].transpose(0, 2, 1)


ROW_GROUP = 8


def _moe_kernel(idx_ref, rows_ref, affr_ref, wg_ref, wu_ref, wd_ref, out_ref, xg_ref, ag_ref, ye_ref):
    f32, bf16 = jnp.float32, jnp.bfloat16
    e = pl.program_id(1)
    n_exp = pl.num_programs(1)
    n_sets, cap = xg_ref.shape[1], xg_ref.shape[2]

    def gather_row(p, s, ee, j):
        i = idx_ref[0, s, ee, j]
        xg_ref[p, s, pl.ds(j, 1), :] = rows_ref[s, pl.ds(i, 1), :]
        ag_ref[p, s, pl.ds(j, 1), :] = affr_ref[s, pl.ds(i, 1), :]

    def scatter_group(p, s, ee, j0):
        ids = [idx_ref[0, s, ee, j0 + r] for r in range(ROW_GROUP)]
        cur = [out_ref[s, pl.ds(i, 1), :] for i in ids]
        for r, i in enumerate(ids):
            out_ref[s, pl.ds(i, 1), :] = cur[r] + ye_ref[p, s, pl.ds(j0 + r, 1), :]

    @pl.when(e == 0)
    def _():
        out_ref[...] = jnp.zeros_like(out_ref)
        ye_ref[1] = jnp.zeros_like(ye_ref[1])

        def body(gi, carry):
            for s in range(n_sets):
                for r in range(ROW_GROUP):
                    gather_row(0, s, 0, gi * ROW_GROUP + r)
            return carry
        lax.fori_loop(0, cap // ROW_GROUP, body, 0)

    def step(p):
        e_prv = jnp.maximum(e - 1, 0)
        e_nxt = jnp.minimum(e + 1, n_exp - 1)
        for s in range(n_sets):
            for j0 in range(0, cap, ROW_GROUP):
                scatter_group(1 - p, s, e_prv, j0)
            for j in range(cap):
                gather_row(1 - p, s, e_nxt, j)
        packed = jnp.concatenate([xg_ref[p, s] for s in range(n_sets)], axis=0)
        xe = jnp.concatenate(
            [pltpu.unpack_elementwise(packed, index=i, packed_dtype=bf16, unpacked_dtype=f32) for i in (0, 1)],
            axis=-1).astype(bf16)
        aff = jnp.concatenate([ag_ref[p, s] for s in range(n_sets)], axis=0)
        gate = jnp.sum(jnp.where(lax.broadcasted_iota(jnp.int32, aff.shape, 1) == e, aff, 0.0),
                       axis=-1, keepdims=True)
        a = _dot_f32(xe, wg_ref[0])
        hid = (a * _sigmoid(a) * _dot_f32(xe, wu_ref[0])).astype(bf16)
        ye = _dot_f32(hid, wd_ref[0]) * gate
        for s in range(n_sets):
            ye_ref[p, s] = ye[s * cap:(s + 1) * cap]

    @pl.when(e % 2 == 0)
    def _():
        step(0)

    @pl.when(e % 2 == 1)
    def _():
        step(1)

    @pl.when(e == n_exp - 1)
    def _():
        def body(gi, carry):
            for s in range(n_sets):
                scatter_group(1, s, e, gi * ROW_GROUP)
            return carry
        lax.fori_loop(0, cap // ROW_GROUP, body, 0)


MOE_SETS = 1


def moe_experts(rows, aff_rows, idx, w_gate, w_up, w_down, layer):
    G, T, RW = rows.shape
    AW = aff_rows.shape[2]
    _, E, cap = idx.shape
    D = w_gate.shape[2]
    ns = MOE_SETS
    assert E % 2 == 0 and cap % ROW_GROUP == 0 and G % ns == 0
    wspec = pl.BlockSpec((None, 1, D, D), lambda g, e: (layer, e, 0, 0))
    return pl.pallas_call(
        _moe_kernel,
        grid=(G // ns, E),
        in_specs=[pl.BlockSpec((1, ns, E, cap), lambda g, e: (g, 0, 0, 0), memory_space=pltpu.SMEM),
                  pl.BlockSpec((ns, T, RW), lambda g, e: (g, 0, 0)),
                  pl.BlockSpec((ns, T, AW), lambda g, e: (g, 0, 0)),
                  wspec, wspec, wspec],
        out_specs=pl.BlockSpec((ns, T, D), lambda g, e: (g, 0, 0)),
        out_shape=jax.ShapeDtypeStruct((G, T, D), jnp.float32),
        scratch_shapes=[pltpu.VMEM((2, ns, cap, RW), rows.dtype), pltpu.VMEM((2, ns, cap, AW), jnp.float32),
                        pltpu.VMEM((2, ns, cap, D), jnp.float32)],
        compiler_params=pltpu.CompilerParams(
            dimension_semantics=("arbitrary", "arbitrary"), vmem_limit_bytes=VMEM_LIMIT),
        name="moe_experts",
    )(idx.reshape(G // ns, ns, E, cap), rows, aff_rows, w_gate, w_up, w_down)


def _residual_norm_kernel(x_ref, y_ref, gate_ref, gain_ref, o_ref):
    y = y_ref[0]
    yn = y * lax.rsqrt(jnp.mean(y * y, axis=-1, keepdims=True) + RMS_EPS) * gain_ref[...]
    o_ref[0] = x_ref[0] + gate_ref[0] * yn


def residual_norm(x, y, gate, gain, *, tm):
    B, T, D = x.shape
    tile = pl.BlockSpec((1, tm, D), lambda b, i: (b, i, 0))
    return pl.pallas_call(
        _residual_norm_kernel,
        grid=(B, T // tm),
        in_specs=[tile, tile, pl.BlockSpec((1, 1, D), lambda b, i: (b, 0, 0)), _const_spec((1, D))],
        out_specs=tile,
        out_shape=jax.ShapeDtypeStruct((B, T, D), jnp.float32),
        compiler_params=pltpu.CompilerParams(
            dimension_semantics=("arbitrary", "arbitrary"), vmem_limit_bytes=VMEM_LIMIT),
        name="residual_norm",
    )(x, y, gate, gain)


ADA_COLS = 1536


def _ada_kernel(c_ref, w_ref, b_ref, o_ref):
    c = c_ref[...]
    h = (c * _sigmoid(c)).astype(jnp.bfloat16)
    o_ref[...] = _dot_f32(h, w_ref[...].astype(jnp.bfloat16)) + b_ref[...]


def ada_modulation(cond, ada_w, ada_b, layer):
    R, D = cond.shape
    N = ada_w.shape[2]
    return pl.pallas_call(
        _ada_kernel,
        grid=(N // ADA_COLS,),
        in_specs=[pl.BlockSpec((R, D), lambda j: (0, 0)),
                  pl.BlockSpec((None, D, ADA_COLS), lambda j: (layer, 0, j)),
                  pl.BlockSpec((None, 1, ADA_COLS), lambda j: (layer, 0, j))],
        out_specs=pl.BlockSpec((R, ADA_COLS), lambda j: (0, j)),
        out_shape=jax.ShapeDtypeStruct((R, N), jnp.float32),
        compiler_params=pltpu.CompilerParams(dimension_semantics=("arbitrary",), vmem_limit_bytes=VMEM_LIMIT),
        name="ada_modulation",
    )(cond, ada_w, ada_b[:, None, :])


def kernel(x, c, ctx, c_ctx, norm_gain, ada_w, ada_b, w_in, mlstm_gate_bias, pool_w, pool_scale,
           na_rpb, w_branch, w_out, router_w, w_gate, w_up, w_down):
    B, S, D = x.shape
    Tc = ctx.shape[1]
    f32, bf16 = jnp.float32, jnp.bfloat16
    rope_tabs = rope_lane_tables(S)
    names = [n for n, _ in IN_GROUPS] + ["gt"]
    cond = jnp.pad(jnp.concatenate([c, c_ctx[None]], axis=0), ((0, (-(B + 1)) % 8), (0, 0)))
    moe_w = (w_gate.astype(bf16), w_up.astype(bf16), w_down.astype(bf16))
    assert B % (S // Tc) == 0
    pending = pending_c = None
    for l in range(DEPTH):
        need_ctx = l < DEPTH - 1
        g = norm_gain[l]
        mods = ada_modulation(cond, ada_w, ada_b, l)
        mod_l = jnp.split(mods[:B, None, :], N_MOD, axis=-1)
        mod_cb = [jnp.broadcast_to(m[None, None], (B, 1, D)) for m in jnp.split(mods[B], N_MOD, axis=-1)]
        w_perm = permute_w_in(w_in[l])
        gb_row = jnp.pad(mlstm_gate_bias[l].reshape(1, -1).astype(f32), ((0, 0), (0, GATE_W - 4 * M_HEADS)))
        proj = in_proj(x, g[0][None], mod_l[0], mod_l[1], w_perm, gb_row, rope_tabs, pending, tm=512)
        if pending is not None:
            x = proj[-1]
        pl_ = dict(zip(names, proj))
        keep_c = None if need_ctx else ("mqk", "mvt", "kn", "vn", "g")
        proj_c = in_proj(ctx, g[0][None], mod_cb[0], mod_cb[1], w_perm, gb_row, None, pending_c, keep_c, tm=Tc)
        if pending_c is not None:
            ctx = proj_c[-1]
        pc_ = dict(zip(names if keep_c is None else list(keep_c) + ["gt"], proj_c))

        m_l, m_c = mlstm(pc_, pl_, need_ctx)
        n_l, n_c = neighbourhood_attention(pl_["qn"], pl_["kn"], pl_["vn"], pc_["kn"], pc_["vn"],
                                           na_bias_table(na_rpb[l]), pc_["qn"] if need_ctx else None)
        gains = jnp.pad(g[1:3], ((0, 6), (0, 0)))
        merge_w = (pool_w[l].astype(bf16), pool_scale[l][None], w_branch[l].astype(bf16),
                   (0.5 * w_out[l]).astype(bf16),
                   router_w[l].T.astype(bf16),
                   jnp.pad(router_w[l], ((0, 0), (0, LANES - N_EXPERTS))).astype(bf16))

        def mods8(mods):
            return jnp.pad(jnp.concatenate([mods[2], mods[3], mods[4]], axis=1), ((0, 0), (0, 5), (0, 0)))

        x1, rows, aff_r, aff_t = merge_out(m_l, n_l, pl_["pu"], pl_["bg"], x, mods8(mod_l), gains, *merge_w, tm=1024)
        y = moe_experts(rows, aff_r, route(aff_t, EC_FACTOR * S // N_EXPERTS), *moe_w, l)
        if need_ctx:
            x, pending = x1, (y, mod_l[5], g[3][None])
        else:
            x = residual_norm(x1, y, mod_l[5], g[3][None], tm=512)
        if need_ctx:
            c1, rows, aff_r, aff_t = merge_out(m_c, n_c, pc_["pu"], pc_["bg"], ctx, mods8(mod_cb), gains, *merge_w, tm=Tc)
            per = S // Tc
            cap_c = EC_FACTOR * Tc // N_EXPERTS
            idx = route(aff_t, cap_c) + (jnp.arange(B, dtype=jnp.int32) % per * Tc)[:, None, None]
            idx = idx.reshape(B // per, per, N_EXPERTS, cap_c).transpose(0, 2, 1, 3).reshape(B // per, N_EXPERTS, per * cap_c)
            y = moe_experts(rows.reshape(B // per, S, D // 2), aff_r.reshape(B // per, S, LANES), idx,
                            *moe_w, l).reshape(B, Tc, D)
            ctx, pending_c = c1, (y, mod_cb[5], g[3][None])
    return x
```

```python
import functools
import math

import jax
import jax.numpy as jnp
import numpy as np
from jax import lax
from jax.experimental import pallas as pl
from jax.experimental.pallas import tpu as pltpu

D_MODEL = 1024
DEPTH = 2
GRID_W = 64
BRANCH_W = D_MODEL // 2
N_BRANCH = 3
M_HEADS = 4
M_DV = BRANCH_W // M_HEADS
M_DQK = M_DV // 2
M_CHUNK = 128
M_INIT = -1e30
M_ONES = 16
POOL_GROUPS = 4
POOL_GC = BRANCH_W // POOL_GROUPS
POOL_WINDOWS = (2, 4, 8, 16)
NA_HEADS = 8
NA_DH = BRANCH_W // NA_HEADS
NA_KH = 8
NA_KW = 16
N_EXPERTS = 16
EC_FACTOR = 2
ROPE_BASE = 10000.0
RMS_EPS = 1e-6
N_MOD = 6
SPLIT_SIZES = (M_HEADS * M_DQK, M_HEADS * M_DQK, M_HEADS * M_DV, M_HEADS * M_DV, 4 * M_HEADS,
               POOL_GROUPS * POOL_GC, NA_HEADS * NA_DH, NA_HEADS * NA_DH, NA_HEADS * NA_DH,
               N_BRANCH * D_MODEL)
PROJ_W = sum(SPLIT_SIZES)

LANES = 128
VMEM_LIMIT = 56 * 1024 * 1024
MASK_NEG = -1e30
GATE_W = LANES

IN_GROUPS = (("mqk", 2 * M_HEADS * M_DQK), ("mvt", BRANCH_W), ("mo", BRANCH_W), ("pu", BRANCH_W),
             ("qn", BRANCH_W), ("kn", BRANCH_W), ("vn", BRANCH_W), ("bg", N_BRANCH * D_MODEL),
             ("g", GATE_W))
IN_W = sum(w for _, w in IN_GROUPS)
MM_COLS = 512
PACK_WORD = jnp.uint32


def _sigmoid(x):
    return 0.5 * jnp.tanh(0.5 * x) + 0.5


def _const_spec(shape):
    nd = len(shape)
    return pl.BlockSpec(shape, lambda *_: (0,) * nd, pipeline_mode=pl.Buffered(1))


def _rope_tile(x, cos, sin_signed):
    half = M_DQK // 4
    lane = lax.broadcasted_iota(jnp.int32, x.shape, 1)
    partner = jnp.where((lane % (2 * half)) < half, pltpu.roll(x, LANES - half, 1), pltpu.roll(x, half, 1))
    return x * cos + partner * sin_signed


def _in_proj_kernel(x_ref, g_ref, shift_ref, scale_ref, w_ref, gb_ref, *rest, rope, pending, keep):
    if rope:
        cos_ref, sin_ref = rest[:2]
        rest = rest[2:]
    if pending:
        r_ref, rgate_ref, rgain_ref = rest[:3]
        rest, xo_ref = rest[3:-1], rest[-1]
        r = r_ref[0]
        x = x_ref[0] + rgate_ref[0] * (r * lax.rsqrt(jnp.mean(r * r, axis=-1, keepdims=True) + RMS_EPS)
                                       * rgain_ref[...])
        xo_ref[0] = x
    else:
        x = x_ref[0]
    o_refs, gt_ref = rest[:-1], rest[-1]
    y = x * lax.rsqrt(jnp.mean(x * x, axis=-1, keepdims=True) + RMS_EPS)
    h = ((y * g_ref[...]) * (1.0 + scale_ref[0]) + shift_ref[0]).astype(jnp.bfloat16)
    off = 0
    o_iter = iter(o_refs)
    for name, width in IN_GROUPS:
        if name not in keep:
            off += width
            continue
        o_ref = next(o_iter)
        for c0 in range(0, width, MM_COLS):
            cw = min(MM_COLS, width - c0)
            acc = jnp.dot(h, w_ref[:, off + c0:off + c0 + cw], preferred_element_type=jnp.float32)
            if name == "g":
                acc = acc + gb_ref[...]
                for c in range(acc.shape[0] // M_CHUNK):
                    gt_ref[0, c] = acc[c * M_CHUNK:(c + 1) * M_CHUNK, :].T[:4 * M_HEADS, :]
            if name == "mqk" and rope:
                cos, sin = cos_ref[...], sin_ref[...]
                acc = jnp.concatenate([_rope_tile(acc[:, t:t + LANES], cos, sin) for t in range(0, cw, LANES)], axis=-1)
            if name == "mvt":
                for c in range(acc.shape[0] // M_CHUNK):
                    o_ref[0, c] = acc[c * M_CHUNK:(c + 1) * M_CHUNK, :].T.astype(o_ref.dtype)
                continue
            o_ref[0, :, c0:c0 + cw] = acc.astype(o_ref.dtype)
        off += width


def in_proj(x, g, shift, scale, w_perm, gate_bias_row, rope_tabs=None, pending=None, keep=None, *, tm):
    B, T, D = x.shape
    keep = tuple(n for n, _ in IN_GROUPS) if keep is None else tuple(keep)
    assert "g" in keep
    out_shape = [jax.ShapeDtypeStruct((B, T, w), jnp.float32 if n == "g" else jnp.bfloat16) for n, w in IN_GROUPS]
    out_specs = [pl.BlockSpec((1, tm, w), lambda b, i: (b, i, 0)) for _, w in IN_GROUPS]
    k_mvt = [n for n, _ in IN_GROUPS].index("mvt")
    assert BRANCH_W == MM_COLS and tm % M_CHUNK == 0
    out_shape[k_mvt] = jax.ShapeDtypeStruct((B, T // M_CHUNK, BRANCH_W, M_CHUNK), jnp.bfloat16)
    out_specs[k_mvt] = pl.BlockSpec((1, tm // M_CHUNK, BRANCH_W, M_CHUNK), lambda b, i: (b, i, 0, 0))
    kept = [k for k, (n, _) in enumerate(IN_GROUPS) if n in keep]
    out_shape, out_specs = [out_shape[k] for k in kept], [out_specs[k] for k in kept]
    out_shape.append(jax.ShapeDtypeStruct((B, T // M_CHUNK, 4 * M_HEADS, M_CHUNK), jnp.float32))
    out_specs.append(pl.BlockSpec((1, tm // M_CHUNK, 4 * M_HEADS, M_CHUNK), lambda b, i: (b, i, 0, 0)))
    in_specs = [
        pl.BlockSpec((1, tm, D), lambda b, i: (b, i, 0)),
        _const_spec((1, D)),
        pl.BlockSpec((1, 1, D), lambda b, i: (b, 0, 0)),
        pl.BlockSpec((1, 1, D), lambda b, i: (b, 0, 0)),
        _const_spec((D, IN_W)),
        _const_spec((1, GATE_W)),
    ]
    args = [x, g, shift, scale, w_perm, gate_bias_row]
    if rope_tabs is not None:
        in_specs += [pl.BlockSpec((tm, LANES), lambda b, i: (i, 0))] * 2
        args += list(rope_tabs)
    if pending is not None:
        in_specs += [pl.BlockSpec((1, tm, D), lambda b, i: (b, i, 0)), pl.BlockSpec((1, 1, D), lambda b, i: (b, 0, 0)),
                     _const_spec((1, D))]
        args += list(pending)
        out_shape.append(jax.ShapeDtypeStruct((B, T, D), jnp.float32))
        out_specs.append(pl.BlockSpec((1, tm, D), lambda b, i: (b, i, 0)))
    return pl.pallas_call(
        functools.partial(_in_proj_kernel, rope=rope_tabs is not None, pending=pending is not None, keep=keep),
        grid=(B, T // tm),
        in_specs=in_specs,
        out_specs=out_specs,
        out_shape=out_shape,
        compiler_params=pltpu.CompilerParams(
            dimension_semantics=("arbitrary", "arbitrary"), vmem_limit_bytes=VMEM_LIMIT),
        name="in_proj",
    )(*args)


def rope_lane_tables(n_tokens):
    t = jnp.arange(n_tokens)
    row = (t // GRID_W).astype(jnp.float32)
    col = (t % GRID_W).astype(jnp.float32)
    half = M_DQK // 2
    inv = ROPE_BASE ** (-jnp.arange(0, half, 2, dtype=jnp.float32) / half)
    ar = row[:, None] * inv[None, :]
    ac = col[:, None] * inv[None, :]
    cos = jnp.concatenate([jnp.cos(ar), jnp.cos(ar), jnp.cos(ac), jnp.cos(ac)], axis=-1)
    sin = jnp.concatenate([-jnp.sin(ar), jnp.sin(ar), -jnp.sin(ac), jnp.sin(ac)], axis=-1)
    return jnp.tile(cos, (1, LANES // M_DQK)), jnp.tile(sin, (1, LANES // M_DQK))


def permute_w_in(w_in_l):
    qm, km, vm, om, gm, pu, qn, kn, vn, bg = jnp.split(w_in_l, [int(s) for s in np.cumsum(SPLIT_SIZES)[:-1]], axis=-1)
    gpad = jnp.pad(gm, ((0, 0), (0, GATE_W - gm.shape[1])))
    w = jnp.concatenate([qm * (M_DQK ** -0.5), km, vm, om, pu, qn * (NA_DH ** -0.5), kn, vn, bg * 0.5, gpad], axis=-1)
    return w.astype(jnp.bfloat16)


def _pair_scores(qp, k_parts, biases):
    n = qp.shape[0]
    lane_lo = lax.broadcasted_iota(jnp.int32, (n, LANES), 1) < NA_DH
    zero = jnp.zeros_like(qp)
    qq = jnp.concatenate([jnp.where(lane_lo, qp, zero), jnp.where(lane_lo, zero, qp)], axis=0)
    scores = []
    for kp, bias in zip(k_parts, biases):
        s = lax.dot_general(qq, kp, (((1,), (1,)), ((), ())), preferred_element_type=jnp.float32)
        scores.append(s if bias is None else s + bias)
    m = scores[0].max(axis=-1, keepdims=True)
    for s in scores[1:]:
        m = jnp.maximum(m, s.max(axis=-1, keepdims=True))
    return [(s - m).astype(jnp.bfloat16) for s in scores]


def _pair_softmax_pv(logits, v_parts):
    n = logits[0].shape[0] // 2
    lane_lo = lax.broadcasted_iota(jnp.int32, (n, LANES), 1) < NA_DH
    l = None
    o = None
    for d, vp in zip(logits, v_parts):
        p = jnp.exp(d)
        ls = p.astype(jnp.float32).sum(axis=-1, keepdims=True)
        os_ = jnp.dot(p, vp, preferred_element_type=jnp.float32)
        l = ls if l is None else l + ls
        o = os_ if o is None else o + os_
    o = o * (1.0 / l)
    return jnp.where(lane_lo, o[:n], o[n:])


def _na_kernel(q_ref, k_ref, v_ref, kc_ref, vc_ref, bias_ref, *rest, need_ctx):
    if need_ctx:
        qc_ref, o_ref, oc_ref, s_ref = rest
    else:
        o_ref, s_ref = rest
    S = q_ref.shape[1]
    Tc = kc_ref.shape[1]
    rows = S // GRID_W
    n_pairs = BRANCH_W // LANES
    n_loc = NA_KH * GRID_W

    def window(r):
        rs = jnp.clip(r - NA_KH // 2, 0, rows - NA_KH)
        return r - rs, pl.multiple_of(r * GRID_W, GRID_W), pl.multiple_of(rs * GRID_W, GRID_W)

    def scores_stage(r, slot, j):
        var, q0, k0 = window(r)
        ls = slice(j * LANES, (j + 1) * LANES)
        s_loc, s_ctx = _pair_scores(q_ref[0, pl.ds(q0, GRID_W), ls],
                                    (k_ref[0, pl.ds(k0, n_loc), ls], kc_ref[0, :, ls]),
                                    (bias_ref[var, j], None))
        s_ref[slot, j, :, :n_loc] = s_loc
        s_ref[slot, j, :, n_loc:] = s_ctx

    def output_stage(r, slot, j):
        _, q0, k0 = window(r)
        ls = slice(j * LANES, (j + 1) * LANES)
        o = _pair_softmax_pv((s_ref[slot, j, :, :n_loc], s_ref[slot, j, :, n_loc:]),
                             (v_ref[0, pl.ds(k0, n_loc), ls], vc_ref[0, :, ls]))
        o_ref[0, pl.ds(q0, GRID_W), ls] = o.astype(o_ref.dtype)

    for j in range(n_pairs):
        scores_stage(0, 0, j)

    def two_rows(i, carry):
        r0 = 2 * i
        for j in range(n_pairs):
            scores_stage(r0 + 1, 1, j)
            output_stage(r0, 0, j)
        for j in range(n_pairs):
            scores_stage(jnp.minimum(r0 + 2, rows - 1), 0, j)
            output_stage(r0 + 1, 1, j)
        return carry

    lax.fori_loop(0, rows // 2, two_rows, 0)

    if need_ctx:
        outs = []
        for j in range(n_pairs):
            ls = slice(j * LANES, (j + 1) * LANES)
            scores = _pair_scores(qc_ref[0, :, ls], (kc_ref[0, :, ls],), (None,))
            outs.append(_pair_softmax_pv(scores, (vc_ref[0, :, ls],)))
        oc_ref[0] = jnp.concatenate(outs, axis=-1).astype(oc_ref.dtype)


def na_bias_table(rpb):
    H = rpb.shape[0]
    var = jnp.arange(NA_KH)
    kr = jnp.arange(NA_KH)
    dr = kr[None, :] - var[:, None] + NA_KH - 1
    cols = jnp.arange(GRID_W)
    dc = jnp.clip(cols[None, :] - cols[:, None] + NA_KW - 1, 0, 2 * NA_KW - 2)
    cs = jnp.clip(cols - NA_KW // 2, 0, GRID_W - NA_KW)
    colmask = (cols[None, :] >= cs[:, None]) & (cols[None, :] < cs[:, None] + NA_KW)
    pick_r = jax.nn.one_hot(dr, 2 * NA_KH - 1, dtype=jnp.float32)
    pick_c = jax.nn.one_hot(dc, 2 * NA_KW - 1, dtype=jnp.float32)
    tab = jnp.einsum('vka,hab,qcb->vhqkc', pick_r, rpb.astype(jnp.float32), pick_c, precision=lax.Precision.HIGHEST)
    tab = jnp.where(colmask[None, None, :, None, :], tab, MASK_NEG)
    return tab.reshape(NA_KH, H // 2, 2 * GRID_W, NA_KH * GRID_W)


def neighbourhood_attention(qn, kn, vn, kc, vc, bias_tab, qc=None):
    B, S, W = qn.shape
    Tc = kc.shape[1]
    need_ctx = qc is not None
    lat = pl.BlockSpec((1, S, W), lambda b: (b, 0, 0))
    cx = pl.BlockSpec((1, Tc, W), lambda b: (b, 0, 0))
    in_specs = [lat, lat, lat, cx, cx, _const_spec(bias_tab.shape)]
    args = [qn, kn, vn, kc, vc, bias_tab]
    out_shape = [jax.ShapeDtypeStruct((B, S, W), jnp.bfloat16)]
    out_specs = [lat]
    if need_ctx:
        in_specs.append(cx)
        args.append(qc)
        out_shape.append(jax.ShapeDtypeStruct((B, Tc, W), jnp.bfloat16))
        out_specs.append(cx)
    res = pl.pallas_call(
        functools.partial(_na_kernel, need_ctx=need_ctx),
        grid=(B,),
        in_specs=in_specs,
        out_specs=out_specs,
        out_shape=out_shape,
        scratch_shapes=[pltpu.VMEM((2, W // LANES, 2 * GRID_W, NA_KH * GRID_W + Tc), jnp.bfloat16)],
        compiler_params=pltpu.CompilerParams(dimension_semantics=("arbitrary",), vmem_limit_bytes=VMEM_LIMIT),
        name="neighbourhood_attention",
    )(*args)
    return (res[0], res[1]) if need_ctx else (res[0], None)


def _split_bf16(x):
    hi = x.astype(jnp.bfloat16)
    r1 = x - hi.astype(jnp.float32)
    mid = r1.astype(jnp.bfloat16)
    lo = (r1 - mid.astype(jnp.float32)).astype(jnp.bfloat16)
    return hi, mid, lo


def _dot_f32(a, b):
    return jnp.dot(a, b, preferred_element_type=jnp.float32)


def _log_sigmoid(x):
    return jnp.minimum(x, 0.0) - jnp.log(1.0 + jnp.exp(-jnp.abs(x)))


def _pair_queries(qp):
    lane_lo = lax.broadcasted_iota(jnp.int32, qp.shape, 1) < M_DQK
    zq = jnp.zeros_like(qp)
    return jnp.concatenate([jnp.where(lane_lo, qp, zq), jnp.where(lane_lo, zq, qp)], axis=0)


def _mlstm_prep(qk_ref, g_ref, gt_ref, sa_ref, ra_ref, slot, ci, fwd):
    L = M_CHUNK
    f32, bf16 = jnp.float32, jnp.bfloat16
    d = 0 if fwd else 1
    rows = pl.ds(pl.multiple_of(ci * L, L), L)
    sq_r = lax.broadcasted_iota(jnp.int32, (L, L), 0)
    sq_c = lax.broadcasted_iota(jnp.int32, (L, L), 1)
    tri_l = jnp.where(sq_r >= sq_c, 1.0, 0.0).astype(bf16)
    tri_u = jnp.where(sq_r <= sq_c, 1.0, 0.0).astype(bf16)
    gates = g_ref[0, rows, :]
    gates_t = gt_ref[0, ci]
    lf, lf_t = _log_sigmoid(gates), _log_sigmoid(gates_t)
    cum = sum(_dot_f32(tri_l if fwd else tri_u, p) for p in _split_bf16(lf))
    cum_t = sum(_dot_f32(p, tri_u if fwd else tri_l) for p in _split_bf16(lf_t))
    t_i = 0 if fwd else 2
    st_s = lax.broadcasted_iota(jnp.int32, (L, 2 * L), 0)
    st_t = lax.broadcasted_iota(jnp.int32, (L, 2 * L), 1) & (L - 1)
    valid = (st_s <= st_t) if fwd else (st_s >= st_t)

    def row2(tile, c0, c1):
        return jnp.concatenate([tile[c0:c0 + 1, :], tile[c1:c1 + 1, :]], axis=-1)

    for j in range(M_HEADS // 2):
        ci0, ci1 = t_i * M_HEADS + 2 * j, t_i * M_HEADS + 2 * j + 1
        cf0, cf1 = ci0 + M_HEADS, ci1 + M_HEADS
        b_row, i_row = row2(cum_t, cf0, cf1), row2(gates_t, ci0, ci1)
        colb = jnp.concatenate([jnp.broadcast_to(gates[:, ci0:ci0 + 1] - cum[:, cf0:cf0 + 1], (L, L)),
                                jnp.broadcast_to(gates[:, ci1:ci1 + 1] - cum[:, cf1:cf1 + 1], (L, L))],
                               axis=-1)
        qp = qk_ref[0, rows, j * LANES:(j + 1) * LANES]
        kp = qk_ref[0, rows, M_HEADS * M_DQK + j * LANES:M_HEADS * M_DQK + (j + 1) * LANES]
        sa_ref[slot, d, j, 0] = jnp.where(valid, b_row + colb, MASK_NEG)
        sa_ref[slot, d, j, 1] = lax.dot_general(kp, _pair_queries(qp), (((1,), (1,)), ((), ())),
                                                preferred_element_type=f32)
        ra_ref[slot, d, j, 0:1, :] = b_row
        ra_ref[slot, d, j, 1:2, :] = i_row


def _mlstm_step(qk_ref, vt_ref, sa_ref, ra_ref, slot, h_ref, ct_ref, m_ref, ci, fwd):
    L = M_CHUNK
    f32, bf16 = jnp.float32, jnp.bfloat16
    d = 0 if fwd else 1
    rows = pl.ds(pl.multiple_of(ci * L, L), L)
    last = L - 1 if fwd else 0
    lane_lo = lax.broadcasted_iota(jnp.int32, (L, LANES), 1) < M_DQK
    lane_lo_row = lax.broadcasted_iota(jnp.int32, (1, LANES), 1) < M_DQK
    ones = jnp.ones((M_ONES, L), bf16)

    def halves(row, f):
        return jnp.concatenate([jnp.broadcast_to(f(row[:, :L]), (1, L)), jnp.broadcast_to(f(row[:, L:]), (1, L))],
                               axis=-1)

    for j in range(M_HEADS // 2):
        h0, h1 = 2 * j, 2 * j + 1
        sidx = 2 * j + d
        b_row, i_row = ra_ref[slot, d, j, 0:1, :], ra_ref[slot, d, j, 1:2, :]
        m_row = m_ref[sidx, 0:1, :]
        kp = qk_ref[0, rows, M_HEADS * M_DQK + j * LANES:M_HEADS * M_DQK + (j + 1) * LANES]
        vx0 = jnp.concatenate([vt_ref[0, ci, h0 * M_DV:(h0 + 1) * M_DV, :], ones], axis=0)
        vx1 = jnp.concatenate([vt_ref[0, ci, h1 * M_DV:(h1 + 1) * M_DV, :], ones], axis=0)
        ct = ct_ref[sidx]
        if h_ref is not None:
            dm = sa_ref[slot, d, j, 0]
            qq = _pair_queries(qk_ref[0, rows, j * LANES:(j + 1) * LANES])
            g_row = b_row + m_row
            mt = jnp.maximum(g_row, dm.max(axis=0, keepdims=True))
            sc = (sa_ref[slot, d, j, 1] * jnp.exp(dm - mt)).astype(bf16)
            inter = jnp.exp(g_row - mt)
            t1 = lax.dot_general(ct.astype(bf16), qq, (((1,), (1,)), ((), ())), preferred_element_type=f32)
            t2 = jnp.concatenate([_dot_f32(vx0, sc[:, :L]), _dot_f32(vx1, sc[:, L:])], axis=-1)
            tot = inter * t1 + t2
            h_t = tot[:M_DV] * (1.0 / jnp.maximum(jnp.abs(tot[M_DV:M_DV + 1]), jnp.exp(-mt)))
            h_ref[rows, h0 * M_DV:(h0 + 1) * M_DV] = h_t[:, :L].T
            h_ref[rows, h1 * M_DV:(h1 + 1) * M_DV] = h_t[:, L:].T
        bl = halves(b_row, lambda r: r[:, last:last + 1])
        w = bl - b_row + i_row
        m_new = jnp.maximum(bl + m_row, halves(w, lambda r: r.max(axis=-1, keepdims=True)))
        decay = jnp.exp(bl + m_row - m_new)
        ws = jnp.exp(w - m_new)
        vxs = jnp.concatenate([vx0.astype(f32) * ws[:, :L], vx1.astype(f32) * ws[:, L:]], axis=-1).astype(bf16)
        zk = jnp.zeros_like(kp)
        kk = jnp.concatenate([jnp.where(lane_lo, kp, zk), jnp.where(lane_lo, zk, kp)], axis=0)
        d_cols = jnp.where(lane_lo_row, jnp.broadcast_to(decay[:, 0:1], (1, LANES)),
                           jnp.broadcast_to(decay[:, L:L + 1], (1, LANES)))
        ct_ref[sidx] = d_cols * ct + _dot_f32(vxs, kk)
        m_ref[sidx, 0:1, :] = m_new


def _mlstm_finish(hf_ref, hb_ref, o_ref, out_ref, n_chunks):
    L = M_CHUNK

    def body(ci, carry):
        rows = pl.ds(pl.multiple_of(ci * L, L), L)
        h = hf_ref[rows, :] + hb_ref[rows, :]
        parts = []
        for hd in range(M_HEADS):
            hh = h[:, hd * M_DV:(hd + 1) * M_DV]
            parts.append(hh * lax.rsqrt(jnp.mean(hh * hh, axis=-1, keepdims=True) + RMS_EPS))
        gate = _sigmoid(o_ref[0, rows, :].astype(jnp.float32))
        out_ref[0, rows, :] = (gate * jnp.concatenate(parts, axis=-1)).astype(out_ref.dtype)
        return carry

    lax.fori_loop(0, n_chunks, body, 0)


def _mlstm_kernel(qk_c, v_c, g_c, gt_c, qk_l, v_l, g_l, gt_l, o_l, *rest, need_ctx):
    if need_ctx:
        o_c, out_l, out_c, hf_l, hb_l, hf_c, hb_c, ct_ref, m_ref, sa_ref, ra_ref = rest
    else:
        out_l, hf_l, hb_l, ct_ref, m_ref, sa_ref, ra_ref = rest
        hf_c = hb_c = None
    nc, nl = qk_c.shape[1] // M_CHUNK, qk_l.shape[1] // M_CHUNK
    ct_ref[...] = jnp.zeros_like(ct_ref)
    m_ref[...] = jnp.full_like(m_ref, M_INIT)

    def phase(qk, vt, g, gt, hf, hb, n):
        def prep(s, slot):
            _mlstm_prep(qk, g, gt, sa_ref, ra_ref, slot, s, True)
            _mlstm_prep(qk, g, gt, sa_ref, ra_ref, slot, n - 1 - s, False)

        def step(s, slot):
            _mlstm_step(qk, vt, sa_ref, ra_ref, slot, hf, ct_ref, m_ref, s, True)
            _mlstm_step(qk, vt, sa_ref, ra_ref, slot, hb, ct_ref, m_ref, n - 1 - s, False)

        prep(0, 0)

        def body(i, carry):
            s0 = 2 * i
            prep(s0 + 1, 1)
            step(s0, 0)
            prep(jnp.minimum(s0 + 2, n - 1), 0)
            step(s0 + 1, 1)
            return carry
        lax.fori_loop(0, n // 2, body, 0)

    phase(qk_c, v_c, g_c, gt_c, hf_c, hb_c, nc)
    phase(qk_l, v_l, g_l, gt_l, hf_l, hb_l, nl)
    _mlstm_finish(hf_l, hb_l, o_l, out_l, nl)
    if need_ctx:
        _mlstm_finish(hf_c, hb_c, o_c, out_c, nc)


def mlstm(pc, pl_, need_ctx):
    B, S, W = pl_["mo"].shape
    Tc = pc["mqk"].shape[1]
    f32 = jnp.float32

    def row_spec(T):
        return pl.BlockSpec((1, T, W), lambda b: (b, 0, 0))

    def specs(T):
        return [row_spec(T), pl.BlockSpec((1, T // M_CHUNK, W, M_CHUNK), lambda b: (b, 0, 0, 0)),
                pl.BlockSpec((1, T, GATE_W), lambda b: (b, 0, 0)),
                pl.BlockSpec((1, T // M_CHUNK, 4 * M_HEADS, M_CHUNK), lambda b: (b, 0, 0, 0))]

    def args(p):
        return [p["mqk"], p["mvt"], p["g"], p["gt"]]

    gate_specs = [row_spec(S)] + ([row_spec(Tc)] if need_ctx else [])
    gate_args = [pl_["mo"]] + ([pc["mo"]] if need_ctx else [])

    out_shape = [jax.ShapeDtypeStruct((B, S, W), jnp.bfloat16)]
    out_specs = [pl.BlockSpec((1, S, W), lambda b: (b, 0, 0))]
    scratch = [pltpu.VMEM((S, W), f32), pltpu.VMEM((S, W), f32)]
    if need_ctx:
        out_shape.append(jax.ShapeDtypeStruct((B, Tc, W), jnp.bfloat16))
        out_specs.append(pl.BlockSpec((1, Tc, W), lambda b: (b, 0, 0)))
        scratch += [pltpu.VMEM((Tc, W), f32), pltpu.VMEM((Tc, W), f32)]
    assert (S // M_CHUNK) % 2 == 0 and (Tc // M_CHUNK) % 2 == 0
    n_pairs = M_HEADS // 2
    scratch += [pltpu.VMEM((M_HEADS, M_DV + M_ONES, LANES), f32), pltpu.VMEM((M_HEADS, 8, 2 * M_CHUNK), f32),
                pltpu.VMEM((2, 2, n_pairs, 2, M_CHUNK, 2 * M_CHUNK), f32),
                pltpu.VMEM((2, 2, n_pairs, 8, 2 * M_CHUNK), f32)]
    res = pl.pallas_call(
        functools.partial(_mlstm_kernel, need_ctx=need_ctx),
        grid=(B,),
        in_specs=specs(Tc) + specs(S) + gate_specs,
        out_specs=out_specs,
        out_shape=out_shape,
        scratch_shapes=scratch,
        compiler_params=pltpu.CompilerParams(dimension_semantics=("arbitrary",), vmem_limit_bytes=VMEM_LIMIT),
        name="mlstm",
    )(*args(pc), *args(pl_), *gate_args)
    return (res[0], res[1]) if need_ctx else (res[0], None)


POOL_HALO = 16


def _pool_mix(ext, inv_cnt, pw_ref, ps_ref):
    n_ext = ext.shape[0]
    ts = n_ext - 2 * POOL_HALO

    def shifted(a, d):
        return pltpu.roll(a, (-d) % n_ext, 0)

    outs = []
    for gi, w in enumerate(POOL_WINDOWS):
        s = ext[:, gi * POOL_GC:(gi + 1) * POOL_GC]
        cur = s[POOL_HALO:POOL_HALO + ts]
        s = shifted(s, -1) + s
        span = 2
        while span < w:
            s = shifted(s, -(span // 2)) + shifted(s, span // 2)
            span *= 2
        mean = s[POOL_HALO:POOL_HALO + ts] * inv_cnt[:, gi * POOL_GC:(gi + 1) * POOL_GC]
        outs.append(_dot_f32((mean - cur).astype(jnp.bfloat16), pw_ref[gi]))
    return jnp.concatenate(outs, axis=-1) * ps_ref[...]


def _pack_bf16_pairs(h):
    n = h.shape[1] // 2
    return pltpu.pack_elementwise([h[:, :n], h[:, n:]], packed_dtype=jnp.bfloat16)


MERGE_SUB = 256


def _merge_kernel(m_ref, n_ref, pu_ref, prev_ref, next_ref, inv_ref, bg_ref, x_ref, mods_ref, gains_ref,
                  pw_ref, ps_ref, wb_ref, wo_ref, wrt_ref, wrp_ref, x1_ref, h2p_ref, affr_ref, aff_ref,
                  ext_ref, y_ref):
    f32, bf16 = jnp.float32, jnp.bfloat16
    i = pl.program_id(1)
    tm, D = x_ref.shape[1], x_ref.shape[2]
    E = wrt_ref.shape[0]
    ts = min(MERGE_SUB, tm)
    ext_ref[0:POOL_HALO] = jnp.where(i > 0, prev_ref[0].astype(f32), 0.0)
    ext_ref[POOL_HALO:POOL_HALO + tm] = pu_ref[0].astype(f32)
    ext_ref[POOL_HALO + tm:] = jnp.where(i < pl.num_programs(1) - 1, next_ref[0].astype(f32), 0.0)

    def mix_stage(k):
        rows = slice(k * ts, (k + 1) * ts)
        p = _pool_mix(ext_ref[k * ts:(k + 1) * ts + 2 * POOL_HALO], inv_ref[rows], pw_ref, ps_ref)
        acc = None
        for bi, br in enumerate((m_ref[0, rows], p.astype(bf16), n_ref[0, rows])):
            gate = jnp.tanh(bg_ref[0, rows, bi * D:(bi + 1) * D].astype(f32)) + 1.0
            term = gate * _dot_f32(br, wb_ref[bi])
            acc = term if acc is None else acc + term
        y_ref[k % 2] = _dot_f32(acc.astype(bf16), wo_ref[...])

    def out_stage(k):
        rows = slice(k * ts, (k + 1) * ts)
        y = y_ref[k % 2]
        yn = y * lax.rsqrt(jnp.mean(y * y, axis=-1, keepdims=True) + RMS_EPS) * gains_ref[0:1, :]
        x1 = x_ref[0, rows] + mods_ref[0, 0:1, :] * yn
        x1_ref[0, rows] = x1
        xn = x1 * lax.rsqrt(jnp.mean(x1 * x1, axis=-1, keepdims=True) + RMS_EPS) * gains_ref[1:2, :]
        h2 = xn * (1.0 + mods_ref[0, 2:3, :]) + mods_ref[0, 1:2, :]
        h2b = h2.astype(bf16)
        lg_t = lax.dot_general(wrt_ref[...], h2b, (((1,), (1,)), ((), ())), preferred_element_type=f32)
        e_t = jnp.exp(lg_t - lg_t.max(axis=0, keepdims=True))
        aff_ref[0, :, rows] = e_t / e_t.sum(axis=0, keepdims=True)
        lg = _dot_f32(h2b, wrp_ref[...])
        lg = jnp.where(lax.broadcasted_iota(jnp.int32, lg.shape, 1) < E, lg, MASK_NEG)
        e_r = jnp.exp(lg - lg.max(axis=-1, keepdims=True))
        affr_ref[0, rows] = e_r / e_r.sum(axis=-1, keepdims=True)
        h2p_ref[0, rows] = _pack_bf16_pairs(h2)

    n_sub = tm // ts
    mix_stage(0)
    for k in range(n_sub):
        if k + 1 < n_sub:
            mix_stage(k + 1)
        out_stage(k)


def pool_inv_counts(seq_len):
    t = jnp.arange(seq_len)
    cols = []
    for w in POOL_WINDOWS:
        lo = jnp.clip(t - w // 2, 0, seq_len - 1)
        hi = jnp.clip(t - w // 2 + w - 1, 0, seq_len - 1)
        cols.append(jnp.broadcast_to((1.0 / (hi - lo + 1).astype(jnp.float32))[:, None], (seq_len, POOL_GC)))
    return jnp.concatenate(cols, axis=-1)


def merge_out(m, n, pu, bg, x, mods, gains, pool_w, pool_scale, w_branch, w_out, w_router_t, w_router_pad, *, tm):
    B, T, D = x.shape
    W = m.shape[2]
    E = w_router_t.shape[0]
    hb = tm // POOL_HALO
    n_halo = T // POOL_HALO
    tile = lambda w: pl.BlockSpec((1, tm, w), lambda b, i: (b, i, 0))
    in_specs = [
        tile(W), tile(W), tile(W),
        pl.BlockSpec((1, POOL_HALO, W), lambda b, i: (b, jnp.maximum(i * hb - 1, 0), 0)),
        pl.BlockSpec((1, POOL_HALO, W), lambda b, i: (b, jnp.minimum((i + 1) * hb, n_halo - 1), 0)),
        pl.BlockSpec((tm, W), lambda b, i: (i, 0)),
        tile(N_BRANCH * D), tile(D),
        pl.BlockSpec((1, 8, D), lambda b, i: (b, 0, 0)),
        _const_spec((8, D)),
        _const_spec(pool_w.shape), _const_spec(pool_scale.shape), _const_spec(w_branch.shape),
        _const_spec(w_out.shape), _const_spec(w_router_t.shape), _const_spec(w_router_pad.shape),
    ]
    return pl.pallas_call(
        _merge_kernel,
        grid=(B, T // tm),
        in_specs=in_specs,
        out_specs=[tile(D), tile(D // 2), tile(LANES), pl.BlockSpec((1, E, tm), lambda b, i: (b, 0, i))],
        out_shape=[jax.ShapeDtypeStruct((B, T, D), jnp.float32),
                   jax.ShapeDtypeStruct((B, T, D // 2), PACK_WORD),
                   jax.ShapeDtypeStruct((B, T, LANES), jnp.float32),
                   jax.ShapeDtypeStruct((B, E, T), jnp.float32)],
        scratch_shapes=[pltpu.VMEM((tm + 2 * POOL_HALO, W), jnp.float32),
                        pltpu.VMEM((2, min(MERGE_SUB, tm), D), jnp.float32)],
        compiler_params=pltpu.CompilerParams(
            dimension_semantics=("arbitrary", "arbitrary"), vmem_limit_bytes=VMEM_LIMIT),
        name="merge_out",
    )(m, n, pu, pu, pu, pool_inv_counts(T), bg, x, mods, gains, pool_w, pool_scale, w_branch, w_out, w_router_t,
      w_router_pad)


def _cumsum_chunks(x01):
    R, T = x01.shape
    r = lax.broadcasted_iota(jnp.int32, (LANES, LANES), 0)
    c = lax.broadcasted_iota(jnp.int32, (LANES, LANES), 1)
    tri_u = jnp.where(r <= c, 1.0, 0.0).astype(jnp.bfloat16)
    run = jnp.zeros((R, 1), jnp.float32)
    outs = []
    for t0 in range(0, T, LANES):
        cs = _dot_f32(x01[:, t0:t0 + LANES].astype(jnp.bfloat16), tri_u) + run
        run = cs[:, LANES - 1:LANES]
        outs.append(cs)
    return outs


def _route_kernel(aff_ref, idx_ref, *, cap):
    f32, bf16 = jnp.float32, jnp.bfloat16
    aff = aff_ref[0]
    E, T = aff.shape
    n_chunks = T // LANES

    def count_ge(cand):
        return jnp.sum(jnp.where(aff >= pltpu.bitcast(cand, f32), 1.0, 0.0), axis=-1, keepdims=True)

    top = jnp.full((E, 1), 1 << 30, jnp.int32)
    lo0 = jnp.where(count_ge(top) >= cap, top, jnp.zeros((E, 1), jnp.int32))

    def search(k, lo):
        shift = 28 - 2 * k
        best = lo
        for q in (1, 2, 3):
            cand = lo | (jnp.int32(q) << shift)
            best = jnp.where(count_ge(cand) >= cap, cand, best)
        return best

    thr = pltpu.bitcast(lax.fori_loop(0, 15, search, lo0), f32)
    gt = jnp.where(aff > thr, 1.0, 0.0)
    eq = jnp.where(aff == thr, 1.0, 0.0)
    room = cap - jnp.sum(gt, axis=-1, keepdims=True)
    sel = gt + eq * jnp.where(jnp.concatenate(_cumsum_chunks(eq), axis=-1) <= room, 1.0, 0.0)
    ranks = [jnp.minimum(cs, float(cap)) for cs in _cumsum_chunks(sel)]
    sel_r = lax.broadcasted_iota(jnp.int32, (T, LANES), 0)
    sel_c = lax.broadcasted_iota(jnp.int32, (T, LANES), 1)
    pick_last = jnp.where(sel_r == sel_c * LANES + (LANES - 1), 1.0, 0.0).astype(bf16)
    chunk_end = _dot_f32(jnp.concatenate(ranks, axis=-1).astype(bf16), pick_last)
    slot = lax.broadcasted_iota(jnp.int32, (cap, LANES), 0).astype(f32)
    lane = lax.broadcasted_iota(jnp.int32, (cap, LANES), 1)
    pad = jnp.zeros((LANES - n_chunks, LANES), f32)
    out = jnp.zeros((cap, LANES), f32)
    for e in range(E):
        ends = jnp.where(lane < n_chunks, jnp.broadcast_to(chunk_end[e:e + 1, :], (cap, LANES)), float(cap))
        n_full = jnp.sum(jnp.where(ends <= slot, 1.0, 0.0), axis=-1, keepdims=True)
        chunk_ranks = jnp.concatenate([r[e:e + 1, :] for r in ranks] + [pad], axis=0)
        mine = _dot_f32(jnp.where(lane.astype(f32) == n_full, 1.0, 0.0).astype(bf16), chunk_ranks.astype(bf16))
        pos = LANES * n_full + jnp.sum(jnp.where(mine <= slot, 1.0, 0.0), axis=-1, keepdims=True)
        out = jnp.where(lane == e, jnp.broadcast_to(pos, (cap, LANES)), out)
    idx_ref[0] = out.astype(jnp.int32)


def route(aff_t, cap):
    B, E, T = aff_t.shape
    assert cap <= 256 and cap % 8 == 0 and T % LANES == 0
    idx_t = pl.pallas_call(
        functools.partial(_route_kernel, cap=cap),
        grid=(B,),
        in_specs=[pl.BlockSpec((1, E, T), lambda b: (b, 0, 0))],
        out_specs=pl.BlockSpec((1, cap, LANES), lambda b: (b, 0, 0)),
        out_shape=jax.ShapeDtypeStruct((B, cap, LANES), jnp.int32),
        compiler_params=pltpu.CompilerParams(dimension_semantics=("arbitrary",), vmem_limit_bytes=VMEM_LIMIT),
        name="route",
    )(aff_t)
    return idx_t[:, :, :E].transpose(0, 2, 1)


ROW_GROUP = 8


def _moe_kernel(idx_ref, rows_ref, affr_ref, wg_ref, wu_ref, wd_ref, out_ref, xg_ref, ag_ref, ye_ref):
    f32, bf16 = jnp.float32, jnp.bfloat16
    e = pl.program_id(1)
    n_exp = pl.num_programs(1)
    n_sets, cap = xg_ref.shape[1], xg_ref.shape[2]

    def gather_row(p, s, ee, j):
        i = idx_ref[0, s, ee, j]
        xg_ref[p, s, pl.ds(j, 1), :] = rows_ref[s, pl.ds(i, 1), :]
        ag_ref[p, s, pl.ds(j, 1), :] = affr_ref[s, pl.ds(i, 1), :]

    def scatter_group(p, s, ee, j0):
        ids = [idx_ref[0, s, ee, j0 + r] for r in range(ROW_GROUP)]
        cur = [out_ref[s, pl.ds(i, 1), :] for i in ids]
        for r, i in enumerate(ids):
            out_ref[s, pl.ds(i, 1), :] = cur[r] + ye_ref[p, s, pl.ds(j0 + r, 1), :]

    @pl.when(e == 0)
    def _():
        out_ref[...] = jnp.zeros_like(out_ref)
        ye_ref[1] = jnp.zeros_like(ye_ref[1])

        def body(gi, carry):
            for s in range(n_sets):
                for r in range(ROW_GROUP):
                    gather_row(0, s, 0, gi * ROW_GROUP + r)
            return carry
        lax.fori_loop(0, cap // ROW_GROUP, body, 0)

    def step(p):
        e_prv = jnp.maximum(e - 1, 0)
        e_nxt = jnp.minimum(e + 1, n_exp - 1)
        for s in range(n_sets):
            for j0 in range(0, cap, ROW_GROUP):
                scatter_group(1 - p, s, e_prv, j0)
            for j in range(cap):
                gather_row(1 - p, s, e_nxt, j)
        packed = jnp.concatenate([xg_ref[p, s] for s in range(n_sets)], axis=0)
        xe = jnp.concatenate(
            [pltpu.unpack_elementwise(packed, index=i, packed_dtype=bf16, unpacked_dtype=f32) for i in (0, 1)],
            axis=-1).astype(bf16)
        aff = jnp.concatenate([ag_ref[p, s] for s in range(n_sets)], axis=0)
        gate = jnp.sum(jnp.where(lax.broadcasted_iota(jnp.int32, aff.shape, 1) == e, aff, 0.0),
                       axis=-1, keepdims=True)
        a = _dot_f32(xe, wg_ref[0])
        hid = (a * _sigmoid(a) * _dot_f32(xe, wu_ref[0])).astype(bf16)
        ye = _dot_f32(hid, wd_ref[0]) * gate
        for s in range(n_sets):
            ye_ref[p, s] = ye[s * cap:(s + 1) * cap]

    @pl.when(e % 2 == 0)
    def _():
        step(0)

    @pl.when(e % 2 == 1)
    def _():
        step(1)

    @pl.when(e == n_exp - 1)
    def _():
        def body(gi, carry):
            for s in range(n_sets):
                scatter_group(1, s, e, gi * ROW_GROUP)
            return carry
        lax.fori_loop(0, cap // ROW_GROUP, body, 0)


MOE_SETS = 1


def moe_experts(rows, aff_rows, idx, w_gate, w_up, w_down, layer):
    G, T, RW = rows.shape
    AW = aff_rows.shape[2]
    _, E, cap = idx.shape
    D = w_gate.shape[2]
    ns = MOE_SETS
    assert E % 2 == 0 and cap % ROW_GROUP == 0 and G % ns == 0
    wspec = pl.BlockSpec((None, 1, D, D), lambda g, e: (layer, e, 0, 0))
    return pl.pallas_call(
        _moe_kernel,
        grid=(G // ns, E),
        in_specs=[pl.BlockSpec((1, ns, E, cap), lambda g, e: (g, 0, 0, 0), memory_space=pltpu.SMEM),
                  pl.BlockSpec((ns, T, RW), lambda g, e: (g, 0, 0)),
                  pl.BlockSpec((ns, T, AW), lambda g, e: (g, 0, 0)),
                  wspec, wspec, wspec],
        out_specs=pl.BlockSpec((ns, T, D), lambda g, e: (g, 0, 0)),
        out_shape=jax.ShapeDtypeStruct((G, T, D), jnp.float32),
        scratch_shapes=[pltpu.VMEM((2, ns, cap, RW), rows.dtype), pltpu.VMEM((2, ns, cap, AW), jnp.float32),
                        pltpu.VMEM((2, ns, cap, D), jnp.float32)],
        compiler_params=pltpu.CompilerParams(
            dimension_semantics=("arbitrary", "arbitrary"), vmem_limit_bytes=VMEM_LIMIT),
        name="moe_experts",
    )(idx.reshape(G // ns, ns, E, cap), rows, aff_rows, w_gate, w_up, w_down)


def _residual_norm_kernel(x_ref, y_ref, gate_ref, gain_ref, o_ref):
    y = y_ref[0]
    yn = y * lax.rsqrt(jnp.mean(y * y, axis=-1, keepdims=True) + RMS_EPS) * gain_ref[...]
    o_ref[0] = x_ref[0] + gate_ref[0] * yn


def residual_norm(x, y, gate, gain, *, tm):
    B, T, D = x.shape
    tile = pl.BlockSpec((1, tm, D), lambda b, i: (b, i, 0))
    return pl.pallas_call(
        _residual_norm_kernel,
        grid=(B, T // tm),
        in_specs=[tile, tile, pl.BlockSpec((1, 1, D), lambda b, i: (b, 0, 0)), _const_spec((1, D))],
        out_specs=tile,
        out_shape=jax.ShapeDtypeStruct((B, T, D), jnp.float32),
        compiler_params=pltpu.CompilerParams(
            dimension_semantics=("arbitrary", "arbitrary"), vmem_limit_bytes=VMEM_LIMIT),
        name="residual_norm",
    )(x, y, gate, gain)


ADA_COLS = 1536


def _ada_kernel(c_ref, w_ref, b_ref, o_ref):
    c = c_ref[...]
    h = (c * _sigmoid(c)).astype(jnp.bfloat16)
    o_ref[...] = _dot_f32(h, w_ref[...].astype(jnp.bfloat16)) + b_ref[...]


def ada_modulation(cond, ada_w, ada_b, layer):
    R, D = cond.shape
    N = ada_w.shape[2]
    return pl.pallas_call(
        _ada_kernel,
        grid=(N // ADA_COLS,),
        in_specs=[pl.BlockSpec((R, D), lambda j: (0, 0)),
                  pl.BlockSpec((None, D, ADA_COLS), lambda j: (layer, 0, j)),
                  pl.BlockSpec((None, 1, ADA_COLS), lambda j: (layer, 0, j))],
        out_specs=pl.BlockSpec((R, ADA_COLS), lambda j: (0, j)),
        out_shape=jax.ShapeDtypeStruct((R, N), jnp.float32),
        compiler_params=pltpu.CompilerParams(dimension_semantics=("arbitrary",), vmem_limit_bytes=VMEM_LIMIT),
        name="ada_modulation",
    )(cond, ada_w, ada_b[:, None, :])


def kernel(x, c, ctx, c_ctx, norm_gain, ada_w, ada_b, w_in, mlstm_gate_bias, pool_w, pool_scale,
           na_rpb, w_branch, w_out, router_w, w_gate, w_up, w_down):
    B, S, D = x.shape
    Tc = ctx.shape[1]
    f32, bf16 = jnp.float32, jnp.bfloat16
    rope_tabs = rope_lane_tables(S)
    names = [n for n, _ in IN_GROUPS] + ["gt"]
    cond = jnp.pad(jnp.concatenate([c, c_ctx[None]], axis=0), ((0, (-(B + 1)) % 8), (0, 0)))
    moe_w = (w_gate.astype(bf16), w_up.astype(bf16), w_down.astype(bf16))
    assert B % (S // Tc) == 0
    pending = pending_c = None
    for l in range(DEPTH):
        need_ctx = l < DEPTH - 1
        g = norm_gain[l]
        mods = ada_modulation(cond, ada_w, ada_b, l)
        mod_l = jnp.split(mods[:B, None, :], N_MOD, axis=-1)
        mod_cb = [jnp.broadcast_to(m[None, None], (B, 1, D)) for m in jnp.split(mods[B], N_MOD, axis=-1)]
        w_perm = permute_w_in(w_in[l])
        gb_row = jnp.pad(mlstm_gate_bias[l].reshape(1, -1).astype(f32), ((0, 0), (0, GATE_W - 4 * M_HEADS)))
        proj = in_proj(x, g[0][None], mod_l[0], mod_l[1], w_perm, gb_row, rope_tabs, pending, tm=512)
        if pending is not None:
            x = proj[-1]
        pl_ = dict(zip(names, proj))
        keep_c = None if need_ctx else ("mqk", "mvt", "kn", "vn", "g")
        proj_c = in_proj(ctx, g[0][None], mod_cb[0], mod_cb[1], w_perm, gb_row, None, pending_c, keep_c, tm=Tc)
        if pending_c is not None:
            ctx = proj_c[-1]
        pc_ = dict(zip(names if keep_c is None else list(keep_c) + ["gt"], proj_c))

        m_l, m_c = mlstm(pc_, pl_, need_ctx)
        n_l, n_c = neighbourhood_attention(pl_["qn"], pl_["kn"], pl_["vn"], pc_["kn"], pc_["vn"],
                                           na_bias_table(na_rpb[l]), pc_["qn"] if need_ctx else None)
        gains = jnp.pad(g[1:3], ((0, 6), (0, 0)))
        merge_w = (pool_w[l].astype(bf16), pool_scale[l][None], w_branch[l].astype(bf16),
                   (0.5 * w_out[l]).astype(bf16),
                   router_w[l].T.astype(bf16),
                   jnp.pad(router_w[l], ((0, 0), (0, LANES - N_EXPERTS))).astype(bf16))

        def mods8(mods):
            return jnp.pad(jnp.concatenate([mods[2], mods[3], mods[4]], axis=1), ((0, 0), (0, 5), (0, 0)))

        x1, rows, aff_r, aff_t = merge_out(m_l, n_l, pl_["pu"], pl_["bg"], x, mods8(mod_l), gains, *merge_w, tm=1024)
        y = moe_experts(rows, aff_r, route(aff_t, EC_FACTOR * S // N_EXPERTS), *moe_w, l)
        if need_ctx:
            x, pending = x1, (y, mod_l[5], g[3][None])
        else:
            x = residual_norm(x1, y, mod_l[5], g[3][None], tm=512)
        if need_ctx:
            c1, rows, aff_r, aff_t = merge_out(m_c, n_c, pc_["pu"], pc_["bg"], ctx, mods8(mod_cb), gains, *merge_w, tm=Tc)
            per = S // Tc
            cap_c = EC_FACTOR * Tc // N_EXPERTS
            idx = route(aff_t, cap_c) + (jnp.arange(B, dtype=jnp.int32) % per * Tc)[:, None, None]
            idx = idx.reshape(B // per, per, N_EXPERTS, cap_c).transpose(0, 2, 1, 3).reshape(B // per, N_EXPERTS, per * cap_c)
            y = moe_experts(rows.reshape(B // per, S, D // 2), aff_r.reshape(B // per, S, LANES), idx,
                            *moe_w, l).reshape(B, Tc, D)
            ctx, pending_c = c1, (y, mod_cb[5], g[3][None])
    return x
```

```python
import functools
import math

import jax
import jax.numpy as jnp
import numpy as np
from jax import lax
from jax.experimental import pallas as pl
from jax.experimental.pallas import tpu as pltpu

D_MODEL = 1024
DEPTH = 2
GRID_W = 64
BRANCH_W = D_MODEL // 2
N_BRANCH = 3
M_HEADS = 4
M_DV = BRANCH_W // M_HEADS
M_DQK = M_DV // 2
M_CHUNK = 128
M_INIT = -1e30
M_ONES = 16
POOL_GROUPS = 4
POOL_GC = BRANCH_W // POOL_GROUPS
POOL_WINDOWS = (2, 4, 8, 16)
NA_HEADS = 8
NA_DH = BRANCH_W // NA_HEADS
NA_KH = 8
NA_KW = 16
N_EXPERTS = 16
EC_FACTOR = 2
ROPE_BASE = 10000.0
RMS_EPS = 1e-6
N_MOD = 6
SPLIT_SIZES = (M_HEADS * M_DQK, M_HEADS * M_DQK, M_HEADS * M_DV, M_HEADS * M_DV, 4 * M_HEADS,
               POOL_GROUPS * POOL_GC, NA_HEADS * NA_DH, NA_HEADS * NA_DH, NA_HEADS * NA_DH,
               N_BRANCH * D_MODEL)
PROJ_W = sum(SPLIT_SIZES)

LANES = 128
VMEM_LIMIT = 56 * 1024 * 1024
MASK_NEG = -1e30
GATE_W = LANES

IN_GROUPS = (("mqk", 2 * M_HEADS * M_DQK), ("mvt", BRANCH_W), ("mo", BRANCH_W), ("pu", BRANCH_W),
             ("qn", BRANCH_W), ("kn", BRANCH_W), ("vn", BRANCH_W), ("bg", N_BRANCH * D_MODEL),
             ("g", GATE_W))
IN_W = sum(w for _, w in IN_GROUPS)
MM_COLS = 512
PACK_WORD = jnp.uint32


def _sigmoid(x):
    return 0.5 * jnp.tanh(0.5 * x) + 0.5


def _const_spec(shape):
    nd = len(shape)
    return pl.BlockSpec(shape, lambda *_: (0,) * nd, pipeline_mode=pl.Buffered(1))


def _rope_tile(x, cos, sin_signed):
    half = M_DQK // 4
    lane = lax.broadcasted_iota(jnp.int32, x.shape, 1)
    partner = jnp.where((lane % (2 * half)) < half, pltpu.roll(x, LANES - half, 1), pltpu.roll(x, half, 1))
    return x * cos + partner * sin_signed


def _in_proj_kernel(x_ref, g_ref, shift_ref, scale_ref, w_ref, gb_ref, *rest, rope, pending, keep, n_cast):
    if rope:
        cos_ref, sin_ref = rest[:2]
        rest = rest[2:]
    if n_cast:
        for src_ref, dst_ref in zip(rest[:n_cast], rest[-n_cast:]):
            dst_ref[...] = src_ref[...].astype(dst_ref.dtype)
        rest = rest[n_cast:-n_cast]
    if pending:
        r_ref, rgate_ref, rgain_ref = rest[:3]
        rest, xo_ref = rest[3:-1], rest[-1]
        r = r_ref[0]
        x = x_ref[0] + rgate_ref[0] * (r * lax.rsqrt(jnp.mean(r * r, axis=-1, keepdims=True) + RMS_EPS)
                                       * rgain_ref[...])
        xo_ref[0] = x
    else:
        x = x_ref[0]
    o_refs, gt_ref = rest[:-1], rest[-1]
    y = x * lax.rsqrt(jnp.mean(x * x, axis=-1, keepdims=True) + RMS_EPS)
    h = ((y * g_ref[...]) * (1.0 + scale_ref[0]) + shift_ref[0]).astype(jnp.bfloat16)
    off = 0
    o_iter = iter(o_refs)
    for name, width in IN_GROUPS:
        if name not in keep:
            off += width
            continue
        o_ref = next(o_iter)
        for c0 in range(0, width, MM_COLS):
            cw = min(MM_COLS, width - c0)
            acc = jnp.dot(h, w_ref[:, off + c0:off + c0 + cw], preferred_element_type=jnp.float32)
            if name == "g":
                acc = acc + gb_ref[...]
                for c in range(acc.shape[0] // M_CHUNK):
                    gt_ref[0, c] = acc[c * M_CHUNK:(c + 1) * M_CHUNK, :].T[:4 * M_HEADS, :]
            if name == "mqk" and rope:
                cos, sin = cos_ref[...], sin_ref[...]
                acc = jnp.concatenate([_rope_tile(acc[:, t:t + LANES], cos, sin) for t in range(0, cw, LANES)], axis=-1)
            if name == "mvt":
                for c in range(acc.shape[0] // M_CHUNK):
                    o_ref[0, c] = acc[c * M_CHUNK:(c + 1) * M_CHUNK, :].T.astype(o_ref.dtype)
                continue
            o_ref[0, :, c0:c0 + cw] = acc.astype(o_ref.dtype)
        off += width


def _slab_spec(slab, cols, per, nt):
    return pl.BlockSpec((1, slab, cols), lambda b, i: ((b * nt + i) // per, (b * nt + i) % per, 0))


def in_proj(x, g, shift, scale, w_perm, gate_bias_row, rope_tabs=None, pending=None, keep=None, cast=(), *, tm):
    B, T, D = x.shape
    keep = tuple(n for n, _ in IN_GROUPS) if keep is None else tuple(keep)
    assert "g" in keep
    out_shape = [jax.ShapeDtypeStruct((B, T, w), jnp.float32 if n == "g" else jnp.bfloat16) for n, w in IN_GROUPS]
    out_specs = [pl.BlockSpec((1, tm, w), lambda b, i: (b, i, 0)) for _, w in IN_GROUPS]
    k_mvt = [n for n, _ in IN_GROUPS].index("mvt")
    assert BRANCH_W == MM_COLS and tm % M_CHUNK == 0
    out_shape[k_mvt] = jax.ShapeDtypeStruct((B, T // M_CHUNK, BRANCH_W, M_CHUNK), jnp.bfloat16)
    out_specs[k_mvt] = pl.BlockSpec((1, tm // M_CHUNK, BRANCH_W, M_CHUNK), lambda b, i: (b, i, 0, 0))
    kept = [k for k, (n, _) in enumerate(IN_GROUPS) if n in keep]
    out_shape, out_specs = [out_shape[k] for k in kept], [out_specs[k] for k in kept]
    out_shape.append(jax.ShapeDtypeStruct((B, T // M_CHUNK, 4 * M_HEADS, M_CHUNK), jnp.float32))
    out_specs.append(pl.BlockSpec((1, tm // M_CHUNK, 4 * M_HEADS, M_CHUNK), lambda b, i: (b, i, 0, 0)))
    in_specs = [
        pl.BlockSpec((1, tm, D), lambda b, i: (b, i, 0)),
        _const_spec((1, D)),
        pl.BlockSpec((1, 1, D), lambda b, i: (b, 0, 0)),
        pl.BlockSpec((1, 1, D), lambda b, i: (b, 0, 0)),
        _const_spec((D, IN_W)),
        _const_spec((1, GATE_W)),
    ]
    args = [x, g, shift, scale, w_perm, gate_bias_row]
    if rope_tabs is not None:
        in_specs += [pl.BlockSpec((tm, LANES), lambda b, i: (i, 0))] * 2
        args += list(rope_tabs)
    if pending is not None:
        in_specs += [pl.BlockSpec((1, tm, D), lambda b, i: (b, i, 0)), pl.BlockSpec((1, 1, D), lambda b, i: (b, 0, 0)),
                     _const_spec((1, D))]
        args += list(pending)
        out_shape.append(jax.ShapeDtypeStruct((B, T, D), jnp.float32))
        out_specs.append(pl.BlockSpec((1, tm, D), lambda b, i: (b, i, 0)))
    nt = T // tm
    cast_specs = []
    for a in cast:
        assert pending is None and (a.shape[0] * a.shape[1]) % (B * nt) == 0
        slab = a.shape[0] * a.shape[1] // (B * nt)
        assert slab % 16 == 0 and a.shape[1] % slab == 0
        cast_specs.append(_slab_spec(slab, a.shape[2], a.shape[1] // slab, nt))
        out_shape.append(jax.ShapeDtypeStruct(a.shape, jnp.bfloat16))
    in_specs += cast_specs
    out_specs += cast_specs
    args += list(cast)
    return pl.pallas_call(
        functools.partial(_in_proj_kernel, rope=rope_tabs is not None, pending=pending is not None, keep=keep,
                          n_cast=len(cast)),
        grid=(B, T // tm),
        in_specs=in_specs,
        out_specs=out_specs,
        out_shape=out_shape,
        compiler_params=pltpu.CompilerParams(
            dimension_semantics=("arbitrary", "arbitrary"), vmem_limit_bytes=VMEM_LIMIT),
        name="in_proj",
    )(*args)


def rope_lane_tables(n_tokens):
    t = jnp.arange(n_tokens)
    row = (t // GRID_W).astype(jnp.float32)
    col = (t % GRID_W).astype(jnp.float32)
    half = M_DQK // 2
    inv = ROPE_BASE ** (-jnp.arange(0, half, 2, dtype=jnp.float32) / half)
    ar = row[:, None] * inv[None, :]
    ac = col[:, None] * inv[None, :]
    cos = jnp.concatenate([jnp.cos(ar), jnp.cos(ar), jnp.cos(ac), jnp.cos(ac)], axis=-1)
    sin = jnp.concatenate([-jnp.sin(ar), jnp.sin(ar), -jnp.sin(ac), jnp.sin(ac)], axis=-1)
    return jnp.tile(cos, (1, LANES // M_DQK)), jnp.tile(sin, (1, LANES // M_DQK))


def permute_w_in(w_in_l):
    qm, km, vm, om, gm, pu, qn, kn, vn, bg = jnp.split(w_in_l, [int(s) for s in np.cumsum(SPLIT_SIZES)[:-1]], axis=-1)
    gpad = jnp.pad(gm, ((0, 0), (0, GATE_W - gm.shape[1])))
    w = jnp.concatenate([qm * (M_DQK ** -0.5), km, vm, om, pu, qn * (NA_DH ** -0.5), kn, vn, bg * 0.5, gpad], axis=-1)
    return w.astype(jnp.bfloat16)


def _pair_scores(qp, k_parts, biases):
    n = qp.shape[0]
    lane_lo = lax.broadcasted_iota(jnp.int32, (n, LANES), 1) < NA_DH
    zero = jnp.zeros_like(qp)
    qq = jnp.concatenate([jnp.where(lane_lo, qp, zero), jnp.where(lane_lo, zero, qp)], axis=0)
    scores = []
    for kp, bias in zip(k_parts, biases):
        s = lax.dot_general(qq, kp, (((1,), (1,)), ((), ())), preferred_element_type=jnp.float32)
        scores.append(s if bias is None else s + bias)
    return scores


def _pair_softmax_pv(scores, v_parts):
    n = scores[0].shape[0] // 2
    lane_lo = lax.broadcasted_iota(jnp.int32, (n, LANES), 1) < NA_DH
    m = scores[0].max(axis=-1, keepdims=True)
    for s in scores[1:]:
        m = jnp.maximum(m, s.max(axis=-1, keepdims=True))
    l = None
    o = None
    for s, vp in zip(scores, v_parts):
        p = jnp.exp((s - m).astype(jnp.bfloat16))
        ls = p.astype(jnp.float32).sum(axis=-1, keepdims=True)
        os_ = jnp.dot(p, vp, preferred_element_type=jnp.float32)
        l = ls if l is None else l + ls
        o = os_ if o is None else o + os_
    o = o * (1.0 / l)
    return jnp.where(lane_lo, o[:n], o[n:])


def _na_kernel(q_ref, k_ref, v_ref, kc_ref, vc_ref, bias_ref, *rest, need_ctx):
    if need_ctx:
        qc_ref, o_ref, oc_ref, s_ref = rest
    else:
        o_ref, s_ref = rest
    S = q_ref.shape[1]
    Tc = kc_ref.shape[1]
    rows = S // GRID_W
    n_pairs = BRANCH_W // LANES
    n_loc = NA_KH * GRID_W

    def window(r):
        rs = jnp.clip(r - NA_KH // 2, 0, rows - NA_KH)
        return r - rs, pl.multiple_of(r * GRID_W, GRID_W), pl.multiple_of(rs * GRID_W, GRID_W)

    def scores_stage(r, slot, j):
        var, q0, k0 = window(r)
        ls = slice(j * LANES, (j + 1) * LANES)
        s_loc, s_ctx = _pair_scores(q_ref[0, pl.ds(q0, GRID_W), ls],
                                    (k_ref[0, pl.ds(k0, n_loc), ls], kc_ref[0, :, ls]),
                                    (bias_ref[var, j], None))
        s_ref[slot, j, :, :n_loc] = s_loc
        s_ref[slot, j, :, n_loc:] = s_ctx

    def output_stage(r, slot, j):
        _, q0, k0 = window(r)
        ls = slice(j * LANES, (j + 1) * LANES)
        o = _pair_softmax_pv((s_ref[slot, j, :, :n_loc], s_ref[slot, j, :, n_loc:]),
                             (v_ref[0, pl.ds(k0, n_loc), ls], vc_ref[0, :, ls]))
        o_ref[0, pl.ds(q0, GRID_W), ls] = o.astype(o_ref.dtype)

    for j in range(n_pairs):
        scores_stage(0, 0, j)

    def two_rows(i, carry):
        r0 = 2 * i
        for j in range(n_pairs):
            scores_stage(r0 + 1, 1, j)
            output_stage(r0, 0, j)
        for j in range(n_pairs):
            scores_stage(jnp.minimum(r0 + 2, rows - 1), 0, j)
            output_stage(r0 + 1, 1, j)
        return carry

    lax.fori_loop(0, rows // 2, two_rows, 0)

    if need_ctx:
        outs = []
        for j in range(n_pairs):
            ls = slice(j * LANES, (j + 1) * LANES)
            scores = _pair_scores(qc_ref[0, :, ls], (kc_ref[0, :, ls],), (None,))
            outs.append(_pair_softmax_pv(scores, (vc_ref[0, :, ls],)))
        oc_ref[0] = jnp.concatenate(outs, axis=-1).astype(oc_ref.dtype)


def na_bias_table(rpb):
    H = rpb.shape[0]
    var = jnp.arange(NA_KH)
    kr = jnp.arange(NA_KH)
    dr = kr[None, :] - var[:, None] + NA_KH - 1
    cols = jnp.arange(GRID_W)
    dc = jnp.clip(cols[None, :] - cols[:, None] + NA_KW - 1, 0, 2 * NA_KW - 2)
    cs = jnp.clip(cols - NA_KW // 2, 0, GRID_W - NA_KW)
    colmask = (cols[None, :] >= cs[:, None]) & (cols[None, :] < cs[:, None] + NA_KW)
    pick_r = jax.nn.one_hot(dr, 2 * NA_KH - 1, dtype=jnp.float32)
    pick_c = jax.nn.one_hot(dc, 2 * NA_KW - 1, dtype=jnp.float32)
    tab = jnp.einsum('vka,hab,qcb->vhqkc', pick_r, rpb.astype(jnp.float32), pick_c, precision=lax.Precision.HIGHEST)
    tab = jnp.where(colmask[None, None, :, None, :], tab, MASK_NEG)
    return tab.reshape(NA_KH, H // 2, 2 * GRID_W, NA_KH * GRID_W)


def neighbourhood_attention(qn, kn, vn, kc, vc, bias_tab, qc=None):
    B, S, W = qn.shape
    Tc = kc.shape[1]
    need_ctx = qc is not None
    lat = pl.BlockSpec((1, S, W), lambda b: (b, 0, 0))
    cx = pl.BlockSpec((1, Tc, W), lambda b: (b, 0, 0))
    in_specs = [lat, lat, lat, cx, cx, _const_spec(bias_tab.shape)]
    args = [qn, kn, vn, kc, vc, bias_tab]
    out_shape = [jax.ShapeDtypeStruct((B, S, W), jnp.bfloat16)]
    out_specs = [lat]
    if need_ctx:
        in_specs.append(cx)
        args.append(qc)
        out_shape.append(jax.ShapeDtypeStruct((B, Tc, W), jnp.bfloat16))
        out_specs.append(cx)
    res = pl.pallas_call(
        functools.partial(_na_kernel, need_ctx=need_ctx),
        grid=(B,),
        in_specs=in_specs,
        out_specs=out_specs,
        out_shape=out_shape,
        scratch_shapes=[pltpu.VMEM((2, W // LANES, 2 * GRID_W, NA_KH * GRID_W + Tc), jnp.float32)],
        compiler_params=pltpu.CompilerParams(dimension_semantics=("arbitrary",), vmem_limit_bytes=VMEM_LIMIT),
        name="neighbourhood_attention",
    )(*args)
    return (res[0], res[1]) if need_ctx else (res[0], None)


def _split_bf16(x):
    hi = x.astype(jnp.bfloat16)
    r1 = x - hi.astype(jnp.float32)
    mid = r1.astype(jnp.bfloat16)
    lo = (r1 - mid.astype(jnp.float32)).astype(jnp.bfloat16)
    return hi, mid, lo


def _dot_f32(a, b):
    return jnp.dot(a, b, preferred_element_type=jnp.float32)


def _log_sigmoid(x):
    return jnp.minimum(x, 0.0) - jnp.log(1.0 + jnp.exp(-jnp.abs(x)))


def _pair_queries(qp):
    lane_lo = lax.broadcasted_iota(jnp.int32, qp.shape, 1) < M_DQK
    zq = jnp.zeros_like(qp)
    return jnp.concatenate([jnp.where(lane_lo, qp, zq), jnp.where(lane_lo, zq, qp)], axis=0)


def _mlstm_prep(qk_ref, g_ref, gt_ref, sa_ref, ra_ref, slot, ci, fwd):
    L = M_CHUNK
    f32, bf16 = jnp.float32, jnp.bfloat16
    d = 0 if fwd else 1
    rows = pl.ds(pl.multiple_of(ci * L, L), L)
    sq_r = lax.broadcasted_iota(jnp.int32, (L, L), 0)
    sq_c = lax.broadcasted_iota(jnp.int32, (L, L), 1)
    tri_l = jnp.where(sq_r >= sq_c, 1.0, 0.0).astype(bf16)
    tri_u = jnp.where(sq_r <= sq_c, 1.0, 0.0).astype(bf16)
    gates = g_ref[0, rows, :]
    gates_t = gt_ref[0, ci]
    lf, lf_t = _log_sigmoid(gates), _log_sigmoid(gates_t)
    cum = sum(_dot_f32(tri_l if fwd else tri_u, p) for p in _split_bf16(lf))
    cum_t = sum(_dot_f32(p, tri_u if fwd else tri_l) for p in _split_bf16(lf_t))
    t_i = 0 if fwd else 2
    st_s = lax.broadcasted_iota(jnp.int32, (L, 2 * L), 0)
    st_t = lax.broadcasted_iota(jnp.int32, (L, 2 * L), 1) & (L - 1)
    valid = (st_s <= st_t) if fwd else (st_s >= st_t)

    def row2(tile, c0, c1):
        return jnp.concatenate([tile[c0:c0 + 1, :], tile[c1:c1 + 1, :]], axis=-1)

    for j in range(M_HEADS // 2):
        ci0, ci1 = t_i * M_HEADS + 2 * j, t_i * M_HEADS + 2 * j + 1
        cf0, cf1 = ci0 + M_HEADS, ci1 + M_HEADS
        b_row, i_row = row2(cum_t, cf0, cf1), row2(gates_t, ci0, ci1)
        colb = jnp.concatenate([jnp.broadcast_to(gates[:, ci0:ci0 + 1] - cum[:, cf0:cf0 + 1], (L, L)),
                                jnp.broadcast_to(gates[:, ci1:ci1 + 1] - cum[:, cf1:cf1 + 1], (L, L))],
                               axis=-1)
        qp = qk_ref[0, rows, j * LANES:(j + 1) * LANES]
        kp = qk_ref[0, rows, M_HEADS * M_DQK + j * LANES:M_HEADS * M_DQK + (j + 1) * LANES]
        sa_ref[slot, d, j, 0] = jnp.where(valid, b_row + colb, MASK_NEG)
        sa_ref[slot, d, j, 1] = lax.dot_general(kp, _pair_queries(qp), (((1,), (1,)), ((), ())),
                                                preferred_element_type=f32)
        ra_ref[slot, d, j, 0:1, :] = b_row
        ra_ref[slot, d, j, 1:2, :] = i_row


def _mlstm_step(qk_ref, vt_ref, sa_ref, ra_ref, slot, h_ref, ct_ref, m_ref, ci, fwd):
    L = M_CHUNK
    f32, bf16 = jnp.float32, jnp.bfloat16
    d = 0 if fwd else 1
    rows = pl.ds(pl.multiple_of(ci * L, L), L)
    last = L - 1 if fwd else 0
    lane_lo = lax.broadcasted_iota(jnp.int32, (L, LANES), 1) < M_DQK
    lane_lo_row = lax.broadcasted_iota(jnp.int32, (1, LANES), 1) < M_DQK
    ones = jnp.ones((M_ONES, L), bf16)

    def halves(row, f):
        return jnp.concatenate([jnp.broadcast_to(f(row[:, :L]), (1, L)), jnp.broadcast_to(f(row[:, L:]), (1, L))],
                               axis=-1)

    for j in range(M_HEADS // 2):
        h0, h1 = 2 * j, 2 * j + 1
        sidx = 2 * j + d
        b_row, i_row = ra_ref[slot, d, j, 0:1, :], ra_ref[slot, d, j, 1:2, :]
        m_row = m_ref[sidx, 0:1, :]
        kp = qk_ref[0, rows, M_HEADS * M_DQK + j * LANES:M_HEADS * M_DQK + (j + 1) * LANES]
        vx0 = jnp.concatenate([vt_ref[0, ci, h0 * M_DV:(h0 + 1) * M_DV, :], ones], axis=0)
        vx1 = jnp.concatenate([vt_ref[0, ci, h1 * M_DV:(h1 + 1) * M_DV, :], ones], axis=0)
        ct = ct_ref[sidx]
        if h_ref is not None:
            dm = sa_ref[slot, d, j, 0]
            qq = _pair_queries(qk_ref[0, rows, j * LANES:(j + 1) * LANES])
            g_row = b_row + m_row
            mt = jnp.maximum(g_row, dm.max(axis=0, keepdims=True))
            sc = (sa_ref[slot, d, j, 1] * jnp.exp(dm - mt)).astype(bf16)
            inter = jnp.exp(g_row - mt)
            t1 = lax.dot_general(ct.astype(bf16), qq, (((1,), (1,)), ((), ())), preferred_element_type=f32)
            t2 = jnp.concatenate([_dot_f32(vx0, sc[:, :L]), _dot_f32(vx1, sc[:, L:])], axis=-1)
            tot = inter * t1 + t2
            h_t = tot[:M_DV] * (1.0 / jnp.maximum(jnp.abs(tot[M_DV:M_DV + 1]), jnp.exp(-mt)))
            h_ref[rows, h0 * M_DV:(h0 + 1) * M_DV] = h_t[:, :L].T
            h_ref[rows, h1 * M_DV:(h1 + 1) * M_DV] = h_t[:, L:].T
        bl = halves(b_row, lambda r: r[:, last:last + 1])
        w = bl - b_row + i_row
        m_new = jnp.maximum(bl + m_row, halves(w, lambda r: r.max(axis=-1, keepdims=True)))
        decay = jnp.exp(bl + m_row - m_new)
        ws = jnp.exp(w - m_new)
        vxs = jnp.concatenate([vx0.astype(f32) * ws[:, :L], vx1.astype(f32) * ws[:, L:]], axis=-1).astype(bf16)
        zk = jnp.zeros_like(kp)
        kk = jnp.concatenate([jnp.where(lane_lo, kp, zk), jnp.where(lane_lo, zk, kp)], axis=0)
        d_cols = jnp.where(lane_lo_row, jnp.broadcast_to(decay[:, 0:1], (1, LANES)),
                           jnp.broadcast_to(decay[:, L:L + 1], (1, LANES)))
        ct_ref[sidx] = d_cols * ct + _dot_f32(vxs, kk)
        m_ref[sidx, 0:1, :] = m_new


def _mlstm_finish(hf_ref, hb_ref, o_ref, out_ref, n_chunks):
    L = M_CHUNK

    def body(ci, carry):
        rows = pl.ds(pl.multiple_of(ci * L, L), L)
        h = hf_ref[rows, :] + hb_ref[rows, :]
        parts = []
        for hd in range(M_HEADS):
            hh = h[:, hd * M_DV:(hd + 1) * M_DV]
            parts.append(hh * lax.rsqrt(jnp.mean(hh * hh, axis=-1, keepdims=True) + RMS_EPS))
        gate = _sigmoid(o_ref[0, rows, :].astype(jnp.float32))
        out_ref[0, rows, :] = (gate * jnp.concatenate(parts, axis=-1)).astype(out_ref.dtype)
        return carry

    lax.fori_loop(0, n_chunks, body, 0)


def _mlstm_kernel(qk_c, v_c, g_c, gt_c, qk_l, v_l, g_l, gt_l, o_l, *rest, need_ctx):
    if need_ctx:
        o_c, out_l, out_c, hf_l, hb_l, hf_c, hb_c, ct_ref, m_ref, sa_ref, ra_ref = rest
    else:
        out_l, hf_l, hb_l, ct_ref, m_ref, sa_ref, ra_ref = rest
        hf_c = hb_c = None
    nc, nl = qk_c.shape[1] // M_CHUNK, qk_l.shape[1] // M_CHUNK
    ct_ref[...] = jnp.zeros_like(ct_ref)
    m_ref[...] = jnp.full_like(m_ref, M_INIT)

    def phase(qk, vt, g, gt, hf, hb, n):
        def prep(s, slot):
            _mlstm_prep(qk, g, gt, sa_ref, ra_ref, slot, s, True)
            _mlstm_prep(qk, g, gt, sa_ref, ra_ref, slot, n - 1 - s, False)

        def step(s, slot):
            _mlstm_step(qk, vt, sa_ref, ra_ref, slot, hf, ct_ref, m_ref, s, True)
            _mlstm_step(qk, vt, sa_ref, ra_ref, slot, hb, ct_ref, m_ref, n - 1 - s, False)

        prep(0, 0)

        def body(i, carry):
            s0 = 2 * i
            prep(s0 + 1, 1)
            step(s0, 0)
            prep(jnp.minimum(s0 + 2, n - 1), 0)
            step(s0 + 1, 1)
            return carry
        lax.fori_loop(0, n // 2, body, 0)

    phase(qk_c, v_c, g_c, gt_c, hf_c, hb_c, nc)
    phase(qk_l, v_l, g_l, gt_l, hf_l, hb_l, nl)
    _mlstm_finish(hf_l, hb_l, o_l, out_l, nl)
    if need_ctx:
        _mlstm_finish(hf_c, hb_c, o_c, out_c, nc)


def mlstm(pc, pl_, need_ctx):
    B, S, W = pl_["mo"].shape
    Tc = pc["mqk"].shape[1]
    f32 = jnp.float32

    def row_spec(T):
        return pl.BlockSpec((1, T, W), lambda b: (b, 0, 0))

    def specs(T):
        return [row_spec(T), pl.BlockSpec((1, T // M_CHUNK, W, M_CHUNK), lambda b: (b, 0, 0, 0)),
                pl.BlockSpec((1, T, GATE_W), lambda b: (b, 0, 0)),
                pl.BlockSpec((1, T // M_CHUNK, 4 * M_HEADS, M_CHUNK), lambda b: (b, 0, 0, 0))]

    def args(p):
        return [p["mqk"], p["mvt"], p["g"], p["gt"]]

    gate_specs = [row_spec(S)] + ([row_spec(Tc)] if need_ctx else [])
    gate_args = [pl_["mo"]] + ([pc["mo"]] if need_ctx else [])

    out_shape = [jax.ShapeDtypeStruct((B, S, W), jnp.bfloat16)]
    out_specs = [pl.BlockSpec((1, S, W), lambda b: (b, 0, 0))]
    scratch = [pltpu.VMEM((S, W), f32), pltpu.VMEM((S, W), f32)]
    if need_ctx:
        out_shape.append(jax.ShapeDtypeStruct((B, Tc, W), jnp.bfloat16))
        out_specs.append(pl.BlockSpec((1, Tc, W), lambda b: (b, 0, 0)))
        scratch += [pltpu.VMEM((Tc, W), f32), pltpu.VMEM((Tc, W), f32)]
    assert (S // M_CHUNK) % 2 == 0 and (Tc // M_CHUNK) % 2 == 0
    n_pairs = M_HEADS // 2
    scratch += [pltpu.VMEM((M_HEADS, M_DV + M_ONES, LANES), f32), pltpu.VMEM((M_HEADS, 8, 2 * M_CHUNK), f32),
                pltpu.VMEM((2, 2, n_pairs, 2, M_CHUNK, 2 * M_CHUNK), f32),
                pltpu.VMEM((2, 2, n_pairs, 8, 2 * M_CHUNK), f32)]
    res = pl.pallas_call(
        functools.partial(_mlstm_kernel, need_ctx=need_ctx),
        grid=(B,),
        in_specs=specs(Tc) + specs(S) + gate_specs,
        out_specs=out_specs,
        out_shape=out_shape,
        scratch_shapes=scratch,
        compiler_params=pltpu.CompilerParams(dimension_semantics=("arbitrary",), vmem_limit_bytes=VMEM_LIMIT),
        name="mlstm",
    )(*args(pc), *args(pl_), *gate_args)
    return (res[0], res[1]) if need_ctx else (res[0], None)


POOL_HALO = 16


def _pool_mix(ext, inv_cnt, pw_ref, ps_ref):
    n_ext = ext.shape[0]
    ts = n_ext - 2 * POOL_HALO

    def shifted(a, d):
        return pltpu.roll(a, (-d) % n_ext, 0)

    outs = []
    for gi, w in enumerate(POOL_WINDOWS):
        s = ext[:, gi * POOL_GC:(gi + 1) * POOL_GC]
        cur = s[POOL_HALO:POOL_HALO + ts]
        s = shifted(s, -1) + s
        span = 2
        while span < w:
            s = shifted(s, -(span // 2)) + shifted(s, span // 2)
            span *= 2
        mean = s[POOL_HALO:POOL_HALO + ts] * inv_cnt[:, gi * POOL_GC:(gi + 1) * POOL_GC]
        outs.append(_dot_f32((mean - cur).astype(jnp.bfloat16), pw_ref[gi]))
    return jnp.concatenate(outs, axis=-1) * ps_ref[...]


def _pack_bf16_pairs(h):
    n = h.shape[1] // 2
    return pltpu.pack_elementwise([h[:, :n], h[:, n:]], packed_dtype=jnp.bfloat16)


MERGE_SUB = 256


def _merge_kernel(m_ref, n_ref, pu_ref, prev_ref, next_ref, inv_ref, bg_ref, x_ref, mods_ref, gains_ref,
                  pw_ref, ps_ref, wb_ref, wo_ref, wrt_ref, wrp_ref, x1_ref, h2p_ref, affr_ref, aff_ref,
                  ext_ref, y_ref):
    f32, bf16 = jnp.float32, jnp.bfloat16
    i = pl.program_id(1)
    tm, D = x_ref.shape[1], x_ref.shape[2]
    E = wrt_ref.shape[0]
    ts = min(MERGE_SUB, tm)
    ext_ref[0:POOL_HALO] = jnp.where(i > 0, prev_ref[0].astype(f32), 0.0)
    ext_ref[POOL_HALO:POOL_HALO + tm] = pu_ref[0].astype(f32)
    ext_ref[POOL_HALO + tm:] = jnp.where(i < pl.num_programs(1) - 1, next_ref[0].astype(f32), 0.0)

    def mix_stage(k):
        rows = slice(k * ts, (k + 1) * ts)
        p = _pool_mix(ext_ref[k * ts:(k + 1) * ts + 2 * POOL_HALO], inv_ref[rows], pw_ref, ps_ref)
        acc = None
        for bi, br in enumerate((m_ref[0, rows], p.astype(bf16), n_ref[0, rows])):
            gate = jnp.tanh(bg_ref[0, rows, bi * D:(bi + 1) * D].astype(f32)) + 1.0
            term = gate * _dot_f32(br, wb_ref[bi])
            acc = term if acc is None else acc + term
        y_ref[k % 2] = _dot_f32(acc.astype(bf16), wo_ref[...])

    def out_stage(k):
        rows = slice(k * ts, (k + 1) * ts)
        y = y_ref[k % 2]
        yn = y * lax.rsqrt(jnp.mean(y * y, axis=-1, keepdims=True) + RMS_EPS) * gains_ref[0:1, :]
        x1 = x_ref[0, rows] + mods_ref[0, 0:1, :] * yn
        x1_ref[0, rows] = x1
        xn = x1 * lax.rsqrt(jnp.mean(x1 * x1, axis=-1, keepdims=True) + RMS_EPS) * gains_ref[1:2, :]
        h2 = xn * (1.0 + mods_ref[0, 2:3, :]) + mods_ref[0, 1:2, :]
        h2b = h2.astype(bf16)
        lg_t = lax.dot_general(wrt_ref[...], h2b, (((1,), (1,)), ((), ())), preferred_element_type=f32)
        e_t = jnp.exp(lg_t - lg_t.max(axis=0, keepdims=True))
        aff_ref[0, :, rows] = e_t / e_t.sum(axis=0, keepdims=True)
        lg = _dot_f32(h2b, wrp_ref[...])
        lg = jnp.where(lax.broadcasted_iota(jnp.int32, lg.shape, 1) < E, lg, MASK_NEG)
        e_r = jnp.exp(lg - lg.max(axis=-1, keepdims=True))
        affr_ref[0, rows] = e_r / e_r.sum(axis=-1, keepdims=True)
        h2p_ref[0, rows] = _pack_bf16_pairs(h2)

    n_sub = tm // ts
    mix_stage(0)
    for k in range(n_sub):
        if k + 1 < n_sub:
            mix_stage(k + 1)
        out_stage(k)


def pool_inv_counts(seq_len):
    t = jnp.arange(seq_len)
    cols = []
    for w in POOL_WINDOWS:
        lo = jnp.clip(t - w // 2, 0, seq_len - 1)
        hi = jnp.clip(t - w // 2 + w - 1, 0, seq_len - 1)
        cols.append(jnp.broadcast_to((1.0 / (hi - lo + 1).astype(jnp.float32))[:, None], (seq_len, POOL_GC)))
    return jnp.concatenate(cols, axis=-1)


def merge_out(m, n, pu, bg, x, mods, gains, pool_w, pool_scale, w_branch, w_out, w_router_t, w_router_pad, *, tm):
    B, T, D = x.shape
    W = m.shape[2]
    E = w_router_t.shape[0]
    hb = tm // POOL_HALO
    n_halo = T // POOL_HALO
    tile = lambda w: pl.BlockSpec((1, tm, w), lambda b, i: (b, i, 0))
    in_specs = [
        tile(W), tile(W), tile(W),
        pl.BlockSpec((1, POOL_HALO, W), lambda b, i: (b, jnp.maximum(i * hb - 1, 0), 0)),
        pl.BlockSpec((1, POOL_HALO, W), lambda b, i: (b, jnp.minimum((i + 1) * hb, n_halo - 1), 0)),
        pl.BlockSpec((tm, W), lambda b, i: (i, 0)),
        tile(N_BRANCH * D), tile(D),
        pl.BlockSpec((1, 8, D), lambda b, i: (b, 0, 0)),
        _const_spec((8, D)),
        _const_spec(pool_w.shape), _const_spec(pool_scale.shape), _const_spec(w_branch.shape),
        _const_spec(w_out.shape), _const_spec(w_router_t.shape), _const_spec(w_router_pad.shape),
    ]
    return pl.pallas_call(
        _merge_kernel,
        grid=(B, T // tm),
        in_specs=in_specs,
        out_specs=[tile(D), tile(D // 2), tile(LANES), pl.BlockSpec((1, E, tm), lambda b, i: (b, 0, i))],
        out_shape=[jax.ShapeDtypeStruct((B, T, D), jnp.float32),
                   jax.ShapeDtypeStruct((B, T, D // 2), PACK_WORD),
                   jax.ShapeDtypeStruct((B, T, LANES), jnp.float32),
                   jax.ShapeDtypeStruct((B, E, T), jnp.float32)],
        scratch_shapes=[pltpu.VMEM((tm + 2 * POOL_HALO, W), jnp.float32),
                        pltpu.VMEM((2, min(MERGE_SUB, tm), D), jnp.float32)],
        compiler_params=pltpu.CompilerParams(
            dimension_semantics=("arbitrary", "arbitrary"), vmem_limit_bytes=VMEM_LIMIT),
        name="merge_out",
    )(m, n, pu, pu, pu, pool_inv_counts(T), bg, x, mods, gains, pool_w, pool_scale, w_branch, w_out, w_router_t,
      w_router_pad)


def _cumsum_chunks(x01):
    R, T = x01.shape
    r = lax.broadcasted_iota(jnp.int32, (LANES, LANES), 0)
    c = lax.broadcasted_iota(jnp.int32, (LANES, LANES), 1)
    tri_u = jnp.where(r <= c, 1.0, 0.0).astype(jnp.bfloat16)
    run = jnp.zeros((R, 1), jnp.float32)
    outs = []
    for t0 in range(0, T, LANES):
        cs = _dot_f32(x01[:, t0:t0 + LANES].astype(jnp.bfloat16), tri_u) + run
        run = cs[:, LANES - 1:LANES]
        outs.append(cs)
    return outs


def _route_kernel(aff_ref, idx_ref, *, cap):
    f32, bf16 = jnp.float32, jnp.bfloat16
    aff = aff_ref[0]
    E, T = aff.shape
    n_chunks = T // LANES

    def count_ge(cand):
        return jnp.sum(jnp.where(aff >= pltpu.bitcast(cand, f32), 1.0, 0.0), axis=-1, keepdims=True)

    top = jnp.full((E, 1), 1 << 30, jnp.int32)
    lo0 = jnp.where(count_ge(top) >= cap, top, jnp.zeros((E, 1), jnp.int32))

    def search(k, lo):
        shift = 28 - 2 * k
        best = lo
        for q in (1, 2, 3):
            cand = lo | (jnp.int32(q) << shift)
            best = jnp.where(count_ge(cand) >= cap, cand, best)
        return best

    thr = pltpu.bitcast(lax.fori_loop(0, 15, search, lo0), f32)
    gt = jnp.where(aff > thr, 1.0, 0.0)
    eq = jnp.where(aff == thr, 1.0, 0.0)
    room = cap - jnp.sum(gt, axis=-1, keepdims=True)
    sel = gt + eq * jnp.where(jnp.concatenate(_cumsum_chunks(eq), axis=-1) <= room, 1.0, 0.0)
    ranks = [jnp.minimum(cs, float(cap)) for cs in _cumsum_chunks(sel)]
    sel_r = lax.broadcasted_iota(jnp.int32, (T, LANES), 0)
    sel_c = lax.broadcasted_iota(jnp.int32, (T, LANES), 1)
    pick_last = jnp.where(sel_r == sel_c * LANES + (LANES - 1), 1.0, 0.0).astype(bf16)
    chunk_end = _dot_f32(jnp.concatenate(ranks, axis=-1).astype(bf16), pick_last)
    slot = lax.broadcasted_iota(jnp.int32, (cap, LANES), 0).astype(f32)
    lane = lax.broadcasted_iota(jnp.int32, (cap, LANES), 1)
    pad = jnp.zeros((LANES - n_chunks, LANES), f32)
    out = jnp.zeros((cap, LANES), f32)
    for e in range(E):
        ends = jnp.where(lane < n_chunks, jnp.broadcast_to(chunk_end[e:e + 1, :], (cap, LANES)), float(cap))
        n_full = jnp.sum(jnp.where(ends <= slot, 1.0, 0.0), axis=-1, keepdims=True)
        chunk_ranks = jnp.concatenate([r[e:e + 1, :] for r in ranks] + [pad], axis=0)
        mine = _dot_f32(jnp.where(lane.astype(f32) == n_full, 1.0, 0.0).astype(bf16), chunk_ranks.astype(bf16))
        pos = LANES * n_full + jnp.sum(jnp.where(mine <= slot, 1.0, 0.0), axis=-1, keepdims=True)
        out = jnp.where(lane == e, jnp.broadcast_to(pos, (cap, LANES)), out)
    idx_ref[0] = out.astype(jnp.int32)


def route(aff_t, cap):
    B, E, T = aff_t.shape
    assert cap <= 256 and cap % 8 == 0 and T % LANES == 0
    idx_t = pl.pallas_call(
        functools.partial(_route_kernel, cap=cap),
        grid=(B,),
        in_specs=[pl.BlockSpec((1, E, T), lambda b: (b, 0, 0))],
        out_specs=pl.BlockSpec((1, cap, LANES), lambda b: (b, 0, 0)),
        out_shape=jax.ShapeDtypeStruct((B, cap, LANES), jnp.int32),
        compiler_params=pltpu.CompilerParams(dimension_semantics=("arbitrary",), vmem_limit_bytes=VMEM_LIMIT),
        name="route",
    )(aff_t)
    return idx_t[:, :, :E].transpose(0, 2, 1)


ROW_GROUP = 8


def _moe_kernel(idx_ref, rows_ref, affr_ref, wg_ref, wu_ref, wd_ref, out_ref, xg_ref, ag_ref, ye_ref):
    f32, bf16 = jnp.float32, jnp.bfloat16
    e = pl.program_id(1)
    n_exp = pl.num_programs(1)
    n_sets, cap = xg_ref.shape[1], xg_ref.shape[2]

    def gather_row(p, s, ee, j):
        i = idx_ref[0, s, ee, j]
        xg_ref[p, s, pl.ds(j, 1), :] = rows_ref[s, pl.ds(i, 1), :]
        ag_ref[p, s, pl.ds(j, 1), :] = affr_ref[s, pl.ds(i, 1), :]

    def scatter_group(p, s, ee, j0):
        ids = [idx_ref[0, s, ee, j0 + r] for r in range(ROW_GROUP)]
        cur = [out_ref[s, pl.ds(i, 1), :] for i in ids]
        for r, i in enumerate(ids):
            out_ref[s, pl.ds(i, 1), :] = cur[r] + ye_ref[p, s, pl.ds(j0 + r, 1), :]

    @pl.when(e == 0)
    def _():
        out_ref[...] = jnp.zeros_like(out_ref)
        ye_ref[1] = jnp.zeros_like(ye_ref[1])

        def body(gi, carry):
            for s in range(n_sets):
                for r in range(ROW_GROUP):
                    gather_row(0, s, 0, gi * ROW_GROUP + r)
            return carry
        lax.fori_loop(0, cap // ROW_GROUP, body, 0)

    def step(p):
        e_prv = jnp.maximum(e - 1, 0)
        e_nxt = jnp.minimum(e + 1, n_exp - 1)
        for s in range(n_sets):
            for j0 in range(0, cap, ROW_GROUP):
                scatter_group(1 - p, s, e_prv, j0)
            for j in range(cap):
                gather_row(1 - p, s, e_nxt, j)
        packed = jnp.concatenate([xg_ref[p, s] for s in range(n_sets)], axis=0)
        xe = jnp.concatenate(
            [pltpu.unpack_elementwise(packed, index=i, packed_dtype=bf16, unpacked_dtype=f32) for i in (0, 1)],
            axis=-1).astype(bf16)
        aff = jnp.concatenate([ag_ref[p, s] for s in range(n_sets)], axis=0)
        gate = jnp.sum(jnp.where(lax.broadcasted_iota(jnp.int32, aff.shape, 1) == e, aff, 0.0),
                       axis=-1, keepdims=True)
        a = _dot_f32(xe, wg_ref[0])
        hid = (a * _sigmoid(a) * _dot_f32(xe, wu_ref[0])).astype(bf16)
        ye = _dot_f32(hid, wd_ref[0]) * gate
        for s in range(n_sets):
            ye_ref[p, s] = ye[s * cap:(s + 1) * cap]

    @pl.when(e % 2 == 0)
    def _():
        step(0)

    @pl.when(e % 2 == 1)
    def _():
        step(1)

    @pl.when(e == n_exp - 1)
    def _():
        def body(gi, carry):
            for s in range(n_sets):
                scatter_group(1, s, e, gi * ROW_GROUP)
            return carry
        lax.fori_loop(0, cap // ROW_GROUP, body, 0)


MOE_SETS = 1


def moe_experts(rows, aff_rows, idx, w_gate, w_up, w_down, layer):
    G, T, RW = rows.shape
    AW = aff_rows.shape[2]
    _, E, cap = idx.shape
    D = w_gate.shape[2]
    ns = MOE_SETS
    assert E % 2 == 0 and cap % ROW_GROUP == 0 and G % ns == 0
    wspec = pl.BlockSpec((None, 1, D, D), lambda g, e: (layer, e, 0, 0))
    return pl.pallas_call(
        _moe_kernel,
        grid=(G // ns, E),
        in_specs=[pl.BlockSpec((1, ns, E, cap), lambda g, e: (g, 0, 0, 0), memory_space=pltpu.SMEM),
                  pl.BlockSpec((ns, T, RW), lambda g, e: (g, 0, 0)),
                  pl.BlockSpec((ns, T, AW), lambda g, e: (g, 0, 0)),
                  wspec, wspec, wspec],
        out_specs=pl.BlockSpec((ns, T, D), lambda g, e: (g, 0, 0)),
        out_shape=jax.ShapeDtypeStruct((G, T, D), jnp.float32),
        scratch_shapes=[pltpu.VMEM((2, ns, cap, RW), rows.dtype), pltpu.VMEM((2, ns, cap, AW), jnp.float32),
                        pltpu.VMEM((2, ns, cap, D), jnp.float32)],
        compiler_params=pltpu.CompilerParams(
            dimension_semantics=("arbitrary", "arbitrary"), vmem_limit_bytes=VMEM_LIMIT),
        name="moe_experts",
    )(idx.reshape(G // ns, ns, E, cap), rows, aff_rows, w_gate, w_up, w_down)


def _residual_norm_kernel(x_ref, y_ref, gate_ref, gain_ref, o_ref):
    y = y_ref[0]
    yn = y * lax.rsqrt(jnp.mean(y * y, axis=-1, keepdims=True) + RMS_EPS) * gain_ref[...]
    o_ref[0] = x_ref[0] + gate_ref[0] * yn


def residual_norm(x, y, gate, gain, *, tm):
    B, T, D = x.shape
    tile = pl.BlockSpec((1, tm, D), lambda b, i: (b, i, 0))
    return pl.pallas_call(
        _residual_norm_kernel,
        grid=(B, T // tm),
        in_specs=[tile, tile, pl.BlockSpec((1, 1, D), lambda b, i: (b, 0, 0)), _const_spec((1, D))],
        out_specs=tile,
        out_shape=jax.ShapeDtypeStruct((B, T, D), jnp.float32),
        compiler_params=pltpu.CompilerParams(
            dimension_semantics=("arbitrary", "arbitrary"), vmem_limit_bytes=VMEM_LIMIT),
        name="residual_norm",
    )(x, y, gate, gain)


ADA_COLS = 1536


def _ada_kernel(c_ref, w_ref, b_ref, o_ref):
    c = c_ref[...]
    h = (c * _sigmoid(c)).astype(jnp.bfloat16)
    o_ref[...] = _dot_f32(h, w_ref[...].astype(jnp.bfloat16)) + b_ref[...]


def ada_modulation(cond, ada_w, ada_b, layer):
    R, D = cond.shape
    N = ada_w.shape[2]
    return pl.pallas_call(
        _ada_kernel,
        grid=(N // ADA_COLS,),
        in_specs=[pl.BlockSpec((R, D), lambda j: (0, 0)),
                  pl.BlockSpec((None, D, ADA_COLS), lambda j: (layer, 0, j)),
                  pl.BlockSpec((None, 1, ADA_COLS), lambda j: (layer, 0, j))],
        out_specs=pl.BlockSpec((R, ADA_COLS), lambda j: (0, j)),
        out_shape=jax.ShapeDtypeStruct((R, N), jnp.float32),
        compiler_params=pltpu.CompilerParams(dimension_semantics=("arbitrary",), vmem_limit_bytes=VMEM_LIMIT),
        name="ada_modulation",
    )(cond, ada_w, ada_b[:, None, :])


def kernel(x, c, ctx, c_ctx, norm_gain, ada_w, ada_b, w_in, mlstm_gate_bias, pool_w, pool_scale,
           na_rpb, w_branch, w_out, router_w, w_gate, w_up, w_down):
    B, S, D = x.shape
    Tc = ctx.shape[1]
    f32, bf16 = jnp.float32, jnp.bfloat16
    rope_tabs = rope_lane_tables(S)
    names = [n for n, _ in IN_GROUPS] + ["gt"]
    cond = jnp.pad(jnp.concatenate([c, c_ctx[None]], axis=0), ((0, (-(B + 1)) % 8), (0, 0)))
    assert B % (S // Tc) == 0
    pending = pending_c = None
    for l in range(DEPTH):
        need_ctx = l < DEPTH - 1
        g = norm_gain[l]
        mods = ada_modulation(cond, ada_w, ada_b, l)
        mod_l = jnp.split(mods[:B, None, :], N_MOD, axis=-1)
        mod_cb = [jnp.broadcast_to(m[None, None], (B, 1, D)) for m in jnp.split(mods[B], N_MOD, axis=-1)]
        w_perm = permute_w_in(w_in[l])
        gb_row = jnp.pad(mlstm_gate_bias[l].reshape(1, -1).astype(f32), ((0, 0), (0, GATE_W - 4 * M_HEADS)))
        cast = tuple(w.reshape((-1,) + w.shape[2:]) for w in (w_gate, w_up, w_down)) if l == 0 else ()
        proj = in_proj(x, g[0][None], mod_l[0], mod_l[1], w_perm, gb_row, rope_tabs, pending, None, cast, tm=512)
        if pending is not None:
            x = proj[-1]
        if l == 0:
            moe_w = tuple(w.reshape(w_gate.shape) for w in proj[-3:])
        pl_ = dict(zip(names, proj))
        keep_c = None if need_ctx else ("mqk", "mvt", "kn", "vn", "g")
        proj_c = in_proj(ctx, g[0][None], mod_cb[0], mod_cb[1], w_perm, gb_row, None, pending_c, keep_c, tm=Tc)
        if pending_c is not None:
            ctx = proj_c[-1]
        pc_ = dict(zip(names if keep_c is None else list(keep_c) + ["gt"], proj_c))

        m_l, m_c = mlstm(pc_, pl_, need_ctx)
        n_l, n_c = neighbourhood_attention(pl_["qn"], pl_["kn"], pl_["vn"], pc_["kn"], pc_["vn"],
                                           na_bias_table(na_rpb[l]), pc_["qn"] if need_ctx else None)
        gains = jnp.pad(g[1:3], ((0, 6), (0, 0)))
        merge_w = (pool_w[l].astype(bf16), pool_scale[l][None], w_branch[l].astype(bf16),
                   (0.5 * w_out[l]).astype(bf16),
                   router_w[l].T.astype(bf16),
                   jnp.pad(router_w[l], ((0, 0), (0, LANES - N_EXPERTS))).astype(bf16))

        def mods8(mods):
            return jnp.pad(jnp.concatenate([mods[2], mods[3], mods[4]], axis=1), ((0, 0), (0, 5), (0, 0)))

        x1, rows, aff_r, aff_t = merge_out(m_l, n_l, pl_["pu"], pl_["bg"], x, mods8(mod_l), gains, *merge_w, tm=1024)
        y = moe_experts(rows, aff_r, route(aff_t, EC_FACTOR * S // N_EXPERTS), *moe_w, l)
        if need_ctx:
            x, pending = x1, (y, mod_l[5], g[3][None])
        else:
            x = residual_norm(x1, y, mod_l[5], g[3][None], tm=512)
        if need_ctx:
            c1, rows, aff_r, aff_t = merge_out(m_c, n_c, pc_["pu"], pc_["bg"], ctx, mods8(mod_cb), gains, *merge_w, tm=Tc)
            per = S // Tc
            cap_c = EC_FACTOR * Tc // N_EXPERTS
            idx = route(aff_t, cap_c) + (jnp.arange(B, dtype=jnp.int32) % per * Tc)[:, None, None]
            idx = idx.reshape(B // per, per, N_EXPERTS, cap_c).transpose(0, 2, 1, 3).reshape(B // per, N_EXPERTS, per * cap_c)
            y = moe_experts(rows.reshape(B // per, S, D // 2), aff_r.reshape(B // per, S, LANES), idx,
                            *moe_w, l).reshape(B, Tc, D)
            ctx, pending_c = c1, (y, mod_cb[5], g[3][None])
    return x
```
